```python
import math
import jax
import jax.numpy as jnp
from jax import lax
import numpy as np

D_MODEL = 1024
BATCH = 4
SEQ = 4096
DEPTH = 2

CTX_LEN = 256
GRID_W = 64
HEAD_DIM = 64
N_GROUPS = 4
GROUP_HEADS = D_MODEL // (N_GROUPS * HEAD_DIM)
GW = GROUP_HEADS * HEAD_DIM
D_MIX = N_GROUPS * GW
DIFF_HALF = HEAD_DIM // 2
Q_BLOCK = 128
ROPE_BASE = 10000.0
DECAY_LORA = 64
ICLR_LORA = 64
GATE_LORA = 128
RWKV_GN_EPS = 64e-5
RWKV_COLS = 3 * GW + 2 * DECAY_LORA + 2 * ICLR_LORA + GATE_LORA
GDN_CONV = 3
GDN_CHUNK = 64
GDN_COLS = 4 * GW + 4 * GROUP_HEADS
WIN_H = 8
WIN_W = 16
ATTN_COLS = 3 * GW
IN_COLS = ATTN_COLS + RWKV_COLS + GDN_COLS + ATTN_COLS
PEER_HEADS = 8
PEER_KEYS = 128
PEER_EXPERTS = PEER_KEYS * PEER_KEYS
PEER_QDIM = 256
PEER_HALF = PEER_QDIM // 2
PEER_TOPK = 16
PEER_BLOCK = 128
DN_ALPHA = (2 * DEPTH) ** 0.25
DN_BETA = (8 * DEPTH) ** -0.25
LN_EPS = 1e-5

kernel_name = 'hybrid_diffusion_parallel_groups_peer'

F32 = jnp.float32


def _split(x, sizes):
    offs = np.cumsum(sizes)[:-1].tolist()
    return jnp.split(x, offs, axis=-1)


def _heads(x):
    return x.reshape(x.shape[0], x.shape[1], -1, HEAD_DIM)


def _layer_norm(x):
    xf = x.astype(F32)
    mu = xf.mean(-1, keepdims=True)
    var = jnp.square(xf - mu).mean(-1, keepdims=True)
    return (xf - mu) * lax.rsqrt(var + LN_EPS)


def _rms_norm(x, g):
    xf = x.astype(F32)
    return xf * lax.rsqrt(jnp.mean(xf * xf, -1, keepdims=True) + 1e-6) * g.astype(F32)


def _l2norm(x):
    xf = x.astype(F32)
    return xf * lax.rsqrt(jnp.sum(xf * xf, -1, keepdims=True) + 1e-6)


def _modulate(h, shift, scale, dtype):
    return (_layer_norm(h) * (1.0 + scale) + shift).astype(dtype)


def _post_norm(h, gate, y, g, b, dtype):
    z = DN_ALPHA * h.astype(F32) + gate.astype(F32) * y.astype(F32)
    return (_layer_norm(z) * g + b).astype(dtype)


def _dwconv_centred(x, w):
    k = w.shape[0]
    pad = k // 2
    t = x.shape[1]
    xp = jnp.pad(x, ((0, 0), (pad, pad), (0, 0)))
    out = xp[:, 0:t] * w[0]
    for i in range(1, k):
        out = out + xp[:, i:i + t] * w[i]
    return out


def _grid_pos(n_tok):
    t = jnp.arange(n_tok, dtype=jnp.int32)
    return t // GRID_W, t % GRID_W


def _rope_1d(x, pos):
    half = x.shape[-1] // 2
    inv = ROPE_BASE ** (-jnp.arange(half, dtype=F32) / half)
    ang = pos.astype(F32)[:, None] * inv
    cos = jnp.cos(ang)[:, None, :]
    sin = jnp.sin(ang)[:, None, :]
    x1 = x[..., :half].astype(F32)
    x2 = x[..., half:].astype(F32)
    return jnp.concatenate([x1 * cos - x2 * sin, x1 * sin + x2 * cos], -1)


def _rope_2d(x, row, col):
    n = x.shape[-1] // 2
    return jnp.concatenate([_rope_1d(x[..., :n], row), _rope_1d(x[..., n:], col)], -1)


def _diff_core(q, k, v, lam):
    scale = DIFF_HALF ** -0.5
    q = q.astype(F32)
    k = k.astype(F32)
    s1 = jnp.einsum('bqhd,bkhd->bhqk', q[..., :DIFF_HALF], k[..., :DIFF_HALF]) * scale
    s2 = jnp.einsum('bqhd,bkhd->bhqk', q[..., DIFF_HALF:], k[..., DIFF_HALF:]) * scale
    p = jax.nn.softmax(s1, -1) - lam * jax.nn.softmax(s2, -1)
    return jnp.einsum('bhqk,bkhd->bqhd', p, v.astype(F32))


def _diff_attention(q, k, v, qc, kc, vc, lam_vecs, norm_g, lam_init, ctx_out):
    bsz, lat_len = q.shape[:2]
    row, col = _grid_pos(lat_len)
    rot = lambda z: jnp.concatenate([_rope_2d(z[..., :DIFF_HALF], row, col),
                                     _rope_2d(z[..., DIFF_HALF:], row, col)], -1)
    q = rot(q)
    k = rot(k)
    lv = lam_vecs.astype(F32)
    lam = jnp.exp(jnp.sum(lv[0] * lv[1])) - jnp.exp(jnp.sum(lv[2] * lv[3])) + lam_init
    k_all = jnp.concatenate([k, kc.astype(F32)], 1)
    v_all = jnp.concatenate([v.astype(F32), vc.astype(F32)], 1)
    nb = lat_len // Q_BLOCK
    qb = jnp.moveaxis(q.reshape(bsz, nb, Q_BLOCK, GROUP_HEADS, HEAD_DIM), 1, 0)
    ob = lax.map(lambda qq: _diff_core(qq, k_all, v_all, lam), qb)
    o = jnp.moveaxis(ob, 0, 1).reshape(bsz, lat_len, GROUP_HEADS, HEAD_DIM)
    post = lambda z: (_rms_norm(z, norm_g) * (1.0 - lam_init)).reshape(z.shape[0], z.shape[1], GW)
    out_ctx = post(_diff_core(qc, kc, vc, lam)) if ctx_out else None
    return post(o), out_ctx


def _wkv7_scan(r, w, k, v, kk, a, s0, reverse):
    def step(s, inp):
        r_t, w_t, k_t, v_t, kk_t, a_t = inp
        sa = jnp.einsum('bhvk,bhk->bhv', s, kk_t)
        s = (s * w_t[:, :, None, :] - sa[..., None] * (kk_t * a_t)[:, :, None, :]
             + v_t[..., None] * k_t[:, :, None, :])
        return s, jnp.einsum('bhvk,bhk->bhv', s, r_t)
    xs = tuple(jnp.moveaxis(z, 1, 0) for z in (r, w, k, v, kk, a))
    s, y = lax.scan(step, s0, xs, reverse=reverse)
    return jnp.moveaxis(y, 0, 1), s


def _rwkv7(f_lat, f_ctx, mu, w0, w2, a0, a2, g2, k_k, k_a, r_k, ln_g, ln_b, ctx_out):
    shift_w = jnp.stack([mu[0], 1.0 - mu[0] - mu[1], mu[1]])

    def prep(f):
        f = _dwconv_centred(f, shift_w).astype(F32)
        r, k, v, wd, ad, gd = _split(f, [GW, GW, GW, 2 * DECAY_LORA, 2 * ICLR_LORA, GATE_LORA])
        bsz, t = f.shape[:2]
        wd = jnp.tanh(wd.reshape(bsz, t, 2, DECAY_LORA))
        ad = ad.reshape(bsz, t, 2, ICLR_LORA)
        w_raw = w0 + jnp.einsum('btdr,drc->btdc', wd, w2)
        decay = jnp.exp(-jnp.exp(-jax.nn.softplus(-w_raw) - 0.5))
        a = jax.nn.sigmoid(a0 + jnp.einsum('btdr,drc->btdc', ad, a2))
        g = jax.nn.sigmoid(gd) @ g2
        kk = _l2norm(_heads(k * k_k))
        kd = k[:, :, None] * (1.0 + (a - 1.0) * k_a)
        return r, v, g, kk, decay, a, kd

    def run(p, d, s0, rev):
        r, v, g, kk, decay, a, kd = p
        return _wkv7_scan(_heads(r), _heads(decay[:, :, d]), _heads(kd[:, :, d]), _heads(v),
                          kk, _heads(a[:, :, d]), s0, rev)

    def post(p, ys):
        r, v, g, kk, decay, a, kd = p
        bsz, t = r.shape[:2]
        y = ys[0] + ys[1]
        m = y.mean(-1, keepdims=True)
        var = jnp.square(y - m).mean(-1, keepdims=True)
        yn = ((y - m) * lax.rsqrt(var + RWKV_GN_EPS)).reshape(bsz, t, GW) * ln_g + ln_b
        rh, vh = _heads(r), _heads(v)
        bonus = ((rh * _heads(kd[:, :, 0]) * r_k).sum(-1, keepdims=True) * vh
                 + (rh * _heads(kd[:, :, 1]) * r_k).sum(-1, keepdims=True) * vh)
        return (yn + bonus.reshape(bsz, t, GW)) * g

    pl, pc = prep(f_lat), prep(f_ctx)
    s0 = jnp.zeros((f_lat.shape[0], GROUP_HEADS, HEAD_DIM, HEAD_DIM), F32)
    y_lat, y_ctx = [], []
    for d, rev in ((0, False), (1, True)):
        yc, sc = run(pc, d, s0, rev)
        yl, _ = run(pl, d, sc, rev)
        y_lat.append(yl)
        y_ctx.append(yc)
    out_ctx = post(pc, y_ctx) if ctx_out else None
    return post(pl, y_lat), out_ctx


def _gdn_chunked(q, k, v, beta, g, s0):
    bsz, t, h, dk = q.shape
    dv = v.shape[-1]
    n = t // GDN_CHUNK
    ch = lambda z: jnp.moveaxis(z.reshape(bsz, n, GDN_CHUNK, h, *z.shape[3:]), 3, 2)
    q, k, v, beta, g = ch(q), ch(k), ch(v), ch(beta), ch(g)
    gc = jnp.cumsum(g, axis=-1)
    i = jnp.arange(GDN_CHUNK)
    incl = i[:, None] >= i[None, :]
    strict = i[:, None] > i[None, :]
    decay = jnp.exp(jnp.where(incl, gc[..., :, None] - gc[..., None, :], -jnp.inf))
    kb = k * beta[..., None]
    a_low = jnp.where(strict, jnp.einsum('bnhid,bnhjd->bnhij', kb, k) * decay, 0.0)
    tmat = a_low + jnp.eye(GDN_CHUNK, dtype=F32)
    rhs = jnp.concatenate([v * beta[..., None], kb * jnp.exp(gc)[..., None]], -1)
    sol = lax.linalg.triangular_solve(tmat, rhs, left_side=True, lower=True, unit_diagonal=True)
    u, w = sol[..., :dv], sol[..., dv:]
    qk = jnp.where(incl, jnp.einsum('bnhid,bnhjd->bnhij', q, k) * decay, 0.0)
    qg = q * jnp.exp(gc)[..., None]
    kg = k * jnp.exp(gc[..., -1:] - gc)[..., None]
    glast = jnp.exp(gc[..., -1])

    def step(s, xs):
        qg_i, kg_i, u_i, w_i, qk_i, gl_i = xs
        v_new = u_i - jnp.einsum('bhcd,bhdv->bhcv', w_i, s)
        o = jnp.einsum('bhcd,bhdv->bhcv', qg_i, s) + jnp.einsum('bhij,bhjv->bhiv', qk_i, v_new)
        s = s * gl_i[..., None, None] + jnp.einsum('bhcd,bhcv->bhdv', kg_i, v_new)
        return s, o

    xs = tuple(jnp.moveaxis(z, 1, 0) for z in (qg, kg, u, w, qk, glast))
    s, o = lax.scan(step, s0, xs)
    o = jnp.moveaxis(jnp.moveaxis(o, 0, 1), 2, 3).reshape(bsz, t, h, dv)
    return o, s


def _gated_deltanet(f_lat, f_ctx, conv_w, a_log, dt_bias, norm_g, ctx_out):
    def prep(f):
        qkv, gate, ab = _split(f, [3 * GW, GW, 4 * GROUP_HEADS])
        qkv = jax.nn.silu(_dwconv_centred(qkv, conv_w).astype(F32))
        q, k, v = [_heads(z) for z in _split(qkv, [GW, GW, GW])]
        q = _l2norm(q) * HEAD_DIM ** -0.5
        k = _l2norm(k)
        ab = ab.astype(F32).reshape(f.shape[0], f.shape[1], 2, 2, GROUP_HEADS)
        log_alpha = -jnp.exp(a_log) * jax.nn.softplus(ab[:, :, 0] + dt_bias)
        beta = jax.nn.sigmoid(ab[:, :, 1])
        return q, k, v, gate, log_alpha, beta

    def run(p, d, s0):
        tr = (lambda z: jnp.flip(z, 1)) if d == 1 else (lambda z: z)
        q, k, v, gate, log_alpha, beta = p
        o, s = _gdn_chunked(tr(q), tr(k), tr(v), tr(beta[:, :, d]), tr(log_alpha[:, :, d]), s0)
        return tr(o), s

    def post(p, os_):
        gate = p[3]
        o = os_[0] + os_[1]
        y = _rms_norm(o, norm_g) * jax.nn.silu(_heads(gate.astype(F32)))
        return y.reshape(o.shape[0], o.shape[1], GW)

    pl, pc = prep(f_lat), prep(f_ctx)
    s0 = jnp.zeros((f_lat.shape[0], GROUP_HEADS, HEAD_DIM, HEAD_DIM), F32)
    o_lat, o_ctx = [], []
    for d in range(2):
        oc, sc = run(pc, d, s0)
        ol, _ = run(pl, d, sc)
        o_lat.append(ol)
        o_ctx.append(oc)
    out_ctx = post(pc, o_ctx) if ctx_out else None
    return post(pl, o_lat), out_ctx


def _softmax_attn(q, k, v):
    s = jnp.einsum('bqhd,bkhd->bhqk', q.astype(F32), k.astype(F32)) * HEAD_DIM ** -0.5
    return jnp.einsum('bhqk,bkhd->bqhd', jax.nn.softmax(s, -1), v.astype(F32))


def _neighbourhood_attention(q, k, v, qc, kc, vc, rpb, ctx_out):
    bsz, lat_len = q.shape[:2]
    rows = lat_len // GRID_W
    kh = min(WIN_H, rows)
    scale = HEAD_DIM ** -0.5
    grid = lambda z: z.astype(F32).reshape(bsz, rows, GRID_W, GROUP_HEADS, HEAD_DIM)
    qg, kg, vg = grid(q), grid(k), grid(v)
    kc32, vc32 = kc.astype(F32), vc.astype(F32)
    rpb = rpb.astype(F32)
    cq = jnp.arange(GRID_W)
    c_start = jnp.clip(cq - WIN_W // 2, 0, GRID_W - WIN_W)
    col_ok = (cq[None, :] >= c_start[:, None]) & (cq[None, :] < c_start[:, None] + WIN_W)
    d_col = jnp.clip(cq[None, :] - cq[:, None], -(WIN_W - 1), WIN_W - 1) + (WIN_W - 1)

    def row_block(r):
        rs = jnp.clip(r - kh // 2, 0, rows - kh)
        q_r = lax.dynamic_index_in_dim(qg, r, axis=1, keepdims=False)
        k_r = lax.dynamic_slice_in_dim(kg, rs, kh, axis=1)
        v_r = lax.dynamic_slice_in_dim(vg, rs, kh, axis=1)
        s = jnp.einsum('bqhd,bkwhd->bqhkw', q_r, k_r) * scale
        d_row = rs + jnp.arange(kh) - r + (WIN_H - 1)
        bias = rpb[:, d_row[None, :, None], d_col[:, None, :]]
        s = jnp.where(col_ok[:, None, None, :], s + jnp.transpose(bias, (1, 0, 2, 3)), -jnp.inf)
        s_c = jnp.einsum('bqhd,bchd->bqhc', q_r, kc32) * scale
        p = jax.nn.softmax(jnp.concatenate(
            [s.reshape(bsz, GRID_W, GROUP_HEADS, kh * GRID_W), s_c], -1), -1)
        p_win = p[..., :kh * GRID_W].reshape(bsz, GRID_W, GROUP_HEADS, kh, GRID_W)
        return (jnp.einsum('bqhkw,bkwhd->bqhd', p_win, v_r)
                + jnp.einsum('bqhc,bchd->bqhd', p[..., kh * GRID_W:], vc32))

    o = lax.map(row_block, jnp.arange(rows))
    out_lat = jnp.moveaxis(o, 0, 1).reshape(bsz, lat_len, GW)
    out_ctx = _softmax_attn(qc, kc, vc).reshape(bsz, qc.shape[1], GW) if ctx_out else None
    return out_lat, out_ctx


def _peer(h, w_q, keys, u_tab, v_tab):
    bsz, t, d = h.shape
    blocks = h.reshape(bsz * t // PEER_BLOCK, PEER_BLOCK, d)
    keys = keys.astype(F32)

    def block(tb):
        q = (tb @ w_q).astype(F32).reshape(PEER_BLOCK, PEER_HEADS, 2, PEER_HALF)
        s = jnp.einsum('thpd,hpnd->thpn', q, keys)
        top_s, top_i = lax.top_k(s, PEER_TOPK)
        cand_s = (top_s[:, :, 0, :, None] + top_s[:, :, 1, None, :]).reshape(
            PEER_BLOCK, PEER_HEADS, PEER_TOPK * PEER_TOPK)
        cand_i = (top_i[:, :, 0, :, None] * PEER_KEYS + top_i[:, :, 1, None, :]).reshape(
            PEER_BLOCK, PEER_HEADS, PEER_TOPK * PEER_TOPK)
        best_s, pos = lax.top_k(cand_s, PEER_TOPK)
        idx = jnp.take_along_axis(cand_i, pos, axis=-1)
        gate = jax.nn.softmax(best_s, -1)
        u = jnp.take(u_tab, idx, axis=0).astype(F32)
        act = jax.nn.gelu(jnp.einsum('td,thkd->thk', tb.astype(F32), u), approximate=False)
        v = jnp.take(v_tab, idx, axis=0).astype(F32)
        return jnp.einsum('thk,thkd->td', gate * act, v)

    return lax.map(block, blocks).reshape(bsz, t, d)


def _qkv_heads(p):
    return [_heads(z) for z in _split(p, [GW, GW, GW])]


def setup_inputs(seed: int = 0) -> dict:
    key = jax.random.key(seed)
    ks = iter(jax.random.split(key, 48))
    nrm = lambda shape, std: jax.random.normal(next(ks), shape, F32) * std
    uni = lambda shape, lo, hi: jax.random.uniform(next(ks), shape, F32, lo, hi)
    L, D = DEPTH, D_MODEL
    return {
        'x': nrm((BATCH, SEQ, D), 1.0),
        'c': nrm((BATCH, D), 1.0),
        'ctx': nrm((BATCH, CTX_LEN, D), 1.0),
        'c_ctx': nrm((D,), 1.0),
        'w_ada': nrm((L, D, 6 * D), 0.5 * D ** -0.5),
        'b_ada': nrm((L, 6 * D), 0.02),
        'w_in': nrm((L, D, IN_COLS), D ** -0.5),
        'w_out': nrm((L, D_MIX, D), DN_BETA * D_MIX ** -0.5),
        'ln_mix_g': 1.0 + nrm((L, D), 0.02),
        'ln_mix_b': nrm((L, D), 0.02),
        'ln_ffn_g': 1.0 + nrm((L, D), 0.02),
        'ln_ffn_b': nrm((L, D), 0.02),
        'diff_lam': nrm((L, 4, DIFF_HALF), 0.1),
        'diff_norm_g': 1.0 + nrm((L, HEAD_DIM), 0.02),
        'rwkv_mu': uni((L, 2, RWKV_COLS), 0.0, 0.5),
        'rwkv_w0': uni((L, 2, GW), -6.0, 0.0),
        'rwkv_w2': nrm((L, 2, DECAY_LORA, GW), 0.1),
        'rwkv_a0': nrm((L, 2, GW), 0.3),
        'rwkv_a2': nrm((L, 2, ICLR_LORA, GW), 0.5 * ICLR_LORA ** -0.5),
        'rwkv_g2': nrm((L, GATE_LORA, GW), GATE_LORA ** -0.5),
        'rwkv_kk': 0.85 + nrm((L, GW), 0.05),
        'rwkv_ka': 1.0 + nrm((L, GW), 0.05),
        'rwkv_rk': nrm((L, GROUP_HEADS, HEAD_DIM), 0.1),
        'rwkv_ln_g': 1.0 + nrm((L, GW), 0.02),
        'rwkv_ln_b': nrm((L, GW), 0.02),
        'gdn_conv': nrm((L, GDN_CONV, 3 * GW), GDN_CONV ** -0.5),
        'gdn_a_log': jnp.log(uni((L, 2, GROUP_HEADS), 1.0, 16.0)),
        'gdn_dt_bias': uni((L, 2, GROUP_HEADS), -5.0, -2.0),
        'gdn_norm_g': 1.0 + nrm((L, HEAD_DIM), 0.02),
        'nat_rpb': nrm((L, GROUP_HEADS, 2 * WIN_H - 1, 2 * WIN_W - 1), 0.2),
        'peer_wq': nrm((L, D, PEER_HEADS * PEER_QDIM), D ** -0.5),
        'peer_keys': nrm((L, PEER_HEADS, 2, PEER_KEYS, PEER_HALF), PEER_HALF ** -0.5),
        'peer_u': nrm((L, PEER_EXPERTS, D), D ** -0.5),
        'peer_v': nrm((L, PEER_EXPERTS, D), DN_BETA),
    }


def reference(x, c, ctx, c_ctx, w_ada, b_ada, w_in, w_out, ln_mix_g, ln_mix_b, ln_ffn_g, ln_ffn_b,
              diff_lam, diff_norm_g, rwkv_mu, rwkv_w0, rwkv_w2, rwkv_a0, rwkv_a2, rwkv_g2,
              rwkv_kk, rwkv_ka, rwkv_rk, rwkv_ln_g, rwkv_ln_b, gdn_conv, gdn_a_log, gdn_dt_bias,
              gdn_norm_g, nat_rpb, peer_wq, peer_keys, peer_u, peer_v):
    dtype = x.dtype
    h_lat, h_ctx = x, ctx
    col_sizes = [ATTN_COLS, RWKV_COLS, GDN_COLS, ATTN_COLS]
    for l in range(DEPTH):
        last = l == DEPTH - 1
        lam_init = 0.8 - 0.6 * math.exp(-0.3 * l)
        m_lat = jnp.split((jax.nn.silu(c) @ w_ada[l] + b_ada[l])[:, None, :], 6, axis=-1)
        m_ctx = jnp.split((jax.nn.silu(c_ctx) @ w_ada[l] + b_ada[l])[None, None, :], 6, axis=-1)
        u_lat = _modulate(h_lat, m_lat[0], m_lat[1], dtype)
        u_ctx = _modulate(h_ctx, m_ctx[0], m_ctx[1], dtype)
        pa, pb, pc, pd = _split(u_lat @ w_in[l], col_sizes)
        ca, cb, cc, cd = _split(u_ctx @ w_in[l], col_sizes)
        qa, ka, va = _qkv_heads(pa)
        qca, kca, vca = _qkv_heads(ca)
        ya, yca = _diff_attention(qa, ka, va, qca, kca, vca, diff_lam[l], diff_norm_g[l],
                                  lam_init, not last)
        yb, ycb = _rwkv7(pb, cb, rwkv_mu[l], rwkv_w0[l], rwkv_w2[l], rwkv_a0[l], rwkv_a2[l],
                         rwkv_g2[l], rwkv_kk[l], rwkv_ka[l], rwkv_rk[l], rwkv_ln_g[l],
                         rwkv_ln_b[l], not last)
        yc, ycc = _gated_deltanet(pc, cc, gdn_conv[l], gdn_a_log[l], gdn_dt_bias[l],
                                  gdn_norm_g[l], not last)
        qd, kd, vd = _qkv_heads(pd)
        qcd, kcd, vcd = _qkv_heads(cd)
        yd, ycd = _neighbourhood_attention(qd, kd, vd, qcd, kcd, vcd, nat_rpb[l], not last)
        mix = jnp.concatenate([ya, yb, yc, yd], -1).astype(dtype) @ w_out[l]
        h_lat = _post_norm(h_lat, m_lat[2], mix, ln_mix_g[l], ln_mix_b[l], dtype)
        ffn = _peer(_modulate(h_lat, m_lat[3], m_lat[4], dtype), peer_wq[l], peer_keys[l],
                    peer_u[l], peer_v[l])
        h_lat = _post_norm(h_lat, m_lat[5], ffn, ln_ffn_g[l], ln_ffn_b[l], dtype)
        if not last:
            mix_c = jnp.concatenate([yca, ycb, ycc, ycd], -1).astype(dtype) @ w_out[l]
            h_ctx = _post_norm(h_ctx, m_ctx[2], mix_c, ln_mix_g[l], ln_mix_b[l], dtype)
            ffn_c = _peer(_modulate(h_ctx, m_ctx[3], m_ctx[4], dtype), peer_wq[l], peer_keys[l],
                          peer_u[l], peer_v[l])
            h_ctx = _post_norm(h_ctx, m_ctx[5], ffn_c, ln_ffn_g[l], ln_ffn_b[l], dtype)
    return h_lat
```

```python
import functools
import math

import jax
import jax.numpy as jnp
import numpy as np
from jax import lax
from jax.experimental import pallas as pl
from jax.experimental.pallas import tpu as pltpu

D_MODEL = 1024
DEPTH = 2
GRID_W = 64
HEAD_DIM = 64
N_GROUPS = 4
GROUP_HEADS = D_MODEL // (N_GROUPS * HEAD_DIM)
GW = GROUP_HEADS * HEAD_DIM
D_MIX = N_GROUPS * GW
DIFF_HALF = HEAD_DIM // 2
Q_BLOCK = 128
ROPE_BASE = 10000.0
DECAY_LORA = 64
ICLR_LORA = 64
GATE_LORA = 128
RWKV_GN_EPS = 64e-5
RWKV_COLS = 3 * GW + 2 * DECAY_LORA + 2 * ICLR_LORA + GATE_LORA
GDN_CONV = 3
GDN_CHUNK = 64
GDN_COLS = 4 * GW + 4 * GROUP_HEADS
WIN_H = 8
WIN_W = 16
ATTN_COLS = 3 * GW
IN_COLS = ATTN_COLS + RWKV_COLS + GDN_COLS + ATTN_COLS
PEER_HEADS = 8
PEER_KEYS = 128
PEER_EXPERTS = PEER_KEYS * PEER_KEYS
PEER_QDIM = 256
PEER_HALF = PEER_QDIM // 2
PEER_TOPK = 16
DN_ALPHA = (2 * DEPTH) ** 0.25
LN_EPS = 1e-5

F32 = jnp.float32
BF16 = jnp.bfloat16

V7X_VMEM_LIMIT_BYTES = 56 * 1024 * 1024
NEG_BIG = -3.0e38


def _split(x, sizes):
    offs = np.cumsum(sizes)[:-1].tolist()
    return jnp.split(x, offs, axis=-1)


def _heads(x):
    return x.reshape(x.shape[0], x.shape[1], -1, HEAD_DIM)


def _layer_norm(x):
    xf = x.astype(F32)
    mu = xf.mean(-1, keepdims=True)
    var = jnp.square(xf - mu).mean(-1, keepdims=True)
    return (xf - mu) * lax.rsqrt(var + LN_EPS)


def _rms_norm(x, g):
    xf = x.astype(F32)
    return xf * lax.rsqrt(jnp.mean(xf * xf, -1, keepdims=True) + 1e-6) * g.astype(F32)


def _l2norm(x):
    xf = x.astype(F32)
    return xf * lax.rsqrt(jnp.sum(xf * xf, -1, keepdims=True) + 1e-6)


def _modulate(h, shift, scale, dtype):
    return (_layer_norm(h) * (1.0 + scale) + shift).astype(dtype)


def _post_norm(h, gate, y, g, b, dtype):
    z = DN_ALPHA * h.astype(F32) + gate.astype(F32) * y.astype(F32)
    return (_layer_norm(z) * g + b).astype(dtype)


def _dwconv_centred(x, w):
    k = w.shape[0]
    pad = k // 2
    t = x.shape[1]
    xp = jnp.pad(x, ((0, 0), (pad, pad), (0, 0)))
    out = xp[:, 0:t] * w[0]
    for i in range(1, k):
        out = out + xp[:, i:i + t] * w[i]
    return out


def _grid_pos(n_tok):
    t = jnp.arange(n_tok, dtype=jnp.int32)
    return t // GRID_W, t % GRID_W


def _rope_1d(x, pos):
    half = x.shape[-1] // 2
    inv = ROPE_BASE ** (-jnp.arange(half, dtype=F32) / half)
    ang = pos.astype(F32)[:, None] * inv
    cos = jnp.cos(ang)[:, None, :]
    sin = jnp.sin(ang)[:, None, :]
    x1 = x[..., :half].astype(F32)
    x2 = x[..., half:].astype(F32)
    return jnp.concatenate([x1 * cos - x2 * sin, x1 * sin + x2 * cos], -1)


def _rope_2d(x, row, col):
    n = x.shape[-1] // 2
    return jnp.concatenate([_rope_1d(x[..., :n], row), _rope_1d(x[..., n:], col)], -1)


def _diff_core(q, k, v, lam):
    scale = DIFF_HALF ** -0.5
    q = q.astype(F32)
    k = k.astype(F32)
    s1 = jnp.einsum('bqhd,bkhd->bhqk', q[..., :DIFF_HALF], k[..., :DIFF_HALF]) * scale
    s2 = jnp.einsum('bqhd,bkhd->bhqk', q[..., DIFF_HALF:], k[..., DIFF_HALF:]) * scale
    p = jax.nn.softmax(s1, -1) - lam * jax.nn.softmax(s2, -1)
    return jnp.einsum('bhqk,bkhd->bqhd', p, v.astype(F32))


def _diff_attention(q, k, v, qc, kc, vc, lam_vecs, norm_g, lam_init, ctx_out):
    bsz, lat_len = q.shape[:2]
    row, col = _grid_pos(lat_len)
    rot = lambda z: jnp.concatenate([_rope_2d(z[..., :DIFF_HALF], row, col),
                                     _rope_2d(z[..., DIFF_HALF:], row, col)], -1)
    q = rot(q)
    k = rot(k)
    lv = lam_vecs.astype(F32)
    lam = jnp.exp(jnp.sum(lv[0] * lv[1])) - jnp.exp(jnp.sum(lv[2] * lv[3])) + lam_init
    k_all = jnp.concatenate([k, kc.astype(F32)], 1)
    v_all = jnp.concatenate([v.astype(F32), vc.astype(F32)], 1)
    nb = lat_len // Q_BLOCK
    qb = jnp.moveaxis(q.reshape(bsz, nb, Q_BLOCK, GROUP_HEADS, HEAD_DIM), 1, 0)
    ob = lax.map(lambda qq: _diff_core(qq, k_all, v_all, lam), qb)
    o = jnp.moveaxis(ob, 0, 1).reshape(bsz, lat_len, GROUP_HEADS, HEAD_DIM)
    post = lambda z: (_rms_norm(z, norm_g) * (1.0 - lam_init)).reshape(z.shape[0], z.shape[1], GW)
    out_ctx = post(_diff_core(qc, kc, vc, lam)) if ctx_out else None
    return post(o), out_ctx


def _wkv7_scan(r, w, k, v, kk, a, s0, reverse):
    def step(s, inp):
        r_t, w_t, k_t, v_t, kk_t, a_t = inp
        sa = jnp.einsum('bhvk,bhk->bhv', s, kk_t)
        s = (s * w_t[:, :, None, :] - sa[..., None] * (kk_t * a_t)[:, :, None, :]
             + v_t[..., None] * k_t[:, :, None, :])
        return s, jnp.einsum('bhvk,bhk->bhv', s, r_t)
    xs = tuple(jnp.moveaxis(z, 1, 0) for z in (r, w, k, v, kk, a))
    s, y = lax.scan(step, s0, xs, reverse=reverse)
    return jnp.moveaxis(y, 0, 1), s


def _rwkv7(f_lat, f_ctx, mu, w0, w2, a0, a2, g2, k_k, k_a, r_k, ln_g, ln_b, ctx_out):
    shift_w = jnp.stack([mu[0], 1.0 - mu[0] - mu[1], mu[1]])

    def prep(f):
        f = _dwconv_centred(f, shift_w).astype(F32)
        r, k, v, wd, ad, gd = _split(f, [GW, GW, GW, 2 * DECAY_LORA, 2 * ICLR_LORA, GATE_LORA])
        bsz, t = f.shape[:2]
        wd = jnp.tanh(wd.reshape(bsz, t, 2, DECAY_LORA))
        ad = ad.reshape(bsz, t, 2, ICLR_LORA)
        w_raw = w0 + jnp.einsum('btdr,drc->btdc', wd, w2)
        decay = jnp.exp(-jnp.exp(-jax.nn.softplus(-w_raw) - 0.5))
        a = jax.nn.sigmoid(a0 + jnp.einsum('btdr,drc->btdc', ad, a2))
        g = jax.nn.sigmoid(gd) @ g2
        kk = _l2norm(_heads(k * k_k))
        kd = k[:, :, None] * (1.0 + (a - 1.0) * k_a)
        return r, v, g, kk, decay, a, kd

    def run(p, d, s0, rev):
        r, v, g, kk, decay, a, kd = p
        return _wkv7_scan(_heads(r), _heads(decay[:, :, d]), _heads(kd[:, :, d]), _heads(v),
                          kk, _heads(a[:, :, d]), s0, rev)

    def post(p, ys):
        r, v, g, kk, decay, a, kd = p
        bsz, t = r.shape[:2]
        y = ys[0] + ys[1]
        m = y.mean(-1, keepdims=True)
        var = jnp.square(y - m).mean(-1, keepdims=True)
        yn = ((y - m) * lax.rsqrt(var + RWKV_GN_EPS)).reshape(bsz, t, GW) * ln_g + ln_b
        rh, vh = _heads(r), _heads(v)
        bonus = ((rh * _heads(kd[:, :, 0]) * r_k).sum(-1, keepdims=True) * vh
                 + (rh * _heads(kd[:, :, 1]) * r_k).sum(-1, keepdims=True) * vh)
        return (yn + bonus.reshape(bsz, t, GW)) * g

    pl_, pc = prep(f_lat), prep(f_ctx)
    s0 = jnp.zeros((f_lat.shape[0], GROUP_HEADS, HEAD_DIM, HEAD_DIM), F32)
    y_lat, y_ctx = [], []
    for d, rev in ((0, False), (1, True)):
        yc, sc = run(pc, d, s0, rev)
        yl, _ = run(pl_, d, sc, rev)
        y_lat.append(yl)
        y_ctx.append(yc)
    out_ctx = post(pc, y_ctx) if ctx_out else None
    return post(pl_, y_lat), out_ctx


def _gdn_chunked(q, k, v, beta, g, s0):
    bsz, t, h, dk = q.shape
    dv = v.shape[-1]
    n = t // GDN_CHUNK
    ch = lambda z: jnp.moveaxis(z.reshape(bsz, n, GDN_CHUNK, h, *z.shape[3:]), 3, 2)
    q, k, v, beta, g = ch(q), ch(k), ch(v), ch(beta), ch(g)
    gc = jnp.cumsum(g, axis=-1)
    i = jnp.arange(GDN_CHUNK)
    incl = i[:, None] >= i[None, :]
    strict = i[:, None] > i[None, :]
    decay = jnp.exp(jnp.where(incl, gc[..., :, None] - gc[..., None, :], -jnp.inf))
    kb = k * beta[..., None]
    a_low = jnp.where(strict, jnp.einsum('bnhid,bnhjd->bnhij', kb, k) * decay, 0.0)
    tmat = a_low + jnp.eye(GDN_CHUNK, dtype=F32)
    rhs = jnp.concatenate([v * beta[..., None], kb * jnp.exp(gc)[..., None]], -1)
    sol = lax.linalg.triangular_solve(tmat, rhs, left_side=True, lower=True, unit_diagonal=True)
    u, w = sol[..., :dv], sol[..., dv:]
    qk = jnp.where(incl, jnp.einsum('bnhid,bnhjd->bnhij', q, k) * decay, 0.0)
    qg = q * jnp.exp(gc)[..., None]
    kg = k * jnp.exp(gc[..., -1:] - gc)[..., None]
    glast = jnp.exp(gc[..., -1])

    def step(s, xs):
        qg_i, kg_i, u_i, w_i, qk_i, gl_i = xs
        v_new = u_i - jnp.einsum('bhcd,bhdv->bhcv', w_i, s)
        o = jnp.einsum('bhcd,bhdv->bhcv', qg_i, s) + jnp.einsum('bhij,bhjv->bhiv', qk_i, v_new)
        s = s * gl_i[..., None, None] + jnp.einsum('bhcd,bhcv->bhdv', kg_i, v_new)
        return s, o

    xs = tuple(jnp.moveaxis(z, 1, 0) for z in (qg, kg, u, w, qk, glast))
    s, o = lax.scan(step, s0, xs)
    o = jnp.moveaxis(jnp.moveaxis(o, 0, 1), 2, 3).reshape(bsz, t, h, dv)
    return o, s


def _gated_deltanet(f_lat, f_ctx, conv_w, a_log, dt_bias, norm_g, ctx_out):
    def prep(f):
        qkv, gate, ab = _split(f, [3 * GW, GW, 4 * GROUP_HEADS])
        qkv = jax.nn.silu(_dwconv_centred(qkv, conv_w).astype(F32))
        q, k, v = [_heads(z) for z in _split(qkv, [GW, GW, GW])]
        q = _l2norm(q) * HEAD_DIM ** -0.5
        k = _l2norm(k)
        ab = ab.astype(F32).reshape(f.shape[0], f.shape[1], 2, 2, GROUP_HEADS)
        log_alpha = -jnp.exp(a_log) * jax.nn.softplus(ab[:, :, 0] + dt_bias)
        beta = jax.nn.sigmoid(ab[:, :, 1])
        return q, k, v, gate, log_alpha, beta

    def run(p, d, s0):
        tr = (lambda z: jnp.flip(z, 1)) if d == 1 else (lambda z: z)
        q, k, v, gate, log_alpha, beta = p
        o, s = _gdn_chunked(tr(q), tr(k), tr(v), tr(beta[:, :, d]), tr(log_alpha[:, :, d]), s0)
        return tr(o), s

    def post(p, os_):
        gate = p[3]
        o = os_[0] + os_[1]
        y = _rms_norm(o, norm_g) * jax.nn.silu(_heads(gate.astype(F32)))
        return y.reshape(o.shape[0], o.shape[1], GW)

    pl_, pc = prep(f_lat), prep(f_ctx)
    s0 = jnp.zeros((f_lat.shape[0], GROUP_HEADS, HEAD_DIM, HEAD_DIM), F32)
    o_lat, o_ctx = [], []
    for d in range(2):
        oc, sc = run(pc, d, s0)
        ol, _ = run(pl_, d, sc)
        o_lat.append(ol)
        o_ctx.append(oc)
    out_ctx = post(pc, o_ctx) if ctx_out else None
    return post(pl_, o_lat), out_ctx


def _softmax_attn(q, k, v):
    s = jnp.einsum('bqhd,bkhd->bhqk', q.astype(F32), k.astype(F32)) * HEAD_DIM ** -0.5
    return jnp.einsum('bhqk,bkhd->bqhd', jax.nn.softmax(s, -1), v.astype(F32))


def _neighbourhood_attention(q, k, v, qc, kc, vc, rpb, ctx_out):
    bsz, lat_len = q.shape[:2]
    rows = lat_len // GRID_W
    kh = min(WIN_H, rows)
    scale = HEAD_DIM ** -0.5
    grid = lambda z: z.astype(F32).reshape(bsz, rows, GRID_W, GROUP_HEADS, HEAD_DIM)
    qg, kg, vg = grid(q), grid(k), grid(v)
    kc32, vc32 = kc.astype(F32), vc.astype(F32)
    rpb = rpb.astype(F32)
    cq = jnp.arange(GRID_W)
    c_start = jnp.clip(cq - WIN_W // 2, 0, GRID_W - WIN_W)
    col_ok = (cq[None, :] >= c_start[:, None]) & (cq[None, :] < c_start[:, None] + WIN_W)
    d_col = jnp.clip(cq[None, :] - cq[:, None], -(WIN_W - 1), WIN_W - 1) + (WIN_W - 1)

    def row_block(r):
        rs = jnp.clip(r - kh // 2, 0, rows - kh)
        q_r = lax.dynamic_index_in_dim(qg, r, axis=1, keepdims=False)
        k_r = lax.dynamic_slice_in_dim(kg, rs, kh, axis=1)
        v_r = lax.dynamic_slice_in_dim(vg, rs, kh, axis=1)
        s = jnp.einsum('bqhd,bkwhd->bqhkw', q_r, k_r) * scale
        d_row = rs + jnp.arange(kh) - r + (WIN_H - 1)
        bias = rpb[:, d_row[None, :, None], d_col[:, None, :]]
        s = jnp.where(col_ok[:, None, None, :], s + jnp.transpose(bias, (1, 0, 2, 3)), -jnp.inf)
        s_c = jnp.einsum('bqhd,bchd->bqhc', q_r, kc32) * scale
        p = jax.nn.softmax(jnp.concatenate(
            [s.reshape(bsz, GRID_W, GROUP_HEADS, kh * GRID_W), s_c], -1), -1)
        p_win = p[..., :kh * GRID_W].reshape(bsz, GRID_W, GROUP_HEADS, kh, GRID_W)
        return (jnp.einsum('bqhkw,bkwhd->bqhd', p_win, v_r)
                + jnp.einsum('bqhc,bchd->bqhd', p[..., kh * GRID_W:], vc32))

    o = lax.map(row_block, jnp.arange(rows))
    out_lat = jnp.moveaxis(o, 0, 1).reshape(bsz, lat_len, GW)
    out_ctx = _softmax_attn(qc, kc, vc).reshape(bsz, qc.shape[1], GW) if ctx_out else None
    return out_lat, out_ctx


def _qkv_heads(p):
    return [_heads(z) for z in _split(p, [GW, GW, GW])]


PEER_STAT_ROWS = 4 * PEER_HEADS


def _topk_rows(x, k):
    rows = []
    cur = x
    for i in range(k):
        m = jnp.max(cur, axis=0, keepdims=True)
        rows.append(m)
        if i + 1 < k:
            cur = jnp.where(cur == m, NEG_BIG, cur)
    return rows


def _peer_score_kernel(x_ref, wq_ref, keys_ref, s_ref, st_ref):
    q = jnp.dot(x_ref[...], wq_ref[...], preferred_element_type=F32).astype(BF16)
    stats = []
    for h in range(PEER_HEADS):
        tops = []
        for p in range(2):
            hp = 2 * h + p
            s_t = lax.dot_general(keys_ref[hp], q[:, hp * PEER_HALF:(hp + 1) * PEER_HALF],
                                  (((1,), (1,)), ((), ())), preferred_element_type=F32)
            s_ref[hp] = s_t
            tops.append(_topk_rows(s_t, PEER_TOPK))
        top_b = jnp.concatenate(tops[1], axis=0)
        cand = jnp.concatenate([tops[0][i] + top_b for i in range(PEER_TOPK)], axis=0)
        tau = _topk_rows(cand, PEER_TOPK)[-1]
        best = tops[0][0] + tops[1][0]
        z = jnp.sum(jnp.where(cand >= tau, jnp.exp(cand - best), 0.0), axis=0, keepdims=True)
        stats += [tau, tops[0][0], tops[1][0], 1.0 / z]
    st_ref[...] = jnp.concatenate(stats, axis=0)


def _peer_expert_kernel(x_ref, s_ref, st_ref, u_ref, vt_ref, o_ref, e_ref, acc_ref, *, rows_per_step):
    j = pl.program_id(1)

    @pl.when(j == 0)
    def _():
        acc_ref[...] = jnp.zeros_like(acc_ref)
        for h in range(PEER_HEADS):
            a0 = st_ref[4 * h + 1:4 * h + 2, :]
            b0 = st_ref[4 * h + 2:4 * h + 3, :]
            rz = st_ref[4 * h + 3:4 * h + 4, :]
            e_ref[2 * h] = jnp.exp(s_ref[2 * h] - a0) * rz
            e_ref[2 * h + 1] = jnp.exp(s_ref[2 * h + 1] - b0)

    x = x_ref[...]
    for il in range(rows_per_step):
        i = j * rows_per_step + il
        act = lax.dot_general(u_ref[il * PEER_KEYS:(il + 1) * PEER_KEYS, :], x,
                              (((1,), (1,)), ((), ())), preferred_element_type=F32)
        act = 0.5 * act * (1.0 + lax.erf(act * (2.0 ** -0.5)))
        gate = jnp.zeros_like(act)
        for h in range(PEER_HEADS):
            c = s_ref[2 * h, pl.ds(i, 1), :] + s_ref[2 * h + 1]
            tau = st_ref[4 * h:4 * h + 1, :]
            gate = gate + jnp.where(c >= tau, e_ref[2 * h, pl.ds(i, 1), :] * e_ref[2 * h + 1], 0.0)
        w = (act * gate).astype(BF16)
        acc_ref[...] += jnp.dot(vt_ref[:, il * PEER_KEYS:(il + 1) * PEER_KEYS], w,
                                preferred_element_type=F32)

    @pl.when(j == pl.num_programs(1) - 1)
    def _():
        o_ref[...] = acc_ref[...].T


def _peer_pallas(h, w_q, keys, u_tab, vt_tab, *, tb, rows_per_step):
    n_tok, d = h.shape
    nhp = 2 * PEER_HEADS
    s, st = pl.pallas_call(
        _peer_score_kernel,
        name="peer_scores",
        grid=(n_tok // tb,),
        in_specs=[pl.BlockSpec((tb, d), lambda i: (i, 0)),
                  pl.BlockSpec(w_q.shape, lambda i: (0, 0)),
                  pl.BlockSpec(keys.shape, lambda i: (0, 0, 0))],
        out_specs=[pl.BlockSpec((nhp, PEER_KEYS, tb), lambda i: (0, 0, i)),
                   pl.BlockSpec((PEER_STAT_ROWS, tb), lambda i: (0, i))],
        out_shape=[jax.ShapeDtypeStruct((nhp, PEER_KEYS, n_tok), F32),
                   jax.ShapeDtypeStruct((PEER_STAT_ROWS, n_tok), F32)],
        compiler_params=pltpu.CompilerParams(dimension_semantics=("arbitrary",),
                                             vmem_limit_bytes=V7X_VMEM_LIMIT_BYTES),
    )(h, w_q, keys)
    ec = rows_per_step * PEER_KEYS
    return pl.pallas_call(
        functools.partial(_peer_expert_kernel, rows_per_step=rows_per_step),
        name="peer_experts",
        grid=(n_tok // tb, PEER_EXPERTS // ec),
        in_specs=[pl.BlockSpec((tb, d), lambda i, j: (i, 0)),
                  pl.BlockSpec((nhp, PEER_KEYS, tb), lambda i, j: (0, 0, i)),
                  pl.BlockSpec((PEER_STAT_ROWS, tb), lambda i, j: (0, i)),
                  pl.BlockSpec((ec, d), lambda i, j: (j, 0)),
                  pl.BlockSpec((d, ec), lambda i, j: (0, j))],
        out_specs=pl.BlockSpec((tb, d), lambda i, j: (i, 0)),
        out_shape=jax.ShapeDtypeStruct((n_tok, d), F32),
        scratch_shapes=[pltpu.VMEM((nhp, PEER_KEYS, tb), F32),
                        pltpu.VMEM((d, tb), F32)],
        compiler_params=pltpu.CompilerParams(dimension_semantics=("arbitrary", "arbitrary"),
                                             vmem_limit_bytes=V7X_VMEM_LIMIT_BYTES),
    )(h, s, st, u_tab, vt_tab)


def _peer(h, w_q, keys, u_tab, vt_tab):
    bsz, t, d = h.shape
    n_tok = bsz * t
    tb = 512 if n_tok % 512 == 0 else 256
    out = _peer_pallas(h.reshape(n_tok, d).astype(BF16), w_q, keys, u_tab, vt_tab, tb=tb, rows_per_step=8)
    return out.reshape(bsz, t, d)


def kernel(x, c, ctx, c_ctx, w_ada, b_ada, w_in, w_out, ln_mix_g, ln_mix_b, ln_ffn_g, ln_ffn_b, diff_lam, diff_norm_g, rwkv_mu, rwkv_w0, rwkv_w2, rwkv_a0, rwkv_a2, rwkv_g2, rwkv_kk, rwkv_ka, rwkv_rk, rwkv_ln_g, rwkv_ln_b, gdn_conv, gdn_a_log, gdn_dt_bias, gdn_norm_g, nat_rpb, peer_wq, peer_keys, peer_u, peer_v):
    dtype = x.dtype
    h_lat, h_ctx = x, ctx
    col_sizes = [ATTN_COLS, RWKV_COLS, GDN_COLS, ATTN_COLS]
    for l in range(DEPTH):
        last = l == DEPTH - 1
        lam_init = 0.8 - 0.6 * math.exp(-0.3 * l)
        m_lat = jnp.split((jax.nn.silu(c) @ w_ada[l] + b_ada[l])[:, None, :], 6, axis=-1)
        m_ctx = jnp.split((jax.nn.silu(c_ctx) @ w_ada[l] + b_ada[l])[None, None, :], 6, axis=-1)
        u_lat = _modulate(h_lat, m_lat[0], m_lat[1], dtype)
        u_ctx = _modulate(h_ctx, m_ctx[0], m_ctx[1], dtype)
        pa, pb, pc, pd = _split(u_lat @ w_in[l], col_sizes)
        ca, cb, cc, cd = _split(u_ctx @ w_in[l], col_sizes)
        qa, ka, va = _qkv_heads(pa)
        qca, kca, vca = _qkv_heads(ca)
        ya, yca = _diff_attention(qa, ka, va, qca, kca, vca, diff_lam[l], diff_norm_g[l],
                                  lam_init, not last)
        yb, ycb = _rwkv7(pb, cb, rwkv_mu[l], rwkv_w0[l], rwkv_w2[l], rwkv_a0[l], rwkv_a2[l],
                         rwkv_g2[l], rwkv_kk[l], rwkv_ka[l], rwkv_rk[l], rwkv_ln_g[l],
                         rwkv_ln_b[l], not last)
        yc, ycc = _gated_deltanet(pc, cc, gdn_conv[l], gdn_a_log[l], gdn_dt_bias[l],
                                  gdn_norm_g[l], not last)
        qd, kd, vd = _qkv_heads(pd)
        qcd, kcd, vcd = _qkv_heads(cd)
        yd, ycd = _neighbourhood_attention(qd, kd, vd, qcd, kcd, vcd, nat_rpb[l], not last)
        mix = jnp.concatenate([ya, yb, yc, yd], -1).astype(dtype) @ w_out[l]
        h_lat = _post_norm(h_lat, m_lat[2], mix, ln_mix_g[l], ln_mix_b[l], dtype)
        wq_b = peer_wq[l].astype(BF16)
        keys_b = peer_keys[l].reshape(2 * PEER_HEADS, PEER_KEYS, PEER_HALF).astype(BF16)
        u_b = peer_u[l].astype(BF16)
        vt_b = peer_v[l].astype(BF16).T
        ffn = _peer(_modulate(h_lat, m_lat[3], m_lat[4], dtype), wq_b, keys_b, u_b, vt_b)
        h_lat = _post_norm(h_lat, m_lat[5], ffn, ln_ffn_g[l], ln_ffn_b[l], dtype)
        if not last:
            mix_c = jnp.concatenate([yca, ycb, ycc, ycd], -1).astype(dtype) @ w_out[l]
            h_ctx = _post_norm(h_ctx, m_ctx[2], mix_c, ln_mix_g[l], ln_mix_b[l], dtype)
            ffn_c = _peer(_modulate(h_ctx, m_ctx[3], m_ctx[4], dtype), wq_b, keys_b, u_b, vt_b)
            h_ctx = _post_norm(h_ctx, m_ctx[5], ffn_c, ln_ffn_g[l], ln_ffn_b[l], dtype)
    return h_lat
```

```python
import functools
import math

import jax
import jax.numpy as jnp
import numpy as np
from jax import lax
from jax.experimental import pallas as pl
from jax.experimental.pallas import tpu as pltpu

D_MODEL = 1024
DEPTH = 2
GRID_W = 64
HEAD_DIM = 64
N_GROUPS = 4
GROUP_HEADS = D_MODEL // (N_GROUPS * HEAD_DIM)
GW = GROUP_HEADS * HEAD_DIM
D_MIX = N_GROUPS * GW
DIFF_HALF = HEAD_DIM // 2
Q_BLOCK = 128
ROPE_BASE = 10000.0
DECAY_LORA = 64
ICLR_LORA = 64
GATE_LORA = 128
RWKV_GN_EPS = 64e-5
RWKV_COLS = 3 * GW + 2 * DECAY_LORA + 2 * ICLR_LORA + GATE_LORA
GDN_CONV = 3
GDN_CHUNK = 64
GDN_COLS = 4 * GW + 4 * GROUP_HEADS
WIN_H = 8
WIN_W = 16
ATTN_COLS = 3 * GW
IN_COLS = ATTN_COLS + RWKV_COLS + GDN_COLS + ATTN_COLS
PEER_HEADS = 8
PEER_KEYS = 128
PEER_EXPERTS = PEER_KEYS * PEER_KEYS
PEER_QDIM = 256
PEER_HALF = PEER_QDIM // 2
PEER_TOPK = 16
DN_ALPHA = (2 * DEPTH) ** 0.25
LN_EPS = 1e-5

F32 = jnp.float32
BF16 = jnp.bfloat16

V7X_VMEM_LIMIT_BYTES = 56 * 1024 * 1024
NEG_BIG = -3.0e38


def _split(x, sizes):
    offs = np.cumsum(sizes)[:-1].tolist()
    return jnp.split(x, offs, axis=-1)


def _heads(x):
    return x.reshape(x.shape[0], x.shape[1], -1, HEAD_DIM)


def _layer_norm(x):
    xf = x.astype(F32)
    mu = xf.mean(-1, keepdims=True)
    var = jnp.square(xf - mu).mean(-1, keepdims=True)
    return (xf - mu) * lax.rsqrt(var + LN_EPS)


def _rms_norm(x, g):
    xf = x.astype(F32)
    return xf * lax.rsqrt(jnp.mean(xf * xf, -1, keepdims=True) + 1e-6) * g.astype(F32)


def _l2norm(x):
    xf = x.astype(F32)
    return xf * lax.rsqrt(jnp.sum(xf * xf, -1, keepdims=True) + 1e-6)


def _modulate(h, shift, scale, dtype):
    return (_layer_norm(h) * (1.0 + scale) + shift).astype(dtype)


def _post_norm(h, gate, y, g, b, dtype):
    z = DN_ALPHA * h.astype(F32) + gate.astype(F32) * y.astype(F32)
    return (_layer_norm(z) * g + b).astype(dtype)


def _dwconv_centred(x, w):
    k = w.shape[0]
    pad = k // 2
    t = x.shape[1]
    xp = jnp.pad(x, ((0, 0), (pad, pad), (0, 0)))
    out = xp[:, 0:t] * w[0]
    for i in range(1, k):
        out = out + xp[:, i:i + t] * w[i]
    return out


def _grid_pos(n_tok):
    t = jnp.arange(n_tok, dtype=jnp.int32)
    return t // GRID_W, t % GRID_W


def _rope_1d(x, pos):
    half = x.shape[-1] // 2
    inv = ROPE_BASE ** (-jnp.arange(half, dtype=F32) / half)
    ang = pos.astype(F32)[:, None] * inv
    cos = jnp.cos(ang)[:, None, :]
    sin = jnp.sin(ang)[:, None, :]
    x1 = x[..., :half].astype(F32)
    x2 = x[..., half:].astype(F32)
    return jnp.concatenate([x1 * cos - x2 * sin, x1 * sin + x2 * cos], -1)


def _rope_2d(x, row, col):
    n = x.shape[-1] // 2
    return jnp.concatenate([_rope_1d(x[..., :n], row), _rope_1d(x[..., n:], col)], -1)


def _diff_core(q, k, v, lam):
    scale = DIFF_HALF ** -0.5
    q = q.astype(F32)
    k = k.astype(F32)
    s1 = jnp.einsum('bqhd,bkhd->bhqk', q[..., :DIFF_HALF], k[..., :DIFF_HALF]) * scale
    s2 = jnp.einsum('bqhd,bkhd->bhqk', q[..., DIFF_HALF:], k[..., DIFF_HALF:]) * scale
    p = jax.nn.softmax(s1, -1) - lam * jax.nn.softmax(s2, -1)
    return jnp.einsum('bhqk,bkhd->bqhd', p, v.astype(F32))


def _diff_attention(q, k, v, qc, kc, vc, lam_vecs, norm_g, lam_init, ctx_out):
    bsz, lat_len = q.shape[:2]
    row, col = _grid_pos(lat_len)
    rot = lambda z: jnp.concatenate([_rope_2d(z[..., :DIFF_HALF], row, col),
                                     _rope_2d(z[..., DIFF_HALF:], row, col)], -1)
    q = rot(q)
    k = rot(k)
    lv = lam_vecs.astype(F32)
    lam = jnp.exp(jnp.sum(lv[0] * lv[1])) - jnp.exp(jnp.sum(lv[2] * lv[3])) + lam_init
    k_all = jnp.concatenate([k, kc.astype(F32)], 1)
    v_all = jnp.concatenate([v.astype(F32), vc.astype(F32)], 1)
    nb = lat_len // Q_BLOCK
    qb = jnp.moveaxis(q.reshape(bsz, nb, Q_BLOCK, GROUP_HEADS, HEAD_DIM), 1, 0)
    ob = lax.map(lambda qq: _diff_core(qq, k_all, v_all, lam), qb)
    o = jnp.moveaxis(ob, 0, 1).reshape(bsz, lat_len, GROUP_HEADS, HEAD_DIM)
    post = lambda z: (_rms_norm(z, norm_g) * (1.0 - lam_init)).reshape(z.shape[0], z.shape[1], GW)
    out_ctx = post(_diff_core(qc, kc, vc, lam)) if ctx_out else None
    return post(o), out_ctx


def _wkv7_scan(r, w, k, v, kk, a, s0, reverse):
    def step(s, inp):
        r_t, w_t, k_t, v_t, kk_t, a_t = inp
        sa = jnp.einsum('bhvk,bhk->bhv', s, kk_t)
        s = (s * w_t[:, :, None, :] - sa[..., None] * (kk_t * a_t)[:, :, None, :]
             + v_t[..., None] * k_t[:, :, None, :])
        return s, jnp.einsum('bhvk,bhk->bhv', s, r_t)
    xs = tuple(jnp.moveaxis(z, 1, 0) for z in (r, w, k, v, kk, a))
    s, y = lax.scan(step, s0, xs, reverse=reverse)
    return jnp.moveaxis(y, 0, 1), s


def _rwkv7(f_lat, f_ctx, mu, w0, w2, a0, a2, g2, k_k, k_a, r_k, ln_g, ln_b, ctx_out):
    shift_w = jnp.stack([mu[0], 1.0 - mu[0] - mu[1], mu[1]])

    def prep(f):
        f = _dwconv_centred(f, shift_w).astype(F32)
        r, k, v, wd, ad, gd = _split(f, [GW, GW, GW, 2 * DECAY_LORA, 2 * ICLR_LORA, GATE_LORA])
        bsz, t = f.shape[:2]
        wd = jnp.tanh(wd.reshape(bsz, t, 2, DECAY_LORA))
        ad = ad.reshape(bsz, t, 2, ICLR_LORA)
        w_raw = w0 + jnp.einsum('btdr,drc->btdc', wd, w2)
        decay = jnp.exp(-jnp.exp(-jax.nn.softplus(-w_raw) - 0.5))
        a = jax.nn.sigmoid(a0 + jnp.einsum('btdr,drc->btdc', ad, a2))
        g = jax.nn.sigmoid(gd) @ g2
        kk = _l2norm(_heads(k * k_k))
        kd = k[:, :, None] * (1.0 + (a - 1.0) * k_a)
        return r, v, g, kk, decay, a, kd

    def run(p, d, s0, rev):
        r, v, g, kk, decay, a, kd = p
        return _wkv7_scan(_heads(r), _heads(decay[:, :, d]), _heads(kd[:, :, d]), _heads(v),
                          kk, _heads(a[:, :, d]), s0, rev)

    def post(p, ys):
        r, v, g, kk, decay, a, kd = p
        bsz, t = r.shape[:2]
        y = ys[0] + ys[1]
        m = y.mean(-1, keepdims=True)
        var = jnp.square(y - m).mean(-1, keepdims=True)
        yn = ((y - m) * lax.rsqrt(var + RWKV_GN_EPS)).reshape(bsz, t, GW) * ln_g + ln_b
        rh, vh = _heads(r), _heads(v)
        bonus = ((rh * _heads(kd[:, :, 0]) * r_k).sum(-1, keepdims=True) * vh
                 + (rh * _heads(kd[:, :, 1]) * r_k).sum(-1, keepdims=True) * vh)
        return (yn + bonus.reshape(bsz, t, GW)) * g

    pl_, pc = prep(f_lat), prep(f_ctx)
    s0 = jnp.zeros((f_lat.shape[0], GROUP_HEADS, HEAD_DIM, HEAD_DIM), F32)
    y_lat, y_ctx = [], []
    for d, rev in ((0, False), (1, True)):
        yc, sc = run(pc, d, s0, rev)
        yl, _ = run(pl_, d, sc, rev)
        y_lat.append(yl)
        y_ctx.append(yc)
    out_ctx = post(pc, y_ctx) if ctx_out else None
    return post(pl_, y_lat), out_ctx


def _gdn_chunked(q, k, v, beta, g, s0):
    bsz, t, h, dk = q.shape
    dv = v.shape[-1]
    n = t // GDN_CHUNK
    ch = lambda z: jnp.moveaxis(z.reshape(bsz, n, GDN_CHUNK, h, *z.shape[3:]), 3, 2)
    q, k, v, beta, g = ch(q), ch(k), ch(v), ch(beta), ch(g)
    gc = jnp.cumsum(g, axis=-1)
    i = jnp.arange(GDN_CHUNK)
    incl = i[:, None] >= i[None, :]
    strict = i[:, None] > i[None, :]
    decay = jnp.exp(jnp.where(incl, gc[..., :, None] - gc[..., None, :], -jnp.inf))
    kb = k * beta[..., None]
    a_low = jnp.where(strict, jnp.einsum('bnhid,bnhjd->bnhij', kb, k) * decay, 0.0)
    tmat = a_low + jnp.eye(GDN_CHUNK, dtype=F32)
    rhs = jnp.concatenate([v * beta[..., None], kb * jnp.exp(gc)[..., None]], -1)
    sol = lax.linalg.triangular_solve(tmat, rhs, left_side=True, lower=True, unit_diagonal=True)
    u, w = sol[..., :dv], sol[..., dv:]
    qk = jnp.where(incl, jnp.einsum('bnhid,bnhjd->bnhij', q, k) * decay, 0.0)
    qg = q * jnp.exp(gc)[..., None]
    kg = k * jnp.exp(gc[..., -1:] - gc)[..., None]
    glast = jnp.exp(gc[..., -1])

    def step(s, xs):
        qg_i, kg_i, u_i, w_i, qk_i, gl_i = xs
        v_new = u_i - jnp.einsum('bhcd,bhdv->bhcv', w_i, s)
        o = jnp.einsum('bhcd,bhdv->bhcv', qg_i, s) + jnp.einsum('bhij,bhjv->bhiv', qk_i, v_new)
        s = s * gl_i[..., None, None] + jnp.einsum('bhcd,bhcv->bhdv', kg_i, v_new)
        return s, o

    xs = tuple(jnp.moveaxis(z, 1, 0) for z in (qg, kg, u, w, qk, glast))
    s, o = lax.scan(step, s0, xs)
    o = jnp.moveaxis(jnp.moveaxis(o, 0, 1), 2, 3).reshape(bsz, t, h, dv)
    return o, s


def _gated_deltanet(f_lat, f_ctx, conv_w, a_log, dt_bias, norm_g, ctx_out):
    def prep(f):
        qkv, gate, ab = _split(f, [3 * GW, GW, 4 * GROUP_HEADS])
        qkv = jax.nn.silu(_dwconv_centred(qkv, conv_w).astype(F32))
        q, k, v = [_heads(z) for z in _split(qkv, [GW, GW, GW])]
        q = _l2norm(q) * HEAD_DIM ** -0.5
        k = _l2norm(k)
        ab = ab.astype(F32).reshape(f.shape[0], f.shape[1], 2, 2, GROUP_HEADS)
        log_alpha = -jnp.exp(a_log) * jax.nn.softplus(ab[:, :, 0] + dt_bias)
        beta = jax.nn.sigmoid(ab[:, :, 1])
        return q, k, v, gate, log_alpha, beta

    def run(p, d, s0):
        tr = (lambda z: jnp.flip(z, 1)) if d == 1 else (lambda z: z)
        q, k, v, gate, log_alpha, beta = p
        o, s = _gdn_chunked(tr(q), tr(k), tr(v), tr(beta[:, :, d]), tr(log_alpha[:, :, d]), s0)
        return tr(o), s

    def post(p, os_):
        gate = p[3]
        o = os_[0] + os_[1]
        y = _rms_norm(o, norm_g) * jax.nn.silu(_heads(gate.astype(F32)))
        return y.reshape(o.shape[0], o.shape[1], GW)

    pl_, pc = prep(f_lat), prep(f_ctx)
    s0 = jnp.zeros((f_lat.shape[0], GROUP_HEADS, HEAD_DIM, HEAD_DIM), F32)
    o_lat, o_ctx = [], []
    for d in range(2):
        oc, sc = run(pc, d, s0)
        ol, _ = run(pl_, d, sc)
        o_lat.append(ol)
        o_ctx.append(oc)
    out_ctx = post(pc, o_ctx) if ctx_out else None
    return post(pl_, o_lat), out_ctx


def _softmax_attn(q, k, v):
    s = jnp.einsum('bqhd,bkhd->bhqk', q.astype(F32), k.astype(F32)) * HEAD_DIM ** -0.5
    return jnp.einsum('bhqk,bkhd->bqhd', jax.nn.softmax(s, -1), v.astype(F32))


def _neighbourhood_attention(q, k, v, qc, kc, vc, rpb, ctx_out):
    bsz, lat_len = q.shape[:2]
    rows = lat_len // GRID_W
    kh = min(WIN_H, rows)
    scale = HEAD_DIM ** -0.5
    grid = lambda z: z.astype(F32).reshape(bsz, rows, GRID_W, GROUP_HEADS, HEAD_DIM)
    qg, kg, vg = grid(q), grid(k), grid(v)
    kc32, vc32 = kc.astype(F32), vc.astype(F32)
    rpb = rpb.astype(F32)
    cq = jnp.arange(GRID_W)
    c_start = jnp.clip(cq - WIN_W // 2, 0, GRID_W - WIN_W)
    col_ok = (cq[None, :] >= c_start[:, None]) & (cq[None, :] < c_start[:, None] + WIN_W)
    d_col = jnp.clip(cq[None, :] - cq[:, None], -(WIN_W - 1), WIN_W - 1) + (WIN_W - 1)

    def row_block(r):
        rs = jnp.clip(r - kh // 2, 0, rows - kh)
        q_r = lax.dynamic_index_in_dim(qg, r, axis=1, keepdims=False)
        k_r = lax.dynamic_slice_in_dim(kg, rs, kh, axis=1)
        v_r = lax.dynamic_slice_in_dim(vg, rs, kh, axis=1)
        s = jnp.einsum('bqhd,bkwhd->bqhkw', q_r, k_r) * scale
        d_row = rs + jnp.arange(kh) - r + (WIN_H - 1)
        bias = rpb[:, d_row[None, :, None], d_col[:, None, :]]
        s = jnp.where(col_ok[:, None, None, :], s + jnp.transpose(bias, (1, 0, 2, 3)), -jnp.inf)
        s_c = jnp.einsum('bqhd,bchd->bqhc', q_r, kc32) * scale
        p = jax.nn.softmax(jnp.concatenate(
            [s.reshape(bsz, GRID_W, GROUP_HEADS, kh * GRID_W), s_c], -1), -1)
        p_win = p[..., :kh * GRID_W].reshape(bsz, GRID_W, GROUP_HEADS, kh, GRID_W)
        return (jnp.einsum('bqhkw,bkwhd->bqhd', p_win, v_r)
                + jnp.einsum('bqhc,bchd->bqhd', p[..., kh * GRID_W:], vc32))

    o = lax.map(row_block, jnp.arange(rows))
    out_lat = jnp.moveaxis(o, 0, 1).reshape(bsz, lat_len, GW)
    out_ctx = _softmax_attn(qc, kc, vc).reshape(bsz, qc.shape[1], GW) if ctx_out else None
    return out_lat, out_ctx


def _qkv_heads(p):
    return [_heads(z) for z in _split(p, [GW, GW, GW])]


PEER_STAT_ROWS = 4 * PEER_HEADS


def _topk_rows(x, k):
    rows = []
    cur = x
    for i in range(k):
        m = jnp.max(cur, axis=0, keepdims=True)
        rows.append(m)
        if i + 1 < k:
            cur = jnp.where(cur == m, NEG_BIG, cur)
    return rows


def _peer_score_kernel(x_ref, wq_ref, keys_ref, s_ref, st_ref):
    q = jnp.dot(x_ref[...], wq_ref[...], preferred_element_type=F32).astype(BF16)
    stats = []
    for h in range(PEER_HEADS):
        tops = []
        for p in range(2):
            hp = 2 * h + p
            s_t = lax.dot_general(keys_ref[hp], q[:, hp * PEER_HALF:(hp + 1) * PEER_HALF],
                                  (((1,), (1,)), ((), ())), preferred_element_type=F32)
            s_ref[hp] = s_t
            tops.append(_topk_rows(s_t, PEER_TOPK))
        top_b = jnp.concatenate(tops[1], axis=0)
        cand = jnp.concatenate([tops[0][i] + top_b for i in range(PEER_TOPK)], axis=0)
        tau = _topk_rows(cand, PEER_TOPK)[-1]
        best = tops[0][0] + tops[1][0]
        z = jnp.sum(jnp.where(cand >= tau, jnp.exp(cand - best), 0.0), axis=0, keepdims=True)
        stats += [tau, tops[0][0], tops[1][0], 1.0 / z]
    st_ref[...] = jnp.concatenate(stats, axis=0)


def _peer_expert_kernel(x_ref, s_ref, st_ref, u_ref, vt_ref, o_ref, e_ref, acc_ref, *, rows_per_step):
    j = pl.program_id(1)

    @pl.when(j == 0)
    def _():
        acc_ref[...] = jnp.zeros_like(acc_ref)
        for h in range(PEER_HEADS):
            a0 = st_ref[4 * h + 1:4 * h + 2, :]
            b0 = st_ref[4 * h + 2:4 * h + 3, :]
            rz = st_ref[4 * h + 3:4 * h + 4, :]
            e_ref[2 * h] = jnp.exp(s_ref[2 * h] - a0) * rz
            e_ref[2 * h + 1] = jnp.exp(s_ref[2 * h + 1] - b0)

    x = x_ref[...]
    for il in range(rows_per_step):
        i = j * rows_per_step + il
        act = lax.dot_general(u_ref[il * PEER_KEYS:(il + 1) * PEER_KEYS, :], x,
                              (((1,), (1,)), ((), ())), preferred_element_type=F32)
        act = 0.5 * act * (1.0 + lax.erf(act * (2.0 ** -0.5)))
        gate = jnp.zeros_like(act)
        for h in range(PEER_HEADS):
            c = s_ref[2 * h, pl.ds(i, 1), :] + s_ref[2 * h + 1]
            tau = st_ref[4 * h:4 * h + 1, :]
            gate = gate + jnp.where(c >= tau, e_ref[2 * h, pl.ds(i, 1), :] * e_ref[2 * h + 1], 0.0)
        w = (act * gate).astype(BF16)
        acc_ref[...] += jnp.dot(vt_ref[:, il * PEER_KEYS:(il + 1) * PEER_KEYS], w,
                                preferred_element_type=F32)

    @pl.when(j == pl.num_programs(1) - 1)
    def _():
        o_ref[...] = acc_ref[...].T


def _peer_pallas(h, w_q, keys, u_tab, vt_tab, *, tb, rows_per_step):
    n_tok, d = h.shape
    nhp = 2 * PEER_HEADS
    s, st = pl.pallas_call(
        _peer_score_kernel,
        name="peer_scores",
        grid=(n_tok // tb,),
        in_specs=[pl.BlockSpec((tb, d), lambda i: (i, 0)),
                  pl.BlockSpec(w_q.shape, lambda i: (0, 0)),
                  pl.BlockSpec(keys.shape, lambda i: (0, 0, 0))],
        out_specs=[pl.BlockSpec((nhp, PEER_KEYS, tb), lambda i: (0, 0, i)),
                   pl.BlockSpec((PEER_STAT_ROWS, tb), lambda i: (0, i))],
        out_shape=[jax.ShapeDtypeStruct((nhp, PEER_KEYS, n_tok), F32),
                   jax.ShapeDtypeStruct((PEER_STAT_ROWS, n_tok), F32)],
        compiler_params=pltpu.CompilerParams(dimension_semantics=("arbitrary",),
                                             vmem_limit_bytes=V7X_VMEM_LIMIT_BYTES),
    )(h, w_q, keys)
    ec = rows_per_step * PEER_KEYS
    return pl.pallas_call(
        functools.partial(_peer_expert_kernel, rows_per_step=rows_per_step),
        name="peer_experts",
        grid=(n_tok // tb, PEER_EXPERTS // ec),
        in_specs=[pl.BlockSpec((tb, d), lambda i, j: (i, 0)),
                  pl.BlockSpec((nhp, PEER_KEYS, tb), lambda i, j: (0, 0, i)),
                  pl.BlockSpec((PEER_STAT_ROWS, tb), lambda i, j: (0, i)),
                  pl.BlockSpec((ec, d), lambda i, j: (j, 0)),
                  pl.BlockSpec((d, ec), lambda i, j: (0, j))],
        out_specs=pl.BlockSpec((tb, d), lambda i, j: (i, 0)),
        out_shape=jax.ShapeDtypeStruct((n_tok, d), F32),
        scratch_shapes=[pltpu.VMEM((nhp, PEER_KEYS, tb), F32),
                        pltpu.VMEM((d, tb), F32)],
        compiler_params=pltpu.CompilerParams(dimension_semantics=("arbitrary", "arbitrary"),
                                             vmem_limit_bytes=V7X_VMEM_LIMIT_BYTES),
    )(h, s, st, u_tab, vt_tab)


def _peer(h, w_q, keys, u_tab, vt_tab):
    bsz, t, d = h.shape
    n_tok = bsz * t
    tb = 512 if n_tok % 512 == 0 else 256
    out = _peer_pallas(h.reshape(n_tok, d).astype(BF16), w_q, keys, u_tab, vt_tab, tb=tb, rows_per_step=8)
    return out.reshape(bsz, t, d)


ROW_BLOCK = 256
HIGHEST = lax.Precision.HIGHEST


def _dot_f32(a, b):
    return jnp.dot(a, b, precision=HIGHEST, preferred_element_type=F32)


def _segment_ones(n, seg, dtype):
    r = lax.broadcasted_iota(jnp.int32, (n, n), 0) // seg
    c = lax.broadcasted_iota(jnp.int32, (n, n), 1) // seg
    return jnp.where(r == c, 1.0, 0.0).astype(dtype)


def _shifted_rows(x, prev_row, next_row):
    t = x.shape[0]
    rows = lax.broadcasted_iota(jnp.int32, x.shape, 0)
    xm = jnp.where(rows == 0, prev_row, pltpu.roll(x, 1, axis=0))
    xp = jnp.where(rows == t - 1, next_row, pltpu.roll(x, t - 1, axis=0))
    return xm, xp


def _segment_edge_flags(i, n_blocks, ctx_blocks):
    is_start = jnp.logical_or(i == 0, i == ctx_blocks)
    is_end = jnp.logical_or(i == ctx_blocks - 1, i == n_blocks - 1)
    return jnp.where(is_start, 0.0, 1.0), jnp.where(is_end, 0.0, 1.0)


def _halo_specs(width, tr):
    g = tr // 8
    prev = pl.BlockSpec((1, 8, width), lambda b, i: (b, jnp.maximum(i * g - 1, 0), 0))
    nxt = lambda n_groups: pl.BlockSpec((1, 8, width), lambda b, i: (b, jnp.minimum((i + 1) * g, n_groups - 1), 0))
    return prev, nxt


def _softplus(z):
    return jnp.maximum(z, 0.0) + jnp.log1p(jnp.exp(-jnp.abs(z)))


def _rwkv_prep_kernel(x_ref, xprev_ref, xnext_ref, mu_ref, w0_ref, w2_ref, a0_ref, a2_ref, g2_ref,
                      kk_ref, ka_ref, rk_ref,
                      r_out, v_out, kkn_out, g_out, bonus_out, w_out, b_out, kt_out, *, ctx_blocks):
    i = pl.program_id(1)
    keep_prev, keep_next = _segment_edge_flags(i, pl.num_programs(1), ctx_blocks)
    x = x_ref[0]
    xm, xp = _shifted_rows(x, xprev_ref[0, 7:8, :] * keep_prev, xnext_ref[0, 0:1, :] * keep_next)
    mu0 = mu_ref[0:1, :]
    mu1 = mu_ref[1:2, :]
    f = xm * mu0 + x * (1.0 - mu0 - mu1) + xp * mu1
    r = f[:, 0:GW]
    k = f[:, GW:2 * GW]
    v = f[:, 2 * GW:3 * GW]
    o = 3 * GW
    wd = jnp.tanh(f[:, o:o + 2 * DECAY_LORA])
    ad = f[:, o + 2 * DECAY_LORA:o + 2 * DECAY_LORA + 2 * ICLR_LORA]
    gd = f[:, o + 2 * DECAY_LORA + 2 * ICLR_LORA:]
    w_raw = w0_ref[...] + _dot_f32(wd, w2_ref[...])
    decay = jnp.exp(-jnp.exp(-_softplus(-w_raw) - 0.5))
    a = jax.nn.sigmoid(a0_ref[...] + _dot_f32(ad, a2_ref[...]))
    g = _dot_f32(jax.nn.sigmoid(gd), g2_ref[...])
    head_sum = _segment_ones(GW, HEAD_DIM, F32)
    kx = k * kk_ref[...]
    kkn = kx * lax.rsqrt(_dot_f32(kx * kx, head_sum) + 1e-6)
    kd_sum = jnp.zeros_like(k)
    for d in range(2):
        a_d = a[:, d * GW:(d + 1) * GW]
        kd = k * (1.0 + (a_d - 1.0) * ka_ref[...])
        kd_sum = kd_sum + kd
        w_out[d, 0] = decay[:, d * GW:(d + 1) * GW]
        b_out[d, 0] = kkn * a_d
        kt_out[d, 0] = kd
    r_out[0] = r
    v_out[0] = v
    kkn_out[0] = kkn
    g_out[0] = g
    bonus_out[0] = _dot_f32(r * kd_sum * rk_ref[...], head_sum) * v


def _rwkv_scan_kernel(rf_ref, vf_ref, kkf_ref, wf_ref, bf_ref, ktf_ref,
                      rb_ref, vb_ref, kkb_ref, wb_ref, bb_ref, ktb_ref,
                      yf_ref, yb_ref, st_ref):
    c = pl.program_id(0)
    bsz, tt, _ = rf_ref.shape
    n_pairs = GW // 128

    @pl.when(c == 0)
    def _():
        st_ref[...] = jnp.zeros_like(st_ref)

    ones_bd = _segment_ones(128, HEAD_DIM, BF16)
    rows = lax.broadcasted_iota(jnp.int32, (HEAD_DIM, 128), 0)
    lanes = lax.broadcasted_iota(jnp.int32, (HEAD_DIM, 128), 1)
    diag = jnp.where(lanes % HEAD_DIM == rows, 1.0, 0.0)
    dirs = ((rf_ref, vf_ref, kkf_ref, wf_ref, bf_ref, ktf_ref, yf_ref),
            (rb_ref, vb_ref, kkb_ref, wb_ref, bb_ref, ktb_ref, yb_ref))

    n_groups = tt // 8

    def group(tg, carry):
        chains = [(b, p) for b in range(bsz) for p in range(n_pairs)]
        nc = len(chains)
        base = [pl.multiple_of((tg if d == 0 else n_groups - 1 - tg) * 8, 8) for d in range(2)]
        tiles = [[], []]
        states = [[], []]
        for d in range(2):
            r_ref, v_ref, kk_ref, w_ref, b_ref, kt_ref, _ = dirs[d]
            for b, p in chains:
                cols = slice(p * 128, (p + 1) * 128)
                tiles[d].append(tuple(ref[b, pl.ds(base[d], 8), cols] for ref in (r_ref, v_ref, kk_ref))
                                + tuple(ref[0, b, pl.ds(base[d], 8), cols] for ref in (w_ref, b_ref, kt_ref)))
                states[d].append(st_ref[d, b * n_pairs + p])
        ys = [[[None] * 8 for _ in chains] for _ in range(2)]
        pending = [None, None]

        def emit_outputs(d, yb, jj):
            for ci in range(nc):
                ys[d][ci][jj] = jnp.sum(yb[ci * HEAD_DIM:(ci + 1) * HEAD_DIM] * diag, axis=0, keepdims=True)

        for j in range(8):
            for d in range(2):
                jj = j if d == 0 else 7 - j
                row = lambda a: a[jj:jj + 1, :]
                parts = ([s * row(t[2]) for s, t in zip(states[d], tiles[d])]
                         + [diag * row(t[1]) for t in tiles[d]])
                if pending[d] is not None:
                    parts += pending[d][0]
                res = jnp.dot(jnp.concatenate(parts, axis=0).astype(BF16), ones_bd, preferred_element_type=F32)
                for ci, t in enumerate(tiles[d]):
                    sa = res[ci * HEAD_DIM:(ci + 1) * HEAD_DIM]
                    vcol = res[(nc + ci) * HEAD_DIM:(nc + ci + 1) * HEAD_DIM]
                    states[d][ci] = states[d][ci] * row(t[3]) - sa * row(t[4]) + vcol * row(t[5])
                if pending[d] is not None:
                    emit_outputs(d, res[2 * nc * HEAD_DIM:], pending[d][1])
                pending[d] = ([s * row(t[0]) for s, t in zip(states[d], tiles[d])], jj)
        for d in range(2):
            q = jnp.concatenate(pending[d][0], axis=0).astype(BF16)
            emit_outputs(d, jnp.dot(q, ones_bd, preferred_element_type=F32), pending[d][1])
            y_ref = dirs[d][6]
            for ci, (b, p) in enumerate(chains):
                st_ref[d, b * n_pairs + p] = states[d][ci]
                y_ref[b, pl.ds(base[d], 8), p * 128:(p + 1) * 128] = jnp.concatenate(ys[d][ci], axis=0)
        return carry

    lax.fori_loop(0, n_groups, group, 0)


def _rwkv_post_kernel(yf_ref, yb_ref, bonus_ref, g_ref, lng_ref, lnb_ref, o_ref):
    y = yf_ref[0] + yb_ref[0]
    head_mean = _segment_ones(GW, HEAD_DIM, F32) * (1.0 / HEAD_DIM)
    m = _dot_f32(y, head_mean)
    yc = y - m
    var = _dot_f32(yc * yc, head_mean)
    yn = yc * lax.rsqrt(var + RWKV_GN_EPS) * lng_ref[...] + lnb_ref[...]
    o_ref[0] = (yn + bonus_ref[0]) * g_ref[0]


def _block_diag2(m):
    z = jnp.zeros_like(m[0])
    return jnp.concatenate([jnp.concatenate([m[0], z], 1), jnp.concatenate([z, m[1]], 1)], 0)


def _rwkv7_pallas(f, mu, w0, w2, a0, a2, g2, k_k, k_a, r_k, ln_g, ln_b, *, ctx_len):
    bsz, seq, cols = f.shape
    tr = ROW_BLOCK
    assert ctx_len % tr == 0 and seq % tr == 0
    nb, ctx_blocks = seq // tr, ctx_len // tr
    prev_spec, next_spec = _halo_specs(cols, tr)
    row2 = lambda a: a.reshape(1, -1).astype(F32)
    full = lambda a: pl.BlockSpec(a.shape, lambda b, i: (0,) * a.ndim)
    params = [mu, row2(w0), _block_diag2(w2), row2(a0), _block_diag2(a2), g2, row2(k_k), row2(k_a), row2(r_k)]
    act = jax.ShapeDtypeStruct((bsz, seq, GW), F32)
    act2 = jax.ShapeDtypeStruct((2, bsz, seq, GW), F32)
    blk = pl.BlockSpec((1, tr, GW), lambda b, i: (b, i, 0))
    blk2 = pl.BlockSpec((2, 1, tr, GW), lambda b, i: (0, b, i, 0))
    r, v, kkn, g, bonus, w, bb, kt = pl.pallas_call(
        functools.partial(_rwkv_prep_kernel, ctx_blocks=ctx_blocks),
        name="rwkv_prep",
        grid=(bsz, nb),
        in_specs=[pl.BlockSpec((1, tr, cols), lambda b, i: (b, i, 0)), prev_spec, next_spec(seq // 8)]
                 + [full(p) for p in params],
        out_specs=[blk] * 5 + [blk2] * 3,
        out_shape=[act] * 5 + [act2] * 3,
        compiler_params=pltpu.CompilerParams(dimension_semantics=("arbitrary", "arbitrary"),
                                             vmem_limit_bytes=V7X_VMEM_LIMIT_BYTES),
    )(f, f, f, *params)

    def bwd_block(c):
        return jnp.where(c < ctx_blocks, ctx_blocks - 1 - c, nb - 1 - (c - ctx_blocks))
    fwd = pl.BlockSpec((bsz, tr, GW), lambda c: (0, c, 0))
    bwd = pl.BlockSpec((bsz, tr, GW), lambda c: (0, bwd_block(c), 0))
    fwd_d = pl.BlockSpec((1, bsz, tr, GW), lambda c: (0, 0, c, 0))
    bwd_d = pl.BlockSpec((1, bsz, tr, GW), lambda c: (1, 0, bwd_block(c), 0))
    yf, yb = pl.pallas_call(
        _rwkv_scan_kernel,
        name="rwkv_scan",
        grid=(nb,),
        in_specs=[fwd, fwd, fwd, fwd_d, fwd_d, fwd_d, bwd, bwd, bwd, bwd_d, bwd_d, bwd_d],
        out_specs=[fwd, bwd],
        out_shape=[act, act],
        scratch_shapes=[pltpu.VMEM((2, bsz * (GW // 128), HEAD_DIM, 128), F32)],
        compiler_params=pltpu.CompilerParams(dimension_semantics=("arbitrary",),
                                             vmem_limit_bytes=V7X_VMEM_LIMIT_BYTES),
    )(r, v, kkn, w, bb, kt, r, v, kkn, w, bb, kt)

    return pl.pallas_call(
        _rwkv_post_kernel,
        name="rwkv_post",
        grid=(bsz, nb),
        in_specs=[blk, blk, blk, blk, full(row2(ln_g)), full(row2(ln_b))],
        out_specs=blk,
        out_shape=act,
        compiler_params=pltpu.CompilerParams(dimension_semantics=("arbitrary", "arbitrary")),
    )(yf, yb, bonus, g, row2(ln_g), row2(ln_b))


GDN_GATE_LANES = 128


def _gdn_prep_kernel(x_ref, xprev_ref, xnext_ref, ab_ref, conv_ref, alog_ref, dtb_ref,
                     q_out, k_out, v_out, gb_out, *, ctx_blocks):
    i = pl.program_id(1)
    keep_prev, keep_next = _segment_edge_flags(i, pl.num_programs(1), ctx_blocks)
    x = x_ref[0]
    xm, xp = _shifted_rows(x, xprev_ref[0, 7:8, :] * keep_prev, xnext_ref[0, 0:1, :] * keep_next)
    y = xm * conv_ref[0:1, :] + x * conv_ref[1:2, :] + xp * conv_ref[2:3, :]
    y = y * jax.nn.sigmoid(y)
    head_sum = _segment_ones(GW, HEAD_DIM, F32)
    q = y[:, 0:GW]
    k = y[:, GW:2 * GW]
    q_out[0] = q * lax.rsqrt(_dot_f32(q * q, head_sum) + 1e-6) * (HEAD_DIM ** -0.5)
    k_out[0] = k * lax.rsqrt(_dot_f32(k * k, head_sum) + 1e-6)
    v_out[0] = y[:, 2 * GW:3 * GW]
    ab = ab_ref[0]
    lane = lax.broadcasted_iota(jnp.int32, ab.shape, 1)
    log_alpha = -jnp.exp(alog_ref[...]) * _softplus(ab + dtb_ref[...])
    gb_out[0] = jnp.where(lane < 2 * GROUP_HEADS, log_alpha, jax.nn.sigmoid(ab))


def _gdn_chunk_kernel(qf_ref, kf_ref, vf_ref, gf_ref, qb_ref, kb_ref, vb_ref, gb_ref, of_ref, ob_ref, st_ref):
    i = pl.program_id(1)
    c = GDN_CHUNK
    n_chunks = qf_ref.shape[1] // c

    @pl.when(i == 0)
    def _():
        st_ref[...] = jnp.zeros_like(st_ref)

    r = lax.broadcasted_iota(jnp.int32, (c, c), 0)
    s = lax.broadcasted_iota(jnp.int32, (c, c), 1)
    eye = r == s
    ones_cc = jnp.ones((c, c), F32)
    incl = (r >= s, r <= s)
    strict = (r > s, r < s)
    levels = []
    b = 1
    while b < c:
        levels.append(jnp.logical_and(r // (2 * b) == s // (2 * b), r // b != s // b))
        b *= 2
    dirs = ((qf_ref, kf_ref, vf_ref, gf_ref, of_ref), (qb_ref, kb_ref, vb_ref, gb_ref, ob_ref))

    bdot = lambda x, y: jnp.dot(x.astype(BF16), y.astype(BF16), preferred_element_type=F32)
    bdot_nt = lambda x, y: lax.dot_general(x.astype(BF16), y.astype(BF16), _NT, preferred_element_type=F32)
    bdot_tn = lambda x, y: lax.dot_general(x.astype(BF16), y.astype(BF16), (((0,), (0,)), ((), ())),
                                           preferred_element_type=F32)

    def chunk(cc, carry):
        chains = [(d, h) for d in range(2) for h in range(GROUP_HEADS)]
        row0 = [pl.multiple_of((cc if d == 0 else n_chunks - 1 - cc) * c, c) for d in range(2)]
        gates = [dirs[d][3][0, pl.ds(row0[d], c), :] for d in range(2)]
        cum = [_dot_f32(jnp.where(incl[d], 1.0, 0.0), gates[d]) for d in range(2)]
        tot = [_dot_f32(ones_cc, gates[d]) for d in range(2)]
        st = []
        for d, h in chains:
            cols = slice(h * HEAD_DIM, (h + 1) * HEAD_DIM)
            lg = d * GROUP_HEADS + h
            q, k, v = (dirs[d][n][0, pl.ds(row0[d], c), cols] for n in range(3))
            gc = cum[d][:, lg:lg + 1]
            st.append(dict(q=q, k=k, v=v, gc=gc, gt=tot[d][:, lg:lg + 1],
                           beta=gates[d][:, 2 * GROUP_HEADS + lg:2 * GROUP_HEADS + lg + 1],
                           gc_row=_dot_f32(ones_cc, jnp.where(eye, gc, 0.0))))
        for (d, h), x in zip(chains, st):
            x["decay"] = jnp.exp(jnp.where(incl[d], x["gc"] - x["gc_row"], NEG_BIG))
            x["kb"] = x["k"] * x["beta"]
            x["a"] = jnp.where(strict[d], bdot_nt(x["kb"], x["k"]) * x["decay"], 0.0)
            x["qk"] = jnp.where(incl[d], bdot_nt(x["q"], x["k"]) * x["decay"], 0.0)
            x["inv"] = jnp.where(eye, 1.0, 0.0)
        for lvl in levels:
            for x in st:
                x["t"] = _dot_f32(jnp.where(lvl, x["a"], 0.0), x["inv"])
            for x in st:
                x["inv"] = x["inv"] - _dot_f32(x["inv"], x["t"])
        for x in st:
            x["eg"] = jnp.exp(x["gc"])
            x["sol"] = _dot_f32(x["inv"], jnp.concatenate([x["v"] * x["beta"], x["kb"] * x["eg"]], axis=-1))
        for (d, h), x in zip(chains, st):
            x["state"] = st_ref[d, h]
            x["ws"] = bdot(x["sol"][:, HEAD_DIM:], x["state"])
            x["qs"] = bdot(x["q"] * x["eg"], x["state"])
        for x in st:
            x["v_new"] = x["sol"][:, :HEAD_DIM] - x["ws"]
            x["o"] = x["qs"] + bdot(x["qk"], x["v_new"])
            x["upd"] = bdot_tn(x["k"] * jnp.exp(x["gt"] - x["gc"]), x["v_new"])
        for (d, h), x in zip(chains, st):
            st_ref[d, h] = x["state"] * jnp.exp(x["gt"][0:1, :]) + x["upd"]
        for d in range(2):
            dirs[d][4][0, pl.ds(row0[d], c), :] = jnp.concatenate(
                [x["o"] for (dd, h), x in zip(chains, st) if dd == d], axis=-1)
        return carry

    lax.fori_loop(0, n_chunks, chunk, 0)


def _gdn_post_kernel(of_ref, ob_ref, gate_ref, g_ref, o_ref):
    o = of_ref[0] + ob_ref[0]
    head_mean = _segment_ones(GW, HEAD_DIM, F32) * (1.0 / HEAD_DIM)
    gate = gate_ref[0]
    o_ref[0] = o * lax.rsqrt(_dot_f32(o * o, head_mean) + 1e-6) * g_ref[...] * (gate * jax.nn.sigmoid(gate))


def _gated_deltanet_pallas(f, conv_w, a_log, dt_bias, norm_g, *, ctx_len):
    bsz, seq, _ = f.shape
    tr = ROW_BLOCK
    assert ctx_len % tr == 0 and seq % tr == 0 and tr % GDN_CHUNK == 0
    nb, ctx_blocks = seq // tr, ctx_len // tr
    qkv = f[:, :, :3 * GW]
    gate = f[:, :, 3 * GW:4 * GW]
    n_ab = 4 * GROUP_HEADS
    ab = jnp.pad(f[:, :, 4 * GW:], ((0, 0), (0, 0), (0, GDN_GATE_LANES - n_ab)))
    lane_pad = lambda a: jnp.pad(a.reshape(1, -1).astype(F32), ((0, 0), (0, GDN_GATE_LANES - a.size)))
    prev_spec, next_spec = _halo_specs(3 * GW, tr)
    full = lambda a: pl.BlockSpec(a.shape, lambda b, i: (0,) * a.ndim)
    act = jax.ShapeDtypeStruct((bsz, seq, GW), F32)
    gact = jax.ShapeDtypeStruct((bsz, seq, GDN_GATE_LANES), F32)
    blk = pl.BlockSpec((1, tr, GW), lambda b, i: (b, i, 0))
    gblk = pl.BlockSpec((1, tr, GDN_GATE_LANES), lambda b, i: (b, i, 0))
    params = [conv_w.astype(F32), lane_pad(a_log), lane_pad(dt_bias)]
    q, k, v, gb = pl.pallas_call(
        functools.partial(_gdn_prep_kernel, ctx_blocks=ctx_blocks),
        name="gdn_prep",
        grid=(bsz, nb),
        in_specs=[pl.BlockSpec((1, tr, 3 * GW), lambda b, i: (b, i, 0)), prev_spec, next_spec(seq // 8), gblk]
                 + [full(p) for p in params],
        out_specs=[blk, blk, blk, gblk],
        out_shape=[act, act, act, gact],
        compiler_params=pltpu.CompilerParams(dimension_semantics=("arbitrary", "arbitrary"),
                                             vmem_limit_bytes=V7X_VMEM_LIMIT_BYTES),
    )(qkv, qkv, qkv, ab, *params)

    def bwd_block(i):
        return jnp.where(i < ctx_blocks, ctx_blocks - 1 - i, nb - 1 - (i - ctx_blocks))
    bblk = pl.BlockSpec((1, tr, GW), lambda b, i: (b, bwd_block(i), 0))
    bgblk = pl.BlockSpec((1, tr, GDN_GATE_LANES), lambda b, i: (b, bwd_block(i), 0))
    of, ob = pl.pallas_call(
        _gdn_chunk_kernel,
        name="gdn_chunks",
        grid=(bsz, nb),
        in_specs=[blk, blk, blk, gblk, bblk, bblk, bblk, bgblk],
        out_specs=[blk, bblk],
        out_shape=[act, act],
        scratch_shapes=[pltpu.VMEM((2, GROUP_HEADS, HEAD_DIM, HEAD_DIM), F32)],
        compiler_params=pltpu.CompilerParams(dimension_semantics=("arbitrary", "arbitrary"),
                                             vmem_limit_bytes=V7X_VMEM_LIMIT_BYTES),
    )(q, k, v, gb, q, k, v, gb)

    g_row = jnp.tile(norm_g.reshape(1, HEAD_DIM).astype(F32), (1, GROUP_HEADS))
    return pl.pallas_call(
        _gdn_post_kernel,
        name="gdn_post",
        grid=(bsz, nb),
        in_specs=[blk, blk, blk, full(g_row)],
        out_specs=blk,
        out_shape=act,
        compiler_params=pltpu.CompilerParams(dimension_semantics=("arbitrary", "arbitrary")),
    )(of, ob, gate, g_row)


ROPE_PAIR = DIFF_HALF // 4


def _rope_tables(seq, ctx_len, q_scale):
    n = jnp.arange(seq - ctx_len, dtype=jnp.int32)
    row, col = n // GRID_W, n % GRID_W
    i = jnp.arange(HEAD_DIM)
    grp = (i % DIFF_HALF) // (2 * ROPE_PAIR)
    inv = ROPE_BASE ** (-(i % ROPE_PAIR).astype(F32) / ROPE_PAIR)
    pos = jnp.where(grp[None, :] == 0, row[:, None], col[:, None]).astype(F32)
    ang = pos * inv[None, :]
    sign = jnp.where((i % (2 * ROPE_PAIR)) < ROPE_PAIR, -1.0, 1.0)
    cos = jnp.concatenate([jnp.ones((ctx_len, HEAD_DIM), F32), jnp.cos(ang)], 0)
    sin = jnp.concatenate([jnp.zeros((ctx_len, HEAD_DIM), F32), jnp.sin(ang) * sign], 0)
    cos = jnp.tile(cos, (1, GROUP_HEADS))
    sin = jnp.tile(sin, (1, GROUP_HEADS))
    return jnp.concatenate([cos * q_scale, cos], 1), jnp.concatenate([sin * q_scale, sin], 1)


def _qkv_prep_kernel(p_ref, cos_ref, sin_ref, q_out, k_out, v_out):
    qk = p_ref[0, :, 0:2 * GW]
    width = 2 * GW
    lane = lax.broadcasted_iota(jnp.int32, qk.shape, 1)
    partner = jnp.where(lane % (2 * ROPE_PAIR) < ROPE_PAIR,
                        pltpu.roll(qk, width - ROPE_PAIR, axis=1), pltpu.roll(qk, ROPE_PAIR, axis=1))
    rot = qk * cos_ref[...] + partner * sin_ref[...]
    q_out[0] = rot[:, 0:GW].astype(BF16)
    v = p_ref[0, :, 2 * GW:3 * GW]
    for h in range(GROUP_HEADS):
        k_out[0, h] = rot[:, GW + h * HEAD_DIM:GW + (h + 1) * HEAD_DIM].astype(BF16)
        v_out[0, h] = v[:, h * HEAD_DIM:(h + 1) * HEAD_DIM].astype(BF16)


def _qkv_prep(p, cos, sin):
    bsz, seq, _ = p.shape
    tr = ROW_BLOCK
    head_major = jax.ShapeDtypeStruct((bsz, GROUP_HEADS, seq, HEAD_DIM), BF16)
    hm_spec = pl.BlockSpec((1, GROUP_HEADS, tr, HEAD_DIM), lambda b, i: (b, 0, i, 0))
    return pl.pallas_call(
        _qkv_prep_kernel,
        name="qkv_prep",
        grid=(bsz, seq // tr),
        in_specs=[pl.BlockSpec((1, tr, 3 * GW), lambda b, i: (b, i, 0)),
                  pl.BlockSpec((tr, 2 * GW), lambda b, i: (i, 0)),
                  pl.BlockSpec((tr, 2 * GW), lambda b, i: (i, 0))],
        out_specs=[pl.BlockSpec((1, tr, GW), lambda b, i: (b, i, 0)), hm_spec, hm_spec],
        out_shape=[jax.ShapeDtypeStruct((bsz, seq, GW), BF16), head_major, head_major],
        compiler_params=pltpu.CompilerParams(dimension_semantics=("arbitrary", "arbitrary")),
    )(p, cos, sin)


_NT = (((1,), (1,)), ((), ()))


def _softmax_pv(s, v):
    m = jnp.max(s, axis=-1, keepdims=True)
    e = jnp.exp(s - m)
    return jnp.dot(e.astype(BF16), v, preferred_element_type=F32) / jnp.sum(e, axis=-1, keepdims=True)


def _diff_attn_kernel(q_ref, k_ref, v_ref, lam_ref, g_ref, o_ref, *, ctx_blocks, ctx_len):
    i = pl.program_id(1)
    lv = lam_ref[...]
    lam_init = lv[4:5, 0:1]
    lam = (jnp.exp(jnp.sum(lv[0:1] * lv[1:2], axis=-1, keepdims=True))
           - jnp.exp(jnp.sum(lv[2:3] * lv[3:4], axis=-1, keepdims=True)) + lam_init)
    lane = lax.broadcasted_iota(jnp.int32, (q_ref.shape[1], HEAD_DIM), 1)

    def attend(n_keys):
        outs = []
        for h in range(GROUP_HEADS):
            qh = q_ref[0, :, h * HEAD_DIM:(h + 1) * HEAD_DIM]
            kh = k_ref[0, h, 0:n_keys, :]
            vh = v_ref[0, h, 0:n_keys, :]
            zero = jnp.zeros_like(qh)
            s1 = lax.dot_general(jnp.where(lane < DIFF_HALF, qh, zero), kh, _NT, preferred_element_type=F32)
            s2 = lax.dot_general(jnp.where(lane >= DIFF_HALF, qh, zero), kh, _NT, preferred_element_type=F32)
            o = _softmax_pv(s1, vh) - lam * _softmax_pv(s2, vh)
            o = o * lax.rsqrt(jnp.mean(o * o, axis=-1, keepdims=True) + 1e-6) * g_ref[...] * (1.0 - lam_init)
            outs.append(o)
        o_ref[0] = jnp.concatenate(outs, axis=-1)

    @pl.when(i < ctx_blocks)
    def _():
        attend(ctx_len)

    @pl.when(i >= ctx_blocks)
    def _():
        attend(k_ref.shape[2])


def _diff_attention_pallas(q, k, v, lam_vecs, norm_g, *, ctx_len, lam_init):
    bsz, seq, _ = q.shape
    tq = ROW_BLOCK
    kv_spec = pl.BlockSpec((1, GROUP_HEADS, seq, HEAD_DIM), lambda b, i: (b, 0, 0, 0))
    lam_rows = jnp.concatenate([lam_vecs.astype(F32), jnp.full((1, lam_vecs.shape[1]), lam_init, F32)], 0)
    return pl.pallas_call(
        functools.partial(_diff_attn_kernel, ctx_blocks=ctx_len // tq, ctx_len=ctx_len),
        name="diff_attn",
        grid=(bsz, seq // tq),
        in_specs=[pl.BlockSpec((1, tq, GW), lambda b, i: (b, i, 0)), kv_spec, kv_spec,
                  pl.BlockSpec(lam_rows.shape, lambda b, i: (0, 0)),
                  pl.BlockSpec((1, HEAD_DIM), lambda b, i: (0, 0))],
        out_specs=pl.BlockSpec((1, tq, GW), lambda b, i: (b, i, 0)),
        out_shape=jax.ShapeDtypeStruct((bsz, seq, GW), F32),
        compiler_params=pltpu.CompilerParams(dimension_semantics=("arbitrary", "arbitrary"),
                                             vmem_limit_bytes=V7X_VMEM_LIMIT_BYTES),
    )(q, k, v, lam_rows, norm_g.reshape(1, HEAD_DIM).astype(F32))


NAT_TILE_ROWS = ROW_BLOCK // GRID_W
NAT_SLAB_ROWS = NAT_TILE_ROWS + WIN_H - 1


def _nat_slab_start(tile, n_rows):
    return np.clip(tile * NAT_TILE_ROWS - WIN_H // 2, 0, n_rows - NAT_SLAB_ROWS)


def _nat_bias_tables(rpb, n_rows):
    n_tiles = n_rows // NAT_TILE_ROWS
    tabs = []
    for tile in (0, 1, n_tiles - 1):
        sr = _nat_slab_start(tile, n_rows)
        qi = np.arange(ROW_BLOCK)
        kj = np.arange(NAT_SLAB_ROWS * GRID_W)
        r, cq = tile * NAT_TILE_ROWS + qi // GRID_W, qi % GRID_W
        kr, ck = sr + kj // GRID_W, kj % GRID_W
        rs = np.clip(r - WIN_H // 2, 0, n_rows - WIN_H)
        c0 = np.clip(cq - WIN_W // 2, 0, GRID_W - WIN_W)
        ok = ((kr[None, :] >= rs[:, None]) & (kr[None, :] < rs[:, None] + WIN_H)
              & (ck[None, :] >= c0[:, None]) & (ck[None, :] < c0[:, None] + WIN_W))
        d_row = np.clip(kr[None, :] - r[:, None] + WIN_H - 1, 0, 2 * WIN_H - 2)
        d_col = np.clip(ck[None, :] - cq[:, None], -(WIN_W - 1), WIN_W - 1) + WIN_W - 1
        tabs.append(jnp.where(ok[None], rpb.astype(F32)[:, d_row, d_col], NEG_BIG))
    return jnp.stack(tabs)


def _nat_attn_kernel(q_ref, k_ref, v_ref, bias_ref, o_ref, *, ctx_blocks, ctx_len, n_rows):
    i = pl.program_id(1)
    n_slab = NAT_SLAB_ROWS * GRID_W

    def heads(fn):
        o_ref[0] = jnp.concatenate(
            [fn(h, q_ref[0, :, h * HEAD_DIM:(h + 1) * HEAD_DIM]) for h in range(GROUP_HEADS)], axis=-1)

    @pl.when(i < ctx_blocks)
    def _():
        def ctx_only(h, qh):
            s = lax.dot_general(qh, k_ref[0, h, 0:ctx_len, :], _NT, preferred_element_type=F32)
            return _softmax_pv(s, v_ref[0, h, 0:ctx_len, :])
        heads(ctx_only)

    @pl.when(i >= ctx_blocks)
    def _():
        tile = i - ctx_blocks
        start = jnp.clip(tile * NAT_TILE_ROWS - WIN_H // 2, 0, n_rows - NAT_SLAB_ROWS)
        off = pl.multiple_of(ctx_len + start * GRID_W, GRID_W)

        def windowed(h, qh):
            s_w = lax.dot_general(qh, k_ref[0, h, pl.ds(off, n_slab), :], _NT,
                                  preferred_element_type=F32) + bias_ref[0, h]
            s_c = lax.dot_general(qh, k_ref[0, h, 0:ctx_len, :], _NT, preferred_element_type=F32)
            m = jnp.maximum(jnp.max(s_w, axis=-1, keepdims=True), jnp.max(s_c, axis=-1, keepdims=True))
            e_w = jnp.exp(s_w - m)
            e_c = jnp.exp(s_c - m)
            den = jnp.sum(e_w, axis=-1, keepdims=True) + jnp.sum(e_c, axis=-1, keepdims=True)
            num = (jnp.dot(e_w.astype(BF16), v_ref[0, h, pl.ds(off, n_slab), :], preferred_element_type=F32)
                   + jnp.dot(e_c.astype(BF16), v_ref[0, h, 0:ctx_len, :], preferred_element_type=F32))
            return num / den
        heads(windowed)


def _nat_attention_pallas(q, k, v, rpb, *, ctx_len):
    bsz, seq, _ = q.shape
    tq = ROW_BLOCK
    ctx_blocks = ctx_len // tq
    n_rows = (seq - ctx_len) // GRID_W
    n_tiles = n_rows // NAT_TILE_ROWS
    assert n_rows >= NAT_SLAB_ROWS and n_tiles >= 3
    bias = _nat_bias_tables(rpb, n_rows)

    def variant(i):
        tile = i - ctx_blocks
        return jnp.where(tile <= 0, 0, jnp.where(tile >= n_tiles - 1, 2, 1))
    kv_spec = pl.BlockSpec((1, GROUP_HEADS, seq, HEAD_DIM), lambda b, i: (b, 0, 0, 0))
    return pl.pallas_call(
        functools.partial(_nat_attn_kernel, ctx_blocks=ctx_blocks, ctx_len=ctx_len, n_rows=n_rows),
        name="nat_attn",
        grid=(bsz, seq // tq),
        in_specs=[pl.BlockSpec((1, tq, GW), lambda b, i: (b, i, 0)), kv_spec, kv_spec,
                  pl.BlockSpec((1,) + bias.shape[1:], lambda b, i: (variant(i), 0, 0, 0))],
        out_specs=pl.BlockSpec((1, tq, GW), lambda b, i: (b, i, 0)),
        out_shape=jax.ShapeDtypeStruct((bsz, seq, GW), F32),
        compiler_params=pltpu.CompilerParams(dimension_semantics=("arbitrary", "arbitrary"),
                                             vmem_limit_bytes=V7X_VMEM_LIMIT_BYTES),
    )(q, k, v, bias)


def kernel(x, c, ctx, c_ctx, w_ada, b_ada, w_in, w_out, ln_mix_g, ln_mix_b, ln_ffn_g, ln_ffn_b, diff_lam, diff_norm_g, rwkv_mu, rwkv_w0, rwkv_w2, rwkv_a0, rwkv_a2, rwkv_g2, rwkv_kk, rwkv_ka, rwkv_rk, rwkv_ln_g, rwkv_ln_b, gdn_conv, gdn_a_log, gdn_dt_bias, gdn_norm_g, nat_rpb, peer_wq, peer_keys, peer_u, peer_v):
    dtype = x.dtype
    bsz, ctx_len = ctx.shape[0], ctx.shape[1]
    hs = jnp.concatenate([ctx, x], axis=1)
    seq = hs.shape[1]
    col_sizes = [ATTN_COLS, RWKV_COLS, GDN_COLS, ATTN_COLS]
    cos_a, sin_a = _rope_tables(seq, ctx_len, DIFF_HALF ** -0.5)
    cos_d = jnp.concatenate([jnp.full((seq, GW), HEAD_DIM ** -0.5, F32), jnp.ones((seq, GW), F32)], 1)
    sin_d = jnp.zeros_like(cos_d)
    is_ctx = (jnp.arange(seq) < ctx_len)[None, :, None]
    for l in range(DEPTH):
        lam_init = 0.8 - 0.6 * math.exp(-0.3 * l)
        m_lat = (jax.nn.silu(c) @ w_ada[l] + b_ada[l])[:, None, :]
        m_ctx = (jax.nn.silu(c_ctx) @ w_ada[l] + b_ada[l])[None, None, :]
        mods = jnp.split(jnp.where(is_ctx, m_ctx, m_lat), 6, axis=-1)
        pa, pb, pc, pd = _split(_modulate(hs, mods[0], mods[1], dtype) @ w_in[l], col_sizes)
        qa, ka, va = _qkv_prep(pa, cos_a, sin_a)
        ya = _diff_attention_pallas(qa, ka, va, diff_lam[l], diff_norm_g[l], ctx_len=ctx_len, lam_init=lam_init)
        yb = _rwkv7_pallas(pb, rwkv_mu[l], rwkv_w0[l], rwkv_w2[l], rwkv_a0[l], rwkv_a2[l], rwkv_g2[l],
                           rwkv_kk[l], rwkv_ka[l], rwkv_rk[l], rwkv_ln_g[l], rwkv_ln_b[l], ctx_len=ctx_len)
        yc = _gated_deltanet_pallas(pc, gdn_conv[l], gdn_a_log[l], gdn_dt_bias[l], gdn_norm_g[l], ctx_len=ctx_len)
        qd, kd, vd = _qkv_prep(pd, cos_d, sin_d)
        yd = _nat_attention_pallas(qd, kd, vd, nat_rpb[l], ctx_len=ctx_len)
        mix = jnp.concatenate([ya, yb, yc, yd], -1).astype(dtype) @ w_out[l]
        hs = _post_norm(hs, mods[2], mix, ln_mix_g[l], ln_mix_b[l], dtype)
        wq_b = peer_wq[l].astype(BF16)
        keys_b = peer_keys[l].reshape(2 * PEER_HEADS, PEER_KEYS, PEER_HALF).astype(BF16)
        u_b = peer_u[l].astype(BF16)
        vt_b = peer_v[l].astype(BF16).T
        ffn = _peer(_modulate(hs, mods[3], mods[4], dtype), wq_b, keys_b, u_b, vt_b)
        hs = _post_norm(hs, mods[5], ffn, ln_ffn_g[l], ln_ffn_b[l], dtype)
    return hs[:, ctx_len:]
```

```python
import functools
import math

import jax
import jax.numpy as jnp
import numpy as np
from jax import lax
from jax.experimental import pallas as pl
from jax.experimental.pallas import tpu as pltpu

D_MODEL = 1024
DEPTH = 2
GRID_W = 64
HEAD_DIM = 64
N_GROUPS = 4
GROUP_HEADS = D_MODEL // (N_GROUPS * HEAD_DIM)
GW = GROUP_HEADS * HEAD_DIM
D_MIX = N_GROUPS * GW
DIFF_HALF = HEAD_DIM // 2
Q_BLOCK = 128
ROPE_BASE = 10000.0
DECAY_LORA = 64
ICLR_LORA = 64
GATE_LORA = 128
RWKV_GN_EPS = 64e-5
RWKV_COLS = 3 * GW + 2 * DECAY_LORA + 2 * ICLR_LORA + GATE_LORA
GDN_CONV = 3
GDN_CHUNK = 64
GDN_COLS = 4 * GW + 4 * GROUP_HEADS
WIN_H = 8
WIN_W = 16
ATTN_COLS = 3 * GW
IN_COLS = ATTN_COLS + RWKV_COLS + GDN_COLS + ATTN_COLS
PEER_HEADS = 8
PEER_KEYS = 128
PEER_EXPERTS = PEER_KEYS * PEER_KEYS
PEER_QDIM = 256
PEER_HALF = PEER_QDIM // 2
PEER_TOPK = 16
DN_ALPHA = (2 * DEPTH) ** 0.25
LN_EPS = 1e-5

F32 = jnp.float32
BF16 = jnp.bfloat16

V7X_VMEM_LIMIT_BYTES = 56 * 1024 * 1024
NEG_BIG = -3.0e38


def _split(x, sizes):
    offs = np.cumsum(sizes)[:-1].tolist()
    return jnp.split(x, offs, axis=-1)


def _heads(x):
    return x.reshape(x.shape[0], x.shape[1], -1, HEAD_DIM)


def _layer_norm(x):
    xf = x.astype(F32)
    mu = xf.mean(-1, keepdims=True)
    var = jnp.square(xf - mu).mean(-1, keepdims=True)
    return (xf - mu) * lax.rsqrt(var + LN_EPS)


def _rms_norm(x, g):
    xf = x.astype(F32)
    return xf * lax.rsqrt(jnp.mean(xf * xf, -1, keepdims=True) + 1e-6) * g.astype(F32)


def _l2norm(x):
    xf = x.astype(F32)
    return xf * lax.rsqrt(jnp.sum(xf * xf, -1, keepdims=True) + 1e-6)


def _modulate(h, shift, scale, dtype):
    return (_layer_norm(h) * (1.0 + scale) + shift).astype(dtype)


def _post_norm(h, gate, y, g, b, dtype):
    z = DN_ALPHA * h.astype(F32) + gate.astype(F32) * y.astype(F32)
    return (_layer_norm(z) * g + b).astype(dtype)


def _dwconv_centred(x, w):
    k = w.shape[0]
    pad = k // 2
    t = x.shape[1]
    xp = jnp.pad(x, ((0, 0), (pad, pad), (0, 0)))
    out = xp[:, 0:t] * w[0]
    for i in range(1, k):
        out = out + xp[:, i:i + t] * w[i]
    return out


def _grid_pos(n_tok):
    t = jnp.arange(n_tok, dtype=jnp.int32)
    return t // GRID_W, t % GRID_W


def _rope_1d(x, pos):
    half = x.shape[-1] // 2
    inv = ROPE_BASE ** (-jnp.arange(half, dtype=F32) / half)
    ang = pos.astype(F32)[:, None] * inv
    cos = jnp.cos(ang)[:, None, :]
    sin = jnp.sin(ang)[:, None, :]
    x1 = x[..., :half].astype(F32)
    x2 = x[..., half:].astype(F32)
    return jnp.concatenate([x1 * cos - x2 * sin, x1 * sin + x2 * cos], -1)


def _rope_2d(x, row, col):
    n = x.shape[-1] // 2
    return jnp.concatenate([_rope_1d(x[..., :n], row), _rope_1d(x[..., n:], col)], -1)


def _diff_core(q, k, v, lam):
    scale = DIFF_HALF ** -0.5
    q = q.astype(F32)
    k = k.astype(F32)
    s1 = jnp.einsum('bqhd,bkhd->bhqk', q[..., :DIFF_HALF], k[..., :DIFF_HALF]) * scale
    s2 = jnp.einsum('bqhd,bkhd->bhqk', q[..., DIFF_HALF:], k[..., DIFF_HALF:]) * scale
    p = jax.nn.softmax(s1, -1) - lam * jax.nn.softmax(s2, -1)
    return jnp.einsum('bhqk,bkhd->bqhd', p, v.astype(F32))


def _diff_attention(q, k, v, qc, kc, vc, lam_vecs, norm_g, lam_init, ctx_out):
    bsz, lat_len = q.shape[:2]
    row, col = _grid_pos(lat_len)
    rot = lambda z: jnp.concatenate([_rope_2d(z[..., :DIFF_HALF], row, col),
                                     _rope_2d(z[..., DIFF_HALF:], row, col)], -1)
    q = rot(q)
    k = rot(k)
    lv = lam_vecs.astype(F32)
    lam = jnp.exp(jnp.sum(lv[0] * lv[1])) - jnp.exp(jnp.sum(lv[2] * lv[3])) + lam_init
    k_all = jnp.concatenate([k, kc.astype(F32)], 1)
    v_all = jnp.concatenate([v.astype(F32), vc.astype(F32)], 1)
    nb = lat_len // Q_BLOCK
    qb = jnp.moveaxis(q.reshape(bsz, nb, Q_BLOCK, GROUP_HEADS, HEAD_DIM), 1, 0)
    ob = lax.map(lambda qq: _diff_core(qq, k_all, v_all, lam), qb)
    o = jnp.moveaxis(ob, 0, 1).reshape(bsz, lat_len, GROUP_HEADS, HEAD_DIM)
    post = lambda z: (_rms_norm(z, norm_g) * (1.0 - lam_init)).reshape(z.shape[0], z.shape[1], GW)
    out_ctx = post(_diff_core(qc, kc, vc, lam)) if ctx_out else None
    return post(o), out_ctx


def _wkv7_scan(r, w, k, v, kk, a, s0, reverse):
    def step(s, inp):
        r_t, w_t, k_t, v_t, kk_t, a_t = inp
        sa = jnp.einsum('bhvk,bhk->bhv', s, kk_t)
        s = (s * w_t[:, :, None, :] - sa[..., None] * (kk_t * a_t)[:, :, None, :]
             + v_t[..., None] * k_t[:, :, None, :])
        return s, jnp.einsum('bhvk,bhk->bhv', s, r_t)
    xs = tuple(jnp.moveaxis(z, 1, 0) for z in (r, w, k, v, kk, a))
    s, y = lax.scan(step, s0, xs, reverse=reverse)
    return jnp.moveaxis(y, 0, 1), s


def _rwkv7(f_lat, f_ctx, mu, w0, w2, a0, a2, g2, k_k, k_a, r_k, ln_g, ln_b, ctx_out):
    shift_w = jnp.stack([mu[0], 1.0 - mu[0] - mu[1], mu[1]])

    def prep(f):
        f = _dwconv_centred(f, shift_w).astype(F32)
        r, k, v, wd, ad, gd = _split(f, [GW, GW, GW, 2 * DECAY_LORA, 2 * ICLR_LORA, GATE_LORA])
        bsz, t = f.shape[:2]
        wd = jnp.tanh(wd.reshape(bsz, t, 2, DECAY_LORA))
        ad = ad.reshape(bsz, t, 2, ICLR_LORA)
        w_raw = w0 + jnp.einsum('btdr,drc->btdc', wd, w2)
        decay = jnp.exp(-jnp.exp(-jax.nn.softplus(-w_raw) - 0.5))
        a = jax.nn.sigmoid(a0 + jnp.einsum('btdr,drc->btdc', ad, a2))
        g = jax.nn.sigmoid(gd) @ g2
        kk = _l2norm(_heads(k * k_k))
        kd = k[:, :, None] * (1.0 + (a - 1.0) * k_a)
        return r, v, g, kk, decay, a, kd

    def run(p, d, s0, rev):
        r, v, g, kk, decay, a, kd = p
        return _wkv7_scan(_heads(r), _heads(decay[:, :, d]), _heads(kd[:, :, d]), _heads(v),
                          kk, _heads(a[:, :, d]), s0, rev)

    def post(p, ys):
        r, v, g, kk, decay, a, kd = p
        bsz, t = r.shape[:2]
        y = ys[0] + ys[1]
        m = y.mean(-1, keepdims=True)
        var = jnp.square(y - m).mean(-1, keepdims=True)
        yn = ((y - m) * lax.rsqrt(var + RWKV_GN_EPS)).reshape(bsz, t, GW) * ln_g + ln_b
        rh, vh = _heads(r), _heads(v)
        bonus = ((rh * _heads(kd[:, :, 0]) * r_k).sum(-1, keepdims=True) * vh
                 + (rh * _heads(kd[:, :, 1]) * r_k).sum(-1, keepdims=True) * vh)
        return (yn + bonus.reshape(bsz, t, GW)) * g

    pl_, pc = prep(f_lat), prep(f_ctx)
    s0 = jnp.zeros((f_lat.shape[0], GROUP_HEADS, HEAD_DIM, HEAD_DIM), F32)
    y_lat, y_ctx = [], []
    for d, rev in ((0, False), (1, True)):
        yc, sc = run(pc, d, s0, rev)
        yl, _ = run(pl_, d, sc, rev)
        y_lat.append(yl)
        y_ctx.append(yc)
    out_ctx = post(pc, y_ctx) if ctx_out else None
    return post(pl_, y_lat), out_ctx


def _gdn_chunked(q, k, v, beta, g, s0):
    bsz, t, h, dk = q.shape
    dv = v.shape[-1]
    n = t // GDN_CHUNK
    ch = lambda z: jnp.moveaxis(z.reshape(bsz, n, GDN_CHUNK, h, *z.shape[3:]), 3, 2)
    q, k, v, beta, g = ch(q), ch(k), ch(v), ch(beta), ch(g)
    gc = jnp.cumsum(g, axis=-1)
    i = jnp.arange(GDN_CHUNK)
    incl = i[:, None] >= i[None, :]
    strict = i[:, None] > i[None, :]
    decay = jnp.exp(jnp.where(incl, gc[..., :, None] - gc[..., None, :], -jnp.inf))
    kb = k * beta[..., None]
    a_low = jnp.where(strict, jnp.einsum('bnhid,bnhjd->bnhij', kb, k) * decay, 0.0)
    tmat = a_low + jnp.eye(GDN_CHUNK, dtype=F32)
    rhs = jnp.concatenate([v * beta[..., None], kb * jnp.exp(gc)[..., None]], -1)
    sol = lax.linalg.triangular_solve(tmat, rhs, left_side=True, lower=True, unit_diagonal=True)
    u, w = sol[..., :dv], sol[..., dv:]
    qk = jnp.where(incl, jnp.einsum('bnhid,bnhjd->bnhij', q, k) * decay, 0.0)
    qg = q * jnp.exp(gc)[..., None]
    kg = k * jnp.exp(gc[..., -1:] - gc)[..., None]
    glast = jnp.exp(gc[..., -1])

    def step(s, xs):
        qg_i, kg_i, u_i, w_i, qk_i, gl_i = xs
        v_new = u_i - jnp.einsum('bhcd,bhdv->bhcv', w_i, s)
        o = jnp.einsum('bhcd,bhdv->bhcv', qg_i, s) + jnp.einsum('bhij,bhjv->bhiv', qk_i, v_new)
        s = s * gl_i[..., None, None] + jnp.einsum('bhcd,bhcv->bhdv', kg_i, v_new)
        return s, o

    xs = tuple(jnp.moveaxis(z, 1, 0) for z in (qg, kg, u, w, qk, glast))
    s, o = lax.scan(step, s0, xs)
    o = jnp.moveaxis(jnp.moveaxis(o, 0, 1), 2, 3).reshape(bsz, t, h, dv)
    return o, s


def _gated_deltanet(f_lat, f_ctx, conv_w, a_log, dt_bias, norm_g, ctx_out):
    def prep(f):
        qkv, gate, ab = _split(f, [3 * GW, GW, 4 * GROUP_HEADS])
        qkv = jax.nn.silu(_dwconv_centred(qkv, conv_w).astype(F32))
        q, k, v = [_heads(z) for z in _split(qkv, [GW, GW, GW])]
        q = _l2norm(q) * HEAD_DIM ** -0.5
        k = _l2norm(k)
        ab = ab.astype(F32).reshape(f.shape[0], f.shape[1], 2, 2, GROUP_HEADS)
        log_alpha = -jnp.exp(a_log) * jax.nn.softplus(ab[:, :, 0] + dt_bias)
        beta = jax.nn.sigmoid(ab[:, :, 1])
        return q, k, v, gate, log_alpha, beta

    def run(p, d, s0):
        tr = (lambda z: jnp.flip(z, 1)) if d == 1 else (lambda z: z)
        q, k, v, gate, log_alpha, beta = p
        o, s = _gdn_chunked(tr(q), tr(k), tr(v), tr(beta[:, :, d]), tr(log_alpha[:, :, d]), s0)
        return tr(o), s

    def post(p, os_):
        gate = p[3]
        o = os_[0] + os_[1]
        y = _rms_norm(o, norm_g) * jax.nn.silu(_heads(gate.astype(F32)))
        return y.reshape(o.shape[0], o.shape[1], GW)

    pl_, pc = prep(f_lat), prep(f_ctx)
    s0 = jnp.zeros((f_lat.shape[0], GROUP_HEADS, HEAD_DIM, HEAD_DIM), F32)
    o_lat, o_ctx = [], []
    for d in range(2):
        oc, sc = run(pc, d, s0)
        ol, _ = run(pl_, d, sc)
        o_lat.append(ol)
        o_ctx.append(oc)
    out_ctx = post(pc, o_ctx) if ctx_out else None
    return post(pl_, o_lat), out_ctx


def _softmax_attn(q, k, v):
    s = jnp.einsum('bqhd,bkhd->bhqk', q.astype(F32), k.astype(F32)) * HEAD_DIM ** -0.5
    return jnp.einsum('bhqk,bkhd->bqhd', jax.nn.softmax(s, -1), v.astype(F32))


def _neighbourhood_attention(q, k, v, qc, kc, vc, rpb, ctx_out):
    bsz, lat_len = q.shape[:2]
    rows = lat_len // GRID_W
    kh = min(WIN_H, rows)
    scale = HEAD_DIM ** -0.5
    grid = lambda z: z.astype(F32).reshape(bsz, rows, GRID_W, GROUP_HEADS, HEAD_DIM)
    qg, kg, vg = grid(q), grid(k), grid(v)
    kc32, vc32 = kc.astype(F32), vc.astype(F32)
    rpb = rpb.astype(F32)
    cq = jnp.arange(GRID_W)
    c_start = jnp.clip(cq - WIN_W // 2, 0, GRID_W - WIN_W)
    col_ok = (cq[None, :] >= c_start[:, None]) & (cq[None, :] < c_start[:, None] + WIN_W)
    d_col = jnp.clip(cq[None, :] - cq[:, None], -(WIN_W - 1), WIN_W - 1) + (WIN_W - 1)

    def row_block(r):
        rs = jnp.clip(r - kh // 2, 0, rows - kh)
        q_r = lax.dynamic_index_in_dim(qg, r, axis=1, keepdims=False)
        k_r = lax.dynamic_slice_in_dim(kg, rs, kh, axis=1)
        v_r = lax.dynamic_slice_in_dim(vg, rs, kh, axis=1)
        s = jnp.einsum('bqhd,bkwhd->bqhkw', q_r, k_r) * scale
        d_row = rs + jnp.arange(kh) - r + (WIN_H - 1)
        bias = rpb[:, d_row[None, :, None], d_col[:, None, :]]
        s = jnp.where(col_ok[:, None, None, :], s + jnp.transpose(bias, (1, 0, 2, 3)), -jnp.inf)
        s_c = jnp.einsum('bqhd,bchd->bqhc', q_r, kc32) * scale
        p = jax.nn.softmax(jnp.concatenate(
            [s.reshape(bsz, GRID_W, GROUP_HEADS, kh * GRID_W), s_c], -1), -1)
        p_win = p[..., :kh * GRID_W].reshape(bsz, GRID_W, GROUP_HEADS, kh, GRID_W)
        return (jnp.einsum('bqhkw,bkwhd->bqhd', p_win, v_r)
                + jnp.einsum('bqhc,bchd->bqhd', p[..., kh * GRID_W:], vc32))

    o = lax.map(row_block, jnp.arange(rows))
    out_lat = jnp.moveaxis(o, 0, 1).reshape(bsz, lat_len, GW)
    out_ctx = _softmax_attn(qc, kc, vc).reshape(bsz, qc.shape[1], GW) if ctx_out else None
    return out_lat, out_ctx


def _qkv_heads(p):
    return [_heads(z) for z in _split(p, [GW, GW, GW])]


PEER_STAT_ROWS = 4 * PEER_HEADS


def _topk_rows(x, k):
    rows = []
    cur = x
    for i in range(k):
        m = jnp.max(cur, axis=0, keepdims=True)
        rows.append(m)
        if i + 1 < k:
            cur = jnp.where(cur == m, NEG_BIG, cur)
    return rows


def _peer_score_kernel(x_ref, wq_ref, keys_ref, s_ref, st_ref):
    q = jnp.dot(x_ref[...], wq_ref[...], preferred_element_type=F32).astype(BF16)
    stats = []
    for h in range(PEER_HEADS):
        tops = []
        for p in range(2):
            hp = 2 * h + p
            s_t = lax.dot_general(keys_ref[hp], q[:, hp * PEER_HALF:(hp + 1) * PEER_HALF],
                                  (((1,), (1,)), ((), ())), preferred_element_type=F32)
            s_ref[hp] = s_t
            tops.append(_topk_rows(s_t, PEER_TOPK))
        top_b = jnp.concatenate(tops[1], axis=0)
        cand = jnp.concatenate([tops[0][i] + top_b for i in range(PEER_TOPK)], axis=0)
        tau = _topk_rows(cand, PEER_TOPK)[-1]
        best = tops[0][0] + tops[1][0]
        z = jnp.sum(jnp.where(cand >= tau, jnp.exp(cand - best), 0.0), axis=0, keepdims=True)
        stats += [tau, tops[0][0], tops[1][0], 1.0 / z]
    st_ref[...] = jnp.concatenate(stats, axis=0)


def _peer_expert_kernel(x_ref, s_ref, st_ref, u_ref, vt_ref, o_ref, e_ref, acc_ref, w_ref, *, rows_per_step, n_chunks):
    j = pl.program_id(1)
    cur = lax.rem(j, 2)

    @pl.when(j == 0)
    def _():
        acc_ref[...] = jnp.zeros_like(acc_ref)
        w_ref[1] = jnp.zeros(w_ref.shape[1:], w_ref.dtype)
        for h in range(PEER_HEADS):
            a0 = st_ref[4 * h + 1:4 * h + 2, :]
            b0 = st_ref[4 * h + 2:4 * h + 3, :]
            rz = st_ref[4 * h + 3:4 * h + 4, :]
            e_ref[2 * h] = jnp.exp(s_ref[2 * h] - a0) * rz
            e_ref[2 * h + 1] = jnp.exp(s_ref[2 * h + 1] - b0)

    @pl.when(j < n_chunks)
    def _():
        act_all = lax.dot_general(u_ref[...], x_ref[...], (((1,), (1,)), ((), ())),
                                  preferred_element_type=F32)
        acc_ref[...] += jnp.dot(vt_ref[...], w_ref[1 - cur], preferred_element_type=F32)
        for il in range(rows_per_step):
            i = j * rows_per_step + il
            act = act_all[il * PEER_KEYS:(il + 1) * PEER_KEYS]
            act = 0.5 * act * (1.0 + lax.erf(act * (2.0 ** -0.5)))
            gate = jnp.zeros_like(act)
            for h in range(PEER_HEADS):
                c = s_ref[2 * h, pl.ds(i, 1), :] + s_ref[2 * h + 1]
                tau = st_ref[4 * h:4 * h + 1, :]
                gate = gate + jnp.where(c >= tau, e_ref[2 * h, pl.ds(i, 1), :] * e_ref[2 * h + 1], 0.0)
            w_ref[cur, il * PEER_KEYS:(il + 1) * PEER_KEYS, :] = (act * gate).astype(BF16)

    @pl.when(j == n_chunks)
    def _():
        o_ref[...] = (acc_ref[...] + jnp.dot(vt_ref[...], w_ref[1 - cur], preferred_element_type=F32)).T


def _peer_pallas(h, w_q, keys, u_tab, vt_tab, *, tb, rows_per_step):
    n_tok, d = h.shape
    nhp = 2 * PEER_HEADS
    s, st = pl.pallas_call(
        _peer_score_kernel,
        name="peer_scores",
        grid=(n_tok // tb,),
        in_specs=[pl.BlockSpec((tb, d), lambda i: (i, 0)),
                  pl.BlockSpec(w_q.shape, lambda i: (0, 0)),
                  pl.BlockSpec(keys.shape, lambda i: (0, 0, 0))],
        out_specs=[pl.BlockSpec((nhp, PEER_KEYS, tb), lambda i: (0, 0, i)),
                   pl.BlockSpec((PEER_STAT_ROWS, tb), lambda i: (0, i))],
        out_shape=[jax.ShapeDtypeStruct((nhp, PEER_KEYS, n_tok), F32),
                   jax.ShapeDtypeStruct((PEER_STAT_ROWS, n_tok), F32)],
        compiler_params=pltpu.CompilerParams(dimension_semantics=("arbitrary",),
                                             vmem_limit_bytes=V7X_VMEM_LIMIT_BYTES),
    )(h, w_q, keys)
    ec = rows_per_step * PEER_KEYS
    n_chunks = PEER_EXPERTS // ec
    return pl.pallas_call(
        functools.partial(_peer_expert_kernel, rows_per_step=rows_per_step, n_chunks=n_chunks),
        name="peer_experts",
        grid=(n_tok // tb, n_chunks + 1),
        in_specs=[pl.BlockSpec((tb, d), lambda i, j: (i, 0)),
                  pl.BlockSpec((nhp, PEER_KEYS, tb), lambda i, j: (0, 0, i)),
                  pl.BlockSpec((PEER_STAT_ROWS, tb), lambda i, j: (0, i)),
                  pl.BlockSpec((ec, d), lambda i, j: (jnp.minimum(j, n_chunks - 1), 0)),
                  pl.BlockSpec((d, ec), lambda i, j: (0, jnp.maximum(j - 1, 0)))],
        out_specs=pl.BlockSpec((tb, d), lambda i, j: (i, 0)),
        out_shape=jax.ShapeDtypeStruct((n_tok, d), F32),
        scratch_shapes=[pltpu.VMEM((nhp, PEER_KEYS, tb), F32),
                        pltpu.VMEM((d, tb), F32),
                        pltpu.VMEM((2, ec, tb), BF16)],
        compiler_params=pltpu.CompilerParams(dimension_semantics=("arbitrary", "arbitrary"),
                                             vmem_limit_bytes=V7X_VMEM_LIMIT_BYTES),
    )(h, s, st, u_tab, vt_tab)


def _peer(h, w_q, keys, u_tab, vt_tab):
    bsz, t, d = h.shape
    n_tok = bsz * t
    tb = 512 if n_tok % 512 == 0 else 256
    out = _peer_pallas(h.reshape(n_tok, d).astype(BF16), w_q, keys, u_tab, vt_tab, tb=tb, rows_per_step=8)
    return out.reshape(bsz, t, d)


ROW_BLOCK = 256
HIGHEST = lax.Precision.HIGHEST


def _dot_f32(a, b):
    return jnp.dot(a, b, precision=HIGHEST, preferred_element_type=F32)


def _segment_ones(n, seg, dtype):
    r = lax.broadcasted_iota(jnp.int32, (n, n), 0) // seg
    c = lax.broadcasted_iota(jnp.int32, (n, n), 1) // seg
    return jnp.where(r == c, 1.0, 0.0).astype(dtype)


def _shifted_rows(x, prev_row, next_row):
    t = x.shape[0]
    rows = lax.broadcasted_iota(jnp.int32, x.shape, 0)
    xm = jnp.where(rows == 0, prev_row, pltpu.roll(x, 1, axis=0))
    xp = jnp.where(rows == t - 1, next_row, pltpu.roll(x, t - 1, axis=0))
    return xm, xp


def _segment_edge_flags(i, n_blocks, ctx_blocks):
    is_start = jnp.logical_or(i == 0, i == ctx_blocks)
    is_end = jnp.logical_or(i == ctx_blocks - 1, i == n_blocks - 1)
    return jnp.where(is_start, 0.0, 1.0), jnp.where(is_end, 0.0, 1.0)


def _halo_specs(width, tr):
    g = tr // 8
    prev = pl.BlockSpec((1, 8, width), lambda b, i: (b, jnp.maximum(i * g - 1, 0), 0))
    nxt = lambda n_groups: pl.BlockSpec((1, 8, width), lambda b, i: (b, jnp.minimum((i + 1) * g, n_groups - 1), 0))
    return prev, nxt


def _softplus(z):
    return jnp.maximum(z, 0.0) + jnp.log1p(jnp.exp(-jnp.abs(z)))


def _rwkv_prep_kernel(x_ref, xprev_ref, xnext_ref, mu_ref, w0_ref, w2_ref, a0_ref, a2_ref, g2_ref,
                      kk_ref, ka_ref, rk_ref,
                      r_out, v_out, kkn_out, g_out, bonus_out, w_out, b_out, kt_out, *, ctx_blocks):
    i = pl.program_id(1)
    keep_prev, keep_next = _segment_edge_flags(i, pl.num_programs(1), ctx_blocks)
    x = x_ref[0]
    xm, xp = _shifted_rows(x, xprev_ref[0, 7:8, :] * keep_prev, xnext_ref[0, 0:1, :] * keep_next)
    mu0 = mu_ref[0:1, :]
    mu1 = mu_ref[1:2, :]
    f = xm * mu0 + x * (1.0 - mu0 - mu1) + xp * mu1
    r = f[:, 0:GW]
    k = f[:, GW:2 * GW]
    v = f[:, 2 * GW:3 * GW]
    o = 3 * GW
    wd = jnp.tanh(f[:, o:o + 2 * DECAY_LORA])
    ad = f[:, o + 2 * DECAY_LORA:o + 2 * DECAY_LORA + 2 * ICLR_LORA]
    gd = f[:, o + 2 * DECAY_LORA + 2 * ICLR_LORA:]
    w_raw = w0_ref[...] + _dot_f32(wd, w2_ref[...])
    decay = jnp.exp(-jnp.exp(-_softplus(-w_raw) - 0.5))
    a = jax.nn.sigmoid(a0_ref[...] + _dot_f32(ad, a2_ref[...]))
    g = _dot_f32(jax.nn.sigmoid(gd), g2_ref[...])
    head_sum = _segment_ones(GW, HEAD_DIM, F32)
    kx = k * kk_ref[...]
    kkn = kx * lax.rsqrt(_dot_f32(kx * kx, head_sum) + 1e-6)
    kd_sum = jnp.zeros_like(k)
    for d in range(2):
        a_d = a[:, d * GW:(d + 1) * GW]
        kd = k * (1.0 + (a_d - 1.0) * ka_ref[...])
        kd_sum = kd_sum + kd
        w_out[d, 0] = decay[:, d * GW:(d + 1) * GW]
        b_out[d, 0] = kkn * a_d
        kt_out[d, 0] = kd
    r_out[0] = r
    v_out[0] = v
    kkn_out[0] = kkn
    g_out[0] = g
    bonus_out[0] = _dot_f32(r * kd_sum * rk_ref[...], head_sum) * v


def _rwkv_scan_kernel(rf_ref, vf_ref, kkf_ref, wf_ref, bf_ref, ktf_ref,
                      rb_ref, vb_ref, kkb_ref, wb_ref, bb_ref, ktb_ref,
                      yf_ref, yb_ref, st_ref):
    c = pl.program_id(0)
    bsz, tt, _ = rf_ref.shape
    n_pairs = GW // 128

    @pl.when(c == 0)
    def _():
        st_ref[...] = jnp.zeros_like(st_ref)

    ones_bd = _segment_ones(128, HEAD_DIM, BF16)
    rows = lax.broadcasted_iota(jnp.int32, (HEAD_DIM, 128), 0)
    lanes = lax.broadcasted_iota(jnp.int32, (HEAD_DIM, 128), 1)
    diag = jnp.where(lanes % HEAD_DIM == rows, 1.0, 0.0)
    dirs = ((rf_ref, vf_ref, kkf_ref, wf_ref, bf_ref, ktf_ref, yf_ref),
            (rb_ref, vb_ref, kkb_ref, wb_ref, bb_ref, ktb_ref, yb_ref))

    n_groups = tt // 8

    def group(tg, carry):
        chains = [(b, p) for b in range(bsz) for p in range(n_pairs)]
        nc = len(chains)
        base = [pl.multiple_of((tg if d == 0 else n_groups - 1 - tg) * 8, 8) for d in range(2)]
        tiles = [[], []]
        states = [[], []]
        for d in range(2):
            r_ref, v_ref, kk_ref, w_ref, b_ref, kt_ref, _ = dirs[d]
            for b, p in chains:
                cols = slice(p * 128, (p + 1) * 128)
                tiles[d].append(tuple(ref[b, pl.ds(base[d], 8), cols] for ref in (r_ref, v_ref, kk_ref))
                                + tuple(ref[0, b, pl.ds(base[d], 8), cols] for ref in (w_ref, b_ref, kt_ref)))
                states[d].append(st_ref[d, b * n_pairs + p])
        ys = [[[None] * 8 for _ in chains] for _ in range(2)]
        pending = [None, None]

        def emit_outputs(d, yb, jj):
            for ci in range(nc):
                ys[d][ci][jj] = jnp.sum(yb[ci * HEAD_DIM:(ci + 1) * HEAD_DIM] * diag, axis=0, keepdims=True)

        for j in range(8):
            for d in range(2):
                jj = j if d == 0 else 7 - j
                row = lambda a: a[jj:jj + 1, :]
                parts = ([s * row(t[2]) for s, t in zip(states[d], tiles[d])]
                         + [diag * row(t[1]) for t in tiles[d]])
                if pending[d] is not None:
                    parts += pending[d][0]
                res = jnp.dot(jnp.concatenate(parts, axis=0).astype(BF16), ones_bd, preferred_element_type=F32)
                for ci, t in enumerate(tiles[d]):
                    sa = res[ci * HEAD_DIM:(ci + 1) * HEAD_DIM]
                    vcol = res[(nc + ci) * HEAD_DIM:(nc + ci + 1) * HEAD_DIM]
                    states[d][ci] = states[d][ci] * row(t[3]) - sa * row(t[4]) + vcol * row(t[5])
                if pending[d] is not None:
                    emit_outputs(d, res[2 * nc * HEAD_DIM:], pending[d][1])
                pending[d] = ([s * row(t[0]) for s, t in zip(states[d], tiles[d])], jj)
        for d in range(2):
            q = jnp.concatenate(pending[d][0], axis=0).astype(BF16)
            emit_outputs(d, jnp.dot(q, ones_bd, preferred_element_type=F32), pending[d][1])
            y_ref = dirs[d][6]
            for ci, (b, p) in enumerate(chains):
                st_ref[d, b * n_pairs + p] = states[d][ci]
                y_ref[b, pl.ds(base[d], 8), p * 128:(p + 1) * 128] = jnp.concatenate(ys[d][ci], axis=0)
        return carry

    lax.fori_loop(0, n_groups, group, 0)


def _rwkv_post_kernel(yf_ref, yb_ref, bonus_ref, g_ref, lng_ref, lnb_ref, o_ref):
    y = yf_ref[0] + yb_ref[0]
    head_mean = _segment_ones(GW, HEAD_DIM, F32) * (1.0 / HEAD_DIM)
    m = _dot_f32(y, head_mean)
    yc = y - m
    var = _dot_f32(yc * yc, head_mean)
    yn = yc * lax.rsqrt(var + RWKV_GN_EPS) * lng_ref[...] + lnb_ref[...]
    o_ref[0] = (yn + bonus_ref[0]) * g_ref[0]


def _block_diag2(m):
    z = jnp.zeros_like(m[0])
    return jnp.concatenate([jnp.concatenate([m[0], z], 1), jnp.concatenate([z, m[1]], 1)], 0)


def _rwkv7_pallas(f, mu, w0, w2, a0, a2, g2, k_k, k_a, r_k, ln_g, ln_b, *, ctx_len):
    bsz, seq, cols = f.shape
    tr = ROW_BLOCK
    assert ctx_len % tr == 0 and seq % tr == 0
    nb, ctx_blocks = seq // tr, ctx_len // tr
    prev_spec, next_spec = _halo_specs(cols, tr)
    row2 = lambda a: a.reshape(1, -1).astype(F32)
    full = lambda a: pl.BlockSpec(a.shape, lambda b, i: (0,) * a.ndim)
    params = [mu, row2(w0), _block_diag2(w2), row2(a0), _block_diag2(a2), g2, row2(k_k), row2(k_a), row2(r_k)]
    act = jax.ShapeDtypeStruct((bsz, seq, GW), F32)
    act2 = jax.ShapeDtypeStruct((2, bsz, seq, GW), F32)
    blk = pl.BlockSpec((1, tr, GW), lambda b, i: (b, i, 0))
    blk2 = pl.BlockSpec((2, 1, tr, GW), lambda b, i: (0, b, i, 0))
    r, v, kkn, g, bonus, w, bb, kt = pl.pallas_call(
        functools.partial(_rwkv_prep_kernel, ctx_blocks=ctx_blocks),
        name="rwkv_prep",
        grid=(bsz, nb),
        in_specs=[pl.BlockSpec((1, tr, cols), lambda b, i: (b, i, 0)), prev_spec, next_spec(seq // 8)]
                 + [full(p) for p in params],
        out_specs=[blk] * 5 + [blk2] * 3,
        out_shape=[act] * 5 + [act2] * 3,
        compiler_params=pltpu.CompilerParams(dimension_semantics=("arbitrary", "arbitrary"),
                                             vmem_limit_bytes=V7X_VMEM_LIMIT_BYTES),
    )(f, f, f, *params)

    def bwd_block(c):
        return jnp.where(c < ctx_blocks, ctx_blocks - 1 - c, nb - 1 - (c - ctx_blocks))
    fwd = pl.BlockSpec((bsz, tr, GW), lambda c: (0, c, 0))
    bwd = pl.BlockSpec((bsz, tr, GW), lambda c: (0, bwd_block(c), 0))
    fwd_d = pl.BlockSpec((1, bsz, tr, GW), lambda c: (0, 0, c, 0))
    bwd_d = pl.BlockSpec((1, bsz, tr, GW), lambda c: (1, 0, bwd_block(c), 0))
    yf, yb = pl.pallas_call(
        _rwkv_scan_kernel,
        name="rwkv_scan",
        grid=(nb,),
        in_specs=[fwd, fwd, fwd, fwd_d, fwd_d, fwd_d, bwd, bwd, bwd, bwd_d, bwd_d, bwd_d],
        out_specs=[fwd, bwd],
        out_shape=[act, act],
        scratch_shapes=[pltpu.VMEM((2, bsz * (GW // 128), HEAD_DIM, 128), F32)],
        compiler_params=pltpu.CompilerParams(dimension_semantics=("arbitrary",),
                                             vmem_limit_bytes=V7X_VMEM_LIMIT_BYTES),
    )(r, v, kkn, w, bb, kt, r, v, kkn, w, bb, kt)

    return pl.pallas_call(
        _rwkv_post_kernel,
        name="rwkv_post",
        grid=(bsz, nb),
        in_specs=[blk, blk, blk, blk, full(row2(ln_g)), full(row2(ln_b))],
        out_specs=blk,
        out_shape=act,
        compiler_params=pltpu.CompilerParams(dimension_semantics=("arbitrary", "arbitrary")),
    )(yf, yb, bonus, g, row2(ln_g), row2(ln_b))


GDN_GATE_LANES = 128
GDN_PADDED_COLS = 4 * GW + GDN_GATE_LANES


def _gdn_prep_kernel(x_ref, xprev_ref, xnext_ref, ab_ref, conv_ref, alog_ref, dtb_ref,
                     q_out, k_out, v_out, gb_out, *, ctx_blocks):
    i = pl.program_id(1)
    keep_prev, keep_next = _segment_edge_flags(i, pl.num_programs(1), ctx_blocks)
    x = x_ref[0]
    xm, xp = _shifted_rows(x, xprev_ref[0, 7:8, :] * keep_prev, xnext_ref[0, 0:1, :] * keep_next)
    y = xm * conv_ref[0:1, :] + x * conv_ref[1:2, :] + xp * conv_ref[2:3, :]
    y = y * jax.nn.sigmoid(y)
    head_sum = _segment_ones(GW, HEAD_DIM, F32)
    q = y[:, 0:GW]
    k = y[:, GW:2 * GW]
    q_out[0] = q * lax.rsqrt(_dot_f32(q * q, head_sum) + 1e-6) * (HEAD_DIM ** -0.5)
    k_out[0] = k * lax.rsqrt(_dot_f32(k * k, head_sum) + 1e-6)
    v_out[0] = y[:, 2 * GW:3 * GW]
    ab = ab_ref[0]
    lane = lax.broadcasted_iota(jnp.int32, ab.shape, 1)
    log_alpha = -jnp.exp(alog_ref[...]) * _softplus(ab + dtb_ref[...])
    gb_out[0] = jnp.where(lane < 2 * GROUP_HEADS, log_alpha, jax.nn.sigmoid(ab))


def _gdn_chunk_kernel(qf_ref, kf_ref, vf_ref, gf_ref, qb_ref, kb_ref, vb_ref, gb_ref, of_ref, ob_ref, st_ref):
    i = pl.program_id(1)
    c = GDN_CHUNK
    n_chunks = qf_ref.shape[1] // c

    @pl.when(i == 0)
    def _():
        st_ref[...] = jnp.zeros_like(st_ref)

    r = lax.broadcasted_iota(jnp.int32, (c, c), 0)
    s = lax.broadcasted_iota(jnp.int32, (c, c), 1)
    eye = r == s
    ones_cc = jnp.ones((c, c), F32)
    incl = (r >= s, r <= s)
    strict = (r > s, r < s)
    levels = []
    b = 1
    while b < c:
        levels.append(jnp.logical_and(r // (2 * b) == s // (2 * b), r // b != s // b))
        b *= 2
    dirs = ((qf_ref, kf_ref, vf_ref, gf_ref, of_ref), (qb_ref, kb_ref, vb_ref, gb_ref, ob_ref))

    bdot = lambda x, y: jnp.dot(x.astype(BF16), y.astype(BF16), preferred_element_type=F32)
    bdot_nt = lambda x, y: lax.dot_general(x.astype(BF16), y.astype(BF16), _NT, preferred_element_type=F32)
    bdot_tn = lambda x, y: lax.dot_general(x.astype(BF16), y.astype(BF16), (((0,), (0,)), ((), ())),
                                           preferred_element_type=F32)

    def chunk(cc, carry):
        chains = [(d, h) for d in range(2) for h in range(GROUP_HEADS)]
        row0 = [pl.multiple_of((cc if d == 0 else n_chunks - 1 - cc) * c, c) for d in range(2)]
        gates = [dirs[d][3][0, pl.ds(row0[d], c), :] for d in range(2)]
        cum = [_dot_f32(jnp.where(incl[d], 1.0, 0.0), gates[d]) for d in range(2)]
        tot = [_dot_f32(ones_cc, gates[d]) for d in range(2)]
        st = []
        for d, h in chains:
            cols = slice(h * HEAD_DIM, (h + 1) * HEAD_DIM)
            lg = d * GROUP_HEADS + h
            q, k, v = (dirs[d][n][0, pl.ds(row0[d], c), cols] for n in range(3))
            gc = cum[d][:, lg:lg + 1]
            st.append(dict(q=q, k=k, v=v, gc=gc, gt=tot[d][:, lg:lg + 1],
                           beta=gates[d][:, 2 * GROUP_HEADS + lg:2 * GROUP_HEADS + lg + 1],
                           gc_row=_dot_f32(ones_cc, jnp.where(eye, gc, 0.0))))
        for (d, h), x in zip(chains, st):
            x["decay"] = jnp.exp(jnp.where(incl[d], x["gc"] - x["gc_row"], NEG_BIG))
            x["kb"] = x["k"] * x["beta"]
            x["a"] = jnp.where(strict[d], bdot_nt(x["kb"], x["k"]) * x["decay"], 0.0)
            x["qk"] = jnp.where(incl[d], bdot_nt(x["q"], x["k"]) * x["decay"], 0.0)
            x["inv"] = jnp.where(eye, 1.0, 0.0)
        for lvl in levels:
            for x in st:
                x["t"] = _dot_f32(jnp.where(lvl, x["a"], 0.0), x["inv"])
            for x in st:
                x["inv"] = x["inv"] - _dot_f32(x["inv"], x["t"])
        for x in st:
            x["eg"] = jnp.exp(x["gc"])
            x["sol"] = _dot_f32(x["inv"], jnp.concatenate([x["v"] * x["beta"], x["kb"] * x["eg"]], axis=-1))
        for (d, h), x in zip(chains, st):
            x["state"] = st_ref[d, h]
            x["ws"] = bdot(x["sol"][:, HEAD_DIM:], x["state"])
            x["qs"] = bdot(x["q"] * x["eg"], x["state"])
        for x in st:
            x["v_new"] = x["sol"][:, :HEAD_DIM] - x["ws"]
            x["o"] = x["qs"] + bdot(x["qk"], x["v_new"])
            x["upd"] = bdot_tn(x["k"] * jnp.exp(x["gt"] - x["gc"]), x["v_new"])
        for (d, h), x in zip(chains, st):
            st_ref[d, h] = x["state"] * jnp.exp(x["gt"][0:1, :]) + x["upd"]
        for d in range(2):
            dirs[d][4][0, pl.ds(row0[d], c), :] = jnp.concatenate(
                [x["o"] for (dd, h), x in zip(chains, st) if dd == d], axis=-1)
        return carry

    lax.fori_loop(0, n_chunks, chunk, 0)


def _gdn_post_kernel(of_ref, ob_ref, gate_ref, g_ref, o_ref):
    o = of_ref[0] + ob_ref[0]
    head_mean = _segment_ones(GW, HEAD_DIM, F32) * (1.0 / HEAD_DIM)
    gate = gate_ref[0]
    o_ref[0] = o * lax.rsqrt(_dot_f32(o * o, head_mean) + 1e-6) * g_ref[...] * (gate * jax.nn.sigmoid(gate))


def _gated_deltanet_pallas(f, conv_w, a_log, dt_bias, norm_g, *, ctx_len):
    bsz, seq, width = f.shape
    tr = ROW_BLOCK
    assert ctx_len % tr == 0 and seq % tr == 0 and tr % GDN_CHUNK == 0
    if width == GDN_COLS:
        f = jnp.pad(f, ((0, 0), (0, 0), (0, GDN_PADDED_COLS - GDN_COLS)))
    assert f.shape[2] == GDN_PADDED_COLS
    nb, ctx_blocks = seq // tr, ctx_len // tr
    lane_pad = lambda a: jnp.pad(a.reshape(1, -1).astype(F32), ((0, 0), (0, GDN_GATE_LANES - a.size)))
    prev_spec, next_spec = _halo_specs(3 * GW, tr)
    full = lambda a: pl.BlockSpec(a.shape, lambda b, i: (0,) * a.ndim)
    act = jax.ShapeDtypeStruct((bsz, seq, GW), F32)
    gact = jax.ShapeDtypeStruct((bsz, seq, GDN_GATE_LANES), F32)
    blk = pl.BlockSpec((1, tr, GW), lambda b, i: (b, i, 0))
    gblk = pl.BlockSpec((1, tr, GDN_GATE_LANES), lambda b, i: (b, i, 0))
    gate_view = pl.BlockSpec((1, tr, GW), lambda b, i: (b, i, 3))
    ab_view = pl.BlockSpec((1, tr, GDN_GATE_LANES), lambda b, i: (b, i, 4 * GW // GDN_GATE_LANES))
    params = [conv_w.astype(F32), lane_pad(a_log), lane_pad(dt_bias)]
    q, k, v, gb = pl.pallas_call(
        functools.partial(_gdn_prep_kernel, ctx_blocks=ctx_blocks),
        name="gdn_prep",
        grid=(bsz, nb),
        in_specs=[pl.BlockSpec((1, tr, 3 * GW), lambda b, i: (b, i, 0)), prev_spec, next_spec(seq // 8), ab_view]
                 + [full(p) for p in params],
        out_specs=[blk, blk, blk, gblk],
        out_shape=[act, act, act, gact],
        compiler_params=pltpu.CompilerParams(dimension_semantics=("arbitrary", "arbitrary"),
                                             vmem_limit_bytes=V7X_VMEM_LIMIT_BYTES),
    )(f, f, f, f, *params)

    def bwd_block(i):
        return jnp.where(i < ctx_blocks, ctx_blocks - 1 - i, nb - 1 - (i - ctx_blocks))
    bblk = pl.BlockSpec((1, tr, GW), lambda b, i: (b, bwd_block(i), 0))
    bgblk = pl.BlockSpec((1, tr, GDN_GATE_LANES), lambda b, i: (b, bwd_block(i), 0))
    of, ob = pl.pallas_call(
        _gdn_chunk_kernel,
        name="gdn_chunks",
        grid=(bsz, nb),
        in_specs=[blk, blk, blk, gblk, bblk, bblk, bblk, bgblk],
        out_specs=[blk, bblk],
        out_shape=[act, act],
        scratch_shapes=[pltpu.VMEM((2, GROUP_HEADS, HEAD_DIM, HEAD_DIM), F32)],
        compiler_params=pltpu.CompilerParams(dimension_semantics=("arbitrary", "arbitrary"),
                                             vmem_limit_bytes=V7X_VMEM_LIMIT_BYTES),
    )(q, k, v, gb, q, k, v, gb)

    g_row = jnp.tile(norm_g.reshape(1, HEAD_DIM).astype(F32), (1, GROUP_HEADS))
    return pl.pallas_call(
        _gdn_post_kernel,
        name="gdn_post",
        grid=(bsz, nb),
        in_specs=[blk, blk, gate_view, full(g_row)],
        out_specs=blk,
        out_shape=act,
        compiler_params=pltpu.CompilerParams(dimension_semantics=("arbitrary", "arbitrary")),
    )(of, ob, f, g_row)


ROPE_PAIR = DIFF_HALF // 4


def _rope_tables(seq, ctx_len, q_scale):
    n = jnp.arange(seq - ctx_len, dtype=jnp.int32)
    row, col = n // GRID_W, n % GRID_W
    i = jnp.arange(HEAD_DIM)
    grp = (i % DIFF_HALF) // (2 * ROPE_PAIR)
    inv = ROPE_BASE ** (-(i % ROPE_PAIR).astype(F32) / ROPE_PAIR)
    pos = jnp.where(grp[None, :] == 0, row[:, None], col[:, None]).astype(F32)
    ang = pos * inv[None, :]
    sign = jnp.where((i % (2 * ROPE_PAIR)) < ROPE_PAIR, -1.0, 1.0)
    cos = jnp.concatenate([jnp.ones((ctx_len, HEAD_DIM), F32), jnp.cos(ang)], 0)
    sin = jnp.concatenate([jnp.zeros((ctx_len, HEAD_DIM), F32), jnp.sin(ang) * sign], 0)
    cos = jnp.tile(cos, (1, GROUP_HEADS))
    sin = jnp.tile(sin, (1, GROUP_HEADS))
    return jnp.concatenate([cos * q_scale, cos], 1), jnp.concatenate([sin * q_scale, sin], 1)


def _qkv_prep_kernel(p_ref, cos_ref, sin_ref, q_out, k_out, v_out):
    qk = p_ref[0, :, 0:2 * GW]
    width = 2 * GW
    lane = lax.broadcasted_iota(jnp.int32, qk.shape, 1)
    partner = jnp.where(lane % (2 * ROPE_PAIR) < ROPE_PAIR,
                        pltpu.roll(qk, width - ROPE_PAIR, axis=1), pltpu.roll(qk, ROPE_PAIR, axis=1))
    rot = qk * cos_ref[...] + partner * sin_ref[...]
    q_out[0] = rot[:, 0:GW].astype(BF16)
    v = p_ref[0, :, 2 * GW:3 * GW]
    for h in range(GROUP_HEADS):
        k_out[0, h] = rot[:, GW + h * HEAD_DIM:GW + (h + 1) * HEAD_DIM].astype(BF16)
        v_out[0, h] = v[:, h * HEAD_DIM:(h + 1) * HEAD_DIM].astype(BF16)


def _qkv_prep(p, cos, sin):
    bsz, seq, _ = p.shape
    tr = ROW_BLOCK
    head_major = jax.ShapeDtypeStruct((bsz, GROUP_HEADS, seq, HEAD_DIM), BF16)
    hm_spec = pl.BlockSpec((1, GROUP_HEADS, tr, HEAD_DIM), lambda b, i: (b, 0, i, 0))
    return pl.pallas_call(
        _qkv_prep_kernel,
        name="qkv_prep",
        grid=(bsz, seq // tr),
        in_specs=[pl.BlockSpec((1, tr, 3 * GW), lambda b, i: (b, i, 0)),
                  pl.BlockSpec((tr, 2 * GW), lambda b, i: (i, 0)),
                  pl.BlockSpec((tr, 2 * GW), lambda b, i: (i, 0))],
        out_specs=[pl.BlockSpec((1, tr, GW), lambda b, i: (b, i, 0)), hm_spec, hm_spec],
        out_shape=[jax.ShapeDtypeStruct((bsz, seq, GW), BF16), head_major, head_major],
        compiler_params=pltpu.CompilerParams(dimension_semantics=("arbitrary", "arbitrary")),
    )(p, cos, sin)


_NT = (((1,), (1,)), ((), ()))


def _softmax_pv(s, v):
    m = jnp.max(s, axis=-1, keepdims=True)
    e = jnp.exp(s - m)
    return jnp.dot(e.astype(BF16), v, preferred_element_type=F32) / jnp.sum(e, axis=-1, keepdims=True)


def _diff_attn_kernel(q_ref, k_ref, v_ref, lam_ref, g_ref, o_ref, *, ctx_blocks, ctx_len):
    i = pl.program_id(1)
    lv = lam_ref[...]
    lam_init = lv[4:5, 0:1]
    lam = (jnp.exp(jnp.sum(lv[0:1] * lv[1:2], axis=-1, keepdims=True))
           - jnp.exp(jnp.sum(lv[2:3] * lv[3:4], axis=-1, keepdims=True)) + lam_init)
    lane = lax.broadcasted_iota(jnp.int32, (q_ref.shape[1], HEAD_DIM), 1)

    def attend(n_keys):
        outs = []
        for h in range(GROUP_HEADS):
            qh = q_ref[0, :, h * HEAD_DIM:(h + 1) * HEAD_DIM]
            kh = k_ref[0, h, 0:n_keys, :]
            vh = v_ref[0, h, 0:n_keys, :]
            zero = jnp.zeros_like(qh)
            s1 = lax.dot_general(jnp.where(lane < DIFF_HALF, qh, zero), kh, _NT, preferred_element_type=F32)
            s2 = lax.dot_general(jnp.where(lane >= DIFF_HALF, qh, zero), kh, _NT, preferred_element_type=F32)
            o = _softmax_pv(s1, vh) - lam * _softmax_pv(s2, vh)
            o = o * lax.rsqrt(jnp.mean(o * o, axis=-1, keepdims=True) + 1e-6) * g_ref[...] * (1.0 - lam_init)
            outs.append(o)
        o_ref[0] = jnp.concatenate(outs, axis=-1)

    @pl.when(i < ctx_blocks)
    def _():
        attend(ctx_len)

    @pl.when(i >= ctx_blocks)
    def _():
        attend(k_ref.shape[2])


def _diff_attention_pallas(q, k, v, lam_vecs, norm_g, *, ctx_len, lam_init):
    bsz, seq, _ = q.shape
    tq = ROW_BLOCK
    kv_spec = pl.BlockSpec((1, GROUP_HEADS, seq, HEAD_DIM), lambda b, i: (b, 0, 0, 0))
    lam_rows = jnp.concatenate([lam_vecs.astype(F32), jnp.full((1, lam_vecs.shape[1]), lam_init, F32)], 0)
    return pl.pallas_call(
        functools.partial(_diff_attn_kernel, ctx_blocks=ctx_len // tq, ctx_len=ctx_len),
        name="diff_attn",
        grid=(bsz, seq // tq),
        in_specs=[pl.BlockSpec((1, tq, GW), lambda b, i: (b, i, 0)), kv_spec, kv_spec,
                  pl.BlockSpec(lam_rows.shape, lambda b, i: (0, 0)),
                  pl.BlockSpec((1, HEAD_DIM), lambda b, i: (0, 0))],
        out_specs=pl.BlockSpec((1, tq, GW), lambda b, i: (b, i, 0)),
        out_shape=jax.ShapeDtypeStruct((bsz, seq, GW), F32),
        compiler_params=pltpu.CompilerParams(dimension_semantics=("arbitrary", "arbitrary"),
                                             vmem_limit_bytes=V7X_VMEM_LIMIT_BYTES),
    )(q, k, v, lam_rows, norm_g.reshape(1, HEAD_DIM).astype(F32))


NAT_TILE_ROWS = ROW_BLOCK // GRID_W
NAT_SLAB_ROWS = NAT_TILE_ROWS + WIN_H - 1


def _nat_slab_start(tile, n_rows):
    return np.clip(tile * NAT_TILE_ROWS - WIN_H // 2, 0, n_rows - NAT_SLAB_ROWS)


def _nat_bias_tables(rpb, n_rows):
    n_tiles = n_rows // NAT_TILE_ROWS
    nq, nk, w = NAT_TILE_ROWS, NAT_SLAB_ROWS, GRID_W
    cq, ck = np.arange(w)[:, None], np.arange(w)[None, :]
    d_col = np.clip(ck - cq, -(WIN_W - 1), WIN_W - 1) + WIN_W - 1
    col_1h = (d_col.reshape(-1)[:, None] == np.arange(2 * WIN_W - 1)[None, :]).astype(np.float32)
    c0 = np.clip(cq - WIN_W // 2, 0, w - WIN_W)
    col_ok = (ck >= c0) & (ck < c0 + WIN_W)
    tabs = []
    for tile in (0, 1, n_tiles - 1):
        r = tile * nq + np.arange(nq)[:, None]
        kr = _nat_slab_start(tile, n_rows) + np.arange(nk)[None, :]
        rs = np.clip(r - WIN_H // 2, 0, n_rows - WIN_H)
        row_ok = (kr >= rs) & (kr < rs + WIN_H)
        d_row = np.clip(kr - r + WIN_H - 1, 0, 2 * WIN_H - 2)
        row_1h = (d_row.reshape(-1)[:, None] == np.arange(2 * WIN_H - 1)[None, :]).astype(np.float32)
        t = jnp.einsum('pa,hab,cb->hpc', row_1h, rpb.astype(F32), col_1h, precision=HIGHEST)
        t = t.reshape(GROUP_HEADS, nq, nk, w, w).transpose(0, 1, 3, 2, 4)
        ok = row_ok[:, None, :, None] & col_ok[None, :, None, :]
        tabs.append(jnp.where(ok[None], t, NEG_BIG).reshape(GROUP_HEADS, nq * w, nk * w))
    return jnp.stack(tabs)


def _nat_attn_kernel(q_ref, k_ref, v_ref, bias_ref, o_ref, *, ctx_blocks, ctx_len, n_rows):
    i = pl.program_id(1)
    n_slab = NAT_SLAB_ROWS * GRID_W

    def heads(fn):
        o_ref[0] = jnp.concatenate(
            [fn(h, q_ref[0, :, h * HEAD_DIM:(h + 1) * HEAD_DIM]) for h in range(GROUP_HEADS)], axis=-1)

    @pl.when(i < ctx_blocks)
    def _():
        def ctx_only(h, qh):
            s = lax.dot_general(qh, k_ref[0, h, 0:ctx_len, :], _NT, preferred_element_type=F32)
            return _softmax_pv(s, v_ref[0, h, 0:ctx_len, :])
        heads(ctx_only)

    @pl.when(i >= ctx_blocks)
    def _():
        tile = i - ctx_blocks
        start = jnp.clip(tile * NAT_TILE_ROWS - WIN_H // 2, 0, n_rows - NAT_SLAB_ROWS)
        off = pl.multiple_of(ctx_len + start * GRID_W, GRID_W)

        def windowed(h, qh):
            s_w = lax.dot_general(qh, k_ref[0, h, pl.ds(off, n_slab), :], _NT,
                                  preferred_element_type=F32) + bias_ref[0, h]
            s_c = lax.dot_general(qh, k_ref[0, h, 0:ctx_len, :], _NT, preferred_element_type=F32)
            m = jnp.maximum(jnp.max(s_w, axis=-1, keepdims=True), jnp.max(s_c, axis=-1, keepdims=True))
            e_w = jnp.exp(s_w - m)
            e_c = jnp.exp(s_c - m)
            den = jnp.sum(e_w, axis=-1, keepdims=True) + jnp.sum(e_c, axis=-1, keepdims=True)
            num = (jnp.dot(e_w.astype(BF16), v_ref[0, h, pl.ds(off, n_slab), :], preferred_element_type=F32)
                   + jnp.dot(e_c.astype(BF16), v_ref[0, h, 0:ctx_len, :], preferred_element_type=F32))
            return num / den
        heads(windowed)


def _nat_attention_pallas(q, k, v, rpb, *, ctx_len):
    bsz, seq, _ = q.shape
    tq = ROW_BLOCK
    ctx_blocks = ctx_len // tq
    n_rows = (seq - ctx_len) // GRID_W
    n_tiles = n_rows // NAT_TILE_ROWS
    assert n_rows >= NAT_SLAB_ROWS and n_tiles >= 3
    bias = _nat_bias_tables(rpb, n_rows)

    def variant(i):
        tile = i - ctx_blocks
        return jnp.where(tile <= 0, 0, jnp.where(tile >= n_tiles - 1, 2, 1))
    kv_spec = pl.BlockSpec((1, GROUP_HEADS, seq, HEAD_DIM), lambda b, i: (b, 0, 0, 0))
    return pl.pallas_call(
        functools.partial(_nat_attn_kernel, ctx_blocks=ctx_blocks, ctx_len=ctx_len, n_rows=n_rows),
        name="nat_attn",
        grid=(bsz, seq // tq),
        in_specs=[pl.BlockSpec((1, tq, GW), lambda b, i: (b, i, 0)), kv_spec, kv_spec,
                  pl.BlockSpec((1,) + bias.shape[1:], lambda b, i: (variant(i), 0, 0, 0))],
        out_specs=pl.BlockSpec((1, tq, GW), lambda b, i: (b, i, 0)),
        out_shape=jax.ShapeDtypeStruct((bsz, seq, GW), F32),
        compiler_params=pltpu.CompilerParams(dimension_semantics=("arbitrary", "arbitrary"),
                                             vmem_limit_bytes=V7X_VMEM_LIMIT_BYTES),
    )(q, k, v, bias)


N_MOD = 6
MATMUL_ROWS = 512


def _ada_kernel(c_ref, w_ref, b_ref, o_ref):
    c = c_ref[...]
    o_ref[...] = _dot_f32(c * jax.nn.sigmoid(c), w_ref[...]) + b_ref[...]


def _ada_modulation(c, c_ctx, w_ada, b_ada):
    bsz, d = c.shape
    rows = 8 * ((bsz + 1 + 7) // 8)
    cc = jnp.zeros((rows, d), F32).at[:bsz].set(c).at[bsz].set(c_ctx)
    tn = d
    m = pl.pallas_call(
        _ada_kernel,
        name="ada_modulation",
        grid=(w_ada.shape[1] // tn,),
        in_specs=[pl.BlockSpec((rows, d), lambda j: (0, 0)),
                  pl.BlockSpec((d, tn), lambda j: (0, j)),
                  pl.BlockSpec((1, tn), lambda j: (0, j))],
        out_specs=pl.BlockSpec((rows, tn), lambda j: (0, j)),
        out_shape=jax.ShapeDtypeStruct((rows, w_ada.shape[1]), F32),
        compiler_params=pltpu.CompilerParams(dimension_semantics=("arbitrary",)),
    )(cc, w_ada, b_ada.reshape(1, -1))
    lat = m[:bsz].reshape(bsz, 1, N_MOD, d)
    ctx = jnp.broadcast_to(m[bsz].reshape(1, 1, N_MOD, d), (bsz, 1, N_MOD, d))
    return jnp.concatenate([ctx, lat], axis=1)


def _ln(x):
    mu = jnp.mean(x, axis=-1, keepdims=True)
    xc = x - mu
    return xc * lax.rsqrt(jnp.mean(xc * xc, axis=-1, keepdims=True) + LN_EPS)


def _modulate_kernel(h_ref, mod_ref, o_ref, *, shift_row):
    shift = mod_ref[0, 0, shift_row:shift_row + 1, :]
    scale = mod_ref[0, 0, shift_row + 1:shift_row + 2, :]
    o_ref[0] = (_ln(h_ref[0]) * (1.0 + scale) + shift).astype(o_ref.dtype)


def _mod_spec(d, ctx_blocks):
    return pl.BlockSpec((1, 1, N_MOD, d), lambda b, i: (b, jnp.where(i < ctx_blocks, 0, 1), 0, 0))


def _modulate_pallas(hs, mod, shift_row, *, ctx_len):
    bsz, seq, d = hs.shape
    tr = ROW_BLOCK
    blk = pl.BlockSpec((1, tr, d), lambda b, i: (b, i, 0))
    return pl.pallas_call(
        functools.partial(_modulate_kernel, shift_row=shift_row),
        name="modulate",
        grid=(bsz, seq // tr),
        in_specs=[blk, _mod_spec(d, ctx_len // tr)],
        out_specs=blk,
        out_shape=jax.ShapeDtypeStruct((bsz, seq, d), BF16),
        compiler_params=pltpu.CompilerParams(dimension_semantics=("arbitrary", "arbitrary")),
    )(hs, mod)


def _matmul_kernel(x_ref, w_ref, o_ref):
    o_ref[...] = jnp.dot(x_ref[...], w_ref[...], preferred_element_type=F32)


def _matmul_pallas(x, w):
    m, k = x.shape
    n = w.shape[1]
    tm = MATMUL_ROWS
    return pl.pallas_call(
        _matmul_kernel,
        name="in_proj",
        grid=(m // tm,),
        in_specs=[pl.BlockSpec((tm, k), lambda i: (i, 0)), pl.BlockSpec((k, n), lambda i: (0, 0))],
        out_specs=pl.BlockSpec((tm, n), lambda i: (i, 0)),
        out_shape=jax.ShapeDtypeStruct((m, n), F32),
        compiler_params=pltpu.CompilerParams(dimension_semantics=("arbitrary",),
                                             vmem_limit_bytes=V7X_VMEM_LIMIT_BYTES),
    )(x, w)


def _post_norm_rows(h, gate, y, g, b):
    return _ln(DN_ALPHA * h + gate * y) * g + b


def _out_proj_kernel(ya_ref, yb_ref, yc_ref, yd_ref, w_ref, h_ref, mod_ref, g_ref, b_ref, o_ref):
    mix = None
    for n, y_ref in enumerate((ya_ref, yb_ref, yc_ref, yd_ref)):
        part = jnp.dot(y_ref[0].astype(BF16), w_ref[n * GW:(n + 1) * GW, :], preferred_element_type=F32)
        mix = part if mix is None else mix + part
    o_ref[0] = _post_norm_rows(h_ref[0], mod_ref[0, 0, 2:3, :], mix, g_ref[...], b_ref[...])


def _out_proj_post_norm(ys, w_out, hs, mod, g, b, *, ctx_len):
    bsz, seq, d = hs.shape
    tr = ROW_BLOCK
    yblk = pl.BlockSpec((1, tr, GW), lambda bb, i: (bb, i, 0))
    blk = pl.BlockSpec((1, tr, d), lambda bb, i: (bb, i, 0))
    row = pl.BlockSpec((1, d), lambda bb, i: (0, 0))
    return pl.pallas_call(
        _out_proj_kernel,
        name="out_proj_post_norm",
        grid=(bsz, seq // tr),
        in_specs=[yblk] * 4 + [pl.BlockSpec(w_out.shape, lambda bb, i: (0, 0)), blk, _mod_spec(d, ctx_len // tr), row, row],
        out_specs=blk,
        out_shape=jax.ShapeDtypeStruct((bsz, seq, d), F32),
        compiler_params=pltpu.CompilerParams(dimension_semantics=("arbitrary", "arbitrary")),
    )(*ys, w_out, hs, mod, g.reshape(1, d), b.reshape(1, d))


def _ffn_post_norm_kernel(h_ref, y_ref, mod_ref, g_ref, b_ref, o_ref):
    o_ref[0] = _post_norm_rows(h_ref[0], mod_ref[0, 0, 5:6, :], y_ref[0], g_ref[...], b_ref[...])


def _ffn_post_norm(hs, y, mod, g, b, *, ctx_len):
    bsz, seq, d = hs.shape
    tr = ROW_BLOCK
    blk = pl.BlockSpec((1, tr, d), lambda bb, i: (bb, i, 0))
    row = pl.BlockSpec((1, d), lambda bb, i: (0, 0))
    return pl.pallas_call(
        _ffn_post_norm_kernel,
        name="ffn_post_norm",
        grid=(bsz, seq // tr),
        in_specs=[blk, blk, _mod_spec(d, ctx_len // tr), row, row],
        out_specs=blk,
        out_shape=jax.ShapeDtypeStruct((bsz, seq, d), F32),
        compiler_params=pltpu.CompilerParams(dimension_semantics=("arbitrary", "arbitrary")),
    )(hs, y, mod, g.reshape(1, d), b.reshape(1, d))


def kernel(x, c, ctx, c_ctx, w_ada, b_ada, w_in, w_out, ln_mix_g, ln_mix_b, ln_ffn_g, ln_ffn_b, diff_lam, diff_norm_g, rwkv_mu, rwkv_w0, rwkv_w2, rwkv_a0, rwkv_a2, rwkv_g2, rwkv_kk, rwkv_ka, rwkv_rk, rwkv_ln_g, rwkv_ln_b, gdn_conv, gdn_a_log, gdn_dt_bias, gdn_norm_g, nat_rpb, peer_wq, peer_keys, peer_u, peer_v):
    dtype = x.dtype
    bsz, ctx_len = ctx.shape[0], ctx.shape[1]
    hs = jnp.concatenate([ctx, x], axis=1)
    seq = hs.shape[1]
    col_sizes = [ATTN_COLS, RWKV_COLS, GDN_COLS, ATTN_COLS]
    cos_a, sin_a = _rope_tables(seq, ctx_len, DIFF_HALF ** -0.5)
    cos_d = jnp.concatenate([jnp.full((seq, GW), HEAD_DIM ** -0.5, F32), jnp.ones((seq, GW), F32)], 1)
    sin_d = jnp.zeros_like(cos_d)
    col_offs = np.cumsum([0] + col_sizes)
    d_model = hs.shape[2]
    for l in range(DEPTH):
        lam_init = 0.8 - 0.6 * math.exp(-0.3 * l)
        mod = _ada_modulation(c, c_ctx, w_ada[l], b_ada[l])
        u = _modulate_pallas(hs, mod, 0, ctx_len=ctx_len).reshape(bsz * seq, d_model)
        w_in_b = w_in[l].astype(BF16)
        w_groups = [w_in_b[:, col_offs[n]:col_offs[n + 1]] for n in range(4)]
        w_groups[2] = jnp.pad(w_groups[2], ((0, 0), (0, GDN_PADDED_COLS - GDN_COLS)))
        pa, pb, pc, pd = [_matmul_pallas(u, w).reshape(bsz, seq, w.shape[1]) for w in w_groups]
        qa, ka, va = _qkv_prep(pa, cos_a, sin_a)
        ya = _diff_attention_pallas(qa, ka, va, diff_lam[l], diff_norm_g[l], ctx_len=ctx_len, lam_init=lam_init)
        yb = _rwkv7_pallas(pb, rwkv_mu[l], rwkv_w0[l], rwkv_w2[l], rwkv_a0[l], rwkv_a2[l], rwkv_g2[l],
                           rwkv_kk[l], rwkv_ka[l], rwkv_rk[l], rwkv_ln_g[l], rwkv_ln_b[l], ctx_len=ctx_len)
        yc = _gated_deltanet_pallas(pc, gdn_conv[l], gdn_a_log[l], gdn_dt_bias[l], gdn_norm_g[l], ctx_len=ctx_len)
        qd, kd, vd = _qkv_prep(pd, cos_d, sin_d)
        yd = _nat_attention_pallas(qd, kd, vd, nat_rpb[l], ctx_len=ctx_len)
        hs = _out_proj_post_norm([ya, yb, yc, yd], w_out[l].astype(BF16), hs, mod, ln_mix_g[l], ln_mix_b[l],
                                 ctx_len=ctx_len)
        wq_b = peer_wq[l].astype(BF16)
        keys_b = peer_keys[l].reshape(2 * PEER_HEADS, PEER_KEYS, PEER_HALF).astype(BF16)
        u_b = peer_u[l].astype(BF16)
        vt_b = peer_v[l].astype(BF16).T
        ffn = _peer(_modulate_pallas(hs, mod, 3, ctx_len=ctx_len), wq_b, keys_b, u_b, vt_b)
        hs = _ffn_post_norm(hs, ffn, mod, ln_ffn_g[l], ln_ffn_b[l], ctx_len=ctx_len)
    return hs[:, ctx_len:].astype(dtype)
```

```python
import functools
import math

import jax
import jax.numpy as jnp
import numpy as np
from jax import lax
from jax.experimental import pallas as pl
from jax.experimental.pallas import tpu as pltpu

D_MODEL = 1024
DEPTH = 2
GRID_W = 64
HEAD_DIM = 64
N_GROUPS = 4
GROUP_HEADS = D_MODEL // (N_GROUPS * HEAD_DIM)
GW = GROUP_HEADS * HEAD_DIM
D_MIX = N_GROUPS * GW
DIFF_HALF = HEAD_DIM // 2
Q_BLOCK = 128
ROPE_BASE = 10000.0
DECAY_LORA = 64
ICLR_LORA = 64
GATE_LORA = 128
RWKV_GN_EPS = 64e-5
RWKV_COLS = 3 * GW + 2 * DECAY_LORA + 2 * ICLR_LORA + GATE_LORA
GDN_CONV = 3
GDN_CHUNK = 64
GDN_COLS = 4 * GW + 4 * GROUP_HEADS
WIN_H = 8
WIN_W = 16
ATTN_COLS = 3 * GW
IN_COLS = ATTN_COLS + RWKV_COLS + GDN_COLS + ATTN_COLS
PEER_HEADS = 8
PEER_KEYS = 128
PEER_EXPERTS = PEER_KEYS * PEER_KEYS
PEER_QDIM = 256
PEER_HALF = PEER_QDIM // 2
PEER_TOPK = 16
DN_ALPHA = (2 * DEPTH) ** 0.25
LN_EPS = 1e-5

F32 = jnp.float32
BF16 = jnp.bfloat16

V7X_VMEM_LIMIT_BYTES = 56 * 1024 * 1024
NEG_BIG = -3.0e38


def _split(x, sizes):
    offs = np.cumsum(sizes)[:-1].tolist()
    return jnp.split(x, offs, axis=-1)


def _heads(x):
    return x.reshape(x.shape[0], x.shape[1], -1, HEAD_DIM)


def _layer_norm(x):
    xf = x.astype(F32)
    mu = xf.mean(-1, keepdims=True)
    var = jnp.square(xf - mu).mean(-1, keepdims=True)
    return (xf - mu) * lax.rsqrt(var + LN_EPS)


def _rms_norm(x, g):
    xf = x.astype(F32)
    return xf * lax.rsqrt(jnp.mean(xf * xf, -1, keepdims=True) + 1e-6) * g.astype(F32)


def _l2norm(x):
    xf = x.astype(F32)
    return xf * lax.rsqrt(jnp.sum(xf * xf, -1, keepdims=True) + 1e-6)


def _modulate(h, shift, scale, dtype):
    return (_layer_norm(h) * (1.0 + scale) + shift).astype(dtype)


def _post_norm(h, gate, y, g, b, dtype):
    z = DN_ALPHA * h.astype(F32) + gate.astype(F32) * y.astype(F32)
    return (_layer_norm(z) * g + b).astype(dtype)


def _dwconv_centred(x, w):
    k = w.shape[0]
    pad = k // 2
    t = x.shape[1]
    xp = jnp.pad(x, ((0, 0), (pad, pad), (0, 0)))
    out = xp[:, 0:t] * w[0]
    for i in range(1, k):
        out = out + xp[:, i:i + t] * w[i]
    return out


def _grid_pos(n_tok):
    t = jnp.arange(n_tok, dtype=jnp.int32)
    return t // GRID_W, t % GRID_W


def _rope_1d(x, pos):
    half = x.shape[-1] // 2
    inv = ROPE_BASE ** (-jnp.arange(half, dtype=F32) / half)
    ang = pos.astype(F32)[:, None] * inv
    cos = jnp.cos(ang)[:, None, :]
    sin = jnp.sin(ang)[:, None, :]
    x1 = x[..., :half].astype(F32)
    x2 = x[..., half:].astype(F32)
    return jnp.concatenate([x1 * cos - x2 * sin, x1 * sin + x2 * cos], -1)


def _rope_2d(x, row, col):
    n = x.shape[-1] // 2
    return jnp.concatenate([_rope_1d(x[..., :n], row), _rope_1d(x[..., n:], col)], -1)


def _diff_core(q, k, v, lam):
    scale = DIFF_HALF ** -0.5
    q = q.astype(F32)
    k = k.astype(F32)
    s1 = jnp.einsum('bqhd,bkhd->bhqk', q[..., :DIFF_HALF], k[..., :DIFF_HALF]) * scale
    s2 = jnp.einsum('bqhd,bkhd->bhqk', q[..., DIFF_HALF:], k[..., DIFF_HALF:]) * scale
    p = jax.nn.softmax(s1, -1) - lam * jax.nn.softmax(s2, -1)
    return jnp.einsum('bhqk,bkhd->bqhd', p, v.astype(F32))


def _diff_attention(q, k, v, qc, kc, vc, lam_vecs, norm_g, lam_init, ctx_out):
    bsz, lat_len = q.shape[:2]
    row, col = _grid_pos(lat_len)
    rot = lambda z: jnp.concatenate([_rope_2d(z[..., :DIFF_HALF], row, col),
                                     _rope_2d(z[..., DIFF_HALF:], row, col)], -1)
    q = rot(q)
    k = rot(k)
    lv = lam_vecs.astype(F32)
    lam = jnp.exp(jnp.sum(lv[0] * lv[1])) - jnp.exp(jnp.sum(lv[2] * lv[3])) + lam_init
    k_all = jnp.concatenate([k, kc.astype(F32)], 1)
    v_all = jnp.concatenate([v.astype(F32), vc.astype(F32)], 1)
    nb = lat_len // Q_BLOCK
    qb = jnp.moveaxis(q.reshape(bsz, nb, Q_BLOCK, GROUP_HEADS, HEAD_DIM), 1, 0)
    ob = lax.map(lambda qq: _diff_core(qq, k_all, v_all, lam), qb)
    o = jnp.moveaxis(ob, 0, 1).reshape(bsz, lat_len, GROUP_HEADS, HEAD_DIM)
    post = lambda z: (_rms_norm(z, norm_g) * (1.0 - lam_init)).reshape(z.shape[0], z.shape[1], GW)
    out_ctx = post(_diff_core(qc, kc, vc, lam)) if ctx_out else None
    return post(o), out_ctx


def _wkv7_scan(r, w, k, v, kk, a, s0, reverse):
    def step(s, inp):
        r_t, w_t, k_t, v_t, kk_t, a_t = inp
        sa = jnp.einsum('bhvk,bhk->bhv', s, kk_t)
        s = (s * w_t[:, :, None, :] - sa[..., None] * (kk_t * a_t)[:, :, None, :]
             + v_t[..., None] * k_t[:, :, None, :])
        return s, jnp.einsum('bhvk,bhk->bhv', s, r_t)
    xs = tuple(jnp.moveaxis(z, 1, 0) for z in (r, w, k, v, kk, a))
    s, y = lax.scan(step, s0, xs, reverse=reverse)
    return jnp.moveaxis(y, 0, 1), s


def _rwkv7(f_lat, f_ctx, mu, w0, w2, a0, a2, g2, k_k, k_a, r_k, ln_g, ln_b, ctx_out):
    shift_w = jnp.stack([mu[0], 1.0 - mu[0] - mu[1], mu[1]])

    def prep(f):
        f = _dwconv_centred(f, shift_w).astype(F32)
        r, k, v, wd, ad, gd = _split(f, [GW, GW, GW, 2 * DECAY_LORA, 2 * ICLR_LORA, GATE_LORA])
        bsz, t = f.shape[:2]
        wd = jnp.tanh(wd.reshape(bsz, t, 2, DECAY_LORA))
        ad = ad.reshape(bsz, t, 2, ICLR_LORA)
        w_raw = w0 + jnp.einsum('btdr,drc->btdc', wd, w2)
        decay = jnp.exp(-jnp.exp(-jax.nn.softplus(-w_raw) - 0.5))
        a = jax.nn.sigmoid(a0 + jnp.einsum('btdr,drc->btdc', ad, a2))
        g = jax.nn.sigmoid(gd) @ g2
        kk = _l2norm(_heads(k * k_k))
        kd = k[:, :, None] * (1.0 + (a - 1.0) * k_a)
        return r, v, g, kk, decay, a, kd

    def run(p, d, s0, rev):
        r, v, g, kk, decay, a, kd = p
        return _wkv7_scan(_heads(r), _heads(decay[:, :, d]), _heads(kd[:, :, d]), _heads(v),
                          kk, _heads(a[:, :, d]), s0, rev)

    def post(p, ys):
        r, v, g, kk, decay, a, kd = p
        bsz, t = r.shape[:2]
        y = ys[0] + ys[1]
        m = y.mean(-1, keepdims=True)
        var = jnp.square(y - m).mean(-1, keepdims=True)
        yn = ((y - m) * lax.rsqrt(var + RWKV_GN_EPS)).reshape(bsz, t, GW) * ln_g + ln_b
        rh, vh = _heads(r), _heads(v)
        bonus = ((rh * _heads(kd[:, :, 0]) * r_k).sum(-1, keepdims=True) * vh
                 + (rh * _heads(kd[:, :, 1]) * r_k).sum(-1, keepdims=True) * vh)
        return (yn + bonus.reshape(bsz, t, GW)) * g

    pl_, pc = prep(f_lat), prep(f_ctx)
    s0 = jnp.zeros((f_lat.shape[0], GROUP_HEADS, HEAD_DIM, HEAD_DIM), F32)
    y_lat, y_ctx = [], []
    for d, rev in ((0, False), (1, True)):
        yc, sc = run(pc, d, s0, rev)
        yl, _ = run(pl_, d, sc, rev)
        y_lat.append(yl)
        y_ctx.append(yc)
    out_ctx = post(pc, y_ctx) if ctx_out else None
    return post(pl_, y_lat), out_ctx


def _gdn_chunked(q, k, v, beta, g, s0):
    bsz, t, h, dk = q.shape
    dv = v.shape[-1]
    n = t // GDN_CHUNK
    ch = lambda z: jnp.moveaxis(z.reshape(bsz, n, GDN_CHUNK, h, *z.shape[3:]), 3, 2)
    q, k, v, beta, g = ch(q), ch(k), ch(v), ch(beta), ch(g)
    gc = jnp.cumsum(g, axis=-1)
    i = jnp.arange(GDN_CHUNK)
    incl = i[:, None] >= i[None, :]
    strict = i[:, None] > i[None, :]
    decay = jnp.exp(jnp.where(incl, gc[..., :, None] - gc[..., None, :], -jnp.inf))
    kb = k * beta[..., None]
    a_low = jnp.where(strict, jnp.einsum('bnhid,bnhjd->bnhij', kb, k) * decay, 0.0)
    tmat = a_low + jnp.eye(GDN_CHUNK, dtype=F32)
    rhs = jnp.concatenate([v * beta[..., None], kb * jnp.exp(gc)[..., None]], -1)
    sol = lax.linalg.triangular_solve(tmat, rhs, left_side=True, lower=True, unit_diagonal=True)
    u, w = sol[..., :dv], sol[..., dv:]
    qk = jnp.where(incl, jnp.einsum('bnhid,bnhjd->bnhij', q, k) * decay, 0.0)
    qg = q * jnp.exp(gc)[..., None]
    kg = k * jnp.exp(gc[..., -1:] - gc)[..., None]
    glast = jnp.exp(gc[..., -1])

    def step(s, xs):
        qg_i, kg_i, u_i, w_i, qk_i, gl_i = xs
        v_new = u_i - jnp.einsum('bhcd,bhdv->bhcv', w_i, s)
        o = jnp.einsum('bhcd,bhdv->bhcv', qg_i, s) + jnp.einsum('bhij,bhjv->bhiv', qk_i, v_new)
        s = s * gl_i[..., None, None] + jnp.einsum('bhcd,bhcv->bhdv', kg_i, v_new)
        return s, o

    xs = tuple(jnp.moveaxis(z, 1, 0) for z in (qg, kg, u, w, qk, glast))
    s, o = lax.scan(step, s0, xs)
    o = jnp.moveaxis(jnp.moveaxis(o, 0, 1), 2, 3).reshape(bsz, t, h, dv)
    return o, s


def _gated_deltanet(f_lat, f_ctx, conv_w, a_log, dt_bias, norm_g, ctx_out):
    def prep(f):
        qkv, gate, ab = _split(f, [3 * GW, GW, 4 * GROUP_HEADS])
        qkv = jax.nn.silu(_dwconv_centred(qkv, conv_w).astype(F32))
        q, k, v = [_heads(z) for z in _split(qkv, [GW, GW, GW])]
        q = _l2norm(q) * HEAD_DIM ** -0.5
        k = _l2norm(k)
        ab = ab.astype(F32).reshape(f.shape[0], f.shape[1], 2, 2, GROUP_HEADS)
        log_alpha = -jnp.exp(a_log) * jax.nn.softplus(ab[:, :, 0] + dt_bias)
        beta = jax.nn.sigmoid(ab[:, :, 1])
        return q, k, v, gate, log_alpha, beta

    def run(p, d, s0):
        tr = (lambda z: jnp.flip(z, 1)) if d == 1 else (lambda z: z)
        q, k, v, gate, log_alpha, beta = p
        o, s = _gdn_chunked(tr(q), tr(k), tr(v), tr(beta[:, :, d]), tr(log_alpha[:, :, d]), s0)
        return tr(o), s

    def post(p, os_):
        gate = p[3]
        o = os_[0] + os_[1]
        y = _rms_norm(o, norm_g) * jax.nn.silu(_heads(gate.astype(F32)))
        return y.reshape(o.shape[0], o.shape[1], GW)

    pl_, pc = prep(f_lat), prep(f_ctx)
    s0 = jnp.zeros((f_lat.shape[0], GROUP_HEADS, HEAD_DIM, HEAD_DIM), F32)
    o_lat, o_ctx = [], []
    for d in range(2):
        oc, sc = run(pc, d, s0)
        ol, _ = run(pl_, d, sc)
        o_lat.append(ol)
        o_ctx.append(oc)
    out_ctx = post(pc, o_ctx) if ctx_out else None
    return post(pl_, o_lat), out_ctx


def _softmax_attn(q, k, v):
    s = jnp.einsum('bqhd,bkhd->bhqk', q.astype(F32), k.astype(F32)) * HEAD_DIM ** -0.5
    return jnp.einsum('bhqk,bkhd->bqhd', jax.nn.softmax(s, -1), v.astype(F32))


def _neighbourhood_attention(q, k, v, qc, kc, vc, rpb, ctx_out):
    bsz, lat_len = q.shape[:2]
    rows = lat_len // GRID_W
    kh = min(WIN_H, rows)
    scale = HEAD_DIM ** -0.5
    grid = lambda z: z.astype(F32).reshape(bsz, rows, GRID_W, GROUP_HEADS, HEAD_DIM)
    qg, kg, vg = grid(q), grid(k), grid(v)
    kc32, vc32 = kc.astype(F32), vc.astype(F32)
    rpb = rpb.astype(F32)
    cq = jnp.arange(GRID_W)
    c_start = jnp.clip(cq - WIN_W // 2, 0, GRID_W - WIN_W)
    col_ok = (cq[None, :] >= c_start[:, None]) & (cq[None, :] < c_start[:, None] + WIN_W)
    d_col = jnp.clip(cq[None, :] - cq[:, None], -(WIN_W - 1), WIN_W - 1) + (WIN_W - 1)

    def row_block(r):
        rs = jnp.clip(r - kh // 2, 0, rows - kh)
        q_r = lax.dynamic_index_in_dim(qg, r, axis=1, keepdims=False)
        k_r = lax.dynamic_slice_in_dim(kg, rs, kh, axis=1)
        v_r = lax.dynamic_slice_in_dim(vg, rs, kh, axis=1)
        s = jnp.einsum('bqhd,bkwhd->bqhkw', q_r, k_r) * scale
        d_row = rs + jnp.arange(kh) - r + (WIN_H - 1)
        bias = rpb[:, d_row[None, :, None], d_col[:, None, :]]
        s = jnp.where(col_ok[:, None, None, :], s + jnp.transpose(bias, (1, 0, 2, 3)), -jnp.inf)
        s_c = jnp.einsum('bqhd,bchd->bqhc', q_r, kc32) * scale
        p = jax.nn.softmax(jnp.concatenate(
            [s.reshape(bsz, GRID_W, GROUP_HEADS, kh * GRID_W), s_c], -1), -1)
        p_win = p[..., :kh * GRID_W].reshape(bsz, GRID_W, GROUP_HEADS, kh, GRID_W)
        return (jnp.einsum('bqhkw,bkwhd->bqhd', p_win, v_r)
                + jnp.einsum('bqhc,bchd->bqhd', p[..., kh * GRID_W:], vc32))

    o = lax.map(row_block, jnp.arange(rows))
    out_lat = jnp.moveaxis(o, 0, 1).reshape(bsz, lat_len, GW)
    out_ctx = _softmax_attn(qc, kc, vc).reshape(bsz, qc.shape[1], GW) if ctx_out else None
    return out_lat, out_ctx


def _qkv_heads(p):
    return [_heads(z) for z in _split(p, [GW, GW, GW])]


PEER_STAT_ROWS = 4 * PEER_HEADS


def _topk_rows(x, k):
    rows = []
    cur = x
    for i in range(k):
        m = jnp.max(cur, axis=0, keepdims=True)
        rows.append(m)
        if i + 1 < k:
            cur = jnp.where(cur == m, NEG_BIG, cur)
    return rows


def _peer_score_kernel(x_ref, wq_ref, keys_ref, s_ref, st_ref):
    q = jnp.dot(x_ref[...], wq_ref[...], preferred_element_type=F32).astype(BF16)
    stats = []
    for h in range(PEER_HEADS):
        tops = []
        for p in range(2):
            hp = 2 * h + p
            s_t = lax.dot_general(keys_ref[hp], q[:, hp * PEER_HALF:(hp + 1) * PEER_HALF],
                                  (((1,), (1,)), ((), ())), preferred_element_type=F32)
            s_ref[hp] = s_t
            tops.append(_topk_rows(s_t, PEER_TOPK + 1))
        top_b = jnp.concatenate(tops[1][:PEER_TOPK], axis=0)
        cand = jnp.concatenate([tops[0][i] + top_b for i in range(PEER_TOPK)], axis=0)
        best_cand = _topk_rows(cand, PEER_TOPK + 1)
        kth = best_cand[PEER_TOPK - 1]
        runner_up = jnp.maximum(best_cand[PEER_TOPK], jnp.maximum(tops[0][PEER_TOPK] + tops[1][0],
                                                                  tops[0][0] + tops[1][PEER_TOPK]))
        best = tops[0][0] + tops[1][0]
        z = jnp.sum(jnp.where(cand >= kth, jnp.exp(cand - best), 0.0), axis=0, keepdims=True)
        stats += [0.5 * (kth + runner_up), tops[0][0], tops[1][0], 1.0 / z]
    st_ref[...] = jnp.concatenate(stats, axis=0)


def _peer_expert_kernel(x_ref, s_ref, st_ref, u_ref, vt_ref, o_ref, e_ref, thr_ref, acc_ref, w_ref,
                        *, rows_per_step, n_chunks):
    j = pl.program_id(1)
    cur = lax.rem(j, 2)

    @pl.when(j == 0)
    def _():
        acc_ref[...] = jnp.zeros_like(acc_ref)
        w_ref[1] = jnp.zeros(w_ref.shape[1:], w_ref.dtype)
        for h in range(PEER_HEADS):
            a0 = st_ref[4 * h + 1:4 * h + 2, :]
            b0 = st_ref[4 * h + 2:4 * h + 3, :]
            rz = st_ref[4 * h + 3:4 * h + 4, :]
            e_ref[2 * h] = jnp.exp(s_ref[2 * h] - a0) * rz
            e_ref[2 * h + 1] = jnp.exp(s_ref[2 * h + 1] - b0)
            thr_ref[h] = st_ref[4 * h:4 * h + 1, :] - s_ref[2 * h]

    @pl.when(j < n_chunks)
    def _():
        tb = x_ref.shape[0]
        halves = [slice(n * (tb // 2), (n + 1) * (tb // 2)) for n in range(2)]
        acts = [lax.dot_general(u_ref[...], x_ref[lanes, :], (((1,), (1,)), ((), ())),
                                preferred_element_type=F32) for lanes in halves]
        acc_ref[...] += jnp.dot(vt_ref[...], w_ref[1 - cur], preferred_element_type=F32)
        for lanes, act_all in zip(halves, acts):
            for il in range(rows_per_step):
                i = j * rows_per_step + il
                act = act_all[il * PEER_KEYS:(il + 1) * PEER_KEYS]
                act = 0.5 * act * (1.0 + lax.erf(act * (2.0 ** -0.5)))
                gate = jnp.zeros_like(act)
                for h in range(PEER_HEADS):
                    chosen = s_ref[2 * h + 1, :, lanes] >= thr_ref[h, pl.ds(i, 1), lanes]
                    gate = gate + jnp.where(chosen, e_ref[2 * h, pl.ds(i, 1), lanes] * e_ref[2 * h + 1, :, lanes], 0.0)
                w_ref[cur, il * PEER_KEYS:(il + 1) * PEER_KEYS, lanes] = (act * gate).astype(BF16)

    @pl.when(j == n_chunks)
    def _():
        o_ref[...] = (acc_ref[...] + jnp.dot(vt_ref[...], w_ref[1 - cur], preferred_element_type=F32)).T


def _peer_pallas(h, w_q, keys, u_tab, vt_tab, *, tb, rows_per_step):
    n_tok, d = h.shape
    nhp = 2 * PEER_HEADS
    s, st = pl.pallas_call(
        _peer_score_kernel,
        name="peer_scores",
        grid=(n_tok // tb,),
        in_specs=[pl.BlockSpec((tb, d), lambda i: (i, 0)),
                  pl.BlockSpec(w_q.shape, lambda i: (0, 0)),
                  pl.BlockSpec(keys.shape, lambda i: (0, 0, 0))],
        out_specs=[pl.BlockSpec((nhp, PEER_KEYS, tb), lambda i: (0, 0, i)),
                   pl.BlockSpec((PEER_STAT_ROWS, tb), lambda i: (0, i))],
        out_shape=[jax.ShapeDtypeStruct((nhp, PEER_KEYS, n_tok), F32),
                   jax.ShapeDtypeStruct((PEER_STAT_ROWS, n_tok), F32)],
        compiler_params=pltpu.CompilerParams(dimension_semantics=("arbitrary",),
                                             vmem_limit_bytes=V7X_VMEM_LIMIT_BYTES),
    )(h, w_q, keys)
    ec = rows_per_step * PEER_KEYS
    n_chunks = PEER_EXPERTS // ec
    return pl.pallas_call(
        functools.partial(_peer_expert_kernel, rows_per_step=rows_per_step, n_chunks=n_chunks),
        name="peer_experts",
        grid=(n_tok // tb, n_chunks + 1),
        in_specs=[pl.BlockSpec((tb, d), lambda i, j: (i, 0)),
                  pl.BlockSpec((nhp, PEER_KEYS, tb), lambda i, j: (0, 0, i)),
                  pl.BlockSpec((PEER_STAT_ROWS, tb), lambda i, j: (0, i)),
                  pl.BlockSpec((ec, d), lambda i, j: (jnp.minimum(j, n_chunks - 1), 0)),
                  pl.BlockSpec((d, ec), lambda i, j: (0, jnp.maximum(j - 1, 0)))],
        out_specs=pl.BlockSpec((tb, d), lambda i, j: (i, 0)),
        out_shape=jax.ShapeDtypeStruct((n_tok, d), F32),
        scratch_shapes=[pltpu.VMEM((nhp, PEER_KEYS, tb), F32),
                        pltpu.VMEM((PEER_HEADS, PEER_KEYS, tb), F32),
                        pltpu.VMEM((d, tb), F32),
                        pltpu.VMEM((2, ec, tb), BF16)],
        compiler_params=pltpu.CompilerParams(dimension_semantics=("arbitrary", "arbitrary"),
                                             vmem_limit_bytes=V7X_VMEM_LIMIT_BYTES),
    )(h, s, st, u_tab, vt_tab)


def _peer(h, w_q, keys, u_tab, vt_tab):
    bsz, t, d = h.shape
    n_tok = bsz * t
    tb = 512 if n_tok % 512 == 0 else 256
    out = _peer_pallas(h.reshape(n_tok, d).astype(BF16), w_q, keys, u_tab, vt_tab, tb=tb, rows_per_step=8)
    return out.reshape(bsz, t, d)


ROW_BLOCK = 256
HIGHEST = lax.Precision.HIGHEST


def _dot_f32(a, b):
    return jnp.dot(a, b, precision=HIGHEST, preferred_element_type=F32)


def _segment_ones(n, seg, dtype):
    r = lax.broadcasted_iota(jnp.int32, (n, n), 0) // seg
    c = lax.broadcasted_iota(jnp.int32, (n, n), 1) // seg
    return jnp.where(r == c, 1.0, 0.0).astype(dtype)


def _shifted_rows(x, prev_row, next_row):
    t = x.shape[0]
    rows = lax.broadcasted_iota(jnp.int32, x.shape, 0)
    xm = jnp.where(rows == 0, prev_row, pltpu.roll(x, 1, axis=0))
    xp = jnp.where(rows == t - 1, next_row, pltpu.roll(x, t - 1, axis=0))
    return xm, xp


def _segment_edge_flags(i, n_blocks, ctx_blocks):
    is_start = jnp.logical_or(i == 0, i == ctx_blocks)
    is_end = jnp.logical_or(i == ctx_blocks - 1, i == n_blocks - 1)
    return jnp.where(is_start, 0.0, 1.0), jnp.where(is_end, 0.0, 1.0)


def _halo_specs(width, tr):
    g = tr // 8
    prev = pl.BlockSpec((1, 8, width), lambda b, i: (b, jnp.maximum(i * g - 1, 0), 0))
    nxt = lambda n_groups: pl.BlockSpec((1, 8, width), lambda b, i: (b, jnp.minimum((i + 1) * g, n_groups - 1), 0))
    return prev, nxt


def _softplus(z):
    return jnp.maximum(z, 0.0) + jnp.log1p(jnp.exp(-jnp.abs(z)))


def _rwkv_prep_kernel(x_ref, xprev_ref, xnext_ref, mu_ref, w0_ref, w2_ref, a0_ref, a2_ref, g2_ref,
                      kk_ref, ka_ref, rk_ref,
                      r_out, v_out, kkn_out, g_out, bonus_out, w_out, b_out, kt_out, *, ctx_blocks):
    i = pl.program_id(1)
    keep_prev, keep_next = _segment_edge_flags(i, pl.num_programs(1), ctx_blocks)
    x = x_ref[0]
    xm, xp = _shifted_rows(x, xprev_ref[0, 7:8, :] * keep_prev, xnext_ref[0, 0:1, :] * keep_next)
    mu0 = mu_ref[0:1, :]
    mu1 = mu_ref[1:2, :]
    f = xm * mu0 + x * (1.0 - mu0 - mu1) + xp * mu1
    r = f[:, 0:GW]
    k = f[:, GW:2 * GW]
    v = f[:, 2 * GW:3 * GW]
    o = 3 * GW
    wd = jnp.tanh(f[:, o:o + 2 * DECAY_LORA])
    ad = f[:, o + 2 * DECAY_LORA:o + 2 * DECAY_LORA + 2 * ICLR_LORA]
    gd = f[:, o + 2 * DECAY_LORA + 2 * ICLR_LORA:]
    w_raw = w0_ref[...] + _dot_f32(wd, w2_ref[...])
    decay = jnp.exp(-jnp.exp(-_softplus(-w_raw) - 0.5))
    a = jax.nn.sigmoid(a0_ref[...] + _dot_f32(ad, a2_ref[...]))
    g = _dot_f32(jax.nn.sigmoid(gd), g2_ref[...])
    head_sum = _segment_ones(GW, HEAD_DIM, F32)
    kx = k * kk_ref[...]
    kkn = kx * lax.rsqrt(_dot_f32(kx * kx, head_sum) + 1e-6)
    kd_sum = jnp.zeros_like(k)
    for d in range(2):
        a_d = a[:, d * GW:(d + 1) * GW]
        kd = k * (1.0 + (a_d - 1.0) * ka_ref[...])
        kd_sum = kd_sum + kd
        w_out[d, 0] = decay[:, d * GW:(d + 1) * GW]
        b_out[d, 0] = kkn * a_d
        kt_out[d, 0] = kd
    r_out[0] = r
    v_out[0] = v
    kkn_out[0] = kkn
    g_out[0] = g
    bonus_out[0] = _dot_f32(r * kd_sum * rk_ref[...], head_sum) * v


def _rwkv_scan_kernel(rf_ref, vf_ref, kkf_ref, wf_ref, bf_ref, ktf_ref,
                      rb_ref, vb_ref, kkb_ref, wb_ref, bb_ref, ktb_ref,
                      yf_ref, yb_ref, st_ref):
    c = pl.program_id(0)
    bsz, tt, _ = rf_ref.shape
    n_pairs = GW // 128

    @pl.when(c == 0)
    def _():
        st_ref[...] = jnp.zeros_like(st_ref)

    ones_bd = _segment_ones(128, HEAD_DIM, BF16)
    rows = lax.broadcasted_iota(jnp.int32, (HEAD_DIM, 128), 0)
    lanes = lax.broadcasted_iota(jnp.int32, (HEAD_DIM, 128), 1)
    diag = jnp.where(lanes % HEAD_DIM == rows, 1.0, 0.0)
    dirs = ((rf_ref, vf_ref, kkf_ref, wf_ref, bf_ref, ktf_ref, yf_ref),
            (rb_ref, vb_ref, kkb_ref, wb_ref, bb_ref, ktb_ref, yb_ref))

    n_groups = tt // 8

    def group(tg, carry):
        chains = [(b, p) for b in range(bsz) for p in range(n_pairs)]
        nc = len(chains)
        base = [pl.multiple_of((tg if d == 0 else n_groups - 1 - tg) * 8, 8) for d in range(2)]
        tiles = [[], []]
        states = [[], []]
        for d in range(2):
            r_ref, v_ref, kk_ref, w_ref, b_ref, kt_ref, _ = dirs[d]
            for b, p in chains:
                cols = slice(p * 128, (p + 1) * 128)
                tiles[d].append(tuple(ref[b, pl.ds(base[d], 8), cols] for ref in (r_ref, v_ref, kk_ref))
                                + tuple(ref[0, b, pl.ds(base[d], 8), cols] for ref in (w_ref, b_ref, kt_ref)))
                states[d].append(st_ref[d, b * n_pairs + p])
        ys = [[[None] * 8 for _ in chains] for _ in range(2)]
        pending = [None, None]

        def emit_outputs(d, yb, jj):
            for ci in range(nc):
                ys[d][ci][jj] = jnp.sum(yb[ci * HEAD_DIM:(ci + 1) * HEAD_DIM] * diag, axis=0, keepdims=True)

        for j in range(8):
            for d in range(2):
                jj = j if d == 0 else 7 - j
                row = lambda a: a[jj:jj + 1, :]
                parts = ([s * row(t[2]) for s, t in zip(states[d], tiles[d])]
                         + [diag * row(t[1]) for t in tiles[d]])
                if pending[d] is not None:
                    parts += pending[d][0]
                res = jnp.dot(jnp.concatenate(parts, axis=0).astype(BF16), ones_bd, preferred_element_type=F32)
                for ci, t in enumerate(tiles[d]):
                    sa = res[ci * HEAD_DIM:(ci + 1) * HEAD_DIM]
                    vcol = res[(nc + ci) * HEAD_DIM:(nc + ci + 1) * HEAD_DIM]
                    states[d][ci] = states[d][ci] * row(t[3]) - sa * row(t[4]) + vcol * row(t[5])
                if pending[d] is not None:
                    emit_outputs(d, res[2 * nc * HEAD_DIM:], pending[d][1])
                pending[d] = ([s * row(t[0]) for s, t in zip(states[d], tiles[d])], jj)
        for d in range(2):
            q = jnp.concatenate(pending[d][0], axis=0).astype(BF16)
            emit_outputs(d, jnp.dot(q, ones_bd, preferred_element_type=F32), pending[d][1])
            y_ref = dirs[d][6]
            for ci, (b, p) in enumerate(chains):
                st_ref[d, b * n_pairs + p] = states[d][ci]
                y_ref[b, pl.ds(base[d], 8), p * 128:(p + 1) * 128] = jnp.concatenate(ys[d][ci], axis=0)
        return carry

    lax.fori_loop(0, n_groups, group, 0)


def _rwkv_post_kernel(yf_ref, yb_ref, bonus_ref, g_ref, lng_ref, lnb_ref, o_ref):
    y = yf_ref[0] + yb_ref[0]
    head_mean = _segment_ones(GW, HEAD_DIM, F32) * (1.0 / HEAD_DIM)
    m = _dot_f32(y, head_mean)
    yc = y - m
    var = _dot_f32(yc * yc, head_mean)
    yn = yc * lax.rsqrt(var + RWKV_GN_EPS) * lng_ref[...] + lnb_ref[...]
    o_ref[0] = (yn + bonus_ref[0]) * g_ref[0]


def _block_diag2(m):
    z = jnp.zeros_like(m[0])
    return jnp.concatenate([jnp.concatenate([m[0], z], 1), jnp.concatenate([z, m[1]], 1)], 0)


def _rwkv7_pallas(f, mu, w0, w2, a0, a2, g2, k_k, k_a, r_k, ln_g, ln_b, *, ctx_len):
    bsz, seq, cols = f.shape
    tr = ROW_BLOCK
    assert ctx_len % tr == 0 and seq % tr == 0
    nb, ctx_blocks = seq // tr, ctx_len // tr
    prev_spec, next_spec = _halo_specs(cols, tr)
    row2 = lambda a: a.reshape(1, -1).astype(F32)
    full = lambda a: pl.BlockSpec(a.shape, lambda b, i: (0,) * a.ndim)
    params = [mu, row2(w0), _block_diag2(w2), row2(a0), _block_diag2(a2), g2, row2(k_k), row2(k_a), row2(r_k)]
    act = jax.ShapeDtypeStruct((bsz, seq, GW), F32)
    act2 = jax.ShapeDtypeStruct((2, bsz, seq, GW), F32)
    blk = pl.BlockSpec((1, tr, GW), lambda b, i: (b, i, 0))
    blk2 = pl.BlockSpec((2, 1, tr, GW), lambda b, i: (0, b, i, 0))
    r, v, kkn, g, bonus, w, bb, kt = pl.pallas_call(
        functools.partial(_rwkv_prep_kernel, ctx_blocks=ctx_blocks),
        name="rwkv_prep",
        grid=(bsz, nb),
        in_specs=[pl.BlockSpec((1, tr, cols), lambda b, i: (b, i, 0)), prev_spec, next_spec(seq // 8)]
                 + [full(p) for p in params],
        out_specs=[blk] * 5 + [blk2] * 3,
        out_shape=[act] * 5 + [act2] * 3,
        compiler_params=pltpu.CompilerParams(dimension_semantics=("arbitrary", "arbitrary"),
                                             vmem_limit_bytes=V7X_VMEM_LIMIT_BYTES),
    )(f, f, f, *params)

    def bwd_block(c):
        return jnp.where(c < ctx_blocks, ctx_blocks - 1 - c, nb - 1 - (c - ctx_blocks))
    fwd = pl.BlockSpec((bsz, tr, GW), lambda c: (0, c, 0))
    bwd = pl.BlockSpec((bsz, tr, GW), lambda c: (0, bwd_block(c), 0))
    fwd_d = pl.BlockSpec((1, bsz, tr, GW), lambda c: (0, 0, c, 0))
    bwd_d = pl.BlockSpec((1, bsz, tr, GW), lambda c: (1, 0, bwd_block(c), 0))
    yf, yb = pl.pallas_call(
        _rwkv_scan_kernel,
        name="rwkv_scan",
        grid=(nb,),
        in_specs=[fwd, fwd, fwd, fwd_d, fwd_d, fwd_d, bwd, bwd, bwd, bwd_d, bwd_d, bwd_d],
        out_specs=[fwd, bwd],
        out_shape=[act, act],
        scratch_shapes=[pltpu.VMEM((2, bsz * (GW // 128), HEAD_DIM, 128), F32)],
        compiler_params=pltpu.CompilerParams(dimension_semantics=("arbitrary",),
                                             vmem_limit_bytes=V7X_VMEM_LIMIT_BYTES),
    )(r, v, kkn, w, bb, kt, r, v, kkn, w, bb, kt)

    return pl.pallas_call(
        _rwkv_post_kernel,
        name="rwkv_post",
        grid=(bsz, nb),
        in_specs=[blk, blk, blk, blk, full(row2(ln_g)), full(row2(ln_b))],
        out_specs=blk,
        out_shape=act,
        compiler_params=pltpu.CompilerParams(dimension_semantics=("arbitrary", "arbitrary")),
    )(yf, yb, bonus, g, row2(ln_g), row2(ln_b))


GDN_GATE_LANES = 128
GDN_PADDED_COLS = 4 * GW + GDN_GATE_LANES


def _gdn_prep_kernel(x_ref, xprev_ref, xnext_ref, ab_ref, conv_ref, alog_ref, dtb_ref,
                     q_out, k_out, v_out, gb_out, *, ctx_blocks):
    i = pl.program_id(1)
    keep_prev, keep_next = _segment_edge_flags(i, pl.num_programs(1), ctx_blocks)
    x = x_ref[0]
    xm, xp = _shifted_rows(x, xprev_ref[0, 7:8, :] * keep_prev, xnext_ref[0, 0:1, :] * keep_next)
    y = xm * conv_ref[0:1, :] + x * conv_ref[1:2, :] + xp * conv_ref[2:3, :]
    y = y * jax.nn.sigmoid(y)
    head_sum = _segment_ones(GW, HEAD_DIM, F32)
    q = y[:, 0:GW]
    k = y[:, GW:2 * GW]
    q_out[0] = q * lax.rsqrt(_dot_f32(q * q, head_sum) + 1e-6) * (HEAD_DIM ** -0.5)
    k_out[0] = k * lax.rsqrt(_dot_f32(k * k, head_sum) + 1e-6)
    v_out[0] = y[:, 2 * GW:3 * GW]
    ab = ab_ref[0]
    lane = lax.broadcasted_iota(jnp.int32, ab.shape, 1)
    log_alpha = -jnp.exp(alog_ref[...]) * _softplus(ab + dtb_ref[...])
    gb_out[0] = jnp.where(lane < 2 * GROUP_HEADS, log_alpha, jax.nn.sigmoid(ab))


def _gdn_chunk_kernel(qf_ref, kf_ref, vf_ref, gf_ref, qb_ref, kb_ref, vb_ref, gb_ref, of_ref, ob_ref, st_ref):
    i = pl.program_id(1)
    c = GDN_CHUNK
    n_chunks = qf_ref.shape[1] // c

    @pl.when(i == 0)
    def _():
        st_ref[...] = jnp.zeros_like(st_ref)

    r = lax.broadcasted_iota(jnp.int32, (c, c), 0)
    s = lax.broadcasted_iota(jnp.int32, (c, c), 1)
    eye = r == s
    ones_cc = jnp.ones((c, c), F32)
    incl = (r >= s, r <= s)
    strict = (r > s, r < s)
    levels = []
    b = 1
    while b < c:
        levels.append(jnp.logical_and(r // (2 * b) == s // (2 * b), r // b != s // b))
        b *= 2
    dirs = ((qf_ref, kf_ref, vf_ref, gf_ref, of_ref), (qb_ref, kb_ref, vb_ref, gb_ref, ob_ref))

    bdot = lambda x, y: jnp.dot(x.astype(BF16), y.astype(BF16), preferred_element_type=F32)
    bdot_nt = lambda x, y: lax.dot_general(x.astype(BF16), y.astype(BF16), _NT, preferred_element_type=F32)
    bdot_tn = lambda x, y: lax.dot_general(x.astype(BF16), y.astype(BF16), (((0,), (0,)), ((), ())),
                                           preferred_element_type=F32)

    heads = [(d, h) for d in range(2) for h in range(GROUP_HEADS)]
    row0 = lambda cc, d: (cc if d == 0 else n_chunks - 1 - cc) * c
    gates = {(cc, d): dirs[d][3][0, row0(cc, d):row0(cc, d) + c, :] for cc in range(n_chunks) for d in range(2)}
    cum = {key: _dot_f32(jnp.where(incl[key[1]], 1.0, 0.0), g) for key, g in gates.items()}
    tot = {key: _dot_f32(ones_cc, g) for key, g in gates.items()}
    chains = [(cc, d, h) for cc in range(n_chunks) for d, h in heads]
    st = {}
    for cc, d, h in chains:
        cols = slice(h * HEAD_DIM, (h + 1) * HEAD_DIM)
        lg = d * GROUP_HEADS + h
        q, k, v = (dirs[d][n][0, row0(cc, d):row0(cc, d) + c, cols] for n in range(3))
        gc = cum[cc, d][:, lg:lg + 1]
        st[cc, d, h] = dict(q=q, k=k, v=v, gc=gc, gt=tot[cc, d][:, lg:lg + 1],
                            beta=gates[cc, d][:, 2 * GROUP_HEADS + lg:2 * GROUP_HEADS + lg + 1],
                            gc_row=_dot_f32(ones_cc, jnp.where(eye, gc, 0.0)))
    for (cc, d, h), x in st.items():
        x["decay"] = jnp.exp(jnp.where(incl[d], x["gc"] - x["gc_row"], NEG_BIG))
        x["kb"] = x["k"] * x["beta"]
        x["a"] = jnp.where(strict[d], bdot_nt(x["kb"], x["k"]) * x["decay"], 0.0)
        x["qk"] = jnp.where(incl[d], bdot_nt(x["q"], x["k"]) * x["decay"], 0.0)
        x["inv"] = jnp.where(eye, 1.0, 0.0)
    for lvl in levels:
        for x in st.values():
            x["t"] = bdot(jnp.where(lvl, x["a"], 0.0), x["inv"])
        for x in st.values():
            x["inv"] = x["inv"] - bdot(x["inv"], x["t"])
    for x in st.values():
        x["eg"] = jnp.exp(x["gc"])
        x["sol"] = bdot(x["inv"], jnp.concatenate([x["v"] * x["beta"], x["kb"] * x["eg"]], axis=-1))
        x["qg"] = x["q"] * x["eg"]
        x["kg"] = x["k"] * jnp.exp(x["gt"] - x["gc"])
    state = {(d, h): st_ref[d, h] for d, h in heads}
    for cc in range(n_chunks):
        cur = [(key, st[(cc,) + key]) for key in heads]
        for key, x in cur:
            x["ws"] = bdot(x["sol"][:, HEAD_DIM:], state[key])
            x["qs"] = bdot(x["qg"], state[key])
        for key, x in cur:
            x["v_new"] = x["sol"][:, :HEAD_DIM] - x["ws"]
            x["o"] = x["qs"] + bdot(x["qk"], x["v_new"])
            x["upd"] = bdot_tn(x["kg"], x["v_new"])
        for key, x in cur:
            state[key] = state[key] * jnp.exp(x["gt"][0:1, :]) + x["upd"]
        for d in range(2):
            dirs[d][4][0, row0(cc, d):row0(cc, d) + c, :] = jnp.concatenate(
                [x["o"] for (dd, h), x in cur if dd == d], axis=-1)
    for d, h in heads:
        st_ref[d, h] = state[d, h]


def _gdn_post_kernel(of_ref, ob_ref, gate_ref, g_ref, o_ref):
    o = of_ref[0] + ob_ref[0]
    head_mean = _segment_ones(GW, HEAD_DIM, F32) * (1.0 / HEAD_DIM)
    gate = gate_ref[0]
    o_ref[0] = o * lax.rsqrt(_dot_f32(o * o, head_mean) + 1e-6) * g_ref[...] * (gate * jax.nn.sigmoid(gate))


def _gated_deltanet_pallas(f, conv_w, a_log, dt_bias, norm_g, *, ctx_len):
    bsz, seq, width = f.shape
    tr = ROW_BLOCK
    assert ctx_len % tr == 0 and seq % tr == 0 and tr % GDN_CHUNK == 0
    if width == GDN_COLS:
        f = jnp.pad(f, ((0, 0), (0, 0), (0, GDN_PADDED_COLS - GDN_COLS)))
    assert f.shape[2] == GDN_PADDED_COLS
    nb, ctx_blocks = seq // tr, ctx_len // tr
    lane_pad = lambda a: jnp.pad(a.reshape(1, -1).astype(F32), ((0, 0), (0, GDN_GATE_LANES - a.size)))
    prev_spec, next_spec = _halo_specs(3 * GW, tr)
    full = lambda a: pl.BlockSpec(a.shape, lambda b, i: (0,) * a.ndim)
    act = jax.ShapeDtypeStruct((bsz, seq, GW), F32)
    gact = jax.ShapeDtypeStruct((bsz, seq, GDN_GATE_LANES), F32)
    blk = pl.BlockSpec((1, tr, GW), lambda b, i: (b, i, 0))
    gblk = pl.BlockSpec((1, tr, GDN_GATE_LANES), lambda b, i: (b, i, 0))
    gate_view = pl.BlockSpec((1, tr, GW), lambda b, i: (b, i, 3))
    ab_view = pl.BlockSpec((1, tr, GDN_GATE_LANES), lambda b, i: (b, i, 4 * GW // GDN_GATE_LANES))
    params = [conv_w.astype(F32), lane_pad(a_log), lane_pad(dt_bias)]
    q, k, v, gb = pl.pallas_call(
        functools.partial(_gdn_prep_kernel, ctx_blocks=ctx_blocks),
        name="gdn_prep",
        grid=(bsz, nb),
        in_specs=[pl.BlockSpec((1, tr, 3 * GW), lambda b, i: (b, i, 0)), prev_spec, next_spec(seq // 8), ab_view]
                 + [full(p) for p in params],
        out_specs=[blk, blk, blk, gblk],
        out_shape=[act, act, act, gact],
        compiler_params=pltpu.CompilerParams(dimension_semantics=("arbitrary", "arbitrary"),
                                             vmem_limit_bytes=V7X_VMEM_LIMIT_BYTES),
    )(f, f, f, f, *params)

    def bwd_block(i):
        return jnp.where(i < ctx_blocks, ctx_blocks - 1 - i, nb - 1 - (i - ctx_blocks))
    bblk = pl.BlockSpec((1, tr, GW), lambda b, i: (b, bwd_block(i), 0))
    bgblk = pl.BlockSpec((1, tr, GDN_GATE_LANES), lambda b, i: (b, bwd_block(i), 0))
    of, ob = pl.pallas_call(
        _gdn_chunk_kernel,
        name="gdn_chunks",
        grid=(bsz, nb),
        in_specs=[blk, blk, blk, gblk, bblk, bblk, bblk, bgblk],
        out_specs=[blk, bblk],
        out_shape=[act, act],
        scratch_shapes=[pltpu.VMEM((2, GROUP_HEADS, HEAD_DIM, HEAD_DIM), F32)],
        compiler_params=pltpu.CompilerParams(dimension_semantics=("arbitrary", "arbitrary"),
                                             vmem_limit_bytes=V7X_VMEM_LIMIT_BYTES),
    )(q, k, v, gb, q, k, v, gb)

    g_row = jnp.tile(norm_g.reshape(1, HEAD_DIM).astype(F32), (1, GROUP_HEADS))
    return pl.pallas_call(
        _gdn_post_kernel,
        name="gdn_post",
        grid=(bsz, nb),
        in_specs=[blk, blk, gate_view, full(g_row)],
        out_specs=blk,
        out_shape=act,
        compiler_params=pltpu.CompilerParams(dimension_semantics=("arbitrary", "arbitrary")),
    )(of, ob, f, g_row)


ROPE_PAIR = DIFF_HALF // 4


def _rope_tables(seq, ctx_len, q_scale):
    n = jnp.arange(seq - ctx_len, dtype=jnp.int32)
    row, col = n // GRID_W, n % GRID_W
    i = jnp.arange(HEAD_DIM)
    grp = (i % DIFF_HALF) // (2 * ROPE_PAIR)
    inv = ROPE_BASE ** (-(i % ROPE_PAIR).astype(F32) / ROPE_PAIR)
    pos = jnp.where(grp[None, :] == 0, row[:, None], col[:, None]).astype(F32)
    ang = pos * inv[None, :]
    sign = jnp.where((i % (2 * ROPE_PAIR)) < ROPE_PAIR, -1.0, 1.0)
    cos = jnp.concatenate([jnp.ones((ctx_len, HEAD_DIM), F32), jnp.cos(ang)], 0)
    sin = jnp.concatenate([jnp.zeros((ctx_len, HEAD_DIM), F32), jnp.sin(ang) * sign], 0)
    cos = jnp.tile(cos, (1, GROUP_HEADS))
    sin = jnp.tile(sin, (1, GROUP_HEADS))
    return jnp.concatenate([cos * q_scale, cos], 1), jnp.concatenate([sin * q_scale, sin], 1)


def _qkv_prep_kernel(p_ref, cos_ref, sin_ref, q_out, k_out, v_out):
    qk = p_ref[0, :, 0:2 * GW]
    width = 2 * GW
    lane = lax.broadcasted_iota(jnp.int32, qk.shape, 1)
    partner = jnp.where(lane % (2 * ROPE_PAIR) < ROPE_PAIR,
                        pltpu.roll(qk, width - ROPE_PAIR, axis=1), pltpu.roll(qk, ROPE_PAIR, axis=1))
    rot = qk * cos_ref[...] + partner * sin_ref[...]
    q_out[0] = rot[:, 0:GW].astype(BF16)
    v = p_ref[0, :, 2 * GW:3 * GW]
    for h in range(GROUP_HEADS):
        k_out[0, h] = rot[:, GW + h * HEAD_DIM:GW + (h + 1) * HEAD_DIM].astype(BF16)
        v_out[0, h] = v[:, h * HEAD_DIM:(h + 1) * HEAD_DIM].astype(BF16)


def _qkv_prep(p, cos, sin):
    bsz, seq, _ = p.shape
    tr = ROW_BLOCK
    head_major = jax.ShapeDtypeStruct((bsz, GROUP_HEADS, seq, HEAD_DIM), BF16)
    hm_spec = pl.BlockSpec((1, GROUP_HEADS, tr, HEAD_DIM), lambda b, i: (b, 0, i, 0))
    return pl.pallas_call(
        _qkv_prep_kernel,
        name="qkv_prep",
        grid=(bsz, seq // tr),
        in_specs=[pl.BlockSpec((1, tr, 3 * GW), lambda b, i: (b, i, 0)),
                  pl.BlockSpec((tr, 2 * GW), lambda b, i: (i, 0)),
                  pl.BlockSpec((tr, 2 * GW), lambda b, i: (i, 0))],
        out_specs=[pl.BlockSpec((1, tr, GW), lambda b, i: (b, i, 0)), hm_spec, hm_spec],
        out_shape=[jax.ShapeDtypeStruct((bsz, seq, GW), BF16), head_major, head_major],
        compiler_params=pltpu.CompilerParams(dimension_semantics=("arbitrary", "arbitrary")),
    )(p, cos, sin)


_NT = (((1,), (1,)), ((), ()))


def _softmax_pv(s, v):
    m = jnp.max(s, axis=-1, keepdims=True)
    e = jnp.exp(s - m)
    return jnp.dot(e.astype(BF16), v, preferred_element_type=F32) / jnp.sum(e, axis=-1, keepdims=True)


def _diff_attn_kernel(q_ref, k_ref, v_ref, lam_ref, g_ref, o_ref, *, ctx_blocks, ctx_len):
    i = pl.program_id(1)
    lv = lam_ref[...]
    lam_init = lv[4:5, 0:1]
    lam = (jnp.exp(jnp.sum(lv[0:1] * lv[1:2], axis=-1, keepdims=True))
           - jnp.exp(jnp.sum(lv[2:3] * lv[3:4], axis=-1, keepdims=True)) + lam_init)
    lane = lax.broadcasted_iota(jnp.int32, (q_ref.shape[1], HEAD_DIM), 1)

    def attend(n_keys):
        outs = []
        for h in range(GROUP_HEADS):
            qh = q_ref[0, :, h * HEAD_DIM:(h + 1) * HEAD_DIM]
            kh = k_ref[0, h, 0:n_keys, :]
            vh = v_ref[0, h, 0:n_keys, :]
            zero = jnp.zeros_like(qh)
            s1 = lax.dot_general(jnp.where(lane < DIFF_HALF, qh, zero), kh, _NT, preferred_element_type=F32)
            s2 = lax.dot_general(jnp.where(lane >= DIFF_HALF, qh, zero), kh, _NT, preferred_element_type=F32)
            o = _softmax_pv(s1, vh) - lam * _softmax_pv(s2, vh)
            o = o * lax.rsqrt(jnp.mean(o * o, axis=-1, keepdims=True) + 1e-6) * g_ref[...] * (1.0 - lam_init)
            outs.append(o)
        o_ref[0] = jnp.concatenate(outs, axis=-1)

    @pl.when(i < ctx_blocks)
    def _():
        attend(ctx_len)

    @pl.when(i >= ctx_blocks)
    def _():
        attend(k_ref.shape[2])


def _diff_attention_pallas(q, k, v, lam_vecs, norm_g, *, ctx_len, lam_init):
    bsz, seq, _ = q.shape
    tq = ROW_BLOCK
    kv_spec = pl.BlockSpec((1, GROUP_HEADS, seq, HEAD_DIM), lambda b, i: (b, 0, 0, 0))
    lam_rows = jnp.concatenate([lam_vecs.astype(F32), jnp.full((1, lam_vecs.shape[1]), lam_init, F32)], 0)
    return pl.pallas_call(
        functools.partial(_diff_attn_kernel, ctx_blocks=ctx_len // tq, ctx_len=ctx_len),
        name="diff_attn",
        grid=(bsz, seq // tq),
        in_specs=[pl.BlockSpec((1, tq, GW), lambda b, i: (b, i, 0)), kv_spec, kv_spec,
                  pl.BlockSpec(lam_rows.shape, lambda b, i: (0, 0)),
                  pl.BlockSpec((1, HEAD_DIM), lambda b, i: (0, 0))],
        out_specs=pl.BlockSpec((1, tq, GW), lambda b, i: (b, i, 0)),
        out_shape=jax.ShapeDtypeStruct((bsz, seq, GW), F32),
        compiler_params=pltpu.CompilerParams(dimension_semantics=("arbitrary", "arbitrary"),
                                             vmem_limit_bytes=V7X_VMEM_LIMIT_BYTES),
    )(q, k, v, lam_rows, norm_g.reshape(1, HEAD_DIM).astype(F32))


NAT_TILE_ROWS = ROW_BLOCK // GRID_W
NAT_SLAB_ROWS = NAT_TILE_ROWS + WIN_H - 1


def _nat_slab_start(tile, n_rows):
    return np.clip(tile * NAT_TILE_ROWS - WIN_H // 2, 0, n_rows - NAT_SLAB_ROWS)


def _nat_bias_tables(rpb, n_rows):
    n_tiles = n_rows // NAT_TILE_ROWS
    nq, nk, w = NAT_TILE_ROWS, NAT_SLAB_ROWS, GRID_W
    cq, ck = np.arange(w)[:, None], np.arange(w)[None, :]
    d_col = np.clip(ck - cq, -(WIN_W - 1), WIN_W - 1) + WIN_W - 1
    col_1h = (d_col.reshape(-1)[:, None] == np.arange(2 * WIN_W - 1)[None, :]).astype(np.float32)
    c0 = np.clip(cq - WIN_W // 2, 0, w - WIN_W)
    col_ok = (ck >= c0) & (ck < c0 + WIN_W)
    tabs = []
    for tile in (0, 1, n_tiles - 1):
        r = tile * nq + np.arange(nq)[:, None]
        kr = _nat_slab_start(tile, n_rows) + np.arange(nk)[None, :]
        rs = np.clip(r - WIN_H // 2, 0, n_rows - WIN_H)
        row_ok = (kr >= rs) & (kr < rs + WIN_H)
        d_row = np.clip(kr - r + WIN_H - 1, 0, 2 * WIN_H - 2)
        row_1h = (d_row.reshape(-1)[:, None] == np.arange(2 * WIN_H - 1)[None, :]).astype(np.float32)
        t = jnp.einsum('pa,hab,cb->hpc', row_1h, rpb.astype(F32), col_1h, precision=HIGHEST)
        t = t.reshape(GROUP_HEADS, nq, nk, w, w).transpose(0, 1, 3, 2, 4)
        ok = row_ok[:, None, :, None] & col_ok[None, :, None, :]
        tabs.append(jnp.where(ok[None], t, NEG_BIG).reshape(GROUP_HEADS, nq * w, nk * w))
    return jnp.stack(tabs)


def _nat_attn_kernel(q_ref, k_ref, v_ref, bias_ref, o_ref, *, ctx_blocks, ctx_len, n_rows):
    i = pl.program_id(1)
    n_slab = NAT_SLAB_ROWS * GRID_W

    def heads(fn):
        o_ref[0] = jnp.concatenate(
            [fn(h, q_ref[0, :, h * HEAD_DIM:(h + 1) * HEAD_DIM]) for h in range(GROUP_HEADS)], axis=-1)

    @pl.when(i < ctx_blocks)
    def _():
        def ctx_only(h, qh):
            s = lax.dot_general(qh, k_ref[0, h, 0:ctx_len, :], _NT, preferred_element_type=F32)
            return _softmax_pv(s, v_ref[0, h, 0:ctx_len, :])
        heads(ctx_only)

    @pl.when(i >= ctx_blocks)
    def _():
        tile = i - ctx_blocks
        start = jnp.clip(tile * NAT_TILE_ROWS - WIN_H // 2, 0, n_rows - NAT_SLAB_ROWS)
        off = pl.multiple_of(ctx_len + start * GRID_W, GRID_W)

        def windowed(h, qh):
            s_w = lax.dot_general(qh, k_ref[0, h, pl.ds(off, n_slab), :], _NT,
                                  preferred_element_type=F32) + bias_ref[0, h]
            s_c = lax.dot_general(qh, k_ref[0, h, 0:ctx_len, :], _NT, preferred_element_type=F32)
            m = jnp.maximum(jnp.max(s_w, axis=-1, keepdims=True), jnp.max(s_c, axis=-1, keepdims=True))
            e_w = jnp.exp(s_w - m)
            e_c = jnp.exp(s_c - m)
            den = jnp.sum(e_w, axis=-1, keepdims=True) + jnp.sum(e_c, axis=-1, keepdims=True)
            num = (jnp.dot(e_w.astype(BF16), v_ref[0, h, pl.ds(off, n_slab), :], preferred_element_type=F32)
                   + jnp.dot(e_c.astype(BF16), v_ref[0, h, 0:ctx_len, :], preferred_element_type=F32))
            return num / den
        heads(windowed)


def _nat_attention_pallas(q, k, v, rpb, *, ctx_len):
    bsz, seq, _ = q.shape
    tq = ROW_BLOCK
    ctx_blocks = ctx_len // tq
    n_rows = (seq - ctx_len) // GRID_W
    n_tiles = n_rows // NAT_TILE_ROWS
    assert n_rows >= NAT_SLAB_ROWS and n_tiles >= 3
    bias = _nat_bias_tables(rpb, n_rows)

    def variant(i):
        tile = i - ctx_blocks
        return jnp.where(tile <= 0, 0, jnp.where(tile >= n_tiles - 1, 2, 1))
    kv_spec = pl.BlockSpec((1, GROUP_HEADS, seq, HEAD_DIM), lambda b, i: (b, 0, 0, 0))
    return pl.pallas_call(
        functools.partial(_nat_attn_kernel, ctx_blocks=ctx_blocks, ctx_len=ctx_len, n_rows=n_rows),
        name="nat_attn",
        grid=(bsz, seq // tq),
        in_specs=[pl.BlockSpec((1, tq, GW), lambda b, i: (b, i, 0)), kv_spec, kv_spec,
                  pl.BlockSpec((1,) + bias.shape[1:], lambda b, i: (variant(i), 0, 0, 0))],
        out_specs=pl.BlockSpec((1, tq, GW), lambda b, i: (b, i, 0)),
        out_shape=jax.ShapeDtypeStruct((bsz, seq, GW), F32),
        compiler_params=pltpu.CompilerParams(dimension_semantics=("arbitrary", "arbitrary"),
                                             vmem_limit_bytes=V7X_VMEM_LIMIT_BYTES),
    )(q, k, v, bias)


N_MOD = 6
MATMUL_ROWS = 512


def _ada_kernel(c_ref, w_ref, b_ref, o_ref):
    c = c_ref[...]
    o_ref[...] = _dot_f32(c * jax.nn.sigmoid(c), w_ref[...]) + b_ref[...]


def _ada_modulation(c, c_ctx, w_ada, b_ada):
    bsz, d = c.shape
    rows = 8 * ((bsz + 1 + 7) // 8)
    cc = jnp.zeros((rows, d), F32).at[:bsz].set(c).at[bsz].set(c_ctx)
    tn = d
    m = pl.pallas_call(
        _ada_kernel,
        name="ada_modulation",
        grid=(w_ada.shape[1] // tn,),
        in_specs=[pl.BlockSpec((rows, d), lambda j: (0, 0)),
                  pl.BlockSpec((d, tn), lambda j: (0, j)),
                  pl.BlockSpec((1, tn), lambda j: (0, j))],
        out_specs=pl.BlockSpec((rows, tn), lambda j: (0, j)),
        out_shape=jax.ShapeDtypeStruct((rows, w_ada.shape[1]), F32),
        compiler_params=pltpu.CompilerParams(dimension_semantics=("arbitrary",)),
    )(cc, w_ada, b_ada.reshape(1, -1))
    lat = m[:bsz].reshape(bsz, 1, N_MOD, d)
    ctx = jnp.broadcast_to(m[bsz].reshape(1, 1, N_MOD, d), (bsz, 1, N_MOD, d))
    return jnp.concatenate([ctx, lat], axis=1)


def _ln(x):
    mu = jnp.mean(x, axis=-1, keepdims=True)
    xc = x - mu
    return xc * lax.rsqrt(jnp.mean(xc * xc, axis=-1, keepdims=True) + LN_EPS)


def _modulate_kernel(h_ref, mod_ref, o_ref, *, shift_row):
    shift = mod_ref[0, 0, shift_row:shift_row + 1, :]
    scale = mod_ref[0, 0, shift_row + 1:shift_row + 2, :]
    o_ref[0] = (_ln(h_ref[0]) * (1.0 + scale) + shift).astype(o_ref.dtype)


def _mod_spec(d, ctx_blocks):
    return pl.BlockSpec((1, 1, N_MOD, d), lambda b, i: (b, jnp.where(i < ctx_blocks, 0, 1), 0, 0))


def _modulate_pallas(hs, mod, shift_row, *, ctx_len):
    bsz, seq, d = hs.shape
    tr = ROW_BLOCK
    blk = pl.BlockSpec((1, tr, d), lambda b, i: (b, i, 0))
    return pl.pallas_call(
        functools.partial(_modulate_kernel, shift_row=shift_row),
        name="modulate",
        grid=(bsz, seq // tr),
        in_specs=[blk, _mod_spec(d, ctx_len // tr)],
        out_specs=blk,
        out_shape=jax.ShapeDtypeStruct((bsz, seq, d), BF16),
        compiler_params=pltpu.CompilerParams(dimension_semantics=("arbitrary", "arbitrary")),
    )(hs, mod)


def _matmul_kernel(x_ref, w_ref, o_ref):
    o_ref[...] = jnp.dot(x_ref[...], w_ref[...], preferred_element_type=F32)


def _matmul_pallas(x, w):
    m, k = x.shape
    n = w.shape[1]
    tm = MATMUL_ROWS
    return pl.pallas_call(
        _matmul_kernel,
        name="in_proj",
        grid=(m // tm,),
        in_specs=[pl.BlockSpec((tm, k), lambda i: (i, 0)), pl.BlockSpec((k, n), lambda i: (0, 0))],
        out_specs=pl.BlockSpec((tm, n), lambda i: (i, 0)),
        out_shape=jax.ShapeDtypeStruct((m, n), F32),
        compiler_params=pltpu.CompilerParams(dimension_semantics=("arbitrary",),
                                             vmem_limit_bytes=V7X_VMEM_LIMIT_BYTES),
    )(x, w)


def _post_norm_rows(h, gate, y, g, b):
    return _ln(DN_ALPHA * h + gate * y) * g + b


def _out_proj_kernel(ya_ref, yb_ref, yc_ref, yd_ref, w_ref, h_ref, mod_ref, g_ref, b_ref, o_ref):
    mix = None
    for n, y_ref in enumerate((ya_ref, yb_ref, yc_ref, yd_ref)):
        part = jnp.dot(y_ref[0].astype(BF16), w_ref[n * GW:(n + 1) * GW, :], preferred_element_type=F32)
        mix = part if mix is None else mix + part
    o_ref[0] = _post_norm_rows(h_ref[0], mod_ref[0, 0, 2:3, :], mix, g_ref[...], b_ref[...])


def _out_proj_post_norm(ys, w_out, hs, mod, g, b, *, ctx_len):
    bsz, seq, d = hs.shape
    tr = ROW_BLOCK
    yblk = pl.BlockSpec((1, tr, GW), lambda bb, i: (bb, i, 0))
    blk = pl.BlockSpec((1, tr, d), lambda bb, i: (bb, i, 0))
    row = pl.BlockSpec((1, d), lambda bb, i: (0, 0))
    return pl.pallas_call(
        _out_proj_kernel,
        name="out_proj_post_norm",
        grid=(bsz, seq // tr),
        in_specs=[yblk] * 4 + [pl.BlockSpec(w_out.shape, lambda bb, i: (0, 0)), blk, _mod_spec(d, ctx_len // tr), row, row],
        out_specs=blk,
        out_shape=jax.ShapeDtypeStruct((bsz, seq, d), F32),
        compiler_params=pltpu.CompilerParams(dimension_semantics=("arbitrary", "arbitrary")),
    )(*ys, w_out, hs, mod, g.reshape(1, d), b.reshape(1, d))


def _ffn_post_norm_kernel(h_ref, y_ref, mod_ref, g_ref, b_ref, o_ref):
    o_ref[0] = _post_norm_rows(h_ref[0], mod_ref[0, 0, 5:6, :], y_ref[0], g_ref[...], b_ref[...])


def _ffn_post_norm(hs, y, mod, g, b, *, ctx_len):
    bsz, seq, d = hs.shape
    tr = ROW_BLOCK
    blk = pl.BlockSpec((1, tr, d), lambda bb, i: (bb, i, 0))
    row = pl.BlockSpec((1, d), lambda bb, i: (0, 0))
    return pl.pallas_call(
        _ffn_post_norm_kernel,
        name="ffn_post_norm",
        grid=(bsz, seq // tr),
        in_specs=[blk, blk, _mod_spec(d, ctx_len // tr), row, row],
        out_specs=blk,
        out_shape=jax.ShapeDtypeStruct((bsz, seq, d), F32),
        compiler_params=pltpu.CompilerParams(dimension_semantics=("arbitrary", "arbitrary")),
    )(hs, y, mod, g.reshape(1, d), b.reshape(1, d))


def kernel(x, c, ctx, c_ctx, w_ada, b_ada, w_in, w_out, ln_mix_g, ln_mix_b, ln_ffn_g, ln_ffn_b, diff_lam, diff_norm_g, rwkv_mu, rwkv_w0, rwkv_w2, rwkv_a0, rwkv_a2, rwkv_g2, rwkv_kk, rwkv_ka, rwkv_rk, rwkv_ln_g, rwkv_ln_b, gdn_conv, gdn_a_log, gdn_dt_bias, gdn_norm_g, nat_rpb, peer_wq, peer_keys, peer_u, peer_v):
    dtype = x.dtype
    bsz, ctx_len = ctx.shape[0], ctx.shape[1]
    hs = jnp.concatenate([ctx, x], axis=1)
    seq = hs.shape[1]
    col_sizes = [ATTN_COLS, RWKV_COLS, GDN_COLS, ATTN_COLS]
    cos_a, sin_a = _rope_tables(seq, ctx_len, DIFF_HALF ** -0.5)
    cos_d = jnp.concatenate([jnp.full((seq, GW), HEAD_DIM ** -0.5, F32), jnp.ones((seq, GW), F32)], 1)
    sin_d = jnp.zeros_like(cos_d)
    col_offs = np.cumsum([0] + col_sizes)
    d_model = hs.shape[2]
    for l in range(DEPTH):
        lam_init = 0.8 - 0.6 * math.exp(-0.3 * l)
        mod = _ada_modulation(c, c_ctx, w_ada[l], b_ada[l])
        u = _modulate_pallas(hs, mod, 0, ctx_len=ctx_len).reshape(bsz * seq, d_model)
        w_in_b = w_in[l].astype(BF16)
        w_groups = [w_in_b[:, col_offs[n]:col_offs[n + 1]] for n in range(4)]
        w_groups[2] = jnp.pad(w_groups[2], ((0, 0), (0, GDN_PADDED_COLS - GDN_COLS)))
        pa, pb, pc, pd = [_matmul_pallas(u, w).reshape(bsz, seq, w.shape[1]) for w in w_groups]
        qa, ka, va = _qkv_prep(pa, cos_a, sin_a)
        ya = _diff_attention_pallas(qa, ka, va, diff_lam[l], diff_norm_g[l], ctx_len=ctx_len, lam_init=lam_init)
        yb = _rwkv7_pallas(pb, rwkv_mu[l], rwkv_w0[l], rwkv_w2[l], rwkv_a0[l], rwkv_a2[l], rwkv_g2[l],
                           rwkv_kk[l], rwkv_ka[l], rwkv_rk[l], rwkv_ln_g[l], rwkv_ln_b[l], ctx_len=ctx_len)
        yc = _gated_deltanet_pallas(pc, gdn_conv[l], gdn_a_log[l], gdn_dt_bias[l], gdn_norm_g[l], ctx_len=ctx_len)
        qd, kd, vd = _qkv_prep(pd, cos_d, sin_d)
        yd = _nat_attention_pallas(qd, kd, vd, nat_rpb[l], ctx_len=ctx_len)
        hs = _out_proj_post_norm([ya, yb, yc, yd], w_out[l].astype(BF16), hs, mod, ln_mix_g[l], ln_mix_b[l],
                                 ctx_len=ctx_len)
        wq_b = peer_wq[l].astype(BF16)
        keys_b = peer_keys[l].reshape(2 * PEER_HEADS, PEER_KEYS, PEER_HALF).astype(BF16)
        u_b = peer_u[l].astype(BF16)
        vt_b = peer_v[l].astype(BF16).T
        ffn = _peer(_modulate_pallas(hs, mod, 3, ctx_len=ctx_len), wq_b, keys_b, u_b, vt_b)
        hs = _ffn_post_norm(hs, ffn, mod, ln_ffn_g[l], ln_ffn_b[l], ctx_len=ctx_len)
    return hs[:, ctx_len:].astype(dtype)
```

```python
import functools
import math

import jax
import jax.numpy as jnp
import numpy as np
from jax import lax
from jax.experimental import pallas as pl
from jax.experimental.pallas import tpu as pltpu

D_MODEL = 1024
DEPTH = 2
GRID_W = 64
HEAD_DIM = 64
N_GROUPS = 4
GROUP_HEADS = D_MODEL // (N_GROUPS * HEAD_DIM)
GW = GROUP_HEADS * HEAD_DIM
D_MIX = N_GROUPS * GW
DIFF_HALF = HEAD_DIM // 2
Q_BLOCK = 128
ROPE_BASE = 10000.0
DECAY_LORA = 64
ICLR_LORA = 64
GATE_LORA = 128
RWKV_GN_EPS = 64e-5
RWKV_COLS = 3 * GW + 2 * DECAY_LORA + 2 * ICLR_LORA + GATE_LORA
GDN_CONV = 3
GDN_CHUNK = 64
GDN_COLS = 4 * GW + 4 * GROUP_HEADS
WIN_H = 8
WIN_W = 16
ATTN_COLS = 3 * GW
IN_COLS = ATTN_COLS + RWKV_COLS + GDN_COLS + ATTN_COLS
PEER_HEADS = 8
PEER_KEYS = 128
PEER_EXPERTS = PEER_KEYS * PEER_KEYS
PEER_QDIM = 256
PEER_HALF = PEER_QDIM // 2
PEER_TOPK = 16
DN_ALPHA = (2 * DEPTH) ** 0.25
LN_EPS = 1e-5

F32 = jnp.float32
BF16 = jnp.bfloat16

V7X_VMEM_LIMIT_BYTES = 56 * 1024 * 1024
NEG_BIG = -3.0e38


def _split(x, sizes):
    offs = np.cumsum(sizes)[:-1].tolist()
    return jnp.split(x, offs, axis=-1)


def _heads(x):
    return x.reshape(x.shape[0], x.shape[1], -1, HEAD_DIM)


def _layer_norm(x):
    xf = x.astype(F32)
    mu = xf.mean(-1, keepdims=True)
    var = jnp.square(xf - mu).mean(-1, keepdims=True)
    return (xf - mu) * lax.rsqrt(var + LN_EPS)


def _rms_norm(x, g):
    xf = x.astype(F32)
    return xf * lax.rsqrt(jnp.mean(xf * xf, -1, keepdims=True) + 1e-6) * g.astype(F32)


def _l2norm(x):
    xf = x.astype(F32)
    return xf * lax.rsqrt(jnp.sum(xf * xf, -1, keepdims=True) + 1e-6)


def _modulate(h, shift, scale, dtype):
    return (_layer_norm(h) * (1.0 + scale) + shift).astype(dtype)


def _post_norm(h, gate, y, g, b, dtype):
    z = DN_ALPHA * h.astype(F32) + gate.astype(F32) * y.astype(F32)
    return (_layer_norm(z) * g + b).astype(dtype)


def _dwconv_centred(x, w):
    k = w.shape[0]
    pad = k // 2
    t = x.shape[1]
    xp = jnp.pad(x, ((0, 0), (pad, pad), (0, 0)))
    out = xp[:, 0:t] * w[0]
    for i in range(1, k):
        out = out + xp[:, i:i + t] * w[i]
    return out


def _grid_pos(n_tok):
    t = jnp.arange(n_tok, dtype=jnp.int32)
    return t // GRID_W, t % GRID_W


def _rope_1d(x, pos):
    half = x.shape[-1] // 2
    inv = ROPE_BASE ** (-jnp.arange(half, dtype=F32) / half)
    ang = pos.astype(F32)[:, None] * inv
    cos = jnp.cos(ang)[:, None, :]
    sin = jnp.sin(ang)[:, None, :]
    x1 = x[..., :half].astype(F32)
    x2 = x[..., half:].astype(F32)
    return jnp.concatenate([x1 * cos - x2 * sin, x1 * sin + x2 * cos], -1)


def _rope_2d(x, row, col):
    n = x.shape[-1] // 2
    return jnp.concatenate([_rope_1d(x[..., :n], row), _rope_1d(x[..., n:], col)], -1)


def _diff_core(q, k, v, lam):
    scale = DIFF_HALF ** -0.5
    q = q.astype(F32)
    k = k.astype(F32)
    s1 = jnp.einsum('bqhd,bkhd->bhqk', q[..., :DIFF_HALF], k[..., :DIFF_HALF]) * scale
    s2 = jnp.einsum('bqhd,bkhd->bhqk', q[..., DIFF_HALF:], k[..., DIFF_HALF:]) * scale
    p = jax.nn.softmax(s1, -1) - lam * jax.nn.softmax(s2, -1)
    return jnp.einsum('bhqk,bkhd->bqhd', p, v.astype(F32))


def _diff_attention(q, k, v, qc, kc, vc, lam_vecs, norm_g, lam_init, ctx_out):
    bsz, lat_len = q.shape[:2]
    row, col = _grid_pos(lat_len)
    rot = lambda z: jnp.concatenate([_rope_2d(z[..., :DIFF_HALF], row, col),
                                     _rope_2d(z[..., DIFF_HALF:], row, col)], -1)
    q = rot(q)
    k = rot(k)
    lv = lam_vecs.astype(F32)
    lam = jnp.exp(jnp.sum(lv[0] * lv[1])) - jnp.exp(jnp.sum(lv[2] * lv[3])) + lam_init
    k_all = jnp.concatenate([k, kc.astype(F32)], 1)
    v_all = jnp.concatenate([v.astype(F32), vc.astype(F32)], 1)
    nb = lat_len // Q_BLOCK
    qb = jnp.moveaxis(q.reshape(bsz, nb, Q_BLOCK, GROUP_HEADS, HEAD_DIM), 1, 0)
    ob = lax.map(lambda qq: _diff_core(qq, k_all, v_all, lam), qb)
    o = jnp.moveaxis(ob, 0, 1).reshape(bsz, lat_len, GROUP_HEADS, HEAD_DIM)
    post = lambda z: (_rms_norm(z, norm_g) * (1.0 - lam_init)).reshape(z.shape[0], z.shape[1], GW)
    out_ctx = post(_diff_core(qc, kc, vc, lam)) if ctx_out else None
    return post(o), out_ctx


def _wkv7_scan(r, w, k, v, kk, a, s0, reverse):
    def step(s, inp):
        r_t, w_t, k_t, v_t, kk_t, a_t = inp
        sa = jnp.einsum('bhvk,bhk->bhv', s, kk_t)
        s = (s * w_t[:, :, None, :] - sa[..., None] * (kk_t * a_t)[:, :, None, :]
             + v_t[..., None] * k_t[:, :, None, :])
        return s, jnp.einsum('bhvk,bhk->bhv', s, r_t)
    xs = tuple(jnp.moveaxis(z, 1, 0) for z in (r, w, k, v, kk, a))
    s, y = lax.scan(step, s0, xs, reverse=reverse)
    return jnp.moveaxis(y, 0, 1), s


def _rwkv7(f_lat, f_ctx, mu, w0, w2, a0, a2, g2, k_k, k_a, r_k, ln_g, ln_b, ctx_out):
    shift_w = jnp.stack([mu[0], 1.0 - mu[0] - mu[1], mu[1]])

    def prep(f):
        f = _dwconv_centred(f, shift_w).astype(F32)
        r, k, v, wd, ad, gd = _split(f, [GW, GW, GW, 2 * DECAY_LORA, 2 * ICLR_LORA, GATE_LORA])
        bsz, t = f.shape[:2]
        wd = jnp.tanh(wd.reshape(bsz, t, 2, DECAY_LORA))
        ad = ad.reshape(bsz, t, 2, ICLR_LORA)
        w_raw = w0 + jnp.einsum('btdr,drc->btdc', wd, w2)
        decay = jnp.exp(-jnp.exp(-jax.nn.softplus(-w_raw) - 0.5))
        a = jax.nn.sigmoid(a0 + jnp.einsum('btdr,drc->btdc', ad, a2))
        g = jax.nn.sigmoid(gd) @ g2
        kk = _l2norm(_heads(k * k_k))
        kd = k[:, :, None] * (1.0 + (a - 1.0) * k_a)
        return r, v, g, kk, decay, a, kd

    def run(p, d, s0, rev):
        r, v, g, kk, decay, a, kd = p
        return _wkv7_scan(_heads(r), _heads(decay[:, :, d]), _heads(kd[:, :, d]), _heads(v),
                          kk, _heads(a[:, :, d]), s0, rev)

    def post(p, ys):
        r, v, g, kk, decay, a, kd = p
        bsz, t = r.shape[:2]
        y = ys[0] + ys[1]
        m = y.mean(-1, keepdims=True)
        var = jnp.square(y - m).mean(-1, keepdims=True)
        yn = ((y - m) * lax.rsqrt(var + RWKV_GN_EPS)).reshape(bsz, t, GW) * ln_g + ln_b
        rh, vh = _heads(r), _heads(v)
        bonus = ((rh * _heads(kd[:, :, 0]) * r_k).sum(-1, keepdims=True) * vh
                 + (rh * _heads(kd[:, :, 1]) * r_k).sum(-1, keepdims=True) * vh)
        return (yn + bonus.reshape(bsz, t, GW)) * g

    pl_, pc = prep(f_lat), prep(f_ctx)
    s0 = jnp.zeros((f_lat.shape[0], GROUP_HEADS, HEAD_DIM, HEAD_DIM), F32)
    y_lat, y_ctx = [], []
    for d, rev in ((0, False), (1, True)):
        yc, sc = run(pc, d, s0, rev)
        yl, _ = run(pl_, d, sc, rev)
        y_lat.append(yl)
        y_ctx.append(yc)
    out_ctx = post(pc, y_ctx) if ctx_out else None
    return post(pl_, y_lat), out_ctx


def _gdn_chunked(q, k, v, beta, g, s0):
    bsz, t, h, dk = q.shape
    dv = v.shape[-1]
    n = t // GDN_CHUNK
    ch = lambda z: jnp.moveaxis(z.reshape(bsz, n, GDN_CHUNK, h, *z.shape[3:]), 3, 2)
    q, k, v, beta, g = ch(q), ch(k), ch(v), ch(beta), ch(g)
    gc = jnp.cumsum(g, axis=-1)
    i = jnp.arange(GDN_CHUNK)
    incl = i[:, None] >= i[None, :]
    strict = i[:, None] > i[None, :]
    decay = jnp.exp(jnp.where(incl, gc[..., :, None] - gc[..., None, :], -jnp.inf))
    kb = k * beta[..., None]
    a_low = jnp.where(strict, jnp.einsum('bnhid,bnhjd->bnhij', kb, k) * decay, 0.0)
    tmat = a_low + jnp.eye(GDN_CHUNK, dtype=F32)
    rhs = jnp.concatenate([v * beta[..., None], kb * jnp.exp(gc)[..., None]], -1)
    sol = lax.linalg.triangular_solve(tmat, rhs, left_side=True, lower=True, unit_diagonal=True)
    u, w = sol[..., :dv], sol[..., dv:]
    qk = jnp.where(incl, jnp.einsum('bnhid,bnhjd->bnhij', q, k) * decay, 0.0)
    qg = q * jnp.exp(gc)[..., None]
    kg = k * jnp.exp(gc[..., -1:] - gc)[..., None]
    glast = jnp.exp(gc[..., -1])

    def step(s, xs):
        qg_i, kg_i, u_i, w_i, qk_i, gl_i = xs
        v_new = u_i - jnp.einsum('bhcd,bhdv->bhcv', w_i, s)
        o = jnp.einsum('bhcd,bhdv->bhcv', qg_i, s) + jnp.einsum('bhij,bhjv->bhiv', qk_i, v_new)
        s = s * gl_i[..., None, None] + jnp.einsum('bhcd,bhcv->bhdv', kg_i, v_new)
        return s, o

    xs = tuple(jnp.moveaxis(z, 1, 0) for z in (qg, kg, u, w, qk, glast))
    s, o = lax.scan(step, s0, xs)
    o = jnp.moveaxis(jnp.moveaxis(o, 0, 1), 2, 3).reshape(bsz, t, h, dv)
    return o, s


def _gated_deltanet(f_lat, f_ctx, conv_w, a_log, dt_bias, norm_g, ctx_out):
    def prep(f):
        qkv, gate, ab = _split(f, [3 * GW, GW, 4 * GROUP_HEADS])
        qkv = jax.nn.silu(_dwconv_centred(qkv, conv_w).astype(F32))
        q, k, v = [_heads(z) for z in _split(qkv, [GW, GW, GW])]
        q = _l2norm(q) * HEAD_DIM ** -0.5
        k = _l2norm(k)
        ab = ab.astype(F32).reshape(f.shape[0], f.shape[1], 2, 2, GROUP_HEADS)
        log_alpha = -jnp.exp(a_log) * jax.nn.softplus(ab[:, :, 0] + dt_bias)
        beta = jax.nn.sigmoid(ab[:, :, 1])
        return q, k, v, gate, log_alpha, beta

    def run(p, d, s0):
        tr = (lambda z: jnp.flip(z, 1)) if d == 1 else (lambda z: z)
        q, k, v, gate, log_alpha, beta = p
        o, s = _gdn_chunked(tr(q), tr(k), tr(v), tr(beta[:, :, d]), tr(log_alpha[:, :, d]), s0)
        return tr(o), s

    def post(p, os_):
        gate = p[3]
        o = os_[0] + os_[1]
        y = _rms_norm(o, norm_g) * jax.nn.silu(_heads(gate.astype(F32)))
        return y.reshape(o.shape[0], o.shape[1], GW)

    pl_, pc = prep(f_lat), prep(f_ctx)
    s0 = jnp.zeros((f_lat.shape[0], GROUP_HEADS, HEAD_DIM, HEAD_DIM), F32)
    o_lat, o_ctx = [], []
    for d in range(2):
        oc, sc = run(pc, d, s0)
        ol, _ = run(pl_, d, sc)
        o_lat.append(ol)
        o_ctx.append(oc)
    out_ctx = post(pc, o_ctx) if ctx_out else None
    return post(pl_, o_lat), out_ctx


def _softmax_attn(q, k, v):
    s = jnp.einsum('bqhd,bkhd->bhqk', q.astype(F32), k.astype(F32)) * HEAD_DIM ** -0.5
    return jnp.einsum('bhqk,bkhd->bqhd', jax.nn.softmax(s, -1), v.astype(F32))


def _neighbourhood_attention(q, k, v, qc, kc, vc, rpb, ctx_out):
    bsz, lat_len = q.shape[:2]
    rows = lat_len // GRID_W
    kh = min(WIN_H, rows)
    scale = HEAD_DIM ** -0.5
    grid = lambda z: z.astype(F32).reshape(bsz, rows, GRID_W, GROUP_HEADS, HEAD_DIM)
    qg, kg, vg = grid(q), grid(k), grid(v)
    kc32, vc32 = kc.astype(F32), vc.astype(F32)
    rpb = rpb.astype(F32)
    cq = jnp.arange(GRID_W)
    c_start = jnp.clip(cq - WIN_W // 2, 0, GRID_W - WIN_W)
    col_ok = (cq[None, :] >= c_start[:, None]) & (cq[None, :] < c_start[:, None] + WIN_W)
    d_col = jnp.clip(cq[None, :] - cq[:, None], -(WIN_W - 1), WIN_W - 1) + (WIN_W - 1)

    def row_block(r):
        rs = jnp.clip(r - kh // 2, 0, rows - kh)
        q_r = lax.dynamic_index_in_dim(qg, r, axis=1, keepdims=False)
        k_r = lax.dynamic_slice_in_dim(kg, rs, kh, axis=1)
        v_r = lax.dynamic_slice_in_dim(vg, rs, kh, axis=1)
        s = jnp.einsum('bqhd,bkwhd->bqhkw', q_r, k_r) * scale
        d_row = rs + jnp.arange(kh) - r + (WIN_H - 1)
        bias = rpb[:, d_row[None, :, None], d_col[:, None, :]]
        s = jnp.where(col_ok[:, None, None, :], s + jnp.transpose(bias, (1, 0, 2, 3)), -jnp.inf)
        s_c = jnp.einsum('bqhd,bchd->bqhc', q_r, kc32) * scale
        p = jax.nn.softmax(jnp.concatenate(
            [s.reshape(bsz, GRID_W, GROUP_HEADS, kh * GRID_W), s_c], -1), -1)
        p_win = p[..., :kh * GRID_W].reshape(bsz, GRID_W, GROUP_HEADS, kh, GRID_W)
        return (jnp.einsum('bqhkw,bkwhd->bqhd', p_win, v_r)
                + jnp.einsum('bqhc,bchd->bqhd', p[..., kh * GRID_W:], vc32))

    o = lax.map(row_block, jnp.arange(rows))
    out_lat = jnp.moveaxis(o, 0, 1).reshape(bsz, lat_len, GW)
    out_ctx = _softmax_attn(qc, kc, vc).reshape(bsz, qc.shape[1], GW) if ctx_out else None
    return out_lat, out_ctx


def _qkv_heads(p):
    return [_heads(z) for z in _split(p, [GW, GW, GW])]


PEER_STAT_ROWS = 4 * PEER_HEADS


def _topk_rows(x, k):
    rows = []
    cur = x
    for i in range(k):
        m = jnp.max(cur, axis=0, keepdims=True)
        rows.append(m)
        if i + 1 < k:
            cur = jnp.where(cur == m, NEG_BIG, cur)
    return rows


def _peer_score_kernel(x_ref, wq_ref, keys_ref, s_ref, st_ref):
    q = jnp.dot(x_ref[...], wq_ref[...], preferred_element_type=F32).astype(BF16)
    stats = []
    for h in range(PEER_HEADS):
        tops = []
        for p in range(2):
            hp = 2 * h + p
            s_t = lax.dot_general(keys_ref[hp], q[:, hp * PEER_HALF:(hp + 1) * PEER_HALF],
                                  (((1,), (1,)), ((), ())), preferred_element_type=F32)
            s_ref[hp] = s_t
            tops.append(_topk_rows(s_t, PEER_TOPK + 1))
        a, b = tops
        pad = [jnp.full_like(a[0], NEG_BIG)] * 7
        b_head = jnp.concatenate(b[:8], axis=0)
        cand = jnp.concatenate([a[0] + jnp.concatenate(b + pad, axis=0)]
                               + [a[i] + b_head for i in range(1, 8)]
                               + [jnp.concatenate(a[8:] + pad, axis=0) + b[0]], axis=0)
        best_cand = _topk_rows(cand, PEER_TOPK + 1)
        kth, runner_up = best_cand[PEER_TOPK - 1], best_cand[PEER_TOPK]
        best = a[0] + b[0]
        z = jnp.sum(jnp.where(cand >= kth, jnp.exp(cand - best), 0.0), axis=0, keepdims=True)
        stats += [0.5 * (kth + runner_up), tops[0][0], tops[1][0], 1.0 / z]
    st_ref[...] = jnp.concatenate(stats, axis=0)


def _peer_expert_kernel(x_ref, s_ref, st_ref, u_ref, vt_ref, o_ref, e_ref, thr_ref, acc_ref, w_ref,
                        *, rows_per_step, n_chunks):
    j = pl.program_id(1)
    cur = lax.rem(j, 2)

    @pl.when(j == 0)
    def _():
        acc_ref[...] = jnp.zeros_like(acc_ref)
        w_ref[1] = jnp.zeros(w_ref.shape[1:], w_ref.dtype)
        for h in range(PEER_HEADS):
            a0 = st_ref[4 * h + 1:4 * h + 2, :]
            b0 = st_ref[4 * h + 2:4 * h + 3, :]
            rz = st_ref[4 * h + 3:4 * h + 4, :]
            e_ref[2 * h] = jnp.exp(s_ref[2 * h] - a0) * rz
            e_ref[2 * h + 1] = jnp.exp(s_ref[2 * h + 1] - b0)
            thr_ref[h] = st_ref[4 * h:4 * h + 1, :] - s_ref[2 * h]

    @pl.when(j < n_chunks)
    def _():
        slab = 2 * PEER_KEYS
        n_slabs = rows_per_step * PEER_KEYS // slab
        act_slab = lambda n: lax.dot_general(u_ref[n * slab:(n + 1) * slab, :], x_ref[...], (((1,), (1,)), ((), ())),
                                             preferred_element_type=F32)
        acts = [act_slab(n) for n in range(n_slabs)]
        acc_ref[...] += jnp.dot(vt_ref[...], w_ref[1 - cur], preferred_element_type=F32)
        for il in range(rows_per_step):
            i = j * rows_per_step + il
            o = il * PEER_KEYS
            act = acts[o // slab][o % slab:o % slab + PEER_KEYS]
            act = 0.5 * act * (1.0 + lax.erf(act * (2.0 ** -0.5)))
            gate = jnp.zeros_like(act)
            for h in range(PEER_HEADS):
                chosen = s_ref[2 * h + 1] >= thr_ref[h, pl.ds(i, 1), :]
                gate = gate + jnp.where(chosen, e_ref[2 * h, pl.ds(i, 1), :] * e_ref[2 * h + 1], 0.0)
            w_ref[cur, o:o + PEER_KEYS, :] = (act * gate).astype(BF16)

    @pl.when(j == n_chunks)
    def _():
        o_ref[...] = (acc_ref[...] + jnp.dot(vt_ref[...], w_ref[1 - cur], preferred_element_type=F32)).T


def _peer_pallas(h, w_q, keys, u_tab, vt_tab, *, tb, rows_per_step):
    n_tok, d = h.shape
    nhp = 2 * PEER_HEADS
    s, st = pl.pallas_call(
        _peer_score_kernel,
        name="peer_scores",
        grid=(n_tok // tb,),
        in_specs=[pl.BlockSpec((tb, d), lambda i: (i, 0)),
                  pl.BlockSpec(w_q.shape, lambda i: (0, 0)),
                  pl.BlockSpec(keys.shape, lambda i: (0, 0, 0))],
        out_specs=[pl.BlockSpec((nhp, PEER_KEYS, tb), lambda i: (0, 0, i)),
                   pl.BlockSpec((PEER_STAT_ROWS, tb), lambda i: (0, i))],
        out_shape=[jax.ShapeDtypeStruct((nhp, PEER_KEYS, n_tok), F32),
                   jax.ShapeDtypeStruct((PEER_STAT_ROWS, n_tok), F32)],
        compiler_params=pltpu.CompilerParams(dimension_semantics=("arbitrary",),
                                             vmem_limit_bytes=V7X_VMEM_LIMIT_BYTES),
    )(h, w_q, keys)
    ec = rows_per_step * PEER_KEYS
    n_chunks = PEER_EXPERTS // ec
    return pl.pallas_call(
        functools.partial(_peer_expert_kernel, rows_per_step=rows_per_step, n_chunks=n_chunks),
        name="peer_experts",
        grid=(n_tok // tb, n_chunks + 1),
        in_specs=[pl.BlockSpec((tb, d), lambda i, j: (i, 0)),
                  pl.BlockSpec((nhp, PEER_KEYS, tb), lambda i, j: (0, 0, i)),
                  pl.BlockSpec((PEER_STAT_ROWS, tb), lambda i, j: (0, i)),
                  pl.BlockSpec((ec, d), lambda i, j: (jnp.minimum(j, n_chunks - 1), 0)),
                  pl.BlockSpec((d, ec), lambda i, j: (0, jnp.maximum(j - 1, 0)))],
        out_specs=pl.BlockSpec((tb, d), lambda i, j: (i, 0)),
        out_shape=jax.ShapeDtypeStruct((n_tok, d), F32),
        scratch_shapes=[pltpu.VMEM((nhp, PEER_KEYS, tb), F32),
                        pltpu.VMEM((PEER_HEADS, PEER_KEYS, tb), F32),
                        pltpu.VMEM((d, tb), F32),
                        pltpu.VMEM((2, ec, tb), BF16)],
        compiler_params=pltpu.CompilerParams(dimension_semantics=("arbitrary", "arbitrary"),
                                             vmem_limit_bytes=V7X_VMEM_LIMIT_BYTES),
    )(h, s, st, u_tab, vt_tab)


def _peer(h, w_q, keys, u_tab, vt_tab):
    bsz, t, d = h.shape
    n_tok = bsz * t
    tb = 512 if n_tok % 512 == 0 else 256
    out = _peer_pallas(h.reshape(n_tok, d).astype(BF16), w_q, keys, u_tab, vt_tab, tb=tb, rows_per_step=8)
    return out.reshape(bsz, t, d)


ROW_BLOCK = 256
HIGHEST = lax.Precision.HIGHEST


def _dot_f32(a, b):
    return jnp.dot(a, b, precision=HIGHEST, preferred_element_type=F32)


def _segment_ones(n, seg, dtype):
    r = lax.broadcasted_iota(jnp.int32, (n, n), 0) // seg
    c = lax.broadcasted_iota(jnp.int32, (n, n), 1) // seg
    return jnp.where(r == c, 1.0, 0.0).astype(dtype)


def _shifted_rows(x, prev_row, next_row):
    t = x.shape[0]
    rows = lax.broadcasted_iota(jnp.int32, x.shape, 0)
    xm = jnp.where(rows == 0, prev_row, pltpu.roll(x, 1, axis=0))
    xp = jnp.where(rows == t - 1, next_row, pltpu.roll(x, t - 1, axis=0))
    return xm, xp


def _segment_edge_flags(i, n_blocks, ctx_blocks):
    is_start = jnp.logical_or(i == 0, i == ctx_blocks)
    is_end = jnp.logical_or(i == ctx_blocks - 1, i == n_blocks - 1)
    return jnp.where(is_start, 0.0, 1.0), jnp.where(is_end, 0.0, 1.0)


def _halo_specs(width, tr):
    g = tr // 8
    prev = pl.BlockSpec((1, 8, width), lambda b, i: (b, jnp.maximum(i * g - 1, 0), 0))
    nxt = lambda n_groups: pl.BlockSpec((1, 8, width), lambda b, i: (b, jnp.minimum((i + 1) * g, n_groups - 1), 0))
    return prev, nxt


def _softplus(z):
    return jnp.maximum(z, 0.0) + jnp.log1p(jnp.exp(-jnp.abs(z)))


def _rwkv_prep_kernel(x_ref, xprev_ref, xnext_ref, mu_ref, w0_ref, w2_ref, a0_ref, a2_ref, g2_ref,
                      kk_ref, ka_ref, rk_ref,
                      r_out, v_out, kkn_out, g_out, bonus_out, w_out, b_out, kt_out, *, ctx_blocks):
    i = pl.program_id(1)
    keep_prev, keep_next = _segment_edge_flags(i, pl.num_programs(1), ctx_blocks)
    x = x_ref[0]
    xm, xp = _shifted_rows(x, xprev_ref[0, 7:8, :] * keep_prev, xnext_ref[0, 0:1, :] * keep_next)
    mu0 = mu_ref[0:1, :]
    mu1 = mu_ref[1:2, :]
    f = xm * mu0 + x * (1.0 - mu0 - mu1) + xp * mu1
    r = f[:, 0:GW]
    k = f[:, GW:2 * GW]
    v = f[:, 2 * GW:3 * GW]
    o = 3 * GW
    wd = jnp.tanh(f[:, o:o + 2 * DECAY_LORA])
    ad = f[:, o + 2 * DECAY_LORA:o + 2 * DECAY_LORA + 2 * ICLR_LORA]
    gd = f[:, o + 2 * DECAY_LORA + 2 * ICLR_LORA:]
    w_raw = w0_ref[...] + _dot_f32(wd, w2_ref[...])
    decay = jnp.exp(-jnp.exp(-_softplus(-w_raw) - 0.5))
    a = jax.nn.sigmoid(a0_ref[...] + _dot_f32(ad, a2_ref[...]))
    g = _dot_f32(jax.nn.sigmoid(gd), g2_ref[...])
    head_sum = _segment_ones(GW, HEAD_DIM, F32)
    kx = k * kk_ref[...]
    kkn = kx * lax.rsqrt(_dot_f32(kx * kx, head_sum) + 1e-6)
    kd_sum = jnp.zeros_like(k)
    for d in range(2):
        a_d = a[:, d * GW:(d + 1) * GW]
        kd = k * (1.0 + (a_d - 1.0) * ka_ref[...])
        kd_sum = kd_sum + kd
        w_out[d, 0] = decay[:, d * GW:(d + 1) * GW]
        b_out[d, 0] = kkn * a_d
        kt_out[d, 0] = kd
    r_out[0] = r
    v_out[0] = v
    kkn_out[0] = kkn
    g_out[0] = g
    bonus_out[0] = _dot_f32(r * kd_sum * rk_ref[...], head_sum) * v


def _rwkv_scan_kernel(rf_ref, vf_ref, kkf_ref, wf_ref, bf_ref, ktf_ref,
                      rb_ref, vb_ref, kkb_ref, wb_ref, bb_ref, ktb_ref,
                      yf_ref, yb_ref, st_ref):
    c = pl.program_id(0)
    bsz, tt, _ = rf_ref.shape
    n_pairs = GW // 128

    @pl.when(c == 0)
    def _():
        st_ref[...] = jnp.zeros_like(st_ref)

    ones_bd = _segment_ones(128, HEAD_DIM, BF16)
    rows = lax.broadcasted_iota(jnp.int32, (HEAD_DIM, 128), 0)
    lanes = lax.broadcasted_iota(jnp.int32, (HEAD_DIM, 128), 1)
    diag = jnp.where(lanes % HEAD_DIM == rows, 1.0, 0.0)
    dirs = ((rf_ref, vf_ref, kkf_ref, wf_ref, bf_ref, ktf_ref, yf_ref),
            (rb_ref, vb_ref, kkb_ref, wb_ref, bb_ref, ktb_ref, yb_ref))

    n_groups = tt // 8

    def group(tg, carry):
        chains = [(b, p) for b in range(bsz) for p in range(n_pairs)]
        nc = len(chains)
        base = [pl.multiple_of((tg if d == 0 else n_groups - 1 - tg) * 8, 8) for d in range(2)]
        tiles = [[], []]
        states = [[], []]
        for d in range(2):
            r_ref, v_ref, kk_ref, w_ref, b_ref, kt_ref, _ = dirs[d]
            for b, p in chains:
                cols = slice(p * 128, (p + 1) * 128)
                tiles[d].append(tuple(ref[b, pl.ds(base[d], 8), cols] for ref in (r_ref, v_ref, kk_ref))
                                + tuple(ref[0, b, pl.ds(base[d], 8), cols] for ref in (w_ref, b_ref, kt_ref)))
                states[d].append(st_ref[d, b * n_pairs + p])
        ys = [[[None] * 8 for _ in chains] for _ in range(2)]
        pending = [None, None]

        def emit_outputs(d, yb, jj):
            for ci in range(nc):
                ys[d][ci][jj] = jnp.sum(yb[ci * HEAD_DIM:(ci + 1) * HEAD_DIM] * diag, axis=0, keepdims=True)

        for j in range(8):
            for d in range(2):
                jj = j if d == 0 else 7 - j
                row = lambda a: a[jj:jj + 1, :]
                parts = ([s * row(t[2]) for s, t in zip(states[d], tiles[d])]
                         + [diag * row(t[1]) for t in tiles[d]])
                if pending[d] is not None:
                    parts += pending[d][0]
                res = jnp.dot(jnp.concatenate(parts, axis=0).astype(BF16), ones_bd, preferred_element_type=F32)
                for ci, t in enumerate(tiles[d]):
                    sa = res[ci * HEAD_DIM:(ci + 1) * HEAD_DIM]
                    vcol = res[(nc + ci) * HEAD_DIM:(nc + ci + 1) * HEAD_DIM]
                    states[d][ci] = states[d][ci] * row(t[3]) - sa * row(t[4]) + vcol * row(t[5])
                if pending[d] is not None:
                    emit_outputs(d, res[2 * nc * HEAD_DIM:], pending[d][1])
                pending[d] = ([s * row(t[0]) for s, t in zip(states[d], tiles[d])], jj)
        for d in range(2):
            q = jnp.concatenate(pending[d][0], axis=0).astype(BF16)
            emit_outputs(d, jnp.dot(q, ones_bd, preferred_element_type=F32), pending[d][1])
            y_ref = dirs[d][6]
            for ci, (b, p) in enumerate(chains):
                st_ref[d, b * n_pairs + p] = states[d][ci]
                y_ref[b, pl.ds(base[d], 8), p * 128:(p + 1) * 128] = jnp.concatenate(ys[d][ci], axis=0)
        return carry

    lax.fori_loop(0, n_groups, group, 0)


def _rwkv_post_kernel(yf_ref, yb_ref, bonus_ref, g_ref, lng_ref, lnb_ref, o_ref):
    y = yf_ref[0] + yb_ref[0]
    head_mean = _segment_ones(GW, HEAD_DIM, F32) * (1.0 / HEAD_DIM)
    m = _dot_f32(y, head_mean)
    yc = y - m
    var = _dot_f32(yc * yc, head_mean)
    yn = yc * lax.rsqrt(var + RWKV_GN_EPS) * lng_ref[...] + lnb_ref[...]
    o_ref[0] = (yn + bonus_ref[0]) * g_ref[0]


def _block_diag2(m):
    z = jnp.zeros_like(m[0])
    return jnp.concatenate([jnp.concatenate([m[0], z], 1), jnp.concatenate([z, m[1]], 1)], 0)


def _rwkv7_pallas(f, mu, w0, w2, a0, a2, g2, k_k, k_a, r_k, ln_g, ln_b, *, ctx_len):
    bsz, seq, cols = f.shape
    tr = ROW_BLOCK
    assert ctx_len % tr == 0 and seq % tr == 0
    nb, ctx_blocks = seq // tr, ctx_len // tr
    prev_spec, next_spec = _halo_specs(cols, tr)
    row2 = lambda a: a.reshape(1, -1).astype(F32)
    full = lambda a: pl.BlockSpec(a.shape, lambda b, i: (0,) * a.ndim)
    params = [mu, row2(w0), _block_diag2(w2), row2(a0), _block_diag2(a2), g2, row2(k_k), row2(k_a), row2(r_k)]
    act = jax.ShapeDtypeStruct((bsz, seq, GW), F32)
    act2 = jax.ShapeDtypeStruct((2, bsz, seq, GW), F32)
    blk = pl.BlockSpec((1, tr, GW), lambda b, i: (b, i, 0))
    blk2 = pl.BlockSpec((2, 1, tr, GW), lambda b, i: (0, b, i, 0))
    r, v, kkn, g, bonus, w, bb, kt = pl.pallas_call(
        functools.partial(_rwkv_prep_kernel, ctx_blocks=ctx_blocks),
        name="rwkv_prep",
        grid=(bsz, nb),
        in_specs=[pl.BlockSpec((1, tr, cols), lambda b, i: (b, i, 0)), prev_spec, next_spec(seq // 8)]
                 + [full(p) for p in params],
        out_specs=[blk] * 5 + [blk2] * 3,
        out_shape=[act] * 5 + [act2] * 3,
        compiler_params=pltpu.CompilerParams(dimension_semantics=("arbitrary", "arbitrary"),
                                             vmem_limit_bytes=V7X_VMEM_LIMIT_BYTES),
    )(f, f, f, *params)

    def bwd_block(c):
        return jnp.where(c < ctx_blocks, ctx_blocks - 1 - c, nb - 1 - (c - ctx_blocks))
    fwd = pl.BlockSpec((bsz, tr, GW), lambda c: (0, c, 0))
    bwd = pl.BlockSpec((bsz, tr, GW), lambda c: (0, bwd_block(c), 0))
    fwd_d = pl.BlockSpec((1, bsz, tr, GW), lambda c: (0, 0, c, 0))
    bwd_d = pl.BlockSpec((1, bsz, tr, GW), lambda c: (1, 0, bwd_block(c), 0))
    yf, yb = pl.pallas_call(
        _rwkv_scan_kernel,
        name="rwkv_scan",
        grid=(nb,),
        in_specs=[fwd, fwd, fwd, fwd_d, fwd_d, fwd_d, bwd, bwd, bwd, bwd_d, bwd_d, bwd_d],
        out_specs=[fwd, bwd],
        out_shape=[act, act],
        scratch_shapes=[pltpu.VMEM((2, bsz * (GW // 128), HEAD_DIM, 128), F32)],
        compiler_params=pltpu.CompilerParams(dimension_semantics=("arbitrary",),
                                             vmem_limit_bytes=V7X_VMEM_LIMIT_BYTES),
    )(r, v, kkn, w, bb, kt, r, v, kkn, w, bb, kt)

    return pl.pallas_call(
        _rwkv_post_kernel,
        name="rwkv_post",
        grid=(bsz, nb),
        in_specs=[blk, blk, blk, blk, full(row2(ln_g)), full(row2(ln_b))],
        out_specs=blk,
        out_shape=act,
        compiler_params=pltpu.CompilerParams(dimension_semantics=("arbitrary", "arbitrary")),
    )(yf, yb, bonus, g, row2(ln_g), row2(ln_b))


GDN_GATE_LANES = 128
GDN_PADDED_COLS = 4 * GW + GDN_GATE_LANES


def _gdn_prep_kernel(x_ref, xprev_ref, xnext_ref, ab_ref, conv_ref, alog_ref, dtb_ref,
                     q_out, k_out, v_out, gb_out, *, ctx_blocks):
    i = pl.program_id(1)
    keep_prev, keep_next = _segment_edge_flags(i, pl.num_programs(1), ctx_blocks)
    x = x_ref[0]
    xm, xp = _shifted_rows(x, xprev_ref[0, 7:8, :] * keep_prev, xnext_ref[0, 0:1, :] * keep_next)
    y = xm * conv_ref[0:1, :] + x * conv_ref[1:2, :] + xp * conv_ref[2:3, :]
    y = y * jax.nn.sigmoid(y)
    head_sum = _segment_ones(GW, HEAD_DIM, F32)
    q = y[:, 0:GW]
    k = y[:, GW:2 * GW]
    q_out[0] = q * lax.rsqrt(_dot_f32(q * q, head_sum) + 1e-6) * (HEAD_DIM ** -0.5)
    k_out[0] = k * lax.rsqrt(_dot_f32(k * k, head_sum) + 1e-6)
    v_out[0] = y[:, 2 * GW:3 * GW]
    ab = ab_ref[0]
    lane = lax.broadcasted_iota(jnp.int32, ab.shape, 1)
    log_alpha = -jnp.exp(alog_ref[...]) * _softplus(ab + dtb_ref[...])
    gb_out[0] = jnp.where(lane < 2 * GROUP_HEADS, log_alpha, jax.nn.sigmoid(ab))


def _gdn_chunk_kernel(qf_ref, kf_ref, vf_ref, gf_ref, qb_ref, kb_ref, vb_ref, gb_ref, of_ref, ob_ref, st_ref):
    i = pl.program_id(1)
    c = GDN_CHUNK
    n_chunks = qf_ref.shape[1] // c

    @pl.when(i == 0)
    def _():
        st_ref[...] = jnp.zeros_like(st_ref)

    r = lax.broadcasted_iota(jnp.int32, (c, c), 0)
    s = lax.broadcasted_iota(jnp.int32, (c, c), 1)
    eye = r == s
    ones_cc = jnp.ones((c, c), F32)
    incl = (r >= s, r <= s)
    strict = (r > s, r < s)
    levels = []
    b = 1
    while b < c:
        levels.append(jnp.logical_and(r // (2 * b) == s // (2 * b), r // b != s // b))
        b *= 2
    dirs = ((qf_ref, kf_ref, vf_ref, gf_ref, of_ref), (qb_ref, kb_ref, vb_ref, gb_ref, ob_ref))

    bdot = lambda x, y: jnp.dot(x.astype(BF16), y.astype(BF16), preferred_element_type=F32)
    bdot_nt = lambda x, y: lax.dot_general(x.astype(BF16), y.astype(BF16), _NT, preferred_element_type=F32)
    bdot_tn = lambda x, y: lax.dot_general(x.astype(BF16), y.astype(BF16), (((0,), (0,)), ((), ())),
                                           preferred_element_type=F32)

    heads = [(d, h) for d in range(2) for h in range(GROUP_HEADS)]
    row0 = lambda cc, d: (cc if d == 0 else n_chunks - 1 - cc) * c
    gates = {(cc, d): dirs[d][3][0, row0(cc, d):row0(cc, d) + c, :] for cc in range(n_chunks) for d in range(2)}
    cum = {key: _dot_f32(jnp.where(incl[key[1]], 1.0, 0.0), g) for key, g in gates.items()}
    tot = {key: _dot_f32(ones_cc, g) for key, g in gates.items()}
    chains = [(cc, d, h) for cc in range(n_chunks) for d, h in heads]
    st = {}
    for cc, d, h in chains:
        cols = slice(h * HEAD_DIM, (h + 1) * HEAD_DIM)
        lg = d * GROUP_HEADS + h
        q, k, v = (dirs[d][n][0, row0(cc, d):row0(cc, d) + c, cols] for n in range(3))
        gc = cum[cc, d][:, lg:lg + 1]
        st[cc, d, h] = dict(q=q, k=k, v=v, gc=gc, gt=tot[cc, d][:, lg:lg + 1],
                            beta=gates[cc, d][:, 2 * GROUP_HEADS + lg:2 * GROUP_HEADS + lg + 1],
                            gc_row=_dot_f32(ones_cc, jnp.where(eye, gc, 0.0)))
    for (cc, d, h), x in st.items():
        x["decay"] = jnp.exp(jnp.where(incl[d], x["gc"] - x["gc_row"], NEG_BIG))
        x["kb"] = x["k"] * x["beta"]
        x["a"] = jnp.where(strict[d], bdot_nt(x["kb"], x["k"]) * x["decay"], 0.0)
        x["qk"] = jnp.where(incl[d], bdot_nt(x["q"], x["k"]) * x["decay"], 0.0)
        x["inv"] = jnp.where(eye, 1.0, 0.0)
    for lvl in levels:
        for x in st.values():
            x["t"] = bdot(jnp.where(lvl, x["a"], 0.0), x["inv"])
        for x in st.values():
            x["inv"] = x["inv"] - bdot(x["inv"], x["t"])
    for x in st.values():
        x["eg"] = jnp.exp(x["gc"])
        x["sol"] = bdot(x["inv"], jnp.concatenate([x["v"] * x["beta"], x["kb"] * x["eg"]], axis=-1))
        x["qg"] = x["q"] * x["eg"]
        x["kg"] = x["k"] * jnp.exp(x["gt"] - x["gc"])
    state = {(d, h): st_ref[d, h] for d, h in heads}
    for cc in range(n_chunks):
        cur = [(key, st[(cc,) + key]) for key in heads]
        for key, x in cur:
            x["ws"] = bdot(x["sol"][:, HEAD_DIM:], state[key])
            x["qs"] = bdot(x["qg"], state[key])
        for key, x in cur:
            x["v_new"] = x["sol"][:, :HEAD_DIM] - x["ws"]
            x["o"] = x["qs"] + bdot(x["qk"], x["v_new"])
            x["upd"] = bdot_tn(x["kg"], x["v_new"])
        for key, x in cur:
            state[key] = state[key] * jnp.exp(x["gt"][0:1, :]) + x["upd"]
        for d in range(2):
            dirs[d][4][0, row0(cc, d):row0(cc, d) + c, :] = jnp.concatenate(
                [x["o"] for (dd, h), x in cur if dd == d], axis=-1)
    for d, h in heads:
        st_ref[d, h] = state[d, h]


def _gdn_post_kernel(of_ref, ob_ref, gate_ref, g_ref, o_ref):
    o = of_ref[0] + ob_ref[0]
    head_mean = _segment_ones(GW, HEAD_DIM, F32) * (1.0 / HEAD_DIM)
    gate = gate_ref[0]
    o_ref[0] = o * lax.rsqrt(_dot_f32(o * o, head_mean) + 1e-6) * g_ref[...] * (gate * jax.nn.sigmoid(gate))


def _gated_deltanet_pallas(f, conv_w, a_log, dt_bias, norm_g, *, ctx_len):
    bsz, seq, width = f.shape
    tr = ROW_BLOCK
    assert ctx_len % tr == 0 and seq % tr == 0 and tr % GDN_CHUNK == 0
    if width == GDN_COLS:
        f = jnp.pad(f, ((0, 0), (0, 0), (0, GDN_PADDED_COLS - GDN_COLS)))
    assert f.shape[2] == GDN_PADDED_COLS
    nb, ctx_blocks = seq // tr, ctx_len // tr
    lane_pad = lambda a: jnp.pad(a.reshape(1, -1).astype(F32), ((0, 0), (0, GDN_GATE_LANES - a.size)))
    prev_spec, next_spec = _halo_specs(3 * GW, tr)
    full = lambda a: pl.BlockSpec(a.shape, lambda b, i: (0,) * a.ndim)
    act = jax.ShapeDtypeStruct((bsz, seq, GW), F32)
    gact = jax.ShapeDtypeStruct((bsz, seq, GDN_GATE_LANES), F32)
    blk = pl.BlockSpec((1, tr, GW), lambda b, i: (b, i, 0))
    gblk = pl.BlockSpec((1, tr, GDN_GATE_LANES), lambda b, i: (b, i, 0))
    gate_view = pl.BlockSpec((1, tr, GW), lambda b, i: (b, i, 3))
    ab_view = pl.BlockSpec((1, tr, GDN_GATE_LANES), lambda b, i: (b, i, 4 * GW // GDN_GATE_LANES))
    params = [conv_w.astype(F32), lane_pad(a_log), lane_pad(dt_bias)]
    q, k, v, gb = pl.pallas_call(
        functools.partial(_gdn_prep_kernel, ctx_blocks=ctx_blocks),
        name="gdn_prep",
        grid=(bsz, nb),
        in_specs=[pl.BlockSpec((1, tr, 3 * GW), lambda b, i: (b, i, 0)), prev_spec, next_spec(seq // 8), ab_view]
                 + [full(p) for p in params],
        out_specs=[blk, blk, blk, gblk],
        out_shape=[act, act, act, gact],
        compiler_params=pltpu.CompilerParams(dimension_semantics=("arbitrary", "arbitrary"),
                                             vmem_limit_bytes=V7X_VMEM_LIMIT_BYTES),
    )(f, f, f, f, *params)

    def bwd_block(i):
        return jnp.where(i < ctx_blocks, ctx_blocks - 1 - i, nb - 1 - (i - ctx_blocks))
    bblk = pl.BlockSpec((1, tr, GW), lambda b, i: (b, bwd_block(i), 0))
    bgblk = pl.BlockSpec((1, tr, GDN_GATE_LANES), lambda b, i: (b, bwd_block(i), 0))
    of, ob = pl.pallas_call(
        _gdn_chunk_kernel,
        name="gdn_chunks",
        grid=(bsz, nb),
        in_specs=[blk, blk, blk, gblk, bblk, bblk, bblk, bgblk],
        out_specs=[blk, bblk],
        out_shape=[act, act],
        scratch_shapes=[pltpu.VMEM((2, GROUP_HEADS, HEAD_DIM, HEAD_DIM), F32)],
        compiler_params=pltpu.CompilerParams(dimension_semantics=("arbitrary", "arbitrary"),
                                             vmem_limit_bytes=V7X_VMEM_LIMIT_BYTES),
    )(q, k, v, gb, q, k, v, gb)

    g_row = jnp.tile(norm_g.reshape(1, HEAD_DIM).astype(F32), (1, GROUP_HEADS))
    return pl.pallas_call(
        _gdn_post_kernel,
        name="gdn_post",
        grid=(bsz, nb),
        in_specs=[blk, blk, gate_view, full(g_row)],
        out_specs=blk,
        out_shape=act,
        compiler_params=pltpu.CompilerParams(dimension_semantics=("arbitrary", "arbitrary")),
    )(of, ob, f, g_row)


ROPE_PAIR = DIFF_HALF // 4


def _rope_tables(seq, ctx_len, q_scale):
    n = jnp.arange(seq - ctx_len, dtype=jnp.int32)
    row, col = n // GRID_W, n % GRID_W
    i = jnp.arange(HEAD_DIM)
    grp = (i % DIFF_HALF) // (2 * ROPE_PAIR)
    inv = ROPE_BASE ** (-(i % ROPE_PAIR).astype(F32) / ROPE_PAIR)
    pos = jnp.where(grp[None, :] == 0, row[:, None], col[:, None]).astype(F32)
    ang = pos * inv[None, :]
    sign = jnp.where((i % (2 * ROPE_PAIR)) < ROPE_PAIR, -1.0, 1.0)
    cos = jnp.concatenate([jnp.ones((ctx_len, HEAD_DIM), F32), jnp.cos(ang)], 0)
    sin = jnp.concatenate([jnp.zeros((ctx_len, HEAD_DIM), F32), jnp.sin(ang) * sign], 0)
    cos = jnp.tile(cos, (1, GROUP_HEADS))
    sin = jnp.tile(sin, (1, GROUP_HEADS))
    return jnp.concatenate([cos * q_scale, cos], 1), jnp.concatenate([sin * q_scale, sin], 1)


def _qkv_prep_kernel(p_ref, cos_ref, sin_ref, q_out, k_out, v_out):
    qk = p_ref[0, :, 0:2 * GW]
    width = 2 * GW
    lane = lax.broadcasted_iota(jnp.int32, qk.shape, 1)
    partner = jnp.where(lane % (2 * ROPE_PAIR) < ROPE_PAIR,
                        pltpu.roll(qk, width - ROPE_PAIR, axis=1), pltpu.roll(qk, ROPE_PAIR, axis=1))
    rot = qk * cos_ref[...] + partner * sin_ref[...]
    q_out[0] = rot[:, 0:GW].astype(BF16)
    v = p_ref[0, :, 2 * GW:3 * GW]
    for h in range(GROUP_HEADS):
        k_out[0, h] = rot[:, GW + h * HEAD_DIM:GW + (h + 1) * HEAD_DIM].astype(BF16)
        v_out[0, h] = v[:, h * HEAD_DIM:(h + 1) * HEAD_DIM].astype(BF16)


def _qkv_prep(p, cos, sin):
    bsz, seq, _ = p.shape
    tr = ROW_BLOCK
    head_major = jax.ShapeDtypeStruct((bsz, GROUP_HEADS, seq, HEAD_DIM), BF16)
    hm_spec = pl.BlockSpec((1, GROUP_HEADS, tr, HEAD_DIM), lambda b, i: (b, 0, i, 0))
    return pl.pallas_call(
        _qkv_prep_kernel,
        name="qkv_prep",
        grid=(bsz, seq // tr),
        in_specs=[pl.BlockSpec((1, tr, 3 * GW), lambda b, i: (b, i, 0)),
                  pl.BlockSpec((tr, 2 * GW), lambda b, i: (i, 0)),
                  pl.BlockSpec((tr, 2 * GW), lambda b, i: (i, 0))],
        out_specs=[pl.BlockSpec((1, tr, GW), lambda b, i: (b, i, 0)), hm_spec, hm_spec],
        out_shape=[jax.ShapeDtypeStruct((bsz, seq, GW), BF16), head_major, head_major],
        compiler_params=pltpu.CompilerParams(dimension_semantics=("arbitrary", "arbitrary")),
    )(p, cos, sin)


_NT = (((1,), (1,)), ((), ()))


def _softmax_pv(s, v):
    m = jnp.max(s, axis=-1, keepdims=True)
    e = jnp.exp(s - m)
    return jnp.dot(e.astype(BF16), v, preferred_element_type=F32) / jnp.sum(e, axis=-1, keepdims=True)


def _diff_attn_kernel(q_ref, k_ref, v_ref, lam_ref, g_ref, o_ref, *, ctx_blocks, ctx_len):
    i = pl.program_id(1)
    lv = lam_ref[...]
    lam_init = lv[4:5, 0:1]
    lam = (jnp.exp(jnp.sum(lv[0:1] * lv[1:2], axis=-1, keepdims=True))
           - jnp.exp(jnp.sum(lv[2:3] * lv[3:4], axis=-1, keepdims=True)) + lam_init)
    lane = lax.broadcasted_iota(jnp.int32, (q_ref.shape[1], HEAD_DIM), 1)

    def attend(n_keys):
        outs = []
        for h in range(GROUP_HEADS):
            qh = q_ref[0, :, h * HEAD_DIM:(h + 1) * HEAD_DIM]
            kh = k_ref[0, h, 0:n_keys, :]
            vh = v_ref[0, h, 0:n_keys, :]
            zero = jnp.zeros_like(qh)
            s1 = lax.dot_general(jnp.where(lane < DIFF_HALF, qh, zero), kh, _NT, preferred_element_type=F32)
            s2 = lax.dot_general(jnp.where(lane >= DIFF_HALF, qh, zero), kh, _NT, preferred_element_type=F32)
            o = _softmax_pv(s1, vh) - lam * _softmax_pv(s2, vh)
            o = o * lax.rsqrt(jnp.mean(o * o, axis=-1, keepdims=True) + 1e-6) * g_ref[...] * (1.0 - lam_init)
            outs.append(o)
        o_ref[0] = jnp.concatenate(outs, axis=-1)

    @pl.when(i < ctx_blocks)
    def _():
        attend(ctx_len)

    @pl.when(i >= ctx_blocks)
    def _():
        attend(k_ref.shape[2])


def _diff_attention_pallas(q, k, v, lam_vecs, norm_g, *, ctx_len, lam_init):
    bsz, seq, _ = q.shape
    tq = ROW_BLOCK
    kv_spec = pl.BlockSpec((1, GROUP_HEADS, seq, HEAD_DIM), lambda b, i: (b, 0, 0, 0))
    lam_rows = jnp.concatenate([lam_vecs.astype(F32), jnp.full((1, lam_vecs.shape[1]), lam_init, F32)], 0)
    return pl.pallas_call(
        functools.partial(_diff_attn_kernel, ctx_blocks=ctx_len // tq, ctx_len=ctx_len),
        name="diff_attn",
        grid=(bsz, seq // tq),
        in_specs=[pl.BlockSpec((1, tq, GW), lambda b, i: (b, i, 0)), kv_spec, kv_spec,
                  pl.BlockSpec(lam_rows.shape, lambda b, i: (0, 0)),
                  pl.BlockSpec((1, HEAD_DIM), lambda b, i: (0, 0))],
        out_specs=pl.BlockSpec((1, tq, GW), lambda b, i: (b, i, 0)),
        out_shape=jax.ShapeDtypeStruct((bsz, seq, GW), F32),
        compiler_params=pltpu.CompilerParams(dimension_semantics=("arbitrary", "arbitrary"),
                                             vmem_limit_bytes=V7X_VMEM_LIMIT_BYTES),
    )(q, k, v, lam_rows, norm_g.reshape(1, HEAD_DIM).astype(F32))


NAT_TILE_ROWS = ROW_BLOCK // GRID_W
NAT_SLAB_ROWS = NAT_TILE_ROWS + WIN_H - 1


def _nat_slab_start(tile, n_rows):
    return np.clip(tile * NAT_TILE_ROWS - WIN_H // 2, 0, n_rows - NAT_SLAB_ROWS)


def _nat_bias_tables(rpb, n_rows):
    n_tiles = n_rows // NAT_TILE_ROWS
    nq, nk, w = NAT_TILE_ROWS, NAT_SLAB_ROWS, GRID_W
    cq, ck = np.arange(w)[:, None], np.arange(w)[None, :]
    d_col = np.clip(ck - cq, -(WIN_W - 1), WIN_W - 1) + WIN_W - 1
    col_1h = (d_col.reshape(-1)[:, None] == np.arange(2 * WIN_W - 1)[None, :]).astype(np.float32)
    c0 = np.clip(cq - WIN_W // 2, 0, w - WIN_W)
    col_ok = (ck >= c0) & (ck < c0 + WIN_W)
    tabs = []
    for tile in (0, 1, n_tiles - 1):
        r = tile * nq + np.arange(nq)[:, None]
        kr = _nat_slab_start(tile, n_rows) + np.arange(nk)[None, :]
        rs = np.clip(r - WIN_H // 2, 0, n_rows - WIN_H)
        row_ok = (kr >= rs) & (kr < rs + WIN_H)
        d_row = np.clip(kr - r + WIN_H - 1, 0, 2 * WIN_H - 2)
        row_1h = (d_row.reshape(-1)[:, None] == np.arange(2 * WIN_H - 1)[None, :]).astype(np.float32)
        t = jnp.einsum('pa,hab,cb->hpc', row_1h, rpb.astype(F32), col_1h, precision=HIGHEST)
        t = t.reshape(GROUP_HEADS, nq, nk, w, w).transpose(0, 1, 3, 2, 4)
        ok = row_ok[:, None, :, None] & col_ok[None, :, None, :]
        tabs.append(jnp.where(ok[None], t, NEG_BIG).reshape(GROUP_HEADS, nq * w, nk * w))
    return jnp.stack(tabs)


def _nat_attn_kernel(q_ref, k_ref, v_ref, bias_ref, o_ref, *, ctx_blocks, ctx_len, n_rows):
    i = pl.program_id(1)
    n_slab = NAT_SLAB_ROWS * GRID_W

    def heads(fn):
        o_ref[0] = jnp.concatenate(
            [fn(h, q_ref[0, :, h * HEAD_DIM:(h + 1) * HEAD_DIM]) for h in range(GROUP_HEADS)], axis=-1)

    @pl.when(i < ctx_blocks)
    def _():
        def ctx_only(h, qh):
            s = lax.dot_general(qh, k_ref[0, h, 0:ctx_len, :], _NT, preferred_element_type=F32)
            return _softmax_pv(s, v_ref[0, h, 0:ctx_len, :])
        heads(ctx_only)

    @pl.when(i >= ctx_blocks)
    def _():
        tile = i - ctx_blocks
        start = jnp.clip(tile * NAT_TILE_ROWS - WIN_H // 2, 0, n_rows - NAT_SLAB_ROWS)
        off = pl.multiple_of(ctx_len + start * GRID_W, GRID_W)

        def windowed(h, qh):
            s_w = lax.dot_general(qh, k_ref[0, h, pl.ds(off, n_slab), :], _NT,
                                  preferred_element_type=F32) + bias_ref[0, h]
            s_c = lax.dot_general(qh, k_ref[0, h, 0:ctx_len, :], _NT, preferred_element_type=F32)
            m = jnp.maximum(jnp.max(s_w, axis=-1, keepdims=True), jnp.max(s_c, axis=-1, keepdims=True))
            e_w = jnp.exp(s_w - m)
            e_c = jnp.exp(s_c - m)
            den = jnp.sum(e_w, axis=-1, keepdims=True) + jnp.sum(e_c, axis=-1, keepdims=True)
            num = (jnp.dot(e_w.astype(BF16), v_ref[0, h, pl.ds(off, n_slab), :], preferred_element_type=F32)
                   + jnp.dot(e_c.astype(BF16), v_ref[0, h, 0:ctx_len, :], preferred_element_type=F32))
            return num / den
        heads(windowed)


def _nat_attention_pallas(q, k, v, rpb, *, ctx_len):
    bsz, seq, _ = q.shape
    tq = ROW_BLOCK
    ctx_blocks = ctx_len // tq
    n_rows = (seq - ctx_len) // GRID_W
    n_tiles = n_rows // NAT_TILE_ROWS
    assert n_rows >= NAT_SLAB_ROWS and n_tiles >= 3
    bias = _nat_bias_tables(rpb, n_rows)

    def variant(i):
        tile = i - ctx_blocks
        return jnp.where(tile <= 0, 0, jnp.where(tile >= n_tiles - 1, 2, 1))
    kv_spec = pl.BlockSpec((1, GROUP_HEADS, seq, HEAD_DIM), lambda b, i: (b, 0, 0, 0))
    return pl.pallas_call(
        functools.partial(_nat_attn_kernel, ctx_blocks=ctx_blocks, ctx_len=ctx_len, n_rows=n_rows),
        name="nat_attn",
        grid=(bsz, seq // tq),
        in_specs=[pl.BlockSpec((1, tq, GW), lambda b, i: (b, i, 0)), kv_spec, kv_spec,
                  pl.BlockSpec((1,) + bias.shape[1:], lambda b, i: (variant(i), 0, 0, 0))],
        out_specs=pl.BlockSpec((1, tq, GW), lambda b, i: (b, i, 0)),
        out_shape=jax.ShapeDtypeStruct((bsz, seq, GW), F32),
        compiler_params=pltpu.CompilerParams(dimension_semantics=("arbitrary", "arbitrary"),
                                             vmem_limit_bytes=V7X_VMEM_LIMIT_BYTES),
    )(q, k, v, bias)


N_MOD = 6
MATMUL_ROWS = 512


def _ada_kernel(c_ref, w_ref, b_ref, o_ref):
    c = c_ref[...]
    o_ref[...] = _dot_f32(c * jax.nn.sigmoid(c), w_ref[...]) + b_ref[...]


def _ada_modulation(c, c_ctx, w_ada, b_ada):
    bsz, d = c.shape
    rows = 8 * ((bsz + 1 + 7) // 8)
    cc = jnp.zeros((rows, d), F32).at[:bsz].set(c).at[bsz].set(c_ctx)
    tn = d
    m = pl.pallas_call(
        _ada_kernel,
        name="ada_modulation",
        grid=(w_ada.shape[1] // tn,),
        in_specs=[pl.BlockSpec((rows, d), lambda j: (0, 0)),
                  pl.BlockSpec((d, tn), lambda j: (0, j)),
                  pl.BlockSpec((1, tn), lambda j: (0, j))],
        out_specs=pl.BlockSpec((rows, tn), lambda j: (0, j)),
        out_shape=jax.ShapeDtypeStruct((rows, w_ada.shape[1]), F32),
        compiler_params=pltpu.CompilerParams(dimension_semantics=("arbitrary",)),
    )(cc, w_ada, b_ada.reshape(1, -1))
    lat = m[:bsz].reshape(bsz, 1, N_MOD, d)
    ctx = jnp.broadcast_to(m[bsz].reshape(1, 1, N_MOD, d), (bsz, 1, N_MOD, d))
    return jnp.concatenate([ctx, lat], axis=1)


def _ln(x):
    mu = jnp.mean(x, axis=-1, keepdims=True)
    xc = x - mu
    return xc * lax.rsqrt(jnp.mean(xc * xc, axis=-1, keepdims=True) + LN_EPS)


def _modulate_kernel(h_ref, mod_ref, o_ref, *, shift_row):
    shift = mod_ref[0, 0, shift_row:shift_row + 1, :]
    scale = mod_ref[0, 0, shift_row + 1:shift_row + 2, :]
    o_ref[0] = (_ln(h_ref[0]) * (1.0 + scale) + shift).astype(o_ref.dtype)


def _mod_spec(d, ctx_blocks):
    return pl.BlockSpec((1, 1, N_MOD, d), lambda b, i: (b, jnp.where(i < ctx_blocks, 0, 1), 0, 0))


def _modulate_pallas(hs, mod, shift_row, *, ctx_len):
    bsz, seq, d = hs.shape
    tr = ROW_BLOCK
    blk = pl.BlockSpec((1, tr, d), lambda b, i: (b, i, 0))
    return pl.pallas_call(
        functools.partial(_modulate_kernel, shift_row=shift_row),
        name="modulate",
        grid=(bsz, seq // tr),
        in_specs=[blk, _mod_spec(d, ctx_len // tr)],
        out_specs=blk,
        out_shape=jax.ShapeDtypeStruct((bsz, seq, d), BF16),
        compiler_params=pltpu.CompilerParams(dimension_semantics=("arbitrary", "arbitrary")),
    )(hs, mod)


def _matmul_kernel(x_ref, w_ref, o_ref):
    o_ref[...] = jnp.dot(x_ref[...], w_ref[...], preferred_element_type=F32)


def _matmul_pallas(x, w):
    m, k = x.shape
    n = w.shape[1]
    tm = MATMUL_ROWS
    return pl.pallas_call(
        _matmul_kernel,
        name="in_proj",
        grid=(m // tm,),
        in_specs=[pl.BlockSpec((tm, k), lambda i: (i, 0)), pl.BlockSpec((k, n), lambda i: (0, 0))],
        out_specs=pl.BlockSpec((tm, n), lambda i: (i, 0)),
        out_shape=jax.ShapeDtypeStruct((m, n), F32),
        compiler_params=pltpu.CompilerParams(dimension_semantics=("arbitrary",),
                                             vmem_limit_bytes=V7X_VMEM_LIMIT_BYTES),
    )(x, w)


def _post_norm_rows(h, gate, y, g, b):
    return _ln(DN_ALPHA * h + gate * y) * g + b


def _out_proj_kernel(ya_ref, yb_ref, yc_ref, yd_ref, w_ref, h_ref, mod_ref, g_ref, b_ref, o_ref):
    mix = None
    for n, y_ref in enumerate((ya_ref, yb_ref, yc_ref, yd_ref)):
        part = jnp.dot(y_ref[0].astype(BF16), w_ref[n * GW:(n + 1) * GW, :], preferred_element_type=F32)
        mix = part if mix is None else mix + part
    o_ref[0] = _post_norm_rows(h_ref[0], mod_ref[0, 0, 2:3, :], mix, g_ref[...], b_ref[...])


def _out_proj_post_norm(ys, w_out, hs, mod, g, b, *, ctx_len):
    bsz, seq, d = hs.shape
    tr = ROW_BLOCK
    yblk = pl.BlockSpec((1, tr, GW), lambda bb, i: (bb, i, 0))
    blk = pl.BlockSpec((1, tr, d), lambda bb, i: (bb, i, 0))
    row = pl.BlockSpec((1, d), lambda bb, i: (0, 0))
    return pl.pallas_call(
        _out_proj_kernel,
        name="out_proj_post_norm",
        grid=(bsz, seq // tr),
        in_specs=[yblk] * 4 + [pl.BlockSpec(w_out.shape, lambda bb, i: (0, 0)), blk, _mod_spec(d, ctx_len // tr), row, row],
        out_specs=blk,
        out_shape=jax.ShapeDtypeStruct((bsz, seq, d), F32),
        compiler_params=pltpu.CompilerParams(dimension_semantics=("arbitrary", "arbitrary")),
    )(*ys, w_out, hs, mod, g.reshape(1, d), b.reshape(1, d))


def _ffn_post_norm_kernel(h_ref, y_ref, mod_ref, g_ref, b_ref, o_ref):
    o_ref[0] = _post_norm_rows(h_ref[0], mod_ref[0, 0, 5:6, :], y_ref[0], g_ref[...], b_ref[...])


def _ffn_post_norm(hs, y, mod, g, b, *, ctx_len):
    bsz, seq, d = hs.shape
    tr = ROW_BLOCK
    blk = pl.BlockSpec((1, tr, d), lambda bb, i: (bb, i, 0))
    row = pl.BlockSpec((1, d), lambda bb, i: (0, 0))
    return pl.pallas_call(
        _ffn_post_norm_kernel,
        name="ffn_post_norm",
        grid=(bsz, seq // tr),
        in_specs=[blk, blk, _mod_spec(d, ctx_len // tr), row, row],
        out_specs=blk,
        out_shape=jax.ShapeDtypeStruct((bsz, seq, d), F32),
        compiler_params=pltpu.CompilerParams(dimension_semantics=("arbitrary", "arbitrary")),
    )(hs, y, mod, g.reshape(1, d), b.reshape(1, d))


def kernel(x, c, ctx, c_ctx, w_ada, b_ada, w_in, w_out, ln_mix_g, ln_mix_b, ln_ffn_g, ln_ffn_b, diff_lam, diff_norm_g, rwkv_mu, rwkv_w0, rwkv_w2, rwkv_a0, rwkv_a2, rwkv_g2, rwkv_kk, rwkv_ka, rwkv_rk, rwkv_ln_g, rwkv_ln_b, gdn_conv, gdn_a_log, gdn_dt_bias, gdn_norm_g, nat_rpb, peer_wq, peer_keys, peer_u, peer_v):
    dtype = x.dtype
    bsz, ctx_len = ctx.shape[0], ctx.shape[1]
    hs = jnp.concatenate([ctx, x], axis=1)
    seq = hs.shape[1]
    col_sizes = [ATTN_COLS, RWKV_COLS, GDN_COLS, ATTN_COLS]
    cos_a, sin_a = _rope_tables(seq, ctx_len, DIFF_HALF ** -0.5)
    cos_d = jnp.concatenate([jnp.full((seq, GW), HEAD_DIM ** -0.5, F32), jnp.ones((seq, GW), F32)], 1)
    sin_d = jnp.zeros_like(cos_d)
    col_offs = np.cumsum([0] + col_sizes)
    d_model = hs.shape[2]
    for l in range(DEPTH):
        lam_init = 0.8 - 0.6 * math.exp(-0.3 * l)
        mod = _ada_modulation(c, c_ctx, w_ada[l], b_ada[l])
        u = _modulate_pallas(hs, mod, 0, ctx_len=ctx_len).reshape(bsz * seq, d_model)
        w_in_b = w_in[l].astype(BF16)
        w_groups = [w_in_b[:, col_offs[n]:col_offs[n + 1]] for n in range(4)]
        w_groups[2] = jnp.pad(w_groups[2], ((0, 0), (0, GDN_PADDED_COLS - GDN_COLS)))
        pa, pb, pc, pd = [_matmul_pallas(u, w).reshape(bsz, seq, w.shape[1]) for w in w_groups]
        qa, ka, va = _qkv_prep(pa, cos_a, sin_a)
        ya = _diff_attention_pallas(qa, ka, va, diff_lam[l], diff_norm_g[l], ctx_len=ctx_len, lam_init=lam_init)
        yb = _rwkv7_pallas(pb, rwkv_mu[l], rwkv_w0[l], rwkv_w2[l], rwkv_a0[l], rwkv_a2[l], rwkv_g2[l],
                           rwkv_kk[l], rwkv_ka[l], rwkv_rk[l], rwkv_ln_g[l], rwkv_ln_b[l], ctx_len=ctx_len)
        yc = _gated_deltanet_pallas(pc, gdn_conv[l], gdn_a_log[l], gdn_dt_bias[l], gdn_norm_g[l], ctx_len=ctx_len)
        qd, kd, vd = _qkv_prep(pd, cos_d, sin_d)
        yd = _nat_attention_pallas(qd, kd, vd, nat_rpb[l], ctx_len=ctx_len)
        hs = _out_proj_post_norm([ya, yb, yc, yd], w_out[l].astype(BF16), hs, mod, ln_mix_g[l], ln_mix_b[l],
                                 ctx_len=ctx_len)
        wq_b = peer_wq[l].astype(BF16)
        keys_b = peer_keys[l].reshape(2 * PEER_HEADS, PEER_KEYS, PEER_HALF).astype(BF16)
        u_b = peer_u[l].astype(BF16)
        vt_b = peer_v[l].astype(BF16).T
        ffn = _peer(_modulate_pallas(hs, mod, 3, ctx_len=ctx_len), wq_b, keys_b, u_b, vt_b)
        hs = _ffn_post_norm(hs, ffn, mod, ln_ffn_g[l], ln_ffn_b[l], ctx_len=ctx_len)
    return hs[:, ctx_len:].astype(dtype)
```

```python
import functools
import math

import jax
import jax.numpy as jnp
import numpy as np
from jax import lax
from jax.experimental import pallas as pl
from jax.experimental.pallas import tpu as pltpu

D_MODEL = 1024
DEPTH = 2
GRID_W = 64
HEAD_DIM = 64
N_GROUPS = 4
GROUP_HEADS = D_MODEL // (N_GROUPS * HEAD_DIM)
GW = GROUP_HEADS * HEAD_DIM
D_MIX = N_GROUPS * GW
DIFF_HALF = HEAD_DIM // 2
Q_BLOCK = 128
ROPE_BASE = 10000.0
DECAY_LORA = 64
ICLR_LORA = 64
GATE_LORA = 128
RWKV_GN_EPS = 64e-5
RWKV_COLS = 3 * GW + 2 * DECAY_LORA + 2 * ICLR_LORA + GATE_LORA
GDN_CONV = 3
GDN_CHUNK = 64
GDN_COLS = 4 * GW + 4 * GROUP_HEADS
WIN_H = 8
WIN_W = 16
ATTN_COLS = 3 * GW
IN_COLS = ATTN_COLS + RWKV_COLS + GDN_COLS + ATTN_COLS
PEER_HEADS = 8
PEER_KEYS = 128
PEER_EXPERTS = PEER_KEYS * PEER_KEYS
PEER_QDIM = 256
PEER_HALF = PEER_QDIM // 2
PEER_TOPK = 16
DN_ALPHA = (2 * DEPTH) ** 0.25
LN_EPS = 1e-5

F32 = jnp.float32
BF16 = jnp.bfloat16

V7X_VMEM_LIMIT_BYTES = 56 * 1024 * 1024
NEG_BIG = -3.0e38


def _split(x, sizes):
    offs = np.cumsum(sizes)[:-1].tolist()
    return jnp.split(x, offs, axis=-1)


def _heads(x):
    return x.reshape(x.shape[0], x.shape[1], -1, HEAD_DIM)


def _layer_norm(x):
    xf = x.astype(F32)
    mu = xf.mean(-1, keepdims=True)
    var = jnp.square(xf - mu).mean(-1, keepdims=True)
    return (xf - mu) * lax.rsqrt(var + LN_EPS)


def _rms_norm(x, g):
    xf = x.astype(F32)
    return xf * lax.rsqrt(jnp.mean(xf * xf, -1, keepdims=True) + 1e-6) * g.astype(F32)


def _l2norm(x):
    xf = x.astype(F32)
    return xf * lax.rsqrt(jnp.sum(xf * xf, -1, keepdims=True) + 1e-6)


def _modulate(h, shift, scale, dtype):
    return (_layer_norm(h) * (1.0 + scale) + shift).astype(dtype)


def _post_norm(h, gate, y, g, b, dtype):
    z = DN_ALPHA * h.astype(F32) + gate.astype(F32) * y.astype(F32)
    return (_layer_norm(z) * g + b).astype(dtype)


def _dwconv_centred(x, w):
    k = w.shape[0]
    pad = k // 2
    t = x.shape[1]
    xp = jnp.pad(x, ((0, 0), (pad, pad), (0, 0)))
    out = xp[:, 0:t] * w[0]
    for i in range(1, k):
        out = out + xp[:, i:i + t] * w[i]
    return out


def _grid_pos(n_tok):
    t = jnp.arange(n_tok, dtype=jnp.int32)
    return t // GRID_W, t % GRID_W


def _rope_1d(x, pos):
    half = x.shape[-1] // 2
    inv = ROPE_BASE ** (-jnp.arange(half, dtype=F32) / half)
    ang = pos.astype(F32)[:, None] * inv
    cos = jnp.cos(ang)[:, None, :]
    sin = jnp.sin(ang)[:, None, :]
    x1 = x[..., :half].astype(F32)
    x2 = x[..., half:].astype(F32)
    return jnp.concatenate([x1 * cos - x2 * sin, x1 * sin + x2 * cos], -1)


def _rope_2d(x, row, col):
    n = x.shape[-1] // 2
    return jnp.concatenate([_rope_1d(x[..., :n], row), _rope_1d(x[..., n:], col)], -1)


def _diff_core(q, k, v, lam):
    scale = DIFF_HALF ** -0.5
    q = q.astype(F32)
    k = k.astype(F32)
    s1 = jnp.einsum('bqhd,bkhd->bhqk', q[..., :DIFF_HALF], k[..., :DIFF_HALF]) * scale
    s2 = jnp.einsum('bqhd,bkhd->bhqk', q[..., DIFF_HALF:], k[..., DIFF_HALF:]) * scale
    p = jax.nn.softmax(s1, -1) - lam * jax.nn.softmax(s2, -1)
    return jnp.einsum('bhqk,bkhd->bqhd', p, v.astype(F32))


def _diff_attention(q, k, v, qc, kc, vc, lam_vecs, norm_g, lam_init, ctx_out):
    bsz, lat_len = q.shape[:2]
    row, col = _grid_pos(lat_len)
    rot = lambda z: jnp.concatenate([_rope_2d(z[..., :DIFF_HALF], row, col),
                                     _rope_2d(z[..., DIFF_HALF:], row, col)], -1)
    q = rot(q)
    k = rot(k)
    lv = lam_vecs.astype(F32)
    lam = jnp.exp(jnp.sum(lv[0] * lv[1])) - jnp.exp(jnp.sum(lv[2] * lv[3])) + lam_init
    k_all = jnp.concatenate([k, kc.astype(F32)], 1)
    v_all = jnp.concatenate([v.astype(F32), vc.astype(F32)], 1)
    nb = lat_len // Q_BLOCK
    qb = jnp.moveaxis(q.reshape(bsz, nb, Q_BLOCK, GROUP_HEADS, HEAD_DIM), 1, 0)
    ob = lax.map(lambda qq: _diff_core(qq, k_all, v_all, lam), qb)
    o = jnp.moveaxis(ob, 0, 1).reshape(bsz, lat_len, GROUP_HEADS, HEAD_DIM)
    post = lambda z: (_rms_norm(z, norm_g) * (1.0 - lam_init)).reshape(z.shape[0], z.shape[1], GW)
    out_ctx = post(_diff_core(qc, kc, vc, lam)) if ctx_out else None
    return post(o), out_ctx


def _wkv7_scan(r, w, k, v, kk, a, s0, reverse):
    def step(s, inp):
        r_t, w_t, k_t, v_t, kk_t, a_t = inp
        sa = jnp.einsum('bhvk,bhk->bhv', s, kk_t)
        s = (s * w_t[:, :, None, :] - sa[..., None] * (kk_t * a_t)[:, :, None, :]
             + v_t[..., None] * k_t[:, :, None, :])
        return s, jnp.einsum('bhvk,bhk->bhv', s, r_t)
    xs = tuple(jnp.moveaxis(z, 1, 0) for z in (r, w, k, v, kk, a))
    s, y = lax.scan(step, s0, xs, reverse=reverse)
    return jnp.moveaxis(y, 0, 1), s


def _rwkv7(f_lat, f_ctx, mu, w0, w2, a0, a2, g2, k_k, k_a, r_k, ln_g, ln_b, ctx_out):
    shift_w = jnp.stack([mu[0], 1.0 - mu[0] - mu[1], mu[1]])

    def prep(f):
        f = _dwconv_centred(f, shift_w).astype(F32)
        r, k, v, wd, ad, gd = _split(f, [GW, GW, GW, 2 * DECAY_LORA, 2 * ICLR_LORA, GATE_LORA])
        bsz, t = f.shape[:2]
        wd = jnp.tanh(wd.reshape(bsz, t, 2, DECAY_LORA))
        ad = ad.reshape(bsz, t, 2, ICLR_LORA)
        w_raw = w0 + jnp.einsum('btdr,drc->btdc', wd, w2)
        decay = jnp.exp(-jnp.exp(-jax.nn.softplus(-w_raw) - 0.5))
        a = jax.nn.sigmoid(a0 + jnp.einsum('btdr,drc->btdc', ad, a2))
        g = jax.nn.sigmoid(gd) @ g2
        kk = _l2norm(_heads(k * k_k))
        kd = k[:, :, None] * (1.0 + (a - 1.0) * k_a)
        return r, v, g, kk, decay, a, kd

    def run(p, d, s0, rev):
        r, v, g, kk, decay, a, kd = p
        return _wkv7_scan(_heads(r), _heads(decay[:, :, d]), _heads(kd[:, :, d]), _heads(v),
                          kk, _heads(a[:, :, d]), s0, rev)

    def post(p, ys):
        r, v, g, kk, decay, a, kd = p
        bsz, t = r.shape[:2]
        y = ys[0] + ys[1]
        m = y.mean(-1, keepdims=True)
        var = jnp.square(y - m).mean(-1, keepdims=True)
        yn = ((y - m) * lax.rsqrt(var + RWKV_GN_EPS)).reshape(bsz, t, GW) * ln_g + ln_b
        rh, vh = _heads(r), _heads(v)
        bonus = ((rh * _heads(kd[:, :, 0]) * r_k).sum(-1, keepdims=True) * vh
                 + (rh * _heads(kd[:, :, 1]) * r_k).sum(-1, keepdims=True) * vh)
        return (yn + bonus.reshape(bsz, t, GW)) * g

    pl_, pc = prep(f_lat), prep(f_ctx)
    s0 = jnp.zeros((f_lat.shape[0], GROUP_HEADS, HEAD_DIM, HEAD_DIM), F32)
    y_lat, y_ctx = [], []
    for d, rev in ((0, False), (1, True)):
        yc, sc = run(pc, d, s0, rev)
        yl, _ = run(pl_, d, sc, rev)
        y_lat.append(yl)
        y_ctx.append(yc)
    out_ctx = post(pc, y_ctx) if ctx_out else None
    return post(pl_, y_lat), out_ctx


def _gdn_chunked(q, k, v, beta, g, s0):
    bsz, t, h, dk = q.shape
    dv = v.shape[-1]
    n = t // GDN_CHUNK
    ch = lambda z: jnp.moveaxis(z.reshape(bsz, n, GDN_CHUNK, h, *z.shape[3:]), 3, 2)
    q, k, v, beta, g = ch(q), ch(k), ch(v), ch(beta), ch(g)
    gc = jnp.cumsum(g, axis=-1)
    i = jnp.arange(GDN_CHUNK)
    incl = i[:, None] >= i[None, :]
    strict = i[:, None] > i[None, :]
    decay = jnp.exp(jnp.where(incl, gc[..., :, None] - gc[..., None, :], -jnp.inf))
    kb = k * beta[..., None]
    a_low = jnp.where(strict, jnp.einsum('bnhid,bnhjd->bnhij', kb, k) * decay, 0.0)
    tmat = a_low + jnp.eye(GDN_CHUNK, dtype=F32)
    rhs = jnp.concatenate([v * beta[..., None], kb * jnp.exp(gc)[..., None]], -1)
    sol = lax.linalg.triangular_solve(tmat, rhs, left_side=True, lower=True, unit_diagonal=True)
    u, w = sol[..., :dv], sol[..., dv:]
    qk = jnp.where(incl, jnp.einsum('bnhid,bnhjd->bnhij', q, k) * decay, 0.0)
    qg = q * jnp.exp(gc)[..., None]
    kg = k * jnp.exp(gc[..., -1:] - gc)[..., None]
    glast = jnp.exp(gc[..., -1])

    def step(s, xs):
        qg_i, kg_i, u_i, w_i, qk_i, gl_i = xs
        v_new = u_i - jnp.einsum('bhcd,bhdv->bhcv', w_i, s)
        o = jnp.einsum('bhcd,bhdv->bhcv', qg_i, s) + jnp.einsum('bhij,bhjv->bhiv', qk_i, v_new)
        s = s * gl_i[..., None, None] + jnp.einsum('bhcd,bhcv->bhdv', kg_i, v_new)
        return s, o

    xs = tuple(jnp.moveaxis(z, 1, 0) for z in (qg, kg, u, w, qk, glast))
    s, o = lax.scan(step, s0, xs)
    o = jnp.moveaxis(jnp.moveaxis(o, 0, 1), 2, 3).reshape(bsz, t, h, dv)
    return o, s


def _gated_deltanet(f_lat, f_ctx, conv_w, a_log, dt_bias, norm_g, ctx_out):
    def prep(f):
        qkv, gate, ab = _split(f, [3 * GW, GW, 4 * GROUP_HEADS])
        qkv = jax.nn.silu(_dwconv_centred(qkv, conv_w).astype(F32))
        q, k, v = [_heads(z) for z in _split(qkv, [GW, GW, GW])]
        q = _l2norm(q) * HEAD_DIM ** -0.5
        k = _l2norm(k)
        ab = ab.astype(F32).reshape(f.shape[0], f.shape[1], 2, 2, GROUP_HEADS)
        log_alpha = -jnp.exp(a_log) * jax.nn.softplus(ab[:, :, 0] + dt_bias)
        beta = jax.nn.sigmoid(ab[:, :, 1])
        return q, k, v, gate, log_alpha, beta

    def run(p, d, s0):
        tr = (lambda z: jnp.flip(z, 1)) if d == 1 else (lambda z: z)
        q, k, v, gate, log_alpha, beta = p
        o, s = _gdn_chunked(tr(q), tr(k), tr(v), tr(beta[:, :, d]), tr(log_alpha[:, :, d]), s0)
        return tr(o), s

    def post(p, os_):
        gate = p[3]
        o = os_[0] + os_[1]
        y = _rms_norm(o, norm_g) * jax.nn.silu(_heads(gate.astype(F32)))
        return y.reshape(o.shape[0], o.shape[1], GW)

    pl_, pc = prep(f_lat), prep(f_ctx)
    s0 = jnp.zeros((f_lat.shape[0], GROUP_HEADS, HEAD_DIM, HEAD_DIM), F32)
    o_lat, o_ctx = [], []
    for d in range(2):
        oc, sc = run(pc, d, s0)
        ol, _ = run(pl_, d, sc)
        o_lat.append(ol)
        o_ctx.append(oc)
    out_ctx = post(pc, o_ctx) if ctx_out else None
    return post(pl_, o_lat), out_ctx


def _softmax_attn(q, k, v):
    s = jnp.einsum('bqhd,bkhd->bhqk', q.astype(F32), k.astype(F32)) * HEAD_DIM ** -0.5
    return jnp.einsum('bhqk,bkhd->bqhd', jax.nn.softmax(s, -1), v.astype(F32))


def _neighbourhood_attention(q, k, v, qc, kc, vc, rpb, ctx_out):
    bsz, lat_len = q.shape[:2]
    rows = lat_len // GRID_W
    kh = min(WIN_H, rows)
    scale = HEAD_DIM ** -0.5
    grid = lambda z: z.astype(F32).reshape(bsz, rows, GRID_W, GROUP_HEADS, HEAD_DIM)
    qg, kg, vg = grid(q), grid(k), grid(v)
    kc32, vc32 = kc.astype(F32), vc.astype(F32)
    rpb = rpb.astype(F32)
    cq = jnp.arange(GRID_W)
    c_start = jnp.clip(cq - WIN_W // 2, 0, GRID_W - WIN_W)
    col_ok = (cq[None, :] >= c_start[:, None]) & (cq[None, :] < c_start[:, None] + WIN_W)
    d_col = jnp.clip(cq[None, :] - cq[:, None], -(WIN_W - 1), WIN_W - 1) + (WIN_W - 1)

    def row_block(r):
        rs = jnp.clip(r - kh // 2, 0, rows - kh)
        q_r = lax.dynamic_index_in_dim(qg, r, axis=1, keepdims=False)
        k_r = lax.dynamic_slice_in_dim(kg, rs, kh, axis=1)
        v_r = lax.dynamic_slice_in_dim(vg, rs, kh, axis=1)
        s = jnp.einsum('bqhd,bkwhd->bqhkw', q_r, k_r) * scale
        d_row = rs + jnp.arange(kh) - r + (WIN_H - 1)
        bias = rpb[:, d_row[None, :, None], d_col[:, None, :]]
        s = jnp.where(col_ok[:, None, None, :], s + jnp.transpose(bias, (1, 0, 2, 3)), -jnp.inf)
        s_c = jnp.einsum('bqhd,bchd->bqhc', q_r, kc32) * scale
        p = jax.nn.softmax(jnp.concatenate(
            [s.reshape(bsz, GRID_W, GROUP_HEADS, kh * GRID_W), s_c], -1), -1)
        p_win = p[..., :kh * GRID_W].reshape(bsz, GRID_W, GROUP_HEADS, kh, GRID_W)
        return (jnp.einsum('bqhkw,bkwhd->bqhd', p_win, v_r)
                + jnp.einsum('bqhc,bchd->bqhd', p[..., kh * GRID_W:], vc32))

    o = lax.map(row_block, jnp.arange(rows))
    out_lat = jnp.moveaxis(o, 0, 1).reshape(bsz, lat_len, GW)
    out_ctx = _softmax_attn(qc, kc, vc).reshape(bsz, qc.shape[1], GW) if ctx_out else None
    return out_lat, out_ctx


def _qkv_heads(p):
    return [_heads(z) for z in _split(p, [GW, GW, GW])]


PEER_STAT_ROWS = 4 * PEER_HEADS
LOG2E = 1.4426950408889634
assert PEER_TOPK == 16


def _topk_rows(x, k):
    rows = []
    cur = x
    for i in range(k):
        m = jnp.max(cur, axis=0, keepdims=True)
        rows.append(m)
        if i + 1 < k:
            cur = jnp.where(cur == m, NEG_BIG, cur)
    return rows


def _peer_score_kernel(x_ref, wq_ref, keys_ref, s_ref, st_ref):
    q = jnp.dot(x_ref[...], wq_ref[...], preferred_element_type=F32).astype(BF16)
    stats = []
    for h in range(PEER_HEADS):
        tops = []
        for p in range(2):
            hp = 2 * h + p
            s_t = lax.dot_general(keys_ref[hp], q[:, hp * PEER_HALF:(hp + 1) * PEER_HALF],
                                  (((1,), (1,)), ((), ())), preferred_element_type=F32)
            s_ref[hp] = s_t
            tops.append(_topk_rows(s_t, PEER_TOPK + 1))
        a, b = tops
        pad = [jnp.full_like(a[0], NEG_BIG)] * 7
        b_head = jnp.concatenate(b[:8], axis=0)
        cand = jnp.concatenate([a[0] + jnp.concatenate(b + pad, axis=0)]
                               + [a[i] + b_head for i in range(1, 8)]
                               + [jnp.concatenate(a[8:] + pad, axis=0) + b[0]], axis=0)
        best_cand = _topk_rows(cand, PEER_TOPK + 1)
        kth, runner_up = best_cand[PEER_TOPK - 1], best_cand[PEER_TOPK]
        best = a[0] + b[0]
        z = jnp.sum(jnp.where(cand >= kth, jnp.exp(cand - best), 0.0), axis=0, keepdims=True)
        stats += [0.5 * (kth + runner_up), tops[0][0], tops[1][0], 1.0 / z]
    st_ref[...] = jnp.concatenate(stats, axis=0)


def _peer_expert_kernel(x_ref, s_ref, st_ref, u_ref, vt_ref, o_ref, e_ref, thr_ref, acc_ref, w_ref,
                        *, rows_per_step, n_chunks):
    j = pl.program_id(1)
    cur = lax.rem(j, 2)

    @pl.when(j == 0)
    def _():
        acc_ref[...] = jnp.zeros_like(acc_ref)
        w_ref[1] = jnp.zeros(w_ref.shape[1:], w_ref.dtype)
        for h in range(PEER_HEADS):
            a0 = st_ref[4 * h + 1:4 * h + 2, :]
            b0 = st_ref[4 * h + 2:4 * h + 3, :]
            rz = st_ref[4 * h + 3:4 * h + 4, :]
            e_ref[2 * h] = (s_ref[2 * h] - a0 + jnp.log(rz)) * LOG2E
            e_ref[2 * h + 1] = (s_ref[2 * h + 1] - b0) * LOG2E
            thr_ref[h] = (st_ref[4 * h:4 * h + 1, :] - s_ref[2 * h] - b0) * LOG2E

    @pl.when(j < n_chunks)
    def _():
        slab = 2 * PEER_KEYS
        n_slabs = rows_per_step * PEER_KEYS // slab
        act_slab = lambda n: lax.dot_general(u_ref[n * slab:(n + 1) * slab, :], x_ref[...], (((1,), (1,)), ((), ())),
                                             preferred_element_type=F32)
        acts = [act_slab(n) for n in range(n_slabs)]
        acc_ref[...] += jnp.dot(vt_ref[...], w_ref[1 - cur], preferred_element_type=F32)
        for il in range(rows_per_step):
            i = j * rows_per_step + il
            o = il * PEER_KEYS
            act = acts[o // slab][o % slab:o % slab + PEER_KEYS]
            act = 0.5 * act * (1.0 + lax.erf(act * (2.0 ** -0.5)))
            gate = None
            for h in range(PEER_HEADS):
                l2 = e_ref[2 * h + 1]
                term = jnp.where(l2 >= thr_ref[h, pl.ds(i, 1), :], jnp.exp2(l2 + e_ref[2 * h, pl.ds(i, 1), :]), 0.0)
                gate = term if gate is None else gate + term
            w_ref[cur, o:o + PEER_KEYS, :] = (act * gate).astype(BF16)

    @pl.when(j == n_chunks)
    def _():
        o_ref[...] = (acc_ref[...] + jnp.dot(vt_ref[...], w_ref[1 - cur], preferred_element_type=F32)).T


def _peer_pallas(h, w_q, keys, u_tab, vt_tab, *, tb, rows_per_step):
    n_tok, d = h.shape
    nhp = 2 * PEER_HEADS
    s, st = pl.pallas_call(
        _peer_score_kernel,
        name="peer_scores",
        grid=(n_tok // tb,),
        in_specs=[pl.BlockSpec((tb, d), lambda i: (i, 0)),
                  pl.BlockSpec(w_q.shape, lambda i: (0, 0)),
                  pl.BlockSpec(keys.shape, lambda i: (0, 0, 0))],
        out_specs=[pl.BlockSpec((nhp, PEER_KEYS, tb), lambda i: (0, 0, i)),
                   pl.BlockSpec((PEER_STAT_ROWS, tb), lambda i: (0, i))],
        out_shape=[jax.ShapeDtypeStruct((nhp, PEER_KEYS, n_tok), F32),
                   jax.ShapeDtypeStruct((PEER_STAT_ROWS, n_tok), F32)],
        compiler_params=pltpu.CompilerParams(dimension_semantics=("arbitrary",),
                                             vmem_limit_bytes=V7X_VMEM_LIMIT_BYTES),
    )(h, w_q, keys)
    ec = rows_per_step * PEER_KEYS
    n_chunks = PEER_EXPERTS // ec
    return pl.pallas_call(
        functools.partial(_peer_expert_kernel, rows_per_step=rows_per_step, n_chunks=n_chunks),
        name="peer_experts",
        grid=(n_tok // tb, n_chunks + 1),
        in_specs=[pl.BlockSpec((tb, d), lambda i, j: (i, 0)),
                  pl.BlockSpec((nhp, PEER_KEYS, tb), lambda i, j: (0, 0, i)),
                  pl.BlockSpec((PEER_STAT_ROWS, tb), lambda i, j: (0, i)),
                  pl.BlockSpec((ec, d), lambda i, j: (jnp.minimum(j, n_chunks - 1), 0)),
                  pl.BlockSpec((d, ec), lambda i, j: (0, jnp.maximum(j - 1, 0)))],
        out_specs=pl.BlockSpec((tb, d), lambda i, j: (i, 0)),
        out_shape=jax.ShapeDtypeStruct((n_tok, d), F32),
        scratch_shapes=[pltpu.VMEM((nhp, PEER_KEYS, tb), F32),
                        pltpu.VMEM((PEER_HEADS, PEER_KEYS, tb), F32),
                        pltpu.VMEM((d, tb), F32),
                        pltpu.VMEM((2, ec, tb), BF16)],
        compiler_params=pltpu.CompilerParams(dimension_semantics=("arbitrary", "arbitrary"),
                                             vmem_limit_bytes=V7X_VMEM_LIMIT_BYTES),
    )(h, s, st, u_tab, vt_tab)


def _peer(h, w_q, keys, u_tab, vt_tab):
    bsz, t, d = h.shape
    n_tok = bsz * t
    tb = 512 if n_tok % 512 == 0 else 256
    out = _peer_pallas(h.reshape(n_tok, d).astype(BF16), w_q, keys, u_tab, vt_tab, tb=tb, rows_per_step=16)
    return out.reshape(bsz, t, d)


ROW_BLOCK = 256
HIGHEST = lax.Precision.HIGHEST


def _dot_f32(a, b):
    return jnp.dot(a, b, precision=HIGHEST, preferred_element_type=F32)


def _segment_ones(n, seg, dtype):
    r = lax.broadcasted_iota(jnp.int32, (n, n), 0) // seg
    c = lax.broadcasted_iota(jnp.int32, (n, n), 1) // seg
    return jnp.where(r == c, 1.0, 0.0).astype(dtype)


def _shifted_rows(x, prev_row, next_row):
    t = x.shape[0]
    rows = lax.broadcasted_iota(jnp.int32, x.shape, 0)
    xm = jnp.where(rows == 0, prev_row, pltpu.roll(x, 1, axis=0))
    xp = jnp.where(rows == t - 1, next_row, pltpu.roll(x, t - 1, axis=0))
    return xm, xp


def _segment_edge_flags(i, n_blocks, ctx_blocks):
    is_start = jnp.logical_or(i == 0, i == ctx_blocks)
    is_end = jnp.logical_or(i == ctx_blocks - 1, i == n_blocks - 1)
    return jnp.where(is_start, 0.0, 1.0), jnp.where(is_end, 0.0, 1.0)


def _halo_specs(width, tr):
    g = tr // 8
    prev = pl.BlockSpec((1, 8, width), lambda b, i: (b, jnp.maximum(i * g - 1, 0), 0))
    nxt = lambda n_groups: pl.BlockSpec((1, 8, width), lambda b, i: (b, jnp.minimum((i + 1) * g, n_groups - 1), 0))
    return prev, nxt


def _softplus(z):
    return jnp.maximum(z, 0.0) + jnp.log1p(jnp.exp(-jnp.abs(z)))


def _rwkv_prep_kernel(x_ref, xprev_ref, xnext_ref, mu_ref, w0_ref, w2_ref, a0_ref, a2_ref, g2_ref,
                      kk_ref, ka_ref, rk_ref,
                      r_out, v_out, kkn_out, g_out, bonus_out, w_out, b_out, kt_out, *, ctx_blocks):
    i = pl.program_id(1)
    keep_prev, keep_next = _segment_edge_flags(i, pl.num_programs(1), ctx_blocks)
    x = x_ref[0]
    xm, xp = _shifted_rows(x, xprev_ref[0, 7:8, :] * keep_prev, xnext_ref[0, 0:1, :] * keep_next)
    mu0 = mu_ref[0:1, :]
    mu1 = mu_ref[1:2, :]
    f = xm * mu0 + x * (1.0 - mu0 - mu1) + xp * mu1
    r = f[:, 0:GW]
    k = f[:, GW:2 * GW]
    v = f[:, 2 * GW:3 * GW]
    o = 3 * GW
    wd = jnp.tanh(f[:, o:o + 2 * DECAY_LORA])
    ad = f[:, o + 2 * DECAY_LORA:o + 2 * DECAY_LORA + 2 * ICLR_LORA]
    gd = f[:, o + 2 * DECAY_LORA + 2 * ICLR_LORA:]
    w_raw = w0_ref[...] + _dot_f32(wd, w2_ref[...])
    decay = jnp.exp(-jnp.exp(-_softplus(-w_raw) - 0.5))
    a = jax.nn.sigmoid(a0_ref[...] + _dot_f32(ad, a2_ref[...]))
    g = _dot_f32(jax.nn.sigmoid(gd), g2_ref[...])
    head_sum = _segment_ones(GW, HEAD_DIM, F32)
    kx = k * kk_ref[...]
    kkn = kx * lax.rsqrt(_dot_f32(kx * kx, head_sum) + 1e-6)
    kd_sum = jnp.zeros_like(k)
    for d in range(2):
        a_d = a[:, d * GW:(d + 1) * GW]
        kd = k * (1.0 + (a_d - 1.0) * ka_ref[...])
        kd_sum = kd_sum + kd
        w_out[d, 0] = decay[:, d * GW:(d + 1) * GW]
        b_out[d, 0] = kkn * a_d
        kt_out[d, 0] = kd
    r_out[0] = r
    v_out[0] = v
    kkn_out[0] = kkn
    g_out[0] = g
    bonus_out[0] = _dot_f32(r * kd_sum * rk_ref[...], head_sum) * v


def _rwkv_scan_kernel(rf_ref, vf_ref, kkf_ref, wf_ref, bf_ref, ktf_ref,
                      rb_ref, vb_ref, kkb_ref, wb_ref, bb_ref, ktb_ref,
                      yf_ref, yb_ref, st_ref):
    c = pl.program_id(0)
    bsz, tt, _ = rf_ref.shape
    n_pairs = GW // 128

    @pl.when(c == 0)
    def _():
        st_ref[...] = jnp.zeros_like(st_ref)

    ones_bd = _segment_ones(128, HEAD_DIM, BF16)
    rows = lax.broadcasted_iota(jnp.int32, (HEAD_DIM, 128), 0)
    lanes = lax.broadcasted_iota(jnp.int32, (HEAD_DIM, 128), 1)
    diag = jnp.where(lanes % HEAD_DIM == rows, 1.0, 0.0)
    dirs = ((rf_ref, vf_ref, kkf_ref, wf_ref, bf_ref, ktf_ref, yf_ref),
            (rb_ref, vb_ref, kkb_ref, wb_ref, bb_ref, ktb_ref, yb_ref))

    n_groups = tt // 8

    def group(tg, carry):
        chains = [(b, p) for b in range(bsz) for p in range(n_pairs)]
        nc = len(chains)
        base = [pl.multiple_of((tg if d == 0 else n_groups - 1 - tg) * 8, 8) for d in range(2)]
        tiles = [[], []]
        states = [[], []]
        for d in range(2):
            r_ref, v_ref, kk_ref, w_ref, b_ref, kt_ref, _ = dirs[d]
            for b, p in chains:
                cols = slice(p * 128, (p + 1) * 128)
                tiles[d].append(tuple(ref[b, pl.ds(base[d], 8), cols] for ref in (r_ref, v_ref, kk_ref))
                                + tuple(ref[0, b, pl.ds(base[d], 8), cols] for ref in (w_ref, b_ref, kt_ref)))
                states[d].append(st_ref[d, b * n_pairs + p])
        ys = [[[None] * 8 for _ in chains] for _ in range(2)]
        pending = [None, None]

        def emit_outputs(d, yb, jj):
            for ci in range(nc):
                ys[d][ci][jj] = jnp.sum(yb[ci * HEAD_DIM:(ci + 1) * HEAD_DIM] * diag, axis=0, keepdims=True)

        for j in range(8):
            for d in range(2):
                jj = j if d == 0 else 7 - j
                row = lambda a: a[jj:jj + 1, :]
                parts = ([s * row(t[2]) for s, t in zip(states[d], tiles[d])]
                         + [diag * row(t[1]) for t in tiles[d]])
                if pending[d] is not None:
                    parts += pending[d][0]
                res = jnp.dot(jnp.concatenate(parts, axis=0).astype(BF16), ones_bd, preferred_element_type=F32)
                for ci, t in enumerate(tiles[d]):
                    sa = res[ci * HEAD_DIM:(ci + 1) * HEAD_DIM]
                    vcol = res[(nc + ci) * HEAD_DIM:(nc + ci + 1) * HEAD_DIM]
                    states[d][ci] = states[d][ci] * row(t[3]) - sa * row(t[4]) + vcol * row(t[5])
                if pending[d] is not None:
                    emit_outputs(d, res[2 * nc * HEAD_DIM:], pending[d][1])
                pending[d] = ([s * row(t[0]) for s, t in zip(states[d], tiles[d])], jj)
        for d in range(2):
            q = jnp.concatenate(pending[d][0], axis=0).astype(BF16)
            emit_outputs(d, jnp.dot(q, ones_bd, preferred_element_type=F32), pending[d][1])
            y_ref = dirs[d][6]
            for ci, (b, p) in enumerate(chains):
                st_ref[d, b * n_pairs + p] = states[d][ci]
                y_ref[b, pl.ds(base[d], 8), p * 128:(p + 1) * 128] = jnp.concatenate(ys[d][ci], axis=0)
        return carry

    lax.fori_loop(0, n_groups, group, 0)


def _rwkv_post_kernel(yf_ref, yb_ref, bonus_ref, g_ref, lng_ref, lnb_ref, o_ref):
    y = yf_ref[0] + yb_ref[0]
    head_mean = _segment_ones(GW, HEAD_DIM, F32) * (1.0 / HEAD_DIM)
    m = _dot_f32(y, head_mean)
    yc = y - m
    var = _dot_f32(yc * yc, head_mean)
    yn = yc * lax.rsqrt(var + RWKV_GN_EPS) * lng_ref[...] + lnb_ref[...]
    o_ref[0] = (yn + bonus_ref[0]) * g_ref[0]


def _block_diag2(m):
    z = jnp.zeros_like(m[0])
    return jnp.concatenate([jnp.concatenate([m[0], z], 1), jnp.concatenate([z, m[1]], 1)], 0)


def _rwkv7_pallas(f, mu, w0, w2, a0, a2, g2, k_k, k_a, r_k, ln_g, ln_b, *, ctx_len):
    bsz, seq, cols = f.shape
    tr = ROW_BLOCK
    assert ctx_len % tr == 0 and seq % tr == 0
    nb, ctx_blocks = seq // tr, ctx_len // tr
    prev_spec, next_spec = _halo_specs(cols, tr)
    row2 = lambda a: a.reshape(1, -1).astype(F32)
    full = lambda a: pl.BlockSpec(a.shape, lambda b, i: (0,) * a.ndim)
    params = [mu, row2(w0), _block_diag2(w2), row2(a0), _block_diag2(a2), g2, row2(k_k), row2(k_a), row2(r_k)]
    act = jax.ShapeDtypeStruct((bsz, seq, GW), F32)
    act2 = jax.ShapeDtypeStruct((2, bsz, seq, GW), F32)
    blk = pl.BlockSpec((1, tr, GW), lambda b, i: (b, i, 0))
    blk2 = pl.BlockSpec((2, 1, tr, GW), lambda b, i: (0, b, i, 0))
    r, v, kkn, g, bonus, w, bb, kt = pl.pallas_call(
        functools.partial(_rwkv_prep_kernel, ctx_blocks=ctx_blocks),
        name="rwkv_prep",
        grid=(bsz, nb),
        in_specs=[pl.BlockSpec((1, tr, cols), lambda b, i: (b, i, 0)), prev_spec, next_spec(seq // 8)]
                 + [full(p) for p in params],
        out_specs=[blk] * 5 + [blk2] * 3,
        out_shape=[act] * 5 + [act2] * 3,
        compiler_params=pltpu.CompilerParams(dimension_semantics=("arbitrary", "arbitrary"),
                                             vmem_limit_bytes=V7X_VMEM_LIMIT_BYTES),
    )(f, f, f, *params)

    def bwd_block(c):
        return jnp.where(c < ctx_blocks, ctx_blocks - 1 - c, nb - 1 - (c - ctx_blocks))
    fwd = pl.BlockSpec((bsz, tr, GW), lambda c: (0, c, 0))
    bwd = pl.BlockSpec((bsz, tr, GW), lambda c: (0, bwd_block(c), 0))
    fwd_d = pl.BlockSpec((1, bsz, tr, GW), lambda c: (0, 0, c, 0))
    bwd_d = pl.BlockSpec((1, bsz, tr, GW), lambda c: (1, 0, bwd_block(c), 0))
    yf, yb = pl.pallas_call(
        _rwkv_scan_kernel,
        name="rwkv_scan",
        grid=(nb,),
        in_specs=[fwd, fwd, fwd, fwd_d, fwd_d, fwd_d, bwd, bwd, bwd, bwd_d, bwd_d, bwd_d],
        out_specs=[fwd, bwd],
        out_shape=[act, act],
        scratch_shapes=[pltpu.VMEM((2, bsz * (GW // 128), HEAD_DIM, 128), F32)],
        compiler_params=pltpu.CompilerParams(dimension_semantics=("arbitrary",),
                                             vmem_limit_bytes=V7X_VMEM_LIMIT_BYTES),
    )(r, v, kkn, w, bb, kt, r, v, kkn, w, bb, kt)

    return pl.pallas_call(
        _rwkv_post_kernel,
        name="rwkv_post",
        grid=(bsz, nb),
        in_specs=[blk, blk, blk, blk, full(row2(ln_g)), full(row2(ln_b))],
        out_specs=blk,
        out_shape=act,
        compiler_params=pltpu.CompilerParams(dimension_semantics=("arbitrary", "arbitrary")),
    )(yf, yb, bonus, g, row2(ln_g), row2(ln_b))


GDN_GATE_LANES = 128
GDN_PADDED_COLS = 4 * GW + GDN_GATE_LANES


def _gdn_prep_kernel(x_ref, xprev_ref, xnext_ref, ab_ref, conv_ref, alog_ref, dtb_ref,
                     q_out, k_out, v_out, gb_out, *, ctx_blocks):
    i = pl.program_id(1)
    keep_prev, keep_next = _segment_edge_flags(i, pl.num_programs(1), ctx_blocks)
    x = x_ref[0]
    xm, xp = _shifted_rows(x, xprev_ref[0, 7:8, :] * keep_prev, xnext_ref[0, 0:1, :] * keep_next)
    y = xm * conv_ref[0:1, :] + x * conv_ref[1:2, :] + xp * conv_ref[2:3, :]
    y = y * jax.nn.sigmoid(y)
    head_sum = _segment_ones(GW, HEAD_DIM, F32)
    q = y[:, 0:GW]
    k = y[:, GW:2 * GW]
    q_out[0] = q * lax.rsqrt(_dot_f32(q * q, head_sum) + 1e-6) * (HEAD_DIM ** -0.5)
    k_out[0] = k * lax.rsqrt(_dot_f32(k * k, head_sum) + 1e-6)
    v_out[0] = y[:, 2 * GW:3 * GW]
    ab = ab_ref[0]
    lane = lax.broadcasted_iota(jnp.int32, ab.shape, 1)
    log_alpha = -jnp.exp(alog_ref[...]) * _softplus(ab + dtb_ref[...])
    gb_out[0] = jnp.where(lane < 2 * GROUP_HEADS, log_alpha, jax.nn.sigmoid(ab))


def _gdn_chunk_kernel(qf_ref, kf_ref, vf_ref, gf_ref, qb_ref, kb_ref, vb_ref, gb_ref, of_ref, ob_ref, st_ref):
    i = pl.program_id(1)
    c = GDN_CHUNK
    n_chunks = qf_ref.shape[1] // c

    @pl.when(i == 0)
    def _():
        st_ref[...] = jnp.zeros_like(st_ref)

    r = lax.broadcasted_iota(jnp.int32, (c, c), 0)
    s = lax.broadcasted_iota(jnp.int32, (c, c), 1)
    eye = r == s
    ones_cc = jnp.ones((c, c), F32)
    incl = (r >= s, r <= s)
    strict = (r > s, r < s)
    levels = []
    b = 1
    while b < c:
        levels.append(jnp.logical_and(r // (2 * b) == s // (2 * b), r // b != s // b))
        b *= 2
    dirs = ((qf_ref, kf_ref, vf_ref, gf_ref, of_ref), (qb_ref, kb_ref, vb_ref, gb_ref, ob_ref))

    bdot = lambda x, y: jnp.dot(x.astype(BF16), y.astype(BF16), preferred_element_type=F32)
    bdot_nt = lambda x, y: lax.dot_general(x.astype(BF16), y.astype(BF16), _NT, preferred_element_type=F32)
    bdot_tn = lambda x, y: lax.dot_general(x.astype(BF16), y.astype(BF16), (((0,), (0,)), ((), ())),
                                           preferred_element_type=F32)

    heads = [(d, h) for d in range(2) for h in range(GROUP_HEADS)]
    row0 = lambda cc, d: (cc if d == 0 else n_chunks - 1 - cc) * c
    gates = {(cc, d): dirs[d][3][0, row0(cc, d):row0(cc, d) + c, :] for cc in range(n_chunks) for d in range(2)}
    cum = {key: _dot_f32(jnp.where(incl[key[1]], 1.0, 0.0), g) for key, g in gates.items()}
    tot = {key: _dot_f32(ones_cc, g) for key, g in gates.items()}
    chains = [(cc, d, h) for cc in range(n_chunks) for d, h in heads]
    st = {}
    for cc, d, h in chains:
        cols = slice(h * HEAD_DIM, (h + 1) * HEAD_DIM)
        lg = d * GROUP_HEADS + h
        q, k, v = (dirs[d][n][0, row0(cc, d):row0(cc, d) + c, cols] for n in range(3))
        gc = cum[cc, d][:, lg:lg + 1]
        st[cc, d, h] = dict(q=q, k=k, v=v, gc=gc, gt=tot[cc, d][:, lg:lg + 1],
                            beta=gates[cc, d][:, 2 * GROUP_HEADS + lg:2 * GROUP_HEADS + lg + 1],
                            gc_row=_dot_f32(ones_cc, jnp.where(eye, gc, 0.0)))
    for (cc, d, h), x in st.items():
        x["decay"] = jnp.exp(jnp.where(incl[d], x["gc"] - x["gc_row"], NEG_BIG))
        x["kb"] = x["k"] * x["beta"]
        x["a"] = jnp.where(strict[d], bdot_nt(x["kb"], x["k"]) * x["decay"], 0.0)
        x["qk"] = jnp.where(incl[d], bdot_nt(x["q"], x["k"]) * x["decay"], 0.0)
        x["inv"] = jnp.where(eye, 1.0, 0.0)
    for lvl in levels:
        for x in st.values():
            x["t"] = bdot(jnp.where(lvl, x["a"], 0.0), x["inv"])
        for x in st.values():
            x["inv"] = x["inv"] - bdot(x["inv"], x["t"])
    for x in st.values():
        x["eg"] = jnp.exp(x["gc"])
        x["sol"] = bdot(x["inv"], jnp.concatenate([x["v"] * x["beta"], x["kb"] * x["eg"]], axis=-1))
        x["qg"] = x["q"] * x["eg"]
        x["kg"] = x["k"] * jnp.exp(x["gt"] - x["gc"])
    state = {(d, h): st_ref[d, h] for d, h in heads}
    for cc in range(n_chunks):
        cur = [(key, st[(cc,) + key]) for key in heads]
        for key, x in cur:
            x["ws"] = bdot(x["sol"][:, HEAD_DIM:], state[key])
            x["qs"] = bdot(x["qg"], state[key])
        for key, x in cur:
            x["v_new"] = x["sol"][:, :HEAD_DIM] - x["ws"]
            x["o"] = x["qs"] + bdot(x["qk"], x["v_new"])
            x["upd"] = bdot_tn(x["kg"], x["v_new"])
        for key, x in cur:
            state[key] = state[key] * jnp.exp(x["gt"][0:1, :]) + x["upd"]
        for d in range(2):
            dirs[d][4][0, row0(cc, d):row0(cc, d) + c, :] = jnp.concatenate(
                [x["o"] for (dd, h), x in cur if dd == d], axis=-1)
    for d, h in heads:
        st_ref[d, h] = state[d, h]


def _gdn_post_kernel(of_ref, ob_ref, gate_ref, g_ref, o_ref):
    o = of_ref[0] + ob_ref[0]
    head_mean = _segment_ones(GW, HEAD_DIM, F32) * (1.0 / HEAD_DIM)
    gate = gate_ref[0]
    o_ref[0] = o * lax.rsqrt(_dot_f32(o * o, head_mean) + 1e-6) * g_ref[...] * (gate * jax.nn.sigmoid(gate))


def _gated_deltanet_pallas(f, conv_w, a_log, dt_bias, norm_g, *, ctx_len):
    bsz, seq, width = f.shape
    tr = ROW_BLOCK
    assert ctx_len % tr == 0 and seq % tr == 0 and tr % GDN_CHUNK == 0
    if width == GDN_COLS:
        f = jnp.pad(f, ((0, 0), (0, 0), (0, GDN_PADDED_COLS - GDN_COLS)))
    assert f.shape[2] == GDN_PADDED_COLS
    nb, ctx_blocks = seq // tr, ctx_len // tr
    lane_pad = lambda a: jnp.pad(a.reshape(1, -1).astype(F32), ((0, 0), (0, GDN_GATE_LANES - a.size)))
    prev_spec, next_spec = _halo_specs(3 * GW, tr)
    full = lambda a: pl.BlockSpec(a.shape, lambda b, i: (0,) * a.ndim)
    act = jax.ShapeDtypeStruct((bsz, seq, GW), F32)
    gact = jax.ShapeDtypeStruct((bsz, seq, GDN_GATE_LANES), F32)
    blk = pl.BlockSpec((1, tr, GW), lambda b, i: (b, i, 0))
    gblk = pl.BlockSpec((1, tr, GDN_GATE_LANES), lambda b, i: (b, i, 0))
    gate_view = pl.BlockSpec((1, tr, GW), lambda b, i: (b, i, 3))
    ab_view = pl.BlockSpec((1, tr, GDN_GATE_LANES), lambda b, i: (b, i, 4 * GW // GDN_GATE_LANES))
    params = [conv_w.astype(F32), lane_pad(a_log), lane_pad(dt_bias)]
    q, k, v, gb = pl.pallas_call(
        functools.partial(_gdn_prep_kernel, ctx_blocks=ctx_blocks),
        name="gdn_prep",
        grid=(bsz, nb),
        in_specs=[pl.BlockSpec((1, tr, 3 * GW), lambda b, i: (b, i, 0)), prev_spec, next_spec(seq // 8), ab_view]
                 + [full(p) for p in params],
        out_specs=[blk, blk, blk, gblk],
        out_shape=[act, act, act, gact],
        compiler_params=pltpu.CompilerParams(dimension_semantics=("arbitrary", "arbitrary"),
                                             vmem_limit_bytes=V7X_VMEM_LIMIT_BYTES),
    )(f, f, f, f, *params)

    def bwd_block(i):
        return jnp.where(i < ctx_blocks, ctx_blocks - 1 - i, nb - 1 - (i - ctx_blocks))
    bblk = pl.BlockSpec((1, tr, GW), lambda b, i: (b, bwd_block(i), 0))
    bgblk = pl.BlockSpec((1, tr, GDN_GATE_LANES), lambda b, i: (b, bwd_block(i), 0))
    of, ob = pl.pallas_call(
        _gdn_chunk_kernel,
        name="gdn_chunks",
        grid=(bsz, nb),
        in_specs=[blk, blk, blk, gblk, bblk, bblk, bblk, bgblk],
        out_specs=[blk, bblk],
        out_shape=[act, act],
        scratch_shapes=[pltpu.VMEM((2, GROUP_HEADS, HEAD_DIM, HEAD_DIM), F32)],
        compiler_params=pltpu.CompilerParams(dimension_semantics=("arbitrary", "arbitrary"),
                                             vmem_limit_bytes=V7X_VMEM_LIMIT_BYTES),
    )(q, k, v, gb, q, k, v, gb)

    g_row = jnp.tile(norm_g.reshape(1, HEAD_DIM).astype(F32), (1, GROUP_HEADS))
    return pl.pallas_call(
        _gdn_post_kernel,
        name="gdn_post",
        grid=(bsz, nb),
        in_specs=[blk, blk, gate_view, full(g_row)],
        out_specs=blk,
        out_shape=act,
        compiler_params=pltpu.CompilerParams(dimension_semantics=("arbitrary", "arbitrary")),
    )(of, ob, f, g_row)


ROPE_PAIR = DIFF_HALF // 4


def _rope_tables(seq, ctx_len, q_scale):
    n = jnp.arange(seq - ctx_len, dtype=jnp.int32)
    row, col = n // GRID_W, n % GRID_W
    i = jnp.arange(HEAD_DIM)
    grp = (i % DIFF_HALF) // (2 * ROPE_PAIR)
    inv = ROPE_BASE ** (-(i % ROPE_PAIR).astype(F32) / ROPE_PAIR)
    pos = jnp.where(grp[None, :] == 0, row[:, None], col[:, None]).astype(F32)
    ang = pos * inv[None, :]
    sign = jnp.where((i % (2 * ROPE_PAIR)) < ROPE_PAIR, -1.0, 1.0)
    cos = jnp.concatenate([jnp.ones((ctx_len, HEAD_DIM), F32), jnp.cos(ang)], 0)
    sin = jnp.concatenate([jnp.zeros((ctx_len, HEAD_DIM), F32), jnp.sin(ang) * sign], 0)
    cos = jnp.tile(cos, (1, GROUP_HEADS))
    sin = jnp.tile(sin, (1, GROUP_HEADS))
    return jnp.concatenate([cos * q_scale, cos], 1), jnp.concatenate([sin * q_scale, sin], 1)


def _qkv_prep_kernel(p_ref, cos_ref, sin_ref, q_out, k_out, v_out):
    qk = p_ref[0, :, 0:2 * GW]
    width = 2 * GW
    lane = lax.broadcasted_iota(jnp.int32, qk.shape, 1)
    partner = jnp.where(lane % (2 * ROPE_PAIR) < ROPE_PAIR,
                        pltpu.roll(qk, width - ROPE_PAIR, axis=1), pltpu.roll(qk, ROPE_PAIR, axis=1))
    rot = qk * cos_ref[...] + partner * sin_ref[...]
    q_out[0] = rot[:, 0:GW].astype(BF16)
    v = p_ref[0, :, 2 * GW:3 * GW]
    for h in range(GROUP_HEADS):
        k_out[0, h] = rot[:, GW + h * HEAD_DIM:GW + (h + 1) * HEAD_DIM].astype(BF16)
        v_out[0, h] = v[:, h * HEAD_DIM:(h + 1) * HEAD_DIM].astype(BF16)


def _qkv_prep(p, cos, sin):
    bsz, seq, _ = p.shape
    tr = ROW_BLOCK
    head_major = jax.ShapeDtypeStruct((bsz, GROUP_HEADS, seq, HEAD_DIM), BF16)
    hm_spec = pl.BlockSpec((1, GROUP_HEADS, tr, HEAD_DIM), lambda b, i: (b, 0, i, 0))
    return pl.pallas_call(
        _qkv_prep_kernel,
        name="qkv_prep",
        grid=(bsz, seq // tr),
        in_specs=[pl.BlockSpec((1, tr, 3 * GW), lambda b, i: (b, i, 0)),
                  pl.BlockSpec((tr, 2 * GW), lambda b, i: (i, 0)),
                  pl.BlockSpec((tr, 2 * GW), lambda b, i: (i, 0))],
        out_specs=[pl.BlockSpec((1, tr, GW), lambda b, i: (b, i, 0)), hm_spec, hm_spec],
        out_shape=[jax.ShapeDtypeStruct((bsz, seq, GW), BF16), head_major, head_major],
        compiler_params=pltpu.CompilerParams(dimension_semantics=("arbitrary", "arbitrary")),
    )(p, cos, sin)


_NT = (((1,), (1,)), ((), ()))


def _softmax_pv(s, v):
    m = jnp.max(s, axis=-1, keepdims=True)
    e = jnp.exp(s - m)
    return jnp.dot(e.astype(BF16), v, preferred_element_type=F32) / jnp.sum(e, axis=-1, keepdims=True)


def _diff_attn_kernel(q_ref, k_ref, v_ref, lam_ref, g_ref, o_ref, *, ctx_blocks, ctx_len):
    i = pl.program_id(1)
    lv = lam_ref[...]
    lam_init = lv[4:5, 0:1]
    lam = (jnp.exp(jnp.sum(lv[0:1] * lv[1:2], axis=-1, keepdims=True))
           - jnp.exp(jnp.sum(lv[2:3] * lv[3:4], axis=-1, keepdims=True)) + lam_init)
    lane = lax.broadcasted_iota(jnp.int32, (q_ref.shape[1], HEAD_DIM), 1)

    def attend(n_keys):
        outs = []
        for h in range(GROUP_HEADS):
            qh = q_ref[0, :, h * HEAD_DIM:(h + 1) * HEAD_DIM]
            kh = k_ref[0, h, 0:n_keys, :]
            vh = v_ref[0, h, 0:n_keys, :]
            zero = jnp.zeros_like(qh)
            s1 = lax.dot_general(jnp.where(lane < DIFF_HALF, qh, zero), kh, _NT, preferred_element_type=F32)
            s2 = lax.dot_general(jnp.where(lane >= DIFF_HALF, qh, zero), kh, _NT, preferred_element_type=F32)
            o = _softmax_pv(s1, vh) - lam * _softmax_pv(s2, vh)
            o = o * lax.rsqrt(jnp.mean(o * o, axis=-1, keepdims=True) + 1e-6) * g_ref[...] * (1.0 - lam_init)
            outs.append(o)
        o_ref[0] = jnp.concatenate(outs, axis=-1)

    @pl.when(i < ctx_blocks)
    def _():
        attend(ctx_len)

    @pl.when(i >= ctx_blocks)
    def _():
        attend(k_ref.shape[2])


def _diff_attention_pallas(q, k, v, lam_vecs, norm_g, *, ctx_len, lam_init):
    bsz, seq, _ = q.shape
    tq = ROW_BLOCK
    kv_spec = pl.BlockSpec((1, GROUP_HEADS, seq, HEAD_DIM), lambda b, i: (b, 0, 0, 0))
    lam_rows = jnp.concatenate([lam_vecs.astype(F32), jnp.full((1, lam_vecs.shape[1]), lam_init, F32)], 0)
    return pl.pallas_call(
        functools.partial(_diff_attn_kernel, ctx_blocks=ctx_len // tq, ctx_len=ctx_len),
        name="diff_attn",
        grid=(bsz, seq // tq),
        in_specs=[pl.BlockSpec((1, tq, GW), lambda b, i: (b, i, 0)), kv_spec, kv_spec,
                  pl.BlockSpec(lam_rows.shape, lambda b, i: (0, 0)),
                  pl.BlockSpec((1, HEAD_DIM), lambda b, i: (0, 0))],
        out_specs=pl.BlockSpec((1, tq, GW), lambda b, i: (b, i, 0)),
        out_shape=jax.ShapeDtypeStruct((bsz, seq, GW), F32),
        compiler_params=pltpu.CompilerParams(dimension_semantics=("arbitrary", "arbitrary"),
                                             vmem_limit_bytes=V7X_VMEM_LIMIT_BYTES),
    )(q, k, v, lam_rows, norm_g.reshape(1, HEAD_DIM).astype(F32))


NAT_TILE_ROWS = ROW_BLOCK // GRID_W
NAT_SLAB_ROWS = NAT_TILE_ROWS + WIN_H - 1


def _nat_slab_start(tile, n_rows):
    return np.clip(tile * NAT_TILE_ROWS - WIN_H // 2, 0, n_rows - NAT_SLAB_ROWS)


def _nat_bias_tables(rpb, n_rows):
    n_tiles = n_rows // NAT_TILE_ROWS
    nq, nk, w = NAT_TILE_ROWS, NAT_SLAB_ROWS, GRID_W
    cq, ck = np.arange(w)[:, None], np.arange(w)[None, :]
    d_col = np.clip(ck - cq, -(WIN_W - 1), WIN_W - 1) + WIN_W - 1
    col_1h = (d_col.reshape(-1)[:, None] == np.arange(2 * WIN_W - 1)[None, :]).astype(np.float32)
    c0 = np.clip(cq - WIN_W // 2, 0, w - WIN_W)
    col_ok = (ck >= c0) & (ck < c0 + WIN_W)
    tabs = []
    for tile in (0, 1, n_tiles - 1):
        r = tile * nq + np.arange(nq)[:, None]
        kr = _nat_slab_start(tile, n_rows) + np.arange(nk)[None, :]
        rs = np.clip(r - WIN_H // 2, 0, n_rows - WIN_H)
        row_ok = (kr >= rs) & (kr < rs + WIN_H)
        d_row = np.clip(kr - r + WIN_H - 1, 0, 2 * WIN_H - 2)
        row_1h = (d_row.reshape(-1)[:, None] == np.arange(2 * WIN_H - 1)[None, :]).astype(np.float32)
        t = jnp.einsum('pa,hab,cb->hpc', row_1h, rpb.astype(F32), col_1h, precision=HIGHEST)
        t = t.reshape(GROUP_HEADS, nq, nk, w, w).transpose(0, 1, 3, 2, 4)
        ok = row_ok[:, None, :, None] & col_ok[None, :, None, :]
        tabs.append(jnp.where(ok[None], t, NEG_BIG).reshape(GROUP_HEADS, nq * w, nk * w))
    return jnp.stack(tabs)


def _nat_attn_kernel(q_ref, k_ref, v_ref, bias_ref, o_ref, *, ctx_blocks, ctx_len, n_rows):
    i = pl.program_id(1)
    n_slab = NAT_SLAB_ROWS * GRID_W

    def heads(fn):
        o_ref[0] = jnp.concatenate(
            [fn(h, q_ref[0, :, h * HEAD_DIM:(h + 1) * HEAD_DIM]) for h in range(GROUP_HEADS)], axis=-1)

    @pl.when(i < ctx_blocks)
    def _():
        def ctx_only(h, qh):
            s = lax.dot_general(qh, k_ref[0, h, 0:ctx_len, :], _NT, preferred_element_type=F32)
            return _softmax_pv(s, v_ref[0, h, 0:ctx_len, :])
        heads(ctx_only)

    @pl.when(i >= ctx_blocks)
    def _():
        tile = i - ctx_blocks
        start = jnp.clip(tile * NAT_TILE_ROWS - WIN_H // 2, 0, n_rows - NAT_SLAB_ROWS)
        off = pl.multiple_of(ctx_len + start * GRID_W, GRID_W)

        def windowed(h, qh):
            s_w = lax.dot_general(qh, k_ref[0, h, pl.ds(off, n_slab), :], _NT,
                                  preferred_element_type=F32) + bias_ref[0, h]
            s_c = lax.dot_general(qh, k_ref[0, h, 0:ctx_len, :], _NT, preferred_element_type=F32)
            m = jnp.maximum(jnp.max(s_w, axis=-1, keepdims=True), jnp.max(s_c, axis=-1, keepdims=True))
            e_w = jnp.exp(s_w - m)
            e_c = jnp.exp(s_c - m)
            den = jnp.sum(e_w, axis=-1, keepdims=True) + jnp.sum(e_c, axis=-1, keepdims=True)
            num = (jnp.dot(e_w.astype(BF16), v_ref[0, h, pl.ds(off, n_slab), :], preferred_element_type=F32)
                   + jnp.dot(e_c.astype(BF16), v_ref[0, h, 0:ctx_len, :], preferred_element_type=F32))
            return num / den
        heads(windowed)


def _nat_attention_pallas(q, k, v, rpb, *, ctx_len):
    bsz, seq, _ = q.shape
    tq = ROW_BLOCK
    ctx_blocks = ctx_len // tq
    n_rows = (seq - ctx_len) // GRID_W
    n_tiles = n_rows // NAT_TILE_ROWS
    assert n_rows >= NAT_SLAB_ROWS and n_tiles >= 3
    bias = _nat_bias_tables(rpb, n_rows)

    def variant(i):
        tile = i - ctx_blocks
        return jnp.where(tile <= 0, 0, jnp.where(tile >= n_tiles - 1, 2, 1))
    kv_spec = pl.BlockSpec((1, GROUP_HEADS, seq, HEAD_DIM), lambda b, i: (b, 0, 0, 0))
    return pl.pallas_call(
        functools.partial(_nat_attn_kernel, ctx_blocks=ctx_blocks, ctx_len=ctx_len, n_rows=n_rows),
        name="nat_attn",
        grid=(bsz, seq // tq),
        in_specs=[pl.BlockSpec((1, tq, GW), lambda b, i: (b, i, 0)), kv_spec, kv_spec,
                  pl.BlockSpec((1,) + bias.shape[1:], lambda b, i: (variant(i), 0, 0, 0))],
        out_specs=pl.BlockSpec((1, tq, GW), lambda b, i: (b, i, 0)),
        out_shape=jax.ShapeDtypeStruct((bsz, seq, GW), F32),
        compiler_params=pltpu.CompilerParams(dimension_semantics=("arbitrary", "arbitrary"),
                                             vmem_limit_bytes=V7X_VMEM_LIMIT_BYTES),
    )(q, k, v, bias)


N_MOD = 6
MATMUL_ROWS = 512


def _ada_kernel(c_ref, w_ref, b_ref, o_ref):
    c = c_ref[...]
    o_ref[...] = _dot_f32(c * jax.nn.sigmoid(c), w_ref[...]) + b_ref[...]


def _ada_modulation(c, c_ctx, w_ada, b_ada):
    bsz, d = c.shape
    rows = 8 * ((bsz + 1 + 7) // 8)
    cc = jnp.zeros((rows, d), F32).at[:bsz].set(c).at[bsz].set(c_ctx)
    tn = d
    m = pl.pallas_call(
        _ada_kernel,
        name="ada_modulation",
        grid=(w_ada.shape[1] // tn,),
        in_specs=[pl.BlockSpec((rows, d), lambda j: (0, 0)),
                  pl.BlockSpec((d, tn), lambda j: (0, j)),
                  pl.BlockSpec((1, tn), lambda j: (0, j))],
        out_specs=pl.BlockSpec((rows, tn), lambda j: (0, j)),
        out_shape=jax.ShapeDtypeStruct((rows, w_ada.shape[1]), F32),
        compiler_params=pltpu.CompilerParams(dimension_semantics=("arbitrary",)),
    )(cc, w_ada, b_ada.reshape(1, -1))
    lat = m[:bsz].reshape(bsz, 1, N_MOD, d)
    ctx = jnp.broadcast_to(m[bsz].reshape(1, 1, N_MOD, d), (bsz, 1, N_MOD, d))
    return jnp.concatenate([ctx, lat], axis=1)


def _ln(x):
    mu = jnp.mean(x, axis=-1, keepdims=True)
    xc = x - mu
    return xc * lax.rsqrt(jnp.mean(xc * xc, axis=-1, keepdims=True) + LN_EPS)


def _modulate_kernel(h_ref, mod_ref, o_ref, *, shift_row):
    shift = mod_ref[0, 0, shift_row:shift_row + 1, :]
    scale = mod_ref[0, 0, shift_row + 1:shift_row + 2, :]
    o_ref[0] = (_ln(h_ref[0]) * (1.0 + scale) + shift).astype(o_ref.dtype)


def _mod_spec(d, ctx_blocks):
    return pl.BlockSpec((1, 1, N_MOD, d), lambda b, i: (b, jnp.where(i < ctx_blocks, 0, 1), 0, 0))


def _modulate_pallas(hs, mod, shift_row, *, ctx_len):
    bsz, seq, d = hs.shape
    tr = ROW_BLOCK
    blk = pl.BlockSpec((1, tr, d), lambda b, i: (b, i, 0))
    return pl.pallas_call(
        functools.partial(_modulate_kernel, shift_row=shift_row),
        name="modulate",
        grid=(bsz, seq // tr),
        in_specs=[blk, _mod_spec(d, ctx_len // tr)],
        out_specs=blk,
        out_shape=jax.ShapeDtypeStruct((bsz, seq, d), BF16),
        compiler_params=pltpu.CompilerParams(dimension_semantics=("arbitrary", "arbitrary")),
    )(hs, mod)


def _matmul_kernel(x_ref, w_ref, o_ref):
    o_ref[...] = jnp.dot(x_ref[...], w_ref[...], preferred_element_type=F32)


def _matmul_pallas(x, w):
    m, k = x.shape
    n = w.shape[1]
    tm = MATMUL_ROWS
    return pl.pallas_call(
        _matmul_kernel,
        name="in_proj",
        grid=(m // tm,),
        in_specs=[pl.BlockSpec((tm, k), lambda i: (i, 0)), pl.BlockSpec((k, n), lambda i: (0, 0))],
        out_specs=pl.BlockSpec((tm, n), lambda i: (i, 0)),
        out_shape=jax.ShapeDtypeStruct((m, n), F32),
        compiler_params=pltpu.CompilerParams(dimension_semantics=("arbitrary",),
                                             vmem_limit_bytes=V7X_VMEM_LIMIT_BYTES),
    )(x, w)


def _post_norm_rows(h, gate, y, g, b):
    return _ln(DN_ALPHA * h + gate * y) * g + b


def _out_proj_kernel(ya_ref, yb_ref, yc_ref, yd_ref, w_ref, h_ref, mod_ref, g_ref, b_ref, o_ref):
    mix = None
    for n, y_ref in enumerate((ya_ref, yb_ref, yc_ref, yd_ref)):
        part = jnp.dot(y_ref[0].astype(BF16), w_ref[n * GW:(n + 1) * GW, :], preferred_element_type=F32)
        mix = part if mix is None else mix + part
    o_ref[0] = _post_norm_rows(h_ref[0], mod_ref[0, 0, 2:3, :], mix, g_ref[...], b_ref[...])


def _out_proj_post_norm(ys, w_out, hs, mod, g, b, *, ctx_len):
    bsz, seq, d = hs.shape
    tr = ROW_BLOCK
    yblk = pl.BlockSpec((1, tr, GW), lambda bb, i: (bb, i, 0))
    blk = pl.BlockSpec((1, tr, d), lambda bb, i: (bb, i, 0))
    row = pl.BlockSpec((1, d), lambda bb, i: (0, 0))
    return pl.pallas_call(
        _out_proj_kernel,
        name="out_proj_post_norm",
        grid=(bsz, seq // tr),
        in_specs=[yblk] * 4 + [pl.BlockSpec(w_out.shape, lambda bb, i: (0, 0)), blk, _mod_spec(d, ctx_len // tr), row, row],
        out_specs=blk,
        out_shape=jax.ShapeDtypeStruct((bsz, seq, d), F32),
        compiler_params=pltpu.CompilerParams(dimension_semantics=("arbitrary", "arbitrary")),
    )(*ys, w_out, hs, mod, g.reshape(1, d), b.reshape(1, d))


def _ffn_post_norm_kernel(h_ref, y_ref, mod_ref, g_ref, b_ref, o_ref):
    o_ref[0] = _post_norm_rows(h_ref[0], mod_ref[0, 0, 5:6, :], y_ref[0], g_ref[...], b_ref[...])


def _ffn_post_norm(hs, y, mod, g, b, *, ctx_len):
    bsz, seq, d = hs.shape
    tr = ROW_BLOCK
    blk = pl.BlockSpec((1, tr, d), lambda bb, i: (bb, i, 0))
    row = pl.BlockSpec((1, d), lambda bb, i: (0, 0))
    return pl.pallas_call(
        _ffn_post_norm_kernel,
        name="ffn_post_norm",
        grid=(bsz, seq // tr),
        in_specs=[blk, blk, _mod_spec(d, ctx_len // tr), row, row],
        out_specs=blk,
        out_shape=jax.ShapeDtypeStruct((bsz, seq, d), F32),
        compiler_params=pltpu.CompilerParams(dimension_semantics=("arbitrary", "arbitrary")),
    )(hs, y, mod, g.reshape(1, d), b.reshape(1, d))


def kernel(x, c, ctx, c_ctx, w_ada, b_ada, w_in, w_out, ln_mix_g, ln_mix_b, ln_ffn_g, ln_ffn_b, diff_lam, diff_norm_g, rwkv_mu, rwkv_w0, rwkv_w2, rwkv_a0, rwkv_a2, rwkv_g2, rwkv_kk, rwkv_ka, rwkv_rk, rwkv_ln_g, rwkv_ln_b, gdn_conv, gdn_a_log, gdn_dt_bias, gdn_norm_g, nat_rpb, peer_wq, peer_keys, peer_u, peer_v):
    dtype = x.dtype
    bsz, ctx_len = ctx.shape[0], ctx.shape[1]
    hs = jnp.concatenate([ctx, x], axis=1)
    seq = hs.shape[1]
    col_sizes = [ATTN_COLS, RWKV_COLS, GDN_COLS, ATTN_COLS]
    cos_a, sin_a = _rope_tables(seq, ctx_len, DIFF_HALF ** -0.5)
    cos_d = jnp.concatenate([jnp.full((seq, GW), HEAD_DIM ** -0.5, F32), jnp.ones((seq, GW), F32)], 1)
    sin_d = jnp.zeros_like(cos_d)
    col_offs = np.cumsum([0] + col_sizes)
    d_model = hs.shape[2]
    for l in range(DEPTH):
        lam_init = 0.8 - 0.6 * math.exp(-0.3 * l)
        mod = _ada_modulation(c, c_ctx, w_ada[l], b_ada[l])
        u = _modulate_pallas(hs, mod, 0, ctx_len=ctx_len).reshape(bsz * seq, d_model)
        w_in_b = w_in[l].astype(BF16)
        w_groups = [w_in_b[:, col_offs[n]:col_offs[n + 1]] for n in range(4)]
        w_groups[2] = jnp.pad(w_groups[2], ((0, 0), (0, GDN_PADDED_COLS - GDN_COLS)))
        pa, pb, pc, pd = [_matmul_pallas(u, w).reshape(bsz, seq, w.shape[1]) for w in w_groups]
        qa, ka, va = _qkv_prep(pa, cos_a, sin_a)
        ya = _diff_attention_pallas(qa, ka, va, diff_lam[l], diff_norm_g[l], ctx_len=ctx_len, lam_init=lam_init)
        yb = _rwkv7_pallas(pb, rwkv_mu[l], rwkv_w0[l], rwkv_w2[l], rwkv_a0[l], rwkv_a2[l], rwkv_g2[l],
                           rwkv_kk[l], rwkv_ka[l], rwkv_rk[l], rwkv_ln_g[l], rwkv_ln_b[l], ctx_len=ctx_len)
        yc = _gated_deltanet_pallas(pc, gdn_conv[l], gdn_a_log[l], gdn_dt_bias[l], gdn_norm_g[l], ctx_len=ctx_len)
        qd, kd, vd = _qkv_prep(pd, cos_d, sin_d)
        yd = _nat_attention_pallas(qd, kd, vd, nat_rpb[l], ctx_len=ctx_len)
        hs = _out_proj_post_norm([ya, yb, yc, yd], w_out[l].astype(BF16), hs, mod, ln_mix_g[l], ln_mix_b[l],
                                 ctx_len=ctx_len)
        wq_b = peer_wq[l].astype(BF16)
        keys_b = peer_keys[l].reshape(2 * PEER_HEADS, PEER_KEYS, PEER_HALF).astype(BF16)
        u_b = peer_u[l].astype(BF16)
        vt_b = peer_v[l].astype(BF16).T
        ffn = _peer(_modulate_pallas(hs, mod, 3, ctx_len=ctx_len), wq_b, keys_b, u_b, vt_b)
        hs = _ffn_post_norm(hs, ffn, mod, ln_ffn_g[l], ln_ffn_b[l], ctx_len=ctx_len)
    return hs[:, ctx_len:].astype(dtype)
```

```python
import functools
import math

import jax
import jax.numpy as jnp
import numpy as np
from jax import lax
from jax.experimental import pallas as pl
from jax.experimental.pallas import tpu as pltpu

D_MODEL = 1024
DEPTH = 2
GRID_W = 64
HEAD_DIM = 64
N_GROUPS = 4
GROUP_HEADS = D_MODEL // (N_GROUPS * HEAD_DIM)
GW = GROUP_HEADS * HEAD_DIM
D_MIX = N_GROUPS * GW
DIFF_HALF = HEAD_DIM // 2
Q_BLOCK = 128
ROPE_BASE = 10000.0
DECAY_LORA = 64
ICLR_LORA = 64
GATE_LORA = 128
RWKV_GN_EPS = 64e-5
RWKV_COLS = 3 * GW + 2 * DECAY_LORA + 2 * ICLR_LORA + GATE_LORA
GDN_CONV = 3
GDN_CHUNK = 64
GDN_COLS = 4 * GW + 4 * GROUP_HEADS
WIN_H = 8
WIN_W = 16
ATTN_COLS = 3 * GW
IN_COLS = ATTN_COLS + RWKV_COLS + GDN_COLS + ATTN_COLS
PEER_HEADS = 8
PEER_KEYS = 128
PEER_EXPERTS = PEER_KEYS * PEER_KEYS
PEER_QDIM = 256
PEER_HALF = PEER_QDIM // 2
PEER_TOPK = 16
DN_ALPHA = (2 * DEPTH) ** 0.25
LN_EPS = 1e-5

F32 = jnp.float32
BF16 = jnp.bfloat16

V7X_VMEM_LIMIT_BYTES = 56 * 1024 * 1024
NEG_BIG = -3.0e38


def _split(x, sizes):
    offs = np.cumsum(sizes)[:-1].tolist()
    return jnp.split(x, offs, axis=-1)


def _heads(x):
    return x.reshape(x.shape[0], x.shape[1], -1, HEAD_DIM)


def _layer_norm(x):
    xf = x.astype(F32)
    mu = xf.mean(-1, keepdims=True)
    var = jnp.square(xf - mu).mean(-1, keepdims=True)
    return (xf - mu) * lax.rsqrt(var + LN_EPS)


def _rms_norm(x, g):
    xf = x.astype(F32)
    return xf * lax.rsqrt(jnp.mean(xf * xf, -1, keepdims=True) + 1e-6) * g.astype(F32)


def _l2norm(x):
    xf = x.astype(F32)
    return xf * lax.rsqrt(jnp.sum(xf * xf, -1, keepdims=True) + 1e-6)


def _modulate(h, shift, scale, dtype):
    return (_layer_norm(h) * (1.0 + scale) + shift).astype(dtype)


def _post_norm(h, gate, y, g, b, dtype):
    z = DN_ALPHA * h.astype(F32) + gate.astype(F32) * y.astype(F32)
    return (_layer_norm(z) * g + b).astype(dtype)


def _dwconv_centred(x, w):
    k = w.shape[0]
    pad = k // 2
    t = x.shape[1]
    xp = jnp.pad(x, ((0, 0), (pad, pad), (0, 0)))
    out = xp[:, 0:t] * w[0]
    for i in range(1, k):
        out = out + xp[:, i:i + t] * w[i]
    return out


def _grid_pos(n_tok):
    t = jnp.arange(n_tok, dtype=jnp.int32)
    return t // GRID_W, t % GRID_W


def _rope_1d(x, pos):
    half = x.shape[-1] // 2
    inv = ROPE_BASE ** (-jnp.arange(half, dtype=F32) / half)
    ang = pos.astype(F32)[:, None] * inv
    cos = jnp.cos(ang)[:, None, :]
    sin = jnp.sin(ang)[:, None, :]
    x1 = x[..., :half].astype(F32)
    x2 = x[..., half:].astype(F32)
    return jnp.concatenate([x1 * cos - x2 * sin, x1 * sin + x2 * cos], -1)


def _rope_2d(x, row, col):
    n = x.shape[-1] // 2
    return jnp.concatenate([_rope_1d(x[..., :n], row), _rope_1d(x[..., n:], col)], -1)


def _diff_core(q, k, v, lam):
    scale = DIFF_HALF ** -0.5
    q = q.astype(F32)
    k = k.astype(F32)
    s1 = jnp.einsum('bqhd,bkhd->bhqk', q[..., :DIFF_HALF], k[..., :DIFF_HALF]) * scale
    s2 = jnp.einsum('bqhd,bkhd->bhqk', q[..., DIFF_HALF:], k[..., DIFF_HALF:]) * scale
    p = jax.nn.softmax(s1, -1) - lam * jax.nn.softmax(s2, -1)
    return jnp.einsum('bhqk,bkhd->bqhd', p, v.astype(F32))


def _diff_attention(q, k, v, qc, kc, vc, lam_vecs, norm_g, lam_init, ctx_out):
    bsz, lat_len = q.shape[:2]
    row, col = _grid_pos(lat_len)
    rot = lambda z: jnp.concatenate([_rope_2d(z[..., :DIFF_HALF], row, col),
                                     _rope_2d(z[..., DIFF_HALF:], row, col)], -1)
    q = rot(q)
    k = rot(k)
    lv = lam_vecs.astype(F32)
    lam = jnp.exp(jnp.sum(lv[0] * lv[1])) - jnp.exp(jnp.sum(lv[2] * lv[3])) + lam_init
    k_all = jnp.concatenate([k, kc.astype(F32)], 1)
    v_all = jnp.concatenate([v.astype(F32), vc.astype(F32)], 1)
    nb = lat_len // Q_BLOCK
    qb = jnp.moveaxis(q.reshape(bsz, nb, Q_BLOCK, GROUP_HEADS, HEAD_DIM), 1, 0)
    ob = lax.map(lambda qq: _diff_core(qq, k_all, v_all, lam), qb)
    o = jnp.moveaxis(ob, 0, 1).reshape(bsz, lat_len, GROUP_HEADS, HEAD_DIM)
    post = lambda z: (_rms_norm(z, norm_g) * (1.0 - lam_init)).reshape(z.shape[0], z.shape[1], GW)
    out_ctx = post(_diff_core(qc, kc, vc, lam)) if ctx_out else None
    return post(o), out_ctx


def _wkv7_scan(r, w, k, v, kk, a, s0, reverse):
    def step(s, inp):
        r_t, w_t, k_t, v_t, kk_t, a_t = inp
        sa = jnp.einsum('bhvk,bhk->bhv', s, kk_t)
        s = (s * w_t[:, :, None, :] - sa[..., None] * (kk_t * a_t)[:, :, None, :]
             + v_t[..., None] * k_t[:, :, None, :])
        return s, jnp.einsum('bhvk,bhk->bhv', s, r_t)
    xs = tuple(jnp.moveaxis(z, 1, 0) for z in (r, w, k, v, kk, a))
    s, y = lax.scan(step, s0, xs, reverse=reverse)
    return jnp.moveaxis(y, 0, 1), s


def _rwkv7(f_lat, f_ctx, mu, w0, w2, a0, a2, g2, k_k, k_a, r_k, ln_g, ln_b, ctx_out):
    shift_w = jnp.stack([mu[0], 1.0 - mu[0] - mu[1], mu[1]])

    def prep(f):
        f = _dwconv_centred(f, shift_w).astype(F32)
        r, k, v, wd, ad, gd = _split(f, [GW, GW, GW, 2 * DECAY_LORA, 2 * ICLR_LORA, GATE_LORA])
        bsz, t = f.shape[:2]
        wd = jnp.tanh(wd.reshape(bsz, t, 2, DECAY_LORA))
        ad = ad.reshape(bsz, t, 2, ICLR_LORA)
        w_raw = w0 + jnp.einsum('btdr,drc->btdc', wd, w2)
        decay = jnp.exp(-jnp.exp(-jax.nn.softplus(-w_raw) - 0.5))
        a = jax.nn.sigmoid(a0 + jnp.einsum('btdr,drc->btdc', ad, a2))
        g = jax.nn.sigmoid(gd) @ g2
        kk = _l2norm(_heads(k * k_k))
        kd = k[:, :, None] * (1.0 + (a - 1.0) * k_a)
        return r, v, g, kk, decay, a, kd

    def run(p, d, s0, rev):
        r, v, g, kk, decay, a, kd = p
        return _wkv7_scan(_heads(r), _heads(decay[:, :, d]), _heads(kd[:, :, d]), _heads(v),
                          kk, _heads(a[:, :, d]), s0, rev)

    def post(p, ys):
        r, v, g, kk, decay, a, kd = p
        bsz, t = r.shape[:2]
        y = ys[0] + ys[1]
        m = y.mean(-1, keepdims=True)
        var = jnp.square(y - m).mean(-1, keepdims=True)
        yn = ((y - m) * lax.rsqrt(var + RWKV_GN_EPS)).reshape(bsz, t, GW) * ln_g + ln_b
        rh, vh = _heads(r), _heads(v)
        bonus = ((rh * _heads(kd[:, :, 0]) * r_k).sum(-1, keepdims=True) * vh
                 + (rh * _heads(kd[:, :, 1]) * r_k).sum(-1, keepdims=True) * vh)
        return (yn + bonus.reshape(bsz, t, GW)) * g

    pl_, pc = prep(f_lat), prep(f_ctx)
    s0 = jnp.zeros((f_lat.shape[0], GROUP_HEADS, HEAD_DIM, HEAD_DIM), F32)
    y_lat, y_ctx = [], []
    for d, rev in ((0, False), (1, True)):
        yc, sc = run(pc, d, s0, rev)
        yl, _ = run(pl_, d, sc, rev)
        y_lat.append(yl)
        y_ctx.append(yc)
    out_ctx = post(pc, y_ctx) if ctx_out else None
    return post(pl_, y_lat), out_ctx


def _gdn_chunked(q, k, v, beta, g, s0):
    bsz, t, h, dk = q.shape
    dv = v.shape[-1]
    n = t // GDN_CHUNK
    ch = lambda z: jnp.moveaxis(z.reshape(bsz, n, GDN_CHUNK, h, *z.shape[3:]), 3, 2)
    q, k, v, beta, g = ch(q), ch(k), ch(v), ch(beta), ch(g)
    gc = jnp.cumsum(g, axis=-1)
    i = jnp.arange(GDN_CHUNK)
    incl = i[:, None] >= i[None, :]
    strict = i[:, None] > i[None, :]
    decay = jnp.exp(jnp.where(incl, gc[..., :, None] - gc[..., None, :], -jnp.inf))
    kb = k * beta[..., None]
    a_low = jnp.where(strict, jnp.einsum('bnhid,bnhjd->bnhij', kb, k) * decay, 0.0)
    tmat = a_low + jnp.eye(GDN_CHUNK, dtype=F32)
    rhs = jnp.concatenate([v * beta[..., None], kb * jnp.exp(gc)[..., None]], -1)
    sol = lax.linalg.triangular_solve(tmat, rhs, left_side=True, lower=True, unit_diagonal=True)
    u, w = sol[..., :dv], sol[..., dv:]
    qk = jnp.where(incl, jnp.einsum('bnhid,bnhjd->bnhij', q, k) * decay, 0.0)
    qg = q * jnp.exp(gc)[..., None]
    kg = k * jnp.exp(gc[..., -1:] - gc)[..., None]
    glast = jnp.exp(gc[..., -1])

    def step(s, xs):
        qg_i, kg_i, u_i, w_i, qk_i, gl_i = xs
        v_new = u_i - jnp.einsum('bhcd,bhdv->bhcv', w_i, s)
        o = jnp.einsum('bhcd,bhdv->bhcv', qg_i, s) + jnp.einsum('bhij,bhjv->bhiv', qk_i, v_new)
        s = s * gl_i[..., None, None] + jnp.einsum('bhcd,bhcv->bhdv', kg_i, v_new)
        return s, o

    xs = tuple(jnp.moveaxis(z, 1, 0) for z in (qg, kg, u, w, qk, glast))
    s, o = lax.scan(step, s0, xs)
    o = jnp.moveaxis(jnp.moveaxis(o, 0, 1), 2, 3).reshape(bsz, t, h, dv)
    return o, s


def _gated_deltanet(f_lat, f_ctx, conv_w, a_log, dt_bias, norm_g, ctx_out):
    def prep(f):
        qkv, gate, ab = _split(f, [3 * GW, GW, 4 * GROUP_HEADS])
        qkv = jax.nn.silu(_dwconv_centred(qkv, conv_w).astype(F32))
        q, k, v = [_heads(z) for z in _split(qkv, [GW, GW, GW])]
        q = _l2norm(q) * HEAD_DIM ** -0.5
        k = _l2norm(k)
        ab = ab.astype(F32).reshape(f.shape[0], f.shape[1], 2, 2, GROUP_HEADS)
        log_alpha = -jnp.exp(a_log) * jax.nn.softplus(ab[:, :, 0] + dt_bias)
        beta = jax.nn.sigmoid(ab[:, :, 1])
        return q, k, v, gate, log_alpha, beta

    def run(p, d, s0):
        tr = (lambda z: jnp.flip(z, 1)) if d == 1 else (lambda z: z)
        q, k, v, gate, log_alpha, beta = p
        o, s = _gdn_chunked(tr(q), tr(k), tr(v), tr(beta[:, :, d]), tr(log_alpha[:, :, d]), s0)
        return tr(o), s

    def post(p, os_):
        gate = p[3]
        o = os_[0] + os_[1]
        y = _rms_norm(o, norm_g) * jax.nn.silu(_heads(gate.astype(F32)))
        return y.reshape(o.shape[0], o.shape[1], GW)

    pl_, pc = prep(f_lat), prep(f_ctx)
    s0 = jnp.zeros((f_lat.shape[0], GROUP_HEADS, HEAD_DIM, HEAD_DIM), F32)
    o_lat, o_ctx = [], []
    for d in range(2):
        oc, sc = run(pc, d, s0)
        ol, _ = run(pl_, d, sc)
        o_lat.append(ol)
        o_ctx.append(oc)
    out_ctx = post(pc, o_ctx) if ctx_out else None
    return post(pl_, o_lat), out_ctx


def _softmax_attn(q, k, v):
    s = jnp.einsum('bqhd,bkhd->bhqk', q.astype(F32), k.astype(F32)) * HEAD_DIM ** -0.5
    return jnp.einsum('bhqk,bkhd->bqhd', jax.nn.softmax(s, -1), v.astype(F32))


def _neighbourhood_attention(q, k, v, qc, kc, vc, rpb, ctx_out):
    bsz, lat_len = q.shape[:2]
    rows = lat_len // GRID_W
    kh = min(WIN_H, rows)
    scale = HEAD_DIM ** -0.5
    grid = lambda z: z.astype(F32).reshape(bsz, rows, GRID_W, GROUP_HEADS, HEAD_DIM)
    qg, kg, vg = grid(q), grid(k), grid(v)
    kc32, vc32 = kc.astype(F32), vc.astype(F32)
    rpb = rpb.astype(F32)
    cq = jnp.arange(GRID_W)
    c_start = jnp.clip(cq - WIN_W // 2, 0, GRID_W - WIN_W)
    col_ok = (cq[None, :] >= c_start[:, None]) & (cq[None, :] < c_start[:, None] + WIN_W)
    d_col = jnp.clip(cq[None, :] - cq[:, None], -(WIN_W - 1), WIN_W - 1) + (WIN_W - 1)

    def row_block(r):
        rs = jnp.clip(r - kh // 2, 0, rows - kh)
        q_r = lax.dynamic_index_in_dim(qg, r, axis=1, keepdims=False)
        k_r = lax.dynamic_slice_in_dim(kg, rs, kh, axis=1)
        v_r = lax.dynamic_slice_in_dim(vg, rs, kh, axis=1)
        s = jnp.einsum('bqhd,bkwhd->bqhkw', q_r, k_r) * scale
        d_row = rs + jnp.arange(kh) - r + (WIN_H - 1)
        bias = rpb[:, d_row[None, :, None], d_col[:, None, :]]
        s = jnp.where(col_ok[:, None, None, :], s + jnp.transpose(bias, (1, 0, 2, 3)), -jnp.inf)
        s_c = jnp.einsum('bqhd,bchd->bqhc', q_r, kc32) * scale
        p = jax.nn.softmax(jnp.concatenate(
            [s.reshape(bsz, GRID_W, GROUP_HEADS, kh * GRID_W), s_c], -1), -1)
        p_win = p[..., :kh * GRID_W].reshape(bsz, GRID_W, GROUP_HEADS, kh, GRID_W)
        return (jnp.einsum('bqhkw,bkwhd->bqhd', p_win, v_r)
                + jnp.einsum('bqhc,bchd->bqhd', p[..., kh * GRID_W:], vc32))

    o = lax.map(row_block, jnp.arange(rows))
    out_lat = jnp.moveaxis(o, 0, 1).reshape(bsz, lat_len, GW)
    out_ctx = _softmax_attn(qc, kc, vc).reshape(bsz, qc.shape[1], GW) if ctx_out else None
    return out_lat, out_ctx


def _qkv_heads(p):
    return [_heads(z) for z in _split(p, [GW, GW, GW])]


PEER_STAT_ROWS = 4 * PEER_HEADS
LOG2E = 1.4426950408889634
assert PEER_TOPK == 16


def _topk_rows(x, k):
    rows = []
    cur = x
    for i in range(k):
        m = jnp.max(cur, axis=0, keepdims=True)
        rows.append(m)
        if i + 1 < k:
            cur = jnp.where(cur == m, NEG_BIG, cur)
    return rows


def _peer_score_kernel(x_ref, wq_ref, keys_ref, s_ref, st_ref):
    q = jnp.dot(x_ref[...], wq_ref[...], preferred_element_type=F32).astype(BF16)
    stats = []
    for h in range(PEER_HEADS):
        tops = []
        for p in range(2):
            hp = 2 * h + p
            s_t = lax.dot_general(keys_ref[hp], q[:, hp * PEER_HALF:(hp + 1) * PEER_HALF],
                                  (((1,), (1,)), ((), ())), preferred_element_type=F32)
            s_ref[hp] = s_t
            tops.append(_topk_rows(s_t, PEER_TOPK + 1))
        a, b = tops
        pad = [jnp.full_like(a[0], NEG_BIG)] * 7
        b_head = jnp.concatenate(b[:8], axis=0)
        cand = jnp.concatenate([a[0] + jnp.concatenate(b + pad, axis=0)]
                               + [a[i] + b_head for i in range(1, 8)]
                               + [jnp.concatenate(a[8:] + pad, axis=0) + b[0]], axis=0)
        best_cand = _topk_rows(cand, PEER_TOPK + 1)
        kth, runner_up = best_cand[PEER_TOPK - 1], best_cand[PEER_TOPK]
        best = a[0] + b[0]
        z = jnp.sum(jnp.where(cand >= kth, jnp.exp(cand - best), 0.0), axis=0, keepdims=True)
        stats += [0.5 * (kth + runner_up), tops[0][0], tops[1][0], 1.0 / z]
    st_ref[...] = jnp.concatenate(stats, axis=0)


def _peer_expert_kernel(x_ref, s_ref, st_ref, u_ref, vt_ref, o_ref, e_ref, thr_ref, acc_ref, w_ref,
                        *, rows_per_step, n_chunks):
    j = pl.program_id(1)
    cur = lax.rem(j, 2)

    @pl.when(j == 0)
    def _():
        acc_ref[...] = jnp.zeros_like(acc_ref)
        w_ref[1] = jnp.zeros(w_ref.shape[1:], w_ref.dtype)
        for h in range(PEER_HEADS):
            a0 = st_ref[4 * h + 1:4 * h + 2, :]
            b0 = st_ref[4 * h + 2:4 * h + 3, :]
            rz = st_ref[4 * h + 3:4 * h + 4, :]
            e_ref[2 * h] = (s_ref[2 * h] - a0 + jnp.log(rz)) * LOG2E
            e_ref[2 * h + 1] = (s_ref[2 * h + 1] - b0) * LOG2E
            thr_ref[h] = (st_ref[4 * h:4 * h + 1, :] - s_ref[2 * h] - b0) * LOG2E

    @pl.when(j < n_chunks)
    def _():
        slab = 2 * PEER_KEYS
        n_slabs = rows_per_step * PEER_KEYS // slab
        act_slab = lambda n: lax.dot_general(u_ref[n * slab:(n + 1) * slab, :], x_ref[...], (((1,), (1,)), ((), ())),
                                             preferred_element_type=F32)
        acts = [act_slab(n) for n in range(n_slabs)]
        acc_ref[...] += jnp.dot(vt_ref[...], w_ref[1 - cur], preferred_element_type=F32)
        for il in range(rows_per_step):
            i = j * rows_per_step + il
            o = il * PEER_KEYS
            act = acts[o // slab][o % slab:o % slab + PEER_KEYS]
            act = 0.5 * act * (1.0 + lax.erf(act * (2.0 ** -0.5)))
            gate = None
            for h in range(PEER_HEADS):
                l2 = e_ref[2 * h + 1]
                term = jnp.where(l2 >= thr_ref[h, pl.ds(i, 1), :], jnp.exp2(l2 + e_ref[2 * h, pl.ds(i, 1), :]), 0.0)
                gate = term if gate is None else gate + term
            w_ref[cur, o:o + PEER_KEYS, :] = (act * gate).astype(BF16)

    @pl.when(j == n_chunks)
    def _():
        o_ref[...] = (acc_ref[...] + jnp.dot(vt_ref[...], w_ref[1 - cur], preferred_element_type=F32)).T


def _peer_pallas(h, w_q, keys, u_tab, vt_tab, *, tb, rows_per_step):
    n_tok, d = h.shape
    nhp = 2 * PEER_HEADS
    s, st = pl.pallas_call(
        _peer_score_kernel,
        name="peer_scores",
        grid=(n_tok // tb,),
        in_specs=[pl.BlockSpec((tb, d), lambda i: (i, 0)),
                  pl.BlockSpec(w_q.shape, lambda i: (0, 0)),
                  pl.BlockSpec(keys.shape, lambda i: (0, 0, 0))],
        out_specs=[pl.BlockSpec((nhp, PEER_KEYS, tb), lambda i: (0, 0, i)),
                   pl.BlockSpec((PEER_STAT_ROWS, tb), lambda i: (0, i))],
        out_shape=[jax.ShapeDtypeStruct((nhp, PEER_KEYS, n_tok), F32),
                   jax.ShapeDtypeStruct((PEER_STAT_ROWS, n_tok), F32)],
        compiler_params=pltpu.CompilerParams(dimension_semantics=("arbitrary",),
                                             vmem_limit_bytes=V7X_VMEM_LIMIT_BYTES),
    )(h, w_q, keys)
    ec = rows_per_step * PEER_KEYS
    n_chunks = PEER_EXPERTS // ec
    return pl.pallas_call(
        functools.partial(_peer_expert_kernel, rows_per_step=rows_per_step, n_chunks=n_chunks),
        name="peer_experts",
        grid=(n_tok // tb, n_chunks + 1),
        in_specs=[pl.BlockSpec((tb, d), lambda i, j: (i, 0)),
                  pl.BlockSpec((nhp, PEER_KEYS, tb), lambda i, j: (0, 0, i)),
                  pl.BlockSpec((PEER_STAT_ROWS, tb), lambda i, j: (0, i)),
                  pl.BlockSpec((ec, d), lambda i, j: (jnp.minimum(j, n_chunks - 1), 0)),
                  pl.BlockSpec((d, ec), lambda i, j: (0, jnp.maximum(j - 1, 0)))],
        out_specs=pl.BlockSpec((tb, d), lambda i, j: (i, 0)),
        out_shape=jax.ShapeDtypeStruct((n_tok, d), F32),
        scratch_shapes=[pltpu.VMEM((nhp, PEER_KEYS, tb), F32),
                        pltpu.VMEM((PEER_HEADS, PEER_KEYS, tb), F32),
                        pltpu.VMEM((d, tb), F32),
                        pltpu.VMEM((2, ec, tb), BF16)],
        compiler_params=pltpu.CompilerParams(dimension_semantics=("arbitrary", "arbitrary"),
                                             vmem_limit_bytes=V7X_VMEM_LIMIT_BYTES),
    )(h, s, st, u_tab, vt_tab)


def _peer(h, w_q, keys, u_tab, vt_tab):
    bsz, t, d = h.shape
    n_tok = bsz * t
    tb = 512 if n_tok % 512 == 0 else 256
    out = _peer_pallas(h.reshape(n_tok, d).astype(BF16), w_q, keys, u_tab, vt_tab, tb=tb, rows_per_step=16)
    return out.reshape(bsz, t, d)


ROW_BLOCK = 256
HIGHEST = lax.Precision.HIGHEST


def _dot_f32(a, b):
    return jnp.dot(a, b, precision=HIGHEST, preferred_element_type=F32)


def _segment_ones(n, seg, dtype):
    r = lax.broadcasted_iota(jnp.int32, (n, n), 0) // seg
    c = lax.broadcasted_iota(jnp.int32, (n, n), 1) // seg
    return jnp.where(r == c, 1.0, 0.0).astype(dtype)


def _shifted_rows(x, prev_row, next_row):
    t = x.shape[0]
    rows = lax.broadcasted_iota(jnp.int32, x.shape, 0)
    xm = jnp.where(rows == 0, prev_row, pltpu.roll(x, 1, axis=0))
    xp = jnp.where(rows == t - 1, next_row, pltpu.roll(x, t - 1, axis=0))
    return xm, xp


def _segment_edge_flags(i, n_blocks, ctx_blocks):
    is_start = jnp.logical_or(i == 0, i == ctx_blocks)
    is_end = jnp.logical_or(i == ctx_blocks - 1, i == n_blocks - 1)
    return jnp.where(is_start, 0.0, 1.0), jnp.where(is_end, 0.0, 1.0)


def _halo_specs(width, tr):
    g = tr // 8
    prev = pl.BlockSpec((1, 8, width), lambda b, i: (b, jnp.maximum(i * g - 1, 0), 0))
    nxt = lambda n_groups: pl.BlockSpec((1, 8, width), lambda b, i: (b, jnp.minimum((i + 1) * g, n_groups - 1), 0))
    return prev, nxt


def _softplus(z):
    return jnp.maximum(z, 0.0) + jnp.log1p(jnp.exp(-jnp.abs(z)))


def _rwkv_prep_kernel(x_ref, xprev_ref, xnext_ref, mu_ref, w0_ref, w2_ref, a0_ref, a2_ref, g2_ref,
                      kk_ref, ka_ref, rk_ref,
                      r_out, v_out, kkn_out, g_out, bonus_out, w_out, b_out, kt_out, *, ctx_blocks):
    i = pl.program_id(1)
    keep_prev, keep_next = _segment_edge_flags(i, pl.num_programs(1), ctx_blocks)
    x = x_ref[0]
    xm, xp = _shifted_rows(x, xprev_ref[0, 7:8, :] * keep_prev, xnext_ref[0, 0:1, :] * keep_next)
    mu0 = mu_ref[0:1, :]
    mu1 = mu_ref[1:2, :]
    f = xm * mu0 + x * (1.0 - mu0 - mu1) + xp * mu1
    r = f[:, 0:GW]
    k = f[:, GW:2 * GW]
    v = f[:, 2 * GW:3 * GW]
    o = 3 * GW
    wd = jnp.tanh(f[:, o:o + 2 * DECAY_LORA])
    ad = f[:, o + 2 * DECAY_LORA:o + 2 * DECAY_LORA + 2 * ICLR_LORA]
    gd = f[:, o + 2 * DECAY_LORA + 2 * ICLR_LORA:]
    w_raw = w0_ref[...] + _dot_f32(wd, w2_ref[...])
    log_decay = -jnp.exp(-_softplus(-w_raw) - 0.5)
    a = jax.nn.sigmoid(a0_ref[...] + _dot_f32(ad, a2_ref[...]))
    g = _dot_f32(jax.nn.sigmoid(gd), g2_ref[...])
    head_sum = _segment_ones(GW, HEAD_DIM, F32)
    kx = k * kk_ref[...]
    kkn = kx * lax.rsqrt(_dot_f32(kx * kx, head_sum) + 1e-6)
    kd_sum = jnp.zeros_like(k)
    for d in range(2):
        a_d = a[:, d * GW:(d + 1) * GW]
        kd = k * (1.0 + (a_d - 1.0) * ka_ref[...])
        kd_sum = kd_sum + kd
        w_out[d, 0] = log_decay[:, d * GW:(d + 1) * GW]
        b_out[d, 0] = kkn * a_d
        kt_out[d, 0] = kd
    r_out[0] = r
    v_out[0] = v
    kkn_out[0] = kkn
    g_out[0] = g
    bonus_out[0] = _dot_f32(r * kd_sum * rk_ref[...], head_sum) * v


def _rwkv_scan_kernel(rf_ref, vf_ref, kkf_ref, wf_ref, bf_ref, ktf_ref,
                      rb_ref, vb_ref, kkb_ref, wb_ref, bb_ref, ktb_ref,
                      yf_ref, yb_ref, st_ref):
    c = pl.program_id(0)
    bsz, tt, _ = rf_ref.shape
    n_pairs = GW // 128

    @pl.when(c == 0)
    def _():
        st_ref[...] = jnp.zeros_like(st_ref)

    ones_bd = _segment_ones(128, HEAD_DIM, BF16)
    rows = lax.broadcasted_iota(jnp.int32, (HEAD_DIM, 128), 0)
    lanes = lax.broadcasted_iota(jnp.int32, (HEAD_DIM, 128), 1)
    diag = jnp.where(lanes % HEAD_DIM == rows, 1.0, 0.0)
    dirs = ((rf_ref, vf_ref, kkf_ref, wf_ref, bf_ref, ktf_ref, yf_ref),
            (rb_ref, vb_ref, kkb_ref, wb_ref, bb_ref, ktb_ref, yb_ref))

    n_groups = tt // 8

    def group(tg, carry):
        chains = [(b, p) for b in range(bsz) for p in range(n_pairs)]
        nc = len(chains)
        base = [pl.multiple_of((tg if d == 0 else n_groups - 1 - tg) * 8, 8) for d in range(2)]
        tiles = [[], []]
        states = [[], []]
        for d in range(2):
            r_ref, v_ref, kk_ref, w_ref, b_ref, kt_ref, _ = dirs[d]
            for b, p in chains:
                cols = slice(p * 128, (p + 1) * 128)
                tiles[d].append(tuple(ref[b, pl.ds(base[d], 8), cols] for ref in (r_ref, v_ref, kk_ref))
                                + tuple(ref[0, b, pl.ds(base[d], 8), cols] for ref in (w_ref, b_ref, kt_ref)))
                states[d].append(st_ref[d, b * n_pairs + p])
        ys = [[[None] * 8 for _ in chains] for _ in range(2)]
        pending = [None, None]

        def emit_outputs(d, yb, jj):
            for ci in range(nc):
                ys[d][ci][jj] = jnp.sum(yb[ci * HEAD_DIM:(ci + 1) * HEAD_DIM] * diag, axis=0, keepdims=True)

        for j in range(8):
            for d in range(2):
                jj = j if d == 0 else 7 - j
                row = lambda a: a[jj:jj + 1, :]
                parts = ([s * row(t[2]) for s, t in zip(states[d], tiles[d])]
                         + [diag * row(t[1]) for t in tiles[d]])
                if pending[d] is not None:
                    parts += pending[d][0]
                res = jnp.dot(jnp.concatenate(parts, axis=0).astype(BF16), ones_bd, preferred_element_type=F32)
                for ci, t in enumerate(tiles[d]):
                    sa = res[ci * HEAD_DIM:(ci + 1) * HEAD_DIM]
                    vcol = res[(nc + ci) * HEAD_DIM:(nc + ci + 1) * HEAD_DIM]
                    states[d][ci] = states[d][ci] * row(t[3]) - sa * row(t[4]) + vcol * row(t[5])
                if pending[d] is not None:
                    emit_outputs(d, res[2 * nc * HEAD_DIM:], pending[d][1])
                pending[d] = ([s * row(t[0]) for s, t in zip(states[d], tiles[d])], jj)
        for d in range(2):
            q = jnp.concatenate(pending[d][0], axis=0).astype(BF16)
            emit_outputs(d, jnp.dot(q, ones_bd, preferred_element_type=F32), pending[d][1])
            y_ref = dirs[d][6]
            for ci, (b, p) in enumerate(chains):
                st_ref[d, b * n_pairs + p] = states[d][ci]
                y_ref[b, pl.ds(base[d], 8), p * 128:(p + 1) * 128] = jnp.concatenate(ys[d][ci], axis=0)
        return carry

    lax.fori_loop(0, n_groups, group, 0)


RWKV_CHUNK = 64


def _rwkv_chunk_kernel(rf_ref, vf_ref, kkf_ref, wf_ref, bf_ref, ktf_ref,
                       rb_ref, vb_ref, kkb_ref, wb_ref, bb_ref, ktb_ref, yf_ref, yb_ref, st_ref):
    i = pl.program_id(1)
    c = RWKV_CHUNK
    n_chunks = rf_ref.shape[1] // c

    @pl.when(i == 0)
    def _():
        st_ref[...] = jnp.zeros_like(st_ref)

    rr = lax.broadcasted_iota(jnp.int32, (c, c), 0)
    ss = lax.broadcasted_iota(jnp.int32, (c, c), 1)
    eye = rr == ss
    ones_cc = jnp.ones((c, c), F32)
    incl = (rr >= ss, rr <= ss)
    strict = (rr > ss, rr < ss)
    levels = []
    blk = 1
    while blk < c:
        levels.append(jnp.logical_and(rr // (2 * blk) == ss // (2 * blk), rr // blk != ss // blk))
        blk *= 2
    bdot = lambda x, y: jnp.dot(x.astype(BF16), y.astype(BF16), preferred_element_type=F32)
    bdot_nt = lambda x, y: lax.dot_general(x.astype(BF16), y.astype(BF16), _NT, preferred_element_type=F32)
    bdot_tn = lambda x, y: lax.dot_general(x.astype(BF16), y.astype(BF16), (((0,), (0,)), ((), ())),
                                           preferred_element_type=F32)
    dirs = ((rf_ref, vf_ref, kkf_ref, wf_ref, bf_ref, ktf_ref, yf_ref),
            (rb_ref, vb_ref, kkb_ref, wb_ref, bb_ref, ktb_ref, yb_ref))
    heads = [(d, h) for d in range(2) for h in range(GROUP_HEADS)]
    row0 = lambda cc, d: (cc if d == 0 else n_chunks - 1 - cc) * c
    rows = lambda cc, d: slice(row0(cc, d), row0(cc, d) + c)

    logw = {(cc, d): dirs[d][3][0, 0, rows(cc, d), :] for cc in range(n_chunks) for d in range(2)}
    cum = {key: _dot_f32(jnp.where(incl[key[1]], 1.0, 0.0), lw) for key, lw in logw.items()}
    tot = {key: _dot_f32(ones_cc, lw) for key, lw in logw.items()}
    st = {}
    for cc in range(n_chunks):
        for d, h in heads:
            cols = slice(h * HEAD_DIM, (h + 1) * HEAD_DIM)
            r_ref, v_ref, kk_ref, _, b_ref, kt_ref, _ = dirs[d]
            lc = cum[cc, d][:, cols]
            grow = jnp.exp(-lc)
            st[cc, d, h] = dict(
                v=v_ref[0, rows(cc, d), cols],
                kap=kk_ref[0, rows(cc, d), cols] * jnp.exp(lc - logw[cc, d][:, cols]),
                bh=b_ref[0, 0, rows(cc, d), cols] * grow,
                kh=kt_ref[0, 0, rows(cc, d), cols] * grow,
                rh=r_ref[0, rows(cc, d), cols] * jnp.exp(lc),
                scale=jnp.exp(_dot_f32(jnp.where(eye, tot[cc, d][:, cols], 0.0), ones_cc)))
    for (cc, d, h), x in st.items():
        x["a_b"] = jnp.where(strict[d], bdot_nt(x["kap"], x["bh"]), 0.0)
        x["a_k"] = jnp.where(strict[d], bdot_nt(x["kap"], x["kh"]), 0.0)
        x["b_k"] = jnp.where(incl[d], bdot_nt(x["rh"], x["kh"]), 0.0)
        x["b_b"] = jnp.where(incl[d], bdot_nt(x["rh"], x["bh"]), 0.0)
        x["inv"] = jnp.where(eye, 1.0, 0.0)
    for lvl in levels:
        for x in st.values():
            x["t"] = bdot(jnp.where(lvl, x["a_b"], 0.0), x["inv"])
        for x in st.values():
            x["inv"] = x["inv"] - bdot(x["inv"], x["t"])
    for x in st.values():
        x["akv"] = bdot(x["a_k"], x["v"])
        x["y_const"] = bdot(x["b_k"], x["v"])
        x["kv"] = bdot_tn(x["kh"], x["v"])
    for x in st.values():
        x["p_state"] = bdot(x["inv"], x["kap"])
        x["p_const"] = bdot(x["inv"], x["akv"])
    state = {key: st_ref[key[0], key[1]] for key in heads}
    for cc in range(n_chunks):
        cur = [(key, st[(cc,) + key]) for key in heads]
        for key, x in cur:
            x["p"] = bdot(x["p_state"], state[key]) + x["p_const"]
            x["y0"] = bdot(x["rh"], state[key]) + x["y_const"]
        for key, x in cur:
            x["y"] = x["y0"] - bdot(x["b_b"], x["p"])
            state[key] = (state[key] + x["kv"] - bdot_tn(x["bh"], x["p"])) * x["scale"]
        for d in range(2):
            dirs[d][6][0, rows(cc, d), :] = jnp.concatenate([x["y"] for (dd, h), x in cur if dd == d], axis=-1)
    for d, h in heads:
        st_ref[d, h] = state[d, h]


def _rwkv_post_kernel(yf_ref, yb_ref, bonus_ref, g_ref, lng_ref, lnb_ref, o_ref):
    y = yf_ref[0] + yb_ref[0]
    head_mean = _segment_ones(GW, HEAD_DIM, F32) * (1.0 / HEAD_DIM)
    m = _dot_f32(y, head_mean)
    yc = y - m
    var = _dot_f32(yc * yc, head_mean)
    yn = yc * lax.rsqrt(var + RWKV_GN_EPS) * lng_ref[...] + lnb_ref[...]
    o_ref[0] = (yn + bonus_ref[0]) * g_ref[0]


def _block_diag2(m):
    z = jnp.zeros_like(m[0])
    return jnp.concatenate([jnp.concatenate([m[0], z], 1), jnp.concatenate([z, m[1]], 1)], 0)


def _rwkv7_pallas(f, mu, w0, w2, a0, a2, g2, k_k, k_a, r_k, ln_g, ln_b, *, ctx_len):
    bsz, seq, cols = f.shape
    tr = ROW_BLOCK
    assert ctx_len % tr == 0 and seq % tr == 0
    nb, ctx_blocks = seq // tr, ctx_len // tr
    prev_spec, next_spec = _halo_specs(cols, tr)
    row2 = lambda a: a.reshape(1, -1).astype(F32)
    full = lambda a: pl.BlockSpec(a.shape, lambda b, i: (0,) * a.ndim)
    params = [mu, row2(w0), _block_diag2(w2), row2(a0), _block_diag2(a2), g2, row2(k_k), row2(k_a), row2(r_k)]
    act = jax.ShapeDtypeStruct((bsz, seq, GW), F32)
    act2 = jax.ShapeDtypeStruct((2, bsz, seq, GW), F32)
    blk = pl.BlockSpec((1, tr, GW), lambda b, i: (b, i, 0))
    blk2 = pl.BlockSpec((2, 1, tr, GW), lambda b, i: (0, b, i, 0))
    r, v, kkn, g, bonus, w, bb, kt = pl.pallas_call(
        functools.partial(_rwkv_prep_kernel, ctx_blocks=ctx_blocks),
        name="rwkv_prep",
        grid=(bsz, nb),
        in_specs=[pl.BlockSpec((1, tr, cols), lambda b, i: (b, i, 0)), prev_spec, next_spec(seq // 8)]
                 + [full(p) for p in params],
        out_specs=[blk] * 5 + [blk2] * 3,
        out_shape=[act] * 5 + [act2] * 3,
        compiler_params=pltpu.CompilerParams(dimension_semantics=("arbitrary", "arbitrary"),
                                             vmem_limit_bytes=V7X_VMEM_LIMIT_BYTES),
    )(f, f, f, *params)

    def bwd_block(c):
        return jnp.where(c < ctx_blocks, ctx_blocks - 1 - c, nb - 1 - (c - ctx_blocks))
    assert tr % RWKV_CHUNK == 0 and RWKV_CHUNK == HEAD_DIM
    fwd = pl.BlockSpec((1, tr, GW), lambda b, c: (b, c, 0))
    bwd = pl.BlockSpec((1, tr, GW), lambda b, c: (b, bwd_block(c), 0))
    fwd_d = pl.BlockSpec((1, 1, tr, GW), lambda b, c: (0, b, c, 0))
    bwd_d = pl.BlockSpec((1, 1, tr, GW), lambda b, c: (1, b, bwd_block(c), 0))
    yf, yb = pl.pallas_call(
        _rwkv_chunk_kernel,
        name="rwkv_chunks",
        grid=(bsz, nb),
        in_specs=[fwd, fwd, fwd, fwd_d, fwd_d, fwd_d, bwd, bwd, bwd, bwd_d, bwd_d, bwd_d],
        out_specs=[fwd, bwd],
        out_shape=[act, act],
        scratch_shapes=[pltpu.VMEM((2, GROUP_HEADS, HEAD_DIM, HEAD_DIM), F32)],
        compiler_params=pltpu.CompilerParams(dimension_semantics=("arbitrary", "arbitrary"),
                                             vmem_limit_bytes=V7X_VMEM_LIMIT_BYTES),
    )(r, v, kkn, w, bb, kt, r, v, kkn, w, bb, kt)

    return pl.pallas_call(
        _rwkv_post_kernel,
        name="rwkv_post",
        grid=(bsz, nb),
        in_specs=[blk, blk, blk, blk, full(row2(ln_g)), full(row2(ln_b))],
        out_specs=blk,
        out_shape=act,
        compiler_params=pltpu.CompilerParams(dimension_semantics=("arbitrary", "arbitrary")),
    )(yf, yb, bonus, g, row2(ln_g), row2(ln_b))


GDN_GATE_LANES = 128
GDN_PADDED_COLS = 4 * GW + GDN_GATE_LANES


def _gdn_prep_kernel(x_ref, xprev_ref, xnext_ref, ab_ref, conv_ref, alog_ref, dtb_ref,
                     q_out, k_out, v_out, gb_out, *, ctx_blocks):
    i = pl.program_id(1)
    keep_prev, keep_next = _segment_edge_flags(i, pl.num_programs(1), ctx_blocks)
    x = x_ref[0]
    xm, xp = _shifted_rows(x, xprev_ref[0, 7:8, :] * keep_prev, xnext_ref[0, 0:1, :] * keep_next)
    y = xm * conv_ref[0:1, :] + x * conv_ref[1:2, :] + xp * conv_ref[2:3, :]
    y = y * jax.nn.sigmoid(y)
    head_sum = _segment_ones(GW, HEAD_DIM, F32)
    q = y[:, 0:GW]
    k = y[:, GW:2 * GW]
    q_out[0] = q * lax.rsqrt(_dot_f32(q * q, head_sum) + 1e-6) * (HEAD_DIM ** -0.5)
    k_out[0] = k * lax.rsqrt(_dot_f32(k * k, head_sum) + 1e-6)
    v_out[0] = y[:, 2 * GW:3 * GW]
    ab = ab_ref[0]
    lane = lax.broadcasted_iota(jnp.int32, ab.shape, 1)
    log_alpha = -jnp.exp(alog_ref[...]) * _softplus(ab + dtb_ref[...])
    gb_out[0] = jnp.where(lane < 2 * GROUP_HEADS, log_alpha, jax.nn.sigmoid(ab))


def _gdn_chunk_kernel(qf_ref, kf_ref, vf_ref, gf_ref, qb_ref, kb_ref, vb_ref, gb_ref, of_ref, ob_ref, st_ref):
    i = pl.program_id(1)
    c = GDN_CHUNK
    n_chunks = qf_ref.shape[1] // c

    @pl.when(i == 0)
    def _():
        st_ref[...] = jnp.zeros_like(st_ref)

    r = lax.broadcasted_iota(jnp.int32, (c, c), 0)
    s = lax.broadcasted_iota(jnp.int32, (c, c), 1)
    eye = r == s
    ones_cc = jnp.ones((c, c), F32)
    incl = (r >= s, r <= s)
    strict = (r > s, r < s)
    levels = []
    b = 1
    while b < c:
        levels.append(jnp.logical_and(r // (2 * b) == s // (2 * b), r // b != s // b))
        b *= 2
    dirs = ((qf_ref, kf_ref, vf_ref, gf_ref, of_ref), (qb_ref, kb_ref, vb_ref, gb_ref, ob_ref))

    bdot = lambda x, y: jnp.dot(x.astype(BF16), y.astype(BF16), preferred_element_type=F32)
    bdot_nt = lambda x, y: lax.dot_general(x.astype(BF16), y.astype(BF16), _NT, preferred_element_type=F32)
    bdot_tn = lambda x, y: lax.dot_general(x.astype(BF16), y.astype(BF16), (((0,), (0,)), ((), ())),
                                           preferred_element_type=F32)

    heads = [(d, h) for d in range(2) for h in range(GROUP_HEADS)]
    row0 = lambda cc, d: (cc if d == 0 else n_chunks - 1 - cc) * c
    gates = {(cc, d): dirs[d][3][0, row0(cc, d):row0(cc, d) + c, :] for cc in range(n_chunks) for d in range(2)}
    cum = {key: _dot_f32(jnp.where(incl[key[1]], 1.0, 0.0), g) for key, g in gates.items()}
    tot = {key: _dot_f32(ones_cc, g) for key, g in gates.items()}
    chains = [(cc, d, h) for cc in range(n_chunks) for d, h in heads]
    st = {}
    for cc, d, h in chains:
        cols = slice(h * HEAD_DIM, (h + 1) * HEAD_DIM)
        lg = d * GROUP_HEADS + h
        q, k, v = (dirs[d][n][0, row0(cc, d):row0(cc, d) + c, cols] for n in range(3))
        gc = cum[cc, d][:, lg:lg + 1]
        st[cc, d, h] = dict(q=q, k=k, v=v, gc=gc, gt=tot[cc, d][:, lg:lg + 1],
                            beta=gates[cc, d][:, 2 * GROUP_HEADS + lg:2 * GROUP_HEADS + lg + 1],
                            gc_row=_dot_f32(ones_cc, jnp.where(eye, gc, 0.0)))
    for (cc, d, h), x in st.items():
        x["decay"] = jnp.exp(jnp.where(incl[d], x["gc"] - x["gc_row"], NEG_BIG))
        x["kb"] = x["k"] * x["beta"]
        x["a"] = jnp.where(strict[d], bdot_nt(x["kb"], x["k"]) * x["decay"], 0.0)
        x["qk"] = jnp.where(incl[d], bdot_nt(x["q"], x["k"]) * x["decay"], 0.0)
        x["inv"] = jnp.where(eye, 1.0, 0.0)
    for lvl in levels:
        for x in st.values():
            x["t"] = bdot(jnp.where(lvl, x["a"], 0.0), x["inv"])
        for x in st.values():
            x["inv"] = x["inv"] - bdot(x["inv"], x["t"])
    for x in st.values():
        x["eg"] = jnp.exp(x["gc"])
        x["sol"] = bdot(x["inv"], jnp.concatenate([x["v"] * x["beta"], x["kb"] * x["eg"]], axis=-1))
        x["qg"] = x["q"] * x["eg"]
        x["kg"] = x["k"] * jnp.exp(x["gt"] - x["gc"])
    state = {(d, h): st_ref[d, h] for d, h in heads}
    for cc in range(n_chunks):
        cur = [(key, st[(cc,) + key]) for key in heads]
        for key, x in cur:
            x["ws"] = bdot(x["sol"][:, HEAD_DIM:], state[key])
            x["qs"] = bdot(x["qg"], state[key])
        for key, x in cur:
            x["v_new"] = x["sol"][:, :HEAD_DIM] - x["ws"]
            x["o"] = x["qs"] + bdot(x["qk"], x["v_new"])
            x["upd"] = bdot_tn(x["kg"], x["v_new"])
        for key, x in cur:
            state[key] = state[key] * jnp.exp(x["gt"][0:1, :]) + x["upd"]
        for d in range(2):
            dirs[d][4][0, row0(cc, d):row0(cc, d) + c, :] = jnp.concatenate(
                [x["o"] for (dd, h), x in cur if dd == d], axis=-1)
    for d, h in heads:
        st_ref[d, h] = state[d, h]


def _gdn_post_kernel(of_ref, ob_ref, gate_ref, g_ref, o_ref):
    o = of_ref[0] + ob_ref[0]
    head_mean = _segment_ones(GW, HEAD_DIM, F32) * (1.0 / HEAD_DIM)
    gate = gate_ref[0]
    o_ref[0] = o * lax.rsqrt(_dot_f32(o * o, head_mean) + 1e-6) * g_ref[...] * (gate * jax.nn.sigmoid(gate))


def _gated_deltanet_pallas(f, conv_w, a_log, dt_bias, norm_g, *, ctx_len):
    bsz, seq, width = f.shape
    tr = ROW_BLOCK
    assert ctx_len % tr == 0 and seq % tr == 0 and tr % GDN_CHUNK == 0
    if width == GDN_COLS:
        f = jnp.pad(f, ((0, 0), (0, 0), (0, GDN_PADDED_COLS - GDN_COLS)))
    assert f.shape[2] == GDN_PADDED_COLS
    nb, ctx_blocks = seq // tr, ctx_len // tr
    lane_pad = lambda a: jnp.pad(a.reshape(1, -1).astype(F32), ((0, 0), (0, GDN_GATE_LANES - a.size)))
    prev_spec, next_spec = _halo_specs(3 * GW, tr)
    full = lambda a: pl.BlockSpec(a.shape, lambda b, i: (0,) * a.ndim)
    act = jax.ShapeDtypeStruct((bsz, seq, GW), F32)
    gact = jax.ShapeDtypeStruct((bsz, seq, GDN_GATE_LANES), F32)
    blk = pl.BlockSpec((1, tr, GW), lambda b, i: (b, i, 0))
    gblk = pl.BlockSpec((1, tr, GDN_GATE_LANES), lambda b, i: (b, i, 0))
    gate_view = pl.BlockSpec((1, tr, GW), lambda b, i: (b, i, 3))
    ab_view = pl.BlockSpec((1, tr, GDN_GATE_LANES), lambda b, i: (b, i, 4 * GW // GDN_GATE_LANES))
    params = [conv_w.astype(F32), lane_pad(a_log), lane_pad(dt_bias)]
    q, k, v, gb = pl.pallas_call(
        functools.partial(_gdn_prep_kernel, ctx_blocks=ctx_blocks),
        name="gdn_prep",
        grid=(bsz, nb),
        in_specs=[pl.BlockSpec((1, tr, 3 * GW), lambda b, i: (b, i, 0)), prev_spec, next_spec(seq // 8), ab_view]
                 + [full(p) for p in params],
        out_specs=[blk, blk, blk, gblk],
        out_shape=[act, act, act, gact],
        compiler_params=pltpu.CompilerParams(dimension_semantics=("arbitrary", "arbitrary"),
                                             vmem_limit_bytes=V7X_VMEM_LIMIT_BYTES),
    )(f, f, f, f, *params)

    def bwd_block(i):
        return jnp.where(i < ctx_blocks, ctx_blocks - 1 - i, nb - 1 - (i - ctx_blocks))
    bblk = pl.BlockSpec((1, tr, GW), lambda b, i: (b, bwd_block(i), 0))
    bgblk = pl.BlockSpec((1, tr, GDN_GATE_LANES), lambda b, i: (b, bwd_block(i), 0))
    of, ob = pl.pallas_call(
        _gdn_chunk_kernel,
        name="gdn_chunks",
        grid=(bsz, nb),
        in_specs=[blk, blk, blk, gblk, bblk, bblk, bblk, bgblk],
        out_specs=[blk, bblk],
        out_shape=[act, act],
        scratch_shapes=[pltpu.VMEM((2, GROUP_HEADS, HEAD_DIM, HEAD_DIM), F32)],
        compiler_params=pltpu.CompilerParams(dimension_semantics=("arbitrary", "arbitrary"),
                                             vmem_limit_bytes=V7X_VMEM_LIMIT_BYTES),
    )(q, k, v, gb, q, k, v, gb)

    g_row = jnp.tile(norm_g.reshape(1, HEAD_DIM).astype(F32), (1, GROUP_HEADS))
    return pl.pallas_call(
        _gdn_post_kernel,
        name="gdn_post",
        grid=(bsz, nb),
        in_specs=[blk, blk, gate_view, full(g_row)],
        out_specs=blk,
        out_shape=act,
        compiler_params=pltpu.CompilerParams(dimension_semantics=("arbitrary", "arbitrary")),
    )(of, ob, f, g_row)


ROPE_PAIR = DIFF_HALF // 4


def _rope_tables(seq, ctx_len, q_scale):
    n = jnp.arange(seq - ctx_len, dtype=jnp.int32)
    row, col = n // GRID_W, n % GRID_W
    i = jnp.arange(HEAD_DIM)
    grp = (i % DIFF_HALF) // (2 * ROPE_PAIR)
    inv = ROPE_BASE ** (-(i % ROPE_PAIR).astype(F32) / ROPE_PAIR)
    pos = jnp.where(grp[None, :] == 0, row[:, None], col[:, None]).astype(F32)
    ang = pos * inv[None, :]
    sign = jnp.where((i % (2 * ROPE_PAIR)) < ROPE_PAIR, -1.0, 1.0)
    cos = jnp.concatenate([jnp.ones((ctx_len, HEAD_DIM), F32), jnp.cos(ang)], 0)
    sin = jnp.concatenate([jnp.zeros((ctx_len, HEAD_DIM), F32), jnp.sin(ang) * sign], 0)
    cos = jnp.tile(cos, (1, GROUP_HEADS))
    sin = jnp.tile(sin, (1, GROUP_HEADS))
    return jnp.concatenate([cos * q_scale, cos], 1), jnp.concatenate([sin * q_scale, sin], 1)


def _qkv_prep_kernel(p_ref, cos_ref, sin_ref, q_out, k_out, v_out):
    qk = p_ref[0, :, 0:2 * GW]
    width = 2 * GW
    lane = lax.broadcasted_iota(jnp.int32, qk.shape, 1)
    partner = jnp.where(lane % (2 * ROPE_PAIR) < ROPE_PAIR,
                        pltpu.roll(qk, width - ROPE_PAIR, axis=1), pltpu.roll(qk, ROPE_PAIR, axis=1))
    rot = qk * cos_ref[...] + partner * sin_ref[...]
    q_out[0] = rot[:, 0:GW].astype(BF16)
    v = p_ref[0, :, 2 * GW:3 * GW]
    for h in range(GROUP_HEADS):
        k_out[0, h] = rot[:, GW + h * HEAD_DIM:GW + (h + 1) * HEAD_DIM].astype(BF16)
        v_out[0, h] = v[:, h * HEAD_DIM:(h + 1) * HEAD_DIM].astype(BF16)


def _qkv_prep(p, cos, sin):
    bsz, seq, _ = p.shape
    tr = ROW_BLOCK
    head_major = jax.ShapeDtypeStruct((bsz, GROUP_HEADS, seq, HEAD_DIM), BF16)
    hm_spec = pl.BlockSpec((1, GROUP_HEADS, tr, HEAD_DIM), lambda b, i: (b, 0, i, 0))
    return pl.pallas_call(
        _qkv_prep_kernel,
        name="qkv_prep",
        grid=(bsz, seq // tr),
        in_specs=[pl.BlockSpec((1, tr, 3 * GW), lambda b, i: (b, i, 0)),
                  pl.BlockSpec((tr, 2 * GW), lambda b, i: (i, 0)),
                  pl.BlockSpec((tr, 2 * GW), lambda b, i: (i, 0))],
        out_specs=[pl.BlockSpec((1, tr, GW), lambda b, i: (b, i, 0)), hm_spec, hm_spec],
        out_shape=[jax.ShapeDtypeStruct((bsz, seq, GW), BF16), head_major, head_major],
        compiler_params=pltpu.CompilerParams(dimension_semantics=("arbitrary", "arbitrary")),
    )(p, cos, sin)


_NT = (((1,), (1,)), ((), ()))


def _softmax_pv(s, v):
    m = jnp.max(s, axis=-1, keepdims=True)
    e = jnp.exp(s - m)
    return jnp.dot(e.astype(BF16), v, preferred_element_type=F32) / jnp.sum(e, axis=-1, keepdims=True)


def _diff_attn_kernel(q_ref, k_ref, v_ref, lam_ref, g_ref, o_ref, *, ctx_blocks, ctx_len):
    i = pl.program_id(1)
    lv = lam_ref[...]
    lam_init = lv[4:5, 0:1]
    lam = (jnp.exp(jnp.sum(lv[0:1] * lv[1:2], axis=-1, keepdims=True))
           - jnp.exp(jnp.sum(lv[2:3] * lv[3:4], axis=-1, keepdims=True)) + lam_init)
    lane = lax.broadcasted_iota(jnp.int32, (q_ref.shape[1], HEAD_DIM), 1)

    def attend(n_keys):
        outs = []
        for h in range(GROUP_HEADS):
            qh = q_ref[0, :, h * HEAD_DIM:(h + 1) * HEAD_DIM]
            kh = k_ref[0, h, 0:n_keys, :]
            vh = v_ref[0, h, 0:n_keys, :]
            zero = jnp.zeros_like(qh)
            s1 = lax.dot_general(jnp.where(lane < DIFF_HALF, qh, zero), kh, _NT, preferred_element_type=F32)
            s2 = lax.dot_general(jnp.where(lane >= DIFF_HALF, qh, zero), kh, _NT, preferred_element_type=F32)
            o = _softmax_pv(s1, vh) - lam * _softmax_pv(s2, vh)
            o = o * lax.rsqrt(jnp.mean(o * o, axis=-1, keepdims=True) + 1e-6) * g_ref[...] * (1.0 - lam_init)
            outs.append(o)
        o_ref[0] = jnp.concatenate(outs, axis=-1)

    @pl.when(i < ctx_blocks)
    def _():
        attend(ctx_len)

    @pl.when(i >= ctx_blocks)
    def _():
        attend(k_ref.shape[2])


def _diff_attention_pallas(q, k, v, lam_vecs, norm_g, *, ctx_len, lam_init):
    bsz, seq, _ = q.shape
    tq = ROW_BLOCK
    kv_spec = pl.BlockSpec((1, GROUP_HEADS, seq, HEAD_DIM), lambda b, i: (b, 0, 0, 0))
    lam_rows = jnp.concatenate([lam_vecs.astype(F32), jnp.full((1, lam_vecs.shape[1]), lam_init, F32)], 0)
    return pl.pallas_call(
        functools.partial(_diff_attn_kernel, ctx_blocks=ctx_len // tq, ctx_len=ctx_len),
        name="diff_attn",
        grid=(bsz, seq // tq),
        in_specs=[pl.BlockSpec((1, tq, GW), lambda b, i: (b, i, 0)), kv_spec, kv_spec,
                  pl.BlockSpec(lam_rows.shape, lambda b, i: (0, 0)),
                  pl.BlockSpec((1, HEAD_DIM), lambda b, i: (0, 0))],
        out_specs=pl.BlockSpec((1, tq, GW), lambda b, i: (b, i, 0)),
        out_shape=jax.ShapeDtypeStruct((bsz, seq, GW), F32),
        compiler_params=pltpu.CompilerParams(dimension_semantics=("arbitrary", "arbitrary"),
                                             vmem_limit_bytes=V7X_VMEM_LIMIT_BYTES),
    )(q, k, v, lam_rows, norm_g.reshape(1, HEAD_DIM).astype(F32))


NAT_TILE_ROWS = ROW_BLOCK // GRID_W
NAT_SLAB_ROWS = NAT_TILE_ROWS + WIN_H - 1


def _nat_slab_start(tile, n_rows):
    return np.clip(tile * NAT_TILE_ROWS - WIN_H // 2, 0, n_rows - NAT_SLAB_ROWS)


def _nat_bias_tables(rpb, n_rows):
    n_tiles = n_rows // NAT_TILE_ROWS
    nq, nk, w = NAT_TILE_ROWS, NAT_SLAB_ROWS, GRID_W
    cq, ck = np.arange(w)[:, None], np.arange(w)[None, :]
    d_col = np.clip(ck - cq, -(WIN_W - 1), WIN_W - 1) + WIN_W - 1
    col_1h = (d_col.reshape(-1)[:, None] == np.arange(2 * WIN_W - 1)[None, :]).astype(np.float32)
    c0 = np.clip(cq - WIN_W // 2, 0, w - WIN_W)
    col_ok = (ck >= c0) & (ck < c0 + WIN_W)
    tabs = []
    for tile in (0, 1, n_tiles - 1):
        r = tile * nq + np.arange(nq)[:, None]
        kr = _nat_slab_start(tile, n_rows) + np.arange(nk)[None, :]
        rs = np.clip(r - WIN_H // 2, 0, n_rows - WIN_H)
        row_ok = (kr >= rs) & (kr < rs + WIN_H)
        d_row = np.clip(kr - r + WIN_H - 1, 0, 2 * WIN_H - 2)
        row_1h = (d_row.reshape(-1)[:, None] == np.arange(2 * WIN_H - 1)[None, :]).astype(np.float32)
        t = jnp.einsum('pa,hab,cb->hpc', row_1h, rpb.astype(F32), col_1h, precision=HIGHEST)
        t = t.reshape(GROUP_HEADS, nq, nk, w, w).transpose(0, 1, 3, 2, 4)
        ok = row_ok[:, None, :, None] & col_ok[None, :, None, :]
        tabs.append(jnp.where(ok[None], t, NEG_BIG).reshape(GROUP_HEADS, nq * w, nk * w))
    return jnp.stack(tabs)


def _nat_attn_kernel(q_ref, k_ref, v_ref, bias_ref, o_ref, *, ctx_blocks, ctx_len, n_rows):
    i = pl.program_id(1)
    n_slab = NAT_SLAB_ROWS * GRID_W

    def heads(fn):
        o_ref[0] = jnp.concatenate(
            [fn(h, q_ref[0, :, h * HEAD_DIM:(h + 1) * HEAD_DIM]) for h in range(GROUP_HEADS)], axis=-1)

    @pl.when(i < ctx_blocks)
    def _():
        def ctx_only(h, qh):
            s = lax.dot_general(qh, k_ref[0, h, 0:ctx_len, :], _NT, preferred_element_type=F32)
            return _softmax_pv(s, v_ref[0, h, 0:ctx_len, :])
        heads(ctx_only)

    @pl.when(i >= ctx_blocks)
    def _():
        tile = i - ctx_blocks
        start = jnp.clip(tile * NAT_TILE_ROWS - WIN_H // 2, 0, n_rows - NAT_SLAB_ROWS)
        off = pl.multiple_of(ctx_len + start * GRID_W, GRID_W)

        def windowed(h, qh):
            s_w = lax.dot_general(qh, k_ref[0, h, pl.ds(off, n_slab), :], _NT,
                                  preferred_element_type=F32) + bias_ref[0, h]
            s_c = lax.dot_general(qh, k_ref[0, h, 0:ctx_len, :], _NT, preferred_element_type=F32)
            m = jnp.maximum(jnp.max(s_w, axis=-1, keepdims=True), jnp.max(s_c, axis=-1, keepdims=True))
            e_w = jnp.exp(s_w - m)
            e_c = jnp.exp(s_c - m)
            den = jnp.sum(e_w, axis=-1, keepdims=True) + jnp.sum(e_c, axis=-1, keepdims=True)
            num = (jnp.dot(e_w.astype(BF16), v_ref[0, h, pl.ds(off, n_slab), :], preferred_element_type=F32)
                   + jnp.dot(e_c.astype(BF16), v_ref[0, h, 0:ctx_len, :], preferred_element_type=F32))
            return num / den
        heads(windowed)


def _nat_attention_pallas(q, k, v, rpb, *, ctx_len):
    bsz, seq, _ = q.shape
    tq = ROW_BLOCK
    ctx_blocks = ctx_len // tq
    n_rows = (seq - ctx_len) // GRID_W
    n_tiles = n_rows // NAT_TILE_ROWS
    assert n_rows >= NAT_SLAB_ROWS and n_tiles >= 3
    bias = _nat_bias_tables(rpb, n_rows)

    def variant(i):
        tile = i - ctx_blocks
        return jnp.where(tile <= 0, 0, jnp.where(tile >= n_tiles - 1, 2, 1))
    kv_spec = pl.BlockSpec((1, GROUP_HEADS, seq, HEAD_DIM), lambda b, i: (b, 0, 0, 0))
    return pl.pallas_call(
        functools.partial(_nat_attn_kernel, ctx_blocks=ctx_blocks, ctx_len=ctx_len, n_rows=n_rows),
        name="nat_attn",
        grid=(bsz, seq // tq),
        in_specs=[pl.BlockSpec((1, tq, GW), lambda b, i: (b, i, 0)), kv_spec, kv_spec,
                  pl.BlockSpec((1,) + bias.shape[1:], lambda b, i: (variant(i), 0, 0, 0))],
        out_specs=pl.BlockSpec((1, tq, GW), lambda b, i: (b, i, 0)),
        out_shape=jax.ShapeDtypeStruct((bsz, seq, GW), F32),
        compiler_params=pltpu.CompilerParams(dimension_semantics=("arbitrary", "arbitrary"),
                                             vmem_limit_bytes=V7X_VMEM_LIMIT_BYTES),
    )(q, k, v, bias)


N_MOD = 6
MATMUL_ROWS = 512


def _ada_kernel(c_ref, w_ref, b_ref, o_ref):
    c = c_ref[...]
    o_ref[...] = _dot_f32(c * jax.nn.sigmoid(c), w_ref[...]) + b_ref[...]


def _ada_modulation(c, c_ctx, w_ada, b_ada):
    bsz, d = c.shape
    rows = 8 * ((bsz + 1 + 7) // 8)
    cc = jnp.zeros((rows, d), F32).at[:bsz].set(c).at[bsz].set(c_ctx)
    tn = d
    m = pl.pallas_call(
        _ada_kernel,
        name="ada_modulation",
        grid=(w_ada.shape[1] // tn,),
        in_specs=[pl.BlockSpec((rows, d), lambda j: (0, 0)),
                  pl.BlockSpec((d, tn), lambda j: (0, j)),
                  pl.BlockSpec((1, tn), lambda j: (0, j))],
        out_specs=pl.BlockSpec((rows, tn), lambda j: (0, j)),
        out_shape=jax.ShapeDtypeStruct((rows, w_ada.shape[1]), F32),
        compiler_params=pltpu.CompilerParams(dimension_semantics=("arbitrary",)),
    )(cc, w_ada, b_ada.reshape(1, -1))
    lat = m[:bsz].reshape(bsz, 1, N_MOD, d)
    ctx = jnp.broadcast_to(m[bsz].reshape(1, 1, N_MOD, d), (bsz, 1, N_MOD, d))
    return jnp.concatenate([ctx, lat], axis=1)


def _ln(x):
    mu = jnp.mean(x, axis=-1, keepdims=True)
    xc = x - mu
    return xc * lax.rsqrt(jnp.mean(xc * xc, axis=-1, keepdims=True) + LN_EPS)


def _modulate_kernel(h_ref, mod_ref, o_ref, *, shift_row):
    shift = mod_ref[0, 0, shift_row:shift_row + 1, :]
    scale = mod_ref[0, 0, shift_row + 1:shift_row + 2, :]
    o_ref[0] = (_ln(h_ref[0]) * (1.0 + scale) + shift).astype(o_ref.dtype)


def _mod_spec(d, ctx_blocks):
    return pl.BlockSpec((1, 1, N_MOD, d), lambda b, i: (b, jnp.where(i < ctx_blocks, 0, 1), 0, 0))


def _modulate_pallas(hs, mod, shift_row, *, ctx_len):
    bsz, seq, d = hs.shape
    tr = ROW_BLOCK
    blk = pl.BlockSpec((1, tr, d), lambda b, i: (b, i, 0))
    return pl.pallas_call(
        functools.partial(_modulate_kernel, shift_row=shift_row),
        name="modulate",
        grid=(bsz, seq // tr),
        in_specs=[blk, _mod_spec(d, ctx_len // tr)],
        out_specs=blk,
        out_shape=jax.ShapeDtypeStruct((bsz, seq, d), BF16),
        compiler_params=pltpu.CompilerParams(dimension_semantics=("arbitrary", "arbitrary")),
    )(hs, mod)


def _matmul_kernel(x_ref, w_ref, o_ref):
    o_ref[...] = jnp.dot(x_ref[...], w_ref[...], preferred_element_type=F32)


def _matmul_pallas(x, w):
    m, k = x.shape
    n = w.shape[1]
    tm = MATMUL_ROWS
    return pl.pallas_call(
        _matmul_kernel,
        name="in_proj",
        grid=(m // tm,),
        in_specs=[pl.BlockSpec((tm, k), lambda i: (i, 0)), pl.BlockSpec((k, n), lambda i: (0, 0))],
        out_specs=pl.BlockSpec((tm, n), lambda i: (i, 0)),
        out_shape=jax.ShapeDtypeStruct((m, n), F32),
        compiler_params=pltpu.CompilerParams(dimension_semantics=("arbitrary",),
                                             vmem_limit_bytes=V7X_VMEM_LIMIT_BYTES),
    )(x, w)


def _post_norm_rows(h, gate, y, g, b):
    return _ln(DN_ALPHA * h + gate * y) * g + b


def _out_proj_kernel(ya_ref, yb_ref, yc_ref, yd_ref, w_ref, h_ref, mod_ref, g_ref, b_ref, o_ref):
    mix = None
    for n, y_ref in enumerate((ya_ref, yb_ref, yc_ref, yd_ref)):
        part = jnp.dot(y_ref[0].astype(BF16), w_ref[n * GW:(n + 1) * GW, :], preferred_element_type=F32)
        mix = part if mix is None else mix + part
    o_ref[0] = _post_norm_rows(h_ref[0], mod_ref[0, 0, 2:3, :], mix, g_ref[...], b_ref[...])


def _out_proj_post_norm(ys, w_out, hs, mod, g, b, *, ctx_len):
    bsz, seq, d = hs.shape
    tr = ROW_BLOCK
    yblk = pl.BlockSpec((1, tr, GW), lambda bb, i: (bb, i, 0))
    blk = pl.BlockSpec((1, tr, d), lambda bb, i: (bb, i, 0))
    row = pl.BlockSpec((1, d), lambda bb, i: (0, 0))
    return pl.pallas_call(
        _out_proj_kernel,
        name="out_proj_post_norm",
        grid=(bsz, seq // tr),
        in_specs=[yblk] * 4 + [pl.BlockSpec(w_out.shape, lambda bb, i: (0, 0)), blk, _mod_spec(d, ctx_len // tr), row, row],
        out_specs=blk,
        out_shape=jax.ShapeDtypeStruct((bsz, seq, d), F32),
        compiler_params=pltpu.CompilerParams(dimension_semantics=("arbitrary", "arbitrary")),
    )(*ys, w_out, hs, mod, g.reshape(1, d), b.reshape(1, d))


def _ffn_post_norm_kernel(h_ref, y_ref, mod_ref, g_ref, b_ref, o_ref):
    o_ref[0] = _post_norm_rows(h_ref[0], mod_ref[0, 0, 5:6, :], y_ref[0], g_ref[...], b_ref[...])


def _ffn_post_norm(hs, y, mod, g, b, *, ctx_len):
    bsz, seq, d = hs.shape
    tr = ROW_BLOCK
    blk = pl.BlockSpec((1, tr, d), lambda bb, i: (bb, i, 0))
    row = pl.BlockSpec((1, d), lambda bb, i: (0, 0))
    return pl.pallas_call(
        _ffn_post_norm_kernel,
        name="ffn_post_norm",
        grid=(bsz, seq // tr),
        in_specs=[blk, blk, _mod_spec(d, ctx_len // tr), row, row],
        out_specs=blk,
        out_shape=jax.ShapeDtypeStruct((bsz, seq, d), F32),
        compiler_params=pltpu.CompilerParams(dimension_semantics=("arbitrary", "arbitrary")),
    )(hs, y, mod, g.reshape(1, d), b.reshape(1, d))


def kernel(x, c, ctx, c_ctx, w_ada, b_ada, w_in, w_out, ln_mix_g, ln_mix_b, ln_ffn_g, ln_ffn_b, diff_lam, diff_norm_g, rwkv_mu, rwkv_w0, rwkv_w2, rwkv_a0, rwkv_a2, rwkv_g2, rwkv_kk, rwkv_ka, rwkv_rk, rwkv_ln_g, rwkv_ln_b, gdn_conv, gdn_a_log, gdn_dt_bias, gdn_norm_g, nat_rpb, peer_wq, peer_keys, peer_u, peer_v):
    dtype = x.dtype
    bsz, ctx_len = ctx.shape[0], ctx.shape[1]
    hs = jnp.concatenate([ctx, x], axis=1)
    seq = hs.shape[1]
    col_sizes = [ATTN_COLS, RWKV_COLS, GDN_COLS, ATTN_COLS]
    cos_a, sin_a = _rope_tables(seq, ctx_len, DIFF_HALF ** -0.5)
    cos_d = jnp.concatenate([jnp.full((seq, GW), HEAD_DIM ** -0.5, F32), jnp.ones((seq, GW), F32)], 1)
    sin_d = jnp.zeros_like(cos_d)
    col_offs = np.cumsum([0] + col_sizes)
    d_model = hs.shape[2]
    for l in range(DEPTH):
        lam_init = 0.8 - 0.6 * math.exp(-0.3 * l)
        mod = _ada_modulation(c, c_ctx, w_ada[l], b_ada[l])
        u = _modulate_pallas(hs, mod, 0, ctx_len=ctx_len).reshape(bsz * seq, d_model)
        w_in_b = w_in[l].astype(BF16)
        w_groups = [w_in_b[:, col_offs[n]:col_offs[n + 1]] for n in range(4)]
        w_groups[2] = jnp.pad(w_groups[2], ((0, 0), (0, GDN_PADDED_COLS - GDN_COLS)))
        pa, pb, pc, pd = [_matmul_pallas(u, w).reshape(bsz, seq, w.shape[1]) for w in w_groups]
        qa, ka, va = _qkv_prep(pa, cos_a, sin_a)
        ya = _diff_attention_pallas(qa, ka, va, diff_lam[l], diff_norm_g[l], ctx_len=ctx_len, lam_init=lam_init)
        yb = _rwkv7_pallas(pb, rwkv_mu[l], rwkv_w0[l], rwkv_w2[l], rwkv_a0[l], rwkv_a2[l], rwkv_g2[l],
                           rwkv_kk[l], rwkv_ka[l], rwkv_rk[l], rwkv_ln_g[l], rwkv_ln_b[l], ctx_len=ctx_len)
        yc = _gated_deltanet_pallas(pc, gdn_conv[l], gdn_a_log[l], gdn_dt_bias[l], gdn_norm_g[l], ctx_len=ctx_len)
        qd, kd, vd = _qkv_prep(pd, cos_d, sin_d)
        yd = _nat_attention_pallas(qd, kd, vd, nat_rpb[l], ctx_len=ctx_len)
        hs = _out_proj_post_norm([ya, yb, yc, yd], w_out[l].astype(BF16), hs, mod, ln_mix_g[l], ln_mix_b[l],
                                 ctx_len=ctx_len)
        wq_b = peer_wq[l].astype(BF16)
        keys_b = peer_keys[l].reshape(2 * PEER_HEADS, PEER_KEYS, PEER_HALF).astype(BF16)
        u_b = peer_u[l].astype(BF16)
        vt_b = peer_v[l].astype(BF16).T
        ffn = _peer(_modulate_pallas(hs, mod, 3, ctx_len=ctx_len), wq_b, keys_b, u_b, vt_b)
        hs = _ffn_post_norm(hs, ffn, mod, ln_ffn_g[l], ln_ffn_b[l], ctx_len=ctx_len)
    return hs[:, ctx_len:].astype(dtype)
```

```python
import functools
import math

import jax
import jax.numpy as jnp
import numpy as np
from jax import lax
from jax.experimental import pallas as pl
from jax.experimental.pallas import tpu as pltpu

D_MODEL = 1024
DEPTH = 2
GRID_W = 64
HEAD_DIM = 64
N_GROUPS = 4
GROUP_HEADS = D_MODEL // (N_GROUPS * HEAD_DIM)
GW = GROUP_HEADS * HEAD_DIM
D_MIX = N_GROUPS * GW
DIFF_HALF = HEAD_DIM // 2
Q_BLOCK = 128
ROPE_BASE = 10000.0
DECAY_LORA = 64
ICLR_LORA = 64
GATE_LORA = 128
RWKV_GN_EPS = 64e-5
RWKV_COLS = 3 * GW + 2 * DECAY_LORA + 2 * ICLR_LORA + GATE_LORA
GDN_CONV = 3
GDN_CHUNK = 64
GDN_COLS = 4 * GW + 4 * GROUP_HEADS
WIN_H = 8
WIN_W = 16
ATTN_COLS = 3 * GW
IN_COLS = ATTN_COLS + RWKV_COLS + GDN_COLS + ATTN_COLS
PEER_HEADS = 8
PEER_KEYS = 128
PEER_EXPERTS = PEER_KEYS * PEER_KEYS
PEER_QDIM = 256
PEER_HALF = PEER_QDIM // 2
PEER_TOPK = 16
DN_ALPHA = (2 * DEPTH) ** 0.25
LN_EPS = 1e-5

F32 = jnp.float32
BF16 = jnp.bfloat16

V7X_VMEM_LIMIT_BYTES = 56 * 1024 * 1024
NEG_BIG = -3.0e38


PEER_STAT_ROWS = 4 * PEER_HEADS
LOG2E = 1.4426950408889634
assert PEER_TOPK == 16


def _topk_rows(x, k):
    rows = []
    cur = x
    for i in range(k):
        m = jnp.max(cur, axis=0, keepdims=True)
        rows.append(m)
        if i + 1 < k:
            cur = jnp.where(cur == m, NEG_BIG, cur)
    return rows


def _peer_score_kernel(x_ref, wq_ref, keys_ref, s_ref, st_ref):
    q = jnp.dot(x_ref[...], wq_ref[...], preferred_element_type=F32).astype(BF16)
    stats = []
    for h in range(PEER_HEADS):
        tops = []
        for p in range(2):
            hp = 2 * h + p
            s_t = lax.dot_general(keys_ref[hp], q[:, hp * PEER_HALF:(hp + 1) * PEER_HALF],
                                  (((1,), (1,)), ((), ())), preferred_element_type=F32)
            s_ref[hp] = s_t
            tops.append(_topk_rows(s_t, PEER_TOPK + 1))
        a, b = tops
        pad = [jnp.full_like(a[0], NEG_BIG)] * 7
        b_head = jnp.concatenate(b[:8], axis=0)
        cand = jnp.concatenate([a[0] + jnp.concatenate(b + pad, axis=0)]
                               + [a[i] + b_head for i in range(1, 8)]
                               + [jnp.concatenate(a[8:] + pad, axis=0) + b[0]], axis=0)
        best_cand = _topk_rows(cand, PEER_TOPK + 1)
        kth, runner_up = best_cand[PEER_TOPK - 1], best_cand[PEER_TOPK]
        best = a[0] + b[0]
        z = jnp.sum(jnp.where(cand >= kth, jnp.exp(cand - best), 0.0), axis=0, keepdims=True)
        stats += [0.5 * (kth + runner_up), tops[0][0], tops[1][0], 1.0 / z]
    st_ref[...] = jnp.concatenate(stats, axis=0)


def _peer_expert_kernel(x_ref, s_ref, st_ref, u_ref, vt_ref, o_ref, e_ref, thr_ref, acc_ref, w_ref,
                        *, rows_per_step, n_chunks):
    j = pl.program_id(1)
    cur = lax.rem(j, 2)

    @pl.when(j == 0)
    def _():
        acc_ref[...] = jnp.zeros_like(acc_ref)
        w_ref[1] = jnp.zeros(w_ref.shape[1:], w_ref.dtype)
        for h in range(PEER_HEADS):
            a0 = st_ref[4 * h + 1:4 * h + 2, :]
            b0 = st_ref[4 * h + 2:4 * h + 3, :]
            rz = st_ref[4 * h + 3:4 * h + 4, :]
            e_ref[2 * h] = (s_ref[2 * h] - a0 + jnp.log(rz)) * LOG2E
            e_ref[2 * h + 1] = (s_ref[2 * h + 1] - b0) * LOG2E
            thr_ref[h] = (st_ref[4 * h:4 * h + 1, :] - s_ref[2 * h] - b0) * LOG2E

    @pl.when(j < n_chunks)
    def _():
        slab = 2 * PEER_KEYS
        n_slabs = rows_per_step * PEER_KEYS // slab
        act_slab = lambda n: lax.dot_general(u_ref[n * slab:(n + 1) * slab, :], x_ref[...], (((1,), (1,)), ((), ())),
                                             preferred_element_type=F32)
        acts = [act_slab(n) for n in range(n_slabs)]
        acc_ref[...] += jnp.dot(vt_ref[...], w_ref[1 - cur], preferred_element_type=F32)
        for il in range(rows_per_step):
            i = j * rows_per_step + il
            o = il * PEER_KEYS
            act = acts[o // slab][o % slab:o % slab + PEER_KEYS]
            act = 0.5 * act * (1.0 + lax.erf(act * (2.0 ** -0.5)))
            gate = None
            for h in range(PEER_HEADS):
                l2 = e_ref[2 * h + 1]
                term = jnp.where(l2 >= thr_ref[h, pl.ds(i, 1), :], jnp.exp2(l2 + e_ref[2 * h, pl.ds(i, 1), :]), 0.0)
                gate = term if gate is None else gate + term
            w_ref[cur, o:o + PEER_KEYS, :] = (act * gate).astype(BF16)

    @pl.when(j == n_chunks)
    def _():
        o_ref[...] = (acc_ref[...] + jnp.dot(vt_ref[...], w_ref[1 - cur], preferred_element_type=F32)).T


def _peer_pallas(h, w_q, keys, u_tab, vt_tab, *, tb, rows_per_step):
    n_tok, d = h.shape
    nhp = 2 * PEER_HEADS
    s, st = pl.pallas_call(
        _peer_score_kernel,
        name="peer_scores",
        grid=(n_tok // tb,),
        in_specs=[pl.BlockSpec((tb, d), lambda i: (i, 0)),
                  pl.BlockSpec(w_q.shape, lambda i: (0, 0)),
                  pl.BlockSpec(keys.shape, lambda i: (0, 0, 0))],
        out_specs=[pl.BlockSpec((nhp, PEER_KEYS, tb), lambda i: (0, 0, i)),
                   pl.BlockSpec((PEER_STAT_ROWS, tb), lambda i: (0, i))],
        out_shape=[jax.ShapeDtypeStruct((nhp, PEER_KEYS, n_tok), F32),
                   jax.ShapeDtypeStruct((PEER_STAT_ROWS, n_tok), F32)],
        compiler_params=pltpu.CompilerParams(dimension_semantics=("arbitrary",),
                                             vmem_limit_bytes=V7X_VMEM_LIMIT_BYTES),
    )(h, w_q, keys)
    ec = rows_per_step * PEER_KEYS
    n_chunks = PEER_EXPERTS // ec
    return pl.pallas_call(
        functools.partial(_peer_expert_kernel, rows_per_step=rows_per_step, n_chunks=n_chunks),
        name="peer_experts",
        grid=(n_tok // tb, n_chunks + 1),
        in_specs=[pl.BlockSpec((tb, d), lambda i, j: (i, 0)),
                  pl.BlockSpec((nhp, PEER_KEYS, tb), lambda i, j: (0, 0, i)),
                  pl.BlockSpec((PEER_STAT_ROWS, tb), lambda i, j: (0, i)),
                  pl.BlockSpec((ec, d), lambda i, j: (jnp.minimum(j, n_chunks - 1), 0)),
                  pl.BlockSpec((d, ec), lambda i, j: (0, jnp.maximum(j - 1, 0)))],
        out_specs=pl.BlockSpec((tb, d), lambda i, j: (i, 0)),
        out_shape=jax.ShapeDtypeStruct((n_tok, d), F32),
        scratch_shapes=[pltpu.VMEM((nhp, PEER_KEYS, tb), F32),
                        pltpu.VMEM((PEER_HEADS, PEER_KEYS, tb), F32),
                        pltpu.VMEM((d, tb), F32),
                        pltpu.VMEM((2, ec, tb), BF16)],
        compiler_params=pltpu.CompilerParams(dimension_semantics=("arbitrary", "arbitrary"),
                                             vmem_limit_bytes=V7X_VMEM_LIMIT_BYTES),
    )(h, s, st, u_tab, vt_tab)


def _peer(h, w_q, keys, u_tab, vt_tab):
    bsz, t, d = h.shape
    n_tok = bsz * t
    tb = 512 if n_tok % 512 == 0 else 256
    out = _peer_pallas(h.reshape(n_tok, d).astype(BF16), w_q, keys, u_tab, vt_tab, tb=tb, rows_per_step=16)
    return out.reshape(bsz, t, d)


ROW_BLOCK = 256
HIGHEST = lax.Precision.HIGHEST


def _dot_f32(a, b):
    return jnp.dot(a, b, precision=HIGHEST, preferred_element_type=F32)


def _segment_ones(n, seg, dtype):
    r = lax.broadcasted_iota(jnp.int32, (n, n), 0) // seg
    c = lax.broadcasted_iota(jnp.int32, (n, n), 1) // seg
    return jnp.where(r == c, 1.0, 0.0).astype(dtype)


def _shifted_rows(x, prev_row, next_row):
    t = x.shape[0]
    rows = lax.broadcasted_iota(jnp.int32, x.shape, 0)
    xm = jnp.where(rows == 0, prev_row, pltpu.roll(x, 1, axis=0))
    xp = jnp.where(rows == t - 1, next_row, pltpu.roll(x, t - 1, axis=0))
    return xm, xp


def _segment_edge_flags(i, n_blocks, ctx_blocks):
    is_start = jnp.logical_or(i == 0, i == ctx_blocks)
    is_end = jnp.logical_or(i == ctx_blocks - 1, i == n_blocks - 1)
    return jnp.where(is_start, 0.0, 1.0), jnp.where(is_end, 0.0, 1.0)


def _halo_specs(width, tr):
    g = tr // 8
    prev = pl.BlockSpec((1, 8, width), lambda b, i: (b, jnp.maximum(i * g - 1, 0), 0))
    nxt = lambda n_groups: pl.BlockSpec((1, 8, width), lambda b, i: (b, jnp.minimum((i + 1) * g, n_groups - 1), 0))
    return prev, nxt


def _softplus(z):
    return jnp.maximum(z, 0.0) + jnp.log1p(jnp.exp(-jnp.abs(z)))


def _rwkv_prep_kernel(x_ref, xprev_ref, xnext_ref, mu_ref, w0_ref, w2_ref, a0_ref, a2_ref, g2_ref,
                      kk_ref, ka_ref, rk_ref,
                      r_out, v_out, kkn_out, g_out, bonus_out, w_out, b_out, kt_out, *, ctx_blocks):
    i = pl.program_id(1)
    keep_prev, keep_next = _segment_edge_flags(i, pl.num_programs(1), ctx_blocks)
    x = x_ref[0]
    xm, xp = _shifted_rows(x, xprev_ref[0, 7:8, :] * keep_prev, xnext_ref[0, 0:1, :] * keep_next)
    mu0 = mu_ref[0:1, :]
    mu1 = mu_ref[1:2, :]
    f = xm * mu0 + x * (1.0 - mu0 - mu1) + xp * mu1
    r = f[:, 0:GW]
    k = f[:, GW:2 * GW]
    v = f[:, 2 * GW:3 * GW]
    o = 3 * GW
    wd = jnp.tanh(f[:, o:o + 2 * DECAY_LORA])
    ad = f[:, o + 2 * DECAY_LORA:o + 2 * DECAY_LORA + 2 * ICLR_LORA]
    gd = f[:, o + 2 * DECAY_LORA + 2 * ICLR_LORA:]
    w_raw = w0_ref[...] + _dot_f32(wd, w2_ref[...])
    log_decay = -jnp.exp(-_softplus(-w_raw) - 0.5)
    a = jax.nn.sigmoid(a0_ref[...] + _dot_f32(ad, a2_ref[...]))
    g = _dot_f32(jax.nn.sigmoid(gd), g2_ref[...])
    head_sum = _segment_ones(GW, HEAD_DIM, F32)
    kx = k * kk_ref[...]
    kkn = kx * lax.rsqrt(_dot_f32(kx * kx, head_sum) + 1e-6)
    kd_sum = jnp.zeros_like(k)
    for d in range(2):
        a_d = a[:, d * GW:(d + 1) * GW]
        kd = k * (1.0 + (a_d - 1.0) * ka_ref[...])
        kd_sum = kd_sum + kd
        w_out[d, 0] = log_decay[:, d * GW:(d + 1) * GW]
        b_out[d, 0] = kkn * a_d
        kt_out[d, 0] = kd
    r_out[0] = r
    v_out[0] = v
    kkn_out[0] = kkn
    g_out[0] = g
    bonus_out[0] = _dot_f32(r * kd_sum * rk_ref[...], head_sum) * v


RWKV_CHUNK = 64


def _rwkv_chunk_kernel(rf_ref, vf_ref, kkf_ref, wf_ref, bf_ref, ktf_ref,
                       rb_ref, vb_ref, kkb_ref, wb_ref, bb_ref, ktb_ref, yf_ref, yb_ref, st_ref):
    i = pl.program_id(1)
    c = RWKV_CHUNK
    n_chunks = rf_ref.shape[1] // c

    @pl.when(i == 0)
    def _():
        st_ref[...] = jnp.zeros_like(st_ref)

    rr = lax.broadcasted_iota(jnp.int32, (c, c), 0)
    ss = lax.broadcasted_iota(jnp.int32, (c, c), 1)
    eye = rr == ss
    ones_cc = jnp.ones((c, c), F32)
    incl = (rr >= ss, rr <= ss)
    strict = (rr > ss, rr < ss)
    levels = []
    blk = 1
    while blk < c:
        levels.append(jnp.logical_and(rr // (2 * blk) == ss // (2 * blk), rr // blk != ss // blk))
        blk *= 2
    bdot = lambda x, y: jnp.dot(x.astype(BF16), y.astype(BF16), preferred_element_type=F32)
    bdot_nt = lambda x, y: lax.dot_general(x.astype(BF16), y.astype(BF16), _NT, preferred_element_type=F32)
    bdot_tn = lambda x, y: lax.dot_general(x.astype(BF16), y.astype(BF16), (((0,), (0,)), ((), ())),
                                           preferred_element_type=F32)
    dirs = ((rf_ref, vf_ref, kkf_ref, wf_ref, bf_ref, ktf_ref, yf_ref),
            (rb_ref, vb_ref, kkb_ref, wb_ref, bb_ref, ktb_ref, yb_ref))
    heads = [(d, h) for d in range(2) for h in range(GROUP_HEADS)]
    row0 = lambda cc, d: (cc if d == 0 else n_chunks - 1 - cc) * c
    rows = lambda cc, d: slice(row0(cc, d), row0(cc, d) + c)

    logw = {(cc, d): dirs[d][3][0, 0, rows(cc, d), :] for cc in range(n_chunks) for d in range(2)}
    cum = {key: _dot_f32(jnp.where(incl[key[1]], 1.0, 0.0), lw) for key, lw in logw.items()}
    tot = {key: _dot_f32(ones_cc, lw) for key, lw in logw.items()}
    st = {}
    for cc in range(n_chunks):
        for d, h in heads:
            cols = slice(h * HEAD_DIM, (h + 1) * HEAD_DIM)
            r_ref, v_ref, kk_ref, _, b_ref, kt_ref, _ = dirs[d]
            lc = cum[cc, d][:, cols]
            grow = jnp.exp(-lc)
            st[cc, d, h] = dict(
                v=v_ref[0, rows(cc, d), cols],
                kap=kk_ref[0, rows(cc, d), cols] * jnp.exp(lc - logw[cc, d][:, cols]),
                bh=b_ref[0, 0, rows(cc, d), cols] * grow,
                kh=kt_ref[0, 0, rows(cc, d), cols] * grow,
                rh=r_ref[0, rows(cc, d), cols] * jnp.exp(lc),
                scale=jnp.exp(_dot_f32(jnp.where(eye, tot[cc, d][:, cols], 0.0), ones_cc)))
    for (cc, d, h), x in st.items():
        x["a_b"] = jnp.where(strict[d], bdot_nt(x["kap"], x["bh"]), 0.0)
        x["a_k"] = jnp.where(strict[d], bdot_nt(x["kap"], x["kh"]), 0.0)
        x["b_k"] = jnp.where(incl[d], bdot_nt(x["rh"], x["kh"]), 0.0)
        x["b_b"] = jnp.where(incl[d], bdot_nt(x["rh"], x["bh"]), 0.0)
        x["inv"] = jnp.where(eye, 1.0, 0.0)
    for lvl in levels:
        for x in st.values():
            x["t"] = bdot(jnp.where(lvl, x["a_b"], 0.0), x["inv"])
        for x in st.values():
            x["inv"] = x["inv"] - bdot(x["inv"], x["t"])
    for x in st.values():
        x["akv"] = bdot(x["a_k"], x["v"])
        x["y_const"] = bdot(x["b_k"], x["v"])
        x["kv"] = bdot_tn(x["kh"], x["v"])
    for x in st.values():
        x["p_state"] = bdot(x["inv"], x["kap"])
        x["p_const"] = bdot(x["inv"], x["akv"])
    state = {key: st_ref[key[0], key[1]] for key in heads}
    for cc in range(n_chunks):
        cur = [(key, st[(cc,) + key]) for key in heads]
        for key, x in cur:
            x["p"] = bdot(x["p_state"], state[key]) + x["p_const"]
            x["y0"] = bdot(x["rh"], state[key]) + x["y_const"]
        for key, x in cur:
            x["y"] = x["y0"] - bdot(x["b_b"], x["p"])
            state[key] = (state[key] + x["kv"] - bdot_tn(x["bh"], x["p"])) * x["scale"]
        for d in range(2):
            dirs[d][6][0, rows(cc, d), :] = jnp.concatenate([x["y"] for (dd, h), x in cur if dd == d], axis=-1)
    for d, h in heads:
        st_ref[d, h] = state[d, h]


def _rwkv_post_rows(y, bonus, g, ln_g, ln_b):
    head_mean = _segment_ones(GW, HEAD_DIM, F32) * (1.0 / HEAD_DIM)
    yc = y - _dot_f32(y, head_mean)
    var = _dot_f32(yc * yc, head_mean)
    return (yc * lax.rsqrt(var + RWKV_GN_EPS) * ln_g + ln_b + bonus) * g


def _rwkv_post_kernel(yf_ref, yb_ref, bonus_ref, g_ref, lng_ref, lnb_ref, o_ref):
    o_ref[0] = _rwkv_post_rows(yf_ref[0] + yb_ref[0], bonus_ref[0], g_ref[0], lng_ref[...], lnb_ref[...])


def _block_diag2(m):
    z = jnp.zeros_like(m[0])
    return jnp.concatenate([jnp.concatenate([m[0], z], 1), jnp.concatenate([z, m[1]], 1)], 0)


def _rwkv7_pallas(f, mu, w0, w2, a0, a2, g2, k_k, k_a, r_k, ln_g, ln_b, *, ctx_len, apply_post=True):
    bsz, seq, cols = f.shape
    tr = ROW_BLOCK
    assert ctx_len % tr == 0 and seq % tr == 0
    nb, ctx_blocks = seq // tr, ctx_len // tr
    prev_spec, next_spec = _halo_specs(cols, tr)
    row2 = lambda a: a.reshape(1, -1).astype(F32)
    full = lambda a: pl.BlockSpec(a.shape, lambda b, i: (0,) * a.ndim)
    params = [mu, row2(w0), _block_diag2(w2), row2(a0), _block_diag2(a2), g2, row2(k_k), row2(k_a), row2(r_k)]
    act = jax.ShapeDtypeStruct((bsz, seq, GW), F32)
    act2 = jax.ShapeDtypeStruct((2, bsz, seq, GW), F32)
    blk = pl.BlockSpec((1, tr, GW), lambda b, i: (b, i, 0))
    blk2 = pl.BlockSpec((2, 1, tr, GW), lambda b, i: (0, b, i, 0))
    r, v, kkn, g, bonus, w, bb, kt = pl.pallas_call(
        functools.partial(_rwkv_prep_kernel, ctx_blocks=ctx_blocks),
        name="rwkv_prep",
        grid=(bsz, nb),
        in_specs=[pl.BlockSpec((1, tr, cols), lambda b, i: (b, i, 0)), prev_spec, next_spec(seq // 8)]
                 + [full(p) for p in params],
        out_specs=[blk] * 5 + [blk2] * 3,
        out_shape=[act] * 5 + [act2] * 3,
        compiler_params=pltpu.CompilerParams(dimension_semantics=("arbitrary", "arbitrary"),
                                             vmem_limit_bytes=V7X_VMEM_LIMIT_BYTES),
    )(f, f, f, *params)

    def bwd_block(c):
        return jnp.where(c < ctx_blocks, ctx_blocks - 1 - c, nb - 1 - (c - ctx_blocks))
    assert tr % RWKV_CHUNK == 0 and RWKV_CHUNK == HEAD_DIM
    fwd = pl.BlockSpec((1, tr, GW), lambda b, c: (b, c, 0))
    bwd = pl.BlockSpec((1, tr, GW), lambda b, c: (b, bwd_block(c), 0))
    fwd_d = pl.BlockSpec((1, 1, tr, GW), lambda b, c: (0, b, c, 0))
    bwd_d = pl.BlockSpec((1, 1, tr, GW), lambda b, c: (1, b, bwd_block(c), 0))
    yf, yb = pl.pallas_call(
        _rwkv_chunk_kernel,
        name="rwkv_chunks",
        grid=(bsz, nb),
        in_specs=[fwd, fwd, fwd, fwd_d, fwd_d, fwd_d, bwd, bwd, bwd, bwd_d, bwd_d, bwd_d],
        out_specs=[fwd, bwd],
        out_shape=[act, act],
        scratch_shapes=[pltpu.VMEM((2, GROUP_HEADS, HEAD_DIM, HEAD_DIM), F32)],
        compiler_params=pltpu.CompilerParams(dimension_semantics=("arbitrary", "arbitrary"),
                                             vmem_limit_bytes=V7X_VMEM_LIMIT_BYTES),
    )(r, v, kkn, w, bb, kt, r, v, kkn, w, bb, kt)
    if not apply_post:
        return yf, yb, bonus, g, row2(ln_g), row2(ln_b)

    return pl.pallas_call(
        _rwkv_post_kernel,
        name="rwkv_post",
        grid=(bsz, nb),
        in_specs=[blk, blk, blk, blk, full(row2(ln_g)), full(row2(ln_b))],
        out_specs=blk,
        out_shape=act,
        compiler_params=pltpu.CompilerParams(dimension_semantics=("arbitrary", "arbitrary")),
    )(yf, yb, bonus, g, row2(ln_g), row2(ln_b))


GDN_GATE_LANES = 128
GDN_PADDED_COLS = 4 * GW + GDN_GATE_LANES


def _gdn_prep_kernel(x_ref, xprev_ref, xnext_ref, ab_ref, conv_ref, alog_ref, dtb_ref,
                     q_out, k_out, v_out, gb_out, *, ctx_blocks):
    i = pl.program_id(1)
    keep_prev, keep_next = _segment_edge_flags(i, pl.num_programs(1), ctx_blocks)
    x = x_ref[0]
    xm, xp = _shifted_rows(x, xprev_ref[0, 7:8, :] * keep_prev, xnext_ref[0, 0:1, :] * keep_next)
    y = xm * conv_ref[0:1, :] + x * conv_ref[1:2, :] + xp * conv_ref[2:3, :]
    y = y * jax.nn.sigmoid(y)
    head_sum = _segment_ones(GW, HEAD_DIM, F32)
    q = y[:, 0:GW]
    k = y[:, GW:2 * GW]
    q_out[0] = q * lax.rsqrt(_dot_f32(q * q, head_sum) + 1e-6) * (HEAD_DIM ** -0.5)
    k_out[0] = k * lax.rsqrt(_dot_f32(k * k, head_sum) + 1e-6)
    v_out[0] = y[:, 2 * GW:3 * GW]
    ab = ab_ref[0]
    lane = lax.broadcasted_iota(jnp.int32, ab.shape, 1)
    log_alpha = -jnp.exp(alog_ref[...]) * _softplus(ab + dtb_ref[...])
    gb_out[0] = jnp.where(lane < 2 * GROUP_HEADS, log_alpha, jax.nn.sigmoid(ab))


def _gdn_chunk_kernel(qf_ref, kf_ref, vf_ref, gf_ref, qb_ref, kb_ref, vb_ref, gb_ref, of_ref, ob_ref, st_ref):
    i = pl.program_id(1)
    c = GDN_CHUNK
    n_chunks = qf_ref.shape[1] // c

    @pl.when(i == 0)
    def _():
        st_ref[...] = jnp.zeros_like(st_ref)

    r = lax.broadcasted_iota(jnp.int32, (c, c), 0)
    s = lax.broadcasted_iota(jnp.int32, (c, c), 1)
    eye = r == s
    ones_cc = jnp.ones((c, c), F32)
    incl = (r >= s, r <= s)
    strict = (r > s, r < s)
    levels = []
    b = 1
    while b < c:
        levels.append(jnp.logical_and(r // (2 * b) == s // (2 * b), r // b != s // b))
        b *= 2
    dirs = ((qf_ref, kf_ref, vf_ref, gf_ref, of_ref), (qb_ref, kb_ref, vb_ref, gb_ref, ob_ref))

    bdot = lambda x, y: jnp.dot(x.astype(BF16), y.astype(BF16), preferred_element_type=F32)
    bdot_nt = lambda x, y: lax.dot_general(x.astype(BF16), y.astype(BF16), _NT, preferred_element_type=F32)
    bdot_tn = lambda x, y: lax.dot_general(x.astype(BF16), y.astype(BF16), (((0,), (0,)), ((), ())),
                                           preferred_element_type=F32)

    heads = [(d, h) for d in range(2) for h in range(GROUP_HEADS)]
    row0 = lambda cc, d: (cc if d == 0 else n_chunks - 1 - cc) * c
    gates = {(cc, d): dirs[d][3][0, row0(cc, d):row0(cc, d) + c, :] for cc in range(n_chunks) for d in range(2)}
    cum = {key: _dot_f32(jnp.where(incl[key[1]], 1.0, 0.0), g) for key, g in gates.items()}
    tot = {key: _dot_f32(ones_cc, g) for key, g in gates.items()}
    chains = [(cc, d, h) for cc in range(n_chunks) for d, h in heads]
    st = {}
    for cc, d, h in chains:
        cols = slice(h * HEAD_DIM, (h + 1) * HEAD_DIM)
        lg = d * GROUP_HEADS + h
        q, k, v = (dirs[d][n][0, row0(cc, d):row0(cc, d) + c, cols] for n in range(3))
        gc = cum[cc, d][:, lg:lg + 1]
        st[cc, d, h] = dict(q=q, k=k, v=v, gc=gc, gt=tot[cc, d][:, lg:lg + 1],
                            beta=gates[cc, d][:, 2 * GROUP_HEADS + lg:2 * GROUP_HEADS + lg + 1],
                            gc_row=_dot_f32(ones_cc, jnp.where(eye, gc, 0.0)))
    for (cc, d, h), x in st.items():
        x["decay"] = jnp.exp(jnp.where(incl[d], x["gc"] - x["gc_row"], NEG_BIG))
        x["kb"] = x["k"] * x["beta"]
        x["a"] = jnp.where(strict[d], bdot_nt(x["kb"], x["k"]) * x["decay"], 0.0)
        x["qk"] = jnp.where(incl[d], bdot_nt(x["q"], x["k"]) * x["decay"], 0.0)
        x["inv"] = jnp.where(eye, 1.0, 0.0)
    for lvl in levels:
        for x in st.values():
            x["t"] = bdot(jnp.where(lvl, x["a"], 0.0), x["inv"])
        for x in st.values():
            x["inv"] = x["inv"] - bdot(x["inv"], x["t"])
    for x in st.values():
        x["eg"] = jnp.exp(x["gc"])
        x["sol"] = bdot(x["inv"], jnp.concatenate([x["v"] * x["beta"], x["kb"] * x["eg"]], axis=-1))
        x["qg"] = x["q"] * x["eg"]
        x["kg"] = x["k"] * jnp.exp(x["gt"] - x["gc"])
    state = {(d, h): st_ref[d, h] for d, h in heads}
    for cc in range(n_chunks):
        cur = [(key, st[(cc,) + key]) for key in heads]
        for key, x in cur:
            x["ws"] = bdot(x["sol"][:, HEAD_DIM:], state[key])
            x["qs"] = bdot(x["qg"], state[key])
        for key, x in cur:
            x["v_new"] = x["sol"][:, :HEAD_DIM] - x["ws"]
            x["o"] = x["qs"] + bdot(x["qk"], x["v_new"])
            x["upd"] = bdot_tn(x["kg"], x["v_new"])
        for key, x in cur:
            state[key] = state[key] * jnp.exp(x["gt"][0:1, :]) + x["upd"]
        for d in range(2):
            dirs[d][4][0, row0(cc, d):row0(cc, d) + c, :] = jnp.concatenate(
                [x["o"] for (dd, h), x in cur if dd == d], axis=-1)
    for d, h in heads:
        st_ref[d, h] = state[d, h]


def _gdn_post_rows(o, gate, g):
    head_mean = _segment_ones(GW, HEAD_DIM, F32) * (1.0 / HEAD_DIM)
    return o * lax.rsqrt(_dot_f32(o * o, head_mean) + 1e-6) * g * (gate * jax.nn.sigmoid(gate))


def _gdn_post_kernel(of_ref, ob_ref, gate_ref, g_ref, o_ref):
    o_ref[0] = _gdn_post_rows(of_ref[0] + ob_ref[0], gate_ref[0], g_ref[...])


def _gated_deltanet_pallas(f, conv_w, a_log, dt_bias, norm_g, *, ctx_len, apply_post=True):
    bsz, seq, width = f.shape
    tr = ROW_BLOCK
    assert ctx_len % tr == 0 and seq % tr == 0 and tr % GDN_CHUNK == 0
    if width == GDN_COLS:
        f = jnp.pad(f, ((0, 0), (0, 0), (0, GDN_PADDED_COLS - GDN_COLS)))
    assert f.shape[2] == GDN_PADDED_COLS
    nb, ctx_blocks = seq // tr, ctx_len // tr
    lane_pad = lambda a: jnp.pad(a.reshape(1, -1).astype(F32), ((0, 0), (0, GDN_GATE_LANES - a.size)))
    prev_spec, next_spec = _halo_specs(3 * GW, tr)
    full = lambda a: pl.BlockSpec(a.shape, lambda b, i: (0,) * a.ndim)
    act = jax.ShapeDtypeStruct((bsz, seq, GW), F32)
    gact = jax.ShapeDtypeStruct((bsz, seq, GDN_GATE_LANES), F32)
    blk = pl.BlockSpec((1, tr, GW), lambda b, i: (b, i, 0))
    gblk = pl.BlockSpec((1, tr, GDN_GATE_LANES), lambda b, i: (b, i, 0))
    gate_view = pl.BlockSpec((1, tr, GW), lambda b, i: (b, i, 3))
    ab_view = pl.BlockSpec((1, tr, GDN_GATE_LANES), lambda b, i: (b, i, 4 * GW // GDN_GATE_LANES))
    params = [conv_w.astype(F32), lane_pad(a_log), lane_pad(dt_bias)]
    q, k, v, gb = pl.pallas_call(
        functools.partial(_gdn_prep_kernel, ctx_blocks=ctx_blocks),
        name="gdn_prep",
        grid=(bsz, nb),
        in_specs=[pl.BlockSpec((1, tr, 3 * GW), lambda b, i: (b, i, 0)), prev_spec, next_spec(seq // 8), ab_view]
                 + [full(p) for p in params],
        out_specs=[blk, blk, blk, gblk],
        out_shape=[act, act, act, gact],
        compiler_params=pltpu.CompilerParams(dimension_semantics=("arbitrary", "arbitrary"),
                                             vmem_limit_bytes=V7X_VMEM_LIMIT_BYTES),
    )(f, f, f, f, *params)

    def bwd_block(i):
        return jnp.where(i < ctx_blocks, ctx_blocks - 1 - i, nb - 1 - (i - ctx_blocks))
    bblk = pl.BlockSpec((1, tr, GW), lambda b, i: (b, bwd_block(i), 0))
    bgblk = pl.BlockSpec((1, tr, GDN_GATE_LANES), lambda b, i: (b, bwd_block(i), 0))
    of, ob = pl.pallas_call(
        _gdn_chunk_kernel,
        name="gdn_chunks",
        grid=(bsz, nb),
        in_specs=[blk, blk, blk, gblk, bblk, bblk, bblk, bgblk],
        out_specs=[blk, bblk],
        out_shape=[act, act],
        scratch_shapes=[pltpu.VMEM((2, GROUP_HEADS, HEAD_DIM, HEAD_DIM), F32)],
        compiler_params=pltpu.CompilerParams(dimension_semantics=("arbitrary", "arbitrary"),
                                             vmem_limit_bytes=V7X_VMEM_LIMIT_BYTES),
    )(q, k, v, gb, q, k, v, gb)

    g_row = jnp.tile(norm_g.reshape(1, HEAD_DIM).astype(F32), (1, GROUP_HEADS))
    if not apply_post:
        return of, ob, f, g_row
    return pl.pallas_call(
        _gdn_post_kernel,
        name="gdn_post",
        grid=(bsz, nb),
        in_specs=[blk, blk, gate_view, full(g_row)],
        out_specs=blk,
        out_shape=act,
        compiler_params=pltpu.CompilerParams(dimension_semantics=("arbitrary", "arbitrary")),
    )(of, ob, f, g_row)


ROPE_PAIR = DIFF_HALF // 4


def _rope_tables(seq, ctx_len, q_scale):
    n = jnp.arange(seq - ctx_len, dtype=jnp.int32)
    row, col = n // GRID_W, n % GRID_W
    i = jnp.arange(HEAD_DIM)
    grp = (i % DIFF_HALF) // (2 * ROPE_PAIR)
    inv = ROPE_BASE ** (-(i % ROPE_PAIR).astype(F32) / ROPE_PAIR)
    pos = jnp.where(grp[None, :] == 0, row[:, None], col[:, None]).astype(F32)
    ang = pos * inv[None, :]
    sign = jnp.where((i % (2 * ROPE_PAIR)) < ROPE_PAIR, -1.0, 1.0)
    cos = jnp.concatenate([jnp.ones((ctx_len, HEAD_DIM), F32), jnp.cos(ang)], 0)
    sin = jnp.concatenate([jnp.zeros((ctx_len, HEAD_DIM), F32), jnp.sin(ang) * sign], 0)
    cos = jnp.tile(cos, (1, GROUP_HEADS))
    sin = jnp.tile(sin, (1, GROUP_HEADS))
    return jnp.concatenate([cos * q_scale, cos], 1), jnp.concatenate([sin * q_scale, sin], 1)


def _qkv_prep_kernel(p_ref, cos_ref, sin_ref, q_out, k_out, v_out):
    qk = p_ref[0, :, 0:2 * GW]
    width = 2 * GW
    lane = lax.broadcasted_iota(jnp.int32, qk.shape, 1)
    partner = jnp.where(lane % (2 * ROPE_PAIR) < ROPE_PAIR,
                        pltpu.roll(qk, width - ROPE_PAIR, axis=1), pltpu.roll(qk, ROPE_PAIR, axis=1))
    rot = qk * cos_ref[...] + partner * sin_ref[...]
    q_out[0] = rot[:, 0:GW].astype(BF16)
    v = p_ref[0, :, 2 * GW:3 * GW]
    for h in range(GROUP_HEADS):
        k_out[0, h] = rot[:, GW + h * HEAD_DIM:GW + (h + 1) * HEAD_DIM].astype(BF16)
        v_out[0, h] = v[:, h * HEAD_DIM:(h + 1) * HEAD_DIM].astype(BF16)


def _qkv_prep(p, cos, sin):
    bsz, seq, _ = p.shape
    tr = ROW_BLOCK
    head_major = jax.ShapeDtypeStruct((bsz, GROUP_HEADS, seq, HEAD_DIM), BF16)
    hm_spec = pl.BlockSpec((1, GROUP_HEADS, tr, HEAD_DIM), lambda b, i: (b, 0, i, 0))
    return pl.pallas_call(
        _qkv_prep_kernel,
        name="qkv_prep",
        grid=(bsz, seq // tr),
        in_specs=[pl.BlockSpec((1, tr, 3 * GW), lambda b, i: (b, i, 0)),
                  pl.BlockSpec((tr, 2 * GW), lambda b, i: (i, 0)),
                  pl.BlockSpec((tr, 2 * GW), lambda b, i: (i, 0))],
        out_specs=[pl.BlockSpec((1, tr, GW), lambda b, i: (b, i, 0)), hm_spec, hm_spec],
        out_shape=[jax.ShapeDtypeStruct((bsz, seq, GW), BF16), head_major, head_major],
        compiler_params=pltpu.CompilerParams(dimension_semantics=("arbitrary", "arbitrary")),
    )(p, cos, sin)


_NT = (((1,), (1,)), ((), ()))


def _softmax_pv(s, v):
    m = jnp.max(s, axis=-1, keepdims=True)
    e = jnp.exp(s - m)
    return jnp.dot(e.astype(BF16), v, preferred_element_type=F32) / jnp.sum(e, axis=-1, keepdims=True)


def _diff_attn_kernel(q_ref, k_ref, v_ref, lam_ref, g_ref, o_ref, *, ctx_blocks, ctx_len):
    i = pl.program_id(1)
    lv = lam_ref[...]
    lam_init = lv[4:5, 0:1]
    lam = (jnp.exp(jnp.sum(lv[0:1] * lv[1:2], axis=-1, keepdims=True))
           - jnp.exp(jnp.sum(lv[2:3] * lv[3:4], axis=-1, keepdims=True)) + lam_init)
    lane = lax.broadcasted_iota(jnp.int32, (q_ref.shape[1], HEAD_DIM), 1)

    def attend(n_keys):
        outs = []
        for h in range(GROUP_HEADS):
            qh = q_ref[0, :, h * HEAD_DIM:(h + 1) * HEAD_DIM]
            kh = k_ref[0, h, 0:n_keys, :]
            vh = v_ref[0, h, 0:n_keys, :]
            zero = jnp.zeros_like(qh)
            s1 = lax.dot_general(jnp.where(lane < DIFF_HALF, qh, zero), kh, _NT, preferred_element_type=F32)
            s2 = lax.dot_general(jnp.where(lane >= DIFF_HALF, qh, zero), kh, _NT, preferred_element_type=F32)
            o = _softmax_pv(s1, vh) - lam * _softmax_pv(s2, vh)
            o = o * lax.rsqrt(jnp.mean(o * o, axis=-1, keepdims=True) + 1e-6) * g_ref[...] * (1.0 - lam_init)
            outs.append(o)
        o_ref[0] = jnp.concatenate(outs, axis=-1)

    @pl.when(i < ctx_blocks)
    def _():
        attend(ctx_len)

    @pl.when(i >= ctx_blocks)
    def _():
        attend(k_ref.shape[2])


def _diff_attention_pallas(q, k, v, lam_vecs, norm_g, *, ctx_len, lam_init):
    bsz, seq, _ = q.shape
    tq = ROW_BLOCK
    kv_spec = pl.BlockSpec((1, GROUP_HEADS, seq, HEAD_DIM), lambda b, i: (b, 0, 0, 0))
    lam_rows = jnp.concatenate([lam_vecs.astype(F32), jnp.full((1, lam_vecs.shape[1]), lam_init, F32)], 0)
    return pl.pallas_call(
        functools.partial(_diff_attn_kernel, ctx_blocks=ctx_len // tq, ctx_len=ctx_len),
        name="diff_attn",
        grid=(bsz, seq // tq),
        in_specs=[pl.BlockSpec((1, tq, GW), lambda b, i: (b, i, 0)), kv_spec, kv_spec,
                  pl.BlockSpec(lam_rows.shape, lambda b, i: (0, 0)),
                  pl.BlockSpec((1, HEAD_DIM), lambda b, i: (0, 0))],
        out_specs=pl.BlockSpec((1, tq, GW), lambda b, i: (b, i, 0)),
        out_shape=jax.ShapeDtypeStruct((bsz, seq, GW), F32),
        compiler_params=pltpu.CompilerParams(dimension_semantics=("arbitrary", "arbitrary"),
                                             vmem_limit_bytes=V7X_VMEM_LIMIT_BYTES),
    )(q, k, v, lam_rows, norm_g.reshape(1, HEAD_DIM).astype(F32))


NAT_TILE_ROWS = ROW_BLOCK // GRID_W
NAT_SLAB_ROWS = NAT_TILE_ROWS + WIN_H - 1


def _nat_slab_start(tile, n_rows):
    return np.clip(tile * NAT_TILE_ROWS - WIN_H // 2, 0, n_rows - NAT_SLAB_ROWS)


def _nat_bias_tables(rpb, n_rows):
    n_tiles = n_rows // NAT_TILE_ROWS
    nq, nk, w = NAT_TILE_ROWS, NAT_SLAB_ROWS, GRID_W
    cq, ck = np.arange(w)[:, None], np.arange(w)[None, :]
    d_col = np.clip(ck - cq, -(WIN_W - 1), WIN_W - 1) + WIN_W - 1
    col_1h = (d_col.reshape(-1)[:, None] == np.arange(2 * WIN_W - 1)[None, :]).astype(np.float32)
    c0 = np.clip(cq - WIN_W // 2, 0, w - WIN_W)
    col_ok = (ck >= c0) & (ck < c0 + WIN_W)
    tabs = []
    for tile in (0, 1, n_tiles - 1):
        r = tile * nq + np.arange(nq)[:, None]
        kr = _nat_slab_start(tile, n_rows) + np.arange(nk)[None, :]
        rs = np.clip(r - WIN_H // 2, 0, n_rows - WIN_H)
        row_ok = (kr >= rs) & (kr < rs + WIN_H)
        d_row = np.clip(kr - r + WIN_H - 1, 0, 2 * WIN_H - 2)
        row_1h = (d_row.reshape(-1)[:, None] == np.arange(2 * WIN_H - 1)[None, :]).astype(np.float32)
        t = jnp.einsum('pa,hab,cb->hpc', row_1h, rpb.astype(F32), col_1h, precision=HIGHEST)
        t = t.reshape(GROUP_HEADS, nq, nk, w, w).transpose(0, 1, 3, 2, 4)
        ok = row_ok[:, None, :, None] & col_ok[None, :, None, :]
        tabs.append(jnp.where(ok[None], t, NEG_BIG).reshape(GROUP_HEADS, nq * w, nk * w))
    return jnp.stack(tabs)


def _nat_attn_kernel(q_ref, k_ref, v_ref, bias_ref, o_ref, *, ctx_blocks, ctx_len, n_rows):
    i = pl.program_id(1)
    n_slab = NAT_SLAB_ROWS * GRID_W

    def heads(fn):
        o_ref[0] = jnp.concatenate(
            [fn(h, q_ref[0, :, h * HEAD_DIM:(h + 1) * HEAD_DIM]) for h in range(GROUP_HEADS)], axis=-1)

    @pl.when(i < ctx_blocks)
    def _():
        def ctx_only(h, qh):
            s = lax.dot_general(qh, k_ref[0, h, 0:ctx_len, :], _NT, preferred_element_type=F32)
            return _softmax_pv(s, v_ref[0, h, 0:ctx_len, :])
        heads(ctx_only)

    @pl.when(i >= ctx_blocks)
    def _():
        tile = i - ctx_blocks
        start = jnp.clip(tile * NAT_TILE_ROWS - WIN_H // 2, 0, n_rows - NAT_SLAB_ROWS)
        off = pl.multiple_of(ctx_len + start * GRID_W, GRID_W)

        def windowed(h, qh):
            s_w = lax.dot_general(qh, k_ref[0, h, pl.ds(off, n_slab), :], _NT,
                                  preferred_element_type=F32) + bias_ref[0, h]
            s_c = lax.dot_general(qh, k_ref[0, h, 0:ctx_len, :], _NT, preferred_element_type=F32)
            m = jnp.maximum(jnp.max(s_w, axis=-1, keepdims=True), jnp.max(s_c, axis=-1, keepdims=True))
            e_w = jnp.exp(s_w - m)
            e_c = jnp.exp(s_c - m)
            den = jnp.sum(e_w, axis=-1, keepdims=True) + jnp.sum(e_c, axis=-1, keepdims=True)
            num = (jnp.dot(e_w.astype(BF16), v_ref[0, h, pl.ds(off, n_slab), :], preferred_element_type=F32)
                   + jnp.dot(e_c.astype(BF16), v_ref[0, h, 0:ctx_len, :], preferred_element_type=F32))
            return num / den
        heads(windowed)


def _nat_attention_pallas(q, k, v, rpb, *, ctx_len):
    bsz, seq, _ = q.shape
    tq = ROW_BLOCK
    ctx_blocks = ctx_len // tq
    n_rows = (seq - ctx_len) // GRID_W
    n_tiles = n_rows // NAT_TILE_ROWS
    assert n_rows >= NAT_SLAB_ROWS and n_tiles >= 3
    bias = _nat_bias_tables(rpb, n_rows)

    def variant(i):
        tile = i - ctx_blocks
        return jnp.where(tile <= 0, 0, jnp.where(tile >= n_tiles - 1, 2, 1))
    kv_spec = pl.BlockSpec((1, GROUP_HEADS, seq, HEAD_DIM), lambda b, i: (b, 0, 0, 0))
    return pl.pallas_call(
        functools.partial(_nat_attn_kernel, ctx_blocks=ctx_blocks, ctx_len=ctx_len, n_rows=n_rows),
        name="nat_attn",
        grid=(bsz, seq // tq),
        in_specs=[pl.BlockSpec((1, tq, GW), lambda b, i: (b, i, 0)), kv_spec, kv_spec,
                  pl.BlockSpec((1,) + bias.shape[1:], lambda b, i: (variant(i), 0, 0, 0))],
        out_specs=pl.BlockSpec((1, tq, GW), lambda b, i: (b, i, 0)),
        out_shape=jax.ShapeDtypeStruct((bsz, seq, GW), F32),
        compiler_params=pltpu.CompilerParams(dimension_semantics=("arbitrary", "arbitrary"),
                                             vmem_limit_bytes=V7X_VMEM_LIMIT_BYTES),
    )(q, k, v, bias)


N_MOD = 6
MATMUL_ROWS = 512


def _ada_kernel(c_ref, w_ref, b_ref, o_ref):
    c = c_ref[...]
    o_ref[...] = _dot_f32(c * jax.nn.sigmoid(c), w_ref[...]) + b_ref[...]


def _ada_modulation(c, c_ctx, w_ada, b_ada):
    bsz, d = c.shape
    rows = 8 * ((bsz + 1 + 7) // 8)
    cc = jnp.zeros((rows, d), F32).at[:bsz].set(c).at[bsz].set(c_ctx)
    tn = d
    m = pl.pallas_call(
        _ada_kernel,
        name="ada_modulation",
        grid=(w_ada.shape[1] // tn,),
        in_specs=[pl.BlockSpec((rows, d), lambda j: (0, 0)),
                  pl.BlockSpec((d, tn), lambda j: (0, j)),
                  pl.BlockSpec((1, tn), lambda j: (0, j))],
        out_specs=pl.BlockSpec((rows, tn), lambda j: (0, j)),
        out_shape=jax.ShapeDtypeStruct((rows, w_ada.shape[1]), F32),
        compiler_params=pltpu.CompilerParams(dimension_semantics=("arbitrary",)),
    )(cc, w_ada, b_ada.reshape(1, -1))
    lat = m[:bsz].reshape(bsz, 1, N_MOD, d)
    ctx = jnp.broadcast_to(m[bsz].reshape(1, 1, N_MOD, d), (bsz, 1, N_MOD, d))
    return jnp.concatenate([ctx, lat], axis=1)


def _ln(x):
    mu = jnp.mean(x, axis=-1, keepdims=True)
    xc = x - mu
    return xc * lax.rsqrt(jnp.mean(xc * xc, axis=-1, keepdims=True) + LN_EPS)


def _modulate_kernel(h_ref, mod_ref, o_ref, *, shift_row):
    shift = mod_ref[0, 0, shift_row:shift_row + 1, :]
    scale = mod_ref[0, 0, shift_row + 1:shift_row + 2, :]
    o_ref[0] = (_ln(h_ref[0]) * (1.0 + scale) + shift).astype(o_ref.dtype)


def _mod_spec(d, ctx_blocks):
    return pl.BlockSpec((1, 1, N_MOD, d), lambda b, i: (b, jnp.where(i < ctx_blocks, 0, 1), 0, 0))


def _modulate_pallas(hs, mod, shift_row, *, ctx_len):
    bsz, seq, d = hs.shape
    tr = ROW_BLOCK
    blk = pl.BlockSpec((1, tr, d), lambda b, i: (b, i, 0))
    return pl.pallas_call(
        functools.partial(_modulate_kernel, shift_row=shift_row),
        name="modulate",
        grid=(bsz, seq // tr),
        in_specs=[blk, _mod_spec(d, ctx_len // tr)],
        out_specs=blk,
        out_shape=jax.ShapeDtypeStruct((bsz, seq, d), BF16),
        compiler_params=pltpu.CompilerParams(dimension_semantics=("arbitrary", "arbitrary")),
    )(hs, mod)


def _matmul_kernel(x_ref, w_ref, o_ref):
    o_ref[...] = jnp.dot(x_ref[...], w_ref[...], preferred_element_type=F32)


def _matmul_pallas(x, w):
    m, k = x.shape
    n = w.shape[1]
    tm = MATMUL_ROWS
    return pl.pallas_call(
        _matmul_kernel,
        name="in_proj",
        grid=(m // tm,),
        in_specs=[pl.BlockSpec((tm, k), lambda i: (i, 0)), pl.BlockSpec((k, n), lambda i: (0, 0))],
        out_specs=pl.BlockSpec((tm, n), lambda i: (i, 0)),
        out_shape=jax.ShapeDtypeStruct((m, n), F32),
        compiler_params=pltpu.CompilerParams(dimension_semantics=("arbitrary",),
                                             vmem_limit_bytes=V7X_VMEM_LIMIT_BYTES),
    )(x, w)


def _post_norm_rows(h, gate, y, g, b):
    return _ln(DN_ALPHA * h + gate * y) * g + b


def _out_proj_kernel(ya_ref, rf_ref, rb_ref, bonus_ref, rg_ref, lng_ref, lnb_ref, of_ref, ob_ref, gate_ref, gn_ref,
                     yd_ref, w_ref, h_ref, mod_ref, g_ref, b_ref, o_ref):
    yb = _rwkv_post_rows(rf_ref[0] + rb_ref[0], bonus_ref[0], rg_ref[0], lng_ref[...], lnb_ref[...])
    yc = _gdn_post_rows(of_ref[0] + ob_ref[0], gate_ref[0], gn_ref[...])
    mix = None
    for n, y in enumerate((ya_ref[0], yb, yc, yd_ref[0])):
        part = jnp.dot(y.astype(BF16), w_ref[n * GW:(n + 1) * GW, :], preferred_element_type=F32)
        mix = part if mix is None else mix + part
    o_ref[0] = _post_norm_rows(h_ref[0], mod_ref[0, 0, 2:3, :], mix, g_ref[...], b_ref[...])


def _out_proj_post_norm(ya, rwkv_parts, gdn_parts, yd, w_out, hs, mod, g, b, *, ctx_len):
    bsz, seq, d = hs.shape
    tr = ROW_BLOCK
    yblk = pl.BlockSpec((1, tr, GW), lambda bb, i: (bb, i, 0))
    blk = pl.BlockSpec((1, tr, d), lambda bb, i: (bb, i, 0))
    row = pl.BlockSpec((1, d), lambda bb, i: (0, 0))
    grow = pl.BlockSpec((1, GW), lambda bb, i: (0, 0))
    gate_view = pl.BlockSpec((1, tr, GW), lambda bb, i: (bb, i, 3))
    return pl.pallas_call(
        _out_proj_kernel,
        name="out_proj_post_norm",
        grid=(bsz, seq // tr),
        in_specs=[yblk] + [yblk] * 4 + [grow, grow] + [yblk, yblk, gate_view, grow] + [yblk]
                 + [pl.BlockSpec(w_out.shape, lambda bb, i: (0, 0)), blk, _mod_spec(d, ctx_len // tr), row, row],
        out_specs=blk,
        out_shape=jax.ShapeDtypeStruct((bsz, seq, d), F32),
        compiler_params=pltpu.CompilerParams(dimension_semantics=("arbitrary", "arbitrary")),
    )(ya, *rwkv_parts, *gdn_parts, yd, w_out, hs, mod, g.reshape(1, d), b.reshape(1, d))


def _ffn_post_norm_kernel(h_ref, y_ref, mod_ref, g_ref, b_ref, o_ref):
    o_ref[0] = _post_norm_rows(h_ref[0], mod_ref[0, 0, 5:6, :], y_ref[0], g_ref[...], b_ref[...])


def _ffn_post_norm(hs, y, mod, g, b, *, ctx_len):
    bsz, seq, d = hs.shape
    tr = ROW_BLOCK
    blk = pl.BlockSpec((1, tr, d), lambda bb, i: (bb, i, 0))
    row = pl.BlockSpec((1, d), lambda bb, i: (0, 0))
    return pl.pallas_call(
        _ffn_post_norm_kernel,
        name="ffn_post_norm",
        grid=(bsz, seq // tr),
        in_specs=[blk, blk, _mod_spec(d, ctx_len // tr), row, row],
        out_specs=blk,
        out_shape=jax.ShapeDtypeStruct((bsz, seq, d), F32),
        compiler_params=pltpu.CompilerParams(dimension_semantics=("arbitrary", "arbitrary")),
    )(hs, y, mod, g.reshape(1, d), b.reshape(1, d))


def kernel(x, c, ctx, c_ctx, w_ada, b_ada, w_in, w_out, ln_mix_g, ln_mix_b, ln_ffn_g, ln_ffn_b, diff_lam, diff_norm_g, rwkv_mu, rwkv_w0, rwkv_w2, rwkv_a0, rwkv_a2, rwkv_g2, rwkv_kk, rwkv_ka, rwkv_rk, rwkv_ln_g, rwkv_ln_b, gdn_conv, gdn_a_log, gdn_dt_bias, gdn_norm_g, nat_rpb, peer_wq, peer_keys, peer_u, peer_v):
    dtype = x.dtype
    bsz, ctx_len = ctx.shape[0], ctx.shape[1]
    hs = jnp.concatenate([ctx, x], axis=1)
    seq = hs.shape[1]
    col_sizes = [ATTN_COLS, RWKV_COLS, GDN_COLS, ATTN_COLS]
    cos_a, sin_a = _rope_tables(seq, ctx_len, DIFF_HALF ** -0.5)
    cos_d = jnp.concatenate([jnp.full((seq, GW), HEAD_DIM ** -0.5, F32), jnp.ones((seq, GW), F32)], 1)
    sin_d = jnp.zeros_like(cos_d)
    col_offs = np.cumsum([0] + col_sizes)
    d_model = hs.shape[2]
    for l in range(DEPTH):
        lam_init = 0.8 - 0.6 * math.exp(-0.3 * l)
        mod = _ada_modulation(c, c_ctx, w_ada[l], b_ada[l])
        u = _modulate_pallas(hs, mod, 0, ctx_len=ctx_len).reshape(bsz * seq, d_model)
        w_in_b = w_in[l].astype(BF16)
        w_groups = [w_in_b[:, col_offs[n]:col_offs[n + 1]] for n in range(4)]
        w_groups[2] = jnp.pad(w_groups[2], ((0, 0), (0, GDN_PADDED_COLS - GDN_COLS)))
        pa, pb, pc, pd = [_matmul_pallas(u, w).reshape(bsz, seq, w.shape[1]) for w in w_groups]
        qa, ka, va = _qkv_prep(pa, cos_a, sin_a)
        ya = _diff_attention_pallas(qa, ka, va, diff_lam[l], diff_norm_g[l], ctx_len=ctx_len, lam_init=lam_init)
        yb_parts = _rwkv7_pallas(pb, rwkv_mu[l], rwkv_w0[l], rwkv_w2[l], rwkv_a0[l], rwkv_a2[l], rwkv_g2[l],
                                 rwkv_kk[l], rwkv_ka[l], rwkv_rk[l], rwkv_ln_g[l], rwkv_ln_b[l], ctx_len=ctx_len,
                                 apply_post=False)
        yc_parts = _gated_deltanet_pallas(pc, gdn_conv[l], gdn_a_log[l], gdn_dt_bias[l], gdn_norm_g[l],
                                          ctx_len=ctx_len, apply_post=False)
        qd, kd, vd = _qkv_prep(pd, cos_d, sin_d)
        yd = _nat_attention_pallas(qd, kd, vd, nat_rpb[l], ctx_len=ctx_len)
        hs = _out_proj_post_norm(ya, yb_parts, yc_parts, yd, w_out[l].astype(BF16), hs, mod, ln_mix_g[l],
                                 ln_mix_b[l], ctx_len=ctx_len)
        wq_b = peer_wq[l].astype(BF16)
        keys_b = peer_keys[l].reshape(2 * PEER_HEADS, PEER_KEYS, PEER_HALF).astype(BF16)
        u_b = peer_u[l].astype(BF16)
        vt_b = peer_v[l].astype(BF16).T
        ffn_ctx = ctx_len
        if l == DEPTH - 1:
            hs, ffn_ctx = hs[:, ctx_len:], 0
        ffn = _peer(_modulate_pallas(hs, mod, 3, ctx_len=ffn_ctx), wq_b, keys_b, u_b, vt_b)
        hs = _ffn_post_norm(hs, ffn, mod, ln_ffn_g[l], ln_ffn_b[l], ctx_len=ffn_ctx)
    return hs.astype(dtype)
```

```python
import functools
import math

import jax
import jax.numpy as jnp
import numpy as np
from jax import lax
from jax.experimental import pallas as pl
from jax.experimental.pallas import tpu as pltpu

D_MODEL = 1024
DEPTH = 2
GRID_W = 64
HEAD_DIM = 64
N_GROUPS = 4
GROUP_HEADS = D_MODEL // (N_GROUPS * HEAD_DIM)
GW = GROUP_HEADS * HEAD_DIM
D_MIX = N_GROUPS * GW
DIFF_HALF = HEAD_DIM // 2
Q_BLOCK = 128
ROPE_BASE = 10000.0
DECAY_LORA = 64
ICLR_LORA = 64
GATE_LORA = 128
RWKV_GN_EPS = 64e-5
RWKV_COLS = 3 * GW + 2 * DECAY_LORA + 2 * ICLR_LORA + GATE_LORA
GDN_CONV = 3
GDN_CHUNK = 64
GDN_COLS = 4 * GW + 4 * GROUP_HEADS
WIN_H = 8
WIN_W = 16
ATTN_COLS = 3 * GW
IN_COLS = ATTN_COLS + RWKV_COLS + GDN_COLS + ATTN_COLS
PEER_HEADS = 8
PEER_KEYS = 128
PEER_EXPERTS = PEER_KEYS * PEER_KEYS
PEER_QDIM = 256
PEER_HALF = PEER_QDIM // 2
PEER_TOPK = 16
DN_ALPHA = (2 * DEPTH) ** 0.25
LN_EPS = 1e-5

F32 = jnp.float32
BF16 = jnp.bfloat16

V7X_VMEM_LIMIT_BYTES = 56 * 1024 * 1024
NEG_BIG = -3.0e38


PEER_STAT_ROWS = 4 * PEER_HEADS
LOG2E = 1.4426950408889634
PEER_POW_HEADS = 4
assert PEER_TOPK == 16


def _topk_rows(x, k):
    rows = []
    cur = x
    for i in range(k):
        m = jnp.max(cur, axis=0, keepdims=True)
        rows.append(m)
        if i + 1 < k:
            cur = jnp.where(cur == m, NEG_BIG, cur)
    return rows


def _peer_score_kernel(x_ref, wq_ref, keys_ref, s_ref, st_ref):
    q = jnp.dot(x_ref[...], wq_ref[...], preferred_element_type=F32).astype(BF16)
    stats = []
    for h in range(PEER_HEADS):
        tops = []
        for p in range(2):
            hp = 2 * h + p
            s_t = lax.dot_general(keys_ref[hp], q[:, hp * PEER_HALF:(hp + 1) * PEER_HALF],
                                  (((1,), (1,)), ((), ())), preferred_element_type=F32)
            s_ref[hp] = s_t
            tops.append(_topk_rows(s_t, PEER_TOPK + 1))
        a, b = tops
        pad = [jnp.full_like(a[0], NEG_BIG)] * 7
        b_head = jnp.concatenate(b[:8], axis=0)
        cand = jnp.concatenate([a[0] + jnp.concatenate(b + pad, axis=0)]
                               + [a[i] + b_head for i in range(1, 8)]
                               + [jnp.concatenate(a[8:] + pad, axis=0) + b[0]], axis=0)
        best_cand = _topk_rows(cand, PEER_TOPK + 1)
        kth, runner_up = best_cand[PEER_TOPK - 1], best_cand[PEER_TOPK]
        best = a[0] + b[0]
        z = jnp.sum(jnp.where(cand >= kth, jnp.exp(cand - best), 0.0), axis=0, keepdims=True)
        stats += [0.5 * (kth + runner_up), tops[0][0], tops[1][0], 1.0 / z]
    st_ref[...] = jnp.concatenate(stats, axis=0)


def _peer_expert_kernel(x_ref, s_ref, st_ref, u_ref, vt_ref, o_ref, e_ref, thr_ref, lin_ref, acc_ref, w_ref,
                        *, rows_per_step, n_chunks):
    j = pl.program_id(1)
    cur = lax.rem(j, 2)

    @pl.when(j == 0)
    def _():
        acc_ref[...] = jnp.zeros_like(acc_ref)
        w_ref[1] = jnp.zeros(w_ref.shape[1:], w_ref.dtype)
        for h in range(PEER_HEADS):
            a0 = st_ref[4 * h + 1:4 * h + 2, :]
            b0 = st_ref[4 * h + 2:4 * h + 3, :]
            rz = st_ref[4 * h + 3:4 * h + 4, :]
            l1 = (s_ref[2 * h] - a0 + jnp.log(rz)) * LOG2E
            l2 = (s_ref[2 * h + 1] - b0) * LOG2E
            e_ref[2 * h] = l1 if h < PEER_POW_HEADS else jnp.exp2(l1)
            e_ref[2 * h + 1] = l2
            if h >= PEER_POW_HEADS:
                lin_ref[h - PEER_POW_HEADS] = jnp.exp2(l2)
            thr_ref[h] = (st_ref[4 * h:4 * h + 1, :] - s_ref[2 * h] - b0) * LOG2E

    @pl.when(j < n_chunks)
    def _():
        slab = 2 * PEER_KEYS
        n_slabs = rows_per_step * PEER_KEYS // slab
        act_slab = lambda n: lax.dot_general(u_ref[n * slab:(n + 1) * slab, :], x_ref[...], (((1,), (1,)), ((), ())),
                                             preferred_element_type=F32)
        acts = [act_slab(n) for n in range(n_slabs)]
        acc_ref[...] += jnp.dot(vt_ref[...], w_ref[1 - cur], preferred_element_type=F32)
        assert rows_per_step % 8 == 0
        for il in range(rows_per_step):
            base = pl.multiple_of(j * rows_per_step + (il // 8) * 8, 8)
            r = il % 8
            o = il * PEER_KEYS
            act = acts[o // slab][o % slab:o % slab + PEER_KEYS]
            act = 0.5 * act * (1.0 + lax.erf(act * (2.0 ** -0.5)))
            gate = None
            for h in range(PEER_HEADS):
                l2 = e_ref[2 * h + 1]
                thr_row = thr_ref[h, pl.ds(base, 8), :][r:r + 1, :]
                half1 = e_ref[2 * h, pl.ds(base, 8), :][r:r + 1, :]
                pair = jnp.exp2(l2 + half1) if h < PEER_POW_HEADS else lin_ref[h - PEER_POW_HEADS] * half1
                term = jnp.where(l2 >= thr_row, pair, 0.0)
                gate = term if gate is None else gate + term
            w_ref[cur, o:o + PEER_KEYS, :] = (act * gate).astype(BF16)

    @pl.when(j == n_chunks)
    def _():
        o_ref[...] = (acc_ref[...] + jnp.dot(vt_ref[...], w_ref[1 - cur], preferred_element_type=F32)).T


def _peer_pallas(h, w_q, keys, u_tab, vt_tab, *, tb, rows_per_step):
    n_tok, d = h.shape
    nhp = 2 * PEER_HEADS
    s, st = pl.pallas_call(
        _peer_score_kernel,
        name="peer_scores",
        grid=(n_tok // tb,),
        in_specs=[pl.BlockSpec((tb, d), lambda i: (i, 0)),
                  pl.BlockSpec(w_q.shape, lambda i: (0, 0)),
                  pl.BlockSpec(keys.shape, lambda i: (0, 0, 0))],
        out_specs=[pl.BlockSpec((nhp, PEER_KEYS, tb), lambda i: (0, 0, i)),
                   pl.BlockSpec((PEER_STAT_ROWS, tb), lambda i: (0, i))],
        out_shape=[jax.ShapeDtypeStruct((nhp, PEER_KEYS, n_tok), F32),
                   jax.ShapeDtypeStruct((PEER_STAT_ROWS, n_tok), F32)],
        compiler_params=pltpu.CompilerParams(dimension_semantics=("arbitrary",),
                                             vmem_limit_bytes=V7X_VMEM_LIMIT_BYTES),
    )(h, w_q, keys)
    ec = rows_per_step * PEER_KEYS
    n_chunks = PEER_EXPERTS // ec
    return pl.pallas_call(
        functools.partial(_peer_expert_kernel, rows_per_step=rows_per_step, n_chunks=n_chunks),
        name="peer_experts",
        grid=(n_tok // tb, n_chunks + 1),
        in_specs=[pl.BlockSpec((tb, d), lambda i, j: (i, 0)),
                  pl.BlockSpec((nhp, PEER_KEYS, tb), lambda i, j: (0, 0, i)),
                  pl.BlockSpec((PEER_STAT_ROWS, tb), lambda i, j: (0, i)),
                  pl.BlockSpec((ec, d), lambda i, j: (jnp.minimum(j, n_chunks - 1), 0)),
                  pl.BlockSpec((d, ec), lambda i, j: (0, jnp.maximum(j - 1, 0)))],
        out_specs=pl.BlockSpec((tb, d), lambda i, j: (i, 0)),
        out_shape=jax.ShapeDtypeStruct((n_tok, d), F32),
        scratch_shapes=[pltpu.VMEM((nhp, PEER_KEYS, tb), F32),
                        pltpu.VMEM((PEER_HEADS, PEER_KEYS, tb), F32),
                        pltpu.VMEM((PEER_HEADS - PEER_POW_HEADS, PEER_KEYS, tb), F32),
                        pltpu.VMEM((d, tb), F32),
                        pltpu.VMEM((2, ec, tb), BF16)],
        compiler_params=pltpu.CompilerParams(dimension_semantics=("arbitrary", "arbitrary"),
                                             vmem_limit_bytes=V7X_VMEM_LIMIT_BYTES),
    )(h, s, st, u_tab, vt_tab)


def _peer(h, w_q, keys, u_tab, vt_tab):
    bsz, t, d = h.shape
    n_tok = bsz * t
    tb = 512 if n_tok % 512 == 0 else 256
    out = _peer_pallas(h.reshape(n_tok, d).astype(BF16), w_q, keys, u_tab, vt_tab, tb=tb, rows_per_step=16)
    return out.reshape(bsz, t, d)


ROW_BLOCK = 256
HIGHEST = lax.Precision.HIGHEST


def _dot_f32(a, b):
    return jnp.dot(a, b, precision=HIGHEST, preferred_element_type=F32)


def _segment_ones(n, seg, dtype):
    r = lax.broadcasted_iota(jnp.int32, (n, n), 0) // seg
    c = lax.broadcasted_iota(jnp.int32, (n, n), 1) // seg
    return jnp.where(r == c, 1.0, 0.0).astype(dtype)


def _shifted_rows(x, prev_row, next_row):
    t = x.shape[0]
    rows = lax.broadcasted_iota(jnp.int32, x.shape, 0)
    xm = jnp.where(rows == 0, prev_row, pltpu.roll(x, 1, axis=0))
    xp = jnp.where(rows == t - 1, next_row, pltpu.roll(x, t - 1, axis=0))
    return xm, xp


def _segment_edge_flags(i, n_blocks, ctx_blocks):
    is_start = jnp.logical_or(i == 0, i == ctx_blocks)
    is_end = jnp.logical_or(i == ctx_blocks - 1, i == n_blocks - 1)
    return jnp.where(is_start, 0.0, 1.0), jnp.where(is_end, 0.0, 1.0)


def _halo_specs(width, tr):
    g = tr // 8
    prev = pl.BlockSpec((1, 8, width), lambda b, i: (b, jnp.maximum(i * g - 1, 0), 0))
    nxt = lambda n_groups: pl.BlockSpec((1, 8, width), lambda b, i: (b, jnp.minimum((i + 1) * g, n_groups - 1), 0))
    return prev, nxt


def _softplus(z):
    return jnp.maximum(z, 0.0) + jnp.log1p(jnp.exp(-jnp.abs(z)))


def _rwkv_prep_kernel(x_ref, xprev_ref, xnext_ref, mu_ref, w0_ref, w2_ref, a0_ref, a2_ref, g2_ref,
                      kk_ref, ka_ref, rk_ref,
                      r_out, v_out, kkn_out, g_out, bonus_out, w_out, b_out, kt_out, *, ctx_blocks):
    i = pl.program_id(1)
    keep_prev, keep_next = _segment_edge_flags(i, pl.num_programs(1), ctx_blocks)
    x = x_ref[0]
    xm, xp = _shifted_rows(x, xprev_ref[0, 7:8, :] * keep_prev, xnext_ref[0, 0:1, :] * keep_next)
    mu0 = mu_ref[0:1, :]
    mu1 = mu_ref[1:2, :]
    f = xm * mu0 + x * (1.0 - mu0 - mu1) + xp * mu1
    r = f[:, 0:GW]
    k = f[:, GW:2 * GW]
    v = f[:, 2 * GW:3 * GW]
    o = 3 * GW
    wd = jnp.tanh(f[:, o:o + 2 * DECAY_LORA])
    ad = f[:, o + 2 * DECAY_LORA:o + 2 * DECAY_LORA + 2 * ICLR_LORA]
    gd = f[:, o + 2 * DECAY_LORA + 2 * ICLR_LORA:]
    w_raw = w0_ref[...] + _dot_f32(wd, w2_ref[...])
    log_decay = -jnp.exp(-_softplus(-w_raw) - 0.5)
    a = jax.nn.sigmoid(a0_ref[...] + _dot_f32(ad, a2_ref[...]))
    g = _dot_f32(jax.nn.sigmoid(gd), g2_ref[...])
    head_sum = _segment_ones(GW, HEAD_DIM, F32)
    kx = k * kk_ref[...]
    kkn = kx * lax.rsqrt(_dot_f32(kx * kx, head_sum) + 1e-6)
    kd_sum = jnp.zeros_like(k)
    for d in range(2):
        a_d = a[:, d * GW:(d + 1) * GW]
        kd = k * (1.0 + (a_d - 1.0) * ka_ref[...])
        kd_sum = kd_sum + kd
        w_out[d, 0] = log_decay[:, d * GW:(d + 1) * GW]
        b_out[d, 0] = kkn * a_d
        kt_out[d, 0] = kd
    r_out[0] = r
    v_out[0] = v
    kkn_out[0] = kkn
    g_out[0] = g
    bonus_out[0] = _dot_f32(r * kd_sum * rk_ref[...], head_sum) * v


RWKV_CHUNK = 64


def _rwkv_chunk_kernel(rf_ref, vf_ref, kkf_ref, wf_ref, bf_ref, ktf_ref,
                       rb_ref, vb_ref, kkb_ref, wb_ref, bb_ref, ktb_ref, yf_ref, yb_ref, st_ref):
    i = pl.program_id(1)
    c = RWKV_CHUNK
    n_chunks = rf_ref.shape[1] // c

    @pl.when(i == 0)
    def _():
        st_ref[...] = jnp.zeros_like(st_ref)

    rr = lax.broadcasted_iota(jnp.int32, (c, c), 0)
    ss = lax.broadcasted_iota(jnp.int32, (c, c), 1)
    eye = rr == ss
    ones_cc = jnp.ones((c, c), F32)
    incl = (rr >= ss, rr <= ss)
    strict = (rr > ss, rr < ss)
    levels = []
    blk = 1
    while blk < c:
        levels.append(jnp.logical_and(rr // (2 * blk) == ss // (2 * blk), rr // blk != ss // blk))
        blk *= 2
    bdot = lambda x, y: jnp.dot(x.astype(BF16), y.astype(BF16), preferred_element_type=F32)
    bdot_nt = lambda x, y: lax.dot_general(x.astype(BF16), y.astype(BF16), _NT, preferred_element_type=F32)
    bdot_tn = lambda x, y: lax.dot_general(x.astype(BF16), y.astype(BF16), (((0,), (0,)), ((), ())),
                                           preferred_element_type=F32)
    dirs = ((rf_ref, vf_ref, kkf_ref, wf_ref, bf_ref, ktf_ref, yf_ref),
            (rb_ref, vb_ref, kkb_ref, wb_ref, bb_ref, ktb_ref, yb_ref))
    heads = [(d, h) for d in range(2) for h in range(GROUP_HEADS)]
    row0 = lambda cc, d: (cc if d == 0 else n_chunks - 1 - cc) * c
    rows = lambda cc, d: slice(row0(cc, d), row0(cc, d) + c)

    logw = {(cc, d): dirs[d][3][0, 0, rows(cc, d), :] for cc in range(n_chunks) for d in range(2)}
    cum = {key: _dot_f32(jnp.where(incl[key[1]], 1.0, 0.0), lw) for key, lw in logw.items()}
    tot = {key: _dot_f32(ones_cc, lw) for key, lw in logw.items()}
    st = {}
    for cc in range(n_chunks):
        for d, h in heads:
            cols = slice(h * HEAD_DIM, (h + 1) * HEAD_DIM)
            r_ref, v_ref, kk_ref, _, b_ref, kt_ref, _ = dirs[d]
            lc = cum[cc, d][:, cols]
            grow = jnp.exp(-lc)
            st[cc, d, h] = dict(
                v=v_ref[0, rows(cc, d), cols],
                kap=kk_ref[0, rows(cc, d), cols] * jnp.exp(lc - logw[cc, d][:, cols]),
                bh=b_ref[0, 0, rows(cc, d), cols] * grow,
                kh=kt_ref[0, 0, rows(cc, d), cols] * grow,
                rh=r_ref[0, rows(cc, d), cols] * jnp.exp(lc),
                scale=jnp.exp(_dot_f32(jnp.where(eye, tot[cc, d][:, cols], 0.0), ones_cc)))
    for (cc, d, h), x in st.items():
        gram = bdot_nt(jnp.concatenate([x["kap"], x["rh"]], axis=0), jnp.concatenate([x["bh"], x["kh"]], axis=0))
        x["a_b"] = jnp.where(strict[d], gram[:c, :c], 0.0)
        x["a_k"] = jnp.where(strict[d], gram[:c, c:], 0.0)
        x["b_b"] = jnp.where(incl[d], gram[c:, :c], 0.0)
        x["b_k"] = jnp.where(incl[d], gram[c:, c:], 0.0)
        x["inv"] = jnp.where(eye, 1.0, 0.0) - jnp.where(levels[0], x["a_b"], 0.0)
    for lvl in levels[1:]:
        for x in st.values():
            x["t"] = bdot(jnp.where(lvl, x["a_b"], 0.0), x["inv"])
        for x in st.values():
            x["inv"] = x["inv"] - bdot(x["inv"], x["t"])
    for x in st.values():
        x["akv"] = bdot(x["a_k"], x["v"])
        x["y_const"] = bdot(x["b_k"], x["v"])
        x["kv"] = bdot_tn(x["kh"], x["v"])
    for x in st.values():
        both = bdot(x["inv"], jnp.concatenate([x["kap"], x["akv"]], axis=-1))
        x["p_state"], x["p_const"] = both[:, :HEAD_DIM], both[:, HEAD_DIM:]
    state = {key: st_ref[key[0], key[1]] for key in heads}
    for cc in range(n_chunks):
        cur = [(key, st[(cc,) + key]) for key in heads]
        for key, x in cur:
            x["p"] = bdot(x["p_state"], state[key]) + x["p_const"]
            x["y0"] = bdot(x["rh"], state[key]) + x["y_const"]
        for key, x in cur:
            x["y"] = x["y0"] - bdot(x["b_b"], x["p"])
            state[key] = (state[key] + x["kv"] - bdot_tn(x["bh"], x["p"])) * x["scale"]
        for d in range(2):
            dirs[d][6][0, rows(cc, d), :] = jnp.concatenate([x["y"] for (dd, h), x in cur if dd == d], axis=-1)
    for d, h in heads:
        st_ref[d, h] = state[d, h]


def _rwkv_post_rows(y, bonus, g, ln_g, ln_b):
    head_mean = _segment_ones(GW, HEAD_DIM, F32) * (1.0 / HEAD_DIM)
    yc = y - _dot_f32(y, head_mean)
    var = _dot_f32(yc * yc, head_mean)
    return (yc * lax.rsqrt(var + RWKV_GN_EPS) * ln_g + ln_b + bonus) * g


def _rwkv_post_kernel(yf_ref, yb_ref, bonus_ref, g_ref, lng_ref, lnb_ref, o_ref):
    o_ref[0] = _rwkv_post_rows(yf_ref[0] + yb_ref[0], bonus_ref[0], g_ref[0], lng_ref[...], lnb_ref[...])


def _block_diag2(m):
    z = jnp.zeros_like(m[0])
    return jnp.concatenate([jnp.concatenate([m[0], z], 1), jnp.concatenate([z, m[1]], 1)], 0)


def _rwkv7_pallas(f, mu, w0, w2, a0, a2, g2, k_k, k_a, r_k, ln_g, ln_b, *, ctx_len, apply_post=True):
    bsz, seq, cols = f.shape
    tr = ROW_BLOCK
    assert ctx_len % tr == 0 and seq % tr == 0
    nb, ctx_blocks = seq // tr, ctx_len // tr
    prev_spec, next_spec = _halo_specs(cols, tr)
    row2 = lambda a: a.reshape(1, -1).astype(F32)
    full = lambda a: pl.BlockSpec(a.shape, lambda b, i: (0,) * a.ndim)
    params = [mu, row2(w0), _block_diag2(w2), row2(a0), _block_diag2(a2), g2, row2(k_k), row2(k_a), row2(r_k)]
    act = jax.ShapeDtypeStruct((bsz, seq, GW), F32)
    act2 = jax.ShapeDtypeStruct((2, bsz, seq, GW), F32)
    blk = pl.BlockSpec((1, tr, GW), lambda b, i: (b, i, 0))
    blk2 = pl.BlockSpec((2, 1, tr, GW), lambda b, i: (0, b, i, 0))
    r, v, kkn, g, bonus, w, bb, kt = pl.pallas_call(
        functools.partial(_rwkv_prep_kernel, ctx_blocks=ctx_blocks),
        name="rwkv_prep",
        grid=(bsz, nb),
        in_specs=[pl.BlockSpec((1, tr, cols), lambda b, i: (b, i, 0)), prev_spec, next_spec(seq // 8)]
                 + [full(p) for p in params],
        out_specs=[blk] * 5 + [blk2] * 3,
        out_shape=[act] * 5 + [act2] * 3,
        compiler_params=pltpu.CompilerParams(dimension_semantics=("arbitrary", "arbitrary"),
                                             vmem_limit_bytes=V7X_VMEM_LIMIT_BYTES),
    )(f, f, f, *params)

    def bwd_block(c):
        return jnp.where(c < ctx_blocks, ctx_blocks - 1 - c, nb - 1 - (c - ctx_blocks))
    assert tr % RWKV_CHUNK == 0 and RWKV_CHUNK == HEAD_DIM
    fwd = pl.BlockSpec((1, tr, GW), lambda b, c: (b, c, 0))
    bwd = pl.BlockSpec((1, tr, GW), lambda b, c: (b, bwd_block(c), 0))
    fwd_d = pl.BlockSpec((1, 1, tr, GW), lambda b, c: (0, b, c, 0))
    bwd_d = pl.BlockSpec((1, 1, tr, GW), lambda b, c: (1, b, bwd_block(c), 0))
    yf, yb = pl.pallas_call(
        _rwkv_chunk_kernel,
        name="rwkv_chunks",
        grid=(bsz, nb),
        in_specs=[fwd, fwd, fwd, fwd_d, fwd_d, fwd_d, bwd, bwd, bwd, bwd_d, bwd_d, bwd_d],
        out_specs=[fwd, bwd],
        out_shape=[act, act],
        scratch_shapes=[pltpu.VMEM((2, GROUP_HEADS, HEAD_DIM, HEAD_DIM), F32)],
        compiler_params=pltpu.CompilerParams(dimension_semantics=("arbitrary", "arbitrary"),
                                             vmem_limit_bytes=V7X_VMEM_LIMIT_BYTES),
    )(r, v, kkn, w, bb, kt, r, v, kkn, w, bb, kt)
    if not apply_post:
        return yf, yb, bonus, g, row2(ln_g), row2(ln_b)

    return pl.pallas_call(
        _rwkv_post_kernel,
        name="rwkv_post",
        grid=(bsz, nb),
        in_specs=[blk, blk, blk, blk, full(row2(ln_g)), full(row2(ln_b))],
        out_specs=blk,
        out_shape=act,
        compiler_params=pltpu.CompilerParams(dimension_semantics=("arbitrary", "arbitrary")),
    )(yf, yb, bonus, g, row2(ln_g), row2(ln_b))


GDN_GATE_LANES = 128
GDN_PADDED_COLS = 4 * GW + GDN_GATE_LANES


def _gdn_prep_kernel(x_ref, xprev_ref, xnext_ref, ab_ref, conv_ref, alog_ref, dtb_ref,
                     q_out, k_out, v_out, gb_out, *, ctx_blocks):
    i = pl.program_id(1)
    keep_prev, keep_next = _segment_edge_flags(i, pl.num_programs(1), ctx_blocks)
    x = x_ref[0]
    xm, xp = _shifted_rows(x, xprev_ref[0, 7:8, :] * keep_prev, xnext_ref[0, 0:1, :] * keep_next)
    y = xm * conv_ref[0:1, :] + x * conv_ref[1:2, :] + xp * conv_ref[2:3, :]
    y = y * jax.nn.sigmoid(y)
    head_sum = _segment_ones(GW, HEAD_DIM, F32)
    q = y[:, 0:GW]
    k = y[:, GW:2 * GW]
    q_out[0] = q * lax.rsqrt(_dot_f32(q * q, head_sum) + 1e-6) * (HEAD_DIM ** -0.5)
    k_out[0] = k * lax.rsqrt(_dot_f32(k * k, head_sum) + 1e-6)
    v_out[0] = y[:, 2 * GW:3 * GW]
    ab = ab_ref[0]
    lane = lax.broadcasted_iota(jnp.int32, ab.shape, 1)
    log_alpha = -jnp.exp(alog_ref[...]) * _softplus(ab + dtb_ref[...])
    gb_out[0] = jnp.where(lane < 2 * GROUP_HEADS, log_alpha, jax.nn.sigmoid(ab))


def _gdn_chunk_kernel(qf_ref, kf_ref, vf_ref, gf_ref, qb_ref, kb_ref, vb_ref, gb_ref, of_ref, ob_ref, st_ref):
    i = pl.program_id(1)
    c = GDN_CHUNK
    n_chunks = qf_ref.shape[1] // c

    @pl.when(i == 0)
    def _():
        st_ref[...] = jnp.zeros_like(st_ref)

    r = lax.broadcasted_iota(jnp.int32, (c, c), 0)
    s = lax.broadcasted_iota(jnp.int32, (c, c), 1)
    eye = r == s
    ones_cc = jnp.ones((c, c), F32)
    incl = (r >= s, r <= s)
    strict = (r > s, r < s)
    levels = []
    b = 1
    while b < c:
        levels.append(jnp.logical_and(r // (2 * b) == s // (2 * b), r // b != s // b))
        b *= 2
    dirs = ((qf_ref, kf_ref, vf_ref, gf_ref, of_ref), (qb_ref, kb_ref, vb_ref, gb_ref, ob_ref))

    bdot = lambda x, y: jnp.dot(x.astype(BF16), y.astype(BF16), preferred_element_type=F32)
    bdot_nt = lambda x, y: lax.dot_general(x.astype(BF16), y.astype(BF16), _NT, preferred_element_type=F32)
    bdot_tn = lambda x, y: lax.dot_general(x.astype(BF16), y.astype(BF16), (((0,), (0,)), ((), ())),
                                           preferred_element_type=F32)

    heads = [(d, h) for d in range(2) for h in range(GROUP_HEADS)]
    row0 = lambda cc, d: (cc if d == 0 else n_chunks - 1 - cc) * c
    gates = {(cc, d): dirs[d][3][0, row0(cc, d):row0(cc, d) + c, :] for cc in range(n_chunks) for d in range(2)}
    cum = {key: _dot_f32(jnp.where(incl[key[1]], 1.0, 0.0), g) for key, g in gates.items()}
    tot = {key: _dot_f32(ones_cc, g) for key, g in gates.items()}
    chains = [(cc, d, h) for cc in range(n_chunks) for d, h in heads]
    st = {}
    for cc, d, h in chains:
        cols = slice(h * HEAD_DIM, (h + 1) * HEAD_DIM)
        lg = d * GROUP_HEADS + h
        q, k, v = (dirs[d][n][0, row0(cc, d):row0(cc, d) + c, cols] for n in range(3))
        gc = cum[cc, d][:, lg:lg + 1]
        st[cc, d, h] = dict(q=q, k=k, v=v, gc=gc, gt=tot[cc, d][:, lg:lg + 1],
                            beta=gates[cc, d][:, 2 * GROUP_HEADS + lg:2 * GROUP_HEADS + lg + 1],
                            gc_row=_dot_f32(ones_cc, jnp.where(eye, gc, 0.0)))
    for (cc, d, h), x in st.items():
        x["decay"] = jnp.exp(jnp.where(incl[d], x["gc"] - x["gc_row"], NEG_BIG))
        x["kb"] = x["k"] * x["beta"]
        gram = bdot_nt(jnp.concatenate([x["kb"], x["q"]], axis=0), x["k"])
        x["a"] = jnp.where(strict[d], gram[:c] * x["decay"], 0.0)
        x["qk"] = jnp.where(incl[d], gram[c:] * x["decay"], 0.0)
        x["inv"] = jnp.where(eye, 1.0, 0.0) - jnp.where(levels[0], x["a"], 0.0)
    for lvl in levels[1:]:
        for x in st.values():
            x["t"] = bdot(jnp.where(lvl, x["a"], 0.0), x["inv"])
        for x in st.values():
            x["inv"] = x["inv"] - bdot(x["inv"], x["t"])
    for x in st.values():
        x["eg"] = jnp.exp(x["gc"])
        x["sol"] = bdot(x["inv"], jnp.concatenate([x["v"] * x["beta"], x["kb"] * x["eg"]], axis=-1))
        x["qg"] = x["q"] * x["eg"]
        x["kg"] = x["k"] * jnp.exp(x["gt"] - x["gc"])
    state = {(d, h): st_ref[d, h] for d, h in heads}
    for cc in range(n_chunks):
        cur = [(key, st[(cc,) + key]) for key in heads]
        for key, x in cur:
            x["ws"] = bdot(x["sol"][:, HEAD_DIM:], state[key])
            x["qs"] = bdot(x["qg"], state[key])
        for key, x in cur:
            x["v_new"] = x["sol"][:, :HEAD_DIM] - x["ws"]
            x["o"] = x["qs"] + bdot(x["qk"], x["v_new"])
            x["upd"] = bdot_tn(x["kg"], x["v_new"])
        for key, x in cur:
            state[key] = state[key] * jnp.exp(x["gt"][0:1, :]) + x["upd"]
        for d in range(2):
            dirs[d][4][0, row0(cc, d):row0(cc, d) + c, :] = jnp.concatenate(
                [x["o"] for (dd, h), x in cur if dd == d], axis=-1)
    for d, h in heads:
        st_ref[d, h] = state[d, h]


def _gdn_post_rows(o, gate, g):
    head_mean = _segment_ones(GW, HEAD_DIM, F32) * (1.0 / HEAD_DIM)
    return o * lax.rsqrt(_dot_f32(o * o, head_mean) + 1e-6) * g * (gate * jax.nn.sigmoid(gate))


def _gdn_post_kernel(of_ref, ob_ref, gate_ref, g_ref, o_ref):
    o_ref[0] = _gdn_post_rows(of_ref[0] + ob_ref[0], gate_ref[0], g_ref[...])


def _gated_deltanet_pallas(f, conv_w, a_log, dt_bias, norm_g, *, ctx_len, apply_post=True):
    bsz, seq, width = f.shape
    tr = ROW_BLOCK
    assert ctx_len % tr == 0 and seq % tr == 0 and tr % GDN_CHUNK == 0
    if width == GDN_COLS:
        f = jnp.pad(f, ((0, 0), (0, 0), (0, GDN_PADDED_COLS - GDN_COLS)))
    assert f.shape[2] == GDN_PADDED_COLS
    nb, ctx_blocks = seq // tr, ctx_len // tr
    lane_pad = lambda a: jnp.pad(a.reshape(1, -1).astype(F32), ((0, 0), (0, GDN_GATE_LANES - a.size)))
    prev_spec, next_spec = _halo_specs(3 * GW, tr)
    full = lambda a: pl.BlockSpec(a.shape, lambda b, i: (0,) * a.ndim)
    act = jax.ShapeDtypeStruct((bsz, seq, GW), F32)
    gact = jax.ShapeDtypeStruct((bsz, seq, GDN_GATE_LANES), F32)
    blk = pl.BlockSpec((1, tr, GW), lambda b, i: (b, i, 0))
    gblk = pl.BlockSpec((1, tr, GDN_GATE_LANES), lambda b, i: (b, i, 0))
    gate_view = pl.BlockSpec((1, tr, GW), lambda b, i: (b, i, 3))
    ab_view = pl.BlockSpec((1, tr, GDN_GATE_LANES), lambda b, i: (b, i, 4 * GW // GDN_GATE_LANES))
    params = [conv_w.astype(F32), lane_pad(a_log), lane_pad(dt_bias)]
    q, k, v, gb = pl.pallas_call(
        functools.partial(_gdn_prep_kernel, ctx_blocks=ctx_blocks),
        name="gdn_prep",
        grid=(bsz, nb),
        in_specs=[pl.BlockSpec((1, tr, 3 * GW), lambda b, i: (b, i, 0)), prev_spec, next_spec(seq // 8), ab_view]
                 + [full(p) for p in params],
        out_specs=[blk, blk, blk, gblk],
        out_shape=[act, act, act, gact],
        compiler_params=pltpu.CompilerParams(dimension_semantics=("arbitrary", "arbitrary"),
                                             vmem_limit_bytes=V7X_VMEM_LIMIT_BYTES),
    )(f, f, f, f, *params)

    def bwd_block(i):
        return jnp.where(i < ctx_blocks, ctx_blocks - 1 - i, nb - 1 - (i - ctx_blocks))
    bblk = pl.BlockSpec((1, tr, GW), lambda b, i: (b, bwd_block(i), 0))
    bgblk = pl.BlockSpec((1, tr, GDN_GATE_LANES), lambda b, i: (b, bwd_block(i), 0))
    of, ob = pl.pallas_call(
        _gdn_chunk_kernel,
        name="gdn_chunks",
        grid=(bsz, nb),
        in_specs=[blk, blk, blk, gblk, bblk, bblk, bblk, bgblk],
        out_specs=[blk, bblk],
        out_shape=[act, act],
        scratch_shapes=[pltpu.VMEM((2, GROUP_HEADS, HEAD_DIM, HEAD_DIM), F32)],
        compiler_params=pltpu.CompilerParams(dimension_semantics=("arbitrary", "arbitrary"),
                                             vmem_limit_bytes=V7X_VMEM_LIMIT_BYTES),
    )(q, k, v, gb, q, k, v, gb)

    g_row = jnp.tile(norm_g.reshape(1, HEAD_DIM).astype(F32), (1, GROUP_HEADS))
    if not apply_post:
        return of, ob, f, g_row
    return pl.pallas_call(
        _gdn_post_kernel,
        name="gdn_post",
        grid=(bsz, nb),
        in_specs=[blk, blk, gate_view, full(g_row)],
        out_specs=blk,
        out_shape=act,
        compiler_params=pltpu.CompilerParams(dimension_semantics=("arbitrary", "arbitrary")),
    )(of, ob, f, g_row)


ROPE_PAIR = DIFF_HALF // 4


def _rope_tables(seq, ctx_len, q_scale):
    n = jnp.arange(seq - ctx_len, dtype=jnp.int32)
    row, col = n // GRID_W, n % GRID_W
    i = jnp.arange(HEAD_DIM)
    grp = (i % DIFF_HALF) // (2 * ROPE_PAIR)
    inv = ROPE_BASE ** (-(i % ROPE_PAIR).astype(F32) / ROPE_PAIR)
    pos = jnp.where(grp[None, :] == 0, row[:, None], col[:, None]).astype(F32)
    ang = pos * inv[None, :]
    sign = jnp.where((i % (2 * ROPE_PAIR)) < ROPE_PAIR, -1.0, 1.0)
    cos = jnp.concatenate([jnp.ones((ctx_len, HEAD_DIM), F32), jnp.cos(ang)], 0)
    sin = jnp.concatenate([jnp.zeros((ctx_len, HEAD_DIM), F32), jnp.sin(ang) * sign], 0)
    cos = jnp.tile(cos, (1, GROUP_HEADS))
    sin = jnp.tile(sin, (1, GROUP_HEADS))
    return jnp.concatenate([cos * q_scale, cos], 1), jnp.concatenate([sin * q_scale, sin], 1)


def _qkv_prep_kernel(p_ref, cos_ref, sin_ref, q_out, k_out, v_out):
    qk = p_ref[0, :, 0:2 * GW]
    width = 2 * GW
    lane = lax.broadcasted_iota(jnp.int32, qk.shape, 1)
    partner = jnp.where(lane % (2 * ROPE_PAIR) < ROPE_PAIR,
                        pltpu.roll(qk, width - ROPE_PAIR, axis=1), pltpu.roll(qk, ROPE_PAIR, axis=1))
    rot = qk * cos_ref[...] + partner * sin_ref[...]
    q_out[0] = rot[:, 0:GW].astype(BF16)
    v = p_ref[0, :, 2 * GW:3 * GW]
    for h in range(GROUP_HEADS):
        k_out[0, h] = rot[:, GW + h * HEAD_DIM:GW + (h + 1) * HEAD_DIM].astype(BF16)
        v_out[0, h] = v[:, h * HEAD_DIM:(h + 1) * HEAD_DIM].astype(BF16)


def _qkv_prep(p, cos, sin):
    bsz, seq, _ = p.shape
    tr = ROW_BLOCK
    head_major = jax.ShapeDtypeStruct((bsz, GROUP_HEADS, seq, HEAD_DIM), BF16)
    hm_spec = pl.BlockSpec((1, GROUP_HEADS, tr, HEAD_DIM), lambda b, i: (b, 0, i, 0))
    return pl.pallas_call(
        _qkv_prep_kernel,
        name="qkv_prep",
        grid=(bsz, seq // tr),
        in_specs=[pl.BlockSpec((1, tr, 3 * GW), lambda b, i: (b, i, 0)),
                  pl.BlockSpec((tr, 2 * GW), lambda b, i: (i, 0)),
                  pl.BlockSpec((tr, 2 * GW), lambda b, i: (i, 0))],
        out_specs=[pl.BlockSpec((1, tr, GW), lambda b, i: (b, i, 0)), hm_spec, hm_spec],
        out_shape=[jax.ShapeDtypeStruct((bsz, seq, GW), BF16), head_major, head_major],
        compiler_params=pltpu.CompilerParams(dimension_semantics=("arbitrary", "arbitrary")),
    )(p, cos, sin)


_NT = (((1,), (1,)), ((), ()))


def _softmax_pv(s, v):
    m = jnp.max(s, axis=-1, keepdims=True)
    e = jnp.exp(s - m)
    return jnp.dot(e.astype(BF16), v, preferred_element_type=F32) / jnp.sum(e, axis=-1, keepdims=True)


def _diff_attn_kernel(q_ref, k_ref, v_ref, lam_ref, g_ref, o_ref, *, ctx_blocks, ctx_len):
    i = pl.program_id(1)
    lv = lam_ref[...]
    lam_init = lv[4:5, 0:1]
    lam = (jnp.exp(jnp.sum(lv[0:1] * lv[1:2], axis=-1, keepdims=True))
           - jnp.exp(jnp.sum(lv[2:3] * lv[3:4], axis=-1, keepdims=True)) + lam_init)
    lane = lax.broadcasted_iota(jnp.int32, (q_ref.shape[1], HEAD_DIM), 1)

    def attend(n_keys):
        outs = []
        for h in range(GROUP_HEADS):
            qh = q_ref[0, :, h * HEAD_DIM:(h + 1) * HEAD_DIM]
            kh = k_ref[0, h, 0:n_keys, :]
            vh = v_ref[0, h, 0:n_keys, :]
            zero = jnp.zeros_like(qh)
            s1 = lax.dot_general(jnp.where(lane < DIFF_HALF, qh, zero), kh, _NT, preferred_element_type=F32)
            s2 = lax.dot_general(jnp.where(lane >= DIFF_HALF, qh, zero), kh, _NT, preferred_element_type=F32)
            o = _softmax_pv(s1, vh) - lam * _softmax_pv(s2, vh)
            o = o * lax.rsqrt(jnp.mean(o * o, axis=-1, keepdims=True) + 1e-6) * g_ref[...] * (1.0 - lam_init)
            outs.append(o)
        o_ref[0] = jnp.concatenate(outs, axis=-1)

    @pl.when(i < ctx_blocks)
    def _():
        attend(ctx_len)

    @pl.when(i >= ctx_blocks)
    def _():
        attend(k_ref.shape[2])


def _diff_attention_pallas(q, k, v, lam_vecs, norm_g, *, ctx_len, lam_init):
    bsz, seq, _ = q.shape
    tq = ROW_BLOCK
    kv_spec = pl.BlockSpec((1, GROUP_HEADS, seq, HEAD_DIM), lambda b, i: (b, 0, 0, 0))
    lam_rows = jnp.concatenate([lam_vecs.astype(F32), jnp.full((1, lam_vecs.shape[1]), lam_init, F32)], 0)
    return pl.pallas_call(
        functools.partial(_diff_attn_kernel, ctx_blocks=ctx_len // tq, ctx_len=ctx_len),
        name="diff_attn",
        grid=(bsz, seq // tq),
        in_specs=[pl.BlockSpec((1, tq, GW), lambda b, i: (b, i, 0)), kv_spec, kv_spec,
                  pl.BlockSpec(lam_rows.shape, lambda b, i: (0, 0)),
                  pl.BlockSpec((1, HEAD_DIM), lambda b, i: (0, 0))],
        out_specs=pl.BlockSpec((1, tq, GW), lambda b, i: (b, i, 0)),
        out_shape=jax.ShapeDtypeStruct((bsz, seq, GW), F32),
        compiler_params=pltpu.CompilerParams(dimension_semantics=("arbitrary", "arbitrary"),
                                             vmem_limit_bytes=V7X_VMEM_LIMIT_BYTES),
    )(q, k, v, lam_rows, norm_g.reshape(1, HEAD_DIM).astype(F32))


NAT_TILE_ROWS = ROW_BLOCK // GRID_W
NAT_SLAB_ROWS = NAT_TILE_ROWS + WIN_H - 1


def _nat_slab_start(tile, n_rows):
    return np.clip(tile * NAT_TILE_ROWS - WIN_H // 2, 0, n_rows - NAT_SLAB_ROWS)


def _nat_bias_tables(rpb, n_rows):
    n_tiles = n_rows // NAT_TILE_ROWS
    nq, nk, w = NAT_TILE_ROWS, NAT_SLAB_ROWS, GRID_W
    cq, ck = np.arange(w)[:, None], np.arange(w)[None, :]
    d_col = np.clip(ck - cq, -(WIN_W - 1), WIN_W - 1) + WIN_W - 1
    col_1h = (d_col.reshape(-1)[:, None] == np.arange(2 * WIN_W - 1)[None, :]).astype(np.float32)
    c0 = np.clip(cq - WIN_W // 2, 0, w - WIN_W)
    col_ok = (ck >= c0) & (ck < c0 + WIN_W)
    tabs = []
    for tile in (0, 1, n_tiles - 1):
        r = tile * nq + np.arange(nq)[:, None]
        kr = _nat_slab_start(tile, n_rows) + np.arange(nk)[None, :]
        rs = np.clip(r - WIN_H // 2, 0, n_rows - WIN_H)
        row_ok = (kr >= rs) & (kr < rs + WIN_H)
        d_row = np.clip(kr - r + WIN_H - 1, 0, 2 * WIN_H - 2)
        row_1h = (d_row.reshape(-1)[:, None] == np.arange(2 * WIN_H - 1)[None, :]).astype(np.float32)
        t = jnp.einsum('pa,hab,cb->hpc', row_1h, rpb.astype(F32), col_1h, precision=HIGHEST)
        t = t.reshape(GROUP_HEADS, nq, nk, w, w).transpose(0, 1, 3, 2, 4)
        ok = row_ok[:, None, :, None] & col_ok[None, :, None, :]
        tabs.append(jnp.where(ok[None], t, NEG_BIG).reshape(GROUP_HEADS, nq * w, nk * w))
    return jnp.stack(tabs)


def _nat_attn_kernel(q_ref, k_ref, v_ref, bias_ref, o_ref, *, ctx_blocks, ctx_len, n_rows):
    i = pl.program_id(1)
    n_slab = NAT_SLAB_ROWS * GRID_W

    def heads(fn):
        o_ref[0] = jnp.concatenate(
            [fn(h, q_ref[0, :, h * HEAD_DIM:(h + 1) * HEAD_DIM]) for h in range(GROUP_HEADS)], axis=-1)

    @pl.when(i < ctx_blocks)
    def _():
        def ctx_only(h, qh):
            s = lax.dot_general(qh, k_ref[0, h, 0:ctx_len, :], _NT, preferred_element_type=F32)
            return _softmax_pv(s, v_ref[0, h, 0:ctx_len, :])
        heads(ctx_only)

    @pl.when(i >= ctx_blocks)
    def _():
        tile = i - ctx_blocks
        start = jnp.clip(tile * NAT_TILE_ROWS - WIN_H // 2, 0, n_rows - NAT_SLAB_ROWS)
        off = pl.multiple_of(ctx_len + start * GRID_W, GRID_W)

        def windowed(h, qh):
            s_w = lax.dot_general(qh, k_ref[0, h, pl.ds(off, n_slab), :], _NT,
                                  preferred_element_type=F32) + bias_ref[0, h]
            s_c = lax.dot_general(qh, k_ref[0, h, 0:ctx_len, :], _NT, preferred_element_type=F32)
            m = jnp.maximum(jnp.max(s_w, axis=-1, keepdims=True), jnp.max(s_c, axis=-1, keepdims=True))
            e_w = jnp.exp(s_w - m)
            e_c = jnp.exp(s_c - m)
            den = jnp.sum(e_w, axis=-1, keepdims=True) + jnp.sum(e_c, axis=-1, keepdims=True)
            num = (jnp.dot(e_w.astype(BF16), v_ref[0, h, pl.ds(off, n_slab), :], preferred_element_type=F32)
                   + jnp.dot(e_c.astype(BF16), v_ref[0, h, 0:ctx_len, :], preferred_element_type=F32))
            return num / den
        heads(windowed)


def _nat_attention_pallas(q, k, v, rpb, *, ctx_len):
    bsz, seq, _ = q.shape
    tq = ROW_BLOCK
    ctx_blocks = ctx_len // tq
    n_rows = (seq - ctx_len) // GRID_W
    n_tiles = n_rows // NAT_TILE_ROWS
    assert n_rows >= NAT_SLAB_ROWS and n_tiles >= 3
    bias = _nat_bias_tables(rpb, n_rows)

    def variant(i):
        tile = i - ctx_blocks
        return jnp.where(tile <= 0, 0, jnp.where(tile >= n_tiles - 1, 2, 1))
    kv_spec = pl.BlockSpec((1, GROUP_HEADS, seq, HEAD_DIM), lambda b, i: (b, 0, 0, 0))
    return pl.pallas_call(
        functools.partial(_nat_attn_kernel, ctx_blocks=ctx_blocks, ctx_len=ctx_len, n_rows=n_rows),
        name="nat_attn",
        grid=(bsz, seq // tq),
        in_specs=[pl.BlockSpec((1, tq, GW), lambda b, i: (b, i, 0)), kv_spec, kv_spec,
                  pl.BlockSpec((1,) + bias.shape[1:], lambda b, i: (variant(i), 0, 0, 0))],
        out_specs=pl.BlockSpec((1, tq, GW), lambda b, i: (b, i, 0)),
        out_shape=jax.ShapeDtypeStruct((bsz, seq, GW), F32),
        compiler_params=pltpu.CompilerParams(dimension_semantics=("arbitrary", "arbitrary"),
                                             vmem_limit_bytes=V7X_VMEM_LIMIT_BYTES),
    )(q, k, v, bias)


N_MOD = 6
MATMUL_ROWS = 512


def _ada_kernel(c_ref, w_ref, b_ref, o_ref):
    c = c_ref[...]
    o_ref[...] = _dot_f32(c * jax.nn.sigmoid(c), w_ref[...]) + b_ref[...]


def _ada_modulation(c, c_ctx, w_ada, b_ada):
    bsz, d = c.shape
    rows = 8 * ((bsz + 1 + 7) // 8)
    cc = jnp.zeros((rows, d), F32).at[:bsz].set(c).at[bsz].set(c_ctx)
    tn = d
    m = pl.pallas_call(
        _ada_kernel,
        name="ada_modulation",
        grid=(w_ada.shape[1] // tn,),
        in_specs=[pl.BlockSpec((rows, d), lambda j: (0, 0)),
                  pl.BlockSpec((d, tn), lambda j: (0, j)),
                  pl.BlockSpec((1, tn), lambda j: (0, j))],
        out_specs=pl.BlockSpec((rows, tn), lambda j: (0, j)),
        out_shape=jax.ShapeDtypeStruct((rows, w_ada.shape[1]), F32),
        compiler_params=pltpu.CompilerParams(dimension_semantics=("arbitrary",)),
    )(cc, w_ada, b_ada.reshape(1, -1))
    lat = m[:bsz].reshape(bsz, 1, N_MOD, d)
    ctx = jnp.broadcast_to(m[bsz].reshape(1, 1, N_MOD, d), (bsz, 1, N_MOD, d))
    return jnp.concatenate([ctx, lat], axis=1)


def _ln(x):
    mu = jnp.mean(x, axis=-1, keepdims=True)
    xc = x - mu
    return xc * lax.rsqrt(jnp.mean(xc * xc, axis=-1, keepdims=True) + LN_EPS)


def _modulate_kernel(h_ref, mod_ref, o_ref, *, shift_row):
    shift = mod_ref[0, 0, shift_row:shift_row + 1, :]
    scale = mod_ref[0, 0, shift_row + 1:shift_row + 2, :]
    o_ref[0] = (_ln(h_ref[0]) * (1.0 + scale) + shift).astype(o_ref.dtype)


def _mod_spec(d, ctx_blocks):
    return pl.BlockSpec((1, 1, N_MOD, d), lambda b, i: (b, jnp.where(i < ctx_blocks, 0, 1), 0, 0))


def _modulate_pallas(hs, mod, shift_row, *, ctx_len):
    bsz, seq, d = hs.shape
    tr = ROW_BLOCK
    blk = pl.BlockSpec((1, tr, d), lambda b, i: (b, i, 0))
    return pl.pallas_call(
        functools.partial(_modulate_kernel, shift_row=shift_row),
        name="modulate",
        grid=(bsz, seq // tr),
        in_specs=[blk, _mod_spec(d, ctx_len // tr)],
        out_specs=blk,
        out_shape=jax.ShapeDtypeStruct((bsz, seq, d), BF16),
        compiler_params=pltpu.CompilerParams(dimension_semantics=("arbitrary", "arbitrary")),
    )(hs, mod)


def _matmul_kernel(x_ref, w_ref, o_ref):
    o_ref[...] = jnp.dot(x_ref[...], w_ref[...], preferred_element_type=F32)


def _matmul_pallas(x, w):
    m, k = x.shape
    n = w.shape[1]
    tm = MATMUL_ROWS
    return pl.pallas_call(
        _matmul_kernel,
        name="in_proj",
        grid=(m // tm,),
        in_specs=[pl.BlockSpec((tm, k), lambda i: (i, 0)), pl.BlockSpec((k, n), lambda i: (0, 0))],
        out_specs=pl.BlockSpec((tm, n), lambda i: (i, 0)),
        out_shape=jax.ShapeDtypeStruct((m, n), F32),
        compiler_params=pltpu.CompilerParams(dimension_semantics=("arbitrary",),
                                             vmem_limit_bytes=V7X_VMEM_LIMIT_BYTES),
    )(x, w)


def _post_norm_rows(h, gate, y, g, b):
    return _ln(DN_ALPHA * h + gate * y) * g + b


def _out_proj_kernel(ya_ref, rf_ref, rb_ref, bonus_ref, rg_ref, lng_ref, lnb_ref, of_ref, ob_ref, gate_ref, gn_ref,
                     yd_ref, w_ref, h_ref, mod_ref, g_ref, b_ref, o_ref):
    yb = _rwkv_post_rows(rf_ref[0] + rb_ref[0], bonus_ref[0], rg_ref[0], lng_ref[...], lnb_ref[...])
    yc = _gdn_post_rows(of_ref[0] + ob_ref[0], gate_ref[0], gn_ref[...])
    mix = None
    for n, y in enumerate((ya_ref[0], yb, yc, yd_ref[0])):
        part = jnp.dot(y.astype(BF16), w_ref[n * GW:(n + 1) * GW, :], preferred_element_type=F32)
        mix = part if mix is None else mix + part
    o_ref[0] = _post_norm_rows(h_ref[0], mod_ref[0, 0, 2:3, :], mix, g_ref[...], b_ref[...])


def _out_proj_post_norm(ya, rwkv_parts, gdn_parts, yd, w_out, hs, mod, g, b, *, ctx_len):
    bsz, seq, d = hs.shape
    tr = ROW_BLOCK
    yblk = pl.BlockSpec((1, tr, GW), lambda bb, i: (bb, i, 0))
    blk = pl.BlockSpec((1, tr, d), lambda bb, i: (bb, i, 0))
    row = pl.BlockSpec((1, d), lambda bb, i: (0, 0))
    grow = pl.BlockSpec((1, GW), lambda bb, i: (0, 0))
    gate_view = pl.BlockSpec((1, tr, GW), lambda bb, i: (bb, i, 3))
    return pl.pallas_call(
        _out_proj_kernel,
        name="out_proj_post_norm",
        grid=(bsz, seq // tr),
        in_specs=[yblk] + [yblk] * 4 + [grow, grow] + [yblk, yblk, gate_view, grow] + [yblk]
                 + [pl.BlockSpec(w_out.shape, lambda bb, i: (0, 0)), blk, _mod_spec(d, ctx_len // tr), row, row],
        out_specs=blk,
        out_shape=jax.ShapeDtypeStruct((bsz, seq, d), F32),
        compiler_params=pltpu.CompilerParams(dimension_semantics=("arbitrary", "arbitrary")),
    )(ya, *rwkv_parts, *gdn_parts, yd, w_out, hs, mod, g.reshape(1, d), b.reshape(1, d))


def _ffn_post_norm_kernel(h_ref, y_ref, mod_ref, g_ref, b_ref, o_ref):
    o_ref[0] = _post_norm_rows(h_ref[0], mod_ref[0, 0, 5:6, :], y_ref[0], g_ref[...], b_ref[...])


def _ffn_post_norm(hs, y, mod, g, b, *, ctx_len):
    bsz, seq, d = hs.shape
    tr = ROW_BLOCK
    blk = pl.BlockSpec((1, tr, d), lambda bb, i: (bb, i, 0))
    row = pl.BlockSpec((1, d), lambda bb, i: (0, 0))
    return pl.pallas_call(
        _ffn_post_norm_kernel,
        name="ffn_post_norm",
        grid=(bsz, seq // tr),
        in_specs=[blk, blk, _mod_spec(d, ctx_len // tr), row, row],
        out_specs=blk,
        out_shape=jax.ShapeDtypeStruct((bsz, seq, d), F32),
        compiler_params=pltpu.CompilerParams(dimension_semantics=("arbitrary", "arbitrary")),
    )(hs, y, mod, g.reshape(1, d), b.reshape(1, d))


def kernel(x, c, ctx, c_ctx, w_ada, b_ada, w_in, w_out, ln_mix_g, ln_mix_b, ln_ffn_g, ln_ffn_b, diff_lam, diff_norm_g, rwkv_mu, rwkv_w0, rwkv_w2, rwkv_a0, rwkv_a2, rwkv_g2, rwkv_kk, rwkv_ka, rwkv_rk, rwkv_ln_g, rwkv_ln_b, gdn_conv, gdn_a_log, gdn_dt_bias, gdn_norm_g, nat_rpb, peer_wq, peer_keys, peer_u, peer_v):
    dtype = x.dtype
    bsz, ctx_len = ctx.shape[0], ctx.shape[1]
    hs = jnp.concatenate([ctx, x], axis=1)
    seq = hs.shape[1]
    col_sizes = [ATTN_COLS, RWKV_COLS, GDN_COLS, ATTN_COLS]
    cos_a, sin_a = _rope_tables(seq, ctx_len, DIFF_HALF ** -0.5)
    cos_d = jnp.concatenate([jnp.full((seq, GW), HEAD_DIM ** -0.5, F32), jnp.ones((seq, GW), F32)], 1)
    sin_d = jnp.zeros_like(cos_d)
    col_offs = np.cumsum([0] + col_sizes)
    d_model = hs.shape[2]
    for l in range(DEPTH):
        lam_init = 0.8 - 0.6 * math.exp(-0.3 * l)
        mod = _ada_modulation(c, c_ctx, w_ada[l], b_ada[l])
        u = _modulate_pallas(hs, mod, 0, ctx_len=ctx_len).reshape(bsz * seq, d_model)
        w_in_b = w_in[l].astype(BF16)
        w_groups = [w_in_b[:, col_offs[n]:col_offs[n + 1]] for n in range(4)]
        w_groups[2] = jnp.pad(w_groups[2], ((0, 0), (0, GDN_PADDED_COLS - GDN_COLS)))
        pa, pb, pc, pd = [_matmul_pallas(u, w).reshape(bsz, seq, w.shape[1]) for w in w_groups]
        qa, ka, va = _qkv_prep(pa, cos_a, sin_a)
        ya = _diff_attention_pallas(qa, ka, va, diff_lam[l], diff_norm_g[l], ctx_len=ctx_len, lam_init=lam_init)
        yb_parts = _rwkv7_pallas(pb, rwkv_mu[l], rwkv_w0[l], rwkv_w2[l], rwkv_a0[l], rwkv_a2[l], rwkv_g2[l],
                                 rwkv_kk[l], rwkv_ka[l], rwkv_rk[l], rwkv_ln_g[l], rwkv_ln_b[l], ctx_len=ctx_len,
                                 apply_post=False)
        yc_parts = _gated_deltanet_pallas(pc, gdn_conv[l], gdn_a_log[l], gdn_dt_bias[l], gdn_norm_g[l],
                                          ctx_len=ctx_len, apply_post=False)
        qd, kd, vd = _qkv_prep(pd, cos_d, sin_d)
        yd = _nat_attention_pallas(qd, kd, vd, nat_rpb[l], ctx_len=ctx_len)
        hs = _out_proj_post_norm(ya, yb_parts, yc_parts, yd, w_out[l].astype(BF16), hs, mod, ln_mix_g[l],
                                 ln_mix_b[l], ctx_len=ctx_len)
        wq_b = peer_wq[l].astype(BF16)
        keys_b = peer_keys[l].reshape(2 * PEER_HEADS, PEER_KEYS, PEER_HALF).astype(BF16)
        u_b = peer_u[l].astype(BF16)
        vt_b = peer_v[l].astype(BF16).T
        ffn_ctx = ctx_len
        if l == DEPTH - 1:
            hs, ffn_ctx = hs[:, ctx_len:], 0
        ffn = _peer(_modulate_pallas(hs, mod, 3, ctx_len=ffn_ctx), wq_b, keys_b, u_b, vt_b)
        hs = _ffn_post_norm(hs, ffn, mod, ln_ffn_g[l], ln_ffn_b[l], ctx_len=ffn_ctx)
    return hs.astype(dtype)
```

```python
import functools
import math

import jax
import jax.numpy as jnp
import numpy as np
from jax import lax
from jax.experimental import pallas as pl
from jax.experimental.pallas import tpu as pltpu

D_MODEL = 1024
DEPTH = 2
GRID_W = 64
HEAD_DIM = 64
N_GROUPS = 4
GROUP_HEADS = D_MODEL // (N_GROUPS * HEAD_DIM)
GW = GROUP_HEADS * HEAD_DIM
D_MIX = N_GROUPS * GW
DIFF_HALF = HEAD_DIM // 2
Q_BLOCK = 128
ROPE_BASE = 10000.0
DECAY_LORA = 64
ICLR_LORA = 64
GATE_LORA = 128
RWKV_GN_EPS = 64e-5
RWKV_COLS = 3 * GW + 2 * DECAY_LORA + 2 * ICLR_LORA + GATE_LORA
GDN_CONV = 3
GDN_CHUNK = 64
GDN_COLS = 4 * GW + 4 * GROUP_HEADS
WIN_H = 8
WIN_W = 16
ATTN_COLS = 3 * GW
IN_COLS = ATTN_COLS + RWKV_COLS + GDN_COLS + ATTN_COLS
PEER_HEADS = 8
PEER_KEYS = 128
PEER_EXPERTS = PEER_KEYS * PEER_KEYS
PEER_QDIM = 256
PEER_HALF = PEER_QDIM // 2
PEER_TOPK = 16
DN_ALPHA = (2 * DEPTH) ** 0.25
LN_EPS = 1e-5

F32 = jnp.float32
BF16 = jnp.bfloat16

V7X_VMEM_LIMIT_BYTES = 56 * 1024 * 1024
NEG_BIG = -3.0e38


PEER_STAT_ROWS = 4 * PEER_HEADS
LOG2E = 1.4426950408889634
PEER_POW_HEADS = 4
assert PEER_TOPK == 16


def _topk_rows(x, k):
    q = x.shape[0] // 4
    lv = [x[i * q:(i + 1) * q] for i in range(4)]
    for a, b in ((0, 1), (2, 3), (0, 2), (1, 3), (1, 2)):
        lv[a], lv[b] = jnp.maximum(lv[a], lv[b]), jnp.minimum(lv[a], lv[b])
    rows = []
    for i in range(k):
        m = jnp.max(lv[0], axis=0, keepdims=True)
        rows.append(m)
        if i + 1 < k:
            hit = lv[0] == m
            lv = [jnp.where(hit, lv[n + 1], lv[n]) for n in range(3)] + [jnp.where(hit, NEG_BIG, lv[3])]
    return rows


def _peer_score_kernel(x_ref, wq_ref, keys_ref, s_ref, st_ref):
    q = jnp.dot(x_ref[...], wq_ref[...], preferred_element_type=F32).astype(BF16)
    stats = []
    for h in range(PEER_HEADS):
        tops = []
        for p in range(2):
            hp = 2 * h + p
            s_t = lax.dot_general(keys_ref[hp], q[:, hp * PEER_HALF:(hp + 1) * PEER_HALF],
                                  (((1,), (1,)), ((), ())), preferred_element_type=F32)
            s_ref[hp] = s_t
            tops.append(_topk_rows(s_t, PEER_TOPK + 1))
        a, b = tops
        pad = [jnp.full_like(a[0], NEG_BIG)] * 7
        b_head = jnp.concatenate(b[:8], axis=0)
        cand = jnp.concatenate([a[0] + jnp.concatenate(b + pad, axis=0)]
                               + [a[i] + b_head for i in range(1, 8)]
                               + [jnp.concatenate(a[8:] + pad, axis=0) + b[0]], axis=0)
        best_cand = _topk_rows(cand, PEER_TOPK + 1)
        kth, runner_up = best_cand[PEER_TOPK - 1], best_cand[PEER_TOPK]
        best = a[0] + b[0]
        z = jnp.sum(jnp.where(cand >= kth, jnp.exp(cand - best), 0.0), axis=0, keepdims=True)
        stats += [0.5 * (kth + runner_up), tops[0][0], tops[1][0], 1.0 / z]
    st_ref[...] = jnp.concatenate(stats, axis=0)


def _peer_expert_kernel(x_ref, s_ref, st_ref, u_ref, vt_ref, o_ref, e_ref, thr_ref, lin_ref, acc_ref, w_ref,
                        *, rows_per_step, n_chunks):
    j = pl.program_id(1)
    cur = lax.rem(j, 2)

    @pl.when(j == 0)
    def _():
        acc_ref[...] = jnp.zeros_like(acc_ref)
        w_ref[1] = jnp.zeros(w_ref.shape[1:], w_ref.dtype)
        for h in range(PEER_HEADS):
            a0 = st_ref[4 * h + 1:4 * h + 2, :]
            b0 = st_ref[4 * h + 2:4 * h + 3, :]
            rz = st_ref[4 * h + 3:4 * h + 4, :]
            l1 = (s_ref[2 * h] - a0 + jnp.log(rz)) * LOG2E
            l2 = (s_ref[2 * h + 1] - b0) * LOG2E
            e_ref[2 * h] = l1 if h < PEER_POW_HEADS else jnp.exp2(l1)
            e_ref[2 * h + 1] = l2
            if h >= PEER_POW_HEADS:
                lin_ref[h - PEER_POW_HEADS] = jnp.exp2(l2)
            thr_ref[h] = (st_ref[4 * h:4 * h + 1, :] - s_ref[2 * h] - b0) * LOG2E

    @pl.when(j < n_chunks)
    def _():
        slab = 2 * PEER_KEYS
        n_slabs = rows_per_step * PEER_KEYS // slab
        act_slab = lambda n: lax.dot_general(u_ref[n * slab:(n + 1) * slab, :], x_ref[...], (((1,), (1,)), ((), ())),
                                             preferred_element_type=F32)
        acts = [act_slab(n) for n in range(n_slabs)]
        acc_ref[...] += jnp.dot(vt_ref[...], w_ref[1 - cur], preferred_element_type=F32)
        assert rows_per_step % 8 == 0
        for il in range(rows_per_step):
            base = pl.multiple_of(j * rows_per_step + (il // 8) * 8, 8)
            r = il % 8
            o = il * PEER_KEYS
            act = acts[o // slab][o % slab:o % slab + PEER_KEYS]
            act = 0.5 * act * (1.0 + lax.erf(act * (2.0 ** -0.5)))
            gate = None
            for h in range(PEER_HEADS):
                l2 = e_ref[2 * h + 1]
                thr_row = thr_ref[h, pl.ds(base, 8), :][r:r + 1, :]
                half1 = e_ref[2 * h, pl.ds(base, 8), :][r:r + 1, :]
                pair = jnp.exp2(l2 + half1) if h < PEER_POW_HEADS else lin_ref[h - PEER_POW_HEADS] * half1
                term = jnp.where(l2 >= thr_row, pair, 0.0)
                gate = term if gate is None else gate + term
            w_ref[cur, o:o + PEER_KEYS, :] = (act * gate).astype(BF16)

    @pl.when(j == n_chunks)
    def _():
        o_ref[...] = (acc_ref[...] + jnp.dot(vt_ref[...], w_ref[1 - cur], preferred_element_type=F32)).T


def _peer_pallas(h, w_q, keys, u_tab, vt_tab, *, tb, rows_per_step):
    n_tok, d = h.shape
    nhp = 2 * PEER_HEADS
    s, st = pl.pallas_call(
        _peer_score_kernel,
        name="peer_scores",
        grid=(n_tok // tb,),
        in_specs=[pl.BlockSpec((tb, d), lambda i: (i, 0)),
                  pl.BlockSpec(w_q.shape, lambda i: (0, 0)),
                  pl.BlockSpec(keys.shape, lambda i: (0, 0, 0))],
        out_specs=[pl.BlockSpec((nhp, PEER_KEYS, tb), lambda i: (0, 0, i)),
                   pl.BlockSpec((PEER_STAT_ROWS, tb), lambda i: (0, i))],
        out_shape=[jax.ShapeDtypeStruct((nhp, PEER_KEYS, n_tok), F32),
                   jax.ShapeDtypeStruct((PEER_STAT_ROWS, n_tok), F32)],
        compiler_params=pltpu.CompilerParams(dimension_semantics=("arbitrary",),
                                             vmem_limit_bytes=V7X_VMEM_LIMIT_BYTES),
    )(h, w_q, keys)
    ec = rows_per_step * PEER_KEYS
    n_chunks = PEER_EXPERTS // ec
    return pl.pallas_call(
        functools.partial(_peer_expert_kernel, rows_per_step=rows_per_step, n_chunks=n_chunks),
        name="peer_experts",
        grid=(n_tok // tb, n_chunks + 1),
        in_specs=[pl.BlockSpec((tb, d), lambda i, j: (i, 0)),
                  pl.BlockSpec((nhp, PEER_KEYS, tb), lambda i, j: (0, 0, i)),
                  pl.BlockSpec((PEER_STAT_ROWS, tb), lambda i, j: (0, i)),
                  pl.BlockSpec((ec, d), lambda i, j: (jnp.minimum(j, n_chunks - 1), 0)),
                  pl.BlockSpec((d, ec), lambda i, j: (0, jnp.maximum(j - 1, 0)))],
        out_specs=pl.BlockSpec((tb, d), lambda i, j: (i, 0)),
        out_shape=jax.ShapeDtypeStruct((n_tok, d), F32),
        scratch_shapes=[pltpu.VMEM((nhp, PEER_KEYS, tb), F32),
                        pltpu.VMEM((PEER_HEADS, PEER_KEYS, tb), F32),
                        pltpu.VMEM((PEER_HEADS - PEER_POW_HEADS, PEER_KEYS, tb), F32),
                        pltpu.VMEM((d, tb), F32),
                        pltpu.VMEM((2, ec, tb), BF16)],
        compiler_params=pltpu.CompilerParams(dimension_semantics=("arbitrary", "arbitrary"),
                                             vmem_limit_bytes=V7X_VMEM_LIMIT_BYTES),
    )(h, s, st, u_tab, vt_tab)


def _peer(h, w_q, keys, u_tab, vt_tab):
    bsz, t, d = h.shape
    n_tok = bsz * t
    tb = 512 if n_tok % 512 == 0 else 256
    out = _peer_pallas(h.reshape(n_tok, d).astype(BF16), w_q, keys, u_tab, vt_tab, tb=tb, rows_per_step=16)
    return out.reshape(bsz, t, d)


ROW_BLOCK = 256
HIGHEST = lax.Precision.HIGHEST


def _dot_f32(a, b):
    return jnp.dot(a, b, precision=HIGHEST, preferred_element_type=F32)


def _segment_ones(n, seg, dtype):
    r = lax.broadcasted_iota(jnp.int32, (n, n), 0) // seg
    c = lax.broadcasted_iota(jnp.int32, (n, n), 1) // seg
    return jnp.where(r == c, 1.0, 0.0).astype(dtype)


def _shifted_rows(x, prev_row, next_row):
    t = x.shape[0]
    rows = lax.broadcasted_iota(jnp.int32, x.shape, 0)
    xm = jnp.where(rows == 0, prev_row, pltpu.roll(x, 1, axis=0))
    xp = jnp.where(rows == t - 1, next_row, pltpu.roll(x, t - 1, axis=0))
    return xm, xp


def _segment_edge_flags(i, n_blocks, ctx_blocks):
    is_start = jnp.logical_or(i == 0, i == ctx_blocks)
    is_end = jnp.logical_or(i == ctx_blocks - 1, i == n_blocks - 1)
    return jnp.where(is_start, 0.0, 1.0), jnp.where(is_end, 0.0, 1.0)


def _halo_specs(width, tr):
    g = tr // 8
    prev = pl.BlockSpec((1, 8, width), lambda b, i: (b, jnp.maximum(i * g - 1, 0), 0))
    nxt = lambda n_groups: pl.BlockSpec((1, 8, width), lambda b, i: (b, jnp.minimum((i + 1) * g, n_groups - 1), 0))
    return prev, nxt


def _softplus(z):
    return jnp.maximum(z, 0.0) + jnp.log1p(jnp.exp(-jnp.abs(z)))


def _rwkv_prep_kernel(x_ref, xprev_ref, xnext_ref, mu_ref, w0_ref, w2_ref, a0_ref, a2_ref, g2_ref,
                      kk_ref, ka_ref, rk_ref,
                      r_out, v_out, kkn_out, g_out, bonus_out, w_out, b_out, kt_out, *, ctx_blocks):
    i = pl.program_id(1)
    keep_prev, keep_next = _segment_edge_flags(i, pl.num_programs(1), ctx_blocks)
    x = x_ref[0]
    xm, xp = _shifted_rows(x, xprev_ref[0, 7:8, :] * keep_prev, xnext_ref[0, 0:1, :] * keep_next)
    mu0 = mu_ref[0:1, :]
    mu1 = mu_ref[1:2, :]
    f = xm * mu0 + x * (1.0 - mu0 - mu1) + xp * mu1
    r = f[:, 0:GW]
    k = f[:, GW:2 * GW]
    v = f[:, 2 * GW:3 * GW]
    o = 3 * GW
    wd = jnp.tanh(f[:, o:o + 2 * DECAY_LORA])
    ad = f[:, o + 2 * DECAY_LORA:o + 2 * DECAY_LORA + 2 * ICLR_LORA]
    gd = f[:, o + 2 * DECAY_LORA + 2 * ICLR_LORA:]
    w_raw = w0_ref[...] + _dot_f32(wd, w2_ref[...])
    log_decay = -jnp.exp(-_softplus(-w_raw) - 0.5)
    a = jax.nn.sigmoid(a0_ref[...] + _dot_f32(ad, a2_ref[...]))
    g = _dot_f32(jax.nn.sigmoid(gd), g2_ref[...])
    head_sum = _segment_ones(GW, HEAD_DIM, F32)
    kx = k * kk_ref[...]
    kkn = kx * lax.rsqrt(_dot_f32(kx * kx, head_sum) + 1e-6)
    kd_sum = jnp.zeros_like(k)
    for d in range(2):
        a_d = a[:, d * GW:(d + 1) * GW]
        kd = k * (1.0 + (a_d - 1.0) * ka_ref[...])
        kd_sum = kd_sum + kd
        w_out[d, 0] = log_decay[:, d * GW:(d + 1) * GW]
        b_out[d, 0] = kkn * a_d
        kt_out[d, 0] = kd
    r_out[0] = r
    v_out[0] = v
    kkn_out[0] = kkn
    g_out[0] = g
    bonus_out[0] = _dot_f32(r * kd_sum * rk_ref[...], head_sum) * v


RWKV_CHUNK = 64


def _rwkv_chunk_kernel(rf_ref, vf_ref, kkf_ref, wf_ref, bf_ref, ktf_ref,
                       rb_ref, vb_ref, kkb_ref, wb_ref, bb_ref, ktb_ref, yf_ref, yb_ref, st_ref):
    i = pl.program_id(1)
    c = RWKV_CHUNK
    n_chunks = rf_ref.shape[1] // c

    @pl.when(i == 0)
    def _():
        st_ref[...] = jnp.zeros_like(st_ref)

    rr = lax.broadcasted_iota(jnp.int32, (c, c), 0)
    ss = lax.broadcasted_iota(jnp.int32, (c, c), 1)
    eye = rr == ss
    ones_cc = jnp.ones((c, c), F32)
    incl = (rr >= ss, rr <= ss)
    strict = (rr > ss, rr < ss)
    levels = []
    blk = 1
    while blk < c:
        levels.append(jnp.logical_and(rr // (2 * blk) == ss // (2 * blk), rr // blk != ss // blk))
        blk *= 2
    bdot = lambda x, y: jnp.dot(x.astype(BF16), y.astype(BF16), preferred_element_type=F32)
    bdot_nt = lambda x, y: lax.dot_general(x.astype(BF16), y.astype(BF16), _NT, preferred_element_type=F32)
    bdot_tn = lambda x, y: lax.dot_general(x.astype(BF16), y.astype(BF16), (((0,), (0,)), ((), ())),
                                           preferred_element_type=F32)
    dirs = ((rf_ref, vf_ref, kkf_ref, wf_ref, bf_ref, ktf_ref, yf_ref),
            (rb_ref, vb_ref, kkb_ref, wb_ref, bb_ref, ktb_ref, yb_ref))
    heads = [(d, h) for d in range(2) for h in range(GROUP_HEADS)]
    row0 = lambda cc, d: (cc if d == 0 else n_chunks - 1 - cc) * c
    rows = lambda cc, d: slice(row0(cc, d), row0(cc, d) + c)

    logw = {(cc, d): dirs[d][3][0, 0, rows(cc, d), :] for cc in range(n_chunks) for d in range(2)}
    cum = {key: _dot_f32(jnp.where(incl[key[1]], 1.0, 0.0), lw) for key, lw in logw.items()}
    tot = {key: _dot_f32(ones_cc, lw) for key, lw in logw.items()}
    st = {}
    for cc in range(n_chunks):
        for d, h in heads:
            cols = slice(h * HEAD_DIM, (h + 1) * HEAD_DIM)
            r_ref, v_ref, kk_ref, _, b_ref, kt_ref, _ = dirs[d]
            lc = cum[cc, d][:, cols]
            grow = jnp.exp(-lc)
            st[cc, d, h] = dict(
                v=v_ref[0, rows(cc, d), cols],
                kap=kk_ref[0, rows(cc, d), cols] * jnp.exp(lc - logw[cc, d][:, cols]),
                bh=b_ref[0, 0, rows(cc, d), cols] * grow,
                kh=kt_ref[0, 0, rows(cc, d), cols] * grow,
                rh=r_ref[0, rows(cc, d), cols] * jnp.exp(lc),
                scale=jnp.exp(_dot_f32(jnp.where(eye, tot[cc, d][:, cols], 0.0), ones_cc)))
    for (cc, d, h), x in st.items():
        gram = bdot_nt(jnp.concatenate([x["kap"], x["rh"]], axis=0), jnp.concatenate([x["bh"], x["kh"]], axis=0))
        x["a_b"] = jnp.where(strict[d], gram[:c, :c], 0.0)
        x["a_k"] = jnp.where(strict[d], gram[:c, c:], 0.0)
        x["b_b"] = jnp.where(incl[d], gram[c:, :c], 0.0)
        x["b_k"] = jnp.where(incl[d], gram[c:, c:], 0.0)
        x["inv"] = jnp.where(eye, 1.0, 0.0) - jnp.where(levels[0], x["a_b"], 0.0)
    for lvl in levels[1:]:
        for x in st.values():
            x["t"] = bdot(jnp.where(lvl, x["a_b"], 0.0), x["inv"])
        for x in st.values():
            x["inv"] = x["inv"] - bdot(x["inv"], x["t"])
    for x in st.values():
        x["akv"] = bdot(x["a_k"], x["v"])
        x["y_const"] = bdot(x["b_k"], x["v"])
        x["kv"] = bdot_tn(x["kh"], x["v"])
    for x in st.values():
        both = bdot(x["inv"], jnp.concatenate([x["kap"], x["akv"]], axis=-1))
        x["p_state"], x["p_const"] = both[:, :HEAD_DIM], both[:, HEAD_DIM:]
    state = {key: st_ref[key[0], key[1]] for key in heads}
    for cc in range(n_chunks):
        cur = [(key, st[(cc,) + key]) for key in heads]
        for key, x in cur:
            x["p"] = bdot(x["p_state"], state[key]) + x["p_const"]
            x["y0"] = bdot(x["rh"], state[key]) + x["y_const"]
        for key, x in cur:
            x["y"] = x["y0"] - bdot(x["b_b"], x["p"])
            state[key] = (state[key] + x["kv"] - bdot_tn(x["bh"], x["p"])) * x["scale"]
        for d in range(2):
            dirs[d][6][0, rows(cc, d), :] = jnp.concatenate([x["y"] for (dd, h), x in cur if dd == d], axis=-1)
    for d, h in heads:
        st_ref[d, h] = state[d, h]


def _rwkv_post_rows(y, bonus, g, ln_g, ln_b):
    head_mean = _segment_ones(GW, HEAD_DIM, F32) * (1.0 / HEAD_DIM)
    yc = y - _dot_f32(y, head_mean)
    var = _dot_f32(yc * yc, head_mean)
    return (yc * lax.rsqrt(var + RWKV_GN_EPS) * ln_g + ln_b + bonus) * g


def _rwkv_post_kernel(yf_ref, yb_ref, bonus_ref, g_ref, lng_ref, lnb_ref, o_ref):
    o_ref[0] = _rwkv_post_rows(yf_ref[0] + yb_ref[0], bonus_ref[0], g_ref[0], lng_ref[...], lnb_ref[...])


def _block_diag2(m):
    z = jnp.zeros_like(m[0])
    return jnp.concatenate([jnp.concatenate([m[0], z], 1), jnp.concatenate([z, m[1]], 1)], 0)


def _rwkv7_pallas(f, mu, w0, w2, a0, a2, g2, k_k, k_a, r_k, ln_g, ln_b, *, ctx_len, apply_post=True):
    bsz, seq, cols = f.shape
    tr = ROW_BLOCK
    assert ctx_len % tr == 0 and seq % tr == 0
    nb, ctx_blocks = seq // tr, ctx_len // tr
    prev_spec, next_spec = _halo_specs(cols, tr)
    row2 = lambda a: a.reshape(1, -1).astype(F32)
    full = lambda a: pl.BlockSpec(a.shape, lambda b, i: (0,) * a.ndim)
    params = [mu, row2(w0), _block_diag2(w2), row2(a0), _block_diag2(a2), g2, row2(k_k), row2(k_a), row2(r_k)]
    act = jax.ShapeDtypeStruct((bsz, seq, GW), F32)
    act2 = jax.ShapeDtypeStruct((2, bsz, seq, GW), F32)
    blk = pl.BlockSpec((1, tr, GW), lambda b, i: (b, i, 0))
    blk2 = pl.BlockSpec((2, 1, tr, GW), lambda b, i: (0, b, i, 0))
    r, v, kkn, g, bonus, w, bb, kt = pl.pallas_call(
        functools.partial(_rwkv_prep_kernel, ctx_blocks=ctx_blocks),
        name="rwkv_prep",
        grid=(bsz, nb),
        in_specs=[pl.BlockSpec((1, tr, cols), lambda b, i: (b, i, 0)), prev_spec, next_spec(seq // 8)]
                 + [full(p) for p in params],
        out_specs=[blk] * 5 + [blk2] * 3,
        out_shape=[act] * 5 + [act2] * 3,
        compiler_params=pltpu.CompilerParams(dimension_semantics=("arbitrary", "arbitrary"),
                                             vmem_limit_bytes=V7X_VMEM_LIMIT_BYTES),
    )(f, f, f, *params)

    def bwd_block(c):
        return jnp.where(c < ctx_blocks, ctx_blocks - 1 - c, nb - 1 - (c - ctx_blocks))
    assert tr % RWKV_CHUNK == 0 and RWKV_CHUNK == HEAD_DIM
    fwd = pl.BlockSpec((1, tr, GW), lambda b, c: (b, c, 0))
    bwd = pl.BlockSpec((1, tr, GW), lambda b, c: (b, bwd_block(c), 0))
    fwd_d = pl.BlockSpec((1, 1, tr, GW), lambda b, c: (0, b, c, 0))
    bwd_d = pl.BlockSpec((1, 1, tr, GW), lambda b, c: (1, b, bwd_block(c), 0))
    yf, yb = pl.pallas_call(
        _rwkv_chunk_kernel,
        name="rwkv_chunks",
        grid=(bsz, nb),
        in_specs=[fwd, fwd, fwd, fwd_d, fwd_d, fwd_d, bwd, bwd, bwd, bwd_d, bwd_d, bwd_d],
        out_specs=[fwd, bwd],
        out_shape=[act, act],
        scratch_shapes=[pltpu.VMEM((2, GROUP_HEADS, HEAD_DIM, HEAD_DIM), F32)],
        compiler_params=pltpu.CompilerParams(dimension_semantics=("arbitrary", "arbitrary"),
                                             vmem_limit_bytes=V7X_VMEM_LIMIT_BYTES),
    )(r, v, kkn, w, bb, kt, r, v, kkn, w, bb, kt)
    if not apply_post:
        return yf, yb, bonus, g, row2(ln_g), row2(ln_b)

    return pl.pallas_call(
        _rwkv_post_kernel,
        name="rwkv_post",
        grid=(bsz, nb),
        in_specs=[blk, blk, blk, blk, full(row2(ln_g)), full(row2(ln_b))],
        out_specs=blk,
        out_shape=act,
        compiler_params=pltpu.CompilerParams(dimension_semantics=("arbitrary", "arbitrary")),
    )(yf, yb, bonus, g, row2(ln_g), row2(ln_b))


GDN_GATE_LANES = 128
GDN_PADDED_COLS = 4 * GW + GDN_GATE_LANES


def _gdn_prep_kernel(x_ref, xprev_ref, xnext_ref, ab_ref, conv_ref, alog_ref, dtb_ref,
                     q_out, k_out, v_out, gb_out, *, ctx_blocks):
    i = pl.program_id(1)
    keep_prev, keep_next = _segment_edge_flags(i, pl.num_programs(1), ctx_blocks)
    x = x_ref[0]
    xm, xp = _shifted_rows(x, xprev_ref[0, 7:8, :] * keep_prev, xnext_ref[0, 0:1, :] * keep_next)
    y = xm * conv_ref[0:1, :] + x * conv_ref[1:2, :] + xp * conv_ref[2:3, :]
    y = y * jax.nn.sigmoid(y)
    head_sum = _segment_ones(GW, HEAD_DIM, F32)
    q = y[:, 0:GW]
    k = y[:, GW:2 * GW]
    q_out[0] = q * lax.rsqrt(_dot_f32(q * q, head_sum) + 1e-6) * (HEAD_DIM ** -0.5)
    k_out[0] = k * lax.rsqrt(_dot_f32(k * k, head_sum) + 1e-6)
    v_out[0] = y[:, 2 * GW:3 * GW]
    ab = ab_ref[0]
    lane = lax.broadcasted_iota(jnp.int32, ab.shape, 1)
    log_alpha = -jnp.exp(alog_ref[...]) * _softplus(ab + dtb_ref[...])
    gb_out[0] = jnp.where(lane < 2 * GROUP_HEADS, log_alpha, jax.nn.sigmoid(ab))


def _gdn_chunk_kernel(qf_ref, kf_ref, vf_ref, gf_ref, qb_ref, kb_ref, vb_ref, gb_ref, of_ref, ob_ref, st_ref):
    i = pl.program_id(1)
    c = GDN_CHUNK
    n_chunks = qf_ref.shape[1] // c

    @pl.when(i == 0)
    def _():
        st_ref[...] = jnp.zeros_like(st_ref)

    r = lax.broadcasted_iota(jnp.int32, (c, c), 0)
    s = lax.broadcasted_iota(jnp.int32, (c, c), 1)
    eye = r == s
    ones_cc = jnp.ones((c, c), F32)
    incl = (r >= s, r <= s)
    strict = (r > s, r < s)
    levels = []
    b = 1
    while b < c:
        levels.append(jnp.logical_and(r // (2 * b) == s // (2 * b), r // b != s // b))
        b *= 2
    dirs = ((qf_ref, kf_ref, vf_ref, gf_ref, of_ref), (qb_ref, kb_ref, vb_ref, gb_ref, ob_ref))

    bdot = lambda x, y: jnp.dot(x.astype(BF16), y.astype(BF16), preferred_element_type=F32)
    bdot_nt = lambda x, y: lax.dot_general(x.astype(BF16), y.astype(BF16), _NT, preferred_element_type=F32)
    bdot_tn = lambda x, y: lax.dot_general(x.astype(BF16), y.astype(BF16), (((0,), (0,)), ((), ())),
                                           preferred_element_type=F32)

    heads = [(d, h) for d in range(2) for h in range(GROUP_HEADS)]
    row0 = lambda cc, d: (cc if d == 0 else n_chunks - 1 - cc) * c
    gates = {(cc, d): dirs[d][3][0, row0(cc, d):row0(cc, d) + c, :] for cc in range(n_chunks) for d in range(2)}
    cum = {key: _dot_f32(jnp.where(incl[key[1]], 1.0, 0.0), g) for key, g in gates.items()}
    tot = {key: _dot_f32(ones_cc, g) for key, g in gates.items()}
    chains = [(cc, d, h) for cc in range(n_chunks) for d, h in heads]
    st = {}
    for cc, d, h in chains:
        cols = slice(h * HEAD_DIM, (h + 1) * HEAD_DIM)
        lg = d * GROUP_HEADS + h
        q, k, v = (dirs[d][n][0, row0(cc, d):row0(cc, d) + c, cols] for n in range(3))
        gc = cum[cc, d][:, lg:lg + 1]
        st[cc, d, h] = dict(q=q, k=k, v=v, gc=gc, gt=tot[cc, d][:, lg:lg + 1],
                            beta=gates[cc, d][:, 2 * GROUP_HEADS + lg:2 * GROUP_HEADS + lg + 1],
                            gc_row=_dot_f32(ones_cc, jnp.where(eye, gc, 0.0)))
    for (cc, d, h), x in st.items():
        x["decay"] = jnp.exp(jnp.where(incl[d], x["gc"] - x["gc_row"], NEG_BIG))
        x["kb"] = x["k"] * x["beta"]
        gram = bdot_nt(jnp.concatenate([x["kb"], x["q"]], axis=0), x["k"])
        x["a"] = jnp.where(strict[d], gram[:c] * x["decay"], 0.0)
        x["qk"] = jnp.where(incl[d], gram[c:] * x["decay"], 0.0)
        x["inv"] = jnp.where(eye, 1.0, 0.0) - jnp.where(levels[0], x["a"], 0.0)
    for lvl in levels[1:]:
        for x in st.values():
            x["t"] = bdot(jnp.where(lvl, x["a"], 0.0), x["inv"])
        for x in st.values():
            x["inv"] = x["inv"] - bdot(x["inv"], x["t"])
    for x in st.values():
        x["eg"] = jnp.exp(x["gc"])
        x["sol"] = bdot(x["inv"], jnp.concatenate([x["v"] * x["beta"], x["kb"] * x["eg"]], axis=-1))
        x["qg"] = x["q"] * x["eg"]
        x["kg"] = x["k"] * jnp.exp(x["gt"] - x["gc"])
    state = {(d, h): st_ref[d, h] for d, h in heads}
    for cc in range(n_chunks):
        cur = [(key, st[(cc,) + key]) for key in heads]
        for key, x in cur:
            x["ws"] = bdot(x["sol"][:, HEAD_DIM:], state[key])
            x["qs"] = bdot(x["qg"], state[key])
        for key, x in cur:
            x["v_new"] = x["sol"][:, :HEAD_DIM] - x["ws"]
            x["o"] = x["qs"] + bdot(x["qk"], x["v_new"])
            x["upd"] = bdot_tn(x["kg"], x["v_new"])
        for key, x in cur:
            state[key] = state[key] * jnp.exp(x["gt"][0:1, :]) + x["upd"]
        for d in range(2):
            dirs[d][4][0, row0(cc, d):row0(cc, d) + c, :] = jnp.concatenate(
                [x["o"] for (dd, h), x in cur if dd == d], axis=-1)
    for d, h in heads:
        st_ref[d, h] = state[d, h]


def _gdn_post_rows(o, gate, g):
    head_mean = _segment_ones(GW, HEAD_DIM, F32) * (1.0 / HEAD_DIM)
    return o * lax.rsqrt(_dot_f32(o * o, head_mean) + 1e-6) * g * (gate * jax.nn.sigmoid(gate))


def _gdn_post_kernel(of_ref, ob_ref, gate_ref, g_ref, o_ref):
    o_ref[0] = _gdn_post_rows(of_ref[0] + ob_ref[0], gate_ref[0], g_ref[...])


def _gated_deltanet_pallas(f, conv_w, a_log, dt_bias, norm_g, *, ctx_len, apply_post=True):
    bsz, seq, width = f.shape
    tr = ROW_BLOCK
    assert ctx_len % tr == 0 and seq % tr == 0 and tr % GDN_CHUNK == 0
    if width == GDN_COLS:
        f = jnp.pad(f, ((0, 0), (0, 0), (0, GDN_PADDED_COLS - GDN_COLS)))
    assert f.shape[2] == GDN_PADDED_COLS
    nb, ctx_blocks = seq // tr, ctx_len // tr
    lane_pad = lambda a: jnp.pad(a.reshape(1, -1).astype(F32), ((0, 0), (0, GDN_GATE_LANES - a.size)))
    prev_spec, next_spec = _halo_specs(3 * GW, tr)
    full = lambda a: pl.BlockSpec(a.shape, lambda b, i: (0,) * a.ndim)
    act = jax.ShapeDtypeStruct((bsz, seq, GW), F32)
    gact = jax.ShapeDtypeStruct((bsz, seq, GDN_GATE_LANES), F32)
    blk = pl.BlockSpec((1, tr, GW), lambda b, i: (b, i, 0))
    gblk = pl.BlockSpec((1, tr, GDN_GATE_LANES), lambda b, i: (b, i, 0))
    gate_view = pl.BlockSpec((1, tr, GW), lambda b, i: (b, i, 3))
    ab_view = pl.BlockSpec((1, tr, GDN_GATE_LANES), lambda b, i: (b, i, 4 * GW // GDN_GATE_LANES))
    params = [conv_w.astype(F32), lane_pad(a_log), lane_pad(dt_bias)]
    q, k, v, gb = pl.pallas_call(
        functools.partial(_gdn_prep_kernel, ctx_blocks=ctx_blocks),
        name="gdn_prep",
        grid=(bsz, nb),
        in_specs=[pl.BlockSpec((1, tr, 3 * GW), lambda b, i: (b, i, 0)), prev_spec, next_spec(seq // 8), ab_view]
                 + [full(p) for p in params],
        out_specs=[blk, blk, blk, gblk],
        out_shape=[act, act, act, gact],
        compiler_params=pltpu.CompilerParams(dimension_semantics=("arbitrary", "arbitrary"),
                                             vmem_limit_bytes=V7X_VMEM_LIMIT_BYTES),
    )(f, f, f, f, *params)

    def bwd_block(i):
        return jnp.where(i < ctx_blocks, ctx_blocks - 1 - i, nb - 1 - (i - ctx_blocks))
    bblk = pl.BlockSpec((1, tr, GW), lambda b, i: (b, bwd_block(i), 0))
    bgblk = pl.BlockSpec((1, tr, GDN_GATE_LANES), lambda b, i: (b, bwd_block(i), 0))
    of, ob = pl.pallas_call(
        _gdn_chunk_kernel,
        name="gdn_chunks",
        grid=(bsz, nb),
        in_specs=[blk, blk, blk, gblk, bblk, bblk, bblk, bgblk],
        out_specs=[blk, bblk],
        out_shape=[act, act],
        scratch_shapes=[pltpu.VMEM((2, GROUP_HEADS, HEAD_DIM, HEAD_DIM), F32)],
        compiler_params=pltpu.CompilerParams(dimension_semantics=("arbitrary", "arbitrary"),
                                             vmem_limit_bytes=V7X_VMEM_LIMIT_BYTES),
    )(q, k, v, gb, q, k, v, gb)

    g_row = jnp.tile(norm_g.reshape(1, HEAD_DIM).astype(F32), (1, GROUP_HEADS))
    if not apply_post:
        return of, ob, f, g_row
    return pl.pallas_call(
        _gdn_post_kernel,
        name="gdn_post",
        grid=(bsz, nb),
        in_specs=[blk, blk, gate_view, full(g_row)],
        out_specs=blk,
        out_shape=act,
        compiler_params=pltpu.CompilerParams(dimension_semantics=("arbitrary", "arbitrary")),
    )(of, ob, f, g_row)


ROPE_PAIR = DIFF_HALF // 4


def _rope_tables(seq, ctx_len, q_scale):
    n = jnp.arange(seq - ctx_len, dtype=jnp.int32)
    row, col = n // GRID_W, n % GRID_W
    i = jnp.arange(HEAD_DIM)
    grp = (i % DIFF_HALF) // (2 * ROPE_PAIR)
    inv = ROPE_BASE ** (-(i % ROPE_PAIR).astype(F32) / ROPE_PAIR)
    pos = jnp.where(grp[None, :] == 0, row[:, None], col[:, None]).astype(F32)
    ang = pos * inv[None, :]
    sign = jnp.where((i % (2 * ROPE_PAIR)) < ROPE_PAIR, -1.0, 1.0)
    cos = jnp.concatenate([jnp.ones((ctx_len, HEAD_DIM), F32), jnp.cos(ang)], 0)
    sin = jnp.concatenate([jnp.zeros((ctx_len, HEAD_DIM), F32), jnp.sin(ang) * sign], 0)
    cos = jnp.tile(cos, (1, GROUP_HEADS))
    sin = jnp.tile(sin, (1, GROUP_HEADS))
    return jnp.concatenate([cos * q_scale, cos], 1), jnp.concatenate([sin * q_scale, sin], 1)


def _qkv_prep_kernel(p_ref, cos_ref, sin_ref, q_out, k_out, v_out):
    qk = p_ref[0, :, 0:2 * GW]
    width = 2 * GW
    lane = lax.broadcasted_iota(jnp.int32, qk.shape, 1)
    partner = jnp.where(lane % (2 * ROPE_PAIR) < ROPE_PAIR,
                        pltpu.roll(qk, width - ROPE_PAIR, axis=1), pltpu.roll(qk, ROPE_PAIR, axis=1))
    rot = qk * cos_ref[...] + partner * sin_ref[...]
    q_out[0] = rot[:, 0:GW].astype(BF16)
    v = p_ref[0, :, 2 * GW:3 * GW]
    for h in range(GROUP_HEADS):
        k_out[0, h] = rot[:, GW + h * HEAD_DIM:GW + (h + 1) * HEAD_DIM].astype(BF16)
        v_out[0, h] = v[:, h * HEAD_DIM:(h + 1) * HEAD_DIM].astype(BF16)


def _qkv_prep(p, cos, sin):
    bsz, seq, _ = p.shape
    tr = ROW_BLOCK
    head_major = jax.ShapeDtypeStruct((bsz, GROUP_HEADS, seq, HEAD_DIM), BF16)
    hm_spec = pl.BlockSpec((1, GROUP_HEADS, tr, HEAD_DIM), lambda b, i: (b, 0, i, 0))
    return pl.pallas_call(
        _qkv_prep_kernel,
        name="qkv_prep",
        grid=(bsz, seq // tr),
        in_specs=[pl.BlockSpec((1, tr, 3 * GW), lambda b, i: (b, i, 0)),
                  pl.BlockSpec((tr, 2 * GW), lambda b, i: (i, 0)),
                  pl.BlockSpec((tr, 2 * GW), lambda b, i: (i, 0))],
        out_specs=[pl.BlockSpec((1, tr, GW), lambda b, i: (b, i, 0)), hm_spec, hm_spec],
        out_shape=[jax.ShapeDtypeStruct((bsz, seq, GW), BF16), head_major, head_major],
        compiler_params=pltpu.CompilerParams(dimension_semantics=("arbitrary", "arbitrary")),
    )(p, cos, sin)


_NT = (((1,), (1,)), ((), ()))


def _softmax_pv(s, v):
    m = jnp.max(s, axis=-1, keepdims=True)
    e = jnp.exp(s - m)
    return jnp.dot(e.astype(BF16), v, preferred_element_type=F32) / jnp.sum(e, axis=-1, keepdims=True)


def _diff_attn_kernel(q_ref, k_ref, v_ref, lam_ref, g_ref, o_ref, *, ctx_blocks, ctx_len):
    i = pl.program_id(1)
    lv = lam_ref[...]
    lam_init = lv[4:5, 0:1]
    lam = (jnp.exp(jnp.sum(lv[0:1] * lv[1:2], axis=-1, keepdims=True))
           - jnp.exp(jnp.sum(lv[2:3] * lv[3:4], axis=-1, keepdims=True)) + lam_init)
    lane = lax.broadcasted_iota(jnp.int32, (q_ref.shape[1], HEAD_DIM), 1)

    def attend(n_keys):
        outs = []
        for h in range(GROUP_HEADS):
            qh = q_ref[0, :, h * HEAD_DIM:(h + 1) * HEAD_DIM]
            kh = k_ref[0, h, 0:n_keys, :]
            vh = v_ref[0, h, 0:n_keys, :]
            zero = jnp.zeros_like(qh)
            s1 = lax.dot_general(jnp.where(lane < DIFF_HALF, qh, zero), kh, _NT, preferred_element_type=F32)
            s2 = lax.dot_general(jnp.where(lane >= DIFF_HALF, qh, zero), kh, _NT, preferred_element_type=F32)
            o = _softmax_pv(s1, vh) - lam * _softmax_pv(s2, vh)
            o = o * lax.rsqrt(jnp.mean(o * o, axis=-1, keepdims=True) + 1e-6) * g_ref[...] * (1.0 - lam_init)
            outs.append(o)
        o_ref[0] = jnp.concatenate(outs, axis=-1)

    @pl.when(i < ctx_blocks)
    def _():
        attend(ctx_len)

    @pl.when(i >= ctx_blocks)
    def _():
        attend(k_ref.shape[2])


def _diff_attention_pallas(q, k, v, lam_vecs, norm_g, *, ctx_len, lam_init):
    bsz, seq, _ = q.shape
    tq = ROW_BLOCK
    kv_spec = pl.BlockSpec((1, GROUP_HEADS, seq, HEAD_DIM), lambda b, i: (b, 0, 0, 0))
    lam_rows = jnp.concatenate([lam_vecs.astype(F32), jnp.full((1, lam_vecs.shape[1]), lam_init, F32)], 0)
    return pl.pallas_call(
        functools.partial(_diff_attn_kernel, ctx_blocks=ctx_len // tq, ctx_len=ctx_len),
        name="diff_attn",
        grid=(bsz, seq // tq),
        in_specs=[pl.BlockSpec((1, tq, GW), lambda b, i: (b, i, 0)), kv_spec, kv_spec,
                  pl.BlockSpec(lam_rows.shape, lambda b, i: (0, 0)),
                  pl.BlockSpec((1, HEAD_DIM), lambda b, i: (0, 0))],
        out_specs=pl.BlockSpec((1, tq, GW), lambda b, i: (b, i, 0)),
        out_shape=jax.ShapeDtypeStruct((bsz, seq, GW), F32),
        compiler_params=pltpu.CompilerParams(dimension_semantics=("arbitrary", "arbitrary"),
                                             vmem_limit_bytes=V7X_VMEM_LIMIT_BYTES),
    )(q, k, v, lam_rows, norm_g.reshape(1, HEAD_DIM).astype(F32))


NAT_TILE_ROWS = ROW_BLOCK // GRID_W
NAT_SLAB_ROWS = NAT_TILE_ROWS + WIN_H - 1


def _nat_slab_start(tile, n_rows):
    return np.clip(tile * NAT_TILE_ROWS - WIN_H // 2, 0, n_rows - NAT_SLAB_ROWS)


def _nat_bias_tables(rpb, n_rows):
    n_tiles = n_rows // NAT_TILE_ROWS
    nq, nk, w = NAT_TILE_ROWS, NAT_SLAB_ROWS, GRID_W
    cq, ck = np.arange(w)[:, None], np.arange(w)[None, :]
    d_col = np.clip(ck - cq, -(WIN_W - 1), WIN_W - 1) + WIN_W - 1
    col_1h = (d_col.reshape(-1)[:, None] == np.arange(2 * WIN_W - 1)[None, :]).astype(np.float32)
    c0 = np.clip(cq - WIN_W // 2, 0, w - WIN_W)
    col_ok = (ck >= c0) & (ck < c0 + WIN_W)
    tabs = []
    for tile in (0, 1, n_tiles - 1):
        r = tile * nq + np.arange(nq)[:, None]
        kr = _nat_slab_start(tile, n_rows) + np.arange(nk)[None, :]
        rs = np.clip(r - WIN_H // 2, 0, n_rows - WIN_H)
        row_ok = (kr >= rs) & (kr < rs + WIN_H)
        d_row = np.clip(kr - r + WIN_H - 1, 0, 2 * WIN_H - 2)
        row_1h = (d_row.reshape(-1)[:, None] == np.arange(2 * WIN_H - 1)[None, :]).astype(np.float32)
        t = jnp.einsum('pa,hab,cb->hpc', row_1h, rpb.astype(F32), col_1h, precision=HIGHEST)
        t = t.reshape(GROUP_HEADS, nq, nk, w, w).transpose(0, 1, 3, 2, 4)
        ok = row_ok[:, None, :, None] & col_ok[None, :, None, :]
        tabs.append(jnp.where(ok[None], t, NEG_BIG).reshape(GROUP_HEADS, nq * w, nk * w))
    return jnp.stack(tabs)


def _nat_attn_kernel(q_ref, k_ref, v_ref, bias_ref, o_ref, *, ctx_blocks, ctx_len, n_rows):
    i = pl.program_id(1)
    n_slab = NAT_SLAB_ROWS * GRID_W

    def heads(fn):
        o_ref[0] = jnp.concatenate(
            [fn(h, q_ref[0, :, h * HEAD_DIM:(h + 1) * HEAD_DIM]) for h in range(GROUP_HEADS)], axis=-1)

    @pl.when(i < ctx_blocks)
    def _():
        def ctx_only(h, qh):
            s = lax.dot_general(qh, k_ref[0, h, 0:ctx_len, :], _NT, preferred_element_type=F32)
            return _softmax_pv(s, v_ref[0, h, 0:ctx_len, :])
        heads(ctx_only)

    @pl.when(i >= ctx_blocks)
    def _():
        tile = i - ctx_blocks
        start = jnp.clip(tile * NAT_TILE_ROWS - WIN_H // 2, 0, n_rows - NAT_SLAB_ROWS)
        off = pl.multiple_of(ctx_len + start * GRID_W, GRID_W)

        def windowed(h, qh):
            s_w = lax.dot_general(qh, k_ref[0, h, pl.ds(off, n_slab), :], _NT,
                                  preferred_element_type=F32) + bias_ref[0, h]
            s_c = lax.dot_general(qh, k_ref[0, h, 0:ctx_len, :], _NT, preferred_element_type=F32)
            m = jnp.maximum(jnp.max(s_w, axis=-1, keepdims=True), jnp.max(s_c, axis=-1, keepdims=True))
            e_w = jnp.exp(s_w - m)
            e_c = jnp.exp(s_c - m)
            den = jnp.sum(e_w, axis=-1, keepdims=True) + jnp.sum(e_c, axis=-1, keepdims=True)
            num = (jnp.dot(e_w.astype(BF16), v_ref[0, h, pl.ds(off, n_slab), :], preferred_element_type=F32)
                   + jnp.dot(e_c.astype(BF16), v_ref[0, h, 0:ctx_len, :], preferred_element_type=F32))
            return num / den
        heads(windowed)


def _nat_attention_pallas(q, k, v, rpb, *, ctx_len):
    bsz, seq, _ = q.shape
    tq = ROW_BLOCK
    ctx_blocks = ctx_len // tq
    n_rows = (seq - ctx_len) // GRID_W
    n_tiles = n_rows // NAT_TILE_ROWS
    assert n_rows >= NAT_SLAB_ROWS and n_tiles >= 3
    bias = _nat_bias_tables(rpb, n_rows)

    def variant(i):
        tile = i - ctx_blocks
        return jnp.where(tile <= 0, 0, jnp.where(tile >= n_tiles - 1, 2, 1))
    kv_spec = pl.BlockSpec((1, GROUP_HEADS, seq, HEAD_DIM), lambda b, i: (b, 0, 0, 0))
    return pl.pallas_call(
        functools.partial(_nat_attn_kernel, ctx_blocks=ctx_blocks, ctx_len=ctx_len, n_rows=n_rows),
        name="nat_attn",
        grid=(bsz, seq // tq),
        in_specs=[pl.BlockSpec((1, tq, GW), lambda b, i: (b, i, 0)), kv_spec, kv_spec,
                  pl.BlockSpec((1,) + bias.shape[1:], lambda b, i: (variant(i), 0, 0, 0))],
        out_specs=pl.BlockSpec((1, tq, GW), lambda b, i: (b, i, 0)),
        out_shape=jax.ShapeDtypeStruct((bsz, seq, GW), F32),
        compiler_params=pltpu.CompilerParams(dimension_semantics=("arbitrary", "arbitrary"),
                                             vmem_limit_bytes=V7X_VMEM_LIMIT_BYTES),
    )(q, k, v, bias)


N_MOD = 6
MATMUL_ROWS = 512


def _ada_kernel(c_ref, w_ref, b_ref, o_ref):
    c = c_ref[...]
    o_ref[...] = _dot_f32(c * jax.nn.sigmoid(c), w_ref[...]) + b_ref[...]


def _ada_modulation(c, c_ctx, w_ada, b_ada):
    bsz, d = c.shape
    rows = 8 * ((bsz + 1 + 7) // 8)
    cc = jnp.zeros((rows, d), F32).at[:bsz].set(c).at[bsz].set(c_ctx)
    tn = d
    m = pl.pallas_call(
        _ada_kernel,
        name="ada_modulation",
        grid=(w_ada.shape[1] // tn,),
        in_specs=[pl.BlockSpec((rows, d), lambda j: (0, 0)),
                  pl.BlockSpec((d, tn), lambda j: (0, j)),
                  pl.BlockSpec((1, tn), lambda j: (0, j))],
        out_specs=pl.BlockSpec((rows, tn), lambda j: (0, j)),
        out_shape=jax.ShapeDtypeStruct((rows, w_ada.shape[1]), F32),
        compiler_params=pltpu.CompilerParams(dimension_semantics=("arbitrary",)),
    )(cc, w_ada, b_ada.reshape(1, -1))
    lat = m[:bsz].reshape(bsz, 1, N_MOD, d)
    ctx = jnp.broadcast_to(m[bsz].reshape(1, 1, N_MOD, d), (bsz, 1, N_MOD, d))
    return jnp.concatenate([ctx, lat], axis=1)


def _ln(x):
    mu = jnp.mean(x, axis=-1, keepdims=True)
    xc = x - mu
    return xc * lax.rsqrt(jnp.mean(xc * xc, axis=-1, keepdims=True) + LN_EPS)


def _modulate_kernel(h_ref, mod_ref, o_ref, *, shift_row):
    shift = mod_ref[0, 0, shift_row:shift_row + 1, :]
    scale = mod_ref[0, 0, shift_row + 1:shift_row + 2, :]
    o_ref[0] = (_ln(h_ref[0]) * (1.0 + scale) + shift).astype(o_ref.dtype)


def _mod_spec(d, ctx_blocks):
    return pl.BlockSpec((1, 1, N_MOD, d), lambda b, i: (b, jnp.where(i < ctx_blocks, 0, 1), 0, 0))


def _modulate_pallas(hs, mod, shift_row, *, ctx_len):
    bsz, seq, d = hs.shape
    tr = ROW_BLOCK
    blk = pl.BlockSpec((1, tr, d), lambda b, i: (b, i, 0))
    return pl.pallas_call(
        functools.partial(_modulate_kernel, shift_row=shift_row),
        name="modulate",
        grid=(bsz, seq // tr),
        in_specs=[blk, _mod_spec(d, ctx_len // tr)],
        out_specs=blk,
        out_shape=jax.ShapeDtypeStruct((bsz, seq, d), BF16),
        compiler_params=pltpu.CompilerParams(dimension_semantics=("arbitrary", "arbitrary")),
    )(hs, mod)


def _in_proj_kernel(x_ref, *refs):
    n = len(refs) // 2
    x = x_ref[...]
    for w_ref, o_ref in zip(refs[:n], refs[n:]):
        o_ref[...] = jnp.dot(x, w_ref[...], preferred_element_type=F32)


def _in_proj_pallas(x, ws):
    m, k = x.shape
    tm = MATMUL_ROWS
    return pl.pallas_call(
        _in_proj_kernel,
        name="in_proj",
        grid=(m // tm,),
        in_specs=[pl.BlockSpec((tm, k), lambda i: (i, 0))] + [pl.BlockSpec(w.shape, lambda i: (0, 0)) for w in ws],
        out_specs=[pl.BlockSpec((tm, w.shape[1]), lambda i: (i, 0)) for w in ws],
        out_shape=[jax.ShapeDtypeStruct((m, w.shape[1]), F32) for w in ws],
        compiler_params=pltpu.CompilerParams(dimension_semantics=("arbitrary",),
                                             vmem_limit_bytes=V7X_VMEM_LIMIT_BYTES),
    )(x, *ws)


def _post_norm_rows(h, gate, y, g, b):
    return _ln(DN_ALPHA * h + gate * y) * g + b


def _out_proj_kernel(ya_ref, rf_ref, rb_ref, bonus_ref, rg_ref, lng_ref, lnb_ref, of_ref, ob_ref, gate_ref, gn_ref,
                     yd_ref, w_ref, h_ref, mod_ref, g_ref, b_ref, o_ref):
    yb = _rwkv_post_rows(rf_ref[0] + rb_ref[0], bonus_ref[0], rg_ref[0], lng_ref[...], lnb_ref[...])
    yc = _gdn_post_rows(of_ref[0] + ob_ref[0], gate_ref[0], gn_ref[...])
    mix = None
    for n, y in enumerate((ya_ref[0], yb, yc, yd_ref[0])):
        part = jnp.dot(y.astype(BF16), w_ref[n * GW:(n + 1) * GW, :], preferred_element_type=F32)
        mix = part if mix is None else mix + part
    o_ref[0] = _post_norm_rows(h_ref[0], mod_ref[0, 0, 2:3, :], mix, g_ref[...], b_ref[...])


def _out_proj_post_norm(ya, rwkv_parts, gdn_parts, yd, w_out, hs, mod, g, b, *, ctx_len):
    bsz, seq, d = hs.shape
    tr = ROW_BLOCK
    yblk = pl.BlockSpec((1, tr, GW), lambda bb, i: (bb, i, 0))
    blk = pl.BlockSpec((1, tr, d), lambda bb, i: (bb, i, 0))
    row = pl.BlockSpec((1, d), lambda bb, i: (0, 0))
    grow = pl.BlockSpec((1, GW), lambda bb, i: (0, 0))
    gate_view = pl.BlockSpec((1, tr, GW), lambda bb, i: (bb, i, 3))
    return pl.pallas_call(
        _out_proj_kernel,
        name="out_proj_post_norm",
        grid=(bsz, seq // tr),
        in_specs=[yblk] + [yblk] * 4 + [grow, grow] + [yblk, yblk, gate_view, grow] + [yblk]
                 + [pl.BlockSpec(w_out.shape, lambda bb, i: (0, 0)), blk, _mod_spec(d, ctx_len // tr), row, row],
        out_specs=blk,
        out_shape=jax.ShapeDtypeStruct((bsz, seq, d), F32),
        compiler_params=pltpu.CompilerParams(dimension_semantics=("arbitrary", "arbitrary")),
    )(ya, *rwkv_parts, *gdn_parts, yd, w_out, hs, mod, g.reshape(1, d), b.reshape(1, d))


def _ffn_post_norm_kernel(h_ref, y_ref, mod_ref, g_ref, b_ref, o_ref):
    o_ref[0] = _post_norm_rows(h_ref[0], mod_ref[0, 0, 5:6, :], y_ref[0], g_ref[...], b_ref[...])


def _ffn_post_norm(hs, y, mod, g, b, *, ctx_len):
    bsz, seq, d = hs.shape
    tr = ROW_BLOCK
    blk = pl.BlockSpec((1, tr, d), lambda bb, i: (bb, i, 0))
    row = pl.BlockSpec((1, d), lambda bb, i: (0, 0))
    return pl.pallas_call(
        _ffn_post_norm_kernel,
        name="ffn_post_norm",
        grid=(bsz, seq // tr),
        in_specs=[blk, blk, _mod_spec(d, ctx_len // tr), row, row],
        out_specs=blk,
        out_shape=jax.ShapeDtypeStruct((bsz, seq, d), F32),
        compiler_params=pltpu.CompilerParams(dimension_semantics=("arbitrary", "arbitrary")),
    )(hs, y, mod, g.reshape(1, d), b.reshape(1, d))


def kernel(x, c, ctx, c_ctx, w_ada, b_ada, w_in, w_out, ln_mix_g, ln_mix_b, ln_ffn_g, ln_ffn_b, diff_lam, diff_norm_g, rwkv_mu, rwkv_w0, rwkv_w2, rwkv_a0, rwkv_a2, rwkv_g2, rwkv_kk, rwkv_ka, rwkv_rk, rwkv_ln_g, rwkv_ln_b, gdn_conv, gdn_a_log, gdn_dt_bias, gdn_norm_g, nat_rpb, peer_wq, peer_keys, peer_u, peer_v):
    dtype = x.dtype
    bsz, ctx_len = ctx.shape[0], ctx.shape[1]
    hs = jnp.concatenate([ctx, x], axis=1)
    seq = hs.shape[1]
    col_sizes = [ATTN_COLS, RWKV_COLS, GDN_COLS, ATTN_COLS]
    cos_a, sin_a = _rope_tables(seq, ctx_len, DIFF_HALF ** -0.5)
    cos_d = jnp.concatenate([jnp.full((seq, GW), HEAD_DIM ** -0.5, F32), jnp.ones((seq, GW), F32)], 1)
    sin_d = jnp.zeros_like(cos_d)
    col_offs = np.cumsum([0] + col_sizes)
    d_model = hs.shape[2]
    for l in range(DEPTH):
        lam_init = 0.8 - 0.6 * math.exp(-0.3 * l)
        mod = _ada_modulation(c, c_ctx, w_ada[l], b_ada[l])
        u = _modulate_pallas(hs, mod, 0, ctx_len=ctx_len).reshape(bsz * seq, d_model)
        w_in_b = w_in[l].astype(BF16)
        w_groups = [w_in_b[:, col_offs[n]:col_offs[n + 1]] for n in range(4)]
        w_groups[2] = jnp.pad(w_groups[2], ((0, 0), (0, GDN_PADDED_COLS - GDN_COLS)))
        pa, pb, pc, pd = [p.reshape(bsz, seq, p.shape[1]) for p in _in_proj_pallas(u, w_groups)]
        qa, ka, va = _qkv_prep(pa, cos_a, sin_a)
        ya = _diff_attention_pallas(qa, ka, va, diff_lam[l], diff_norm_g[l], ctx_len=ctx_len, lam_init=lam_init)
        yb_parts = _rwkv7_pallas(pb, rwkv_mu[l], rwkv_w0[l], rwkv_w2[l], rwkv_a0[l], rwkv_a2[l], rwkv_g2[l],
                                 rwkv_kk[l], rwkv_ka[l], rwkv_rk[l], rwkv_ln_g[l], rwkv_ln_b[l], ctx_len=ctx_len,
                                 apply_post=False)
        yc_parts = _gated_deltanet_pallas(pc, gdn_conv[l], gdn_a_log[l], gdn_dt_bias[l], gdn_norm_g[l],
                                          ctx_len=ctx_len, apply_post=False)
        qd, kd, vd = _qkv_prep(pd, cos_d, sin_d)
        yd = _nat_attention_pallas(qd, kd, vd, nat_rpb[l], ctx_len=ctx_len)
        hs = _out_proj_post_norm(ya, yb_parts, yc_parts, yd, w_out[l].astype(BF16), hs, mod, ln_mix_g[l],
                                 ln_mix_b[l], ctx_len=ctx_len)
        wq_b = peer_wq[l].astype(BF16)
        keys_b = peer_keys[l].reshape(2 * PEER_HEADS, PEER_KEYS, PEER_HALF).astype(BF16)
        u_b = peer_u[l].astype(BF16)
        vt_b = peer_v[l].astype(BF16).T
        ffn_ctx = ctx_len
        if l == DEPTH - 1:
            hs, ffn_ctx = hs[:, ctx_len:], 0
        ffn = _peer(_modulate_pallas(hs, mod, 3, ctx_len=ffn_ctx), wq_b, keys_b, u_b, vt_b)
        hs = _ffn_post_norm(hs, ffn, mod, ln_ffn_g[l], ln_ffn_b[l], ctx_len=ffn_ctx)
    return hs.astype(dtype)
```

```python
import functools
import math

import jax
import jax.numpy as jnp
import numpy as np
from jax import lax
from jax.experimental import pallas as pl
from jax.experimental.pallas import tpu as pltpu

D_MODEL = 1024
DEPTH = 2
GRID_W = 64
HEAD_DIM = 64
N_GROUPS = 4
GROUP_HEADS = D_MODEL // (N_GROUPS * HEAD_DIM)
GW = GROUP_HEADS * HEAD_DIM
DIFF_HALF = HEAD_DIM // 2
ROPE_BASE = 10000.0
DECAY_LORA = 64
ICLR_LORA = 64
GATE_LORA = 128
RWKV_GN_EPS = 64e-5
RWKV_COLS = 3 * GW + 2 * DECAY_LORA + 2 * ICLR_LORA + GATE_LORA
GDN_CHUNK = 64
GDN_COLS = 4 * GW + 4 * GROUP_HEADS
WIN_H = 8
WIN_W = 16
ATTN_COLS = 3 * GW
PEER_HEADS = 8
PEER_KEYS = 128
PEER_EXPERTS = PEER_KEYS * PEER_KEYS
PEER_QDIM = 256
PEER_HALF = PEER_QDIM // 2
PEER_TOPK = 16
DN_ALPHA = (2 * DEPTH) ** 0.25
LN_EPS = 1e-5

F32 = jnp.float32
BF16 = jnp.bfloat16

V7X_VMEM_LIMIT_BYTES = 56 * 1024 * 1024
NEG_BIG = -3.0e38


PEER_STAT_ROWS = 4 * PEER_HEADS
LOG2E = 1.4426950408889634
PEER_POW_HEADS = 4
assert PEER_TOPK == 16


def _topk_rows(x, k):
    q = x.shape[0] // 4
    lv = [x[i * q:(i + 1) * q] for i in range(4)]
    for a, b in ((0, 1), (2, 3), (0, 2), (1, 3), (1, 2)):
        lv[a], lv[b] = jnp.maximum(lv[a], lv[b]), jnp.minimum(lv[a], lv[b])
    rows = []
    for i in range(k):
        m = jnp.max(lv[0], axis=0, keepdims=True)
        rows.append(m)
        if i + 1 < k:
            hit = lv[0] == m
            lv = [jnp.where(hit, lv[n + 1], lv[n]) for n in range(3)] + [jnp.where(hit, NEG_BIG, lv[3])]
    return rows


def _peer_score_kernel(x_ref, wq_ref, keys_ref, s_ref, st_ref):
    q = jnp.dot(x_ref[...], wq_ref[...], preferred_element_type=F32).astype(BF16)
    stats = []
    for h in range(PEER_HEADS):
        tops = []
        for p in range(2):
            hp = 2 * h + p
            s_t = lax.dot_general(keys_ref[hp], q[:, hp * PEER_HALF:(hp + 1) * PEER_HALF],
                                  (((1,), (1,)), ((), ())), preferred_element_type=F32)
            s_ref[hp] = s_t
            tops.append(_topk_rows(s_t, PEER_TOPK + 1))
        a, b = tops
        pad = [jnp.full_like(a[0], NEG_BIG)] * 7
        b_head = jnp.concatenate(b[:8], axis=0)
        cand = jnp.concatenate([a[0] + jnp.concatenate(b + pad, axis=0)]
                               + [a[i] + b_head for i in range(1, 8)]
                               + [jnp.concatenate(a[8:] + pad, axis=0) + b[0]], axis=0)
        best_cand = _topk_rows(cand, PEER_TOPK + 1)
        kth, runner_up = best_cand[PEER_TOPK - 1], best_cand[PEER_TOPK]
        best = a[0] + b[0]
        z = jnp.sum(jnp.where(cand >= kth, jnp.exp(cand - best), 0.0), axis=0, keepdims=True)
        stats += [0.5 * (kth + runner_up), tops[0][0], tops[1][0], 1.0 / z]
    st_ref[...] = jnp.concatenate(stats, axis=0)


def _peer_expert_kernel(x_ref, s_ref, st_ref, u_ref, vt_ref, o_ref, e_ref, thr_ref, lin_ref, acc_ref, w_ref,
                        *, rows_per_step, n_chunks):
    j = pl.program_id(1)
    cur = lax.rem(j, 2)

    @pl.when(j == 0)
    def _():
        acc_ref[...] = jnp.zeros_like(acc_ref)
        w_ref[1] = jnp.zeros(w_ref.shape[1:], w_ref.dtype)
        for h in range(PEER_HEADS):
            a0 = st_ref[4 * h + 1:4 * h + 2, :]
            b0 = st_ref[4 * h + 2:4 * h + 3, :]
            rz = st_ref[4 * h + 3:4 * h + 4, :]
            l1 = (s_ref[2 * h] - a0 + jnp.log(rz)) * LOG2E
            l2 = (s_ref[2 * h + 1] - b0) * LOG2E
            e_ref[2 * h] = l1 if h < PEER_POW_HEADS else jnp.exp2(l1)
            e_ref[2 * h + 1] = l2
            if h >= PEER_POW_HEADS:
                lin_ref[h - PEER_POW_HEADS] = jnp.exp2(l2)
            thr_ref[h] = (st_ref[4 * h:4 * h + 1, :] - s_ref[2 * h] - b0) * LOG2E

    @pl.when(j < n_chunks)
    def _():
        slab = 2 * PEER_KEYS
        n_slabs = rows_per_step * PEER_KEYS // slab
        act_slab = lambda n: lax.dot_general(u_ref[n * slab:(n + 1) * slab, :], x_ref[...], (((1,), (1,)), ((), ())),
                                             preferred_element_type=F32)
        acts = [act_slab(n) for n in range(n_slabs)]
        acc_ref[...] += jnp.dot(vt_ref[...], w_ref[1 - cur], preferred_element_type=F32)
        assert rows_per_step % 8 == 0
        for il in range(rows_per_step):
            base = pl.multiple_of(j * rows_per_step + (il // 8) * 8, 8)
            r = il % 8
            o = il * PEER_KEYS
            act = acts[o // slab][o % slab:o % slab + PEER_KEYS]
            act = 0.5 * act * (1.0 + lax.erf(act * (2.0 ** -0.5)))
            gate = None
            for h in range(PEER_HEADS):
                l2 = e_ref[2 * h + 1]
                thr_row = thr_ref[h, pl.ds(base, 8), :][r:r + 1, :]
                half1 = e_ref[2 * h, pl.ds(base, 8), :][r:r + 1, :]
                pair = jnp.exp2(l2 + half1) if h < PEER_POW_HEADS else lin_ref[h - PEER_POW_HEADS] * half1
                term = jnp.where(l2 >= thr_row, pair, 0.0)
                gate = term if gate is None else gate + term
            w_ref[cur, o:o + PEER_KEYS, :] = (act * gate).astype(BF16)

    @pl.when(j == n_chunks)
    def _():
        o_ref[...] = (acc_ref[...] + jnp.dot(vt_ref[...], w_ref[1 - cur], preferred_element_type=F32)).T


def _peer_pallas(h, w_q, keys, u_tab, vt_tab, *, tb, rows_per_step):
    n_tok, d = h.shape
    nhp = 2 * PEER_HEADS
    s, st = pl.pallas_call(
        _peer_score_kernel,
        name="peer_scores",
        grid=(n_tok // tb,),
        in_specs=[pl.BlockSpec((tb, d), lambda i: (i, 0)),
                  pl.BlockSpec(w_q.shape, lambda i: (0, 0)),
                  pl.BlockSpec(keys.shape, lambda i: (0, 0, 0))],
        out_specs=[pl.BlockSpec((nhp, PEER_KEYS, tb), lambda i: (0, 0, i)),
                   pl.BlockSpec((PEER_STAT_ROWS, tb), lambda i: (0, i))],
        out_shape=[jax.ShapeDtypeStruct((nhp, PEER_KEYS, n_tok), F32),
                   jax.ShapeDtypeStruct((PEER_STAT_ROWS, n_tok), F32)],
        compiler_params=pltpu.CompilerParams(dimension_semantics=("arbitrary",),
                                             vmem_limit_bytes=V7X_VMEM_LIMIT_BYTES),
    )(h, w_q, keys)
    ec = rows_per_step * PEER_KEYS
    n_chunks = PEER_EXPERTS // ec
    return pl.pallas_call(
        functools.partial(_peer_expert_kernel, rows_per_step=rows_per_step, n_chunks=n_chunks),
        name="peer_experts",
        grid=(n_tok // tb, n_chunks + 1),
        in_specs=[pl.BlockSpec((tb, d), lambda i, j: (i, 0)),
                  pl.BlockSpec((nhp, PEER_KEYS, tb), lambda i, j: (0, 0, i)),
                  pl.BlockSpec((PEER_STAT_ROWS, tb), lambda i, j: (0, i)),
                  pl.BlockSpec((ec, d), lambda i, j: (jnp.minimum(j, n_chunks - 1), 0)),
                  pl.BlockSpec((d, ec), lambda i, j: (0, jnp.maximum(j - 1, 0)))],
        out_specs=pl.BlockSpec((tb, d), lambda i, j: (i, 0)),
        out_shape=jax.ShapeDtypeStruct((n_tok, d), F32),
        scratch_shapes=[pltpu.VMEM((nhp, PEER_KEYS, tb), F32),
                        pltpu.VMEM((PEER_HEADS, PEER_KEYS, tb), F32),
                        pltpu.VMEM((PEER_HEADS - PEER_POW_HEADS, PEER_KEYS, tb), F32),
                        pltpu.VMEM((d, tb), F32),
                        pltpu.VMEM((2, ec, tb), BF16)],
        compiler_params=pltpu.CompilerParams(dimension_semantics=("arbitrary", "arbitrary"),
                                             vmem_limit_bytes=V7X_VMEM_LIMIT_BYTES),
    )(h, s, st, u_tab, vt_tab)


def _peer(h, w_q, keys, u_tab, vt_tab):
    bsz, t, d = h.shape
    n_tok = bsz * t
    tb = 512 if n_tok % 512 == 0 else 256
    out = _peer_pallas(h.reshape(n_tok, d).astype(BF16), w_q, keys, u_tab, vt_tab, tb=tb, rows_per_step=16)
    return out.reshape(bsz, t, d)


ROW_BLOCK = 256
HIGHEST = lax.Precision.HIGHEST


def _dot_f32(a, b):
    return jnp.dot(a, b, precision=HIGHEST, preferred_element_type=F32)


def _segment_ones(n, seg, dtype):
    r = lax.broadcasted_iota(jnp.int32, (n, n), 0) // seg
    c = lax.broadcasted_iota(jnp.int32, (n, n), 1) // seg
    return jnp.where(r == c, 1.0, 0.0).astype(dtype)


def _shifted_rows(x, prev_row, next_row):
    t = x.shape[0]
    rows = lax.broadcasted_iota(jnp.int32, x.shape, 0)
    xm = jnp.where(rows == 0, prev_row, pltpu.roll(x, 1, axis=0))
    xp = jnp.where(rows == t - 1, next_row, pltpu.roll(x, t - 1, axis=0))
    return xm, xp


def _segment_edge_flags(i, n_blocks, ctx_blocks):
    is_start = jnp.logical_or(i == 0, i == ctx_blocks)
    is_end = jnp.logical_or(i == ctx_blocks - 1, i == n_blocks - 1)
    return jnp.where(is_start, 0.0, 1.0), jnp.where(is_end, 0.0, 1.0)


def _halo_specs(width, tr):
    g = tr // 8
    prev = pl.BlockSpec((1, 8, width), lambda b, i: (b, jnp.maximum(i * g - 1, 0), 0))
    nxt = lambda n_groups: pl.BlockSpec((1, 8, width), lambda b, i: (b, jnp.minimum((i + 1) * g, n_groups - 1), 0))
    return prev, nxt


def _softplus(z):
    return jnp.maximum(z, 0.0) + jnp.log1p(jnp.exp(-jnp.abs(z)))


def _rwkv_prep_kernel(x_ref, xprev_ref, xnext_ref, mu_ref, w0_ref, w2_ref, a0_ref, a2_ref, g2_ref,
                      kk_ref, ka_ref, rk_ref,
                      r_out, v_out, kkn_out, g_out, bonus_out, w_out, b_out, kt_out, *, ctx_blocks):
    i = pl.program_id(1)
    keep_prev, keep_next = _segment_edge_flags(i, pl.num_programs(1), ctx_blocks)
    x = x_ref[0]
    xm, xp = _shifted_rows(x, xprev_ref[0, 7:8, :] * keep_prev, xnext_ref[0, 0:1, :] * keep_next)
    mu0 = mu_ref[0:1, :]
    mu1 = mu_ref[1:2, :]
    f = xm * mu0 + x * (1.0 - mu0 - mu1) + xp * mu1
    r = f[:, 0:GW]
    k = f[:, GW:2 * GW]
    v = f[:, 2 * GW:3 * GW]
    o = 3 * GW
    wd = jnp.tanh(f[:, o:o + 2 * DECAY_LORA])
    ad = f[:, o + 2 * DECAY_LORA:o + 2 * DECAY_LORA + 2 * ICLR_LORA]
    gd = f[:, o + 2 * DECAY_LORA + 2 * ICLR_LORA:]
    w_raw = w0_ref[...] + _dot_f32(wd, w2_ref[...])
    log_decay = -jnp.exp(-_softplus(-w_raw) - 0.5)
    a = jax.nn.sigmoid(a0_ref[...] + _dot_f32(ad, a2_ref[...]))
    g = _dot_f32(jax.nn.sigmoid(gd), g2_ref[...])
    head_sum = _segment_ones(GW, HEAD_DIM, F32)
    kx = k * kk_ref[...]
    kkn = kx * lax.rsqrt(_dot_f32(kx * kx, head_sum) + 1e-6)
    kd_sum = jnp.zeros_like(k)
    for d in range(2):
        a_d = a[:, d * GW:(d + 1) * GW]
        kd = k * (1.0 + (a_d - 1.0) * ka_ref[...])
        kd_sum = kd_sum + kd
        w_out[d, 0] = log_decay[:, d * GW:(d + 1) * GW]
        b_out[d, 0] = kkn * a_d
        kt_out[d, 0] = kd
    r_out[0] = r
    v_out[0] = v
    kkn_out[0] = kkn
    g_out[0] = g
    bonus_out[0] = _dot_f32(r * kd_sum * rk_ref[...], head_sum) * v


RWKV_CHUNK = 64


def _rwkv_chunk_kernel(rf_ref, vf_ref, kkf_ref, wf_ref, bf_ref, ktf_ref,
                       rb_ref, vb_ref, kkb_ref, wb_ref, bb_ref, ktb_ref, yf_ref, yb_ref, st_ref):
    i = pl.program_id(1)
    c = RWKV_CHUNK
    n_chunks = rf_ref.shape[1] // c

    @pl.when(i == 0)
    def _():
        st_ref[...] = jnp.zeros_like(st_ref)

    rr = lax.broadcasted_iota(jnp.int32, (c, c), 0)
    ss = lax.broadcasted_iota(jnp.int32, (c, c), 1)
    eye = rr == ss
    ones_cc = jnp.ones((c, c), F32)
    incl = (rr >= ss, rr <= ss)
    strict = (rr > ss, rr < ss)
    levels = []
    blk = 1
    while blk < c:
        levels.append(jnp.logical_and(rr // (2 * blk) == ss // (2 * blk), rr // blk != ss // blk))
        blk *= 2
    bdot = lambda x, y: jnp.dot(x.astype(BF16), y.astype(BF16), preferred_element_type=F32)
    bdot_nt = lambda x, y: lax.dot_general(x.astype(BF16), y.astype(BF16), _NT, preferred_element_type=F32)
    bdot_tn = lambda x, y: lax.dot_general(x.astype(BF16), y.astype(BF16), (((0,), (0,)), ((), ())),
                                           preferred_element_type=F32)
    dirs = ((rf_ref, vf_ref, kkf_ref, wf_ref, bf_ref, ktf_ref, yf_ref),
            (rb_ref, vb_ref, kkb_ref, wb_ref, bb_ref, ktb_ref, yb_ref))
    heads = [(d, h) for d in range(2) for h in range(GROUP_HEADS)]
    row0 = lambda cc, d: (cc if d == 0 else n_chunks - 1 - cc) * c
    rows = lambda cc, d: slice(row0(cc, d), row0(cc, d) + c)

    logw = {(cc, d): dirs[d][3][0, 0, rows(cc, d), :] for cc in range(n_chunks) for d in range(2)}
    cum = {key: _dot_f32(jnp.where(incl[key[1]], 1.0, 0.0), lw) for key, lw in logw.items()}
    tot = {key: _dot_f32(ones_cc, lw) for key, lw in logw.items()}
    st = {}
    for cc in range(n_chunks):
        for d, h in heads:
            cols = slice(h * HEAD_DIM, (h + 1) * HEAD_DIM)
            r_ref, v_ref, kk_ref, _, b_ref, kt_ref, _ = dirs[d]
            lc = cum[cc, d][:, cols]
            grow = jnp.exp(-lc)
            st[cc, d, h] = dict(
                v=v_ref[0, rows(cc, d), cols],
                kap=kk_ref[0, rows(cc, d), cols] * jnp.exp(lc - logw[cc, d][:, cols]),
                bh=b_ref[0, 0, rows(cc, d), cols] * grow,
                kh=kt_ref[0, 0, rows(cc, d), cols] * grow,
                rh=r_ref[0, rows(cc, d), cols] * jnp.exp(lc),
                scale=jnp.exp(_dot_f32(jnp.where(eye, tot[cc, d][:, cols], 0.0), ones_cc)))
    for (cc, d, h), x in st.items():
        gram = bdot_nt(jnp.concatenate([x["kap"], x["rh"]], axis=0), jnp.concatenate([x["bh"], x["kh"]], axis=0))
        x["a_b"] = jnp.where(strict[d], gram[:c, :c], 0.0)
        x["a_k"] = jnp.where(strict[d], gram[:c, c:], 0.0)
        x["b_b"] = jnp.where(incl[d], gram[c:, :c], 0.0)
        x["b_k"] = jnp.where(incl[d], gram[c:, c:], 0.0)
        x["inv"] = jnp.where(eye, 1.0, 0.0) - jnp.where(levels[0], x["a_b"], 0.0)
    for lvl in levels[1:]:
        for x in st.values():
            x["t"] = bdot(jnp.where(lvl, x["a_b"], 0.0), x["inv"])
        for x in st.values():
            x["inv"] = x["inv"] - bdot(x["inv"], x["t"])
    for x in st.values():
        x["akv"] = bdot(x["a_k"], x["v"])
        x["y_const"] = bdot(x["b_k"], x["v"])
        x["kv"] = bdot_tn(x["kh"], x["v"])
    for x in st.values():
        both = bdot(x["inv"], jnp.concatenate([x["kap"], x["akv"]], axis=-1))
        x["p_state"], x["p_const"] = both[:, :HEAD_DIM], both[:, HEAD_DIM:]
    state = {key: st_ref[key[0], key[1]] for key in heads}
    for cc in range(n_chunks):
        cur = [(key, st[(cc,) + key]) for key in heads]
        for key, x in cur:
            x["p"] = bdot(x["p_state"], state[key]) + x["p_const"]
            x["y0"] = bdot(x["rh"], state[key]) + x["y_const"]
        for key, x in cur:
            x["y"] = x["y0"] - bdot(x["b_b"], x["p"])
            state[key] = (state[key] + x["kv"] - bdot_tn(x["bh"], x["p"])) * x["scale"]
        for d in range(2):
            dirs[d][6][0, rows(cc, d), :] = jnp.concatenate([x["y"] for (dd, h), x in cur if dd == d], axis=-1)
    for d, h in heads:
        st_ref[d, h] = state[d, h]


def _rwkv_post_rows(y, bonus, g, ln_g, ln_b):
    head_mean = _segment_ones(GW, HEAD_DIM, F32) * (1.0 / HEAD_DIM)
    yc = y - _dot_f32(y, head_mean)
    var = _dot_f32(yc * yc, head_mean)
    return (yc * lax.rsqrt(var + RWKV_GN_EPS) * ln_g + ln_b + bonus) * g


def _rwkv_post_kernel(yf_ref, yb_ref, bonus_ref, g_ref, lng_ref, lnb_ref, o_ref):
    o_ref[0] = _rwkv_post_rows(yf_ref[0] + yb_ref[0], bonus_ref[0], g_ref[0], lng_ref[...], lnb_ref[...])


def _block_diag2(m):
    z = jnp.zeros_like(m[0])
    return jnp.concatenate([jnp.concatenate([m[0], z], 1), jnp.concatenate([z, m[1]], 1)], 0)


def _rwkv7_pallas(f, mu, w0, w2, a0, a2, g2, k_k, k_a, r_k, ln_g, ln_b, *, ctx_len, apply_post=True):
    bsz, seq, cols = f.shape
    tr = ROW_BLOCK
    assert ctx_len % tr == 0 and seq % tr == 0
    nb, ctx_blocks = seq // tr, ctx_len // tr
    prev_spec, next_spec = _halo_specs(cols, tr)
    row2 = lambda a: a.reshape(1, -1).astype(F32)
    full = lambda a: pl.BlockSpec(a.shape, lambda b, i: (0,) * a.ndim)
    params = [mu, row2(w0), _block_diag2(w2), row2(a0), _block_diag2(a2), g2, row2(k_k), row2(k_a), row2(r_k)]
    act = jax.ShapeDtypeStruct((bsz, seq, GW), F32)
    act2 = jax.ShapeDtypeStruct((2, bsz, seq, GW), F32)
    blk = pl.BlockSpec((1, tr, GW), lambda b, i: (b, i, 0))
    blk2 = pl.BlockSpec((2, 1, tr, GW), lambda b, i: (0, b, i, 0))
    r, v, kkn, g, bonus, w, bb, kt = pl.pallas_call(
        functools.partial(_rwkv_prep_kernel, ctx_blocks=ctx_blocks),
        name="rwkv_prep",
        grid=(bsz, nb),
        in_specs=[pl.BlockSpec((1, tr, cols), lambda b, i: (b, i, 0)), prev_spec, next_spec(seq // 8)]
                 + [full(p) for p in params],
        out_specs=[blk] * 5 + [blk2] * 3,
        out_shape=[act] * 5 + [act2] * 3,
        compiler_params=pltpu.CompilerParams(dimension_semantics=("arbitrary", "arbitrary"),
                                             vmem_limit_bytes=V7X_VMEM_LIMIT_BYTES),
    )(f, f, f, *params)

    def bwd_block(c):
        return jnp.where(c < ctx_blocks, ctx_blocks - 1 - c, nb - 1 - (c - ctx_blocks))
    assert tr % RWKV_CHUNK == 0 and RWKV_CHUNK == HEAD_DIM
    fwd = pl.BlockSpec((1, tr, GW), lambda b, c: (b, c, 0))
    bwd = pl.BlockSpec((1, tr, GW), lambda b, c: (b, bwd_block(c), 0))
    fwd_d = pl.BlockSpec((1, 1, tr, GW), lambda b, c: (0, b, c, 0))
    bwd_d = pl.BlockSpec((1, 1, tr, GW), lambda b, c: (1, b, bwd_block(c), 0))
    yf, yb = pl.pallas_call(
        _rwkv_chunk_kernel,
        name="rwkv_chunks",
        grid=(bsz, nb),
        in_specs=[fwd, fwd, fwd, fwd_d, fwd_d, fwd_d, bwd, bwd, bwd, bwd_d, bwd_d, bwd_d],
        out_specs=[fwd, bwd],
        out_shape=[act, act],
        scratch_shapes=[pltpu.VMEM((2, GROUP_HEADS, HEAD_DIM, HEAD_DIM), F32)],
        compiler_params=pltpu.CompilerParams(dimension_semantics=("arbitrary", "arbitrary"),
                                             vmem_limit_bytes=V7X_VMEM_LIMIT_BYTES),
    )(r, v, kkn, w, bb, kt, r, v, kkn, w, bb, kt)
    if not apply_post:
        return yf, yb, bonus, g, row2(ln_g), row2(ln_b)

    return pl.pallas_call(
        _rwkv_post_kernel,
        name="rwkv_post",
        grid=(bsz, nb),
        in_specs=[blk, blk, blk, blk, full(row2(ln_g)), full(row2(ln_b))],
        out_specs=blk,
        out_shape=act,
        compiler_params=pltpu.CompilerParams(dimension_semantics=("arbitrary", "arbitrary")),
    )(yf, yb, bonus, g, row2(ln_g), row2(ln_b))


GDN_GATE_LANES = 128
GDN_PADDED_COLS = 4 * GW + GDN_GATE_LANES


def _gdn_prep_kernel(x_ref, xprev_ref, xnext_ref, ab_ref, conv_ref, alog_ref, dtb_ref,
                     q_out, k_out, v_out, gb_out, *, ctx_blocks):
    i = pl.program_id(1)
    keep_prev, keep_next = _segment_edge_flags(i, pl.num_programs(1), ctx_blocks)
    x = x_ref[0]
    xm, xp = _shifted_rows(x, xprev_ref[0, 7:8, :] * keep_prev, xnext_ref[0, 0:1, :] * keep_next)
    y = xm * conv_ref[0:1, :] + x * conv_ref[1:2, :] + xp * conv_ref[2:3, :]
    y = y * jax.nn.sigmoid(y)
    head_sum = _segment_ones(GW, HEAD_DIM, F32)
    q = y[:, 0:GW]
    k = y[:, GW:2 * GW]
    q_out[0] = q * lax.rsqrt(_dot_f32(q * q, head_sum) + 1e-6) * (HEAD_DIM ** -0.5)
    k_out[0] = k * lax.rsqrt(_dot_f32(k * k, head_sum) + 1e-6)
    v_out[0] = y[:, 2 * GW:3 * GW]
    ab = ab_ref[0]
    lane = lax.broadcasted_iota(jnp.int32, ab.shape, 1)
    log_alpha = -jnp.exp(alog_ref[...]) * _softplus(ab + dtb_ref[...])
    gb_out[0] = jnp.where(lane < 2 * GROUP_HEADS, log_alpha, jax.nn.sigmoid(ab))


def _gdn_chunk_kernel(qf_ref, kf_ref, vf_ref, gf_ref, qb_ref, kb_ref, vb_ref, gb_ref, of_ref, ob_ref, st_ref):
    i = pl.program_id(1)
    c = GDN_CHUNK
    n_chunks = qf_ref.shape[1] // c

    @pl.when(i == 0)
    def _():
        st_ref[...] = jnp.zeros_like(st_ref)

    r = lax.broadcasted_iota(jnp.int32, (c, c), 0)
    s = lax.broadcasted_iota(jnp.int32, (c, c), 1)
    eye = r == s
    ones_cc = jnp.ones((c, c), F32)
    incl = (r >= s, r <= s)
    strict = (r > s, r < s)
    levels = []
    b = 1
    while b < c:
        levels.append(jnp.logical_and(r // (2 * b) == s // (2 * b), r // b != s // b))
        b *= 2
    dirs = ((qf_ref, kf_ref, vf_ref, gf_ref, of_ref), (qb_ref, kb_ref, vb_ref, gb_ref, ob_ref))

    bdot = lambda x, y: jnp.dot(x.astype(BF16), y.astype(BF16), preferred_element_type=F32)
    bdot_nt = lambda x, y: lax.dot_general(x.astype(BF16), y.astype(BF16), _NT, preferred_element_type=F32)
    bdot_tn = lambda x, y: lax.dot_general(x.astype(BF16), y.astype(BF16), (((0,), (0,)), ((), ())),
                                           preferred_element_type=F32)

    heads = [(d, h) for d in range(2) for h in range(GROUP_HEADS)]
    row0 = lambda cc, d: (cc if d == 0 else n_chunks - 1 - cc) * c
    gates = {(cc, d): dirs[d][3][0, row0(cc, d):row0(cc, d) + c, :] for cc in range(n_chunks) for d in range(2)}
    cum = {key: _dot_f32(jnp.where(incl[key[1]], 1.0, 0.0), g) for key, g in gates.items()}
    tot = {key: _dot_f32(ones_cc, g) for key, g in gates.items()}
    chains = [(cc, d, h) for cc in range(n_chunks) for d, h in heads]
    st = {}
    for cc, d, h in chains:
        cols = slice(h * HEAD_DIM, (h + 1) * HEAD_DIM)
        lg = d * GROUP_HEADS + h
        q, k, v = (dirs[d][n][0, row0(cc, d):row0(cc, d) + c, cols] for n in range(3))
        gc = cum[cc, d][:, lg:lg + 1]
        st[cc, d, h] = dict(q=q, k=k, v=v, gc=gc, gt=tot[cc, d][:, lg:lg + 1],
                            beta=gates[cc, d][:, 2 * GROUP_HEADS + lg:2 * GROUP_HEADS + lg + 1],
                            gc_row=_dot_f32(ones_cc, jnp.where(eye, gc, 0.0)))
    for (cc, d, h), x in st.items():
        x["decay"] = jnp.exp(jnp.where(incl[d], x["gc"] - x["gc_row"], NEG_BIG))
        x["kb"] = x["k"] * x["beta"]
        gram = bdot_nt(jnp.concatenate([x["kb"], x["q"]], axis=0), x["k"])
        x["a"] = jnp.where(strict[d], gram[:c] * x["decay"], 0.0)
        x["qk"] = jnp.where(incl[d], gram[c:] * x["decay"], 0.0)
        x["inv"] = jnp.where(eye, 1.0, 0.0) - jnp.where(levels[0], x["a"], 0.0)
    for lvl in levels[1:]:
        for x in st.values():
            x["t"] = bdot(jnp.where(lvl, x["a"], 0.0), x["inv"])
        for x in st.values():
            x["inv"] = x["inv"] - bdot(x["inv"], x["t"])
    for x in st.values():
        x["eg"] = jnp.exp(x["gc"])
        x["sol"] = bdot(x["inv"], jnp.concatenate([x["v"] * x["beta"], x["kb"] * x["eg"]], axis=-1))
        x["qg"] = x["q"] * x["eg"]
        x["kg"] = x["k"] * jnp.exp(x["gt"] - x["gc"])
    state = {(d, h): st_ref[d, h] for d, h in heads}
    for cc in range(n_chunks):
        cur = [(key, st[(cc,) + key]) for key in heads]
        for key, x in cur:
            x["ws"] = bdot(x["sol"][:, HEAD_DIM:], state[key])
            x["qs"] = bdot(x["qg"], state[key])
        for key, x in cur:
            x["v_new"] = x["sol"][:, :HEAD_DIM] - x["ws"]
            x["o"] = x["qs"] + bdot(x["qk"], x["v_new"])
            x["upd"] = bdot_tn(x["kg"], x["v_new"])
        for key, x in cur:
            state[key] = state[key] * jnp.exp(x["gt"][0:1, :]) + x["upd"]
        for d in range(2):
            dirs[d][4][0, row0(cc, d):row0(cc, d) + c, :] = jnp.concatenate(
                [x["o"] for (dd, h), x in cur if dd == d], axis=-1)
    for d, h in heads:
        st_ref[d, h] = state[d, h]


def _gdn_post_rows(o, gate, g):
    head_mean = _segment_ones(GW, HEAD_DIM, F32) * (1.0 / HEAD_DIM)
    return o * lax.rsqrt(_dot_f32(o * o, head_mean) + 1e-6) * g * (gate * jax.nn.sigmoid(gate))


def _gdn_post_kernel(of_ref, ob_ref, gate_ref, g_ref, o_ref):
    o_ref[0] = _gdn_post_rows(of_ref[0] + ob_ref[0], gate_ref[0], g_ref[...])


def _gated_deltanet_pallas(f, conv_w, a_log, dt_bias, norm_g, *, ctx_len, apply_post=True):
    bsz, seq, width = f.shape
    tr = ROW_BLOCK
    assert ctx_len % tr == 0 and seq % tr == 0 and tr % GDN_CHUNK == 0
    if width == GDN_COLS:
        f = jnp.pad(f, ((0, 0), (0, 0), (0, GDN_PADDED_COLS - GDN_COLS)))
    assert f.shape[2] == GDN_PADDED_COLS
    nb, ctx_blocks = seq // tr, ctx_len // tr
    lane_pad = lambda a: jnp.pad(a.reshape(1, -1).astype(F32), ((0, 0), (0, GDN_GATE_LANES - a.size)))
    prev_spec, next_spec = _halo_specs(3 * GW, tr)
    full = lambda a: pl.BlockSpec(a.shape, lambda b, i: (0,) * a.ndim)
    act = jax.ShapeDtypeStruct((bsz, seq, GW), F32)
    gact = jax.ShapeDtypeStruct((bsz, seq, GDN_GATE_LANES), F32)
    blk = pl.BlockSpec((1, tr, GW), lambda b, i: (b, i, 0))
    gblk = pl.BlockSpec((1, tr, GDN_GATE_LANES), lambda b, i: (b, i, 0))
    gate_view = pl.BlockSpec((1, tr, GW), lambda b, i: (b, i, 3))
    ab_view = pl.BlockSpec((1, tr, GDN_GATE_LANES), lambda b, i: (b, i, 4 * GW // GDN_GATE_LANES))
    params = [conv_w.astype(F32), lane_pad(a_log), lane_pad(dt_bias)]
    q, k, v, gb = pl.pallas_call(
        functools.partial(_gdn_prep_kernel, ctx_blocks=ctx_blocks),
        name="gdn_prep",
        grid=(bsz, nb),
        in_specs=[pl.BlockSpec((1, tr, 3 * GW), lambda b, i: (b, i, 0)), prev_spec, next_spec(seq // 8), ab_view]
                 + [full(p) for p in params],
        out_specs=[blk, blk, blk, gblk],
        out_shape=[act, act, act, gact],
        compiler_params=pltpu.CompilerParams(dimension_semantics=("arbitrary", "arbitrary"),
                                             vmem_limit_bytes=V7X_VMEM_LIMIT_BYTES),
    )(f, f, f, f, *params)

    def bwd_block(i):
        return jnp.where(i < ctx_blocks, ctx_blocks - 1 - i, nb - 1 - (i - ctx_blocks))
    bblk = pl.BlockSpec((1, tr, GW), lambda b, i: (b, bwd_block(i), 0))
    bgblk = pl.BlockSpec((1, tr, GDN_GATE_LANES), lambda b, i: (b, bwd_block(i), 0))
    of, ob = pl.pallas_call(
        _gdn_chunk_kernel,
        name="gdn_chunks",
        grid=(bsz, nb),
        in_specs=[blk, blk, blk, gblk, bblk, bblk, bblk, bgblk],
        out_specs=[blk, bblk],
        out_shape=[act, act],
        scratch_shapes=[pltpu.VMEM((2, GROUP_HEADS, HEAD_DIM, HEAD_DIM), F32)],
        compiler_params=pltpu.CompilerParams(dimension_semantics=("arbitrary", "arbitrary"),
                                             vmem_limit_bytes=V7X_VMEM_LIMIT_BYTES),
    )(q, k, v, gb, q, k, v, gb)

    g_row = jnp.tile(norm_g.reshape(1, HEAD_DIM).astype(F32), (1, GROUP_HEADS))
    if not apply_post:
        return of, ob, f, g_row
    return pl.pallas_call(
        _gdn_post_kernel,
        name="gdn_post",
        grid=(bsz, nb),
        in_specs=[blk, blk, gate_view, full(g_row)],
        out_specs=blk,
        out_shape=act,
        compiler_params=pltpu.CompilerParams(dimension_semantics=("arbitrary", "arbitrary")),
    )(of, ob, f, g_row)


ROPE_PAIR = DIFF_HALF // 4


def _rope_tables(seq, ctx_len, q_scale):
    n = jnp.arange(seq - ctx_len, dtype=jnp.int32)
    row, col = n // GRID_W, n % GRID_W
    i = jnp.arange(HEAD_DIM)
    grp = (i % DIFF_HALF) // (2 * ROPE_PAIR)
    inv = ROPE_BASE ** (-(i % ROPE_PAIR).astype(F32) / ROPE_PAIR)
    pos = jnp.where(grp[None, :] == 0, row[:, None], col[:, None]).astype(F32)
    ang = pos * inv[None, :]
    sign = jnp.where((i % (2 * ROPE_PAIR)) < ROPE_PAIR, -1.0, 1.0)
    cos = jnp.concatenate([jnp.ones((ctx_len, HEAD_DIM), F32), jnp.cos(ang)], 0)
    sin = jnp.concatenate([jnp.zeros((ctx_len, HEAD_DIM), F32), jnp.sin(ang) * sign], 0)
    cos = jnp.tile(cos, (1, GROUP_HEADS))
    sin = jnp.tile(sin, (1, GROUP_HEADS))
    return jnp.concatenate([cos * q_scale, cos], 1), jnp.concatenate([sin * q_scale, sin], 1)


def _rotate_qk(qk, cos, sin):
    lane = lax.broadcasted_iota(jnp.int32, qk.shape, 1)
    partner = jnp.where(lane % (2 * ROPE_PAIR) < ROPE_PAIR,
                        pltpu.roll(qk, qk.shape[1] - ROPE_PAIR, axis=1), pltpu.roll(qk, ROPE_PAIR, axis=1))
    return qk * cos + partner * sin


def _store_qkv(q, k, v, q_out, k_out, v_out):
    q_out[...] = q.astype(BF16)
    for h in range(GROUP_HEADS):
        k_out[h] = k[:, h * HEAD_DIM:(h + 1) * HEAD_DIM].astype(BF16)
        v_out[h] = v[:, h * HEAD_DIM:(h + 1) * HEAD_DIM].astype(BF16)


_NT = (((1,), (1,)), ((), ()))


def _softmax_pv(s, v):
    m = jnp.max(s, axis=-1, keepdims=True)
    e = jnp.exp(s - m)
    return jnp.dot(e.astype(BF16), v, preferred_element_type=F32) / jnp.sum(e, axis=-1, keepdims=True)


def _diff_attn_kernel(q_ref, k_ref, v_ref, lam_ref, g_ref, o_ref, *, ctx_blocks, ctx_len):
    i = pl.program_id(1)
    lv = lam_ref[...]
    lam_init = lv[4:5, 0:1]
    lam = (jnp.exp(jnp.sum(lv[0:1] * lv[1:2], axis=-1, keepdims=True))
           - jnp.exp(jnp.sum(lv[2:3] * lv[3:4], axis=-1, keepdims=True)) + lam_init)
    lane = lax.broadcasted_iota(jnp.int32, (q_ref.shape[1], HEAD_DIM), 1)

    def attend(n_keys):
        outs = []
        for h in range(GROUP_HEADS):
            qh = q_ref[0, :, h * HEAD_DIM:(h + 1) * HEAD_DIM]
            kh = k_ref[h, 0, 0:n_keys, :]
            vh = v_ref[h, 0, 0:n_keys, :]
            zero = jnp.zeros_like(qh)
            s1 = lax.dot_general(jnp.where(lane < DIFF_HALF, qh, zero), kh, _NT, preferred_element_type=F32)
            s2 = lax.dot_general(jnp.where(lane >= DIFF_HALF, qh, zero), kh, _NT, preferred_element_type=F32)
            o = _softmax_pv(s1, vh) - lam * _softmax_pv(s2, vh)
            o = o * lax.rsqrt(jnp.mean(o * o, axis=-1, keepdims=True) + 1e-6) * g_ref[...] * (1.0 - lam_init)
            outs.append(o)
        o_ref[0] = jnp.concatenate(outs, axis=-1)

    @pl.when(i < ctx_blocks)
    def _():
        attend(ctx_len)

    @pl.when(i >= ctx_blocks)
    def _():
        attend(k_ref.shape[2])


def _diff_attention_pallas(q, k, v, lam_vecs, norm_g, *, ctx_len, lam_init):
    bsz, seq, _ = q.shape
    tq = ROW_BLOCK
    kv_spec = pl.BlockSpec((GROUP_HEADS, 1, seq, HEAD_DIM), lambda b, i: (0, b, 0, 0))
    lam_rows = jnp.concatenate([lam_vecs.astype(F32), jnp.full((1, lam_vecs.shape[1]), lam_init, F32)], 0)
    return pl.pallas_call(
        functools.partial(_diff_attn_kernel, ctx_blocks=ctx_len // tq, ctx_len=ctx_len),
        name="diff_attn",
        grid=(bsz, seq // tq),
        in_specs=[pl.BlockSpec((1, tq, GW), lambda b, i: (b, i, 0)), kv_spec, kv_spec,
                  pl.BlockSpec(lam_rows.shape, lambda b, i: (0, 0)),
                  pl.BlockSpec((1, HEAD_DIM), lambda b, i: (0, 0))],
        out_specs=pl.BlockSpec((1, tq, GW), lambda b, i: (b, i, 0)),
        out_shape=jax.ShapeDtypeStruct((bsz, seq, GW), F32),
        compiler_params=pltpu.CompilerParams(dimension_semantics=("arbitrary", "arbitrary"),
                                             vmem_limit_bytes=V7X_VMEM_LIMIT_BYTES),
    )(q, k, v, lam_rows, norm_g.reshape(1, HEAD_DIM).astype(F32))


NAT_TILE_ROWS = ROW_BLOCK // GRID_W
NAT_SLAB_ROWS = NAT_TILE_ROWS + WIN_H - 1


def _nat_slab_start(tile, n_rows):
    return np.clip(tile * NAT_TILE_ROWS - WIN_H // 2, 0, n_rows - NAT_SLAB_ROWS)


def _nat_bias_tables(rpb, n_rows):
    n_tiles = n_rows // NAT_TILE_ROWS
    nq, nk, w = NAT_TILE_ROWS, NAT_SLAB_ROWS, GRID_W
    cq, ck = np.arange(w)[:, None], np.arange(w)[None, :]
    d_col = np.clip(ck - cq, -(WIN_W - 1), WIN_W - 1) + WIN_W - 1
    col_1h = (d_col.reshape(-1)[:, None] == np.arange(2 * WIN_W - 1)[None, :]).astype(np.float32)
    c0 = np.clip(cq - WIN_W // 2, 0, w - WIN_W)
    col_ok = (ck >= c0) & (ck < c0 + WIN_W)
    tabs = []
    for tile in (0, 1, n_tiles - 1):
        r = tile * nq + np.arange(nq)[:, None]
        kr = _nat_slab_start(tile, n_rows) + np.arange(nk)[None, :]
        rs = np.clip(r - WIN_H // 2, 0, n_rows - WIN_H)
        row_ok = (kr >= rs) & (kr < rs + WIN_H)
        d_row = np.clip(kr - r + WIN_H - 1, 0, 2 * WIN_H - 2)
        row_1h = (d_row.reshape(-1)[:, None] == np.arange(2 * WIN_H - 1)[None, :]).astype(np.float32)
        t = jnp.einsum('pa,hab,cb->hpc', row_1h, rpb.astype(F32), col_1h, precision=HIGHEST)
        t = t.reshape(GROUP_HEADS, nq, nk, w, w).transpose(0, 1, 3, 2, 4)
        ok = row_ok[:, None, :, None] & col_ok[None, :, None, :]
        tabs.append(jnp.where(ok[None], t, NEG_BIG).reshape(GROUP_HEADS, nq * w, nk * w))
    return jnp.stack(tabs)


def _nat_attn_kernel(q_ref, k_ref, v_ref, bias_ref, o_ref, *, ctx_blocks, ctx_len, n_rows):
    i = pl.program_id(1)
    n_slab = NAT_SLAB_ROWS * GRID_W

    def heads(fn):
        o_ref[0] = jnp.concatenate(
            [fn(h, q_ref[0, :, h * HEAD_DIM:(h + 1) * HEAD_DIM]) for h in range(GROUP_HEADS)], axis=-1)

    @pl.when(i < ctx_blocks)
    def _():
        def ctx_only(h, qh):
            s = lax.dot_general(qh, k_ref[h, 0, 0:ctx_len, :], _NT, preferred_element_type=F32)
            return _softmax_pv(s, v_ref[h, 0, 0:ctx_len, :])
        heads(ctx_only)

    @pl.when(i >= ctx_blocks)
    def _():
        tile = i - ctx_blocks
        start = jnp.clip(tile * NAT_TILE_ROWS - WIN_H // 2, 0, n_rows - NAT_SLAB_ROWS)
        off = pl.multiple_of(ctx_len + start * GRID_W, GRID_W)

        def windowed(h, qh):
            s_w = lax.dot_general(qh, k_ref[h, 0, pl.ds(off, n_slab), :], _NT,
                                  preferred_element_type=F32) + bias_ref[0, h]
            s_c = lax.dot_general(qh, k_ref[h, 0, 0:ctx_len, :], _NT, preferred_element_type=F32)
            m = jnp.maximum(jnp.max(s_w, axis=-1, keepdims=True), jnp.max(s_c, axis=-1, keepdims=True))
            e_w = jnp.exp(s_w - m)
            e_c = jnp.exp(s_c - m)
            den = jnp.sum(e_w, axis=-1, keepdims=True) + jnp.sum(e_c, axis=-1, keepdims=True)
            num = (jnp.dot(e_w.astype(BF16), v_ref[h, 0, pl.ds(off, n_slab), :], preferred_element_type=F32)
                   + jnp.dot(e_c.astype(BF16), v_ref[h, 0, 0:ctx_len, :], preferred_element_type=F32))
            return num / den
        heads(windowed)


def _nat_attention_pallas(q, k, v, rpb, *, ctx_len):
    bsz, seq, _ = q.shape
    tq = ROW_BLOCK
    ctx_blocks = ctx_len // tq
    n_rows = (seq - ctx_len) // GRID_W
    n_tiles = n_rows // NAT_TILE_ROWS
    assert n_rows >= NAT_SLAB_ROWS and n_tiles >= 3
    bias = _nat_bias_tables(rpb, n_rows)

    def variant(i):
        tile = i - ctx_blocks
        return jnp.where(tile <= 0, 0, jnp.where(tile >= n_tiles - 1, 2, 1))
    kv_spec = pl.BlockSpec((GROUP_HEADS, 1, seq, HEAD_DIM), lambda b, i: (0, b, 0, 0))
    return pl.pallas_call(
        functools.partial(_nat_attn_kernel, ctx_blocks=ctx_blocks, ctx_len=ctx_len, n_rows=n_rows),
        name="nat_attn",
        grid=(bsz, seq // tq),
        in_specs=[pl.BlockSpec((1, tq, GW), lambda b, i: (b, i, 0)), kv_spec, kv_spec,
                  pl.BlockSpec((1,) + bias.shape[1:], lambda b, i: (variant(i), 0, 0, 0))],
        out_specs=pl.BlockSpec((1, tq, GW), lambda b, i: (b, i, 0)),
        out_shape=jax.ShapeDtypeStruct((bsz, seq, GW), F32),
        compiler_params=pltpu.CompilerParams(dimension_semantics=("arbitrary", "arbitrary"),
                                             vmem_limit_bytes=V7X_VMEM_LIMIT_BYTES),
    )(q, k, v, bias)


N_MOD = 6
MATMUL_ROWS = 512


def _ada_kernel(c_ref, w_ref, b_ref, o_ref):
    c = c_ref[...]
    o_ref[...] = _dot_f32(c * jax.nn.sigmoid(c), w_ref[...]) + b_ref[...]


def _ada_modulation(c, c_ctx, w_ada, b_ada):
    bsz, d = c.shape
    rows = 8 * ((bsz + 1 + 7) // 8)
    cc = jnp.zeros((rows, d), F32).at[:bsz].set(c).at[bsz].set(c_ctx)
    tn = d
    m = pl.pallas_call(
        _ada_kernel,
        name="ada_modulation",
        grid=(w_ada.shape[1] // tn,),
        in_specs=[pl.BlockSpec((rows, d), lambda j: (0, 0)),
                  pl.BlockSpec((d, tn), lambda j: (0, j)),
                  pl.BlockSpec((1, tn), lambda j: (0, j))],
        out_specs=pl.BlockSpec((rows, tn), lambda j: (0, j)),
        out_shape=jax.ShapeDtypeStruct((rows, w_ada.shape[1]), F32),
        compiler_params=pltpu.CompilerParams(dimension_semantics=("arbitrary",)),
    )(cc, w_ada, b_ada.reshape(1, -1))
    lat = m[:bsz].reshape(bsz, 1, N_MOD, d)
    ctx = jnp.broadcast_to(m[bsz].reshape(1, 1, N_MOD, d), (bsz, 1, N_MOD, d))
    return jnp.concatenate([ctx, lat], axis=1)


def _ln(x):
    mu = jnp.mean(x, axis=-1, keepdims=True)
    xc = x - mu
    return xc * lax.rsqrt(jnp.mean(xc * xc, axis=-1, keepdims=True) + LN_EPS)


def _modulate_kernel(h_ref, mod_ref, o_ref, *, shift_row):
    shift = mod_ref[0, 0, shift_row:shift_row + 1, :]
    scale = mod_ref[0, 0, shift_row + 1:shift_row + 2, :]
    o_ref[0] = (_ln(h_ref[0]) * (1.0 + scale) + shift).astype(o_ref.dtype)


def _mod_spec(d, ctx_blocks):
    return pl.BlockSpec((1, 1, N_MOD, d), lambda b, i: (b, jnp.where(i < ctx_blocks, 0, 1), 0, 0))


def _modulate_pallas(hs, mod, shift_row, *, ctx_len):
    bsz, seq, d = hs.shape
    tr = ROW_BLOCK
    blk = pl.BlockSpec((1, tr, d), lambda b, i: (b, i, 0))
    return pl.pallas_call(
        functools.partial(_modulate_kernel, shift_row=shift_row),
        name="modulate",
        grid=(bsz, seq // tr),
        in_specs=[blk, _mod_spec(d, ctx_len // tr)],
        out_specs=blk,
        out_shape=jax.ShapeDtypeStruct((bsz, seq, d), BF16),
        compiler_params=pltpu.CompilerParams(dimension_semantics=("arbitrary", "arbitrary")),
    )(hs, mod)


def _in_proj_kernel(x_ref, wa_ref, wb_ref, wc_ref, wd_ref, cos_ref, sin_ref,
                    qa_out, ka_out, va_out, pb_out, pc_out, qd_out, kd_out, vd_out):
    x = x_ref[...]
    pa = jnp.dot(x, wa_ref[...], preferred_element_type=F32)
    qk = _rotate_qk(pa[:, :2 * GW], cos_ref[...], sin_ref[...])
    _store_qkv(qk[:, :GW], qk[:, GW:], pa[:, 2 * GW:], qa_out, ka_out, va_out)
    pb_out[...] = jnp.dot(x, wb_ref[...], preferred_element_type=F32)
    pc_out[...] = jnp.dot(x, wc_ref[...], preferred_element_type=F32)
    pd = jnp.dot(x, wd_ref[...], preferred_element_type=F32)
    _store_qkv(pd[:, :GW] * (HEAD_DIM ** -0.5), pd[:, GW:2 * GW], pd[:, 2 * GW:], qd_out, kd_out, vd_out)


def _in_proj_pallas(x, ws, cos, sin):
    m, k = x.shape
    tm = MATMUL_ROWS
    row = lambda n: pl.BlockSpec((tm, n), lambda i: (i, 0))
    hm = pl.BlockSpec((GROUP_HEADS, tm, HEAD_DIM), lambda i: (0, i, 0))
    q_shape = jax.ShapeDtypeStruct((m, GW), BF16)
    hm_shape = jax.ShapeDtypeStruct((GROUP_HEADS, m, HEAD_DIM), BF16)
    f32 = lambda n: jax.ShapeDtypeStruct((m, n), F32)
    return pl.pallas_call(
        _in_proj_kernel,
        name="in_proj",
        grid=(m // tm,),
        in_specs=[row(k)] + [pl.BlockSpec(w.shape, lambda i: (0, 0)) for w in ws] + [row(2 * GW), row(2 * GW)],
        out_specs=[row(GW), hm, hm, row(ws[1].shape[1]), row(ws[2].shape[1]), row(GW), hm, hm],
        out_shape=[q_shape, hm_shape, hm_shape, f32(ws[1].shape[1]), f32(ws[2].shape[1]), q_shape, hm_shape, hm_shape],
        compiler_params=pltpu.CompilerParams(dimension_semantics=("arbitrary",),
                                             vmem_limit_bytes=V7X_VMEM_LIMIT_BYTES),
    )(x, *ws, cos, sin)


def _post_norm_rows(h, gate, y, g, b):
    return _ln(DN_ALPHA * h + gate * y) * g + b


def _out_proj_kernel(ya_ref, rf_ref, rb_ref, bonus_ref, rg_ref, lng_ref, lnb_ref, of_ref, ob_ref, gate_ref, gn_ref,
                     yd_ref, w_ref, h_ref, mod_ref, g_ref, b_ref, o_ref):
    yb = _rwkv_post_rows(rf_ref[0] + rb_ref[0], bonus_ref[0], rg_ref[0], lng_ref[...], lnb_ref[...])
    yc = _gdn_post_rows(of_ref[0] + ob_ref[0], gate_ref[0], gn_ref[...])
    mix = None
    for n, y in enumerate((ya_ref[0], yb, yc, yd_ref[0])):
        part = jnp.dot(y.astype(BF16), w_ref[n * GW:(n + 1) * GW, :], preferred_element_type=F32)
        mix = part if mix is None else mix + part
    o_ref[0] = _post_norm_rows(h_ref[0], mod_ref[0, 0, 2:3, :], mix, g_ref[...], b_ref[...])


def _out_proj_post_norm(ya, rwkv_parts, gdn_parts, yd, w_out, hs, mod, g, b, *, ctx_len):
    bsz, seq, d = hs.shape
    tr = ROW_BLOCK
    yblk = pl.BlockSpec((1, tr, GW), lambda bb, i: (bb, i, 0))
    blk = pl.BlockSpec((1, tr, d), lambda bb, i: (bb, i, 0))
    row = pl.BlockSpec((1, d), lambda bb, i: (0, 0))
    grow = pl.BlockSpec((1, GW), lambda bb, i: (0, 0))
    gate_view = pl.BlockSpec((1, tr, GW), lambda bb, i: (bb, i, 3))
    return pl.pallas_call(
        _out_proj_kernel,
        name="out_proj_post_norm",
        grid=(bsz, seq // tr),
        in_specs=[yblk] + [yblk] * 4 + [grow, grow] + [yblk, yblk, gate_view, grow] + [yblk]
                 + [pl.BlockSpec(w_out.shape, lambda bb, i: (0, 0)), blk, _mod_spec(d, ctx_len // tr), row, row],
        out_specs=blk,
        out_shape=jax.ShapeDtypeStruct((bsz, seq, d), F32),
        compiler_params=pltpu.CompilerParams(dimension_semantics=("arbitrary", "arbitrary")),
    )(ya, *rwkv_parts, *gdn_parts, yd, w_out, hs, mod, g.reshape(1, d), b.reshape(1, d))


def _ffn_post_norm_kernel(h_ref, y_ref, mod_ref, g_ref, b_ref, o_ref):
    o_ref[0] = _post_norm_rows(h_ref[0], mod_ref[0, 0, 5:6, :], y_ref[0], g_ref[...], b_ref[...])


def _ffn_post_norm(hs, y, mod, g, b, *, ctx_len):
    bsz, seq, d = hs.shape
    tr = ROW_BLOCK
    blk = pl.BlockSpec((1, tr, d), lambda bb, i: (bb, i, 0))
    row = pl.BlockSpec((1, d), lambda bb, i: (0, 0))
    return pl.pallas_call(
        _ffn_post_norm_kernel,
        name="ffn_post_norm",
        grid=(bsz, seq // tr),
        in_specs=[blk, blk, _mod_spec(d, ctx_len // tr), row, row],
        out_specs=blk,
        out_shape=jax.ShapeDtypeStruct((bsz, seq, d), F32),
        compiler_params=pltpu.CompilerParams(dimension_semantics=("arbitrary", "arbitrary")),
    )(hs, y, mod, g.reshape(1, d), b.reshape(1, d))


def kernel(x, c, ctx, c_ctx, w_ada, b_ada, w_in, w_out, ln_mix_g, ln_mix_b, ln_ffn_g, ln_ffn_b, diff_lam, diff_norm_g, rwkv_mu, rwkv_w0, rwkv_w2, rwkv_a0, rwkv_a2, rwkv_g2, rwkv_kk, rwkv_ka, rwkv_rk, rwkv_ln_g, rwkv_ln_b, gdn_conv, gdn_a_log, gdn_dt_bias, gdn_norm_g, nat_rpb, peer_wq, peer_keys, peer_u, peer_v):
    dtype = x.dtype
    bsz, ctx_len = ctx.shape[0], ctx.shape[1]
    hs = jnp.concatenate([ctx, x], axis=1)
    seq = hs.shape[1]
    col_sizes = [ATTN_COLS, RWKV_COLS, GDN_COLS, ATTN_COLS]
    cos_a, sin_a = (jnp.tile(t, (bsz, 1)) for t in _rope_tables(seq, ctx_len, DIFF_HALF ** -0.5))
    col_offs = np.cumsum([0] + col_sizes)
    d_model = hs.shape[2]
    for l in range(DEPTH):
        lam_init = 0.8 - 0.6 * math.exp(-0.3 * l)
        mod = _ada_modulation(c, c_ctx, w_ada[l], b_ada[l])
        u = _modulate_pallas(hs, mod, 0, ctx_len=ctx_len).reshape(bsz * seq, d_model)
        w_in_b = w_in[l].astype(BF16)
        w_groups = [w_in_b[:, col_offs[n]:col_offs[n + 1]] for n in range(4)]
        w_groups[2] = jnp.pad(w_groups[2], ((0, 0), (0, GDN_PADDED_COLS - GDN_COLS)))
        qa, ka, va, pb, pc, qd, kd, vd = _in_proj_pallas(u, w_groups, cos_a, sin_a)
        rows3 = lambda p: p.reshape(bsz, seq, p.shape[1])
        heads4 = lambda p: p.reshape(GROUP_HEADS, bsz, seq, HEAD_DIM)
        qa, pb, pc, qd = rows3(qa), rows3(pb), rows3(pc), rows3(qd)
        ka, va, kd, vd = heads4(ka), heads4(va), heads4(kd), heads4(vd)
        ya = _diff_attention_pallas(qa, ka, va, diff_lam[l], diff_norm_g[l], ctx_len=ctx_len, lam_init=lam_init)
        yb_parts = _rwkv7_pallas(pb, rwkv_mu[l], rwkv_w0[l], rwkv_w2[l], rwkv_a0[l], rwkv_a2[l], rwkv_g2[l],
                                 rwkv_kk[l], rwkv_ka[l], rwkv_rk[l], rwkv_ln_g[l], rwkv_ln_b[l], ctx_len=ctx_len,
                                 apply_post=False)
        yc_parts = _gated_deltanet_pallas(pc, gdn_conv[l], gdn_a_log[l], gdn_dt_bias[l], gdn_norm_g[l],
                                          ctx_len=ctx_len, apply_post=False)
        yd = _nat_attention_pallas(qd, kd, vd, nat_rpb[l], ctx_len=ctx_len)
        hs = _out_proj_post_norm(ya, yb_parts, yc_parts, yd, w_out[l].astype(BF16), hs, mod, ln_mix_g[l],
                                 ln_mix_b[l], ctx_len=ctx_len)
        wq_b = peer_wq[l].astype(BF16)
        keys_b = peer_keys[l].reshape(2 * PEER_HEADS, PEER_KEYS, PEER_HALF).astype(BF16)
        u_b = peer_u[l].astype(BF16)
        vt_b = peer_v[l].astype(BF16).T
        ffn_ctx = ctx_len
        if l == DEPTH - 1:
            hs, ffn_ctx = hs[:, ctx_len:], 0
        ffn = _peer(_modulate_pallas(hs, mod, 3, ctx_len=ffn_ctx), wq_b, keys_b, u_b, vt_b)
        hs = _ffn_post_norm(hs, ffn, mod, ln_ffn_g[l], ln_ffn_b[l], ctx_len=ffn_ctx)
    return hs.astype(dtype)
```

```python
import functools
import math

import jax
import jax.numpy as jnp
import numpy as np
from jax import lax
from jax.experimental import pallas as pl
from jax.experimental.pallas import tpu as pltpu

D_MODEL = 1024
DEPTH = 2
GRID_W = 64
HEAD_DIM = 64
N_GROUPS = 4
GROUP_HEADS = D_MODEL // (N_GROUPS * HEAD_DIM)
GW = GROUP_HEADS * HEAD_DIM
DIFF_HALF = HEAD_DIM // 2
ROPE_BASE = 10000.0
DECAY_LORA = 64
ICLR_LORA = 64
GATE_LORA = 128
RWKV_GN_EPS = 64e-5
RWKV_COLS = 3 * GW + 2 * DECAY_LORA + 2 * ICLR_LORA + GATE_LORA
GDN_CHUNK = 64
GDN_COLS = 4 * GW + 4 * GROUP_HEADS
WIN_H = 8
WIN_W = 16
ATTN_COLS = 3 * GW
PEER_HEADS = 8
PEER_KEYS = 128
PEER_EXPERTS = PEER_KEYS * PEER_KEYS
PEER_QDIM = 256
PEER_HALF = PEER_QDIM // 2
PEER_TOPK = 16
DN_ALPHA = (2 * DEPTH) ** 0.25
LN_EPS = 1e-5

F32 = jnp.float32
BF16 = jnp.bfloat16

V7X_VMEM_LIMIT_BYTES = 56 * 1024 * 1024
NEG_BIG = -3.0e38


PEER_STAT_ROWS = 4 * PEER_HEADS
LOG2E = 1.4426950408889634
PEER_POW_HEADS = 4
assert PEER_TOPK == 16


def _flat_mod_specs(tm, seq, d, grid_rank):
    first = lambda i, *_: ((i * tm) // seq, 0, 0, 0)
    last = lambda i, *_: ((i * tm + tm - 1) // seq, 0, 0, 0)
    assert grid_rank in (1, 2) and tm <= seq
    return [pl.BlockSpec((1, 2, N_MOD, d), first), pl.BlockSpec((1, 2, N_MOD, d), last)]


def _flat_row_mod(mod0_ref, mod1_ref, i, shape, seq, ctx_len, rows):
    tm = shape[0]
    r0 = i * tm
    b0 = r0 // seq
    boundary = (b0 + 1) * seq
    idx = r0 + lax.broadcasted_iota(jnp.int32, shape, 0)
    second = idx >= boundary
    is_ctx = (idx - jnp.where(second, boundary, b0 * seq)) < ctx_len
    out = []
    for k in rows:
        v = jnp.where(second, mod1_ref[0, 1, k:k + 1, :], mod0_ref[0, 1, k:k + 1, :])
        out.append(jnp.where(is_ctx, mod0_ref[0, 0, k:k + 1, :], v) if ctx_len else v)
    return out


def _topk_rows(x, k):
    q = x.shape[0] // 4
    lv = [x[i * q:(i + 1) * q] for i in range(4)]
    for a, b in ((0, 1), (2, 3), (0, 2), (1, 3), (1, 2)):
        lv[a], lv[b] = jnp.maximum(lv[a], lv[b]), jnp.minimum(lv[a], lv[b])
    rows = []
    for i in range(k):
        m = jnp.max(lv[0], axis=0, keepdims=True)
        rows.append(m)
        if i + 1 < k:
            hit = lv[0] == m
            lv = [jnp.where(hit, lv[n + 1], lv[n]) for n in range(3)] + [jnp.where(hit, NEG_BIG, lv[3])]
    return rows


def _peer_score_kernel(h_ref, mod0_ref, mod1_ref, wq_ref, keys_ref, x_out, s_ref, st_ref, *, seq, ctx_len):
    h = h_ref[...]
    shift, scale = _flat_row_mod(mod0_ref, mod1_ref, pl.program_id(0), h.shape, seq, ctx_len, (3, 4))
    x = (_ln(h) * (1.0 + scale) + shift).astype(BF16)
    x_out[...] = x
    q = jnp.dot(x, wq_ref[...], preferred_element_type=F32).astype(BF16)
    stats = []
    for h in range(PEER_HEADS):
        tops = []
        for p in range(2):
            hp = 2 * h + p
            s_t = lax.dot_general(keys_ref[hp], q[:, hp * PEER_HALF:(hp + 1) * PEER_HALF],
                                  (((1,), (1,)), ((), ())), preferred_element_type=F32)
            s_ref[hp] = s_t
            tops.append(_topk_rows(s_t, PEER_TOPK + 1))
        a, b = tops
        pad = [jnp.full_like(a[0], NEG_BIG)] * 7
        b_head = jnp.concatenate(b[:8], axis=0)
        cand = jnp.concatenate([a[0] + jnp.concatenate(b + pad, axis=0)]
                               + [a[i] + b_head for i in range(1, 8)]
                               + [jnp.concatenate(a[8:] + pad, axis=0) + b[0]], axis=0)
        best_cand = _topk_rows(cand, PEER_TOPK + 1)
        kth, runner_up = best_cand[PEER_TOPK - 1], best_cand[PEER_TOPK]
        best = a[0] + b[0]
        z = jnp.sum(jnp.where(cand >= kth, jnp.exp(cand - best), 0.0), axis=0, keepdims=True)
        stats += [0.5 * (kth + runner_up), tops[0][0], tops[1][0], 1.0 / z]
    st_ref[...] = jnp.concatenate(stats, axis=0)


def _peer_expert_kernel(x_ref, s_ref, st_ref, u_ref, vt_ref, h_ref, mod0_ref, mod1_ref, g_ref, b_ref, o_ref,
                        e_ref, thr_ref, lin_ref, acc_ref, w_ref, *, rows_per_step, n_chunks, seq, ctx_len):
    j = pl.program_id(1)
    cur = lax.rem(j, 2)

    @pl.when(j == 0)
    def _():
        acc_ref[...] = jnp.zeros_like(acc_ref)
        w_ref[1] = jnp.zeros(w_ref.shape[1:], w_ref.dtype)
        for h in range(PEER_HEADS):
            a0 = st_ref[4 * h + 1:4 * h + 2, :]
            b0 = st_ref[4 * h + 2:4 * h + 3, :]
            rz = st_ref[4 * h + 3:4 * h + 4, :]
            l1 = (s_ref[2 * h] - a0 + jnp.log(rz)) * LOG2E
            l2 = (s_ref[2 * h + 1] - b0) * LOG2E
            e_ref[2 * h] = l1 if h < PEER_POW_HEADS else jnp.exp2(l1)
            e_ref[2 * h + 1] = l2
            if h >= PEER_POW_HEADS:
                lin_ref[h - PEER_POW_HEADS] = jnp.exp2(l2)
            thr_ref[h] = (st_ref[4 * h:4 * h + 1, :] - s_ref[2 * h] - b0) * LOG2E

    @pl.when(j < n_chunks)
    def _():
        slab = 2 * PEER_KEYS
        n_slabs = rows_per_step * PEER_KEYS // slab
        act_slab = lambda n: lax.dot_general(u_ref[n * slab:(n + 1) * slab, :], x_ref[...], (((1,), (1,)), ((), ())),
                                             preferred_element_type=F32)
        acts = [act_slab(n) for n in range(n_slabs)]
        acc_ref[...] += jnp.dot(vt_ref[...], w_ref[1 - cur], preferred_element_type=F32)
        assert rows_per_step % 8 == 0
        for il in range(rows_per_step):
            base = pl.multiple_of(j * rows_per_step + (il // 8) * 8, 8)
            r = il % 8
            o = il * PEER_KEYS
            act = acts[o // slab][o % slab:o % slab + PEER_KEYS]
            act = 0.5 * act * (1.0 + lax.erf(act * (2.0 ** -0.5)))
            gate = None
            for h in range(PEER_HEADS):
                l2 = e_ref[2 * h + 1]
                thr_row = thr_ref[h, pl.ds(base, 8), :][r:r + 1, :]
                half1 = e_ref[2 * h, pl.ds(base, 8), :][r:r + 1, :]
                pair = jnp.exp2(l2 + half1) if h < PEER_POW_HEADS else lin_ref[h - PEER_POW_HEADS] * half1
                term = jnp.where(l2 >= thr_row, pair, 0.0)
                gate = term if gate is None else gate + term
            w_ref[cur, o:o + PEER_KEYS, :] = (act * gate).astype(BF16)

    @pl.when(j == n_chunks)
    def _():
        ffn = (acc_ref[...] + jnp.dot(vt_ref[...], w_ref[1 - cur], preferred_element_type=F32)).T
        h = h_ref[...]
        gate, = _flat_row_mod(mod0_ref, mod1_ref, pl.program_id(0), h.shape, seq, ctx_len, (5,))
        o_ref[...] = _post_norm_rows(h, gate, ffn, g_ref[...], b_ref[...])


def _peer_pallas(h, mod, g, b, w_q, keys, u_tab, vt_tab, *, seq, ctx_len, tb, rows_per_step):
    n_tok, d = h.shape
    nhp = 2 * PEER_HEADS
    x, s, st = pl.pallas_call(
        functools.partial(_peer_score_kernel, seq=seq, ctx_len=ctx_len),
        name="peer_scores",
        grid=(n_tok // tb,),
        in_specs=[pl.BlockSpec((tb, d), lambda i: (i, 0))] + _flat_mod_specs(tb, seq, d, 1)
                 + [pl.BlockSpec(w_q.shape, lambda i: (0, 0)),
                    pl.BlockSpec(keys.shape, lambda i: (0, 0, 0))],
        out_specs=[pl.BlockSpec((tb, d), lambda i: (i, 0)),
                   pl.BlockSpec((nhp, PEER_KEYS, tb), lambda i: (0, 0, i)),
                   pl.BlockSpec((PEER_STAT_ROWS, tb), lambda i: (0, i))],
        out_shape=[jax.ShapeDtypeStruct((n_tok, d), BF16),
                   jax.ShapeDtypeStruct((nhp, PEER_KEYS, n_tok), F32),
                   jax.ShapeDtypeStruct((PEER_STAT_ROWS, n_tok), F32)],
        compiler_params=pltpu.CompilerParams(dimension_semantics=("arbitrary",),
                                             vmem_limit_bytes=V7X_VMEM_LIMIT_BYTES),
    )(h, mod, mod, w_q, keys)
    ec = rows_per_step * PEER_KEYS
    n_chunks = PEER_EXPERTS // ec
    row = pl.BlockSpec((1, d), lambda i, j: (0, 0))
    return pl.pallas_call(
        functools.partial(_peer_expert_kernel, rows_per_step=rows_per_step, n_chunks=n_chunks, seq=seq,
                          ctx_len=ctx_len),
        name="peer_experts",
        grid=(n_tok // tb, n_chunks + 1),
        in_specs=[pl.BlockSpec((tb, d), lambda i, j: (i, 0)),
                  pl.BlockSpec((nhp, PEER_KEYS, tb), lambda i, j: (0, 0, i)),
                  pl.BlockSpec((PEER_STAT_ROWS, tb), lambda i, j: (0, i)),
                  pl.BlockSpec((ec, d), lambda i, j: (jnp.minimum(j, n_chunks - 1), 0)),
                  pl.BlockSpec((d, ec), lambda i, j: (0, jnp.maximum(j - 1, 0))),
                  pl.BlockSpec((tb, d), lambda i, j: (i, 0))] + _flat_mod_specs(tb, seq, d, 2) + [row, row],
        out_specs=pl.BlockSpec((tb, d), lambda i, j: (i, 0)),
        out_shape=jax.ShapeDtypeStruct((n_tok, d), F32),
        scratch_shapes=[pltpu.VMEM((nhp, PEER_KEYS, tb), F32),
                        pltpu.VMEM((PEER_HEADS, PEER_KEYS, tb), F32),
                        pltpu.VMEM((PEER_HEADS - PEER_POW_HEADS, PEER_KEYS, tb), F32),
                        pltpu.VMEM((d, tb), F32),
                        pltpu.VMEM((2, ec, tb), BF16)],
        compiler_params=pltpu.CompilerParams(dimension_semantics=("arbitrary", "arbitrary"),
                                             vmem_limit_bytes=V7X_VMEM_LIMIT_BYTES),
    )(x, s, st, u_tab, vt_tab, h, mod, mod, g.reshape(1, d).astype(F32), b.reshape(1, d).astype(F32))


def _peer_ffn(hs, mod, g, b, w_q, keys, u_tab, vt_tab, *, ctx_len):
    bsz, t, d = hs.shape
    n_tok = bsz * t
    tb = 512 if n_tok % 512 == 0 else 256
    out = _peer_pallas(hs.reshape(n_tok, d), mod, g, b, w_q, keys, u_tab, vt_tab, seq=t, ctx_len=ctx_len, tb=tb,
                       rows_per_step=16)
    return out.reshape(bsz, t, d)


ROW_BLOCK = 256
HIGHEST = lax.Precision.HIGHEST


def _dot_f32(a, b):
    return jnp.dot(a, b, precision=HIGHEST, preferred_element_type=F32)


def _segment_ones(n, seg, dtype):
    r = lax.broadcasted_iota(jnp.int32, (n, n), 0) // seg
    c = lax.broadcasted_iota(jnp.int32, (n, n), 1) // seg
    return jnp.where(r == c, 1.0, 0.0).astype(dtype)


def _shifted_rows(x, prev_row, next_row):
    t = x.shape[0]
    rows = lax.broadcasted_iota(jnp.int32, x.shape, 0)
    xm = jnp.where(rows == 0, prev_row, pltpu.roll(x, 1, axis=0))
    xp = jnp.where(rows == t - 1, next_row, pltpu.roll(x, t - 1, axis=0))
    return xm, xp


def _segment_edge_flags(i, n_blocks, ctx_blocks):
    is_start = jnp.logical_or(i == 0, i == ctx_blocks)
    is_end = jnp.logical_or(i == ctx_blocks - 1, i == n_blocks - 1)
    return jnp.where(is_start, 0.0, 1.0), jnp.where(is_end, 0.0, 1.0)


def _halo_specs(width, tr):
    g = tr // 8
    prev = pl.BlockSpec((1, 8, width), lambda b, i: (b, jnp.maximum(i * g - 1, 0), 0))
    nxt = lambda n_groups: pl.BlockSpec((1, 8, width), lambda b, i: (b, jnp.minimum((i + 1) * g, n_groups - 1), 0))
    return prev, nxt


def _softplus(z):
    return jnp.maximum(z, 0.0) + jnp.log1p(jnp.exp(-jnp.abs(z)))


def _rwkv_prep_kernel(x_ref, xprev_ref, xnext_ref, mu_ref, w0_ref, w2_ref, a0_ref, a2_ref, g2_ref,
                      kk_ref, ka_ref, rk_ref,
                      r_out, v_out, kkn_out, g_out, bonus_out, w_out, b_out, kt_out, *, ctx_blocks):
    i = pl.program_id(1)
    keep_prev, keep_next = _segment_edge_flags(i, pl.num_programs(1), ctx_blocks)
    x = x_ref[0]
    xm, xp = _shifted_rows(x, xprev_ref[0, 7:8, :] * keep_prev, xnext_ref[0, 0:1, :] * keep_next)
    mu0 = mu_ref[0:1, :]
    mu1 = mu_ref[1:2, :]
    f = xm * mu0 + x * (1.0 - mu0 - mu1) + xp * mu1
    r = f[:, 0:GW]
    k = f[:, GW:2 * GW]
    v = f[:, 2 * GW:3 * GW]
    o = 3 * GW
    wd = jnp.tanh(f[:, o:o + 2 * DECAY_LORA])
    ad = f[:, o + 2 * DECAY_LORA:o + 2 * DECAY_LORA + 2 * ICLR_LORA]
    gd = f[:, o + 2 * DECAY_LORA + 2 * ICLR_LORA:]
    w_raw = w0_ref[...] + _dot_f32(wd, w2_ref[...])
    log_decay = -jnp.exp(-_softplus(-w_raw) - 0.5)
    a = jax.nn.sigmoid(a0_ref[...] + _dot_f32(ad, a2_ref[...]))
    g = _dot_f32(jax.nn.sigmoid(gd), g2_ref[...])
    head_sum = _segment_ones(GW, HEAD_DIM, F32)
    kx = k * kk_ref[...]
    kkn = kx * lax.rsqrt(_dot_f32(kx * kx, head_sum) + 1e-6)
    kd_sum = jnp.zeros_like(k)
    for d in range(2):
        a_d = a[:, d * GW:(d + 1) * GW]
        kd = k * (1.0 + (a_d - 1.0) * ka_ref[...])
        kd_sum = kd_sum + kd
        w_out[d, 0] = log_decay[:, d * GW:(d + 1) * GW]
        b_out[d, 0] = kkn * a_d
        kt_out[d, 0] = kd
    r_out[0] = r
    v_out[0] = v
    kkn_out[0] = kkn
    g_out[0] = g
    bonus_out[0] = _dot_f32(r * kd_sum * rk_ref[...], head_sum) * v


RWKV_CHUNK = 64


def _rwkv_chunk_kernel(rf_ref, vf_ref, kkf_ref, wf_ref, bf_ref, ktf_ref,
                       rb_ref, vb_ref, kkb_ref, wb_ref, bb_ref, ktb_ref, yf_ref, yb_ref, st_ref):
    i = pl.program_id(1)
    c = RWKV_CHUNK
    n_chunks = rf_ref.shape[1] // c

    @pl.when(i == 0)
    def _():
        st_ref[...] = jnp.zeros_like(st_ref)

    rr = lax.broadcasted_iota(jnp.int32, (c, c), 0)
    ss = lax.broadcasted_iota(jnp.int32, (c, c), 1)
    eye = rr == ss
    ones_cc = jnp.ones((c, c), F32)
    incl = (rr >= ss, rr <= ss)
    strict = (rr > ss, rr < ss)
    levels = []
    blk = 1
    while blk < c:
        levels.append(jnp.logical_and(rr // (2 * blk) == ss // (2 * blk), rr // blk != ss // blk))
        blk *= 2
    bdot = lambda x, y: jnp.dot(x.astype(BF16), y.astype(BF16), preferred_element_type=F32)
    bdot_nt = lambda x, y: lax.dot_general(x.astype(BF16), y.astype(BF16), _NT, preferred_element_type=F32)
    bdot_tn = lambda x, y: lax.dot_general(x.astype(BF16), y.astype(BF16), (((0,), (0,)), ((), ())),
                                           preferred_element_type=F32)
    dirs = ((rf_ref, vf_ref, kkf_ref, wf_ref, bf_ref, ktf_ref, yf_ref),
            (rb_ref, vb_ref, kkb_ref, wb_ref, bb_ref, ktb_ref, yb_ref))
    heads = [(d, h) for d in range(2) for h in range(GROUP_HEADS)]
    row0 = lambda cc, d: (cc if d == 0 else n_chunks - 1 - cc) * c
    rows = lambda cc, d: slice(row0(cc, d), row0(cc, d) + c)

    logw = {(cc, d): dirs[d][3][0, 0, rows(cc, d), :] for cc in range(n_chunks) for d in range(2)}
    cum = {key: _dot_f32(jnp.where(incl[key[1]], 1.0, 0.0), lw) for key, lw in logw.items()}
    tot = {key: _dot_f32(ones_cc, lw) for key, lw in logw.items()}
    st = {}
    for cc in range(n_chunks):
        for d, h in heads:
            cols = slice(h * HEAD_DIM, (h + 1) * HEAD_DIM)
            r_ref, v_ref, kk_ref, _, b_ref, kt_ref, _ = dirs[d]
            lc = cum[cc, d][:, cols]
            grow = jnp.exp(-lc)
            st[cc, d, h] = dict(
                v=v_ref[0, rows(cc, d), cols],
                kap=kk_ref[0, rows(cc, d), cols] * jnp.exp(lc - logw[cc, d][:, cols]),
                bh=b_ref[0, 0, rows(cc, d), cols] * grow,
                kh=kt_ref[0, 0, rows(cc, d), cols] * grow,
                rh=r_ref[0, rows(cc, d), cols] * jnp.exp(lc),
                scale=jnp.exp(_dot_f32(jnp.where(eye, tot[cc, d][:, cols], 0.0), ones_cc)))
    for (cc, d, h), x in st.items():
        gram = bdot_nt(jnp.concatenate([x["kap"], x["rh"]], axis=0), jnp.concatenate([x["bh"], x["kh"]], axis=0))
        x["a_b"] = jnp.where(strict[d], gram[:c, :c], 0.0)
        x["a_k"] = jnp.where(strict[d], gram[:c, c:], 0.0)
        x["b_b"] = jnp.where(incl[d], gram[c:, :c], 0.0)
        x["b_k"] = jnp.where(incl[d], gram[c:, c:], 0.0)
        x["inv"] = jnp.where(eye, 1.0, 0.0) - jnp.where(levels[0], x["a_b"], 0.0)
    for lvl in levels[1:]:
        for x in st.values():
            x["t"] = bdot(jnp.where(lvl, x["a_b"], 0.0), x["inv"])
        for x in st.values():
            x["inv"] = x["inv"] - bdot(x["inv"], x["t"])
    for x in st.values():
        x["akv"] = bdot(x["a_k"], x["v"])
        x["y_const"] = bdot(x["b_k"], x["v"])
        x["kv"] = bdot_tn(x["kh"], x["v"])
    for x in st.values():
        both = bdot(x["inv"], jnp.concatenate([x["kap"], x["akv"]], axis=-1))
        x["p_state"], x["p_const"] = both[:, :HEAD_DIM], both[:, HEAD_DIM:]
    state = {key: st_ref[key[0], key[1]] for key in heads}
    for cc in range(n_chunks):
        cur = [(key, st[(cc,) + key]) for key in heads]
        for key, x in cur:
            x["p"] = bdot(x["p_state"], state[key]) + x["p_const"]
            x["y0"] = bdot(x["rh"], state[key]) + x["y_const"]
        for key, x in cur:
            x["y"] = x["y0"] - bdot(x["b_b"], x["p"])
            state[key] = (state[key] + x["kv"] - bdot_tn(x["bh"], x["p"])) * x["scale"]
        for d in range(2):
            dirs[d][6][0, rows(cc, d), :] = jnp.concatenate([x["y"] for (dd, h), x in cur if dd == d], axis=-1)
    for d, h in heads:
        st_ref[d, h] = state[d, h]


def _rwkv_post_rows(y, bonus, g, ln_g, ln_b):
    head_mean = _segment_ones(GW, HEAD_DIM, F32) * (1.0 / HEAD_DIM)
    yc = y - _dot_f32(y, head_mean)
    var = _dot_f32(yc * yc, head_mean)
    return (yc * lax.rsqrt(var + RWKV_GN_EPS) * ln_g + ln_b + bonus) * g


def _rwkv_post_kernel(yf_ref, yb_ref, bonus_ref, g_ref, lng_ref, lnb_ref, o_ref):
    o_ref[0] = _rwkv_post_rows(yf_ref[0] + yb_ref[0], bonus_ref[0], g_ref[0], lng_ref[...], lnb_ref[...])


def _block_diag2(m):
    z = jnp.zeros_like(m[0])
    return jnp.concatenate([jnp.concatenate([m[0], z], 1), jnp.concatenate([z, m[1]], 1)], 0)


def _rwkv7_pallas(f, mu, w0, w2, a0, a2, g2, k_k, k_a, r_k, ln_g, ln_b, *, ctx_len, apply_post=True):
    bsz, seq, cols = f.shape
    tr = ROW_BLOCK
    assert ctx_len % tr == 0 and seq % tr == 0
    nb, ctx_blocks = seq // tr, ctx_len // tr
    prev_spec, next_spec = _halo_specs(cols, tr)
    row2 = lambda a: a.reshape(1, -1).astype(F32)
    full = lambda a: pl.BlockSpec(a.shape, lambda b, i: (0,) * a.ndim)
    params = [mu, row2(w0), _block_diag2(w2), row2(a0), _block_diag2(a2), g2, row2(k_k), row2(k_a), row2(r_k)]
    act = jax.ShapeDtypeStruct((bsz, seq, GW), F32)
    act2 = jax.ShapeDtypeStruct((2, bsz, seq, GW), F32)
    blk = pl.BlockSpec((1, tr, GW), lambda b, i: (b, i, 0))
    blk2 = pl.BlockSpec((2, 1, tr, GW), lambda b, i: (0, b, i, 0))
    r, v, kkn, g, bonus, w, bb, kt = pl.pallas_call(
        functools.partial(_rwkv_prep_kernel, ctx_blocks=ctx_blocks),
        name="rwkv_prep",
        grid=(bsz, nb),
        in_specs=[pl.BlockSpec((1, tr, cols), lambda b, i: (b, i, 0)), prev_spec, next_spec(seq // 8)]
                 + [full(p) for p in params],
        out_specs=[blk] * 5 + [blk2] * 3,
        out_shape=[act] * 5 + [act2] * 3,
        compiler_params=pltpu.CompilerParams(dimension_semantics=("arbitrary", "arbitrary"),
                                             vmem_limit_bytes=V7X_VMEM_LIMIT_BYTES),
    )(f, f, f, *params)

    def bwd_block(c):
        return jnp.where(c < ctx_blocks, ctx_blocks - 1 - c, nb - 1 - (c - ctx_blocks))
    assert tr % RWKV_CHUNK == 0 and RWKV_CHUNK == HEAD_DIM
    fwd = pl.BlockSpec((1, tr, GW), lambda b, c: (b, c, 0))
    bwd = pl.BlockSpec((1, tr, GW), lambda b, c: (b, bwd_block(c), 0))
    fwd_d = pl.BlockSpec((1, 1, tr, GW), lambda b, c: (0, b, c, 0))
    bwd_d = pl.BlockSpec((1, 1, tr, GW), lambda b, c: (1, b, bwd_block(c), 0))
    yf, yb = pl.pallas_call(
        _rwkv_chunk_kernel,
        name="rwkv_chunks",
        grid=(bsz, nb),
        in_specs=[fwd, fwd, fwd, fwd_d, fwd_d, fwd_d, bwd, bwd, bwd, bwd_d, bwd_d, bwd_d],
        out_specs=[fwd, bwd],
        out_shape=[act, act],
        scratch_shapes=[pltpu.VMEM((2, GROUP_HEADS, HEAD_DIM, HEAD_DIM), F32)],
        compiler_params=pltpu.CompilerParams(dimension_semantics=("arbitrary", "arbitrary"),
                                             vmem_limit_bytes=V7X_VMEM_LIMIT_BYTES),
    )(r, v, kkn, w, bb, kt, r, v, kkn, w, bb, kt)
    if not apply_post:
        return yf, yb, bonus, g, row2(ln_g), row2(ln_b)

    return pl.pallas_call(
        _rwkv_post_kernel,
        name="rwkv_post",
        grid=(bsz, nb),
        in_specs=[blk, blk, blk, blk, full(row2(ln_g)), full(row2(ln_b))],
        out_specs=blk,
        out_shape=act,
        compiler_params=pltpu.CompilerParams(dimension_semantics=("arbitrary", "arbitrary")),
    )(yf, yb, bonus, g, row2(ln_g), row2(ln_b))


GDN_GATE_LANES = 128
GDN_PADDED_COLS = 4 * GW + GDN_GATE_LANES


def _gdn_prep_kernel(x_ref, xprev_ref, xnext_ref, ab_ref, conv_ref, alog_ref, dtb_ref,
                     q_out, k_out, v_out, gb_out, *, ctx_blocks):
    i = pl.program_id(1)
    keep_prev, keep_next = _segment_edge_flags(i, pl.num_programs(1), ctx_blocks)
    x = x_ref[0]
    xm, xp = _shifted_rows(x, xprev_ref[0, 7:8, :] * keep_prev, xnext_ref[0, 0:1, :] * keep_next)
    y = xm * conv_ref[0:1, :] + x * conv_ref[1:2, :] + xp * conv_ref[2:3, :]
    y = y * jax.nn.sigmoid(y)
    head_sum = _segment_ones(GW, HEAD_DIM, F32)
    q = y[:, 0:GW]
    k = y[:, GW:2 * GW]
    q_out[0] = q * lax.rsqrt(_dot_f32(q * q, head_sum) + 1e-6) * (HEAD_DIM ** -0.5)
    k_out[0] = k * lax.rsqrt(_dot_f32(k * k, head_sum) + 1e-6)
    v_out[0] = y[:, 2 * GW:3 * GW]
    ab = ab_ref[0]
    lane = lax.broadcasted_iota(jnp.int32, ab.shape, 1)
    log_alpha = -jnp.exp(alog_ref[...]) * _softplus(ab + dtb_ref[...])
    gb_out[0] = jnp.where(lane < 2 * GROUP_HEADS, log_alpha, jax.nn.sigmoid(ab))


def _gdn_chunk_kernel(qf_ref, kf_ref, vf_ref, gf_ref, qb_ref, kb_ref, vb_ref, gb_ref, of_ref, ob_ref, st_ref):
    i = pl.program_id(1)
    c = GDN_CHUNK
    n_chunks = qf_ref.shape[1] // c

    @pl.when(i == 0)
    def _():
        st_ref[...] = jnp.zeros_like(st_ref)

    r = lax.broadcasted_iota(jnp.int32, (c, c), 0)
    s = lax.broadcasted_iota(jnp.int32, (c, c), 1)
    eye = r == s
    ones_cc = jnp.ones((c, c), F32)
    incl = (r >= s, r <= s)
    strict = (r > s, r < s)
    levels = []
    b = 1
    while b < c:
        levels.append(jnp.logical_and(r // (2 * b) == s // (2 * b), r // b != s // b))
        b *= 2
    dirs = ((qf_ref, kf_ref, vf_ref, gf_ref, of_ref), (qb_ref, kb_ref, vb_ref, gb_ref, ob_ref))

    bdot = lambda x, y: jnp.dot(x.astype(BF16), y.astype(BF16), preferred_element_type=F32)
    bdot_nt = lambda x, y: lax.dot_general(x.astype(BF16), y.astype(BF16), _NT, preferred_element_type=F32)
    bdot_tn = lambda x, y: lax.dot_general(x.astype(BF16), y.astype(BF16), (((0,), (0,)), ((), ())),
                                           preferred_element_type=F32)

    heads = [(d, h) for d in range(2) for h in range(GROUP_HEADS)]
    row0 = lambda cc, d: (cc if d == 0 else n_chunks - 1 - cc) * c
    gates = {(cc, d): dirs[d][3][0, row0(cc, d):row0(cc, d) + c, :] for cc in range(n_chunks) for d in range(2)}
    cum = {key: _dot_f32(jnp.where(incl[key[1]], 1.0, 0.0), g) for key, g in gates.items()}
    tot = {key: _dot_f32(ones_cc, g) for key, g in gates.items()}
    chains = [(cc, d, h) for cc in range(n_chunks) for d, h in heads]
    st = {}
    for cc, d, h in chains:
        cols = slice(h * HEAD_DIM, (h + 1) * HEAD_DIM)
        lg = d * GROUP_HEADS + h
        q, k, v = (dirs[d][n][0, row0(cc, d):row0(cc, d) + c, cols] for n in range(3))
        gc = cum[cc, d][:, lg:lg + 1]
        st[cc, d, h] = dict(q=q, k=k, v=v, gc=gc, gt=tot[cc, d][:, lg:lg + 1],
                            beta=gates[cc, d][:, 2 * GROUP_HEADS + lg:2 * GROUP_HEADS + lg + 1],
                            gc_row=_dot_f32(ones_cc, jnp.where(eye, gc, 0.0)))
    for (cc, d, h), x in st.items():
        x["decay"] = jnp.exp(jnp.where(incl[d], x["gc"] - x["gc_row"], NEG_BIG))
        x["kb"] = x["k"] * x["beta"]
        gram = bdot_nt(jnp.concatenate([x["kb"], x["q"]], axis=0), x["k"])
        x["a"] = jnp.where(strict[d], gram[:c] * x["decay"], 0.0)
        x["qk"] = jnp.where(incl[d], gram[c:] * x["decay"], 0.0)
        x["inv"] = jnp.where(eye, 1.0, 0.0) - jnp.where(levels[0], x["a"], 0.0)
    for lvl in levels[1:]:
        for x in st.values():
            x["t"] = bdot(jnp.where(lvl, x["a"], 0.0), x["inv"])
        for x in st.values():
            x["inv"] = x["inv"] - bdot(x["inv"], x["t"])
    for x in st.values():
        x["eg"] = jnp.exp(x["gc"])
        x["sol"] = bdot(x["inv"], jnp.concatenate([x["v"] * x["beta"], x["kb"] * x["eg"]], axis=-1))
        x["qg"] = x["q"] * x["eg"]
        x["kg"] = x["k"] * jnp.exp(x["gt"] - x["gc"])
    state = {(d, h): st_ref[d, h] for d, h in heads}
    for cc in range(n_chunks):
        cur = [(key, st[(cc,) + key]) for key in heads]
        for key, x in cur:
            x["ws"] = bdot(x["sol"][:, HEAD_DIM:], state[key])
            x["qs"] = bdot(x["qg"], state[key])
        for key, x in cur:
            x["v_new"] = x["sol"][:, :HEAD_DIM] - x["ws"]
            x["o"] = x["qs"] + bdot(x["qk"], x["v_new"])
            x["upd"] = bdot_tn(x["kg"], x["v_new"])
        for key, x in cur:
            state[key] = state[key] * jnp.exp(x["gt"][0:1, :]) + x["upd"]
        for d in range(2):
            dirs[d][4][0, row0(cc, d):row0(cc, d) + c, :] = jnp.concatenate(
                [x["o"] for (dd, h), x in cur if dd == d], axis=-1)
    for d, h in heads:
        st_ref[d, h] = state[d, h]


def _gdn_post_rows(o, gate, g):
    head_mean = _segment_ones(GW, HEAD_DIM, F32) * (1.0 / HEAD_DIM)
    return o * lax.rsqrt(_dot_f32(o * o, head_mean) + 1e-6) * g * (gate * jax.nn.sigmoid(gate))


def _gdn_post_kernel(of_ref, ob_ref, gate_ref, g_ref, o_ref):
    o_ref[0] = _gdn_post_rows(of_ref[0] + ob_ref[0], gate_ref[0], g_ref[...])


def _gated_deltanet_pallas(f, conv_w, a_log, dt_bias, norm_g, *, ctx_len, apply_post=True):
    bsz, seq, width = f.shape
    tr = ROW_BLOCK
    assert ctx_len % tr == 0 and seq % tr == 0 and tr % GDN_CHUNK == 0
    if width == GDN_COLS:
        f = jnp.pad(f, ((0, 0), (0, 0), (0, GDN_PADDED_COLS - GDN_COLS)))
    assert f.shape[2] == GDN_PADDED_COLS
    nb, ctx_blocks = seq // tr, ctx_len // tr
    lane_pad = lambda a: jnp.pad(a.reshape(1, -1).astype(F32), ((0, 0), (0, GDN_GATE_LANES - a.size)))
    prev_spec, next_spec = _halo_specs(3 * GW, tr)
    full = lambda a: pl.BlockSpec(a.shape, lambda b, i: (0,) * a.ndim)
    act = jax.ShapeDtypeStruct((bsz, seq, GW), F32)
    gact = jax.ShapeDtypeStruct((bsz, seq, GDN_GATE_LANES), F32)
    blk = pl.BlockSpec((1, tr, GW), lambda b, i: (b, i, 0))
    gblk = pl.BlockSpec((1, tr, GDN_GATE_LANES), lambda b, i: (b, i, 0))
    gate_view = pl.BlockSpec((1, tr, GW), lambda b, i: (b, i, 3))
    ab_view = pl.BlockSpec((1, tr, GDN_GATE_LANES), lambda b, i: (b, i, 4 * GW // GDN_GATE_LANES))
    params = [conv_w.astype(F32), lane_pad(a_log), lane_pad(dt_bias)]
    q, k, v, gb = pl.pallas_call(
        functools.partial(_gdn_prep_kernel, ctx_blocks=ctx_blocks),
        name="gdn_prep",
        grid=(bsz, nb),
        in_specs=[pl.BlockSpec((1, tr, 3 * GW), lambda b, i: (b, i, 0)), prev_spec, next_spec(seq // 8), ab_view]
                 + [full(p) for p in params],
        out_specs=[blk, blk, blk, gblk],
        out_shape=[act, act, act, gact],
        compiler_params=pltpu.CompilerParams(dimension_semantics=("arbitrary", "arbitrary"),
                                             vmem_limit_bytes=V7X_VMEM_LIMIT_BYTES),
    )(f, f, f, f, *params)

    def bwd_block(i):
        return jnp.where(i < ctx_blocks, ctx_blocks - 1 - i, nb - 1 - (i - ctx_blocks))
    bblk = pl.BlockSpec((1, tr, GW), lambda b, i: (b, bwd_block(i), 0))
    bgblk = pl.BlockSpec((1, tr, GDN_GATE_LANES), lambda b, i: (b, bwd_block(i), 0))
    of, ob = pl.pallas_call(
        _gdn_chunk_kernel,
        name="gdn_chunks",
        grid=(bsz, nb),
        in_specs=[blk, blk, blk, gblk, bblk, bblk, bblk, bgblk],
        out_specs=[blk, bblk],
        out_shape=[act, act],
        scratch_shapes=[pltpu.VMEM((2, GROUP_HEADS, HEAD_DIM, HEAD_DIM), F32)],
        compiler_params=pltpu.CompilerParams(dimension_semantics=("arbitrary", "arbitrary"),
                                             vmem_limit_bytes=V7X_VMEM_LIMIT_BYTES),
    )(q, k, v, gb, q, k, v, gb)

    g_row = jnp.tile(norm_g.reshape(1, HEAD_DIM).astype(F32), (1, GROUP_HEADS))
    if not apply_post:
        return of, ob, f, g_row
    return pl.pallas_call(
        _gdn_post_kernel,
        name="gdn_post",
        grid=(bsz, nb),
        in_specs=[blk, blk, gate_view, full(g_row)],
        out_specs=blk,
        out_shape=act,
        compiler_params=pltpu.CompilerParams(dimension_semantics=("arbitrary", "arbitrary")),
    )(of, ob, f, g_row)


ROPE_PAIR = DIFF_HALF // 4


def _rope_tables(seq, ctx_len, q_scale):
    n = jnp.arange(seq - ctx_len, dtype=jnp.int32)
    row, col = n // GRID_W, n % GRID_W
    i = jnp.arange(HEAD_DIM)
    grp = (i % DIFF_HALF) // (2 * ROPE_PAIR)
    inv = ROPE_BASE ** (-(i % ROPE_PAIR).astype(F32) / ROPE_PAIR)
    pos = jnp.where(grp[None, :] == 0, row[:, None], col[:, None]).astype(F32)
    ang = pos * inv[None, :]
    sign = jnp.where((i % (2 * ROPE_PAIR)) < ROPE_PAIR, -1.0, 1.0)
    cos = jnp.concatenate([jnp.ones((ctx_len, HEAD_DIM), F32), jnp.cos(ang)], 0)
    sin = jnp.concatenate([jnp.zeros((ctx_len, HEAD_DIM), F32), jnp.sin(ang) * sign], 0)
    cos = jnp.tile(cos, (1, GROUP_HEADS))
    sin = jnp.tile(sin, (1, GROUP_HEADS))
    return jnp.concatenate([cos * q_scale, cos], 1), jnp.concatenate([sin * q_scale, sin], 1)


def _rotate_qk(qk, cos, sin):
    lane = lax.broadcasted_iota(jnp.int32, qk.shape, 1)
    partner = jnp.where(lane % (2 * ROPE_PAIR) < ROPE_PAIR,
                        pltpu.roll(qk, qk.shape[1] - ROPE_PAIR, axis=1), pltpu.roll(qk, ROPE_PAIR, axis=1))
    return qk * cos + partner * sin


def _store_qkv(q, k, v, q_out, k_out, v_out):
    q_out[...] = q.astype(BF16)
    for h in range(GROUP_HEADS):
        k_out[h] = k[:, h * HEAD_DIM:(h + 1) * HEAD_DIM].astype(BF16)
        v_out[h] = v[:, h * HEAD_DIM:(h + 1) * HEAD_DIM].astype(BF16)


_NT = (((1,), (1,)), ((), ()))


def _softmax_pv(s, v):
    m = jnp.max(s, axis=-1, keepdims=True)
    e = jnp.exp(s - m)
    return jnp.dot(e.astype(BF16), v, preferred_element_type=F32) / jnp.sum(e, axis=-1, keepdims=True)


def _diff_attn_kernel(q_ref, k_ref, v_ref, lam_ref, g_ref, o_ref, *, ctx_blocks, ctx_len):
    i = pl.program_id(1)
    lv = lam_ref[...]
    lam_init = lv[4:5, 0:1]
    lam = (jnp.exp(jnp.sum(lv[0:1] * lv[1:2], axis=-1, keepdims=True))
           - jnp.exp(jnp.sum(lv[2:3] * lv[3:4], axis=-1, keepdims=True)) + lam_init)
    lane = lax.broadcasted_iota(jnp.int32, (q_ref.shape[1], HEAD_DIM), 1)

    def attend(n_keys):
        outs = []
        for h in range(GROUP_HEADS):
            qh = q_ref[0, :, h * HEAD_DIM:(h + 1) * HEAD_DIM]
            kh = k_ref[h, 0, 0:n_keys, :]
            vh = v_ref[h, 0, 0:n_keys, :]
            zero = jnp.zeros_like(qh)
            s1 = lax.dot_general(jnp.where(lane < DIFF_HALF, qh, zero), kh, _NT, preferred_element_type=F32)
            s2 = lax.dot_general(jnp.where(lane >= DIFF_HALF, qh, zero), kh, _NT, preferred_element_type=F32)
            o = _softmax_pv(s1, vh) - lam * _softmax_pv(s2, vh)
            o = o * lax.rsqrt(jnp.mean(o * o, axis=-1, keepdims=True) + 1e-6) * g_ref[...] * (1.0 - lam_init)
            outs.append(o)
        o_ref[0] = jnp.concatenate(outs, axis=-1)

    @pl.when(i < ctx_blocks)
    def _():
        attend(ctx_len)

    @pl.when(i >= ctx_blocks)
    def _():
        attend(k_ref.shape[2])


def _diff_attention_pallas(q, k, v, lam_vecs, norm_g, *, ctx_len, lam_init):
    bsz, seq, _ = q.shape
    tq = ROW_BLOCK
    kv_spec = pl.BlockSpec((GROUP_HEADS, 1, seq, HEAD_DIM), lambda b, i: (0, b, 0, 0))
    lam_rows = jnp.concatenate([lam_vecs.astype(F32), jnp.full((1, lam_vecs.shape[1]), lam_init, F32)], 0)
    return pl.pallas_call(
        functools.partial(_diff_attn_kernel, ctx_blocks=ctx_len // tq, ctx_len=ctx_len),
        name="diff_attn",
        grid=(bsz, seq // tq),
        in_specs=[pl.BlockSpec((1, tq, GW), lambda b, i: (b, i, 0)), kv_spec, kv_spec,
                  pl.BlockSpec(lam_rows.shape, lambda b, i: (0, 0)),
                  pl.BlockSpec((1, HEAD_DIM), lambda b, i: (0, 0))],
        out_specs=pl.BlockSpec((1, tq, GW), lambda b, i: (b, i, 0)),
        out_shape=jax.ShapeDtypeStruct((bsz, seq, GW), F32),
        compiler_params=pltpu.CompilerParams(dimension_semantics=("arbitrary", "arbitrary"),
                                             vmem_limit_bytes=V7X_VMEM_LIMIT_BYTES),
    )(q, k, v, lam_rows, norm_g.reshape(1, HEAD_DIM).astype(F32))


NAT_TILE_ROWS = ROW_BLOCK // GRID_W
NAT_SLAB_ROWS = NAT_TILE_ROWS + WIN_H - 1


def _nat_slab_start(tile, n_rows):
    return np.clip(tile * NAT_TILE_ROWS - WIN_H // 2, 0, n_rows - NAT_SLAB_ROWS)


def _nat_bias_tables(rpb, n_rows):
    n_tiles = n_rows // NAT_TILE_ROWS
    nq, nk, w = NAT_TILE_ROWS, NAT_SLAB_ROWS, GRID_W
    cq, ck = np.arange(w)[:, None], np.arange(w)[None, :]
    d_col = np.clip(ck - cq, -(WIN_W - 1), WIN_W - 1) + WIN_W - 1
    col_1h = (d_col.reshape(-1)[:, None] == np.arange(2 * WIN_W - 1)[None, :]).astype(np.float32)
    c0 = np.clip(cq - WIN_W // 2, 0, w - WIN_W)
    col_ok = (ck >= c0) & (ck < c0 + WIN_W)
    tabs = []
    for tile in (0, 1, n_tiles - 1):
        r = tile * nq + np.arange(nq)[:, None]
        kr = _nat_slab_start(tile, n_rows) + np.arange(nk)[None, :]
        rs = np.clip(r - WIN_H // 2, 0, n_rows - WIN_H)
        row_ok = (kr >= rs) & (kr < rs + WIN_H)
        d_row = np.clip(kr - r + WIN_H - 1, 0, 2 * WIN_H - 2)
        row_1h = (d_row.reshape(-1)[:, None] == np.arange(2 * WIN_H - 1)[None, :]).astype(np.float32)
        t = jnp.einsum('pa,hab,cb->hpc', row_1h, rpb.astype(F32), col_1h, precision=HIGHEST)
        t = t.reshape(GROUP_HEADS, nq, nk, w, w).transpose(0, 1, 3, 2, 4)
        ok = row_ok[:, None, :, None] & col_ok[None, :, None, :]
        tabs.append(jnp.where(ok[None], t, NEG_BIG).reshape(GROUP_HEADS, nq * w, nk * w))
    return jnp.stack(tabs)


def _nat_attn_kernel(q_ref, k_ref, v_ref, bias_ref, o_ref, *, ctx_blocks, ctx_len, n_rows):
    i = pl.program_id(1)
    n_slab = NAT_SLAB_ROWS * GRID_W

    def heads(fn):
        o_ref[0] = jnp.concatenate(
            [fn(h, q_ref[0, :, h * HEAD_DIM:(h + 1) * HEAD_DIM]) for h in range(GROUP_HEADS)], axis=-1)

    @pl.when(i < ctx_blocks)
    def _():
        def ctx_only(h, qh):
            s = lax.dot_general(qh, k_ref[h, 0, 0:ctx_len, :], _NT, preferred_element_type=F32)
            return _softmax_pv(s, v_ref[h, 0, 0:ctx_len, :])
        heads(ctx_only)

    @pl.when(i >= ctx_blocks)
    def _():
        tile = i - ctx_blocks
        start = jnp.clip(tile * NAT_TILE_ROWS - WIN_H // 2, 0, n_rows - NAT_SLAB_ROWS)
        off = pl.multiple_of(ctx_len + start * GRID_W, GRID_W)

        def windowed(h, qh):
            s_w = lax.dot_general(qh, k_ref[h, 0, pl.ds(off, n_slab), :], _NT,
                                  preferred_element_type=F32) + bias_ref[0, h]
            s_c = lax.dot_general(qh, k_ref[h, 0, 0:ctx_len, :], _NT, preferred_element_type=F32)
            m = jnp.maximum(jnp.max(s_w, axis=-1, keepdims=True), jnp.max(s_c, axis=-1, keepdims=True))
            e_w = jnp.exp(s_w - m)
            e_c = jnp.exp(s_c - m)
            den = jnp.sum(e_w, axis=-1, keepdims=True) + jnp.sum(e_c, axis=-1, keepdims=True)
            num = (jnp.dot(e_w.astype(BF16), v_ref[h, 0, pl.ds(off, n_slab), :], preferred_element_type=F32)
                   + jnp.dot(e_c.astype(BF16), v_ref[h, 0, 0:ctx_len, :], preferred_element_type=F32))
            return num / den
        heads(windowed)


def _nat_attention_pallas(q, k, v, rpb, *, ctx_len):
    bsz, seq, _ = q.shape
    tq = ROW_BLOCK
    ctx_blocks = ctx_len // tq
    n_rows = (seq - ctx_len) // GRID_W
    n_tiles = n_rows // NAT_TILE_ROWS
    assert n_rows >= NAT_SLAB_ROWS and n_tiles >= 3
    bias = _nat_bias_tables(rpb, n_rows)

    def variant(i):
        tile = i - ctx_blocks
        return jnp.where(tile <= 0, 0, jnp.where(tile >= n_tiles - 1, 2, 1))
    kv_spec = pl.BlockSpec((GROUP_HEADS, 1, seq, HEAD_DIM), lambda b, i: (0, b, 0, 0))
    return pl.pallas_call(
        functools.partial(_nat_attn_kernel, ctx_blocks=ctx_blocks, ctx_len=ctx_len, n_rows=n_rows),
        name="nat_attn",
        grid=(bsz, seq // tq),
        in_specs=[pl.BlockSpec((1, tq, GW), lambda b, i: (b, i, 0)), kv_spec, kv_spec,
                  pl.BlockSpec((1,) + bias.shape[1:], lambda b, i: (variant(i), 0, 0, 0))],
        out_specs=pl.BlockSpec((1, tq, GW), lambda b, i: (b, i, 0)),
        out_shape=jax.ShapeDtypeStruct((bsz, seq, GW), F32),
        compiler_params=pltpu.CompilerParams(dimension_semantics=("arbitrary", "arbitrary"),
                                             vmem_limit_bytes=V7X_VMEM_LIMIT_BYTES),
    )(q, k, v, bias)


N_MOD = 6
MATMUL_ROWS = 512


def _ada_kernel(c_ref, w_ref, b_ref, o_ref):
    c = c_ref[...]
    o_ref[...] = _dot_f32(c * jax.nn.sigmoid(c), w_ref[...]) + b_ref[...]


def _ada_modulation(c, c_ctx, w_ada, b_ada):
    bsz, d = c.shape
    rows = 8 * ((bsz + 1 + 7) // 8)
    cc = jnp.zeros((rows, d), F32).at[:bsz].set(c).at[bsz].set(c_ctx)
    tn = d
    m = pl.pallas_call(
        _ada_kernel,
        name="ada_modulation",
        grid=(w_ada.shape[1] // tn,),
        in_specs=[pl.BlockSpec((rows, d), lambda j: (0, 0)),
                  pl.BlockSpec((d, tn), lambda j: (0, j)),
                  pl.BlockSpec((1, tn), lambda j: (0, j))],
        out_specs=pl.BlockSpec((rows, tn), lambda j: (0, j)),
        out_shape=jax.ShapeDtypeStruct((rows, w_ada.shape[1]), F32),
        compiler_params=pltpu.CompilerParams(dimension_semantics=("arbitrary",)),
    )(cc, w_ada, b_ada.reshape(1, -1))
    lat = m[:bsz].reshape(bsz, 1, N_MOD, d)
    ctx = jnp.broadcast_to(m[bsz].reshape(1, 1, N_MOD, d), (bsz, 1, N_MOD, d))
    return jnp.concatenate([ctx, lat], axis=1)


def _ln(x):
    mu = jnp.mean(x, axis=-1, keepdims=True)
    xc = x - mu
    return xc * lax.rsqrt(jnp.mean(xc * xc, axis=-1, keepdims=True) + LN_EPS)


def _mod_spec(d, ctx_blocks):
    return pl.BlockSpec((1, 1, N_MOD, d), lambda b, i: (b, jnp.where(i < ctx_blocks, 0, 1), 0, 0))


def _in_proj_kernel(h_ref, mod0_ref, mod1_ref, wa_ref, wb_ref, wc_ref, wd_ref, cos_ref, sin_ref,
                    qa_out, ka_out, va_out, pb_out, pc_out, qd_out, kd_out, vd_out, *, seq, ctx_len):
    h = h_ref[...]
    shift, scale = _flat_row_mod(mod0_ref, mod1_ref, pl.program_id(0), h.shape, seq, ctx_len, (0, 1))
    x = (_ln(h) * (1.0 + scale) + shift).astype(BF16)
    pa = jnp.dot(x, wa_ref[...], preferred_element_type=F32)
    qk = _rotate_qk(pa[:, :2 * GW], cos_ref[...], sin_ref[...])
    _store_qkv(qk[:, :GW], qk[:, GW:], pa[:, 2 * GW:], qa_out, ka_out, va_out)
    pb_out[...] = jnp.dot(x, wb_ref[...], preferred_element_type=F32)
    pc_out[...] = jnp.dot(x, wc_ref[...], preferred_element_type=F32)
    pd = jnp.dot(x, wd_ref[...], preferred_element_type=F32)
    _store_qkv(pd[:, :GW] * (HEAD_DIM ** -0.5), pd[:, GW:2 * GW], pd[:, 2 * GW:], qd_out, kd_out, vd_out)


def _in_proj_pallas(x, mod, ws, cos, sin, *, seq, ctx_len):
    m, k = x.shape
    tm = MATMUL_ROWS
    row = lambda n: pl.BlockSpec((tm, n), lambda i: (i, 0))
    hm = pl.BlockSpec((GROUP_HEADS, tm, HEAD_DIM), lambda i: (0, i, 0))
    q_shape = jax.ShapeDtypeStruct((m, GW), BF16)
    hm_shape = jax.ShapeDtypeStruct((GROUP_HEADS, m, HEAD_DIM), BF16)
    f32 = lambda n: jax.ShapeDtypeStruct((m, n), F32)
    return pl.pallas_call(
        functools.partial(_in_proj_kernel, seq=seq, ctx_len=ctx_len),
        name="in_proj",
        grid=(m // tm,),
        in_specs=[row(k)] + _flat_mod_specs(tm, seq, k, 1) + [pl.BlockSpec(w.shape, lambda i: (0, 0)) for w in ws]
                 + [row(2 * GW), row(2 * GW)],
        out_specs=[row(GW), hm, hm, row(ws[1].shape[1]), row(ws[2].shape[1]), row(GW), hm, hm],
        out_shape=[q_shape, hm_shape, hm_shape, f32(ws[1].shape[1]), f32(ws[2].shape[1]), q_shape, hm_shape, hm_shape],
        compiler_params=pltpu.CompilerParams(dimension_semantics=("arbitrary",),
                                             vmem_limit_bytes=V7X_VMEM_LIMIT_BYTES),
    )(x, mod, mod, *ws, cos, sin)


def _post_norm_rows(h, gate, y, g, b):
    return _ln(DN_ALPHA * h + gate * y) * g + b


def _out_proj_kernel(ya_ref, rf_ref, rb_ref, bonus_ref, rg_ref, lng_ref, lnb_ref, of_ref, ob_ref, gate_ref, gn_ref,
                     yd_ref, w_ref, h_ref, mod_ref, g_ref, b_ref, o_ref):
    yb = _rwkv_post_rows(rf_ref[0] + rb_ref[0], bonus_ref[0], rg_ref[0], lng_ref[...], lnb_ref[...])
    yc = _gdn_post_rows(of_ref[0] + ob_ref[0], gate_ref[0], gn_ref[...])
    mix = None
    for n, y in enumerate((ya_ref[0], yb, yc, yd_ref[0])):
        part = jnp.dot(y.astype(BF16), w_ref[n * GW:(n + 1) * GW, :], preferred_element_type=F32)
        mix = part if mix is None else mix + part
    o_ref[0] = _post_norm_rows(h_ref[0], mod_ref[0, 0, 2:3, :], mix, g_ref[...], b_ref[...])


def _out_proj_post_norm(ya, rwkv_parts, gdn_parts, yd, w_out, hs, mod, g, b, *, ctx_len):
    bsz, seq, d = hs.shape
    tr = ROW_BLOCK
    yblk = pl.BlockSpec((1, tr, GW), lambda bb, i: (bb, i, 0))
    blk = pl.BlockSpec((1, tr, d), lambda bb, i: (bb, i, 0))
    row = pl.BlockSpec((1, d), lambda bb, i: (0, 0))
    grow = pl.BlockSpec((1, GW), lambda bb, i: (0, 0))
    gate_view = pl.BlockSpec((1, tr, GW), lambda bb, i: (bb, i, 3))
    return pl.pallas_call(
        _out_proj_kernel,
        name="out_proj_post_norm",
        grid=(bsz, seq // tr),
        in_specs=[yblk] + [yblk] * 4 + [grow, grow] + [yblk, yblk, gate_view, grow] + [yblk]
                 + [pl.BlockSpec(w_out.shape, lambda bb, i: (0, 0)), blk, _mod_spec(d, ctx_len // tr), row, row],
        out_specs=blk,
        out_shape=jax.ShapeDtypeStruct((bsz, seq, d), F32),
        compiler_params=pltpu.CompilerParams(dimension_semantics=("arbitrary", "arbitrary")),
    )(ya, *rwkv_parts, *gdn_parts, yd, w_out, hs, mod, g.reshape(1, d), b.reshape(1, d))


def kernel(x, c, ctx, c_ctx, w_ada, b_ada, w_in, w_out, ln_mix_g, ln_mix_b, ln_ffn_g, ln_ffn_b, diff_lam, diff_norm_g, rwkv_mu, rwkv_w0, rwkv_w2, rwkv_a0, rwkv_a2, rwkv_g2, rwkv_kk, rwkv_ka, rwkv_rk, rwkv_ln_g, rwkv_ln_b, gdn_conv, gdn_a_log, gdn_dt_bias, gdn_norm_g, nat_rpb, peer_wq, peer_keys, peer_u, peer_v):
    dtype = x.dtype
    bsz, ctx_len = ctx.shape[0], ctx.shape[1]
    hs = jnp.concatenate([ctx, x], axis=1)
    seq = hs.shape[1]
    col_sizes = [ATTN_COLS, RWKV_COLS, GDN_COLS, ATTN_COLS]
    cos_a, sin_a = (jnp.tile(t, (bsz, 1)) for t in _rope_tables(seq, ctx_len, DIFF_HALF ** -0.5))
    col_offs = np.cumsum([0] + col_sizes)
    d_model = hs.shape[2]
    for l in range(DEPTH):
        lam_init = 0.8 - 0.6 * math.exp(-0.3 * l)
        mod = _ada_modulation(c, c_ctx, w_ada[l], b_ada[l])
        w_in_b = w_in[l].astype(BF16)
        w_groups = [w_in_b[:, col_offs[n]:col_offs[n + 1]] for n in range(4)]
        w_groups[2] = jnp.pad(w_groups[2], ((0, 0), (0, GDN_PADDED_COLS - GDN_COLS)))
        qa, ka, va, pb, pc, qd, kd, vd = _in_proj_pallas(hs.reshape(bsz * seq, d_model), mod, w_groups, cos_a, sin_a,
                                                         seq=seq, ctx_len=ctx_len)
        rows3 = lambda p: p.reshape(bsz, seq, p.shape[1])
        heads4 = lambda p: p.reshape(GROUP_HEADS, bsz, seq, HEAD_DIM)
        qa, pb, pc, qd = rows3(qa), rows3(pb), rows3(pc), rows3(qd)
        ka, va, kd, vd = heads4(ka), heads4(va), heads4(kd), heads4(vd)
        ya = _diff_attention_pallas(qa, ka, va, diff_lam[l], diff_norm_g[l], ctx_len=ctx_len, lam_init=lam_init)
        yb_parts = _rwkv7_pallas(pb, rwkv_mu[l], rwkv_w0[l], rwkv_w2[l], rwkv_a0[l], rwkv_a2[l], rwkv_g2[l],
                                 rwkv_kk[l], rwkv_ka[l], rwkv_rk[l], rwkv_ln_g[l], rwkv_ln_b[l], ctx_len=ctx_len,
                                 apply_post=False)
        yc_parts = _gated_deltanet_pallas(pc, gdn_conv[l], gdn_a_log[l], gdn_dt_bias[l], gdn_norm_g[l],
                                          ctx_len=ctx_len, apply_post=False)
        yd = _nat_attention_pallas(qd, kd, vd, nat_rpb[l], ctx_len=ctx_len)
        hs = _out_proj_post_norm(ya, yb_parts, yc_parts, yd, w_out[l].astype(BF16), hs, mod, ln_mix_g[l],
                                 ln_mix_b[l], ctx_len=ctx_len)
        wq_b = peer_wq[l].astype(BF16)
        keys_b = peer_keys[l].reshape(2 * PEER_HEADS, PEER_KEYS, PEER_HALF).astype(BF16)
        u_b = peer_u[l].astype(BF16)
        vt_b = peer_v[l].astype(BF16).T
        ffn_ctx = ctx_len
        if l == DEPTH - 1:
            hs, ffn_ctx = hs[:, ctx_len:], 0
        hs = _peer_ffn(hs, mod, ln_ffn_g[l], ln_ffn_b[l], wq_b, keys_b, u_b, vt_b, ctx_len=ffn_ctx)
    return hs.astype(dtype)
```

```python
import functools
import math

import jax
import jax.numpy as jnp
import numpy as np
from jax import lax
from jax.experimental import pallas as pl
from jax.experimental.pallas import tpu as pltpu

D_MODEL = 1024
DEPTH = 2
GRID_W = 64
HEAD_DIM = 64
N_GROUPS = 4
GROUP_HEADS = D_MODEL // (N_GROUPS * HEAD_DIM)
GW = GROUP_HEADS * HEAD_DIM
DIFF_HALF = HEAD_DIM // 2
ROPE_BASE = 10000.0
DECAY_LORA = 64
ICLR_LORA = 64
GATE_LORA = 128
RWKV_GN_EPS = 64e-5
RWKV_COLS = 3 * GW + 2 * DECAY_LORA + 2 * ICLR_LORA + GATE_LORA
GDN_CHUNK = 64
GDN_COLS = 4 * GW + 4 * GROUP_HEADS
WIN_H = 8
WIN_W = 16
ATTN_COLS = 3 * GW
PEER_HEADS = 8
PEER_KEYS = 128
PEER_EXPERTS = PEER_KEYS * PEER_KEYS
PEER_QDIM = 256
PEER_HALF = PEER_QDIM // 2
PEER_TOPK = 16
DN_ALPHA = (2 * DEPTH) ** 0.25
LN_EPS = 1e-5

F32 = jnp.float32
BF16 = jnp.bfloat16

V7X_VMEM_LIMIT_BYTES = 56 * 1024 * 1024
NEG_BIG = -3.0e38


PEER_STAT_ROWS = 4 * PEER_HEADS
LOG2E = 1.4426950408889634
PEER_POW_HEADS = 4
assert PEER_TOPK == 16


def _flat_mod_specs(tm, seq, d, grid_rank):
    first = lambda i, *_: ((i * tm) // seq, 0, 0, 0)
    last = lambda i, *_: ((i * tm + tm - 1) // seq, 0, 0, 0)
    assert grid_rank in (1, 2) and tm <= seq
    return [pl.BlockSpec((1, 2, N_MOD, d), first), pl.BlockSpec((1, 2, N_MOD, d), last)]


def _flat_row_mod(mod0_ref, mod1_ref, i, shape, seq, ctx_len, rows):
    tm = shape[0]
    r0 = i * tm
    b0 = r0 // seq
    boundary = (b0 + 1) * seq
    idx = r0 + lax.broadcasted_iota(jnp.int32, shape, 0)
    second = idx >= boundary
    is_ctx = (idx - jnp.where(second, boundary, b0 * seq)) < ctx_len
    out = []
    for k in rows:
        v = jnp.where(second, mod1_ref[0, 1, k:k + 1, :], mod0_ref[0, 1, k:k + 1, :])
        out.append(jnp.where(is_ctx, mod0_ref[0, 0, k:k + 1, :], v) if ctx_len else v)
    return out


def _topk_rows(x, k):
    q = x.shape[0] // 4
    lv = [x[i * q:(i + 1) * q] for i in range(4)]
    for a, b in ((0, 1), (2, 3), (0, 2), (1, 3), (1, 2)):
        lv[a], lv[b] = jnp.maximum(lv[a], lv[b]), jnp.minimum(lv[a], lv[b])
    rows = []
    for i in range(k):
        m = jnp.max(lv[0], axis=0, keepdims=True)
        rows.append(m)
        if i + 1 < k:
            hit = lv[0] == m
            lv = [jnp.where(hit, lv[n + 1], lv[n]) for n in range(3)] + [jnp.where(hit, NEG_BIG, lv[3])]
    return rows


def _peer_score_kernel(h_ref, mod0_ref, mod1_ref, wq_ref, keys_ref, x_out, s_ref, st_ref, *, seq, ctx_len):
    h = h_ref[...]
    shift, scale = _flat_row_mod(mod0_ref, mod1_ref, pl.program_id(0), h.shape, seq, ctx_len, (3, 4))
    x = (_ln(h) * (1.0 + scale) + shift).astype(BF16)
    x_out[...] = x
    q = jnp.dot(x, wq_ref[...], preferred_element_type=F32).astype(BF16)
    stats = []
    for h in range(PEER_HEADS):
        tops = []
        for p in range(2):
            hp = 2 * h + p
            s_t = lax.dot_general(keys_ref[hp], q[:, hp * PEER_HALF:(hp + 1) * PEER_HALF],
                                  (((1,), (1,)), ((), ())), preferred_element_type=F32)
            s_ref[hp] = s_t
            tops.append(_topk_rows(s_t, PEER_TOPK + 1))
        a, b = tops
        pad = [jnp.full_like(a[0], NEG_BIG)] * 7
        b_head = jnp.concatenate(b[:8], axis=0)
        cand = jnp.concatenate([a[0] + jnp.concatenate(b + pad, axis=0)]
                               + [a[i] + b_head for i in range(1, 8)]
                               + [jnp.concatenate(a[8:] + pad, axis=0) + b[0]], axis=0)
        best_cand = _topk_rows(cand, PEER_TOPK + 1)
        kth, runner_up = best_cand[PEER_TOPK - 1], best_cand[PEER_TOPK]
        best = a[0] + b[0]
        z = jnp.sum(jnp.where(cand >= kth, jnp.exp(cand - best), 0.0), axis=0, keepdims=True)
        stats += [0.5 * (kth + runner_up), tops[0][0], tops[1][0], 1.0 / z]
    st_ref[...] = jnp.concatenate(stats, axis=0)


def _peer_expert_kernel(x_ref, s_ref, st_ref, u_ref, vt_ref, h_ref, mod0_ref, mod1_ref, g_ref, b_ref, o_ref,
                        e_ref, thr_ref, lin_ref, acc_ref, w_ref, *, rows_per_step, n_chunks, seq, ctx_len):
    j = pl.program_id(1)
    cur = lax.rem(j, 2)

    @pl.when(j == 0)
    def _():
        acc_ref[...] = jnp.zeros_like(acc_ref)
        w_ref[1] = jnp.zeros(w_ref.shape[1:], w_ref.dtype)
        for h in range(PEER_HEADS):
            a0 = st_ref[4 * h + 1:4 * h + 2, :]
            b0 = st_ref[4 * h + 2:4 * h + 3, :]
            rz = st_ref[4 * h + 3:4 * h + 4, :]
            l1 = (s_ref[2 * h] - a0 + jnp.log(rz)) * LOG2E
            l2 = (s_ref[2 * h + 1] - b0) * LOG2E
            e_ref[2 * h] = l1 if h < PEER_POW_HEADS else jnp.exp2(l1)
            e_ref[2 * h + 1] = l2
            if h >= PEER_POW_HEADS:
                lin_ref[h - PEER_POW_HEADS] = jnp.exp2(l2)
            thr_ref[h] = (st_ref[4 * h:4 * h + 1, :] - s_ref[2 * h] - b0) * LOG2E

    @pl.when(j < n_chunks)
    def _():
        slab = 2 * PEER_KEYS
        n_slabs = rows_per_step * PEER_KEYS // slab
        act_slab = lambda n: lax.dot_general(u_ref[n * slab:(n + 1) * slab, :], x_ref[...], (((1,), (1,)), ((), ())),
                                             preferred_element_type=F32)
        acts = [act_slab(n) for n in range(n_slabs)]
        acc_ref[...] += jnp.dot(vt_ref[...], w_ref[1 - cur], preferred_element_type=F32)
        assert rows_per_step % 8 == 0
        for il in range(rows_per_step):
            base = pl.multiple_of(j * rows_per_step + (il // 8) * 8, 8)
            r = il % 8
            o = il * PEER_KEYS
            act = acts[o // slab][o % slab:o % slab + PEER_KEYS]
            act = 0.5 * act * (1.0 + lax.erf(act * (2.0 ** -0.5)))
            gate = None
            for h in range(PEER_HEADS):
                l2 = e_ref[2 * h + 1]
                thr_row = thr_ref[h, pl.ds(base, 8), :][r:r + 1, :]
                half1 = e_ref[2 * h, pl.ds(base, 8), :][r:r + 1, :]
                pair = jnp.exp2(l2 + half1) if h < PEER_POW_HEADS else lin_ref[h - PEER_POW_HEADS] * half1
                term = jnp.where(l2 >= thr_row, pair, 0.0)
                gate = term if gate is None else gate + term
            w_ref[cur, o:o + PEER_KEYS, :] = (act * gate).astype(BF16)

    @pl.when(j == n_chunks)
    def _():
        ffn = (acc_ref[...] + jnp.dot(vt_ref[...], w_ref[1 - cur], preferred_element_type=F32)).T
        h = h_ref[...]
        gate, = _flat_row_mod(mod0_ref, mod1_ref, pl.program_id(0), h.shape, seq, ctx_len, (5,))
        o_ref[...] = _post_norm_rows(h, gate, ffn, g_ref[...], b_ref[...])


def _peer_pallas(h, mod, g, b, w_q, keys, u_tab, vt_tab, *, seq, ctx_len, tb, rows_per_step):
    n_tok, d = h.shape
    nhp = 2 * PEER_HEADS
    x, s, st = pl.pallas_call(
        functools.partial(_peer_score_kernel, seq=seq, ctx_len=ctx_len),
        name="peer_scores",
        grid=(n_tok // tb,),
        in_specs=[pl.BlockSpec((tb, d), lambda i: (i, 0))] + _flat_mod_specs(tb, seq, d, 1)
                 + [pl.BlockSpec(w_q.shape, lambda i: (0, 0)),
                    pl.BlockSpec(keys.shape, lambda i: (0, 0, 0))],
        out_specs=[pl.BlockSpec((tb, d), lambda i: (i, 0)),
                   pl.BlockSpec((nhp, PEER_KEYS, tb), lambda i: (0, 0, i)),
                   pl.BlockSpec((PEER_STAT_ROWS, tb), lambda i: (0, i))],
        out_shape=[jax.ShapeDtypeStruct((n_tok, d), BF16),
                   jax.ShapeDtypeStruct((nhp, PEER_KEYS, n_tok), F32),
                   jax.ShapeDtypeStruct((PEER_STAT_ROWS, n_tok), F32)],
        compiler_params=pltpu.CompilerParams(dimension_semantics=("arbitrary",),
                                             vmem_limit_bytes=V7X_VMEM_LIMIT_BYTES),
    )(h, mod, mod, w_q, keys)
    ec = rows_per_step * PEER_KEYS
    n_chunks = PEER_EXPERTS // ec
    row = pl.BlockSpec((1, d), lambda i, j: (0, 0))
    return pl.pallas_call(
        functools.partial(_peer_expert_kernel, rows_per_step=rows_per_step, n_chunks=n_chunks, seq=seq,
                          ctx_len=ctx_len),
        name="peer_experts",
        grid=(n_tok // tb, n_chunks + 1),
        in_specs=[pl.BlockSpec((tb, d), lambda i, j: (i, 0)),
                  pl.BlockSpec((nhp, PEER_KEYS, tb), lambda i, j: (0, 0, i)),
                  pl.BlockSpec((PEER_STAT_ROWS, tb), lambda i, j: (0, i)),
                  pl.BlockSpec((ec, d), lambda i, j: (jnp.minimum(j, n_chunks - 1), 0)),
                  pl.BlockSpec((d, ec), lambda i, j: (0, jnp.maximum(j - 1, 0))),
                  pl.BlockSpec((tb, d), lambda i, j: (i, 0))] + _flat_mod_specs(tb, seq, d, 2) + [row, row],
        out_specs=pl.BlockSpec((tb, d), lambda i, j: (i, 0)),
        out_shape=jax.ShapeDtypeStruct((n_tok, d), F32),
        scratch_shapes=[pltpu.VMEM((nhp, PEER_KEYS, tb), F32),
                        pltpu.VMEM((PEER_HEADS, PEER_KEYS, tb), F32),
                        pltpu.VMEM((PEER_HEADS - PEER_POW_HEADS, PEER_KEYS, tb), F32),
                        pltpu.VMEM((d, tb), F32),
                        pltpu.VMEM((2, ec, tb), BF16)],
        compiler_params=pltpu.CompilerParams(dimension_semantics=("arbitrary", "arbitrary"),
                                             vmem_limit_bytes=V7X_VMEM_LIMIT_BYTES),
    )(x, s, st, u_tab, vt_tab, h, mod, mod, g.reshape(1, d).astype(F32), b.reshape(1, d).astype(F32))


def _peer_ffn(hs, mod, g, b, w_q, keys, u_tab, vt_tab, *, ctx_len):
    bsz, t, d = hs.shape
    n_tok = bsz * t
    tb = 512 if n_tok % 512 == 0 else 256
    out = _peer_pallas(hs.reshape(n_tok, d), mod, g, b, w_q, keys, u_tab, vt_tab, seq=t, ctx_len=ctx_len, tb=tb,
                       rows_per_step=16)
    return out.reshape(bsz, t, d)


ROW_BLOCK = 256
HIGHEST = lax.Precision.HIGHEST


def _dot_f32(a, b):
    return jnp.dot(a, b, precision=HIGHEST, preferred_element_type=F32)


def _segment_ones(n, seg, dtype):
    r = lax.broadcasted_iota(jnp.int32, (n, n), 0) // seg
    c = lax.broadcasted_iota(jnp.int32, (n, n), 1) // seg
    return jnp.where(r == c, 1.0, 0.0).astype(dtype)


def _shifted_rows(x, prev_row, next_row):
    t = x.shape[0]
    rows = lax.broadcasted_iota(jnp.int32, x.shape, 0)
    xm = jnp.where(rows == 0, prev_row, pltpu.roll(x, 1, axis=0))
    xp = jnp.where(rows == t - 1, next_row, pltpu.roll(x, t - 1, axis=0))
    return xm, xp


def _segment_edge_flags(i, n_blocks, ctx_blocks):
    is_start = jnp.logical_or(i == 0, i == ctx_blocks)
    is_end = jnp.logical_or(i == ctx_blocks - 1, i == n_blocks - 1)
    return jnp.where(is_start, 0.0, 1.0), jnp.where(is_end, 0.0, 1.0)


def _halo_specs(width, tr):
    g = tr // 8
    prev = pl.BlockSpec((1, 8, width), lambda b, i: (b, jnp.maximum(i * g - 1, 0), 0))
    nxt = lambda n_groups: pl.BlockSpec((1, 8, width), lambda b, i: (b, jnp.minimum((i + 1) * g, n_groups - 1), 0))
    return prev, nxt


def _softplus(z):
    return jnp.maximum(z, 0.0) + jnp.log1p(jnp.exp(-jnp.abs(z)))


def _rwkv_prep_kernel(x_ref, xprev_ref, xnext_ref, mu_ref, w0_ref, w2_ref, a0_ref, a2_ref, g2_ref,
                      kk_ref, ka_ref, rk_ref,
                      r_out, v_out, kkn_out, g_out, bonus_out, w_out, b_out, kt_out, *, ctx_blocks):
    i = pl.program_id(1)
    keep_prev, keep_next = _segment_edge_flags(i, pl.num_programs(1), ctx_blocks)
    x = x_ref[0]
    xm, xp = _shifted_rows(x, xprev_ref[0, 7:8, :] * keep_prev, xnext_ref[0, 0:1, :] * keep_next)
    mu0 = mu_ref[0:1, :]
    mu1 = mu_ref[1:2, :]
    f = xm * mu0 + x * (1.0 - mu0 - mu1) + xp * mu1
    r = f[:, 0:GW]
    k = f[:, GW:2 * GW]
    v = f[:, 2 * GW:3 * GW]
    o = 3 * GW
    wd = jnp.tanh(f[:, o:o + 2 * DECAY_LORA])
    ad = f[:, o + 2 * DECAY_LORA:o + 2 * DECAY_LORA + 2 * ICLR_LORA]
    gd = f[:, o + 2 * DECAY_LORA + 2 * ICLR_LORA:]
    w_raw = w0_ref[...] + _dot_f32(wd, w2_ref[...])
    log_decay = -jnp.exp(-_softplus(-w_raw) - 0.5)
    a = jax.nn.sigmoid(a0_ref[...] + _dot_f32(ad, a2_ref[...]))
    g = _dot_f32(jax.nn.sigmoid(gd), g2_ref[...])
    head_sum = _segment_ones(GW, HEAD_DIM, F32)
    kx = k * kk_ref[...]
    kkn = kx * lax.rsqrt(_dot_f32(kx * kx, head_sum) + 1e-6)
    kd_sum = jnp.zeros_like(k)
    for d in range(2):
        a_d = a[:, d * GW:(d + 1) * GW]
        kd = k * (1.0 + (a_d - 1.0) * ka_ref[...])
        kd_sum = kd_sum + kd
        w_out[d, 0] = log_decay[:, d * GW:(d + 1) * GW]
        b_out[d, 0] = kkn * a_d
        kt_out[d, 0] = kd
    r_out[0] = r
    v_out[0] = v
    kkn_out[0] = kkn
    g_out[0] = g
    bonus_out[0] = _dot_f32(r * kd_sum * rk_ref[...], head_sum) * v


RWKV_CHUNK = 64


def _rwkv_chunk_kernel(rf_ref, vf_ref, kkf_ref, wf_ref, bf_ref, ktf_ref,
                       rb_ref, vb_ref, kkb_ref, wb_ref, bb_ref, ktb_ref, yf_ref, yb_ref, st_ref):
    i = pl.program_id(1)
    c = RWKV_CHUNK
    n_chunks = rf_ref.shape[1] // c

    @pl.when(i == 0)
    def _():
        st_ref[...] = jnp.zeros_like(st_ref)

    rr = lax.broadcasted_iota(jnp.int32, (c, c), 0)
    ss = lax.broadcasted_iota(jnp.int32, (c, c), 1)
    eye = rr == ss
    ones_cc = jnp.ones((c, c), F32)
    incl = (rr >= ss, rr <= ss)
    strict = (rr > ss, rr < ss)
    levels = []
    blk = 1
    while blk < c:
        levels.append(jnp.logical_and(rr // (2 * blk) == ss // (2 * blk), rr // blk != ss // blk))
        blk *= 2
    bdot = lambda x, y: jnp.dot(x.astype(BF16), y.astype(BF16), preferred_element_type=F32)
    bdot_nt = lambda x, y: lax.dot_general(x.astype(BF16), y.astype(BF16), _NT, preferred_element_type=F32)
    bdot_tn = lambda x, y: lax.dot_general(x.astype(BF16), y.astype(BF16), (((0,), (0,)), ((), ())),
                                           preferred_element_type=F32)
    dirs = ((rf_ref, vf_ref, kkf_ref, wf_ref, bf_ref, ktf_ref, yf_ref),
            (rb_ref, vb_ref, kkb_ref, wb_ref, bb_ref, ktb_ref, yb_ref))
    heads = [(d, h) for d in range(2) for h in range(GROUP_HEADS)]
    row0 = lambda cc, d: (cc if d == 0 else n_chunks - 1 - cc) * c
    rows = lambda cc, d: slice(row0(cc, d), row0(cc, d) + c)

    logw = {(cc, d): dirs[d][3][0, 0, rows(cc, d), :] for cc in range(n_chunks) for d in range(2)}
    cum = {key: _dot_f32(jnp.where(incl[key[1]], 1.0, 0.0), lw) for key, lw in logw.items()}
    tot = {key: _dot_f32(ones_cc, lw) for key, lw in logw.items()}
    st = {}
    for cc in range(n_chunks):
        for d, h in heads:
            cols = slice(h * HEAD_DIM, (h + 1) * HEAD_DIM)
            r_ref, v_ref, kk_ref, _, b_ref, kt_ref, _ = dirs[d]
            lc = cum[cc, d][:, cols]
            grow = jnp.exp(-lc)
            st[cc, d, h] = dict(
                v=v_ref[0, rows(cc, d), cols],
                kap=kk_ref[0, rows(cc, d), cols] * jnp.exp(lc - logw[cc, d][:, cols]),
                bh=b_ref[0, 0, rows(cc, d), cols] * grow,
                kh=kt_ref[0, 0, rows(cc, d), cols] * grow,
                rh=r_ref[0, rows(cc, d), cols] * jnp.exp(lc),
                scale=jnp.exp(_dot_f32(jnp.where(eye, tot[cc, d][:, cols], 0.0), ones_cc)))
    for (cc, d, h), x in st.items():
        gram = bdot_nt(jnp.concatenate([x["kap"], x["rh"]], axis=0), jnp.concatenate([x["bh"], x["kh"]], axis=0))
        x["a_b"] = jnp.where(strict[d], gram[:c, :c], 0.0)
        x["a_k"] = jnp.where(strict[d], gram[:c, c:], 0.0)
        x["b_b"] = jnp.where(incl[d], gram[c:, :c], 0.0)
        x["b_k"] = jnp.where(incl[d], gram[c:, c:], 0.0)
        x["inv"] = jnp.where(eye, 1.0, 0.0) - jnp.where(levels[0], x["a_b"], 0.0)
    for lvl in levels[1:]:
        for x in st.values():
            x["t"] = bdot(jnp.where(lvl, x["a_b"], 0.0), x["inv"])
        for x in st.values():
            x["inv"] = x["inv"] - bdot(x["inv"], x["t"])
    for x in st.values():
        x["akv"] = bdot(x["a_k"], x["v"])
        x["y_const"] = bdot(x["b_k"], x["v"])
        x["kv"] = bdot_tn(x["kh"], x["v"])
    for x in st.values():
        both = bdot(x["inv"], jnp.concatenate([x["kap"], x["akv"]], axis=-1))
        x["p_state"], x["p_const"] = both[:, :HEAD_DIM], both[:, HEAD_DIM:]
    state = {key: st_ref[key[0], key[1]] for key in heads}
    for cc in range(n_chunks):
        cur = [(key, st[(cc,) + key]) for key in heads]
        for key, x in cur:
            x["p"] = bdot(x["p_state"], state[key]) + x["p_const"]
            x["y0"] = bdot(x["rh"], state[key]) + x["y_const"]
        for key, x in cur:
            x["y"] = x["y0"] - bdot(x["b_b"], x["p"])
            state[key] = (state[key] + x["kv"] - bdot_tn(x["bh"], x["p"])) * x["scale"]
        for d in range(2):
            dirs[d][6][0, rows(cc, d), :] = jnp.concatenate([x["y"] for (dd, h), x in cur if dd == d], axis=-1)
    for d, h in heads:
        st_ref[d, h] = state[d, h]


def _rwkv_post_rows(y, bonus, g, ln_g, ln_b):
    head_mean = _segment_ones(GW, HEAD_DIM, F32) * (1.0 / HEAD_DIM)
    yc = y - _dot_f32(y, head_mean)
    var = _dot_f32(yc * yc, head_mean)
    return (yc * lax.rsqrt(var + RWKV_GN_EPS) * ln_g + ln_b + bonus) * g


def _rwkv_post_kernel(yf_ref, yb_ref, bonus_ref, g_ref, lng_ref, lnb_ref, o_ref):
    o_ref[0] = _rwkv_post_rows(yf_ref[0] + yb_ref[0], bonus_ref[0], g_ref[0], lng_ref[...], lnb_ref[...])


def _block_diag2(m):
    z = jnp.zeros_like(m[0])
    return jnp.concatenate([jnp.concatenate([m[0], z], 1), jnp.concatenate([z, m[1]], 1)], 0)


def _rwkv7_pallas(f, mu, w0, w2, a0, a2, g2, k_k, k_a, r_k, ln_g, ln_b, *, ctx_len, apply_post=True):
    bsz, seq, cols = f.shape
    tr = ROW_BLOCK
    assert ctx_len % tr == 0 and seq % tr == 0
    nb, ctx_blocks = seq // tr, ctx_len // tr
    prev_spec, next_spec = _halo_specs(cols, tr)
    row2 = lambda a: a.reshape(1, -1).astype(F32)
    full = lambda a: pl.BlockSpec(a.shape, lambda b, i: (0,) * a.ndim)
    params = [mu, row2(w0), _block_diag2(w2), row2(a0), _block_diag2(a2), g2, row2(k_k), row2(k_a), row2(r_k)]
    act = jax.ShapeDtypeStruct((bsz, seq, GW), F32)
    act2 = jax.ShapeDtypeStruct((2, bsz, seq, GW), F32)
    blk = pl.BlockSpec((1, tr, GW), lambda b, i: (b, i, 0))
    blk2 = pl.BlockSpec((2, 1, tr, GW), lambda b, i: (0, b, i, 0))
    r, v, kkn, g, bonus, w, bb, kt = pl.pallas_call(
        functools.partial(_rwkv_prep_kernel, ctx_blocks=ctx_blocks),
        name="rwkv_prep",
        grid=(bsz, nb),
        in_specs=[pl.BlockSpec((1, tr, cols), lambda b, i: (b, i, 0)), prev_spec, next_spec(seq // 8)]
                 + [full(p) for p in params],
        out_specs=[blk] * 5 + [blk2] * 3,
        out_shape=[act] * 5 + [act2] * 3,
        compiler_params=pltpu.CompilerParams(dimension_semantics=("arbitrary", "arbitrary"),
                                             vmem_limit_bytes=V7X_VMEM_LIMIT_BYTES),
    )(f, f, f, *params)

    def bwd_block(c):
        return jnp.where(c < ctx_blocks, ctx_blocks - 1 - c, nb - 1 - (c - ctx_blocks))
    assert tr % RWKV_CHUNK == 0 and RWKV_CHUNK == HEAD_DIM
    fwd = pl.BlockSpec((1, tr, GW), lambda b, c: (b, c, 0))
    bwd = pl.BlockSpec((1, tr, GW), lambda b, c: (b, bwd_block(c), 0))
    fwd_d = pl.BlockSpec((1, 1, tr, GW), lambda b, c: (0, b, c, 0))
    bwd_d = pl.BlockSpec((1, 1, tr, GW), lambda b, c: (1, b, bwd_block(c), 0))
    yf, yb = pl.pallas_call(
        _rwkv_chunk_kernel,
        name="rwkv_chunks",
        grid=(bsz, nb),
        in_specs=[fwd, fwd, fwd, fwd_d, fwd_d, fwd_d, bwd, bwd, bwd, bwd_d, bwd_d, bwd_d],
        out_specs=[fwd, bwd],
        out_shape=[act, act],
        scratch_shapes=[pltpu.VMEM((2, GROUP_HEADS, HEAD_DIM, HEAD_DIM), F32)],
        compiler_params=pltpu.CompilerParams(dimension_semantics=("arbitrary", "arbitrary"),
                                             vmem_limit_bytes=V7X_VMEM_LIMIT_BYTES),
    )(r, v, kkn, w, bb, kt, r, v, kkn, w, bb, kt)
    if not apply_post:
        return yf, yb, bonus, g, row2(ln_g), row2(ln_b)

    return pl.pallas_call(
        _rwkv_post_kernel,
        name="rwkv_post",
        grid=(bsz, nb),
        in_specs=[blk, blk, blk, blk, full(row2(ln_g)), full(row2(ln_b))],
        out_specs=blk,
        out_shape=act,
        compiler_params=pltpu.CompilerParams(dimension_semantics=("arbitrary", "arbitrary")),
    )(yf, yb, bonus, g, row2(ln_g), row2(ln_b))


GDN_GATE_LANES = 128
GDN_PADDED_COLS = 4 * GW + GDN_GATE_LANES


def _gdn_prep_kernel(x_ref, xprev_ref, xnext_ref, ab_ref, conv_ref, alog_ref, dtb_ref,
                     q_out, k_out, v_out, gb_out, *, ctx_blocks):
    i = pl.program_id(1)
    keep_prev, keep_next = _segment_edge_flags(i, pl.num_programs(1), ctx_blocks)
    x = x_ref[0]
    xm, xp = _shifted_rows(x, xprev_ref[0, 7:8, :] * keep_prev, xnext_ref[0, 0:1, :] * keep_next)
    y = xm * conv_ref[0:1, :] + x * conv_ref[1:2, :] + xp * conv_ref[2:3, :]
    y = y * jax.nn.sigmoid(y)
    head_sum = _segment_ones(GW, HEAD_DIM, F32)
    q = y[:, 0:GW]
    k = y[:, GW:2 * GW]
    q_out[0] = q * lax.rsqrt(_dot_f32(q * q, head_sum) + 1e-6) * (HEAD_DIM ** -0.5)
    k_out[0] = k * lax.rsqrt(_dot_f32(k * k, head_sum) + 1e-6)
    v_out[0] = y[:, 2 * GW:3 * GW]
    ab = ab_ref[0]
    lane = lax.broadcasted_iota(jnp.int32, ab.shape, 1)
    log_alpha = -jnp.exp(alog_ref[...]) * _softplus(ab + dtb_ref[...])
    gb_out[0] = jnp.where(lane < 2 * GROUP_HEADS, log_alpha, jax.nn.sigmoid(ab))


def _gdn_chunk_kernel(qf_ref, kf_ref, vf_ref, gf_ref, qb_ref, kb_ref, vb_ref, gb_ref, of_ref, ob_ref, st_ref):
    i = pl.program_id(1)
    c = GDN_CHUNK
    n_chunks = qf_ref.shape[1] // c

    @pl.when(i == 0)
    def _():
        st_ref[...] = jnp.zeros_like(st_ref)

    r = lax.broadcasted_iota(jnp.int32, (c, c), 0)
    s = lax.broadcasted_iota(jnp.int32, (c, c), 1)
    eye = r == s
    ones_cc = jnp.ones((c, c), F32)
    incl = (r >= s, r <= s)
    strict = (r > s, r < s)
    levels = []
    b = 1
    while b < c:
        levels.append(jnp.logical_and(r // (2 * b) == s // (2 * b), r // b != s // b))
        b *= 2
    dirs = ((qf_ref, kf_ref, vf_ref, gf_ref, of_ref), (qb_ref, kb_ref, vb_ref, gb_ref, ob_ref))

    bdot = lambda x, y: jnp.dot(x.astype(BF16), y.astype(BF16), preferred_element_type=F32)
    bdot_nt = lambda x, y: lax.dot_general(x.astype(BF16), y.astype(BF16), _NT, preferred_element_type=F32)
    bdot_tn = lambda x, y: lax.dot_general(x.astype(BF16), y.astype(BF16), (((0,), (0,)), ((), ())),
                                           preferred_element_type=F32)

    heads = [(d, h) for d in range(2) for h in range(GROUP_HEADS)]
    row0 = lambda cc, d: (cc if d == 0 else n_chunks - 1 - cc) * c
    gates = {(cc, d): dirs[d][3][0, row0(cc, d):row0(cc, d) + c, :] for cc in range(n_chunks) for d in range(2)}
    cum = {key: _dot_f32(jnp.where(incl[key[1]], 1.0, 0.0), g) for key, g in gates.items()}
    tot = {key: _dot_f32(ones_cc, g) for key, g in gates.items()}
    chains = [(cc, d, h) for cc in range(n_chunks) for d, h in heads]
    st = {}
    for cc, d, h in chains:
        cols = slice(h * HEAD_DIM, (h + 1) * HEAD_DIM)
        lg = d * GROUP_HEADS + h
        q, k, v = (dirs[d][n][0, row0(cc, d):row0(cc, d) + c, cols] for n in range(3))
        gc = cum[cc, d][:, lg:lg + 1]
        st[cc, d, h] = dict(q=q, k=k, v=v, gc=gc, gt=tot[cc, d][:, lg:lg + 1],
                            beta=gates[cc, d][:, 2 * GROUP_HEADS + lg:2 * GROUP_HEADS + lg + 1],
                            gc_row=_dot_f32(ones_cc, jnp.where(eye, gc, 0.0)))
    for (cc, d, h), x in st.items():
        x["decay"] = jnp.exp(jnp.where(incl[d], x["gc"] - x["gc_row"], NEG_BIG))
        x["kb"] = x["k"] * x["beta"]
        gram = bdot_nt(jnp.concatenate([x["kb"], x["q"]], axis=0), x["k"])
        x["a"] = jnp.where(strict[d], gram[:c] * x["decay"], 0.0)
        x["qk"] = jnp.where(incl[d], gram[c:] * x["decay"], 0.0)
        x["inv"] = jnp.where(eye, 1.0, 0.0) - jnp.where(levels[0], x["a"], 0.0)
    for lvl in levels[1:]:
        for x in st.values():
            x["t"] = bdot(jnp.where(lvl, x["a"], 0.0), x["inv"])
        for x in st.values():
            x["inv"] = x["inv"] - bdot(x["inv"], x["t"])
    for x in st.values():
        x["eg"] = jnp.exp(x["gc"])
        x["sol"] = bdot(x["inv"], jnp.concatenate([x["v"] * x["beta"], x["kb"] * x["eg"]], axis=-1))
        x["qg"] = x["q"] * x["eg"]
        x["kg"] = x["k"] * jnp.exp(x["gt"] - x["gc"])
    state = {(d, h): st_ref[d, h] for d, h in heads}
    for cc in range(n_chunks):
        cur = [(key, st[(cc,) + key]) for key in heads]
        for key, x in cur:
            x["ws"] = bdot(x["sol"][:, HEAD_DIM:], state[key])
            x["qs"] = bdot(x["qg"], state[key])
        for key, x in cur:
            x["v_new"] = x["sol"][:, :HEAD_DIM] - x["ws"]
            x["o"] = x["qs"] + bdot(x["qk"], x["v_new"])
            x["upd"] = bdot_tn(x["kg"], x["v_new"])
        for key, x in cur:
            state[key] = state[key] * jnp.exp(x["gt"][0:1, :]) + x["upd"]
        for d in range(2):
            dirs[d][4][0, row0(cc, d):row0(cc, d) + c, :] = jnp.concatenate(
                [x["o"] for (dd, h), x in cur if dd == d], axis=-1)
    for d, h in heads:
        st_ref[d, h] = state[d, h]


def _gdn_post_rows(o, gate, g):
    head_mean = _segment_ones(GW, HEAD_DIM, F32) * (1.0 / HEAD_DIM)
    return o * lax.rsqrt(_dot_f32(o * o, head_mean) + 1e-6) * g * (gate * jax.nn.sigmoid(gate))


def _gdn_post_kernel(of_ref, ob_ref, gate_ref, g_ref, o_ref):
    o_ref[0] = _gdn_post_rows(of_ref[0] + ob_ref[0], gate_ref[0], g_ref[...])


def _gated_deltanet_pallas(f, conv_w, a_log, dt_bias, norm_g, *, ctx_len, apply_post=True):
    bsz, seq, width = f.shape
    tr = ROW_BLOCK
    assert ctx_len % tr == 0 and seq % tr == 0 and tr % GDN_CHUNK == 0
    if width == GDN_COLS:
        f = jnp.pad(f, ((0, 0), (0, 0), (0, GDN_PADDED_COLS - GDN_COLS)))
    assert f.shape[2] == GDN_PADDED_COLS
    nb, ctx_blocks = seq // tr, ctx_len // tr
    lane_pad = lambda a: jnp.pad(a.reshape(1, -1).astype(F32), ((0, 0), (0, GDN_GATE_LANES - a.size)))
    prev_spec, next_spec = _halo_specs(3 * GW, tr)
    full = lambda a: pl.BlockSpec(a.shape, lambda b, i: (0,) * a.ndim)
    act = jax.ShapeDtypeStruct((bsz, seq, GW), F32)
    gact = jax.ShapeDtypeStruct((bsz, seq, GDN_GATE_LANES), F32)
    blk = pl.BlockSpec((1, tr, GW), lambda b, i: (b, i, 0))
    gblk = pl.BlockSpec((1, tr, GDN_GATE_LANES), lambda b, i: (b, i, 0))
    gate_view = pl.BlockSpec((1, tr, GW), lambda b, i: (b, i, 3))
    ab_view = pl.BlockSpec((1, tr, GDN_GATE_LANES), lambda b, i: (b, i, 4 * GW // GDN_GATE_LANES))
    params = [conv_w.astype(F32), lane_pad(a_log), lane_pad(dt_bias)]
    q, k, v, gb = pl.pallas_call(
        functools.partial(_gdn_prep_kernel, ctx_blocks=ctx_blocks),
        name="gdn_prep",
        grid=(bsz, nb),
        in_specs=[pl.BlockSpec((1, tr, 3 * GW), lambda b, i: (b, i, 0)), prev_spec, next_spec(seq // 8), ab_view]
                 + [full(p) for p in params],
        out_specs=[blk, blk, blk, gblk],
        out_shape=[act, act, act, gact],
        compiler_params=pltpu.CompilerParams(dimension_semantics=("arbitrary", "arbitrary"),
                                             vmem_limit_bytes=V7X_VMEM_LIMIT_BYTES),
    )(f, f, f, f, *params)

    def bwd_block(i):
        return jnp.where(i < ctx_blocks, ctx_blocks - 1 - i, nb - 1 - (i - ctx_blocks))
    bblk = pl.BlockSpec((1, tr, GW), lambda b, i: (b, bwd_block(i), 0))
    bgblk = pl.BlockSpec((1, tr, GDN_GATE_LANES), lambda b, i: (b, bwd_block(i), 0))
    of, ob = pl.pallas_call(
        _gdn_chunk_kernel,
        name="gdn_chunks",
        grid=(bsz, nb),
        in_specs=[blk, blk, blk, gblk, bblk, bblk, bblk, bgblk],
        out_specs=[blk, bblk],
        out_shape=[act, act],
        scratch_shapes=[pltpu.VMEM((2, GROUP_HEADS, HEAD_DIM, HEAD_DIM), F32)],
        compiler_params=pltpu.CompilerParams(dimension_semantics=("arbitrary", "arbitrary"),
                                             vmem_limit_bytes=V7X_VMEM_LIMIT_BYTES),
    )(q, k, v, gb, q, k, v, gb)

    g_row = jnp.tile(norm_g.reshape(1, HEAD_DIM).astype(F32), (1, GROUP_HEADS))
    if not apply_post:
        return of, ob, f, g_row
    return pl.pallas_call(
        _gdn_post_kernel,
        name="gdn_post",
        grid=(bsz, nb),
        in_specs=[blk, blk, gate_view, full(g_row)],
        out_specs=blk,
        out_shape=act,
        compiler_params=pltpu.CompilerParams(dimension_semantics=("arbitrary", "arbitrary")),
    )(of, ob, f, g_row)


ROPE_PAIR = DIFF_HALF // 4


def _rope_tables(seq, ctx_len, q_scale):
    n = jnp.arange(seq - ctx_len, dtype=jnp.int32)
    row, col = n // GRID_W, n % GRID_W
    i = jnp.arange(HEAD_DIM)
    grp = (i % DIFF_HALF) // (2 * ROPE_PAIR)
    inv = ROPE_BASE ** (-(i % ROPE_PAIR).astype(F32) / ROPE_PAIR)
    pos = jnp.where(grp[None, :] == 0, row[:, None], col[:, None]).astype(F32)
    ang = pos * inv[None, :]
    sign = jnp.where((i % (2 * ROPE_PAIR)) < ROPE_PAIR, -1.0, 1.0)
    cos = jnp.concatenate([jnp.ones((ctx_len, HEAD_DIM), F32), jnp.cos(ang)], 0)
    sin = jnp.concatenate([jnp.zeros((ctx_len, HEAD_DIM), F32), jnp.sin(ang) * sign], 0)
    cos = jnp.tile(cos, (1, GROUP_HEADS))
    sin = jnp.tile(sin, (1, GROUP_HEADS))
    return jnp.concatenate([cos * q_scale, cos], 1), jnp.concatenate([sin * q_scale, sin], 1)


def _rotate_qk(qk, cos, sin):
    lane = lax.broadcasted_iota(jnp.int32, qk.shape, 1)
    partner = jnp.where(lane % (2 * ROPE_PAIR) < ROPE_PAIR,
                        pltpu.roll(qk, qk.shape[1] - ROPE_PAIR, axis=1), pltpu.roll(qk, ROPE_PAIR, axis=1))
    return qk * cos + partner * sin


def _store_qkv(q, k, v, q_out, k_out, v_out, v_transposed=False):
    q_out[...] = q.astype(BF16)
    vt = v.T if v_transposed else None
    for h in range(GROUP_HEADS):
        cols = slice(h * HEAD_DIM, (h + 1) * HEAD_DIM)
        k_out[h] = k[:, cols].astype(BF16)
        v_out[h] = vt[cols, :].astype(BF16) if v_transposed else v[:, cols].astype(BF16)


_NT = (((1,), (1,)), ((), ()))


def _softmax_pv(s, v):
    m = jnp.max(s, axis=-1, keepdims=True)
    e = jnp.exp(s - m)
    return jnp.dot(e.astype(BF16), v, preferred_element_type=F32) / jnp.sum(e, axis=-1, keepdims=True)


def _softmax_pv_t(s, vt):
    m = jnp.max(s, axis=-1, keepdims=True)
    e = jnp.exp(s - m)
    out_t = lax.dot_general(vt, e.astype(BF16), _NT, preferred_element_type=F32)
    return out_t.T / jnp.sum(e, axis=-1, keepdims=True)


def _diff_attn_kernel(q_ref, k_ref, vt_ref, lam_ref, g_ref, o_ref, *, ctx_blocks, ctx_len):
    i = pl.program_id(1)
    lv = lam_ref[...]
    lam_init = lv[4:5, 0:1]
    lam = (jnp.exp(jnp.sum(lv[0:1] * lv[1:2], axis=-1, keepdims=True))
           - jnp.exp(jnp.sum(lv[2:3] * lv[3:4], axis=-1, keepdims=True)) + lam_init)
    lane = lax.broadcasted_iota(jnp.int32, (q_ref.shape[1], HEAD_DIM), 1)

    def attend(n_keys):
        outs = []
        for h in range(GROUP_HEADS):
            qh = q_ref[0, :, h * HEAD_DIM:(h + 1) * HEAD_DIM]
            kh = k_ref[h, 0, 0:n_keys, :]
            vt = vt_ref[h, :, 0:n_keys]
            zero = jnp.zeros_like(qh)
            s1 = lax.dot_general(jnp.where(lane < DIFF_HALF, qh, zero), kh, _NT, preferred_element_type=F32)
            s2 = lax.dot_general(jnp.where(lane >= DIFF_HALF, qh, zero), kh, _NT, preferred_element_type=F32)
            o = _softmax_pv_t(s1, vt) - lam * _softmax_pv_t(s2, vt)
            o = o * lax.rsqrt(jnp.mean(o * o, axis=-1, keepdims=True) + 1e-6) * g_ref[...] * (1.0 - lam_init)
            outs.append(o)
        o_ref[0] = jnp.concatenate(outs, axis=-1)

    @pl.when(i < ctx_blocks)
    def _():
        attend(ctx_len)

    @pl.when(i >= ctx_blocks)
    def _():
        attend(k_ref.shape[2])


def _diff_attention_pallas(q, k, vt, lam_vecs, norm_g, *, ctx_len, lam_init):
    bsz, seq, _ = q.shape
    tq = ROW_BLOCK
    k_spec = pl.BlockSpec((GROUP_HEADS, 1, seq, HEAD_DIM), lambda b, i: (0, b, 0, 0))
    vt_spec = pl.BlockSpec((GROUP_HEADS, HEAD_DIM, seq), lambda b, i: (0, 0, b))
    lam_rows = jnp.concatenate([lam_vecs.astype(F32), jnp.full((1, lam_vecs.shape[1]), lam_init, F32)], 0)
    return pl.pallas_call(
        functools.partial(_diff_attn_kernel, ctx_blocks=ctx_len // tq, ctx_len=ctx_len),
        name="diff_attn",
        grid=(bsz, seq // tq),
        in_specs=[pl.BlockSpec((1, tq, GW), lambda b, i: (b, i, 0)), k_spec, vt_spec,
                  pl.BlockSpec(lam_rows.shape, lambda b, i: (0, 0)),
                  pl.BlockSpec((1, HEAD_DIM), lambda b, i: (0, 0))],
        out_specs=pl.BlockSpec((1, tq, GW), lambda b, i: (b, i, 0)),
        out_shape=jax.ShapeDtypeStruct((bsz, seq, GW), F32),
        compiler_params=pltpu.CompilerParams(dimension_semantics=("arbitrary", "arbitrary"),
                                             vmem_limit_bytes=V7X_VMEM_LIMIT_BYTES),
    )(q, k, vt, lam_rows, norm_g.reshape(1, HEAD_DIM).astype(F32))


NAT_TILE_ROWS = ROW_BLOCK // GRID_W
NAT_SLAB_ROWS = NAT_TILE_ROWS + WIN_H - 1


def _nat_slab_start(tile, n_rows):
    return np.clip(tile * NAT_TILE_ROWS - WIN_H // 2, 0, n_rows - NAT_SLAB_ROWS)


def _nat_bias_tables(rpb, n_rows):
    n_tiles = n_rows // NAT_TILE_ROWS
    nq, nk, w = NAT_TILE_ROWS, NAT_SLAB_ROWS, GRID_W
    cq, ck = np.arange(w)[:, None], np.arange(w)[None, :]
    d_col = np.clip(ck - cq, -(WIN_W - 1), WIN_W - 1) + WIN_W - 1
    col_1h = (d_col.reshape(-1)[:, None] == np.arange(2 * WIN_W - 1)[None, :]).astype(np.float32)
    c0 = np.clip(cq - WIN_W // 2, 0, w - WIN_W)
    col_ok = (ck >= c0) & (ck < c0 + WIN_W)
    tabs = []
    for tile in (0, 1, n_tiles - 1):
        r = tile * nq + np.arange(nq)[:, None]
        kr = _nat_slab_start(tile, n_rows) + np.arange(nk)[None, :]
        rs = np.clip(r - WIN_H // 2, 0, n_rows - WIN_H)
        row_ok = (kr >= rs) & (kr < rs + WIN_H)
        d_row = np.clip(kr - r + WIN_H - 1, 0, 2 * WIN_H - 2)
        row_1h = (d_row.reshape(-1)[:, None] == np.arange(2 * WIN_H - 1)[None, :]).astype(np.float32)
        t = jnp.einsum('pa,hab,cb->hpc', row_1h, rpb.astype(F32), col_1h, precision=HIGHEST)
        t = t.reshape(GROUP_HEADS, nq, nk, w, w).transpose(0, 1, 3, 2, 4)
        ok = row_ok[:, None, :, None] & col_ok[None, :, None, :]
        tabs.append(jnp.where(ok[None], t, NEG_BIG).reshape(GROUP_HEADS, nq * w, nk * w))
    return jnp.stack(tabs)


def _nat_attn_kernel(q_ref, k_ref, v_ref, bias_ref, o_ref, *, ctx_blocks, ctx_len, n_rows):
    i = pl.program_id(1)
    n_slab = NAT_SLAB_ROWS * GRID_W

    def heads(fn):
        o_ref[0] = jnp.concatenate(
            [fn(h, q_ref[0, :, h * HEAD_DIM:(h + 1) * HEAD_DIM]) for h in range(GROUP_HEADS)], axis=-1)

    @pl.when(i < ctx_blocks)
    def _():
        def ctx_only(h, qh):
            s = lax.dot_general(qh, k_ref[h, 0, 0:ctx_len, :], _NT, preferred_element_type=F32)
            return _softmax_pv(s, v_ref[h, 0, 0:ctx_len, :])
        heads(ctx_only)

    @pl.when(i >= ctx_blocks)
    def _():
        tile = i - ctx_blocks
        start = jnp.clip(tile * NAT_TILE_ROWS - WIN_H // 2, 0, n_rows - NAT_SLAB_ROWS)
        off = pl.multiple_of(ctx_len + start * GRID_W, GRID_W)

        def windowed(h, qh):
            s_w = lax.dot_general(qh, k_ref[h, 0, pl.ds(off, n_slab), :], _NT,
                                  preferred_element_type=F32) + bias_ref[0, h]
            s_c = lax.dot_general(qh, k_ref[h, 0, 0:ctx_len, :], _NT, preferred_element_type=F32)
            m = jnp.maximum(jnp.max(s_w, axis=-1, keepdims=True), jnp.max(s_c, axis=-1, keepdims=True))
            e_w = jnp.exp(s_w - m)
            e_c = jnp.exp(s_c - m)
            den = jnp.sum(e_w, axis=-1, keepdims=True) + jnp.sum(e_c, axis=-1, keepdims=True)
            num = (jnp.dot(e_w.astype(BF16), v_ref[h, 0, pl.ds(off, n_slab), :], preferred_element_type=F32)
                   + jnp.dot(e_c.astype(BF16), v_ref[h, 0, 0:ctx_len, :], preferred_element_type=F32))
            return num / den
        heads(windowed)


def _nat_attention_pallas(q, k, v, rpb, *, ctx_len):
    bsz, seq, _ = q.shape
    tq = ROW_BLOCK
    ctx_blocks = ctx_len // tq
    n_rows = (seq - ctx_len) // GRID_W
    n_tiles = n_rows // NAT_TILE_ROWS
    assert n_rows >= NAT_SLAB_ROWS and n_tiles >= 3
    bias = _nat_bias_tables(rpb, n_rows)

    def variant(i):
        tile = i - ctx_blocks
        return jnp.where(tile <= 0, 0, jnp.where(tile >= n_tiles - 1, 2, 1))
    kv_spec = pl.BlockSpec((GROUP_HEADS, 1, seq, HEAD_DIM), lambda b, i: (0, b, 0, 0))
    return pl.pallas_call(
        functools.partial(_nat_attn_kernel, ctx_blocks=ctx_blocks, ctx_len=ctx_len, n_rows=n_rows),
        name="nat_attn",
        grid=(bsz, seq // tq),
        in_specs=[pl.BlockSpec((1, tq, GW), lambda b, i: (b, i, 0)), kv_spec, kv_spec,
                  pl.BlockSpec((1,) + bias.shape[1:], lambda b, i: (variant(i), 0, 0, 0))],
        out_specs=pl.BlockSpec((1, tq, GW), lambda b, i: (b, i, 0)),
        out_shape=jax.ShapeDtypeStruct((bsz, seq, GW), F32),
        compiler_params=pltpu.CompilerParams(dimension_semantics=("arbitrary", "arbitrary"),
                                             vmem_limit_bytes=V7X_VMEM_LIMIT_BYTES),
    )(q, k, v, bias)


N_MOD = 6
MATMUL_ROWS = 512


def _ada_kernel(c_ref, w_ref, b_ref, o_ref):
    c = c_ref[...]
    o_ref[...] = _dot_f32(c * jax.nn.sigmoid(c), w_ref[...]) + b_ref[...]


def _ada_modulation(c, c_ctx, w_ada, b_ada):
    bsz, d = c.shape
    rows = 8 * ((bsz + 1 + 7) // 8)
    cc = jnp.zeros((rows, d), F32).at[:bsz].set(c).at[bsz].set(c_ctx)
    tn = d
    m = pl.pallas_call(
        _ada_kernel,
        name="ada_modulation",
        grid=(w_ada.shape[1] // tn,),
        in_specs=[pl.BlockSpec((rows, d), lambda j: (0, 0)),
                  pl.BlockSpec((d, tn), lambda j: (0, j)),
                  pl.BlockSpec((1, tn), lambda j: (0, j))],
        out_specs=pl.BlockSpec((rows, tn), lambda j: (0, j)),
        out_shape=jax.ShapeDtypeStruct((rows, w_ada.shape[1]), F32),
        compiler_params=pltpu.CompilerParams(dimension_semantics=("arbitrary",)),
    )(cc, w_ada, b_ada.reshape(1, -1))
    lat = m[:bsz].reshape(bsz, 1, N_MOD, d)
    ctx = jnp.broadcast_to(m[bsz].reshape(1, 1, N_MOD, d), (bsz, 1, N_MOD, d))
    return jnp.concatenate([ctx, lat], axis=1)


def _ln(x):
    mu = jnp.mean(x, axis=-1, keepdims=True)
    xc = x - mu
    return xc * lax.rsqrt(jnp.mean(xc * xc, axis=-1, keepdims=True) + LN_EPS)


def _mod_spec(d, ctx_blocks):
    return pl.BlockSpec((1, 1, N_MOD, d), lambda b, i: (b, jnp.where(i < ctx_blocks, 0, 1), 0, 0))


def _in_proj_kernel(h_ref, mod0_ref, mod1_ref, wa_ref, wb_ref, wc_ref, wd_ref, cos_ref, sin_ref,
                    qa_out, ka_out, va_out, pb_out, pc_out, qd_out, kd_out, vd_out, *, seq, ctx_len):
    h = h_ref[...]
    shift, scale = _flat_row_mod(mod0_ref, mod1_ref, pl.program_id(0), h.shape, seq, ctx_len, (0, 1))
    x = (_ln(h) * (1.0 + scale) + shift).astype(BF16)
    pa = jnp.dot(x, wa_ref[...], preferred_element_type=F32)
    qk = _rotate_qk(pa[:, :2 * GW], cos_ref[...], sin_ref[...])
    _store_qkv(qk[:, :GW], qk[:, GW:], pa[:, 2 * GW:], qa_out, ka_out, va_out, v_transposed=True)
    pb_out[...] = jnp.dot(x, wb_ref[...], preferred_element_type=F32)
    pc_out[...] = jnp.dot(x, wc_ref[...], preferred_element_type=F32)
    pd = jnp.dot(x, wd_ref[...], preferred_element_type=F32)
    _store_qkv(pd[:, :GW] * (HEAD_DIM ** -0.5), pd[:, GW:2 * GW], pd[:, 2 * GW:], qd_out, kd_out, vd_out)


def _in_proj_pallas(x, mod, ws, cos, sin, *, seq, ctx_len):
    m, k = x.shape
    tm = MATMUL_ROWS
    row = lambda n: pl.BlockSpec((tm, n), lambda i: (i, 0))
    hm = pl.BlockSpec((GROUP_HEADS, tm, HEAD_DIM), lambda i: (0, i, 0))
    q_shape = jax.ShapeDtypeStruct((m, GW), BF16)
    hm_shape = jax.ShapeDtypeStruct((GROUP_HEADS, m, HEAD_DIM), BF16)
    hm_t = pl.BlockSpec((GROUP_HEADS, HEAD_DIM, tm), lambda i: (0, 0, i))
    hm_t_shape = jax.ShapeDtypeStruct((GROUP_HEADS, HEAD_DIM, m), BF16)
    f32 = lambda n: jax.ShapeDtypeStruct((m, n), F32)
    return pl.pallas_call(
        functools.partial(_in_proj_kernel, seq=seq, ctx_len=ctx_len),
        name="in_proj",
        grid=(m // tm,),
        in_specs=[row(k)] + _flat_mod_specs(tm, seq, k, 1) + [pl.BlockSpec(w.shape, lambda i: (0, 0)) for w in ws]
                 + [row(2 * GW), row(2 * GW)],
        out_specs=[row(GW), hm, hm_t, row(ws[1].shape[1]), row(ws[2].shape[1]), row(GW), hm, hm],
        out_shape=[q_shape, hm_shape, hm_t_shape, f32(ws[1].shape[1]), f32(ws[2].shape[1]), q_shape, hm_shape,
                   hm_shape],
        compiler_params=pltpu.CompilerParams(dimension_semantics=("arbitrary",),
                                             vmem_limit_bytes=V7X_VMEM_LIMIT_BYTES),
    )(x, mod, mod, *ws, cos, sin)


def _post_norm_rows(h, gate, y, g, b):
    return _ln(DN_ALPHA * h + gate * y) * g + b


def _out_proj_kernel(ya_ref, rf_ref, rb_ref, bonus_ref, rg_ref, lng_ref, lnb_ref, of_ref, ob_ref, gate_ref, gn_ref,
                     yd_ref, w_ref, h_ref, mod_ref, g_ref, b_ref, o_ref):
    yb = _rwkv_post_rows(rf_ref[0] + rb_ref[0], bonus_ref[0], rg_ref[0], lng_ref[...], lnb_ref[...])
    yc = _gdn_post_rows(of_ref[0] + ob_ref[0], gate_ref[0], gn_ref[...])
    mix = None
    for n, y in enumerate((ya_ref[0], yb, yc, yd_ref[0])):
        part = jnp.dot(y.astype(BF16), w_ref[n * GW:(n + 1) * GW, :], preferred_element_type=F32)
        mix = part if mix is None else mix + part
    o_ref[0] = _post_norm_rows(h_ref[0], mod_ref[0, 0, 2:3, :], mix, g_ref[...], b_ref[...])


def _out_proj_post_norm(ya, rwkv_parts, gdn_parts, yd, w_out, hs, mod, g, b, *, ctx_len):
    bsz, seq, d = hs.shape
    tr = ROW_BLOCK
    yblk = pl.BlockSpec((1, tr, GW), lambda bb, i: (bb, i, 0))
    blk = pl.BlockSpec((1, tr, d), lambda bb, i: (bb, i, 0))
    row = pl.BlockSpec((1, d), lambda bb, i: (0, 0))
    grow = pl.BlockSpec((1, GW), lambda bb, i: (0, 0))
    gate_view = pl.BlockSpec((1, tr, GW), lambda bb, i: (bb, i, 3))
    return pl.pallas_call(
        _out_proj_kernel,
        name="out_proj_post_norm",
        grid=(bsz, seq // tr),
        in_specs=[yblk] + [yblk] * 4 + [grow, grow] + [yblk, yblk, gate_view, grow] + [yblk]
                 + [pl.BlockSpec(w_out.shape, lambda bb, i: (0, 0)), blk, _mod_spec(d, ctx_len // tr), row, row],
        out_specs=blk,
        out_shape=jax.ShapeDtypeStruct((bsz, seq, d), F32),
        compiler_params=pltpu.CompilerParams(dimension_semantics=("arbitrary", "arbitrary")),
    )(ya, *rwkv_parts, *gdn_parts, yd, w_out, hs, mod, g.reshape(1, d), b.reshape(1, d))


def kernel(x, c, ctx, c_ctx, w_ada, b_ada, w_in, w_out, ln_mix_g, ln_mix_b, ln_ffn_g, ln_ffn_b, diff_lam, diff_norm_g, rwkv_mu, rwkv_w0, rwkv_w2, rwkv_a0, rwkv_a2, rwkv_g2, rwkv_kk, rwkv_ka, rwkv_rk, rwkv_ln_g, rwkv_ln_b, gdn_conv, gdn_a_log, gdn_dt_bias, gdn_norm_g, nat_rpb, peer_wq, peer_keys, peer_u, peer_v):
    dtype = x.dtype
    bsz, ctx_len = ctx.shape[0], ctx.shape[1]
    hs = jnp.concatenate([ctx, x], axis=1)
    seq = hs.shape[1]
    col_sizes = [ATTN_COLS, RWKV_COLS, GDN_COLS, ATTN_COLS]
    cos_a, sin_a = (jnp.tile(t, (bsz, 1)) for t in _rope_tables(seq, ctx_len, DIFF_HALF ** -0.5))
    col_offs = np.cumsum([0] + col_sizes)
    d_model = hs.shape[2]
    for l in range(DEPTH):
        lam_init = 0.8 - 0.6 * math.exp(-0.3 * l)
        mod = _ada_modulation(c, c_ctx, w_ada[l], b_ada[l])
        w_in_b = w_in[l].astype(BF16)
        w_groups = [w_in_b[:, col_offs[n]:col_offs[n + 1]] for n in range(4)]
        w_groups[2] = jnp.pad(w_groups[2], ((0, 0), (0, GDN_PADDED_COLS - GDN_COLS)))
        qa, ka, va, pb, pc, qd, kd, vd = _in_proj_pallas(hs.reshape(bsz * seq, d_model), mod, w_groups, cos_a, sin_a,
                                                         seq=seq, ctx_len=ctx_len)
        rows3 = lambda p: p.reshape(bsz, seq, p.shape[1])
        heads4 = lambda p: p.reshape(GROUP_HEADS, bsz, seq, HEAD_DIM)
        qa, pb, pc, qd = rows3(qa), rows3(pb), rows3(pc), rows3(qd)
        ka, kd, vd = heads4(ka), heads4(kd), heads4(vd)
        ya = _diff_attention_pallas(qa, ka, va, diff_lam[l], diff_norm_g[l], ctx_len=ctx_len, lam_init=lam_init)
        yb_parts = _rwkv7_pallas(pb, rwkv_mu[l], rwkv_w0[l], rwkv_w2[l], rwkv_a0[l], rwkv_a2[l], rwkv_g2[l],
                                 rwkv_kk[l], rwkv_ka[l], rwkv_rk[l], rwkv_ln_g[l], rwkv_ln_b[l], ctx_len=ctx_len,
                                 apply_post=False)
        yc_parts = _gated_deltanet_pallas(pc, gdn_conv[l], gdn_a_log[l], gdn_dt_bias[l], gdn_norm_g[l],
                                          ctx_len=ctx_len, apply_post=False)
        yd = _nat_attention_pallas(qd, kd, vd, nat_rpb[l], ctx_len=ctx_len)
        hs = _out_proj_post_norm(ya, yb_parts, yc_parts, yd, w_out[l].astype(BF16), hs, mod, ln_mix_g[l],
                                 ln_mix_b[l], ctx_len=ctx_len)
        wq_b = peer_wq[l].astype(BF16)
        keys_b = peer_keys[l].reshape(2 * PEER_HEADS, PEER_KEYS, PEER_HALF).astype(BF16)
        u_b = peer_u[l].astype(BF16)
        vt_b = peer_v[l].astype(BF16).T
        ffn_ctx = ctx_len
        if l == DEPTH - 1:
            hs, ffn_ctx = hs[:, ctx_len:], 0
        hs = _peer_ffn(hs, mod, ln_ffn_g[l], ln_ffn_b[l], wq_b, keys_b, u_b, vt_b, ctx_len=ffn_ctx)
    return hs.astype(dtype)
```

```python
import functools
import math

import jax
import jax.numpy as jnp
import numpy as np
from jax import lax
from jax.experimental import pallas as pl
from jax.experimental.pallas import tpu as pltpu

D_MODEL = 1024
DEPTH = 2
GRID_W = 64
HEAD_DIM = 64
N_GROUPS = 4
GROUP_HEADS = D_MODEL // (N_GROUPS * HEAD_DIM)
GW = GROUP_HEADS * HEAD_DIM
DIFF_HALF = HEAD_DIM // 2
ROPE_BASE = 10000.0
DECAY_LORA = 64
ICLR_LORA = 64
GATE_LORA = 128
RWKV_GN_EPS = 64e-5
RWKV_COLS = 3 * GW + 2 * DECAY_LORA + 2 * ICLR_LORA + GATE_LORA
GDN_CHUNK = 64
GDN_COLS = 4 * GW + 4 * GROUP_HEADS
WIN_H = 8
WIN_W = 16
ATTN_COLS = 3 * GW
PEER_HEADS = 8
PEER_KEYS = 128
PEER_EXPERTS = PEER_KEYS * PEER_KEYS
PEER_QDIM = 256
PEER_HALF = PEER_QDIM // 2
PEER_TOPK = 16
DN_ALPHA = (2 * DEPTH) ** 0.25
LN_EPS = 1e-5

F32 = jnp.float32
BF16 = jnp.bfloat16

V7X_VMEM_LIMIT_BYTES = 56 * 1024 * 1024
NEG_BIG = -3.0e38


PEER_STAT_ROWS = 4 * PEER_HEADS
LOG2E = 1.4426950408889634
PEER_POW_HEADS = 4
assert PEER_TOPK == 16


def _flat_mod_specs(tm, seq, d, grid_rank):
    first = lambda i, *_: ((i * tm) // seq, 0, 0, 0)
    last = lambda i, *_: ((i * tm + tm - 1) // seq, 0, 0, 0)
    assert grid_rank in (1, 2) and tm <= seq
    return [pl.BlockSpec((1, 2, N_MOD, d), first), pl.BlockSpec((1, 2, N_MOD, d), last)]


def _flat_row_mod(mod0_ref, mod1_ref, i, shape, seq, ctx_len, rows):
    tm = shape[0]
    r0 = i * tm
    b0 = r0 // seq
    boundary = (b0 + 1) * seq
    idx = r0 + lax.broadcasted_iota(jnp.int32, shape, 0)
    second = idx >= boundary
    is_ctx = (idx - jnp.where(second, boundary, b0 * seq)) < ctx_len
    out = []
    for k in rows:
        v = jnp.where(second, mod1_ref[0, 1, k:k + 1, :], mod0_ref[0, 1, k:k + 1, :])
        out.append(jnp.where(is_ctx, mod0_ref[0, 0, k:k + 1, :], v) if ctx_len else v)
    return out


def _topk_rows(x, k):
    q = x.shape[0] // 4
    lv = [x[i * q:(i + 1) * q] for i in range(4)]
    for a, b in ((0, 1), (2, 3), (0, 2), (1, 3), (1, 2)):
        lv[a], lv[b] = jnp.maximum(lv[a], lv[b]), jnp.minimum(lv[a], lv[b])
    rows = []
    for i in range(k):
        m = jnp.max(lv[0], axis=0, keepdims=True)
        rows.append(m)
        if i + 1 < k:
            hit = lv[0] == m
            lv = [jnp.where(hit, lv[n + 1], lv[n]) for n in range(3)] + [jnp.where(hit, NEG_BIG, lv[3])]
    return rows


def _peer_score_kernel(h_ref, mod0_ref, mod1_ref, wq_ref, keys_ref, x_out, s_ref, st_ref, *, seq, ctx_len):
    h = h_ref[...]
    shift, scale = _flat_row_mod(mod0_ref, mod1_ref, pl.program_id(0), h.shape, seq, ctx_len, (3, 4))
    x = (_ln(h) * (1.0 + scale) + shift).astype(BF16)
    x_out[...] = x
    q = jnp.dot(x, wq_ref[...], preferred_element_type=F32).astype(BF16)
    stats = []
    for h in range(PEER_HEADS):
        tops = []
        for p in range(2):
            hp = 2 * h + p
            s_t = lax.dot_general(keys_ref[hp], q[:, hp * PEER_HALF:(hp + 1) * PEER_HALF],
                                  (((1,), (1,)), ((), ())), preferred_element_type=F32)
            s_ref[hp] = s_t
            tops.append(_topk_rows(s_t, PEER_TOPK + 1))
        a, b = tops
        pad = [jnp.full_like(a[0], NEG_BIG)] * 7
        b_head = jnp.concatenate(b[:8], axis=0)
        cand = jnp.concatenate([a[0] + jnp.concatenate(b + pad, axis=0)]
                               + [a[i] + b_head for i in range(1, 8)]
                               + [jnp.concatenate(a[8:] + pad, axis=0) + b[0]], axis=0)
        best_cand = _topk_rows(cand, PEER_TOPK + 1)
        kth, runner_up = best_cand[PEER_TOPK - 1], best_cand[PEER_TOPK]
        best = a[0] + b[0]
        z = jnp.sum(jnp.where(cand >= kth, jnp.exp(cand - best), 0.0), axis=0, keepdims=True)
        stats += [0.5 * (kth + runner_up), tops[0][0], tops[1][0], 1.0 / z]
    st_ref[...] = jnp.concatenate(stats, axis=0)


def _peer_expert_kernel(x_ref, s_ref, st_ref, u_ref, vt_ref, h_ref, mod0_ref, mod1_ref, g_ref, b_ref, o_ref,
                        e_ref, thr_ref, lin_ref, acc_ref, w_ref, *, rows_per_step, n_chunks, seq, ctx_len):
    j = pl.program_id(1)
    cur = lax.rem(j, 2)

    @pl.when(j == 0)
    def _():
        acc_ref[...] = jnp.zeros_like(acc_ref)
        w_ref[1] = jnp.zeros(w_ref.shape[1:], w_ref.dtype)
        for h in range(PEER_HEADS):
            a0 = st_ref[4 * h + 1:4 * h + 2, :]
            b0 = st_ref[4 * h + 2:4 * h + 3, :]
            rz = st_ref[4 * h + 3:4 * h + 4, :]
            l1 = (s_ref[2 * h] - a0 + jnp.log(rz)) * LOG2E
            l2 = (s_ref[2 * h + 1] - b0) * LOG2E
            e_ref[2 * h] = l1 if h < PEER_POW_HEADS else jnp.exp2(l1)
            e_ref[2 * h + 1] = l2
            if h >= PEER_POW_HEADS:
                lin_ref[h - PEER_POW_HEADS] = jnp.exp2(l2)
            thr_ref[h] = (st_ref[4 * h:4 * h + 1, :] - s_ref[2 * h] - b0) * LOG2E

    @pl.when(j < n_chunks)
    def _():
        slab = 2 * PEER_KEYS
        n_slabs = rows_per_step * PEER_KEYS // slab
        act_slab = lambda n: lax.dot_general(u_ref[n * slab:(n + 1) * slab, :], x_ref[...], (((1,), (1,)), ((), ())),
                                             preferred_element_type=F32)
        acts = [act_slab(n) for n in range(n_slabs)]
        acc_ref[...] += jnp.dot(vt_ref[...], w_ref[1 - cur], preferred_element_type=F32)
        assert rows_per_step % 8 == 0
        for il in range(rows_per_step):
            base = pl.multiple_of(j * rows_per_step + (il // 8) * 8, 8)
            r = il % 8
            o = il * PEER_KEYS
            act = acts[o // slab][o % slab:o % slab + PEER_KEYS]
            act = 0.5 * act * (1.0 + lax.erf(act * (2.0 ** -0.5)))
            gate = None
            for h in range(PEER_HEADS):
                l2 = e_ref[2 * h + 1]
                thr_row = thr_ref[h, pl.ds(base, 8), :][r:r + 1, :]
                half1 = e_ref[2 * h, pl.ds(base, 8), :][r:r + 1, :]
                pair = jnp.exp2(l2 + half1) if h < PEER_POW_HEADS else lin_ref[h - PEER_POW_HEADS] * half1
                term = jnp.where(l2 >= thr_row, pair, 0.0)
                gate = term if gate is None else gate + term
            w_ref[cur, o:o + PEER_KEYS, :] = (act * gate).astype(BF16)

    @pl.when(j == n_chunks)
    def _():
        ffn = (acc_ref[...] + jnp.dot(vt_ref[...], w_ref[1 - cur], preferred_element_type=F32)).T
        h = h_ref[...]
        gate, = _flat_row_mod(mod0_ref, mod1_ref, pl.program_id(0), h.shape, seq, ctx_len, (5,))
        o_ref[...] = _post_norm_rows(h, gate, ffn, g_ref[...], b_ref[...])


def _peer_pallas(h, mod, g, b, w_q, keys, u_tab, vt_tab, *, seq, ctx_len, tb, rows_per_step):
    n_tok, d = h.shape
    nhp = 2 * PEER_HEADS
    x, s, st = pl.pallas_call(
        functools.partial(_peer_score_kernel, seq=seq, ctx_len=ctx_len),
        name="peer_scores",
        grid=(n_tok // tb,),
        in_specs=[pl.BlockSpec((tb, d), lambda i: (i, 0))] + _flat_mod_specs(tb, seq, d, 1)
                 + [pl.BlockSpec(w_q.shape, lambda i: (0, 0)),
                    pl.BlockSpec(keys.shape, lambda i: (0, 0, 0))],
        out_specs=[pl.BlockSpec((tb, d), lambda i: (i, 0)),
                   pl.BlockSpec((nhp, PEER_KEYS, tb), lambda i: (0, 0, i)),
                   pl.BlockSpec((PEER_STAT_ROWS, tb), lambda i: (0, i))],
        out_shape=[jax.ShapeDtypeStruct((n_tok, d), BF16),
                   jax.ShapeDtypeStruct((nhp, PEER_KEYS, n_tok), F32),
                   jax.ShapeDtypeStruct((PEER_STAT_ROWS, n_tok), F32)],
        compiler_params=pltpu.CompilerParams(dimension_semantics=("arbitrary",),
                                             vmem_limit_bytes=V7X_VMEM_LIMIT_BYTES),
    )(h, mod, mod, w_q, keys)
    ec = rows_per_step * PEER_KEYS
    n_chunks = PEER_EXPERTS // ec
    row = pl.BlockSpec((1, d), lambda i, j: (0, 0))
    return pl.pallas_call(
        functools.partial(_peer_expert_kernel, rows_per_step=rows_per_step, n_chunks=n_chunks, seq=seq,
                          ctx_len=ctx_len),
        name="peer_experts",
        grid=(n_tok // tb, n_chunks + 1),
        in_specs=[pl.BlockSpec((tb, d), lambda i, j: (i, 0)),
                  pl.BlockSpec((nhp, PEER_KEYS, tb), lambda i, j: (0, 0, i)),
                  pl.BlockSpec((PEER_STAT_ROWS, tb), lambda i, j: (0, i)),
                  pl.BlockSpec((ec, d), lambda i, j: (jnp.minimum(j, n_chunks - 1), 0)),
                  pl.BlockSpec((d, ec), lambda i, j: (0, jnp.maximum(j - 1, 0))),
                  pl.BlockSpec((tb, d), lambda i, j: (i, 0))] + _flat_mod_specs(tb, seq, d, 2) + [row, row],
        out_specs=pl.BlockSpec((tb, d), lambda i, j: (i, 0)),
        out_shape=jax.ShapeDtypeStruct((n_tok, d), F32),
        scratch_shapes=[pltpu.VMEM((nhp, PEER_KEYS, tb), F32),
                        pltpu.VMEM((PEER_HEADS, PEER_KEYS, tb), F32),
                        pltpu.VMEM((PEER_HEADS - PEER_POW_HEADS, PEER_KEYS, tb), F32),
                        pltpu.VMEM((d, tb), F32),
                        pltpu.VMEM((2, ec, tb), BF16)],
        compiler_params=pltpu.CompilerParams(dimension_semantics=("arbitrary", "arbitrary"),
                                             vmem_limit_bytes=V7X_VMEM_LIMIT_BYTES),
    )(x, s, st, u_tab, vt_tab, h, mod, mod, g.reshape(1, d).astype(F32), b.reshape(1, d).astype(F32))


def _peer_ffn(hs, mod, g, b, w_q, keys, u_tab, vt_tab, *, ctx_len):
    bsz, t, d = hs.shape
    n_tok = bsz * t
    tb = 512 if n_tok % 512 == 0 else 256
    out = _peer_pallas(hs.reshape(n_tok, d), mod, g, b, w_q, keys, u_tab, vt_tab, seq=t, ctx_len=ctx_len, tb=tb,
                       rows_per_step=16)
    return out.reshape(bsz, t, d)


ROW_BLOCK = 256
HIGHEST = lax.Precision.HIGHEST


def _dot_f32(a, b):
    return jnp.dot(a, b, precision=HIGHEST, preferred_element_type=F32)


def _bf16_terms(x):
    hi = x.astype(BF16)
    rest = x - hi.astype(F32)
    mid = rest.astype(BF16)
    return hi, mid, (rest - mid.astype(F32)).astype(BF16)


def _select_sum(sel, x):
    n = x.shape[1]
    out = jnp.dot(sel.astype(BF16), jnp.concatenate(_bf16_terms(x), axis=1), preferred_element_type=F32)
    return out[:, :n] + out[:, n:2 * n] + out[:, 2 * n:]


def _sum_select(x, sel):
    m = x.shape[0]
    out = jnp.dot(jnp.concatenate(_bf16_terms(x), axis=0), sel.astype(BF16), preferred_element_type=F32)
    return out[:m] + out[m:2 * m] + out[2 * m:]


def _segment_ones(n, seg, dtype):
    r = lax.broadcasted_iota(jnp.int32, (n, n), 0) // seg
    c = lax.broadcasted_iota(jnp.int32, (n, n), 1) // seg
    return jnp.where(r == c, 1.0, 0.0).astype(dtype)


def _shifted_rows(x, prev_row, next_row):
    t = x.shape[0]
    rows = lax.broadcasted_iota(jnp.int32, x.shape, 0)
    xm = jnp.where(rows == 0, prev_row, pltpu.roll(x, 1, axis=0))
    xp = jnp.where(rows == t - 1, next_row, pltpu.roll(x, t - 1, axis=0))
    return xm, xp


def _segment_edge_flags(i, n_blocks, ctx_blocks):
    is_start = jnp.logical_or(i == 0, i == ctx_blocks)
    is_end = jnp.logical_or(i == ctx_blocks - 1, i == n_blocks - 1)
    return jnp.where(is_start, 0.0, 1.0), jnp.where(is_end, 0.0, 1.0)


def _halo_specs(width, tr):
    g = tr // 8
    prev = pl.BlockSpec((1, 8, width), lambda b, i: (b, jnp.maximum(i * g - 1, 0), 0))
    nxt = lambda n_groups: pl.BlockSpec((1, 8, width), lambda b, i: (b, jnp.minimum((i + 1) * g, n_groups - 1), 0))
    return prev, nxt


def _softplus(z):
    return jnp.maximum(z, 0.0) + jnp.log1p(jnp.exp(-jnp.abs(z)))


def _rwkv_prep_kernel(x_ref, xprev_ref, xnext_ref, mu_ref, w0_ref, w2_ref, a0_ref, a2_ref, g2_ref,
                      kk_ref, ka_ref, rk_ref,
                      r_out, v_out, kkn_out, g_out, bonus_out, w_out, b_out, kt_out, *, ctx_blocks):
    i = pl.program_id(1)
    keep_prev, keep_next = _segment_edge_flags(i, pl.num_programs(1), ctx_blocks)
    x = x_ref[0]
    xm, xp = _shifted_rows(x, xprev_ref[0, 7:8, :] * keep_prev, xnext_ref[0, 0:1, :] * keep_next)
    mu0 = mu_ref[0:1, :]
    mu1 = mu_ref[1:2, :]
    f = xm * mu0 + x * (1.0 - mu0 - mu1) + xp * mu1
    r = f[:, 0:GW]
    k = f[:, GW:2 * GW]
    v = f[:, 2 * GW:3 * GW]
    o = 3 * GW
    wd = jnp.tanh(f[:, o:o + 2 * DECAY_LORA])
    ad = f[:, o + 2 * DECAY_LORA:o + 2 * DECAY_LORA + 2 * ICLR_LORA]
    gd = f[:, o + 2 * DECAY_LORA + 2 * ICLR_LORA:]
    w_raw = w0_ref[...] + _dot_f32(wd, w2_ref[...])
    log_decay = -jnp.exp(-_softplus(-w_raw) - 0.5)
    a = jax.nn.sigmoid(a0_ref[...] + _dot_f32(ad, a2_ref[...]))
    g = _dot_f32(jax.nn.sigmoid(gd), g2_ref[...])
    head_sum = _segment_ones(GW, HEAD_DIM, F32)
    kx = k * kk_ref[...]
    kkn = kx * lax.rsqrt(_dot_f32(kx * kx, head_sum) + 1e-6)
    kd_sum = jnp.zeros_like(k)
    for d in range(2):
        a_d = a[:, d * GW:(d + 1) * GW]
        kd = k * (1.0 + (a_d - 1.0) * ka_ref[...])
        kd_sum = kd_sum + kd
        w_out[d, 0] = log_decay[:, d * GW:(d + 1) * GW]
        b_out[d, 0] = kkn * a_d
        kt_out[d, 0] = kd
    r_out[0] = r
    v_out[0] = v
    kkn_out[0] = kkn
    g_out[0] = g
    bonus_out[0] = _dot_f32(r * kd_sum * rk_ref[...], head_sum) * v


RWKV_CHUNK = 64


def _rwkv_chunk_kernel(rf_ref, vf_ref, kkf_ref, wf_ref, bf_ref, ktf_ref,
                       rb_ref, vb_ref, kkb_ref, wb_ref, bb_ref, ktb_ref, yf_ref, yb_ref, st_ref):
    i = pl.program_id(1)
    c = RWKV_CHUNK
    n_chunks = rf_ref.shape[1] // c

    @pl.when(i == 0)
    def _():
        st_ref[...] = jnp.zeros_like(st_ref)

    rr = lax.broadcasted_iota(jnp.int32, (c, c), 0)
    ss = lax.broadcasted_iota(jnp.int32, (c, c), 1)
    eye = rr == ss
    ones_cc = jnp.ones((c, c), F32)
    incl = (rr >= ss, rr <= ss)
    strict = (rr > ss, rr < ss)
    levels = []
    blk = 1
    while blk < c:
        levels.append(jnp.logical_and(rr // (2 * blk) == ss // (2 * blk), rr // blk != ss // blk))
        blk *= 2
    bdot = lambda x, y: jnp.dot(x.astype(BF16), y.astype(BF16), preferred_element_type=F32)
    bdot_nt = lambda x, y: lax.dot_general(x.astype(BF16), y.astype(BF16), _NT, preferred_element_type=F32)
    bdot_tn = lambda x, y: lax.dot_general(x.astype(BF16), y.astype(BF16), (((0,), (0,)), ((), ())),
                                           preferred_element_type=F32)
    dirs = ((rf_ref, vf_ref, kkf_ref, wf_ref, bf_ref, ktf_ref, yf_ref),
            (rb_ref, vb_ref, kkb_ref, wb_ref, bb_ref, ktb_ref, yb_ref))
    heads = [(d, h) for d in range(2) for h in range(GROUP_HEADS)]
    row0 = lambda cc, d: (cc if d == 0 else n_chunks - 1 - cc) * c
    rows = lambda cc, d: slice(row0(cc, d), row0(cc, d) + c)

    logw = {(cc, d): dirs[d][3][0, 0, rows(cc, d), :] for cc in range(n_chunks) for d in range(2)}
    cum = {key: _select_sum(jnp.where(incl[key[1]], 1.0, 0.0), lw) for key, lw in logw.items()}
    tot = {key: _select_sum(ones_cc, lw) for key, lw in logw.items()}
    st = {}
    for cc in range(n_chunks):
        for d, h in heads:
            cols = slice(h * HEAD_DIM, (h + 1) * HEAD_DIM)
            r_ref, v_ref, kk_ref, _, b_ref, kt_ref, _ = dirs[d]
            lc = cum[cc, d][:, cols]
            grow = jnp.exp(-lc)
            st[cc, d, h] = dict(
                v=v_ref[0, rows(cc, d), cols],
                kap=kk_ref[0, rows(cc, d), cols] * jnp.exp(lc - logw[cc, d][:, cols]),
                bh=b_ref[0, 0, rows(cc, d), cols] * grow,
                kh=kt_ref[0, 0, rows(cc, d), cols] * grow,
                rh=r_ref[0, rows(cc, d), cols] * jnp.exp(lc),
                scale=jnp.exp(_sum_select(jnp.where(eye, tot[cc, d][:, cols], 0.0), ones_cc)))
    for (cc, d, h), x in st.items():
        gram = bdot_nt(jnp.concatenate([x["kap"], x["rh"]], axis=0), jnp.concatenate([x["bh"], x["kh"]], axis=0))
        x["a_b"] = jnp.where(strict[d], gram[:c, :c], 0.0)
        x["a_k"] = jnp.where(strict[d], gram[:c, c:], 0.0)
        x["b_b"] = jnp.where(incl[d], gram[c:, :c], 0.0)
        x["b_k"] = jnp.where(incl[d], gram[c:, c:], 0.0)
        x["inv"] = jnp.where(eye, 1.0, 0.0) - jnp.where(levels[0], x["a_b"], 0.0)
    for lvl in levels[1:]:
        for x in st.values():
            x["t"] = bdot(jnp.where(lvl, x["a_b"], 0.0), x["inv"])
        for x in st.values():
            x["inv"] = x["inv"] - bdot(x["inv"], x["t"])
    for x in st.values():
        x["akv"] = bdot(x["a_k"], x["v"])
        x["y_const"] = bdot(x["b_k"], x["v"])
        x["kv"] = bdot_tn(x["kh"], x["v"])
    for x in st.values():
        both = bdot(x["inv"], jnp.concatenate([x["kap"], x["akv"]], axis=-1))
        x["p_state"], x["p_const"] = both[:, :HEAD_DIM], both[:, HEAD_DIM:]
    state = {key: st_ref[key[0], key[1]] for key in heads}
    for cc in range(n_chunks):
        cur = [(key, st[(cc,) + key]) for key in heads]
        for key, x in cur:
            x["p"] = bdot(x["p_state"], state[key]) + x["p_const"]
            x["y0"] = bdot(x["rh"], state[key]) + x["y_const"]
        for key, x in cur:
            x["y"] = x["y0"] - bdot(x["b_b"], x["p"])
            state[key] = (state[key] + x["kv"] - bdot_tn(x["bh"], x["p"])) * x["scale"]
        for d in range(2):
            dirs[d][6][0, rows(cc, d), :] = jnp.concatenate([x["y"] for (dd, h), x in cur if dd == d], axis=-1)
    for d, h in heads:
        st_ref[d, h] = state[d, h]


def _rwkv_post_rows(y, bonus, g, ln_g, ln_b):
    head_mean = _segment_ones(GW, HEAD_DIM, F32) * (1.0 / HEAD_DIM)
    yc = y - _dot_f32(y, head_mean)
    var = _dot_f32(yc * yc, head_mean)
    return (yc * lax.rsqrt(var + RWKV_GN_EPS) * ln_g + ln_b + bonus) * g


def _rwkv_post_kernel(yf_ref, yb_ref, bonus_ref, g_ref, lng_ref, lnb_ref, o_ref):
    o_ref[0] = _rwkv_post_rows(yf_ref[0] + yb_ref[0], bonus_ref[0], g_ref[0], lng_ref[...], lnb_ref[...])


def _block_diag2(m):
    z = jnp.zeros_like(m[0])
    return jnp.concatenate([jnp.concatenate([m[0], z], 1), jnp.concatenate([z, m[1]], 1)], 0)


def _rwkv7_pallas(f, mu, w0, w2, a0, a2, g2, k_k, k_a, r_k, ln_g, ln_b, *, ctx_len, apply_post=True):
    bsz, seq, cols = f.shape
    tr = ROW_BLOCK
    assert ctx_len % tr == 0 and seq % tr == 0
    nb, ctx_blocks = seq // tr, ctx_len // tr
    prev_spec, next_spec = _halo_specs(cols, tr)
    row2 = lambda a: a.reshape(1, -1).astype(F32)
    full = lambda a: pl.BlockSpec(a.shape, lambda b, i: (0,) * a.ndim)
    params = [mu, row2(w0), _block_diag2(w2), row2(a0), _block_diag2(a2), g2, row2(k_k), row2(k_a), row2(r_k)]
    act = jax.ShapeDtypeStruct((bsz, seq, GW), F32)
    act2 = jax.ShapeDtypeStruct((2, bsz, seq, GW), F32)
    blk = pl.BlockSpec((1, tr, GW), lambda b, i: (b, i, 0))
    blk2 = pl.BlockSpec((2, 1, tr, GW), lambda b, i: (0, b, i, 0))
    r, v, kkn, g, bonus, w, bb, kt = pl.pallas_call(
        functools.partial(_rwkv_prep_kernel, ctx_blocks=ctx_blocks),
        name="rwkv_prep",
        grid=(bsz, nb),
        in_specs=[pl.BlockSpec((1, tr, cols), lambda b, i: (b, i, 0)), prev_spec, next_spec(seq // 8)]
                 + [full(p) for p in params],
        out_specs=[blk] * 5 + [blk2] * 3,
        out_shape=[act] * 5 + [act2] * 3,
        compiler_params=pltpu.CompilerParams(dimension_semantics=("arbitrary", "arbitrary"),
                                             vmem_limit_bytes=V7X_VMEM_LIMIT_BYTES),
    )(f, f, f, *params)

    def bwd_block(c):
        return jnp.where(c < ctx_blocks, ctx_blocks - 1 - c, nb - 1 - (c - ctx_blocks))
    assert tr % RWKV_CHUNK == 0 and RWKV_CHUNK == HEAD_DIM
    fwd = pl.BlockSpec((1, tr, GW), lambda b, c: (b, c, 0))
    bwd = pl.BlockSpec((1, tr, GW), lambda b, c: (b, bwd_block(c), 0))
    fwd_d = pl.BlockSpec((1, 1, tr, GW), lambda b, c: (0, b, c, 0))
    bwd_d = pl.BlockSpec((1, 1, tr, GW), lambda b, c: (1, b, bwd_block(c), 0))
    yf, yb = pl.pallas_call(
        _rwkv_chunk_kernel,
        name="rwkv_chunks",
        grid=(bsz, nb),
        in_specs=[fwd, fwd, fwd, fwd_d, fwd_d, fwd_d, bwd, bwd, bwd, bwd_d, bwd_d, bwd_d],
        out_specs=[fwd, bwd],
        out_shape=[act, act],
        scratch_shapes=[pltpu.VMEM((2, GROUP_HEADS, HEAD_DIM, HEAD_DIM), F32)],
        compiler_params=pltpu.CompilerParams(dimension_semantics=("arbitrary", "arbitrary"),
                                             vmem_limit_bytes=V7X_VMEM_LIMIT_BYTES),
    )(r, v, kkn, w, bb, kt, r, v, kkn, w, bb, kt)
    if not apply_post:
        return yf, yb, bonus, g, row2(ln_g), row2(ln_b)

    return pl.pallas_call(
        _rwkv_post_kernel,
        name="rwkv_post",
        grid=(bsz, nb),
        in_specs=[blk, blk, blk, blk, full(row2(ln_g)), full(row2(ln_b))],
        out_specs=blk,
        out_shape=act,
        compiler_params=pltpu.CompilerParams(dimension_semantics=("arbitrary", "arbitrary")),
    )(yf, yb, bonus, g, row2(ln_g), row2(ln_b))


GDN_GATE_LANES = 128
GDN_PADDED_COLS = 4 * GW + GDN_GATE_LANES


def _gdn_prep_kernel(x_ref, xprev_ref, xnext_ref, ab_ref, conv_ref, alog_ref, dtb_ref,
                     q_out, k_out, v_out, gb_out, *, ctx_blocks):
    i = pl.program_id(1)
    keep_prev, keep_next = _segment_edge_flags(i, pl.num_programs(1), ctx_blocks)
    x = x_ref[0]
    xm, xp = _shifted_rows(x, xprev_ref[0, 7:8, :] * keep_prev, xnext_ref[0, 0:1, :] * keep_next)
    y = xm * conv_ref[0:1, :] + x * conv_ref[1:2, :] + xp * conv_ref[2:3, :]
    y = y * jax.nn.sigmoid(y)
    head_sum = _segment_ones(GW, HEAD_DIM, F32)
    q = y[:, 0:GW]
    k = y[:, GW:2 * GW]
    q_out[0] = q * lax.rsqrt(_dot_f32(q * q, head_sum) + 1e-6) * (HEAD_DIM ** -0.5)
    k_out[0] = k * lax.rsqrt(_dot_f32(k * k, head_sum) + 1e-6)
    v_out[0] = y[:, 2 * GW:3 * GW]
    ab = ab_ref[0]
    lane = lax.broadcasted_iota(jnp.int32, ab.shape, 1)
    log_alpha = -jnp.exp(alog_ref[...]) * _softplus(ab + dtb_ref[...])
    gb_out[0] = jnp.where(lane < 2 * GROUP_HEADS, log_alpha, jax.nn.sigmoid(ab))


def _gdn_chunk_kernel(qf_ref, kf_ref, vf_ref, gf_ref, qb_ref, kb_ref, vb_ref, gb_ref, of_ref, ob_ref, st_ref):
    i = pl.program_id(1)
    c = GDN_CHUNK
    n_chunks = qf_ref.shape[1] // c

    @pl.when(i == 0)
    def _():
        st_ref[...] = jnp.zeros_like(st_ref)

    r = lax.broadcasted_iota(jnp.int32, (c, c), 0)
    s = lax.broadcasted_iota(jnp.int32, (c, c), 1)
    eye = r == s
    ones_cc = jnp.ones((c, c), F32)
    incl = (r >= s, r <= s)
    strict = (r > s, r < s)
    levels = []
    b = 1
    while b < c:
        levels.append(jnp.logical_and(r // (2 * b) == s // (2 * b), r // b != s // b))
        b *= 2
    dirs = ((qf_ref, kf_ref, vf_ref, gf_ref, of_ref), (qb_ref, kb_ref, vb_ref, gb_ref, ob_ref))

    bdot = lambda x, y: jnp.dot(x.astype(BF16), y.astype(BF16), preferred_element_type=F32)
    bdot_nt = lambda x, y: lax.dot_general(x.astype(BF16), y.astype(BF16), _NT, preferred_element_type=F32)
    bdot_tn = lambda x, y: lax.dot_general(x.astype(BF16), y.astype(BF16), (((0,), (0,)), ((), ())),
                                           preferred_element_type=F32)

    heads = [(d, h) for d in range(2) for h in range(GROUP_HEADS)]
    row0 = lambda cc, d: (cc if d == 0 else n_chunks - 1 - cc) * c
    gates = {(cc, d): dirs[d][3][0, row0(cc, d):row0(cc, d) + c, :] for cc in range(n_chunks) for d in range(2)}
    cum = {key: _select_sum(jnp.where(incl[key[1]], 1.0, 0.0), g) for key, g in gates.items()}
    tot = {key: _select_sum(ones_cc, g) for key, g in gates.items()}
    chains = [(cc, d, h) for cc in range(n_chunks) for d, h in heads]
    st = {}
    for cc, d, h in chains:
        cols = slice(h * HEAD_DIM, (h + 1) * HEAD_DIM)
        lg = d * GROUP_HEADS + h
        q, k, v = (dirs[d][n][0, row0(cc, d):row0(cc, d) + c, cols] for n in range(3))
        gc = cum[cc, d][:, lg:lg + 1]
        st[cc, d, h] = dict(q=q, k=k, v=v, gc=gc, gt=tot[cc, d][:, lg:lg + 1],
                            beta=gates[cc, d][:, 2 * GROUP_HEADS + lg:2 * GROUP_HEADS + lg + 1],
                            gc_row=_select_sum(ones_cc, jnp.where(eye, gc, 0.0)))
    for (cc, d, h), x in st.items():
        x["decay"] = jnp.exp(jnp.where(incl[d], x["gc"] - x["gc_row"], NEG_BIG))
        x["kb"] = x["k"] * x["beta"]
        gram = bdot_nt(jnp.concatenate([x["kb"], x["q"]], axis=0), x["k"])
        x["a"] = jnp.where(strict[d], gram[:c] * x["decay"], 0.0)
        x["qk"] = jnp.where(incl[d], gram[c:] * x["decay"], 0.0)
        x["inv"] = jnp.where(eye, 1.0, 0.0) - jnp.where(levels[0], x["a"], 0.0)
    for lvl in levels[1:]:
        for x in st.values():
            x["t"] = bdot(jnp.where(lvl, x["a"], 0.0), x["inv"])
        for x in st.values():
            x["inv"] = x["inv"] - bdot(x["inv"], x["t"])
    for x in st.values():
        x["eg"] = jnp.exp(x["gc"])
        x["sol"] = bdot(x["inv"], jnp.concatenate([x["v"] * x["beta"], x["kb"] * x["eg"]], axis=-1))
        x["qg"] = x["q"] * x["eg"]
        x["kg"] = x["k"] * jnp.exp(x["gt"] - x["gc"])
    state = {(d, h): st_ref[d, h] for d, h in heads}
    for cc in range(n_chunks):
        cur = [(key, st[(cc,) + key]) for key in heads]
        for key, x in cur:
            x["ws"] = bdot(x["sol"][:, HEAD_DIM:], state[key])
            x["qs"] = bdot(x["qg"], state[key])
        for key, x in cur:
            x["v_new"] = x["sol"][:, :HEAD_DIM] - x["ws"]
            x["o"] = x["qs"] + bdot(x["qk"], x["v_new"])
            x["upd"] = bdot_tn(x["kg"], x["v_new"])
        for key, x in cur:
            state[key] = state[key] * jnp.exp(x["gt"][0:1, :]) + x["upd"]
        for d in range(2):
            dirs[d][4][0, row0(cc, d):row0(cc, d) + c, :] = jnp.concatenate(
                [x["o"] for (dd, h), x in cur if dd == d], axis=-1)
    for d, h in heads:
        st_ref[d, h] = state[d, h]


def _gdn_post_rows(o, gate, g):
    head_mean = _segment_ones(GW, HEAD_DIM, F32) * (1.0 / HEAD_DIM)
    return o * lax.rsqrt(_dot_f32(o * o, head_mean) + 1e-6) * g * (gate * jax.nn.sigmoid(gate))


def _gdn_post_kernel(of_ref, ob_ref, gate_ref, g_ref, o_ref):
    o_ref[0] = _gdn_post_rows(of_ref[0] + ob_ref[0], gate_ref[0], g_ref[...])


def _gated_deltanet_pallas(f, conv_w, a_log, dt_bias, norm_g, *, ctx_len, apply_post=True):
    bsz, seq, width = f.shape
    tr = ROW_BLOCK
    assert ctx_len % tr == 0 and seq % tr == 0 and tr % GDN_CHUNK == 0
    if width == GDN_COLS:
        f = jnp.pad(f, ((0, 0), (0, 0), (0, GDN_PADDED_COLS - GDN_COLS)))
    assert f.shape[2] == GDN_PADDED_COLS
    nb, ctx_blocks = seq // tr, ctx_len // tr
    lane_pad = lambda a: jnp.pad(a.reshape(1, -1).astype(F32), ((0, 0), (0, GDN_GATE_LANES - a.size)))
    prev_spec, next_spec = _halo_specs(3 * GW, tr)
    full = lambda a: pl.BlockSpec(a.shape, lambda b, i: (0,) * a.ndim)
    act = jax.ShapeDtypeStruct((bsz, seq, GW), F32)
    gact = jax.ShapeDtypeStruct((bsz, seq, GDN_GATE_LANES), F32)
    blk = pl.BlockSpec((1, tr, GW), lambda b, i: (b, i, 0))
    gblk = pl.BlockSpec((1, tr, GDN_GATE_LANES), lambda b, i: (b, i, 0))
    gate_view = pl.BlockSpec((1, tr, GW), lambda b, i: (b, i, 3))
    ab_view = pl.BlockSpec((1, tr, GDN_GATE_LANES), lambda b, i: (b, i, 4 * GW // GDN_GATE_LANES))
    params = [conv_w.astype(F32), lane_pad(a_log), lane_pad(dt_bias)]
    q, k, v, gb = pl.pallas_call(
        functools.partial(_gdn_prep_kernel, ctx_blocks=ctx_blocks),
        name="gdn_prep",
        grid=(bsz, nb),
        in_specs=[pl.BlockSpec((1, tr, 3 * GW), lambda b, i: (b, i, 0)), prev_spec, next_spec(seq // 8), ab_view]
                 + [full(p) for p in params],
        out_specs=[blk, blk, blk, gblk],
        out_shape=[act, act, act, gact],
        compiler_params=pltpu.CompilerParams(dimension_semantics=("arbitrary", "arbitrary"),
                                             vmem_limit_bytes=V7X_VMEM_LIMIT_BYTES),
    )(f, f, f, f, *params)

    def bwd_block(i):
        return jnp.where(i < ctx_blocks, ctx_blocks - 1 - i, nb - 1 - (i - ctx_blocks))
    bblk = pl.BlockSpec((1, tr, GW), lambda b, i: (b, bwd_block(i), 0))
    bgblk = pl.BlockSpec((1, tr, GDN_GATE_LANES), lambda b, i: (b, bwd_block(i), 0))
    of, ob = pl.pallas_call(
        _gdn_chunk_kernel,
        name="gdn_chunks",
        grid=(bsz, nb),
        in_specs=[blk, blk, blk, gblk, bblk, bblk, bblk, bgblk],
        out_specs=[blk, bblk],
        out_shape=[act, act],
        scratch_shapes=[pltpu.VMEM((2, GROUP_HEADS, HEAD_DIM, HEAD_DIM), F32)],
        compiler_params=pltpu.CompilerParams(dimension_semantics=("arbitrary", "arbitrary"),
                                             vmem_limit_bytes=V7X_VMEM_LIMIT_BYTES),
    )(q, k, v, gb, q, k, v, gb)

    g_row = jnp.tile(norm_g.reshape(1, HEAD_DIM).astype(F32), (1, GROUP_HEADS))
    if not apply_post:
        return of, ob, f, g_row
    return pl.pallas_call(
        _gdn_post_kernel,
        name="gdn_post",
        grid=(bsz, nb),
        in_specs=[blk, blk, gate_view, full(g_row)],
        out_specs=blk,
        out_shape=act,
        compiler_params=pltpu.CompilerParams(dimension_semantics=("arbitrary", "arbitrary")),
    )(of, ob, f, g_row)


ROPE_PAIR = DIFF_HALF // 4


def _rope_tables(seq, ctx_len, q_scale):
    n = jnp.arange(seq - ctx_len, dtype=jnp.int32)
    row, col = n // GRID_W, n % GRID_W
    i = jnp.arange(HEAD_DIM)
    grp = (i % DIFF_HALF) // (2 * ROPE_PAIR)
    inv = ROPE_BASE ** (-(i % ROPE_PAIR).astype(F32) / ROPE_PAIR)
    pos = jnp.where(grp[None, :] == 0, row[:, None], col[:, None]).astype(F32)
    ang = pos * inv[None, :]
    sign = jnp.where((i % (2 * ROPE_PAIR)) < ROPE_PAIR, -1.0, 1.0)
    cos = jnp.concatenate([jnp.ones((ctx_len, HEAD_DIM), F32), jnp.cos(ang)], 0)
    sin = jnp.concatenate([jnp.zeros((ctx_len, HEAD_DIM), F32), jnp.sin(ang) * sign], 0)
    cos = jnp.tile(cos, (1, GROUP_HEADS))
    sin = jnp.tile(sin, (1, GROUP_HEADS))
    return jnp.concatenate([cos * q_scale, cos], 1), jnp.concatenate([sin * q_scale, sin], 1)


def _rotate_qk(qk, cos, sin):
    lane = lax.broadcasted_iota(jnp.int32, qk.shape, 1)
    partner = jnp.where(lane % (2 * ROPE_PAIR) < ROPE_PAIR,
                        pltpu.roll(qk, qk.shape[1] - ROPE_PAIR, axis=1), pltpu.roll(qk, ROPE_PAIR, axis=1))
    return qk * cos + partner * sin


def _store_qkv(q, k, v, q_out, k_out, v_out):
    q_out[...] = q.astype(BF16)
    for h in range(GROUP_HEADS):
        k_out[h] = k[:, h * HEAD_DIM:(h + 1) * HEAD_DIM].astype(BF16)
        v_out[h] = v[:, h * HEAD_DIM:(h + 1) * HEAD_DIM].astype(BF16)


_NT = (((1,), (1,)), ((), ()))


def _softmax_pv(s, v):
    m = jnp.max(s, axis=-1, keepdims=True)
    e = jnp.exp(s - m)
    return jnp.dot(e.astype(BF16), v, preferred_element_type=F32) / jnp.sum(e, axis=-1, keepdims=True)


def _diff_attn_kernel(q_ref, k_ref, v_ref, lam_ref, g_ref, o_ref, *, ctx_blocks, ctx_len):
    i = pl.program_id(1)
    lv = lam_ref[...]
    lam_init = lv[4:5, 0:1]
    lam = (jnp.exp(jnp.sum(lv[0:1] * lv[1:2], axis=-1, keepdims=True))
           - jnp.exp(jnp.sum(lv[2:3] * lv[3:4], axis=-1, keepdims=True)) + lam_init)
    lane = lax.broadcasted_iota(jnp.int32, (q_ref.shape[1], HEAD_DIM), 1)

    def attend(n_keys):
        outs = []
        for h in range(GROUP_HEADS):
            qh = q_ref[0, :, h * HEAD_DIM:(h + 1) * HEAD_DIM]
            kh = k_ref[h, 0, 0:n_keys, :]
            vh = v_ref[h, 0, 0:n_keys, :]
            zero = jnp.zeros_like(qh)
            s1 = lax.dot_general(jnp.where(lane < DIFF_HALF, qh, zero), kh, _NT, preferred_element_type=F32)
            s2 = lax.dot_general(jnp.where(lane >= DIFF_HALF, qh, zero), kh, _NT, preferred_element_type=F32)
            o = _softmax_pv(s1, vh) - lam * _softmax_pv(s2, vh)
            o = o * lax.rsqrt(jnp.mean(o * o, axis=-1, keepdims=True) + 1e-6) * g_ref[...] * (1.0 - lam_init)
            outs.append(o)
        o_ref[0] = jnp.concatenate(outs, axis=-1)

    @pl.when(i < ctx_blocks)
    def _():
        attend(ctx_len)

    @pl.when(i >= ctx_blocks)
    def _():
        attend(k_ref.shape[2])


def _diff_attention_pallas(q, k, v, lam_vecs, norm_g, *, ctx_len, lam_init):
    bsz, seq, _ = q.shape
    tq = ROW_BLOCK
    kv_spec = pl.BlockSpec((GROUP_HEADS, 1, seq, HEAD_DIM), lambda b, i: (0, b, 0, 0))
    lam_rows = jnp.concatenate([lam_vecs.astype(F32), jnp.full((1, lam_vecs.shape[1]), lam_init, F32)], 0)
    return pl.pallas_call(
        functools.partial(_diff_attn_kernel, ctx_blocks=ctx_len // tq, ctx_len=ctx_len),
        name="diff_attn",
        grid=(bsz, seq // tq),
        in_specs=[pl.BlockSpec((1, tq, GW), lambda b, i: (b, i, 0)), kv_spec, kv_spec,
                  pl.BlockSpec(lam_rows.shape, lambda b, i: (0, 0)),
                  pl.BlockSpec((1, HEAD_DIM), lambda b, i: (0, 0))],
        out_specs=pl.BlockSpec((1, tq, GW), lambda b, i: (b, i, 0)),
        out_shape=jax.ShapeDtypeStruct((bsz, seq, GW), F32),
        compiler_params=pltpu.CompilerParams(dimension_semantics=("arbitrary", "arbitrary"),
                                             vmem_limit_bytes=V7X_VMEM_LIMIT_BYTES),
    )(q, k, v, lam_rows, norm_g.reshape(1, HEAD_DIM).astype(F32))


NAT_TILE_ROWS = ROW_BLOCK // GRID_W
NAT_SLAB_ROWS = NAT_TILE_ROWS + WIN_H - 1


def _nat_slab_start(tile, n_rows):
    return np.clip(tile * NAT_TILE_ROWS - WIN_H // 2, 0, n_rows - NAT_SLAB_ROWS)


def _nat_bias_tables(rpb, n_rows):
    n_tiles = n_rows // NAT_TILE_ROWS
    nq, nk, w = NAT_TILE_ROWS, NAT_SLAB_ROWS, GRID_W
    cq, ck = np.arange(w)[:, None], np.arange(w)[None, :]
    d_col = np.clip(ck - cq, -(WIN_W - 1), WIN_W - 1) + WIN_W - 1
    col_1h = (d_col.reshape(-1)[:, None] == np.arange(2 * WIN_W - 1)[None, :]).astype(np.float32)
    c0 = np.clip(cq - WIN_W // 2, 0, w - WIN_W)
    col_ok = (ck >= c0) & (ck < c0 + WIN_W)
    tabs = []
    for tile in (0, 1, n_tiles - 1):
        r = tile * nq + np.arange(nq)[:, None]
        kr = _nat_slab_start(tile, n_rows) + np.arange(nk)[None, :]
        rs = np.clip(r - WIN_H // 2, 0, n_rows - WIN_H)
        row_ok = (kr >= rs) & (kr < rs + WIN_H)
        d_row = np.clip(kr - r + WIN_H - 1, 0, 2 * WIN_H - 2)
        row_1h = (d_row.reshape(-1)[:, None] == np.arange(2 * WIN_H - 1)[None, :]).astype(np.float32)
        t = jnp.einsum('pa,hab,cb->hpc', row_1h, rpb.astype(F32), col_1h, precision=HIGHEST)
        t = t.reshape(GROUP_HEADS, nq, nk, w, w).transpose(0, 1, 3, 2, 4)
        ok = row_ok[:, None, :, None] & col_ok[None, :, None, :]
        tabs.append(jnp.where(ok[None], t, NEG_BIG).reshape(GROUP_HEADS, nq * w, nk * w))
    return jnp.stack(tabs)


def _nat_attn_kernel(q_ref, k_ref, v_ref, bias_ref, o_ref, *, ctx_blocks, ctx_len, n_rows):
    i = pl.program_id(1)
    n_slab = NAT_SLAB_ROWS * GRID_W

    def heads(fn):
        o_ref[0] = jnp.concatenate(
            [fn(h, q_ref[0, :, h * HEAD_DIM:(h + 1) * HEAD_DIM]) for h in range(GROUP_HEADS)], axis=-1)

    @pl.when(i < ctx_blocks)
    def _():
        def ctx_only(h, qh):
            s = lax.dot_general(qh, k_ref[h, 0, 0:ctx_len, :], _NT, preferred_element_type=F32)
            return _softmax_pv(s, v_ref[h, 0, 0:ctx_len, :])
        heads(ctx_only)

    @pl.when(i >= ctx_blocks)
    def _():
        tile = i - ctx_blocks
        start = jnp.clip(tile * NAT_TILE_ROWS - WIN_H // 2, 0, n_rows - NAT_SLAB_ROWS)
        off = pl.multiple_of(ctx_len + start * GRID_W, GRID_W)

        def windowed(h, qh):
            s_w = lax.dot_general(qh, k_ref[h, 0, pl.ds(off, n_slab), :], _NT,
                                  preferred_element_type=F32) + bias_ref[0, h]
            s_c = lax.dot_general(qh, k_ref[h, 0, 0:ctx_len, :], _NT, preferred_element_type=F32)
            m = jnp.maximum(jnp.max(s_w, axis=-1, keepdims=True), jnp.max(s_c, axis=-1, keepdims=True))
            e_w = jnp.exp(s_w - m)
            e_c = jnp.exp(s_c - m)
            den = jnp.sum(e_w, axis=-1, keepdims=True) + jnp.sum(e_c, axis=-1, keepdims=True)
            num = (jnp.dot(e_w.astype(BF16), v_ref[h, 0, pl.ds(off, n_slab), :], preferred_element_type=F32)
                   + jnp.dot(e_c.astype(BF16), v_ref[h, 0, 0:ctx_len, :], preferred_element_type=F32))
            return num / den
        heads(windowed)


def _nat_attention_pallas(q, k, v, rpb, *, ctx_len):
    bsz, seq, _ = q.shape
    tq = ROW_BLOCK
    ctx_blocks = ctx_len // tq
    n_rows = (seq - ctx_len) // GRID_W
    n_tiles = n_rows // NAT_TILE_ROWS
    assert n_rows >= NAT_SLAB_ROWS and n_tiles >= 3
    bias = _nat_bias_tables(rpb, n_rows)

    def variant(i):
        tile = i - ctx_blocks
        return jnp.where(tile <= 0, 0, jnp.where(tile >= n_tiles - 1, 2, 1))
    kv_spec = pl.BlockSpec((GROUP_HEADS, 1, seq, HEAD_DIM), lambda b, i: (0, b, 0, 0))
    return pl.pallas_call(
        functools.partial(_nat_attn_kernel, ctx_blocks=ctx_blocks, ctx_len=ctx_len, n_rows=n_rows),
        name="nat_attn",
        grid=(bsz, seq // tq),
        in_specs=[pl.BlockSpec((1, tq, GW), lambda b, i: (b, i, 0)), kv_spec, kv_spec,
                  pl.BlockSpec((1,) + bias.shape[1:], lambda b, i: (variant(i), 0, 0, 0))],
        out_specs=pl.BlockSpec((1, tq, GW), lambda b, i: (b, i, 0)),
        out_shape=jax.ShapeDtypeStruct((bsz, seq, GW), F32),
        compiler_params=pltpu.CompilerParams(dimension_semantics=("arbitrary", "arbitrary"),
                                             vmem_limit_bytes=V7X_VMEM_LIMIT_BYTES),
    )(q, k, v, bias)


N_MOD = 6
MATMUL_ROWS = 512


def _ada_kernel(c_ref, w_ref, b_ref, o_ref):
    c = c_ref[...]
    o_ref[...] = _dot_f32(c * jax.nn.sigmoid(c), w_ref[...]) + b_ref[...]


def _ada_modulation(c, c_ctx, w_ada, b_ada):
    bsz, d = c.shape
    rows = 8 * ((bsz + 1 + 7) // 8)
    cc = jnp.zeros((rows, d), F32).at[:bsz].set(c).at[bsz].set(c_ctx)
    tn = d
    m = pl.pallas_call(
        _ada_kernel,
        name="ada_modulation",
        grid=(w_ada.shape[1] // tn,),
        in_specs=[pl.BlockSpec((rows, d), lambda j: (0, 0)),
                  pl.BlockSpec((d, tn), lambda j: (0, j)),
                  pl.BlockSpec((1, tn), lambda j: (0, j))],
        out_specs=pl.BlockSpec((rows, tn), lambda j: (0, j)),
        out_shape=jax.ShapeDtypeStruct((rows, w_ada.shape[1]), F32),
        compiler_params=pltpu.CompilerParams(dimension_semantics=("arbitrary",)),
    )(cc, w_ada, b_ada.reshape(1, -1))
    lat = m[:bsz].reshape(bsz, 1, N_MOD, d)
    ctx = jnp.broadcast_to(m[bsz].reshape(1, 1, N_MOD, d), (bsz, 1, N_MOD, d))
    return jnp.concatenate([ctx, lat], axis=1)


def _ln(x):
    mu = jnp.mean(x, axis=-1, keepdims=True)
    xc = x - mu
    return xc * lax.rsqrt(jnp.mean(xc * xc, axis=-1, keepdims=True) + LN_EPS)


def _mod_spec(d, ctx_blocks):
    return pl.BlockSpec((1, 1, N_MOD, d), lambda b, i: (b, jnp.where(i < ctx_blocks, 0, 1), 0, 0))


def _in_proj_kernel(h_ref, mod0_ref, mod1_ref, wa_ref, wb_ref, wc_ref, wd_ref, cos_ref, sin_ref,
                    qa_out, ka_out, va_out, pb_out, pc_out, qd_out, kd_out, vd_out, *, seq, ctx_len):
    h = h_ref[...]
    shift, scale = _flat_row_mod(mod0_ref, mod1_ref, pl.program_id(0), h.shape, seq, ctx_len, (0, 1))
    x = (_ln(h) * (1.0 + scale) + shift).astype(BF16)
    pa = jnp.dot(x, wa_ref[...], preferred_element_type=F32)
    qk = _rotate_qk(pa[:, :2 * GW], cos_ref[...], sin_ref[...])
    _store_qkv(qk[:, :GW], qk[:, GW:], pa[:, 2 * GW:], qa_out, ka_out, va_out)
    pb_out[...] = jnp.dot(x, wb_ref[...], preferred_element_type=F32)
    pc_out[...] = jnp.dot(x, wc_ref[...], preferred_element_type=F32)
    pd = jnp.dot(x, wd_ref[...], preferred_element_type=F32)
    _store_qkv(pd[:, :GW] * (HEAD_DIM ** -0.5), pd[:, GW:2 * GW], pd[:, 2 * GW:], qd_out, kd_out, vd_out)


def _in_proj_pallas(x, mod, ws, cos, sin, *, seq, ctx_len):
    m, k = x.shape
    tm = MATMUL_ROWS
    row = lambda n: pl.BlockSpec((tm, n), lambda i: (i, 0))
    hm = pl.BlockSpec((GROUP_HEADS, tm, HEAD_DIM), lambda i: (0, i, 0))
    q_shape = jax.ShapeDtypeStruct((m, GW), BF16)
    hm_shape = jax.ShapeDtypeStruct((GROUP_HEADS, m, HEAD_DIM), BF16)
    f32 = lambda n: jax.ShapeDtypeStruct((m, n), F32)
    return pl.pallas_call(
        functools.partial(_in_proj_kernel, seq=seq, ctx_len=ctx_len),
        name="in_proj",
        grid=(m // tm,),
        in_specs=[row(k)] + _flat_mod_specs(tm, seq, k, 1) + [pl.BlockSpec(w.shape, lambda i: (0, 0)) for w in ws]
                 + [row(2 * GW), row(2 * GW)],
        out_specs=[row(GW), hm, hm, row(ws[1].shape[1]), row(ws[2].shape[1]), row(GW), hm, hm],
        out_shape=[q_shape, hm_shape, hm_shape, f32(ws[1].shape[1]), f32(ws[2].shape[1]), q_shape, hm_shape, hm_shape],
        compiler_params=pltpu.CompilerParams(dimension_semantics=("arbitrary",),
                                             vmem_limit_bytes=V7X_VMEM_LIMIT_BYTES),
    )(x, mod, mod, *ws, cos, sin)


def _post_norm_rows(h, gate, y, g, b):
    return _ln(DN_ALPHA * h + gate * y) * g + b


def _out_proj_kernel(ya_ref, rf_ref, rb_ref, bonus_ref, rg_ref, lng_ref, lnb_ref, of_ref, ob_ref, gate_ref, gn_ref,
                     yd_ref, w_ref, h_ref, mod_ref, g_ref, b_ref, o_ref):
    yb = _rwkv_post_rows(rf_ref[0] + rb_ref[0], bonus_ref[0], rg_ref[0], lng_ref[...], lnb_ref[...])
    yc = _gdn_post_rows(of_ref[0] + ob_ref[0], gate_ref[0], gn_ref[...])
    mix = None
    for n, y in enumerate((ya_ref[0], yb, yc, yd_ref[0])):
        part = jnp.dot(y.astype(BF16), w_ref[n * GW:(n + 1) * GW, :], preferred_element_type=F32)
        mix = part if mix is None else mix + part
    o_ref[0] = _post_norm_rows(h_ref[0], mod_ref[0, 0, 2:3, :], mix, g_ref[...], b_ref[...])


def _out_proj_post_norm(ya, rwkv_parts, gdn_parts, yd, w_out, hs, mod, g, b, *, ctx_len):
    bsz, seq, d = hs.shape
    tr = ROW_BLOCK
    yblk = pl.BlockSpec((1, tr, GW), lambda bb, i: (bb, i, 0))
    blk = pl.BlockSpec((1, tr, d), lambda bb, i: (bb, i, 0))
    row = pl.BlockSpec((1, d), lambda bb, i: (0, 0))
    grow = pl.BlockSpec((1, GW), lambda bb, i: (0, 0))
    gate_view = pl.BlockSpec((1, tr, GW), lambda bb, i: (bb, i, 3))
    return pl.pallas_call(
        _out_proj_kernel,
        name="out_proj_post_norm",
        grid=(bsz, seq // tr),
        in_specs=[yblk] + [yblk] * 4 + [grow, grow] + [yblk, yblk, gate_view, grow] + [yblk]
                 + [pl.BlockSpec(w_out.shape, lambda bb, i: (0, 0)), blk, _mod_spec(d, ctx_len // tr), row, row],
        out_specs=blk,
        out_shape=jax.ShapeDtypeStruct((bsz, seq, d), F32),
        compiler_params=pltpu.CompilerParams(dimension_semantics=("arbitrary", "arbitrary")),
    )(ya, *rwkv_parts, *gdn_parts, yd, w_out, hs, mod, g.reshape(1, d), b.reshape(1, d))


def kernel(x, c, ctx, c_ctx, w_ada, b_ada, w_in, w_out, ln_mix_g, ln_mix_b, ln_ffn_g, ln_ffn_b, diff_lam, diff_norm_g, rwkv_mu, rwkv_w0, rwkv_w2, rwkv_a0, rwkv_a2, rwkv_g2, rwkv_kk, rwkv_ka, rwkv_rk, rwkv_ln_g, rwkv_ln_b, gdn_conv, gdn_a_log, gdn_dt_bias, gdn_norm_g, nat_rpb, peer_wq, peer_keys, peer_u, peer_v):
    dtype = x.dtype
    bsz, ctx_len = ctx.shape[0], ctx.shape[1]
    hs = jnp.concatenate([ctx, x], axis=1)
    seq = hs.shape[1]
    col_sizes = [ATTN_COLS, RWKV_COLS, GDN_COLS, ATTN_COLS]
    cos_a, sin_a = (jnp.tile(t, (bsz, 1)) for t in _rope_tables(seq, ctx_len, DIFF_HALF ** -0.5))
    col_offs = np.cumsum([0] + col_sizes)
    d_model = hs.shape[2]
    for l in range(DEPTH):
        lam_init = 0.8 - 0.6 * math.exp(-0.3 * l)
        mod = _ada_modulation(c, c_ctx, w_ada[l], b_ada[l])
        w_in_b = w_in[l].astype(BF16)
        w_groups = [w_in_b[:, col_offs[n]:col_offs[n + 1]] for n in range(4)]
        w_groups[2] = jnp.pad(w_groups[2], ((0, 0), (0, GDN_PADDED_COLS - GDN_COLS)))
        qa, ka, va, pb, pc, qd, kd, vd = _in_proj_pallas(hs.reshape(bsz * seq, d_model), mod, w_groups, cos_a, sin_a,
                                                         seq=seq, ctx_len=ctx_len)
        rows3 = lambda p: p.reshape(bsz, seq, p.shape[1])
        heads4 = lambda p: p.reshape(GROUP_HEADS, bsz, seq, HEAD_DIM)
        qa, pb, pc, qd = rows3(qa), rows3(pb), rows3(pc), rows3(qd)
        ka, va, kd, vd = heads4(ka), heads4(va), heads4(kd), heads4(vd)
        ya = _diff_attention_pallas(qa, ka, va, diff_lam[l], diff_norm_g[l], ctx_len=ctx_len, lam_init=lam_init)
        yb_parts = _rwkv7_pallas(pb, rwkv_mu[l], rwkv_w0[l], rwkv_w2[l], rwkv_a0[l], rwkv_a2[l], rwkv_g2[l],
                                 rwkv_kk[l], rwkv_ka[l], rwkv_rk[l], rwkv_ln_g[l], rwkv_ln_b[l], ctx_len=ctx_len,
                                 apply_post=False)
        yc_parts = _gated_deltanet_pallas(pc, gdn_conv[l], gdn_a_log[l], gdn_dt_bias[l], gdn_norm_g[l],
                                          ctx_len=ctx_len, apply_post=False)
        yd = _nat_attention_pallas(qd, kd, vd, nat_rpb[l], ctx_len=ctx_len)
        hs = _out_proj_post_norm(ya, yb_parts, yc_parts, yd, w_out[l].astype(BF16), hs, mod, ln_mix_g[l],
                                 ln_mix_b[l], ctx_len=ctx_len)
        wq_b = peer_wq[l].astype(BF16)
        keys_b = peer_keys[l].reshape(2 * PEER_HEADS, PEER_KEYS, PEER_HALF).astype(BF16)
        u_b = peer_u[l].astype(BF16)
        vt_b = peer_v[l].astype(BF16).T
        ffn_ctx = ctx_len
        if l == DEPTH - 1:
            hs, ffn_ctx = hs[:, ctx_len:], 0
        hs = _peer_ffn(hs, mod, ln_ffn_g[l], ln_ffn_b[l], wq_b, keys_b, u_b, vt_b, ctx_len=ffn_ctx)
    return hs.astype(dtype)
```

```python
import functools
import math

import jax
import jax.numpy as jnp
import numpy as np
from jax import lax
from jax.experimental import pallas as pl
from jax.experimental.pallas import tpu as pltpu

D_MODEL = 1024
DEPTH = 2
GRID_W = 64
HEAD_DIM = 64
N_GROUPS = 4
GROUP_HEADS = D_MODEL // (N_GROUPS * HEAD_DIM)
GW = GROUP_HEADS * HEAD_DIM
DIFF_HALF = HEAD_DIM // 2
ROPE_BASE = 10000.0
DECAY_LORA = 64
ICLR_LORA = 64
GATE_LORA = 128
RWKV_GN_EPS = 64e-5
RWKV_COLS = 3 * GW + 2 * DECAY_LORA + 2 * ICLR_LORA + GATE_LORA
GDN_CHUNK = 64
GDN_COLS = 4 * GW + 4 * GROUP_HEADS
WIN_H = 8
WIN_W = 16
ATTN_COLS = 3 * GW
PEER_HEADS = 8
PEER_KEYS = 128
PEER_EXPERTS = PEER_KEYS * PEER_KEYS
PEER_QDIM = 256
PEER_HALF = PEER_QDIM // 2
PEER_TOPK = 16
DN_ALPHA = (2 * DEPTH) ** 0.25
LN_EPS = 1e-5

F32 = jnp.float32
BF16 = jnp.bfloat16

V7X_VMEM_LIMIT_BYTES = 56 * 1024 * 1024
NEG_BIG = -3.0e38


PEER_STAT_ROWS = 4 * PEER_HEADS
LOG2E = 1.4426950408889634
PEER_POW_HEADS = 4
assert PEER_TOPK == 16


def _flat_mod_specs(tm, seq, d, grid_rank):
    first = lambda i, *_: ((i * tm) // seq, 0, 0, 0)
    last = lambda i, *_: ((i * tm + tm - 1) // seq, 0, 0, 0)
    assert grid_rank in (1, 2) and tm <= seq
    return [pl.BlockSpec((1, 2, N_MOD, d), first), pl.BlockSpec((1, 2, N_MOD, d), last)]


def _flat_row_mod(mod0_ref, mod1_ref, i, shape, seq, ctx_len, rows):
    tm = shape[0]
    r0 = i * tm
    b0 = r0 // seq
    boundary = (b0 + 1) * seq
    idx = r0 + lax.broadcasted_iota(jnp.int32, shape, 0)
    second = idx >= boundary
    is_ctx = (idx - jnp.where(second, boundary, b0 * seq)) < ctx_len
    out = []
    for k in rows:
        v = jnp.where(second, mod1_ref[0, 1, k:k + 1, :], mod0_ref[0, 1, k:k + 1, :])
        out.append(jnp.where(is_ctx, mod0_ref[0, 0, k:k + 1, :], v) if ctx_len else v)
    return out


def _topk_rows(x, k):
    q = x.shape[0] // 4
    lv = [x[i * q:(i + 1) * q] for i in range(4)]
    for a, b in ((0, 1), (2, 3), (0, 2), (1, 3), (1, 2)):
        lv[a], lv[b] = jnp.maximum(lv[a], lv[b]), jnp.minimum(lv[a], lv[b])
    rows = []
    for i in range(k):
        m = jnp.max(lv[0], axis=0, keepdims=True)
        rows.append(m)
        if i + 1 < k:
            hit = lv[0] == m
            lv = [jnp.where(hit, lv[n + 1], lv[n]) for n in range(3)] + [jnp.where(hit, NEG_BIG, lv[3])]
    return rows


def _peer_score_kernel(h_ref, mod0_ref, mod1_ref, wq_ref, keys_ref, x_out, s_ref, st_ref, *, seq, ctx_len):
    h = h_ref[...]
    shift, scale = _flat_row_mod(mod0_ref, mod1_ref, pl.program_id(0), h.shape, seq, ctx_len, (3, 4))
    x = (_ln(h) * (1.0 + scale) + shift).astype(BF16)
    x_out[...] = x
    q = jnp.dot(x, wq_ref[...], preferred_element_type=F32).astype(BF16)
    stats = []
    for h in range(PEER_HEADS):
        tops = []
        for p in range(2):
            hp = 2 * h + p
            s_t = lax.dot_general(keys_ref[hp], q[:, hp * PEER_HALF:(hp + 1) * PEER_HALF],
                                  (((1,), (1,)), ((), ())), preferred_element_type=F32)
            s_ref[hp] = s_t
            tops.append(_topk_rows(s_t, PEER_TOPK + 1))
        a, b = tops
        pad = [jnp.full_like(a[0], NEG_BIG)] * 7
        b_head = jnp.concatenate(b[:8], axis=0)
        cand = jnp.concatenate([a[0] + jnp.concatenate(b + pad, axis=0)]
                               + [a[i] + b_head for i in range(1, 8)]
                               + [jnp.concatenate(a[8:] + pad, axis=0) + b[0]], axis=0)
        best_cand = _topk_rows(cand, PEER_TOPK + 1)
        kth, runner_up = best_cand[PEER_TOPK - 1], best_cand[PEER_TOPK]
        best = a[0] + b[0]
        z = jnp.sum(jnp.where(cand >= kth, jnp.exp(cand - best), 0.0), axis=0, keepdims=True)
        stats += [0.5 * (kth + runner_up), tops[0][0], tops[1][0], 1.0 / z]
    st_ref[...] = jnp.concatenate(stats, axis=0)


def _peer_expert_kernel(x_ref, s_ref, st_ref, u_ref, vt_ref, h_ref, mod0_ref, mod1_ref, g_ref, b_ref, o_ref,
                        e_ref, thr_ref, lin_ref, acc_ref, w_ref, *, rows_per_step, n_chunks, seq, ctx_len):
    j = pl.program_id(1)
    cur = lax.rem(j, 2)

    @pl.when(j == 0)
    def _():
        acc_ref[...] = jnp.zeros_like(acc_ref)
        w_ref[1] = jnp.zeros(w_ref.shape[1:], w_ref.dtype)
        for h in range(PEER_HEADS):
            a0 = st_ref[4 * h + 1:4 * h + 2, :]
            b0 = st_ref[4 * h + 2:4 * h + 3, :]
            rz = st_ref[4 * h + 3:4 * h + 4, :]
            l1 = (s_ref[2 * h] - a0 + jnp.log(rz)) * LOG2E
            l2 = (s_ref[2 * h + 1] - b0) * LOG2E
            e_ref[2 * h] = l1 if h < PEER_POW_HEADS else jnp.exp2(l1)
            e_ref[2 * h + 1] = l2
            if h >= PEER_POW_HEADS:
                lin_ref[h - PEER_POW_HEADS] = jnp.exp2(l2)
            thr_ref[h] = (st_ref[4 * h:4 * h + 1, :] - s_ref[2 * h] - b0) * LOG2E

    @pl.when(j < n_chunks)
    def _():
        slab = 2 * PEER_KEYS
        n_slabs = rows_per_step * PEER_KEYS // slab
        act_slab = lambda n: lax.dot_general(u_ref[n * slab:(n + 1) * slab, :], x_ref[...], (((1,), (1,)), ((), ())),
                                             preferred_element_type=F32)
        acts = [act_slab(n) for n in range(n_slabs)]
        acc_ref[...] += jnp.dot(vt_ref[...], w_ref[1 - cur], preferred_element_type=F32)
        assert rows_per_step % 8 == 0
        for il in range(rows_per_step):
            base = pl.multiple_of(j * rows_per_step + (il // 8) * 8, 8)
            r = il % 8
            o = il * PEER_KEYS
            act = acts[o // slab][o % slab:o % slab + PEER_KEYS]
            act = 0.5 * act * (1.0 + lax.erf(act * (2.0 ** -0.5)))
            gate = None
            for h in range(PEER_HEADS):
                l2 = e_ref[2 * h + 1]
                thr_row = thr_ref[h, pl.ds(base, 8), :][r:r + 1, :]
                half1 = e_ref[2 * h, pl.ds(base, 8), :][r:r + 1, :]
                pair = jnp.exp2(l2 + half1) if h < PEER_POW_HEADS else lin_ref[h - PEER_POW_HEADS] * half1
                term = jnp.where(l2 >= thr_row, pair, 0.0)
                gate = term if gate is None else gate + term
            w_ref[cur, o:o + PEER_KEYS, :] = (act * gate).astype(BF16)

    @pl.when(j == n_chunks)
    def _():
        ffn = (acc_ref[...] + jnp.dot(vt_ref[...], w_ref[1 - cur], preferred_element_type=F32)).T
        h = h_ref[...]
        gate, = _flat_row_mod(mod0_ref, mod1_ref, pl.program_id(0), h.shape, seq, ctx_len, (5,))
        o_ref[...] = _post_norm_rows(h, gate, ffn, g_ref[...], b_ref[...])


def _peer_pallas(h, mod, g, b, w_q, keys, u_tab, vt_tab, *, seq, ctx_len, tb, rows_per_step):
    n_tok, d = h.shape
    nhp = 2 * PEER_HEADS
    x, s, st = pl.pallas_call(
        functools.partial(_peer_score_kernel, seq=seq, ctx_len=ctx_len),
        name="peer_scores",
        grid=(n_tok // tb,),
        in_specs=[pl.BlockSpec((tb, d), lambda i: (i, 0))] + _flat_mod_specs(tb, seq, d, 1)
                 + [pl.BlockSpec(w_q.shape, lambda i: (0, 0)),
                    pl.BlockSpec(keys.shape, lambda i: (0, 0, 0))],
        out_specs=[pl.BlockSpec((tb, d), lambda i: (i, 0)),
                   pl.BlockSpec((nhp, PEER_KEYS, tb), lambda i: (0, 0, i)),
                   pl.BlockSpec((PEER_STAT_ROWS, tb), lambda i: (0, i))],
        out_shape=[jax.ShapeDtypeStruct((n_tok, d), BF16),
                   jax.ShapeDtypeStruct((nhp, PEER_KEYS, n_tok), F32),
                   jax.ShapeDtypeStruct((PEER_STAT_ROWS, n_tok), F32)],
        compiler_params=pltpu.CompilerParams(dimension_semantics=("arbitrary",),
                                             vmem_limit_bytes=V7X_VMEM_LIMIT_BYTES),
    )(h, mod, mod, w_q, keys)
    ec = rows_per_step * PEER_KEYS
    n_chunks = PEER_EXPERTS // ec
    row = pl.BlockSpec((1, d), lambda i, j: (0, 0))
    return pl.pallas_call(
        functools.partial(_peer_expert_kernel, rows_per_step=rows_per_step, n_chunks=n_chunks, seq=seq,
                          ctx_len=ctx_len),
        name="peer_experts",
        grid=(n_tok // tb, n_chunks + 1),
        in_specs=[pl.BlockSpec((tb, d), lambda i, j: (i, 0)),
                  pl.BlockSpec((nhp, PEER_KEYS, tb), lambda i, j: (0, 0, i)),
                  pl.BlockSpec((PEER_STAT_ROWS, tb), lambda i, j: (0, i)),
                  pl.BlockSpec((ec, d), lambda i, j: (jnp.minimum(j, n_chunks - 1), 0)),
                  pl.BlockSpec((d, ec), lambda i, j: (0, jnp.maximum(j - 1, 0))),
                  pl.BlockSpec((tb, d), lambda i, j: (i, 0))] + _flat_mod_specs(tb, seq, d, 2) + [row, row],
        out_specs=pl.BlockSpec((tb, d), lambda i, j: (i, 0)),
        out_shape=jax.ShapeDtypeStruct((n_tok, d), F32),
        scratch_shapes=[pltpu.VMEM((nhp, PEER_KEYS, tb), F32),
                        pltpu.VMEM((PEER_HEADS, PEER_KEYS, tb), F32),
                        pltpu.VMEM((PEER_HEADS - PEER_POW_HEADS, PEER_KEYS, tb), F32),
                        pltpu.VMEM((d, tb), F32),
                        pltpu.VMEM((2, ec, tb), BF16)],
        compiler_params=pltpu.CompilerParams(dimension_semantics=("arbitrary", "arbitrary"),
                                             vmem_limit_bytes=V7X_VMEM_LIMIT_BYTES),
    )(x, s, st, u_tab, vt_tab, h, mod, mod, g.reshape(1, d).astype(F32), b.reshape(1, d).astype(F32))


def _peer_ffn(hs, mod, g, b, w_q, keys, u_tab, vt_tab, *, ctx_len):
    bsz, t, d = hs.shape
    n_tok = bsz * t
    tb = 512 if n_tok % 512 == 0 else 256
    out = _peer_pallas(hs.reshape(n_tok, d), mod, g, b, w_q, keys, u_tab, vt_tab, seq=t, ctx_len=ctx_len, tb=tb,
                       rows_per_step=16)
    return out.reshape(bsz, t, d)


ROW_BLOCK = 256
HIGHEST = lax.Precision.HIGHEST


def _dot_f32(a, b):
    return jnp.dot(a, b, precision=HIGHEST, preferred_element_type=F32)


def _bf16_terms(x):
    hi = x.astype(BF16)
    rest = x - hi.astype(F32)
    mid = rest.astype(BF16)
    return hi, mid, (rest - mid.astype(F32)).astype(BF16)


def _select_sum(sel, x):
    n = x.shape[1]
    out = jnp.dot(sel.astype(BF16), jnp.concatenate(_bf16_terms(x), axis=1), preferred_element_type=F32)
    return out[:, :n] + out[:, n:2 * n] + out[:, 2 * n:]


def _sum_select(x, sel):
    m = x.shape[0]
    out = jnp.dot(jnp.concatenate(_bf16_terms(x), axis=0), sel.astype(BF16), preferred_element_type=F32)
    return out[:m] + out[m:2 * m] + out[2 * m:]


def _segment_ones(n, seg, dtype):
    r = lax.broadcasted_iota(jnp.int32, (n, n), 0) // seg
    c = lax.broadcasted_iota(jnp.int32, (n, n), 1) // seg
    return jnp.where(r == c, 1.0, 0.0).astype(dtype)


def _shifted_rows(x, prev_row, next_row):
    t = x.shape[0]
    rows = lax.broadcasted_iota(jnp.int32, x.shape, 0)
    xm = jnp.where(rows == 0, prev_row, pltpu.roll(x, 1, axis=0))
    xp = jnp.where(rows == t - 1, next_row, pltpu.roll(x, t - 1, axis=0))
    return xm, xp


def _segment_edge_flags(i, n_blocks, ctx_blocks):
    is_start = jnp.logical_or(i == 0, i == ctx_blocks)
    is_end = jnp.logical_or(i == ctx_blocks - 1, i == n_blocks - 1)
    return jnp.where(is_start, 0.0, 1.0), jnp.where(is_end, 0.0, 1.0)


def _halo_specs(width, tr):
    g = tr // 8
    prev = pl.BlockSpec((1, 8, width), lambda b, i: (b, jnp.maximum(i * g - 1, 0), 0))
    nxt = lambda n_groups: pl.BlockSpec((1, 8, width), lambda b, i: (b, jnp.minimum((i + 1) * g, n_groups - 1), 0))
    return prev, nxt


def _softplus(z):
    return jnp.maximum(z, 0.0) + jnp.log1p(jnp.exp(-jnp.abs(z)))


def _rwkv_prep_kernel(x_ref, xprev_ref, xnext_ref, mu_ref, w0_ref, w2_ref, a0_ref, a2_ref, g2_ref,
                      kk_ref, ka_ref, rk_ref,
                      r_out, v_out, kkn_out, g_out, bonus_out, w_out, b_out, kt_out, *, ctx_blocks):
    i = pl.program_id(1)
    keep_prev, keep_next = _segment_edge_flags(i, pl.num_programs(1), ctx_blocks)
    x = x_ref[0]
    xm, xp = _shifted_rows(x, xprev_ref[0, 7:8, :] * keep_prev, xnext_ref[0, 0:1, :] * keep_next)
    mu0 = mu_ref[0:1, :]
    mu1 = mu_ref[1:2, :]
    f = xm * mu0 + x * (1.0 - mu0 - mu1) + xp * mu1
    r = f[:, 0:GW]
    k = f[:, GW:2 * GW]
    v = f[:, 2 * GW:3 * GW]
    o = 3 * GW
    wd = jnp.tanh(f[:, o:o + 2 * DECAY_LORA])
    ad = f[:, o + 2 * DECAY_LORA:o + 2 * DECAY_LORA + 2 * ICLR_LORA]
    gd = f[:, o + 2 * DECAY_LORA + 2 * ICLR_LORA:]
    w_raw = w0_ref[...] + _dot_f32(wd, w2_ref[...])
    log_decay = -jnp.exp(-_softplus(-w_raw) - 0.5)
    a = jax.nn.sigmoid(a0_ref[...] + _dot_f32(ad, a2_ref[...]))
    g = _dot_f32(jax.nn.sigmoid(gd), g2_ref[...])
    head_sum = _segment_ones(GW, HEAD_DIM, F32)
    kx = k * kk_ref[...]
    kkn = kx * lax.rsqrt(_dot_f32(kx * kx, head_sum) + 1e-6)
    kd_sum = jnp.zeros_like(k)
    for d in range(2):
        a_d = a[:, d * GW:(d + 1) * GW]
        kd = k * (1.0 + (a_d - 1.0) * ka_ref[...])
        kd_sum = kd_sum + kd
        w_out[d, 0] = log_decay[:, d * GW:(d + 1) * GW]
        b_out[d, 0] = kkn * a_d
        kt_out[d, 0] = kd
    r_out[0] = r
    v_out[0] = v
    kkn_out[0] = kkn
    g_out[0] = g
    bonus_out[0] = _dot_f32(r * kd_sum * rk_ref[...], head_sum) * v


RWKV_CHUNK = 64


def _rwkv_chunk_kernel(rf_ref, vf_ref, kkf_ref, wf_ref, bf_ref, ktf_ref,
                       rb_ref, vb_ref, kkb_ref, wb_ref, bb_ref, ktb_ref, yf_ref, yb_ref, st_ref):
    i = pl.program_id(1)
    c = RWKV_CHUNK
    n_chunks = rf_ref.shape[1] // c

    @pl.when(i == 0)
    def _():
        st_ref[...] = jnp.zeros_like(st_ref)

    rr = lax.broadcasted_iota(jnp.int32, (c, c), 0)
    ss = lax.broadcasted_iota(jnp.int32, (c, c), 1)
    eye = rr == ss
    ones_cc = jnp.ones((c, c), F32)
    incl = (rr >= ss, rr <= ss)
    strict = (rr > ss, rr < ss)
    levels = []
    blk = 1
    while blk < c:
        levels.append(jnp.logical_and(rr // (2 * blk) == ss // (2 * blk), rr // blk != ss // blk))
        blk *= 2
    bdot = lambda x, y: jnp.dot(x.astype(BF16), y.astype(BF16), preferred_element_type=F32)
    bdot_nt = lambda x, y: lax.dot_general(x.astype(BF16), y.astype(BF16), _NT, preferred_element_type=F32)
    bdot_tn = lambda x, y: lax.dot_general(x.astype(BF16), y.astype(BF16), (((0,), (0,)), ((), ())),
                                           preferred_element_type=F32)
    dirs = ((rf_ref, vf_ref, kkf_ref, wf_ref, bf_ref, ktf_ref, yf_ref),
            (rb_ref, vb_ref, kkb_ref, wb_ref, bb_ref, ktb_ref, yb_ref))
    heads = [(d, h) for d in range(2) for h in range(GROUP_HEADS)]
    row0 = lambda cc, d: (cc if d == 0 else n_chunks - 1 - cc) * c
    rows = lambda cc, d: slice(row0(cc, d), row0(cc, d) + c)

    logw = {(cc, d): dirs[d][3][0, 0, rows(cc, d), :] for cc in range(n_chunks) for d in range(2)}
    cum = {key: _select_sum(jnp.where(incl[key[1]], 1.0, 0.0), lw) for key, lw in logw.items()}
    tot = {key: _select_sum(ones_cc, lw) for key, lw in logw.items()}
    st = {}
    for cc in range(n_chunks):
        for d, h in heads:
            cols = slice(h * HEAD_DIM, (h + 1) * HEAD_DIM)
            r_ref, v_ref, kk_ref, _, b_ref, kt_ref, _ = dirs[d]
            lc = cum[cc, d][:, cols]
            grow = jnp.exp(-lc)
            st[cc, d, h] = dict(
                v=v_ref[0, rows(cc, d), cols],
                kap=kk_ref[0, rows(cc, d), cols] * jnp.exp(lc - logw[cc, d][:, cols]),
                bh=b_ref[0, 0, rows(cc, d), cols] * grow,
                kh=kt_ref[0, 0, rows(cc, d), cols] * grow,
                rh=r_ref[0, rows(cc, d), cols] * jnp.exp(lc),
                scale=jnp.exp(_sum_select(jnp.where(eye, tot[cc, d][:, cols], 0.0), ones_cc)))
    for (cc, d, h), x in st.items():
        gram = bdot_nt(jnp.concatenate([x["kap"], x["rh"]], axis=0), jnp.concatenate([x["bh"], x["kh"]], axis=0))
        x["a_b"] = jnp.where(strict[d], gram[:c, :c], 0.0)
        x["a_k"] = jnp.where(strict[d], gram[:c, c:], 0.0)
        x["b_b"] = jnp.where(incl[d], gram[c:, :c], 0.0)
        x["b_k"] = jnp.where(incl[d], gram[c:, c:], 0.0)
        x["inv"] = jnp.where(eye, 1.0, 0.0) - jnp.where(levels[0], x["a_b"], 0.0)
    for lvl in levels[1:]:
        for x in st.values():
            x["t"] = bdot(jnp.where(lvl, x["a_b"], 0.0), x["inv"])
        for x in st.values():
            x["inv"] = x["inv"] - bdot(x["inv"], x["t"])
    for x in st.values():
        x["akv"] = bdot(x["a_k"], x["v"])
        x["y_const"] = bdot(x["b_k"], x["v"])
        x["kv"] = bdot_tn(x["kh"], x["v"])
    for x in st.values():
        both = bdot(x["inv"], jnp.concatenate([x["kap"], x["akv"]], axis=-1))
        x["p_state"], x["p_const"] = both[:, :HEAD_DIM], both[:, HEAD_DIM:]
    state = {key: st_ref[key[0], key[1]] for key in heads}
    for cc in range(n_chunks):
        cur = [(key, st[(cc,) + key]) for key in heads]
        for key, x in cur:
            x["p"] = bdot(x["p_state"], state[key]) + x["p_const"]
            x["y0"] = bdot(x["rh"], state[key]) + x["y_const"]
        for key, x in cur:
            x["y"] = x["y0"] - bdot(x["b_b"], x["p"])
            state[key] = (state[key] + x["kv"] - bdot_tn(x["bh"], x["p"])) * x["scale"]
        for d in range(2):
            dirs[d][6][0, rows(cc, d), :] = jnp.concatenate([x["y"] for (dd, h), x in cur if dd == d], axis=-1)
    for d, h in heads:
        st_ref[d, h] = state[d, h]


def _rwkv_post_rows(y, bonus, g, ln_g, ln_b):
    head_mean = _segment_ones(GW, HEAD_DIM, F32) * (1.0 / HEAD_DIM)
    yc = y - _dot_f32(y, head_mean)
    var = _dot_f32(yc * yc, head_mean)
    return (yc * lax.rsqrt(var + RWKV_GN_EPS) * ln_g + ln_b + bonus) * g


def _rwkv_post_kernel(yf_ref, yb_ref, bonus_ref, g_ref, lng_ref, lnb_ref, o_ref):
    o_ref[0] = _rwkv_post_rows(yf_ref[0] + yb_ref[0], bonus_ref[0], g_ref[0], lng_ref[...], lnb_ref[...])


def _block_diag2(m):
    z = jnp.zeros_like(m[0])
    return jnp.concatenate([jnp.concatenate([m[0], z], 1), jnp.concatenate([z, m[1]], 1)], 0)


def _rwkv7_pallas(f, mu, w0, w2, a0, a2, g2, k_k, k_a, r_k, ln_g, ln_b, *, ctx_len, apply_post=True):
    bsz, seq, cols = f.shape
    tr = ROW_BLOCK
    assert ctx_len % tr == 0 and seq % tr == 0
    nb, ctx_blocks = seq // tr, ctx_len // tr
    prev_spec, next_spec = _halo_specs(cols, tr)
    row2 = lambda a: a.reshape(1, -1).astype(F32)
    full = lambda a: pl.BlockSpec(a.shape, lambda b, i: (0,) * a.ndim)
    params = [mu, row2(w0), _block_diag2(w2), row2(a0), _block_diag2(a2), g2, row2(k_k), row2(k_a), row2(r_k)]
    act = jax.ShapeDtypeStruct((bsz, seq, GW), F32)
    act2 = jax.ShapeDtypeStruct((2, bsz, seq, GW), F32)
    blk = pl.BlockSpec((1, tr, GW), lambda b, i: (b, i, 0))
    blk2 = pl.BlockSpec((2, 1, tr, GW), lambda b, i: (0, b, i, 0))
    r, v, kkn, g, bonus, w, bb, kt = pl.pallas_call(
        functools.partial(_rwkv_prep_kernel, ctx_blocks=ctx_blocks),
        name="rwkv_prep",
        grid=(bsz, nb),
        in_specs=[pl.BlockSpec((1, tr, cols), lambda b, i: (b, i, 0)), prev_spec, next_spec(seq // 8)]
                 + [full(p) for p in params],
        out_specs=[blk] * 5 + [blk2] * 3,
        out_shape=[act] * 5 + [act2] * 3,
        compiler_params=pltpu.CompilerParams(dimension_semantics=("arbitrary", "arbitrary"),
                                             vmem_limit_bytes=V7X_VMEM_LIMIT_BYTES),
    )(f, f, f, *params)

    def bwd_block(c):
        return jnp.where(c < ctx_blocks, ctx_blocks - 1 - c, nb - 1 - (c - ctx_blocks))
    assert tr % RWKV_CHUNK == 0 and RWKV_CHUNK == HEAD_DIM
    fwd = pl.BlockSpec((1, tr, GW), lambda b, c: (b, c, 0))
    bwd = pl.BlockSpec((1, tr, GW), lambda b, c: (b, bwd_block(c), 0))
    fwd_d = pl.BlockSpec((1, 1, tr, GW), lambda b, c: (0, b, c, 0))
    bwd_d = pl.BlockSpec((1, 1, tr, GW), lambda b, c: (1, b, bwd_block(c), 0))
    yf, yb = pl.pallas_call(
        _rwkv_chunk_kernel,
        name="rwkv_chunks",
        grid=(bsz, nb),
        in_specs=[fwd, fwd, fwd, fwd_d, fwd_d, fwd_d, bwd, bwd, bwd, bwd_d, bwd_d, bwd_d],
        out_specs=[fwd, bwd],
        out_shape=[act, act],
        scratch_shapes=[pltpu.VMEM((2, GROUP_HEADS, HEAD_DIM, HEAD_DIM), F32)],
        compiler_params=pltpu.CompilerParams(dimension_semantics=("arbitrary", "arbitrary"),
                                             vmem_limit_bytes=V7X_VMEM_LIMIT_BYTES),
    )(r, v, kkn, w, bb, kt, r, v, kkn, w, bb, kt)
    if not apply_post:
        return yf, yb, bonus, g, row2(ln_g), row2(ln_b)

    return pl.pallas_call(
        _rwkv_post_kernel,
        name="rwkv_post",
        grid=(bsz, nb),
        in_specs=[blk, blk, blk, blk, full(row2(ln_g)), full(row2(ln_b))],
        out_specs=blk,
        out_shape=act,
        compiler_params=pltpu.CompilerParams(dimension_semantics=("arbitrary", "arbitrary")),
    )(yf, yb, bonus, g, row2(ln_g), row2(ln_b))


GDN_GATE_LANES = 128
GDN_PADDED_COLS = 4 * GW + GDN_GATE_LANES


def _gdn_prep_kernel(x_ref, xprev_ref, xnext_ref, ab_ref, conv_ref, alog_ref, dtb_ref,
                     q_out, k_out, v_out, gb_out, *, ctx_blocks):
    i = pl.program_id(1)
    keep_prev, keep_next = _segment_edge_flags(i, pl.num_programs(1), ctx_blocks)
    x = x_ref[0]
    xm, xp = _shifted_rows(x, xprev_ref[0, 7:8, :] * keep_prev, xnext_ref[0, 0:1, :] * keep_next)
    y = xm * conv_ref[0:1, :] + x * conv_ref[1:2, :] + xp * conv_ref[2:3, :]
    y = y * jax.nn.sigmoid(y)
    head_sum = _segment_ones(GW, HEAD_DIM, F32)
    q = y[:, 0:GW]
    k = y[:, GW:2 * GW]
    q_out[0] = q * lax.rsqrt(_dot_f32(q * q, head_sum) + 1e-6) * (HEAD_DIM ** -0.5)
    k_out[0] = k * lax.rsqrt(_dot_f32(k * k, head_sum) + 1e-6)
    v_out[0] = y[:, 2 * GW:3 * GW]
    ab = ab_ref[0]
    lane = lax.broadcasted_iota(jnp.int32, ab.shape, 1)
    log_alpha = -jnp.exp(alog_ref[...]) * _softplus(ab + dtb_ref[...])
    gb_out[0] = jnp.where(lane < 2 * GROUP_HEADS, log_alpha, jax.nn.sigmoid(ab))


def _gdn_chunk_kernel(qf_ref, kf_ref, vf_ref, gf_ref, qb_ref, kb_ref, vb_ref, gb_ref, of_ref, ob_ref, st_ref):
    i = pl.program_id(1)
    c = GDN_CHUNK
    n_chunks = qf_ref.shape[1] // c

    @pl.when(i == 0)
    def _():
        st_ref[...] = jnp.zeros_like(st_ref)

    r = lax.broadcasted_iota(jnp.int32, (c, c), 0)
    s = lax.broadcasted_iota(jnp.int32, (c, c), 1)
    eye = r == s
    ones_cc = jnp.ones((c, c), F32)
    incl = (r >= s, r <= s)
    strict = (r > s, r < s)
    levels = []
    b = 1
    while b < c:
        levels.append(jnp.logical_and(r // (2 * b) == s // (2 * b), r // b != s // b))
        b *= 2
    dirs = ((qf_ref, kf_ref, vf_ref, gf_ref, of_ref), (qb_ref, kb_ref, vb_ref, gb_ref, ob_ref))

    bdot = lambda x, y: jnp.dot(x.astype(BF16), y.astype(BF16), preferred_element_type=F32)
    bdot_nt = lambda x, y: lax.dot_general(x.astype(BF16), y.astype(BF16), _NT, preferred_element_type=F32)
    bdot_tn = lambda x, y: lax.dot_general(x.astype(BF16), y.astype(BF16), (((0,), (0,)), ((), ())),
                                           preferred_element_type=F32)

    heads = [(d, h) for d in range(2) for h in range(GROUP_HEADS)]
    row0 = lambda cc, d: (cc if d == 0 else n_chunks - 1 - cc) * c
    gates = {(cc, d): dirs[d][3][0, row0(cc, d):row0(cc, d) + c, :] for cc in range(n_chunks) for d in range(2)}
    cum = {key: _select_sum(jnp.where(incl[key[1]], 1.0, 0.0), g) for key, g in gates.items()}
    tot = {key: _select_sum(ones_cc, g) for key, g in gates.items()}
    lane_src = lax.broadcasted_iota(jnp.int32, (GDN_GATE_LANES, GW), 0)
    lane_head = lax.broadcasted_iota(jnp.int32, (GDN_GATE_LANES, GW), 1) // HEAD_DIM
    spread = [jnp.where(lane_src == d * GROUP_HEADS + lane_head, 1.0, 0.0) for d in range(2)]
    spread_beta = [jnp.where(lane_src == (2 + d) * GROUP_HEADS + lane_head, 1.0, 0.0) for d in range(2)]
    cum_t = {key: _sum_select(x, spread[key[1]]) for key, x in cum.items()}
    tot_t = {key: _sum_select(x, spread[key[1]]) for key, x in tot.items()}
    beta_t = {key: _sum_select(x, spread_beta[key[1]]) for key, x in gates.items()}
    chains = [(cc, d, h) for cc in range(n_chunks) for d, h in heads]
    st = {}
    for cc, d, h in chains:
        cols = slice(h * HEAD_DIM, (h + 1) * HEAD_DIM)
        q, k, v = (dirs[d][n][0, row0(cc, d):row0(cc, d) + c, cols] for n in range(3))
        gc = cum_t[cc, d][:, cols]
        st[cc, d, h] = dict(q=q, k=k, v=v, gc=gc, gt=tot_t[cc, d][:, cols], beta=beta_t[cc, d][:, cols],
                            gc_row=_select_sum(ones_cc, jnp.where(eye, gc, 0.0)))
    for (cc, d, h), x in st.items():
        x["decay"] = jnp.exp(jnp.where(incl[d], x["gc"] - x["gc_row"], NEG_BIG))
        x["kb"] = x["k"] * x["beta"]
        gram = bdot_nt(jnp.concatenate([x["kb"], x["q"]], axis=0), x["k"])
        x["a"] = jnp.where(strict[d], gram[:c] * x["decay"], 0.0)
        x["qk"] = jnp.where(incl[d], gram[c:] * x["decay"], 0.0)
        x["inv"] = jnp.where(eye, 1.0, 0.0) - jnp.where(levels[0], x["a"], 0.0)
    for lvl in levels[1:]:
        for x in st.values():
            x["t"] = bdot(jnp.where(lvl, x["a"], 0.0), x["inv"])
        for x in st.values():
            x["inv"] = x["inv"] - bdot(x["inv"], x["t"])
    for x in st.values():
        x["eg"] = jnp.exp(x["gc"])
        x["sol"] = bdot(x["inv"], jnp.concatenate([x["v"] * x["beta"], x["kb"] * x["eg"]], axis=-1))
        x["qg"] = x["q"] * x["eg"]
        x["kg"] = x["k"] * jnp.exp(x["gt"] - x["gc"])
    state = {(d, h): st_ref[d, h] for d, h in heads}
    for cc in range(n_chunks):
        cur = [(key, st[(cc,) + key]) for key in heads]
        for key, x in cur:
            x["ws"] = bdot(x["sol"][:, HEAD_DIM:], state[key])
            x["qs"] = bdot(x["qg"], state[key])
        for key, x in cur:
            x["v_new"] = x["sol"][:, :HEAD_DIM] - x["ws"]
            x["o"] = x["qs"] + bdot(x["qk"], x["v_new"])
            x["upd"] = bdot_tn(x["kg"], x["v_new"])
        for key, x in cur:
            state[key] = state[key] * jnp.exp(x["gt"][0:1, :]) + x["upd"]
        for d in range(2):
            dirs[d][4][0, row0(cc, d):row0(cc, d) + c, :] = jnp.concatenate(
                [x["o"] for (dd, h), x in cur if dd == d], axis=-1)
    for d, h in heads:
        st_ref[d, h] = state[d, h]


def _gdn_post_rows(o, gate, g):
    head_mean = _segment_ones(GW, HEAD_DIM, F32) * (1.0 / HEAD_DIM)
    return o * lax.rsqrt(_dot_f32(o * o, head_mean) + 1e-6) * g * (gate * jax.nn.sigmoid(gate))


def _gdn_post_kernel(of_ref, ob_ref, gate_ref, g_ref, o_ref):
    o_ref[0] = _gdn_post_rows(of_ref[0] + ob_ref[0], gate_ref[0], g_ref[...])


def _gated_deltanet_pallas(f, conv_w, a_log, dt_bias, norm_g, *, ctx_len, apply_post=True):
    bsz, seq, width = f.shape
    tr = ROW_BLOCK
    assert ctx_len % tr == 0 and seq % tr == 0 and tr % GDN_CHUNK == 0
    if width == GDN_COLS:
        f = jnp.pad(f, ((0, 0), (0, 0), (0, GDN_PADDED_COLS - GDN_COLS)))
    assert f.shape[2] == GDN_PADDED_COLS
    nb, ctx_blocks = seq // tr, ctx_len // tr
    lane_pad = lambda a: jnp.pad(a.reshape(1, -1).astype(F32), ((0, 0), (0, GDN_GATE_LANES - a.size)))
    prev_spec, next_spec = _halo_specs(3 * GW, tr)
    full = lambda a: pl.BlockSpec(a.shape, lambda b, i: (0,) * a.ndim)
    act = jax.ShapeDtypeStruct((bsz, seq, GW), F32)
    gact = jax.ShapeDtypeStruct((bsz, seq, GDN_GATE_LANES), F32)
    blk = pl.BlockSpec((1, tr, GW), lambda b, i: (b, i, 0))
    gblk = pl.BlockSpec((1, tr, GDN_GATE_LANES), lambda b, i: (b, i, 0))
    gate_view = pl.BlockSpec((1, tr, GW), lambda b, i: (b, i, 3))
    ab_view = pl.BlockSpec((1, tr, GDN_GATE_LANES), lambda b, i: (b, i, 4 * GW // GDN_GATE_LANES))
    params = [conv_w.astype(F32), lane_pad(a_log), lane_pad(dt_bias)]
    q, k, v, gb = pl.pallas_call(
        functools.partial(_gdn_prep_kernel, ctx_blocks=ctx_blocks),
        name="gdn_prep",
        grid=(bsz, nb),
        in_specs=[pl.BlockSpec((1, tr, 3 * GW), lambda b, i: (b, i, 0)), prev_spec, next_spec(seq // 8), ab_view]
                 + [full(p) for p in params],
        out_specs=[blk, blk, blk, gblk],
        out_shape=[act, act, act, gact],
        compiler_params=pltpu.CompilerParams(dimension_semantics=("arbitrary", "arbitrary"),
                                             vmem_limit_bytes=V7X_VMEM_LIMIT_BYTES),
    )(f, f, f, f, *params)

    def bwd_block(i):
        return jnp.where(i < ctx_blocks, ctx_blocks - 1 - i, nb - 1 - (i - ctx_blocks))
    bblk = pl.BlockSpec((1, tr, GW), lambda b, i: (b, bwd_block(i), 0))
    bgblk = pl.BlockSpec((1, tr, GDN_GATE_LANES), lambda b, i: (b, bwd_block(i), 0))
    of, ob = pl.pallas_call(
        _gdn_chunk_kernel,
        name="gdn_chunks",
        grid=(bsz, nb),
        in_specs=[blk, blk, blk, gblk, bblk, bblk, bblk, bgblk],
        out_specs=[blk, bblk],
        out_shape=[act, act],
        scratch_shapes=[pltpu.VMEM((2, GROUP_HEADS, HEAD_DIM, HEAD_DIM), F32)],
        compiler_params=pltpu.CompilerParams(dimension_semantics=("arbitrary", "arbitrary"),
                                             vmem_limit_bytes=V7X_VMEM_LIMIT_BYTES),
    )(q, k, v, gb, q, k, v, gb)

    g_row = jnp.tile(norm_g.reshape(1, HEAD_DIM).astype(F32), (1, GROUP_HEADS))
    if not apply_post:
        return of, ob, f, g_row
    return pl.pallas_call(
        _gdn_post_kernel,
        name="gdn_post",
        grid=(bsz, nb),
        in_specs=[blk, blk, gate_view, full(g_row)],
        out_specs=blk,
        out_shape=act,
        compiler_params=pltpu.CompilerParams(dimension_semantics=("arbitrary", "arbitrary")),
    )(of, ob, f, g_row)


ROPE_PAIR = DIFF_HALF // 4


def _rope_tables(seq, ctx_len, q_scale):
    n = jnp.arange(seq - ctx_len, dtype=jnp.int32)
    row, col = n // GRID_W, n % GRID_W
    i = jnp.arange(HEAD_DIM)
    grp = (i % DIFF_HALF) // (2 * ROPE_PAIR)
    inv = ROPE_BASE ** (-(i % ROPE_PAIR).astype(F32) / ROPE_PAIR)
    pos = jnp.where(grp[None, :] == 0, row[:, None], col[:, None]).astype(F32)
    ang = pos * inv[None, :]
    sign = jnp.where((i % (2 * ROPE_PAIR)) < ROPE_PAIR, -1.0, 1.0)
    cos = jnp.concatenate([jnp.ones((ctx_len, HEAD_DIM), F32), jnp.cos(ang)], 0)
    sin = jnp.concatenate([jnp.zeros((ctx_len, HEAD_DIM), F32), jnp.sin(ang) * sign], 0)
    cos = jnp.tile(cos, (1, GROUP_HEADS))
    sin = jnp.tile(sin, (1, GROUP_HEADS))
    return jnp.concatenate([cos * q_scale, cos], 1), jnp.concatenate([sin * q_scale, sin], 1)


def _rotate_qk(qk, cos, sin):
    lane = lax.broadcasted_iota(jnp.int32, qk.shape, 1)
    partner = jnp.where(lane % (2 * ROPE_PAIR) < ROPE_PAIR,
                        pltpu.roll(qk, qk.shape[1] - ROPE_PAIR, axis=1), pltpu.roll(qk, ROPE_PAIR, axis=1))
    return qk * cos + partner * sin


def _store_qkv(q, k, v, q_out, k_out, v_out):
    q_out[...] = q.astype(BF16)
    for h in range(GROUP_HEADS):
        k_out[h] = k[:, h * HEAD_DIM:(h + 1) * HEAD_DIM].astype(BF16)
        v_out[h] = v[:, h * HEAD_DIM:(h + 1) * HEAD_DIM].astype(BF16)


_NT = (((1,), (1,)), ((), ()))


def _softmax_pv(s, v):
    m = jnp.max(s, axis=-1, keepdims=True)
    e = jnp.exp(s - m)
    return jnp.dot(e.astype(BF16), v, preferred_element_type=F32) / jnp.sum(e, axis=-1, keepdims=True)


def _diff_attn_kernel(q_ref, k_ref, v_ref, lam_ref, g_ref, o_ref, *, ctx_blocks, ctx_len):
    i = pl.program_id(1)
    lv = lam_ref[...]
    lam_init = lv[4:5, 0:1]
    lam = (jnp.exp(jnp.sum(lv[0:1] * lv[1:2], axis=-1, keepdims=True))
           - jnp.exp(jnp.sum(lv[2:3] * lv[3:4], axis=-1, keepdims=True)) + lam_init)
    lane = lax.broadcasted_iota(jnp.int32, (q_ref.shape[1], HEAD_DIM), 1)

    def attend(n_keys):
        outs = []
        for h in range(GROUP_HEADS):
            qh = q_ref[0, :, h * HEAD_DIM:(h + 1) * HEAD_DIM]
            kh = k_ref[h, 0, 0:n_keys, :]
            vh = v_ref[h, 0, 0:n_keys, :]
            zero = jnp.zeros_like(qh)
            s1 = lax.dot_general(jnp.where(lane < DIFF_HALF, qh, zero), kh, _NT, preferred_element_type=F32)
            s2 = lax.dot_general(jnp.where(lane >= DIFF_HALF, qh, zero), kh, _NT, preferred_element_type=F32)
            o = _softmax_pv(s1, vh) - lam * _softmax_pv(s2, vh)
            o = o * lax.rsqrt(jnp.mean(o * o, axis=-1, keepdims=True) + 1e-6) * g_ref[...] * (1.0 - lam_init)
            outs.append(o)
        o_ref[0] = jnp.concatenate(outs, axis=-1)

    @pl.when(i < ctx_blocks)
    def _():
        attend(ctx_len)

    @pl.when(i >= ctx_blocks)
    def _():
        attend(k_ref.shape[2])


def _diff_attention_pallas(q, k, v, lam_vecs, norm_g, *, ctx_len, lam_init):
    bsz, seq, _ = q.shape
    tq = ROW_BLOCK
    kv_spec = pl.BlockSpec((GROUP_HEADS, 1, seq, HEAD_DIM), lambda b, i: (0, b, 0, 0))
    lam_rows = jnp.concatenate([lam_vecs.astype(F32), jnp.full((1, lam_vecs.shape[1]), lam_init, F32)], 0)
    return pl.pallas_call(
        functools.partial(_diff_attn_kernel, ctx_blocks=ctx_len // tq, ctx_len=ctx_len),
        name="diff_attn",
        grid=(bsz, seq // tq),
        in_specs=[pl.BlockSpec((1, tq, GW), lambda b, i: (b, i, 0)), kv_spec, kv_spec,
                  pl.BlockSpec(lam_rows.shape, lambda b, i: (0, 0)),
                  pl.BlockSpec((1, HEAD_DIM), lambda b, i: (0, 0))],
        out_specs=pl.BlockSpec((1, tq, GW), lambda b, i: (b, i, 0)),
        out_shape=jax.ShapeDtypeStruct((bsz, seq, GW), F32),
        compiler_params=pltpu.CompilerParams(dimension_semantics=("arbitrary", "arbitrary"),
                                             vmem_limit_bytes=V7X_VMEM_LIMIT_BYTES),
    )(q, k, v, lam_rows, norm_g.reshape(1, HEAD_DIM).astype(F32))


NAT_TILE_ROWS = ROW_BLOCK // GRID_W
NAT_SLAB_ROWS = NAT_TILE_ROWS + WIN_H - 1


def _nat_slab_start(tile, n_rows):
    return np.clip(tile * NAT_TILE_ROWS - WIN_H // 2, 0, n_rows - NAT_SLAB_ROWS)


def _nat_bias_tables(rpb, n_rows):
    n_tiles = n_rows // NAT_TILE_ROWS
    nq, nk, w = NAT_TILE_ROWS, NAT_SLAB_ROWS, GRID_W
    cq, ck = np.arange(w)[:, None], np.arange(w)[None, :]
    d_col = np.clip(ck - cq, -(WIN_W - 1), WIN_W - 1) + WIN_W - 1
    col_1h = (d_col.reshape(-1)[:, None] == np.arange(2 * WIN_W - 1)[None, :]).astype(np.float32)
    c0 = np.clip(cq - WIN_W // 2, 0, w - WIN_W)
    col_ok = (ck >= c0) & (ck < c0 + WIN_W)
    tabs = []
    for tile in (0, 1, n_tiles - 1):
        r = tile * nq + np.arange(nq)[:, None]
        kr = _nat_slab_start(tile, n_rows) + np.arange(nk)[None, :]
        rs = np.clip(r - WIN_H // 2, 0, n_rows - WIN_H)
        row_ok = (kr >= rs) & (kr < rs + WIN_H)
        d_row = np.clip(kr - r + WIN_H - 1, 0, 2 * WIN_H - 2)
        row_1h = (d_row.reshape(-1)[:, None] == np.arange(2 * WIN_H - 1)[None, :]).astype(np.float32)
        t = jnp.einsum('pa,hab,cb->hpc', row_1h, rpb.astype(F32), col_1h, precision=HIGHEST)
        t = t.reshape(GROUP_HEADS, nq, nk, w, w).transpose(0, 1, 3, 2, 4)
        ok = row_ok[:, None, :, None] & col_ok[None, :, None, :]
        tabs.append(jnp.where(ok[None], t, NEG_BIG).reshape(GROUP_HEADS, nq * w, nk * w))
    return jnp.stack(tabs)


def _nat_attn_kernel(q_ref, k_ref, v_ref, bias_ref, o_ref, *, ctx_blocks, ctx_len, n_rows):
    i = pl.program_id(1)
    n_slab = NAT_SLAB_ROWS * GRID_W

    def heads(fn):
        o_ref[0] = jnp.concatenate(
            [fn(h, q_ref[0, :, h * HEAD_DIM:(h + 1) * HEAD_DIM]) for h in range(GROUP_HEADS)], axis=-1)

    @pl.when(i < ctx_blocks)
    def _():
        def ctx_only(h, qh):
            s = lax.dot_general(qh, k_ref[h, 0, 0:ctx_len, :], _NT, preferred_element_type=F32)
            return _softmax_pv(s, v_ref[h, 0, 0:ctx_len, :])
        heads(ctx_only)

    @pl.when(i >= ctx_blocks)
    def _():
        tile = i - ctx_blocks
        start = jnp.clip(tile * NAT_TILE_ROWS - WIN_H // 2, 0, n_rows - NAT_SLAB_ROWS)
        off = pl.multiple_of(ctx_len + start * GRID_W, GRID_W)

        def windowed(h, qh):
            s_w = lax.dot_general(qh, k_ref[h, 0, pl.ds(off, n_slab), :], _NT,
                                  preferred_element_type=F32) + bias_ref[0, h]
            s_c = lax.dot_general(qh, k_ref[h, 0, 0:ctx_len, :], _NT, preferred_element_type=F32)
            m = jnp.maximum(jnp.max(s_w, axis=-1, keepdims=True), jnp.max(s_c, axis=-1, keepdims=True))
            e_w = jnp.exp(s_w - m)
            e_c = jnp.exp(s_c - m)
            den = jnp.sum(e_w, axis=-1, keepdims=True) + jnp.sum(e_c, axis=-1, keepdims=True)
            num = (jnp.dot(e_w.astype(BF16), v_ref[h, 0, pl.ds(off, n_slab), :], preferred_element_type=F32)
                   + jnp.dot(e_c.astype(BF16), v_ref[h, 0, 0:ctx_len, :], preferred_element_type=F32))
            return num / den
        heads(windowed)


def _nat_attention_pallas(q, k, v, rpb, *, ctx_len):
    bsz, seq, _ = q.shape
    tq = ROW_BLOCK
    ctx_blocks = ctx_len // tq
    n_rows = (seq - ctx_len) // GRID_W
    n_tiles = n_rows // NAT_TILE_ROWS
    assert n_rows >= NAT_SLAB_ROWS and n_tiles >= 3
    bias = _nat_bias_tables(rpb, n_rows)

    def variant(i):
        tile = i - ctx_blocks
        return jnp.where(tile <= 0, 0, jnp.where(tile >= n_tiles - 1, 2, 1))
    kv_spec = pl.BlockSpec((GROUP_HEADS, 1, seq, HEAD_DIM), lambda b, i: (0, b, 0, 0))
    return pl.pallas_call(
        functools.partial(_nat_attn_kernel, ctx_blocks=ctx_blocks, ctx_len=ctx_len, n_rows=n_rows),
        name="nat_attn",
        grid=(bsz, seq // tq),
        in_specs=[pl.BlockSpec((1, tq, GW), lambda b, i: (b, i, 0)), kv_spec, kv_spec,
                  pl.BlockSpec((1,) + bias.shape[1:], lambda b, i: (variant(i), 0, 0, 0))],
        out_specs=pl.BlockSpec((1, tq, GW), lambda b, i: (b, i, 0)),
        out_shape=jax.ShapeDtypeStruct((bsz, seq, GW), F32),
        compiler_params=pltpu.CompilerParams(dimension_semantics=("arbitrary", "arbitrary"),
                                             vmem_limit_bytes=V7X_VMEM_LIMIT_BYTES),
    )(q, k, v, bias)


N_MOD = 6
MATMUL_ROWS = 512


def _ada_kernel(c_ref, w_ref, b_ref, o_ref):
    c = c_ref[...]
    o_ref[...] = _dot_f32(c * jax.nn.sigmoid(c), w_ref[...]) + b_ref[...]


def _ada_modulation(c, c_ctx, w_ada, b_ada):
    bsz, d = c.shape
    rows = 8 * ((bsz + 1 + 7) // 8)
    cc = jnp.zeros((rows, d), F32).at[:bsz].set(c).at[bsz].set(c_ctx)
    tn = d
    m = pl.pallas_call(
        _ada_kernel,
        name="ada_modulation",
        grid=(w_ada.shape[1] // tn,),
        in_specs=[pl.BlockSpec((rows, d), lambda j: (0, 0)),
                  pl.BlockSpec((d, tn), lambda j: (0, j)),
                  pl.BlockSpec((1, tn), lambda j: (0, j))],
        out_specs=pl.BlockSpec((rows, tn), lambda j: (0, j)),
        out_shape=jax.ShapeDtypeStruct((rows, w_ada.shape[1]), F32),
        compiler_params=pltpu.CompilerParams(dimension_semantics=("arbitrary",)),
    )(cc, w_ada, b_ada.reshape(1, -1))
    lat = m[:bsz].reshape(bsz, 1, N_MOD, d)
    ctx = jnp.broadcast_to(m[bsz].reshape(1, 1, N_MOD, d), (bsz, 1, N_MOD, d))
    return jnp.concatenate([ctx, lat], axis=1)


def _ln(x):
    mu = jnp.mean(x, axis=-1, keepdims=True)
    xc = x - mu
    return xc * lax.rsqrt(jnp.mean(xc * xc, axis=-1, keepdims=True) + LN_EPS)


def _mod_spec(d, ctx_blocks):
    return pl.BlockSpec((1, 1, N_MOD, d), lambda b, i: (b, jnp.where(i < ctx_blocks, 0, 1), 0, 0))


def _in_proj_kernel(h_ref, mod0_ref, mod1_ref, wa_ref, wb_ref, wc_ref, wd_ref, cos_ref, sin_ref,
                    qa_out, ka_out, va_out, pb_out, pc_out, qd_out, kd_out, vd_out, *, seq, ctx_len):
    h = h_ref[...]
    shift, scale = _flat_row_mod(mod0_ref, mod1_ref, pl.program_id(0), h.shape, seq, ctx_len, (0, 1))
    x = (_ln(h) * (1.0 + scale) + shift).astype(BF16)
    pa = jnp.dot(x, wa_ref[...], preferred_element_type=F32)
    qk = _rotate_qk(pa[:, :2 * GW], cos_ref[...], sin_ref[...])
    _store_qkv(qk[:, :GW], qk[:, GW:], pa[:, 2 * GW:], qa_out, ka_out, va_out)
    pb_out[...] = jnp.dot(x, wb_ref[...], preferred_element_type=F32)
    pc_out[...] = jnp.dot(x, wc_ref[...], preferred_element_type=F32)
    pd = jnp.dot(x, wd_ref[...], preferred_element_type=F32)
    _store_qkv(pd[:, :GW] * (HEAD_DIM ** -0.5), pd[:, GW:2 * GW], pd[:, 2 * GW:], qd_out, kd_out, vd_out)


def _in_proj_pallas(x, mod, ws, cos, sin, *, seq, ctx_len):
    m, k = x.shape
    tm = MATMUL_ROWS
    row = lambda n: pl.BlockSpec((tm, n), lambda i: (i, 0))
    hm = pl.BlockSpec((GROUP_HEADS, tm, HEAD_DIM), lambda i: (0, i, 0))
    q_shape = jax.ShapeDtypeStruct((m, GW), BF16)
    hm_shape = jax.ShapeDtypeStruct((GROUP_HEADS, m, HEAD_DIM), BF16)
    f32 = lambda n: jax.ShapeDtypeStruct((m, n), F32)
    return pl.pallas_call(
        functools.partial(_in_proj_kernel, seq=seq, ctx_len=ctx_len),
        name="in_proj",
        grid=(m // tm,),
        in_specs=[row(k)] + _flat_mod_specs(tm, seq, k, 1) + [pl.BlockSpec(w.shape, lambda i: (0, 0)) for w in ws]
                 + [row(2 * GW), row(2 * GW)],
        out_specs=[row(GW), hm, hm, row(ws[1].shape[1]), row(ws[2].shape[1]), row(GW), hm, hm],
        out_shape=[q_shape, hm_shape, hm_shape, f32(ws[1].shape[1]), f32(ws[2].shape[1]), q_shape, hm_shape, hm_shape],
        compiler_params=pltpu.CompilerParams(dimension_semantics=("arbitrary",),
                                             vmem_limit_bytes=V7X_VMEM_LIMIT_BYTES),
    )(x, mod, mod, *ws, cos, sin)


def _post_norm_rows(h, gate, y, g, b):
    return _ln(DN_ALPHA * h + gate * y) * g + b


def _out_proj_kernel(ya_ref, rf_ref, rb_ref, bonus_ref, rg_ref, lng_ref, lnb_ref, of_ref, ob_ref, gate_ref, gn_ref,
                     yd_ref, w_ref, h_ref, mod_ref, g_ref, b_ref, o_ref):
    yb = _rwkv_post_rows(rf_ref[0] + rb_ref[0], bonus_ref[0], rg_ref[0], lng_ref[...], lnb_ref[...])
    yc = _gdn_post_rows(of_ref[0] + ob_ref[0], gate_ref[0], gn_ref[...])
    mix = None
    for n, y in enumerate((ya_ref[0], yb, yc, yd_ref[0])):
        part = jnp.dot(y.astype(BF16), w_ref[n * GW:(n + 1) * GW, :], preferred_element_type=F32)
        mix = part if mix is None else mix + part
    o_ref[0] = _post_norm_rows(h_ref[0], mod_ref[0, 0, 2:3, :], mix, g_ref[...], b_ref[...])


def _out_proj_post_norm(ya, rwkv_parts, gdn_parts, yd, w_out, hs, mod, g, b, *, ctx_len):
    bsz, seq, d = hs.shape
    tr = ROW_BLOCK
    yblk = pl.BlockSpec((1, tr, GW), lambda bb, i: (bb, i, 0))
    blk = pl.BlockSpec((1, tr, d), lambda bb, i: (bb, i, 0))
    row = pl.BlockSpec((1, d), lambda bb, i: (0, 0))
    grow = pl.BlockSpec((1, GW), lambda bb, i: (0, 0))
    gate_view = pl.BlockSpec((1, tr, GW), lambda bb, i: (bb, i, 3))
    return pl.pallas_call(
        _out_proj_kernel,
        name="out_proj_post_norm",
        grid=(bsz, seq // tr),
        in_specs=[yblk] + [yblk] * 4 + [grow, grow] + [yblk, yblk, gate_view, grow] + [yblk]
                 + [pl.BlockSpec(w_out.shape, lambda bb, i: (0, 0)), blk, _mod_spec(d, ctx_len // tr), row, row],
        out_specs=blk,
        out_shape=jax.ShapeDtypeStruct((bsz, seq, d), F32),
        compiler_params=pltpu.CompilerParams(dimension_semantics=("arbitrary", "arbitrary")),
    )(ya, *rwkv_parts, *gdn_parts, yd, w_out, hs, mod, g.reshape(1, d), b.reshape(1, d))


def kernel(x, c, ctx, c_ctx, w_ada, b_ada, w_in, w_out, ln_mix_g, ln_mix_b, ln_ffn_g, ln_ffn_b, diff_lam, diff_norm_g, rwkv_mu, rwkv_w0, rwkv_w2, rwkv_a0, rwkv_a2, rwkv_g2, rwkv_kk, rwkv_ka, rwkv_rk, rwkv_ln_g, rwkv_ln_b, gdn_conv, gdn_a_log, gdn_dt_bias, gdn_norm_g, nat_rpb, peer_wq, peer_keys, peer_u, peer_v):
    dtype = x.dtype
    bsz, ctx_len = ctx.shape[0], ctx.shape[1]
    hs = jnp.concatenate([ctx, x], axis=1)
    seq = hs.shape[1]
    col_sizes = [ATTN_COLS, RWKV_COLS, GDN_COLS, ATTN_COLS]
    cos_a, sin_a = (jnp.tile(t, (bsz, 1)) for t in _rope_tables(seq, ctx_len, DIFF_HALF ** -0.5))
    col_offs = np.cumsum([0] + col_sizes)
    d_model = hs.shape[2]
    for l in range(DEPTH):
        lam_init = 0.8 - 0.6 * math.exp(-0.3 * l)
        mod = _ada_modulation(c, c_ctx, w_ada[l], b_ada[l])
        w_in_b = w_in[l].astype(BF16)
        w_groups = [w_in_b[:, col_offs[n]:col_offs[n + 1]] for n in range(4)]
        w_groups[2] = jnp.pad(w_groups[2], ((0, 0), (0, GDN_PADDED_COLS - GDN_COLS)))
        qa, ka, va, pb, pc, qd, kd, vd = _in_proj_pallas(hs.reshape(bsz * seq, d_model), mod, w_groups, cos_a, sin_a,
                                                         seq=seq, ctx_len=ctx_len)
        rows3 = lambda p: p.reshape(bsz, seq, p.shape[1])
        heads4 = lambda p: p.reshape(GROUP_HEADS, bsz, seq, HEAD_DIM)
        qa, pb, pc, qd = rows3(qa), rows3(pb), rows3(pc), rows3(qd)
        ka, va, kd, vd = heads4(ka), heads4(va), heads4(kd), heads4(vd)
        ya = _diff_attention_pallas(qa, ka, va, diff_lam[l], diff_norm_g[l], ctx_len=ctx_len, lam_init=lam_init)
        yb_parts = _rwkv7_pallas(pb, rwkv_mu[l], rwkv_w0[l], rwkv_w2[l], rwkv_a0[l], rwkv_a2[l], rwkv_g2[l],
                                 rwkv_kk[l], rwkv_ka[l], rwkv_rk[l], rwkv_ln_g[l], rwkv_ln_b[l], ctx_len=ctx_len,
                                 apply_post=False)
        yc_parts = _gated_deltanet_pallas(pc, gdn_conv[l], gdn_a_log[l], gdn_dt_bias[l], gdn_norm_g[l],
                                          ctx_len=ctx_len, apply_post=False)
        yd = _nat_attention_pallas(qd, kd, vd, nat_rpb[l], ctx_len=ctx_len)
        hs = _out_proj_post_norm(ya, yb_parts, yc_parts, yd, w_out[l].astype(BF16), hs, mod, ln_mix_g[l],
                                 ln_mix_b[l], ctx_len=ctx_len)
        wq_b = peer_wq[l].astype(BF16)
        keys_b = peer_keys[l].reshape(2 * PEER_HEADS, PEER_KEYS, PEER_HALF).astype(BF16)
        u_b = peer_u[l].astype(BF16)
        vt_b = peer_v[l].astype(BF16).T
        ffn_ctx = ctx_len
        if l == DEPTH - 1:
            hs, ffn_ctx = hs[:, ctx_len:], 0
        hs = _peer_ffn(hs, mod, ln_ffn_g[l], ln_ffn_b[l], wq_b, keys_b, u_b, vt_b, ctx_len=ffn_ctx)
    return hs.astype(dtype)
```

```python
import functools
import math

import jax
import jax.numpy as jnp
import numpy as np
from jax import lax
from jax.experimental import pallas as pl
from jax.experimental.pallas import tpu as pltpu

D_MODEL = 1024
DEPTH = 2
GRID_W = 64
HEAD_DIM = 64
N_GROUPS = 4
GROUP_HEADS = D_MODEL // (N_GROUPS * HEAD_DIM)
GW = GROUP_HEADS * HEAD_DIM
DIFF_HALF = HEAD_DIM // 2
ROPE_BASE = 10000.0
DECAY_LORA = 64
ICLR_LORA = 64
GATE_LORA = 128
RWKV_GN_EPS = 64e-5
RWKV_COLS = 3 * GW + 2 * DECAY_LORA + 2 * ICLR_LORA + GATE_LORA
GDN_CHUNK = 64
GDN_COLS = 4 * GW + 4 * GROUP_HEADS
WIN_H = 8
WIN_W = 16
ATTN_COLS = 3 * GW
PEER_HEADS = 8
PEER_KEYS = 128
PEER_EXPERTS = PEER_KEYS * PEER_KEYS
PEER_QDIM = 256
PEER_HALF = PEER_QDIM // 2
PEER_TOPK = 16
DN_ALPHA = (2 * DEPTH) ** 0.25
LN_EPS = 1e-5

F32 = jnp.float32
BF16 = jnp.bfloat16

V7X_VMEM_LIMIT_BYTES = 56 * 1024 * 1024
NEG_BIG = -3.0e38


PEER_STAT_ROWS = 4 * PEER_HEADS
LOG2E = 1.4426950408889634
PEER_POW_HEADS = 4
assert PEER_TOPK == 16


def _flat_mod_specs(tm, seq, d, grid_rank):
    first = lambda i, *_: ((i * tm) // seq, 0, 0, 0)
    last = lambda i, *_: ((i * tm + tm - 1) // seq, 0, 0, 0)
    assert grid_rank in (1, 2) and tm <= seq
    return [pl.BlockSpec((1, 2, N_MOD, d), first), pl.BlockSpec((1, 2, N_MOD, d), last)]


def _flat_row_mod(mod0_ref, mod1_ref, i, shape, seq, ctx_len, rows):
    tm = shape[0]
    r0 = i * tm
    b0 = r0 // seq
    boundary = (b0 + 1) * seq
    idx = r0 + lax.broadcasted_iota(jnp.int32, shape, 0)
    second = idx >= boundary
    is_ctx = (idx - jnp.where(second, boundary, b0 * seq)) < ctx_len
    out = []
    for k in rows:
        v = jnp.where(second, mod1_ref[0, 1, k:k + 1, :], mod0_ref[0, 1, k:k + 1, :])
        out.append(jnp.where(is_ctx, mod0_ref[0, 0, k:k + 1, :], v) if ctx_len else v)
    return out


def _topk_rows(x, k):
    q = x.shape[0] // 4
    lv = [x[i * q:(i + 1) * q] for i in range(4)]
    for a, b in ((0, 1), (2, 3), (0, 2), (1, 3), (1, 2)):
        lv[a], lv[b] = jnp.maximum(lv[a], lv[b]), jnp.minimum(lv[a], lv[b])
    rows = []
    for i in range(k):
        m = jnp.max(lv[0], axis=0, keepdims=True)
        rows.append(m)
        if i + 1 < k:
            hit = lv[0] == m
            lv = [jnp.where(hit, lv[n + 1], lv[n]) for n in range(3)] + [jnp.where(hit, NEG_BIG, lv[3])]
    return rows


def _peer_score_kernel(h_ref, mod0_ref, mod1_ref, wq_ref, keys_ref, x_out, s_ref, st_ref, *, seq, ctx_len):
    h = h_ref[...]
    shift, scale = _flat_row_mod(mod0_ref, mod1_ref, pl.program_id(0), h.shape, seq, ctx_len, (3, 4))
    x = (_ln(h) * (1.0 + scale) + shift).astype(BF16)
    x_out[...] = x
    q = jnp.dot(x, wq_ref[...], preferred_element_type=F32).astype(BF16)
    stats = []
    for h in range(PEER_HEADS):
        tops = []
        for p in range(2):
            hp = 2 * h + p
            s_t = lax.dot_general(keys_ref[hp], q[:, hp * PEER_HALF:(hp + 1) * PEER_HALF],
                                  (((1,), (1,)), ((), ())), preferred_element_type=F32)
            s_ref[hp] = s_t
            tops.append(_topk_rows(s_t, PEER_TOPK + 1))
        a, b = tops
        pad = [jnp.full_like(a[0], NEG_BIG)] * 7
        b_head = jnp.concatenate(b[:8], axis=0)
        cand = jnp.concatenate([a[0] + jnp.concatenate(b + pad, axis=0)]
                               + [a[i] + b_head for i in range(1, 8)]
                               + [jnp.concatenate(a[8:] + pad, axis=0) + b[0]], axis=0)
        best_cand = _topk_rows(cand, PEER_TOPK + 1)
        kth, runner_up = best_cand[PEER_TOPK - 1], best_cand[PEER_TOPK]
        best = a[0] + b[0]
        z = jnp.sum(jnp.where(cand >= kth, jnp.exp(cand - best), 0.0), axis=0, keepdims=True)
        stats += [0.5 * (kth + runner_up), tops[0][0], tops[1][0], 1.0 / z]
    st_ref[...] = jnp.concatenate(stats, axis=0)


def _peer_expert_kernel(x_ref, s_ref, st_ref, u_ref, vt_ref, h_ref, mod0_ref, mod1_ref, g_ref, b_ref, o_ref,
                        e_ref, thr_ref, lin_ref, acc_ref, w_ref, *, rows_per_step, n_chunks, seq, ctx_len):
    j = pl.program_id(1)
    cur = lax.rem(j, 2)

    @pl.when(j == 0)
    def _():
        acc_ref[...] = jnp.zeros_like(acc_ref)
        w_ref[1] = jnp.zeros(w_ref.shape[1:], w_ref.dtype)
        for h in range(PEER_HEADS):
            a0 = st_ref[4 * h + 1:4 * h + 2, :]
            b0 = st_ref[4 * h + 2:4 * h + 3, :]
            rz = st_ref[4 * h + 3:4 * h + 4, :]
            l1 = (s_ref[2 * h] - a0 + jnp.log(rz)) * LOG2E
            l2 = (s_ref[2 * h + 1] - b0) * LOG2E
            e_ref[2 * h] = l1 if h < PEER_POW_HEADS else jnp.exp2(l1)
            e_ref[2 * h + 1] = l2
            if h >= PEER_POW_HEADS:
                lin_ref[h - PEER_POW_HEADS] = jnp.exp2(l2)
            thr_ref[h] = (st_ref[4 * h:4 * h + 1, :] - s_ref[2 * h] - b0) * LOG2E

    @pl.when(j < n_chunks)
    def _():
        slab = 2 * PEER_KEYS
        n_slabs = rows_per_step * PEER_KEYS // slab
        act_slab = lambda n: lax.dot_general(u_ref[n * slab:(n + 1) * slab, :], x_ref[...], (((1,), (1,)), ((), ())),
                                             preferred_element_type=F32)
        acts = [act_slab(n) for n in range(n_slabs)]
        acc_ref[...] += jnp.dot(vt_ref[...], w_ref[1 - cur], preferred_element_type=F32)
        assert rows_per_step % 8 == 0
        for il in range(rows_per_step):
            base = pl.multiple_of(j * rows_per_step + (il // 8) * 8, 8)
            r = il % 8
            o = il * PEER_KEYS
            act = acts[o // slab][o % slab:o % slab + PEER_KEYS]
            act = 0.5 * act * (1.0 + lax.erf(act * (2.0 ** -0.5)))
            gate = None
            for h in range(PEER_HEADS):
                l2 = e_ref[2 * h + 1]
                thr_row = thr_ref[h, pl.ds(base, 8), :][r:r + 1, :]
                half1 = e_ref[2 * h, pl.ds(base, 8), :][r:r + 1, :]
                pair = jnp.exp2(l2 + half1) if h < PEER_POW_HEADS else lin_ref[h - PEER_POW_HEADS] * half1
                term = jnp.where(l2 >= thr_row, pair, 0.0)
                gate = term if gate is None else gate + term
            w_ref[cur, o:o + PEER_KEYS, :] = (act * gate).astype(BF16)

    @pl.when(j == n_chunks)
    def _():
        ffn = (acc_ref[...] + jnp.dot(vt_ref[...], w_ref[1 - cur], preferred_element_type=F32)).T
        h = h_ref[...]
        gate, = _flat_row_mod(mod0_ref, mod1_ref, pl.program_id(0), h.shape, seq, ctx_len, (5,))
        o_ref[...] = _post_norm_rows(h, gate, ffn, g_ref[...], b_ref[...])


def _peer_pallas(h, mod, g, b, w_q, keys, u_tab, vt_tab, *, seq, ctx_len, tb, rows_per_step):
    n_tok, d = h.shape
    nhp = 2 * PEER_HEADS
    x, s, st = pl.pallas_call(
        functools.partial(_peer_score_kernel, seq=seq, ctx_len=ctx_len),
        name="peer_scores",
        grid=(n_tok // tb,),
        in_specs=[pl.BlockSpec((tb, d), lambda i: (i, 0))] + _flat_mod_specs(tb, seq, d, 1)
                 + [pl.BlockSpec(w_q.shape, lambda i: (0, 0)),
                    pl.BlockSpec(keys.shape, lambda i: (0, 0, 0))],
        out_specs=[pl.BlockSpec((tb, d), lambda i: (i, 0)),
                   pl.BlockSpec((nhp, PEER_KEYS, tb), lambda i: (0, 0, i)),
                   pl.BlockSpec((PEER_STAT_ROWS, tb), lambda i: (0, i))],
        out_shape=[jax.ShapeDtypeStruct((n_tok, d), BF16),
                   jax.ShapeDtypeStruct((nhp, PEER_KEYS, n_tok), F32),
                   jax.ShapeDtypeStruct((PEER_STAT_ROWS, n_tok), F32)],
        compiler_params=pltpu.CompilerParams(dimension_semantics=("arbitrary",),
                                             vmem_limit_bytes=V7X_VMEM_LIMIT_BYTES),
    )(h, mod, mod, w_q, keys)
    ec = rows_per_step * PEER_KEYS
    n_chunks = PEER_EXPERTS // ec
    row = pl.BlockSpec((1, d), lambda i, j: (0, 0))
    return pl.pallas_call(
        functools.partial(_peer_expert_kernel, rows_per_step=rows_per_step, n_chunks=n_chunks, seq=seq,
                          ctx_len=ctx_len),
        name="peer_experts",
        grid=(n_tok // tb, n_chunks + 1),
        in_specs=[pl.BlockSpec((tb, d), lambda i, j: (i, 0)),
                  pl.BlockSpec((nhp, PEER_KEYS, tb), lambda i, j: (0, 0, i)),
                  pl.BlockSpec((PEER_STAT_ROWS, tb), lambda i, j: (0, i)),
                  pl.BlockSpec((ec, d), lambda i, j: (jnp.minimum(j, n_chunks - 1), 0)),
                  pl.BlockSpec((d, ec), lambda i, j: (0, jnp.maximum(j - 1, 0))),
                  pl.BlockSpec((tb, d), lambda i, j: (i, 0))] + _flat_mod_specs(tb, seq, d, 2) + [row, row],
        out_specs=pl.BlockSpec((tb, d), lambda i, j: (i, 0)),
        out_shape=jax.ShapeDtypeStruct((n_tok, d), F32),
        scratch_shapes=[pltpu.VMEM((nhp, PEER_KEYS, tb), F32),
                        pltpu.VMEM((PEER_HEADS, PEER_KEYS, tb), F32),
                        pltpu.VMEM((PEER_HEADS - PEER_POW_HEADS, PEER_KEYS, tb), F32),
                        pltpu.VMEM((d, tb), F32),
                        pltpu.VMEM((2, ec, tb), BF16)],
        compiler_params=pltpu.CompilerParams(dimension_semantics=("arbitrary", "arbitrary"),
                                             vmem_limit_bytes=V7X_VMEM_LIMIT_BYTES),
    )(x, s, st, u_tab, vt_tab, h, mod, mod, g.reshape(1, d).astype(F32), b.reshape(1, d).astype(F32))


def _peer_ffn(hs, mod, g, b, w_q, keys, u_tab, vt_tab, *, ctx_len):
    bsz, t, d = hs.shape
    n_tok = bsz * t
    tb = 512 if n_tok % 512 == 0 else 256
    out = _peer_pallas(hs.reshape(n_tok, d), mod, g, b, w_q, keys, u_tab, vt_tab, seq=t, ctx_len=ctx_len, tb=tb,
                       rows_per_step=16)
    return out.reshape(bsz, t, d)


ROW_BLOCK = 256
HIGHEST = lax.Precision.HIGHEST


def _dot_f32(a, b):
    return jnp.dot(a, b, precision=HIGHEST, preferred_element_type=F32)


def _bf16_terms(x):
    hi = x.astype(BF16)
    rest = x - hi.astype(F32)
    mid = rest.astype(BF16)
    return hi, mid, (rest - mid.astype(F32)).astype(BF16)


def _select_sum(sel, x):
    n = x.shape[1]
    out = jnp.dot(sel.astype(BF16), jnp.concatenate(_bf16_terms(x), axis=1), preferred_element_type=F32)
    return out[:, :n] + out[:, n:2 * n] + out[:, 2 * n:]


def _sum_select(x, sel):
    m = x.shape[0]
    out = jnp.dot(jnp.concatenate(_bf16_terms(x), axis=0), sel.astype(BF16), preferred_element_type=F32)
    return out[:m] + out[m:2 * m] + out[2 * m:]


def _segment_ones(n, seg, dtype):
    r = lax.broadcasted_iota(jnp.int32, (n, n), 0) // seg
    c = lax.broadcasted_iota(jnp.int32, (n, n), 1) // seg
    return jnp.where(r == c, 1.0, 0.0).astype(dtype)


def _shifted_rows(x, prev_row, next_row):
    t = x.shape[0]
    rows = lax.broadcasted_iota(jnp.int32, x.shape, 0)
    xm = jnp.where(rows == 0, prev_row, pltpu.roll(x, 1, axis=0))
    xp = jnp.where(rows == t - 1, next_row, pltpu.roll(x, t - 1, axis=0))
    return xm, xp


def _segment_edge_flags(i, n_blocks, ctx_blocks):
    is_start = jnp.logical_or(i == 0, i == ctx_blocks)
    is_end = jnp.logical_or(i == ctx_blocks - 1, i == n_blocks - 1)
    return jnp.where(is_start, 0.0, 1.0), jnp.where(is_end, 0.0, 1.0)


def _halo_specs(width, tr):
    g = tr // 8
    prev = pl.BlockSpec((1, 8, width), lambda b, i: (b, jnp.maximum(i * g - 1, 0), 0))
    nxt = lambda n_groups: pl.BlockSpec((1, 8, width), lambda b, i: (b, jnp.minimum((i + 1) * g, n_groups - 1), 0))
    return prev, nxt


def _softplus(z):
    return jnp.maximum(z, 0.0) + jnp.log1p(jnp.exp(-jnp.abs(z)))


def _rwkv_prep_kernel(x_ref, xprev_ref, xnext_ref, mu_ref, w0_ref, w2_ref, a0_ref, a2_ref, g2_ref,
                      kk_ref, ka_ref, rk_ref,
                      r_out, v_out, kkn_out, g_out, bonus_out, w_out, b_out, kt_out, *, ctx_blocks):
    i = pl.program_id(1)
    keep_prev, keep_next = _segment_edge_flags(i, pl.num_programs(1), ctx_blocks)
    x = x_ref[0]
    xm, xp = _shifted_rows(x, xprev_ref[0, 7:8, :] * keep_prev, xnext_ref[0, 0:1, :] * keep_next)
    mu0 = mu_ref[0:1, :]
    mu1 = mu_ref[1:2, :]
    f = xm * mu0 + x * (1.0 - mu0 - mu1) + xp * mu1
    r = f[:, 0:GW]
    k = f[:, GW:2 * GW]
    v = f[:, 2 * GW:3 * GW]
    o = 3 * GW
    wd = jnp.tanh(f[:, o:o + 2 * DECAY_LORA])
    ad = f[:, o + 2 * DECAY_LORA:o + 2 * DECAY_LORA + 2 * ICLR_LORA]
    gd = f[:, o + 2 * DECAY_LORA + 2 * ICLR_LORA:]
    w_raw = w0_ref[...] + _dot_f32(wd, w2_ref[...])
    log_decay = -jnp.exp(-_softplus(-w_raw) - 0.5)
    a = jax.nn.sigmoid(a0_ref[...] + _dot_f32(ad, a2_ref[...]))
    g = _dot_f32(jax.nn.sigmoid(gd), g2_ref[...])
    head_sum = _segment_ones(GW, HEAD_DIM, F32)
    kx = k * kk_ref[...]
    kkn = kx * lax.rsqrt(_dot_f32(kx * kx, head_sum) + 1e-6)
    kd_sum = jnp.zeros_like(k)
    for d in range(2):
        a_d = a[:, d * GW:(d + 1) * GW]
        kd = k * (1.0 + (a_d - 1.0) * ka_ref[...])
        kd_sum = kd_sum + kd
        w_out[d, 0] = log_decay[:, d * GW:(d + 1) * GW]
        b_out[d, 0] = kkn * a_d
        kt_out[d, 0] = kd
    r_out[0] = r
    v_out[0] = v
    kkn_out[0] = kkn
    g_out[0] = g
    bonus_out[0] = _dot_f32(r * kd_sum * rk_ref[...], head_sum) * v


RWKV_CHUNK = 64


def _rwkv_chunk_kernel(rf_ref, vf_ref, kkf_ref, wf_ref, bf_ref, ktf_ref,
                       rb_ref, vb_ref, kkb_ref, wb_ref, bb_ref, ktb_ref, yf_ref, yb_ref, st_ref):
    i = pl.program_id(1)
    c = RWKV_CHUNK
    n_chunks = rf_ref.shape[1] // c

    @pl.when(i == 0)
    def _():
        st_ref[...] = jnp.zeros_like(st_ref)

    rr = lax.broadcasted_iota(jnp.int32, (c, c), 0)
    ss = lax.broadcasted_iota(jnp.int32, (c, c), 1)
    eye = rr == ss
    ones_cc = jnp.ones((c, c), F32)
    incl = (rr >= ss, rr <= ss)
    strict = (rr > ss, rr < ss)
    levels = []
    blk = 1
    while blk < c:
        levels.append(jnp.logical_and(rr // (2 * blk) == ss // (2 * blk), rr // blk != ss // blk))
        blk *= 2
    bdot = lambda x, y: jnp.dot(x.astype(BF16), y.astype(BF16), preferred_element_type=F32)
    bdot_nt = lambda x, y: lax.dot_general(x.astype(BF16), y.astype(BF16), _NT, preferred_element_type=F32)
    bdot_tn = lambda x, y: lax.dot_general(x.astype(BF16), y.astype(BF16), (((0,), (0,)), ((), ())),
                                           preferred_element_type=F32)
    dirs = ((rf_ref, vf_ref, kkf_ref, wf_ref, bf_ref, ktf_ref, yf_ref),
            (rb_ref, vb_ref, kkb_ref, wb_ref, bb_ref, ktb_ref, yb_ref))
    heads = [(d, h) for d in range(2) for h in range(GROUP_HEADS)]
    row0 = lambda cc, d: (cc if d == 0 else n_chunks - 1 - cc) * c
    rows = lambda cc, d: slice(row0(cc, d), row0(cc, d) + c)

    logw = {(cc, d): dirs[d][3][0, 0, rows(cc, d), :] for cc in range(n_chunks) for d in range(2)}
    cum = {key: _select_sum(jnp.where(incl[key[1]], 1.0, 0.0), lw) for key, lw in logw.items()}
    tot = {key: _select_sum(ones_cc, lw) for key, lw in logw.items()}
    st = {}
    for cc in range(n_chunks):
        for d, h in heads:
            cols = slice(h * HEAD_DIM, (h + 1) * HEAD_DIM)
            r_ref, v_ref, kk_ref, _, b_ref, kt_ref, _ = dirs[d]
            lc = cum[cc, d][:, cols]
            grow = jnp.exp(-lc)
            st[cc, d, h] = dict(
                v=v_ref[0, rows(cc, d), cols],
                kap=kk_ref[0, rows(cc, d), cols] * jnp.exp(lc - logw[cc, d][:, cols]),
                bh=b_ref[0, 0, rows(cc, d), cols] * grow,
                kh=kt_ref[0, 0, rows(cc, d), cols] * grow,
                rh=r_ref[0, rows(cc, d), cols] * jnp.exp(lc),
                scale=jnp.exp(_sum_select(jnp.where(eye, tot[cc, d][:, cols], 0.0), ones_cc)))
    for (cc, d, h), x in st.items():
        gram = bdot_nt(jnp.concatenate([x["kap"], x["rh"]], axis=0), jnp.concatenate([x["bh"], x["kh"]], axis=0))
        x["a_b"] = jnp.where(strict[d], gram[:c, :c], 0.0)
        x["a_k"] = jnp.where(strict[d], gram[:c, c:], 0.0)
        x["b_b"] = jnp.where(incl[d], gram[c:, :c], 0.0)
        x["b_k"] = jnp.where(incl[d], gram[c:, c:], 0.0)
        x["inv"] = jnp.where(eye, 1.0, 0.0) - jnp.where(levels[0], x["a_b"], 0.0)
    for lvl in levels[1:]:
        for x in st.values():
            x["t"] = bdot(jnp.where(lvl, x["a_b"], 0.0), x["inv"])
        for x in st.values():
            x["inv"] = x["inv"] - bdot(x["inv"], x["t"])
    for x in st.values():
        x["akv"] = bdot(x["a_k"], x["v"])
        x["y_const"] = bdot(x["b_k"], x["v"])
        x["kv"] = bdot_tn(x["kh"], x["v"])
    for x in st.values():
        both = bdot(x["inv"], jnp.concatenate([x["kap"], x["akv"]], axis=-1))
        x["p_state"], x["p_const"] = both[:, :HEAD_DIM], both[:, HEAD_DIM:]
    state = {key: st_ref[key[0], key[1]] for key in heads}
    for cc in range(n_chunks):
        cur = [(key, st[(cc,) + key]) for key in heads]
        for key, x in cur:
            x["p"] = bdot(x["p_state"], state[key]) + x["p_const"]
            x["y0"] = bdot(x["rh"], state[key]) + x["y_const"]
        for key, x in cur:
            x["y"] = x["y0"] - bdot(x["b_b"], x["p"])
            state[key] = (state[key] + x["kv"] - bdot_tn(x["bh"], x["p"])) * x["scale"]
        for d in range(2):
            dirs[d][6][0, rows(cc, d), :] = jnp.concatenate([x["y"] for (dd, h), x in cur if dd == d], axis=-1)
    for d, h in heads:
        st_ref[d, h] = state[d, h]


def _rwkv_post_rows(y, bonus, g, ln_g, ln_b):
    head_mean = _segment_ones(GW, HEAD_DIM, F32) * (1.0 / HEAD_DIM)
    yc = y - _dot_f32(y, head_mean)
    var = _dot_f32(yc * yc, head_mean)
    return (yc * lax.rsqrt(var + RWKV_GN_EPS) * ln_g + ln_b + bonus) * g


def _rwkv_post_kernel(yf_ref, yb_ref, bonus_ref, g_ref, lng_ref, lnb_ref, o_ref):
    o_ref[0] = _rwkv_post_rows(yf_ref[0] + yb_ref[0], bonus_ref[0], g_ref[0], lng_ref[...], lnb_ref[...])


def _block_diag2(m):
    z = jnp.zeros_like(m[0])
    return jnp.concatenate([jnp.concatenate([m[0], z], 1), jnp.concatenate([z, m[1]], 1)], 0)


def _rwkv7_pallas(f, mu, w0, w2, a0, a2, g2, k_k, k_a, r_k, ln_g, ln_b, *, ctx_len, apply_post=True):
    bsz, seq, cols = f.shape
    tr = ROW_BLOCK
    assert ctx_len % tr == 0 and seq % tr == 0
    nb, ctx_blocks = seq // tr, ctx_len // tr
    prev_spec, next_spec = _halo_specs(cols, tr)
    row2 = lambda a: a.reshape(1, -1).astype(F32)
    full = lambda a: pl.BlockSpec(a.shape, lambda b, i: (0,) * a.ndim)
    params = [mu, row2(w0), _block_diag2(w2), row2(a0), _block_diag2(a2), g2, row2(k_k), row2(k_a), row2(r_k)]
    act = jax.ShapeDtypeStruct((bsz, seq, GW), F32)
    act2 = jax.ShapeDtypeStruct((2, bsz, seq, GW), F32)
    blk = pl.BlockSpec((1, tr, GW), lambda b, i: (b, i, 0))
    blk2 = pl.BlockSpec((2, 1, tr, GW), lambda b, i: (0, b, i, 0))
    r, v, kkn, g, bonus, w, bb, kt = pl.pallas_call(
        functools.partial(_rwkv_prep_kernel, ctx_blocks=ctx_blocks),
        name="rwkv_prep",
        grid=(bsz, nb),
        in_specs=[pl.BlockSpec((1, tr, cols), lambda b, i: (b, i, 0)), prev_spec, next_spec(seq // 8)]
                 + [full(p) for p in params],
        out_specs=[blk] * 5 + [blk2] * 3,
        out_shape=[act] * 5 + [act2] * 3,
        compiler_params=pltpu.CompilerParams(dimension_semantics=("arbitrary", "arbitrary"),
                                             vmem_limit_bytes=V7X_VMEM_LIMIT_BYTES),
    )(f, f, f, *params)

    def bwd_block(c):
        return jnp.where(c < ctx_blocks, ctx_blocks - 1 - c, nb - 1 - (c - ctx_blocks))
    assert tr % RWKV_CHUNK == 0 and RWKV_CHUNK == HEAD_DIM
    fwd = pl.BlockSpec((1, tr, GW), lambda b, c: (b, c, 0))
    bwd = pl.BlockSpec((1, tr, GW), lambda b, c: (b, bwd_block(c), 0))
    fwd_d = pl.BlockSpec((1, 1, tr, GW), lambda b, c: (0, b, c, 0))
    bwd_d = pl.BlockSpec((1, 1, tr, GW), lambda b, c: (1, b, bwd_block(c), 0))
    yf, yb = pl.pallas_call(
        _rwkv_chunk_kernel,
        name="rwkv_chunks",
        grid=(bsz, nb),
        in_specs=[fwd, fwd, fwd, fwd_d, fwd_d, fwd_d, bwd, bwd, bwd, bwd_d, bwd_d, bwd_d],
        out_specs=[fwd, bwd],
        out_shape=[act, act],
        scratch_shapes=[pltpu.VMEM((2, GROUP_HEADS, HEAD_DIM, HEAD_DIM), F32)],
        compiler_params=pltpu.CompilerParams(dimension_semantics=("arbitrary", "arbitrary"),
                                             vmem_limit_bytes=V7X_VMEM_LIMIT_BYTES),
    )(r, v, kkn, w, bb, kt, r, v, kkn, w, bb, kt)
    if not apply_post:
        return yf, yb, bonus, g, row2(ln_g), row2(ln_b)

    return pl.pallas_call(
        _rwkv_post_kernel,
        name="rwkv_post",
        grid=(bsz, nb),
        in_specs=[blk, blk, blk, blk, full(row2(ln_g)), full(row2(ln_b))],
        out_specs=blk,
        out_shape=act,
        compiler_params=pltpu.CompilerParams(dimension_semantics=("arbitrary", "arbitrary")),
    )(yf, yb, bonus, g, row2(ln_g), row2(ln_b))


GDN_GATE_LANES = 128
GDN_PADDED_COLS = 4 * GW + GDN_GATE_LANES


def _gdn_prep_kernel(x_ref, xprev_ref, xnext_ref, ab_ref, conv_ref, alog_ref, dtb_ref,
                     q_out, k_out, v_out, gb_out, *, ctx_blocks):
    i = pl.program_id(1)
    keep_prev, keep_next = _segment_edge_flags(i, pl.num_programs(1), ctx_blocks)
    x = x_ref[0]
    xm, xp = _shifted_rows(x, xprev_ref[0, 7:8, :] * keep_prev, xnext_ref[0, 0:1, :] * keep_next)
    y = xm * conv_ref[0:1, :] + x * conv_ref[1:2, :] + xp * conv_ref[2:3, :]
    y = y * jax.nn.sigmoid(y)
    head_sum = _segment_ones(GW, HEAD_DIM, F32)
    q = y[:, 0:GW]
    k = y[:, GW:2 * GW]
    q_out[0] = q * lax.rsqrt(_dot_f32(q * q, head_sum) + 1e-6) * (HEAD_DIM ** -0.5)
    k_out[0] = k * lax.rsqrt(_dot_f32(k * k, head_sum) + 1e-6)
    v_out[0] = y[:, 2 * GW:3 * GW]
    ab = ab_ref[0]
    lane = lax.broadcasted_iota(jnp.int32, ab.shape, 1)
    log_alpha = -jnp.exp(alog_ref[...]) * _softplus(ab + dtb_ref[...])
    gb_out[0] = jnp.where(lane < 2 * GROUP_HEADS, log_alpha, jax.nn.sigmoid(ab))


def _gdn_chunk_kernel(qf_ref, kf_ref, vf_ref, gf_ref, qb_ref, kb_ref, vb_ref, gb_ref, of_ref, ob_ref, st_ref):
    i = pl.program_id(1)
    c = GDN_CHUNK
    n_chunks = qf_ref.shape[1] // c

    @pl.when(i == 0)
    def _():
        st_ref[...] = jnp.zeros_like(st_ref)

    r = lax.broadcasted_iota(jnp.int32, (c, c), 0)
    s = lax.broadcasted_iota(jnp.int32, (c, c), 1)
    eye = r == s
    ones_cc = jnp.ones((c, c), F32)
    incl = (r >= s, r <= s)
    strict = (r > s, r < s)
    levels = []
    b = 1
    while b < c:
        levels.append(jnp.logical_and(r // (2 * b) == s // (2 * b), r // b != s // b))
        b *= 2
    dirs = ((qf_ref, kf_ref, vf_ref, gf_ref, of_ref), (qb_ref, kb_ref, vb_ref, gb_ref, ob_ref))

    bdot = lambda x, y: jnp.dot(x.astype(BF16), y.astype(BF16), preferred_element_type=F32)
    bdot_nt = lambda x, y: lax.dot_general(x.astype(BF16), y.astype(BF16), _NT, preferred_element_type=F32)
    bdot_tn = lambda x, y: lax.dot_general(x.astype(BF16), y.astype(BF16), (((0,), (0,)), ((), ())),
                                           preferred_element_type=F32)

    heads = [(d, h) for d in range(2) for h in range(GROUP_HEADS)]
    row0 = lambda cc, d: (cc if d == 0 else n_chunks - 1 - cc) * c
    gates = {(cc, d): dirs[d][3][0, row0(cc, d):row0(cc, d) + c, :] for cc in range(n_chunks) for d in range(2)}
    cum = {key: _select_sum(jnp.where(incl[key[1]], 1.0, 0.0), g) for key, g in gates.items()}
    tot = {key: _select_sum(ones_cc, g) for key, g in gates.items()}
    lane_src = lax.broadcasted_iota(jnp.int32, (GDN_GATE_LANES, GW), 0)
    lane_head = lax.broadcasted_iota(jnp.int32, (GDN_GATE_LANES, GW), 1) // HEAD_DIM
    spread = [jnp.where(lane_src == d * GROUP_HEADS + lane_head, 1.0, 0.0) for d in range(2)]
    spread_beta = [jnp.where(lane_src == (2 + d) * GROUP_HEADS + lane_head, 1.0, 0.0) for d in range(2)]
    cum_t = {key: _sum_select(x, spread[key[1]]) for key, x in cum.items()}
    tot_t = {key: _sum_select(x, spread[key[1]]) for key, x in tot.items()}
    beta_t = {key: _sum_select(x, spread_beta[key[1]]) for key, x in gates.items()}
    chains = [(cc, d, h) for cc in range(n_chunks) for d, h in heads]
    st = {}
    for cc, d, h in chains:
        cols = slice(h * HEAD_DIM, (h + 1) * HEAD_DIM)
        q, k, v = (dirs[d][n][0, row0(cc, d):row0(cc, d) + c, cols] for n in range(3))
        gc = cum_t[cc, d][:, cols]
        st[cc, d, h] = dict(q=q, k=k, v=v, gc=gc, gt=tot_t[cc, d][:, cols], beta=beta_t[cc, d][:, cols],
                            gc_row=_select_sum(ones_cc, jnp.where(eye, gc, 0.0)))
    for (cc, d, h), x in st.items():
        x["decay"] = jnp.exp(jnp.where(incl[d], x["gc"] - x["gc_row"], NEG_BIG))
        x["kb"] = x["k"] * x["beta"]
        gram = bdot_nt(jnp.concatenate([x["kb"], x["q"]], axis=0), x["k"])
        x["a"] = jnp.where(strict[d], gram[:c] * x["decay"], 0.0)
        x["qk"] = jnp.where(incl[d], gram[c:] * x["decay"], 0.0)
        x["inv"] = jnp.where(eye, 1.0, 0.0) - jnp.where(levels[0], x["a"], 0.0)
    for lvl in levels[1:]:
        for x in st.values():
            x["t"] = bdot(jnp.where(lvl, x["a"], 0.0), x["inv"])
        for x in st.values():
            x["inv"] = x["inv"] - bdot(x["inv"], x["t"])
    for x in st.values():
        x["eg"] = jnp.exp(x["gc"])
        x["sol"] = bdot(x["inv"], jnp.concatenate([x["v"] * x["beta"], x["kb"] * x["eg"]], axis=-1))
        x["qg"] = x["q"] * x["eg"]
        x["kg"] = x["k"] * jnp.exp(x["gt"] - x["gc"])
    state = {(d, h): st_ref[d, h] for d, h in heads}
    for cc in range(n_chunks):
        cur = [(key, st[(cc,) + key]) for key in heads]
        for key, x in cur:
            x["ws"] = bdot(x["sol"][:, HEAD_DIM:], state[key])
            x["qs"] = bdot(x["qg"], state[key])
        for key, x in cur:
            x["v_new"] = x["sol"][:, :HEAD_DIM] - x["ws"]
            x["o"] = x["qs"] + bdot(x["qk"], x["v_new"])
            x["upd"] = bdot_tn(x["kg"], x["v_new"])
        for key, x in cur:
            state[key] = state[key] * jnp.exp(x["gt"][0:1, :]) + x["upd"]
        for d in range(2):
            dirs[d][4][0, row0(cc, d):row0(cc, d) + c, :] = jnp.concatenate(
                [x["o"] for (dd, h), x in cur if dd == d], axis=-1)
    for d, h in heads:
        st_ref[d, h] = state[d, h]


def _gdn_post_rows(o, gate, g):
    head_mean = _segment_ones(GW, HEAD_DIM, F32) * (1.0 / HEAD_DIM)
    return o * lax.rsqrt(_dot_f32(o * o, head_mean) + 1e-6) * g * (gate * jax.nn.sigmoid(gate))


def _gdn_post_kernel(of_ref, ob_ref, gate_ref, g_ref, o_ref):
    o_ref[0] = _gdn_post_rows(of_ref[0] + ob_ref[0], gate_ref[0], g_ref[...])


def _gated_deltanet_pallas(f, conv_w, a_log, dt_bias, norm_g, *, ctx_len, apply_post=True):
    bsz, seq, width = f.shape
    tr = ROW_BLOCK
    assert ctx_len % tr == 0 and seq % tr == 0 and tr % GDN_CHUNK == 0
    if width == GDN_COLS:
        f = jnp.pad(f, ((0, 0), (0, 0), (0, GDN_PADDED_COLS - GDN_COLS)))
    assert f.shape[2] == GDN_PADDED_COLS
    nb, ctx_blocks = seq // tr, ctx_len // tr
    lane_pad = lambda a: jnp.pad(a.reshape(1, -1).astype(F32), ((0, 0), (0, GDN_GATE_LANES - a.size)))
    prev_spec, next_spec = _halo_specs(3 * GW, tr)
    full = lambda a: pl.BlockSpec(a.shape, lambda b, i: (0,) * a.ndim)
    act = jax.ShapeDtypeStruct((bsz, seq, GW), F32)
    gact = jax.ShapeDtypeStruct((bsz, seq, GDN_GATE_LANES), F32)
    blk = pl.BlockSpec((1, tr, GW), lambda b, i: (b, i, 0))
    gblk = pl.BlockSpec((1, tr, GDN_GATE_LANES), lambda b, i: (b, i, 0))
    gate_view = pl.BlockSpec((1, tr, GW), lambda b, i: (b, i, 3))
    ab_view = pl.BlockSpec((1, tr, GDN_GATE_LANES), lambda b, i: (b, i, 4 * GW // GDN_GATE_LANES))
    params = [conv_w.astype(F32), lane_pad(a_log), lane_pad(dt_bias)]
    q, k, v, gb = pl.pallas_call(
        functools.partial(_gdn_prep_kernel, ctx_blocks=ctx_blocks),
        name="gdn_prep",
        grid=(bsz, nb),
        in_specs=[pl.BlockSpec((1, tr, 3 * GW), lambda b, i: (b, i, 0)), prev_spec, next_spec(seq // 8), ab_view]
                 + [full(p) for p in params],
        out_specs=[blk, blk, blk, gblk],
        out_shape=[act, act, act, gact],
        compiler_params=pltpu.CompilerParams(dimension_semantics=("arbitrary", "arbitrary"),
                                             vmem_limit_bytes=V7X_VMEM_LIMIT_BYTES),
    )(f, f, f, f, *params)

    def bwd_block(i):
        return jnp.where(i < ctx_blocks, ctx_blocks - 1 - i, nb - 1 - (i - ctx_blocks))
    bblk = pl.BlockSpec((1, tr, GW), lambda b, i: (b, bwd_block(i), 0))
    bgblk = pl.BlockSpec((1, tr, GDN_GATE_LANES), lambda b, i: (b, bwd_block(i), 0))
    of, ob = pl.pallas_call(
        _gdn_chunk_kernel,
        name="gdn_chunks",
        grid=(bsz, nb),
        in_specs=[blk, blk, blk, gblk, bblk, bblk, bblk, bgblk],
        out_specs=[blk, bblk],
        out_shape=[act, act],
        scratch_shapes=[pltpu.VMEM((2, GROUP_HEADS, HEAD_DIM, HEAD_DIM), F32)],
        compiler_params=pltpu.CompilerParams(dimension_semantics=("arbitrary", "arbitrary"),
                                             vmem_limit_bytes=V7X_VMEM_LIMIT_BYTES),
    )(q, k, v, gb, q, k, v, gb)

    g_row = jnp.tile(norm_g.reshape(1, HEAD_DIM).astype(F32), (1, GROUP_HEADS))
    if not apply_post:
        return of, ob, f, g_row
    return pl.pallas_call(
        _gdn_post_kernel,
        name="gdn_post",
        grid=(bsz, nb),
        in_specs=[blk, blk, gate_view, full(g_row)],
        out_specs=blk,
        out_shape=act,
        compiler_params=pltpu.CompilerParams(dimension_semantics=("arbitrary", "arbitrary")),
    )(of, ob, f, g_row)


ROPE_PAIR = DIFF_HALF // 4


def _rope_tables(seq, ctx_len, q_scale):
    n = jnp.arange(seq - ctx_len, dtype=jnp.int32)
    row, col = n // GRID_W, n % GRID_W
    i = jnp.arange(HEAD_DIM)
    grp = (i % DIFF_HALF) // (2 * ROPE_PAIR)
    inv = ROPE_BASE ** (-(i % ROPE_PAIR).astype(F32) / ROPE_PAIR)
    pos = jnp.where(grp[None, :] == 0, row[:, None], col[:, None]).astype(F32)
    ang = pos * inv[None, :]
    sign = jnp.where((i % (2 * ROPE_PAIR)) < ROPE_PAIR, -1.0, 1.0)
    cos = jnp.concatenate([jnp.ones((ctx_len, HEAD_DIM), F32), jnp.cos(ang)], 0)
    sin = jnp.concatenate([jnp.zeros((ctx_len, HEAD_DIM), F32), jnp.sin(ang) * sign], 0)
    cos = jnp.tile(cos, (1, GROUP_HEADS))
    sin = jnp.tile(sin, (1, GROUP_HEADS))
    return jnp.concatenate([cos * q_scale, cos], 1), jnp.concatenate([sin * q_scale, sin], 1)


def _rotate_qk(qk, cos, sin):
    lane = lax.broadcasted_iota(jnp.int32, qk.shape, 1)
    partner = jnp.where(lane % (2 * ROPE_PAIR) < ROPE_PAIR,
                        pltpu.roll(qk, qk.shape[1] - ROPE_PAIR, axis=1), pltpu.roll(qk, ROPE_PAIR, axis=1))
    return qk * cos + partner * sin


def _store_qkv(q, k, v, q_out, k_out, v_out):
    q_out[...] = q.astype(BF16)
    for h in range(GROUP_HEADS):
        k_out[h] = k[:, h * HEAD_DIM:(h + 1) * HEAD_DIM].astype(BF16)
        v_out[h] = v[:, h * HEAD_DIM:(h + 1) * HEAD_DIM].astype(BF16)


_NT = (((1,), (1,)), ((), ()))


def _softmax_pv(s, v, base2=False):
    m = jnp.max(s, axis=-1, keepdims=True)
    e = jnp.exp2(s - m) if base2 else jnp.exp(s - m)
    return jnp.dot(e.astype(BF16), v, preferred_element_type=F32) / jnp.sum(e, axis=-1, keepdims=True)


def _diff_attn_kernel(q_ref, k_ref, v_ref, lam_ref, g_ref, o_ref, *, ctx_blocks, ctx_len):
    i = pl.program_id(1)
    lv = lam_ref[...]
    lam_init = lv[4:5, 0:1]
    lam = (jnp.exp(jnp.sum(lv[0:1] * lv[1:2], axis=-1, keepdims=True))
           - jnp.exp(jnp.sum(lv[2:3] * lv[3:4], axis=-1, keepdims=True)) + lam_init)
    lane = lax.broadcasted_iota(jnp.int32, (q_ref.shape[1], HEAD_DIM), 1)

    def attend(n_keys):
        outs = []
        for h in range(GROUP_HEADS):
            qh = q_ref[0, :, h * HEAD_DIM:(h + 1) * HEAD_DIM]
            kh = k_ref[h, 0, 0:n_keys, :]
            vh = v_ref[h, 0, 0:n_keys, :]
            zero = jnp.zeros_like(qh)
            s1 = lax.dot_general(jnp.where(lane < DIFF_HALF, qh, zero), kh, _NT, preferred_element_type=F32)
            s2 = lax.dot_general(jnp.where(lane >= DIFF_HALF, qh, zero), kh, _NT, preferred_element_type=F32)
            o = _softmax_pv(s1, vh, base2=True) - lam * _softmax_pv(s2, vh, base2=True)
            o = o * lax.rsqrt(jnp.mean(o * o, axis=-1, keepdims=True) + 1e-6) * g_ref[...] * (1.0 - lam_init)
            outs.append(o)
        o_ref[0] = jnp.concatenate(outs, axis=-1)

    @pl.when(i < ctx_blocks)
    def _():
        attend(ctx_len)

    @pl.when(i >= ctx_blocks)
    def _():
        attend(k_ref.shape[2])


def _diff_attention_pallas(q, k, v, lam_vecs, norm_g, *, ctx_len, lam_init):
    bsz, seq, _ = q.shape
    tq = ROW_BLOCK
    kv_spec = pl.BlockSpec((GROUP_HEADS, 1, seq, HEAD_DIM), lambda b, i: (0, b, 0, 0))
    lam_rows = jnp.concatenate([lam_vecs.astype(F32), jnp.full((1, lam_vecs.shape[1]), lam_init, F32)], 0)
    return pl.pallas_call(
        functools.partial(_diff_attn_kernel, ctx_blocks=ctx_len // tq, ctx_len=ctx_len),
        name="diff_attn",
        grid=(bsz, seq // tq),
        in_specs=[pl.BlockSpec((1, tq, GW), lambda b, i: (b, i, 0)), kv_spec, kv_spec,
                  pl.BlockSpec(lam_rows.shape, lambda b, i: (0, 0)),
                  pl.BlockSpec((1, HEAD_DIM), lambda b, i: (0, 0))],
        out_specs=pl.BlockSpec((1, tq, GW), lambda b, i: (b, i, 0)),
        out_shape=jax.ShapeDtypeStruct((bsz, seq, GW), F32),
        compiler_params=pltpu.CompilerParams(dimension_semantics=("arbitrary", "arbitrary"),
                                             vmem_limit_bytes=V7X_VMEM_LIMIT_BYTES),
    )(q, k, v, lam_rows, norm_g.reshape(1, HEAD_DIM).astype(F32))


NAT_TILE_ROWS = ROW_BLOCK // GRID_W
NAT_SLAB_ROWS = NAT_TILE_ROWS + WIN_H - 1


def _nat_slab_start(tile, n_rows):
    return np.clip(tile * NAT_TILE_ROWS - WIN_H // 2, 0, n_rows - NAT_SLAB_ROWS)


def _nat_bias_tables(rpb, n_rows):
    n_tiles = n_rows // NAT_TILE_ROWS
    nq, nk, w = NAT_TILE_ROWS, NAT_SLAB_ROWS, GRID_W
    cq, ck = np.arange(w)[:, None], np.arange(w)[None, :]
    d_col = np.clip(ck - cq, -(WIN_W - 1), WIN_W - 1) + WIN_W - 1
    col_1h = (d_col.reshape(-1)[:, None] == np.arange(2 * WIN_W - 1)[None, :]).astype(np.float32)
    c0 = np.clip(cq - WIN_W // 2, 0, w - WIN_W)
    col_ok = (ck >= c0) & (ck < c0 + WIN_W)
    tabs = []
    for tile in (0, 1, n_tiles - 1):
        r = tile * nq + np.arange(nq)[:, None]
        kr = _nat_slab_start(tile, n_rows) + np.arange(nk)[None, :]
        rs = np.clip(r - WIN_H // 2, 0, n_rows - WIN_H)
        row_ok = (kr >= rs) & (kr < rs + WIN_H)
        d_row = np.clip(kr - r + WIN_H - 1, 0, 2 * WIN_H - 2)
        row_1h = (d_row.reshape(-1)[:, None] == np.arange(2 * WIN_H - 1)[None, :]).astype(np.float32)
        t = jnp.einsum('pa,hab,cb->hpc', row_1h, rpb.astype(F32), col_1h, precision=HIGHEST)
        t = t.reshape(GROUP_HEADS, nq, nk, w, w).transpose(0, 1, 3, 2, 4)
        ok = row_ok[:, None, :, None] & col_ok[None, :, None, :]
        tabs.append(jnp.where(ok[None], t, NEG_BIG).reshape(GROUP_HEADS, nq * w, nk * w))
    return jnp.stack(tabs)


def _nat_attn_kernel(q_ref, k_ref, v_ref, bias_ref, o_ref, *, ctx_blocks, ctx_len, n_rows):
    i = pl.program_id(1)
    n_slab = NAT_SLAB_ROWS * GRID_W

    def heads(fn):
        o_ref[0] = jnp.concatenate(
            [fn(h, q_ref[0, :, h * HEAD_DIM:(h + 1) * HEAD_DIM]) for h in range(GROUP_HEADS)], axis=-1)

    @pl.when(i < ctx_blocks)
    def _():
        def ctx_only(h, qh):
            s = lax.dot_general(qh, k_ref[h, 0, 0:ctx_len, :], _NT, preferred_element_type=F32)
            return _softmax_pv(s, v_ref[h, 0, 0:ctx_len, :])
        heads(ctx_only)

    @pl.when(i >= ctx_blocks)
    def _():
        tile = i - ctx_blocks
        start = jnp.clip(tile * NAT_TILE_ROWS - WIN_H // 2, 0, n_rows - NAT_SLAB_ROWS)
        off = pl.multiple_of(ctx_len + start * GRID_W, GRID_W)

        def windowed(h, qh):
            s_w = lax.dot_general(qh, k_ref[h, 0, pl.ds(off, n_slab), :], _NT,
                                  preferred_element_type=F32) + bias_ref[0, h]
            s_c = lax.dot_general(qh, k_ref[h, 0, 0:ctx_len, :], _NT, preferred_element_type=F32)
            m = jnp.maximum(jnp.max(s_w, axis=-1, keepdims=True), jnp.max(s_c, axis=-1, keepdims=True))
            e_w = jnp.exp(s_w - m)
            e_c = jnp.exp(s_c - m)
            den = jnp.sum(e_w, axis=-1, keepdims=True) + jnp.sum(e_c, axis=-1, keepdims=True)
            num = (jnp.dot(e_w.astype(BF16), v_ref[h, 0, pl.ds(off, n_slab), :], preferred_element_type=F32)
                   + jnp.dot(e_c.astype(BF16), v_ref[h, 0, 0:ctx_len, :], preferred_element_type=F32))
            return num / den
        heads(windowed)


def _nat_attention_pallas(q, k, v, rpb, *, ctx_len):
    bsz, seq, _ = q.shape
    tq = ROW_BLOCK
    ctx_blocks = ctx_len // tq
    n_rows = (seq - ctx_len) // GRID_W
    n_tiles = n_rows // NAT_TILE_ROWS
    assert n_rows >= NAT_SLAB_ROWS and n_tiles >= 3
    bias = _nat_bias_tables(rpb, n_rows)

    def variant(i):
        tile = i - ctx_blocks
        return jnp.where(tile <= 0, 0, jnp.where(tile >= n_tiles - 1, 2, 1))
    kv_spec = pl.BlockSpec((GROUP_HEADS, 1, seq, HEAD_DIM), lambda b, i: (0, b, 0, 0))
    return pl.pallas_call(
        functools.partial(_nat_attn_kernel, ctx_blocks=ctx_blocks, ctx_len=ctx_len, n_rows=n_rows),
        name="nat_attn",
        grid=(bsz, seq // tq),
        in_specs=[pl.BlockSpec((1, tq, GW), lambda b, i: (b, i, 0)), kv_spec, kv_spec,
                  pl.BlockSpec((1,) + bias.shape[1:], lambda b, i: (variant(i), 0, 0, 0))],
        out_specs=pl.BlockSpec((1, tq, GW), lambda b, i: (b, i, 0)),
        out_shape=jax.ShapeDtypeStruct((bsz, seq, GW), F32),
        compiler_params=pltpu.CompilerParams(dimension_semantics=("arbitrary", "arbitrary"),
                                             vmem_limit_bytes=V7X_VMEM_LIMIT_BYTES),
    )(q, k, v, bias)


N_MOD = 6
MATMUL_ROWS = 512


def _ada_kernel(c_ref, w_ref, b_ref, o_ref):
    c = c_ref[...]
    o_ref[...] = _dot_f32(c * jax.nn.sigmoid(c), w_ref[...]) + b_ref[...]


def _ada_modulation(c, c_ctx, w_ada, b_ada):
    bsz, d = c.shape
    rows = 8 * ((bsz + 1 + 7) // 8)
    cc = jnp.zeros((rows, d), F32).at[:bsz].set(c).at[bsz].set(c_ctx)
    tn = d
    m = pl.pallas_call(
        _ada_kernel,
        name="ada_modulation",
        grid=(w_ada.shape[1] // tn,),
        in_specs=[pl.BlockSpec((rows, d), lambda j: (0, 0)),
                  pl.BlockSpec((d, tn), lambda j: (0, j)),
                  pl.BlockSpec((1, tn), lambda j: (0, j))],
        out_specs=pl.BlockSpec((rows, tn), lambda j: (0, j)),
        out_shape=jax.ShapeDtypeStruct((rows, w_ada.shape[1]), F32),
        compiler_params=pltpu.CompilerParams(dimension_semantics=("arbitrary",)),
    )(cc, w_ada, b_ada.reshape(1, -1))
    lat = m[:bsz].reshape(bsz, 1, N_MOD, d)
    ctx = jnp.broadcast_to(m[bsz].reshape(1, 1, N_MOD, d), (bsz, 1, N_MOD, d))
    return jnp.concatenate([ctx, lat], axis=1)


def _ln(x):
    mu = jnp.mean(x, axis=-1, keepdims=True)
    xc = x - mu
    return xc * lax.rsqrt(jnp.mean(xc * xc, axis=-1, keepdims=True) + LN_EPS)


def _mod_spec(d, ctx_blocks):
    return pl.BlockSpec((1, 1, N_MOD, d), lambda b, i: (b, jnp.where(i < ctx_blocks, 0, 1), 0, 0))


def _in_proj_kernel(h_ref, mod0_ref, mod1_ref, wa_ref, wb_ref, wc_ref, wd_ref, cos_ref, sin_ref,
                    qa_out, ka_out, va_out, pb_out, pc_out, qd_out, kd_out, vd_out, *, seq, ctx_len):
    h = h_ref[...]
    shift, scale = _flat_row_mod(mod0_ref, mod1_ref, pl.program_id(0), h.shape, seq, ctx_len, (0, 1))
    x = (_ln(h) * (1.0 + scale) + shift).astype(BF16)
    pa = jnp.dot(x, wa_ref[...], preferred_element_type=F32)
    qk = _rotate_qk(pa[:, :2 * GW], cos_ref[...], sin_ref[...])
    _store_qkv(qk[:, :GW], qk[:, GW:], pa[:, 2 * GW:], qa_out, ka_out, va_out)
    pb_out[...] = jnp.dot(x, wb_ref[...], preferred_element_type=F32)
    pc_out[...] = jnp.dot(x, wc_ref[...], preferred_element_type=F32)
    pd = jnp.dot(x, wd_ref[...], preferred_element_type=F32)
    _store_qkv(pd[:, :GW] * (HEAD_DIM ** -0.5), pd[:, GW:2 * GW], pd[:, 2 * GW:], qd_out, kd_out, vd_out)


def _in_proj_pallas(x, mod, ws, cos, sin, *, seq, ctx_len):
    m, k = x.shape
    tm = MATMUL_ROWS
    row = lambda n: pl.BlockSpec((tm, n), lambda i: (i, 0))
    hm = pl.BlockSpec((GROUP_HEADS, tm, HEAD_DIM), lambda i: (0, i, 0))
    q_shape = jax.ShapeDtypeStruct((m, GW), BF16)
    hm_shape = jax.ShapeDtypeStruct((GROUP_HEADS, m, HEAD_DIM), BF16)
    f32 = lambda n: jax.ShapeDtypeStruct((m, n), F32)
    return pl.pallas_call(
        functools.partial(_in_proj_kernel, seq=seq, ctx_len=ctx_len),
        name="in_proj",
        grid=(m // tm,),
        in_specs=[row(k)] + _flat_mod_specs(tm, seq, k, 1) + [pl.BlockSpec(w.shape, lambda i: (0, 0)) for w in ws]
                 + [row(2 * GW), row(2 * GW)],
        out_specs=[row(GW), hm, hm, row(ws[1].shape[1]), row(ws[2].shape[1]), row(GW), hm, hm],
        out_shape=[q_shape, hm_shape, hm_shape, f32(ws[1].shape[1]), f32(ws[2].shape[1]), q_shape, hm_shape, hm_shape],
        compiler_params=pltpu.CompilerParams(dimension_semantics=("arbitrary",),
                                             vmem_limit_bytes=V7X_VMEM_LIMIT_BYTES),
    )(x, mod, mod, *ws, cos, sin)


def _post_norm_rows(h, gate, y, g, b):
    return _ln(DN_ALPHA * h + gate * y) * g + b


def _out_proj_kernel(ya_ref, rf_ref, rb_ref, bonus_ref, rg_ref, lng_ref, lnb_ref, of_ref, ob_ref, gate_ref, gn_ref,
                     yd_ref, w_ref, h_ref, mod_ref, g_ref, b_ref, o_ref):
    yb = _rwkv_post_rows(rf_ref[0] + rb_ref[0], bonus_ref[0], rg_ref[0], lng_ref[...], lnb_ref[...])
    yc = _gdn_post_rows(of_ref[0] + ob_ref[0], gate_ref[0], gn_ref[...])
    mix = None
    for n, y in enumerate((ya_ref[0], yb, yc, yd_ref[0])):
        part = jnp.dot(y.astype(BF16), w_ref[n * GW:(n + 1) * GW, :], preferred_element_type=F32)
        mix = part if mix is None else mix + part
    o_ref[0] = _post_norm_rows(h_ref[0], mod_ref[0, 0, 2:3, :], mix, g_ref[...], b_ref[...])


def _out_proj_post_norm(ya, rwkv_parts, gdn_parts, yd, w_out, hs, mod, g, b, *, ctx_len):
    bsz, seq, d = hs.shape
    tr = ROW_BLOCK
    yblk = pl.BlockSpec((1, tr, GW), lambda bb, i: (bb, i, 0))
    blk = pl.BlockSpec((1, tr, d), lambda bb, i: (bb, i, 0))
    row = pl.BlockSpec((1, d), lambda bb, i: (0, 0))
    grow = pl.BlockSpec((1, GW), lambda bb, i: (0, 0))
    gate_view = pl.BlockSpec((1, tr, GW), lambda bb, i: (bb, i, 3))
    return pl.pallas_call(
        _out_proj_kernel,
        name="out_proj_post_norm",
        grid=(bsz, seq // tr),
        in_specs=[yblk] + [yblk] * 4 + [grow, grow] + [yblk, yblk, gate_view, grow] + [yblk]
                 + [pl.BlockSpec(w_out.shape, lambda bb, i: (0, 0)), blk, _mod_spec(d, ctx_len // tr), row, row],
        out_specs=blk,
        out_shape=jax.ShapeDtypeStruct((bsz, seq, d), F32),
        compiler_params=pltpu.CompilerParams(dimension_semantics=("arbitrary", "arbitrary")),
    )(ya, *rwkv_parts, *gdn_parts, yd, w_out, hs, mod, g.reshape(1, d), b.reshape(1, d))


def kernel(x, c, ctx, c_ctx, w_ada, b_ada, w_in, w_out, ln_mix_g, ln_mix_b, ln_ffn_g, ln_ffn_b, diff_lam, diff_norm_g, rwkv_mu, rwkv_w0, rwkv_w2, rwkv_a0, rwkv_a2, rwkv_g2, rwkv_kk, rwkv_ka, rwkv_rk, rwkv_ln_g, rwkv_ln_b, gdn_conv, gdn_a_log, gdn_dt_bias, gdn_norm_g, nat_rpb, peer_wq, peer_keys, peer_u, peer_v):
    dtype = x.dtype
    bsz, ctx_len = ctx.shape[0], ctx.shape[1]
    hs = jnp.concatenate([ctx, x], axis=1)
    seq = hs.shape[1]
    col_sizes = [ATTN_COLS, RWKV_COLS, GDN_COLS, ATTN_COLS]
    cos_a, sin_a = (jnp.tile(t, (bsz, 1)) for t in _rope_tables(seq, ctx_len, DIFF_HALF ** -0.5 * LOG2E))
    col_offs = np.cumsum([0] + col_sizes)
    d_model = hs.shape[2]
    for l in range(DEPTH):
        lam_init = 0.8 - 0.6 * math.exp(-0.3 * l)
        mod = _ada_modulation(c, c_ctx, w_ada[l], b_ada[l])
        w_in_b = w_in[l].astype(BF16)
        w_groups = [w_in_b[:, col_offs[n]:col_offs[n + 1]] for n in range(4)]
        w_groups[2] = jnp.pad(w_groups[2], ((0, 0), (0, GDN_PADDED_COLS - GDN_COLS)))
        qa, ka, va, pb, pc, qd, kd, vd = _in_proj_pallas(hs.reshape(bsz * seq, d_model), mod, w_groups, cos_a, sin_a,
                                                         seq=seq, ctx_len=ctx_len)
        rows3 = lambda p: p.reshape(bsz, seq, p.shape[1])
        heads4 = lambda p: p.reshape(GROUP_HEADS, bsz, seq, HEAD_DIM)
        qa, pb, pc, qd = rows3(qa), rows3(pb), rows3(pc), rows3(qd)
        ka, va, kd, vd = heads4(ka), heads4(va), heads4(kd), heads4(vd)
        ya = _diff_attention_pallas(qa, ka, va, diff_lam[l], diff_norm_g[l], ctx_len=ctx_len, lam_init=lam_init)
        yb_parts = _rwkv7_pallas(pb, rwkv_mu[l], rwkv_w0[l], rwkv_w2[l], rwkv_a0[l], rwkv_a2[l], rwkv_g2[l],
                                 rwkv_kk[l], rwkv_ka[l], rwkv_rk[l], rwkv_ln_g[l], rwkv_ln_b[l], ctx_len=ctx_len,
                                 apply_post=False)
        yc_parts = _gated_deltanet_pallas(pc, gdn_conv[l], gdn_a_log[l], gdn_dt_bias[l], gdn_norm_g[l],
                                          ctx_len=ctx_len, apply_post=False)
        yd = _nat_attention_pallas(qd, kd, vd, nat_rpb[l], ctx_len=ctx_len)
        hs = _out_proj_post_norm(ya, yb_parts, yc_parts, yd, w_out[l].astype(BF16), hs, mod, ln_mix_g[l],
                                 ln_mix_b[l], ctx_len=ctx_len)
        wq_b = peer_wq[l].astype(BF16)
        keys_b = peer_keys[l].reshape(2 * PEER_HEADS, PEER_KEYS, PEER_HALF).astype(BF16)
        u_b = peer_u[l].astype(BF16)
        vt_b = peer_v[l].astype(BF16).T
        ffn_ctx = ctx_len
        if l == DEPTH - 1:
            hs, ffn_ctx = hs[:, ctx_len:], 0
        hs = _peer_ffn(hs, mod, ln_ffn_g[l], ln_ffn_b[l], wq_b, keys_b, u_b, vt_b, ctx_len=ffn_ctx)
    return hs.astype(dtype)
```
